```python
import jax, jax.numpy as jnp
from jax import lax
import numpy as np

D_MODEL = 1024
BATCH = 8
SEQ = 4096
DEPTH = 4

N_META = 16
POOL_WIDTH = D_MODEL
POOL_GROUPS = 4
POOL_GROUP_DIM = POOL_WIDTH // POOL_GROUPS
POOL_WINDOWS = (2, 4, 8, 16)
SSM_EXPAND = 2
D_INNER = SSM_EXPAND * D_MODEL
SSM_HEAD_DIM = 64
SSM_HEADS = D_INNER // SSM_HEAD_DIM
SSM_GROUPS = 8
HEADS_PER_GROUP = SSM_HEADS // SSM_GROUPS
D_STATE = 128
CONV_WIDTH = 4
CHUNK = 128
D_XBC = D_INNER + 2 * SSM_GROUPS * D_STATE
D_FF = 4 * D_MODEL
N_BRANCHES = 2
EPS = 1e-5

OFF_POOL = 0
OFF_Z = OFF_POOL + POOL_WIDTH
OFF_XBC = OFF_Z + D_INNER
OFF_DT = OFF_XBC + D_XBC
OFF_GATE = OFF_DT + SSM_HEADS
IN_COLS = OFF_GATE + N_BRANCHES * D_MODEL

kernel_name = "hybrid_pool_ssd_gated_parallel"


def rmsnorm(x, w):
    xf = x.astype(jnp.float32)
    xf = xf * lax.rsqrt(jnp.mean(xf * xf, axis=-1, keepdims=True) + EPS)
    return xf.astype(x.dtype) * w


def pool_mixer(u, w_group, scale):
    bsz, seqlen, _ = u.shape
    ug = u.reshape(bsz, seqlen, POOL_GROUPS, POOL_GROUP_DIM)
    pos = jnp.arange(seqlen)[None, :, None]
    outs = []
    for g, win in enumerate(POOL_WINDOWS):
        xg = ug[:, :, g, :]
        cs = jnp.cumsum(xg.astype(jnp.float32), axis=1)
        shifted = jnp.pad(cs, ((0, 0), (win, 0), (0, 0)))[:, :seqlen]
        count = jnp.minimum(pos + 1, win).astype(jnp.float32)
        mean = (cs - shifted) / count
        outs.append(mean.astype(u.dtype) - xg)
    pooled = jnp.stack(outs, axis=2)
    y = jnp.einsum("blgc,gcd->blgd", pooled, w_group).reshape(bsz, seqlen, POOL_WIDTH)
    return y * scale


def causal_depthwise_conv(x, w, b):
    seqlen = x.shape[1]
    xp = jnp.pad(x, ((0, 0), (CONV_WIDTH - 1, 0), (0, 0)))
    y = b
    for k in range(CONV_WIDTH):
        y = y + xp[:, k:k + seqlen] * w[k]
    return y


def ssd_chunked(x, dt, a, b_mat, c_mat):
    bsz, seqlen = x.shape[0], x.shape[1]
    pad = (-N_META) % CHUNK
    lp = seqlen + pad
    nc = lp // CHUNK

    def fpad(t):
        return jnp.pad(t, [(0, 0), (pad, 0)] + [(0, 0)] * (t.ndim - 2))

    xf = fpad(x).astype(jnp.float32)
    dtf = fpad(dt).astype(jnp.float32)
    bf = fpad(b_mat).astype(jnp.float32).reshape(bsz, nc, CHUNK, SSM_GROUPS, D_STATE)
    cf = fpad(c_mat).astype(jnp.float32).reshape(bsz, nc, CHUNK, SSM_GROUPS, D_STATE)
    xdt = (xf * dtf[..., None]).reshape(bsz, nc, CHUNK, SSM_GROUPS, HEADS_PER_GROUP, SSM_HEAD_DIM)
    a_dt = (dtf * a.astype(jnp.float32)).reshape(bsz, nc, CHUNK, SSM_GROUPS, HEADS_PER_GROUP)
    a_cs = jnp.cumsum(a_dt, axis=2)

    mask = jnp.tril(jnp.ones((CHUNK, CHUNK), dtype=bool))[:, :, None, None]
    diff = a_cs[:, :, :, None] - a_cs[:, :, None, :]
    lmat = jnp.exp(jnp.where(mask, diff, -jnp.inf))
    cb = jnp.einsum("bclgn,bcsgn->bclsg", cf, bf)
    y_diag = jnp.einsum("bclsg,bclsgr,bcsgrp->bclgrp", cb, lmat, xdt)

    decay_states = jnp.exp(a_cs[:, :, -1:] - a_cs)
    states = jnp.einsum("bclgn,bclgr,bclgrp->bcgrpn", bf, decay_states, xdt)
    chunk_decay = jnp.exp(a_cs[:, :, -1])

    def step(h, inp):
        dec, st = inp
        return dec[..., None, None] * h + st, h

    h0 = jnp.zeros((bsz, SSM_GROUPS, HEADS_PER_GROUP, SSM_HEAD_DIM, D_STATE), jnp.float32)
    _, prev = lax.scan(step, h0, (jnp.moveaxis(chunk_decay, 1, 0), jnp.moveaxis(states, 1, 0)))
    prev = jnp.moveaxis(prev, 0, 1)

    y_off = jnp.einsum("bclgn,bcgrpn,bclgr->bclgrp", cf, prev, jnp.exp(a_cs))
    y = (y_diag + y_off).reshape(bsz, lp, SSM_HEADS, SSM_HEAD_DIM)[:, pad:]
    return y.astype(x.dtype)


def mamba2_branch(z, xbc, dt_raw, conv_w, conv_b, dt_bias, a_log, d_skip, norm_w):
    bsz, seqlen, _ = z.shape
    xbc = jax.nn.silu(causal_depthwise_conv(xbc, conv_w, conv_b))
    xs = xbc[..., :D_INNER].reshape(bsz, seqlen, SSM_HEADS, SSM_HEAD_DIM)
    b_mat = xbc[..., D_INNER:D_INNER + SSM_GROUPS * D_STATE].reshape(bsz, seqlen, SSM_GROUPS, D_STATE)
    c_mat = xbc[..., D_INNER + SSM_GROUPS * D_STATE:].reshape(bsz, seqlen, SSM_GROUPS, D_STATE)
    dt = jax.nn.softplus(dt_raw + dt_bias)
    a = -jnp.exp(a_log)
    y = ssd_chunked(xs, dt, a, b_mat, c_mat) + xs * d_skip[:, None]
    y = y.reshape(bsz, seqlen, D_INNER) * jax.nn.silu(z)
    yg = y.reshape(bsz, seqlen, SSM_GROUPS, D_INNER // SSM_GROUPS).astype(jnp.float32)
    yg = yg * lax.rsqrt(jnp.mean(yg * yg, axis=-1, keepdims=True) + EPS)
    return yg.reshape(bsz, seqlen, D_INNER).astype(z.dtype) * norm_w


def _fwd_setup_inputs(seed: int = 0) -> dict:
    key = jax.random.key(seed)
    ks = jax.random.split(key, 24)
    f32 = jnp.float32
    nrm = lambda k, shape, s: jax.random.normal(k, shape, f32) * s
    dt_init = jnp.exp(jax.random.uniform(ks[9], (DEPTH, SSM_HEADS), f32, np.log(1e-3), np.log(1e-1)))
    return {
        "x": nrm(ks[0], (BATCH, SEQ, D_MODEL), 1.0),
        "meta_tokens": nrm(ks[1], (N_META, D_MODEL), 1.0),
        "mix_norm_w": 1.0 + nrm(ks[2], (DEPTH, D_MODEL), 0.02),
        "w_in": nrm(ks[3], (DEPTH, D_MODEL, IN_COLS), D_MODEL ** -0.5),
        "b_gate": nrm(ks[4], (DEPTH, N_BRANCHES * D_MODEL), 0.02),
        "pool_w_group": nrm(ks[5], (DEPTH, POOL_GROUPS, POOL_GROUP_DIM, POOL_GROUP_DIM), POOL_GROUP_DIM ** -0.5),
        "pool_scale": 1.0 + nrm(ks[6], (DEPTH, POOL_WIDTH), 0.02),
        "w_pool_up": nrm(ks[7], (DEPTH, POOL_WIDTH, D_MODEL), POOL_WIDTH ** -0.5),
        "conv_w": nrm(ks[8], (DEPTH, CONV_WIDTH, D_XBC), CONV_WIDTH ** -0.5),
        "conv_b": nrm(ks[10], (DEPTH, D_XBC), 0.02),
        "dt_bias": dt_init + jnp.log(-jnp.expm1(-dt_init)),
        "a_log": jnp.log(jax.random.uniform(ks[11], (DEPTH, SSM_HEADS), f32, 1.0, 16.0)),
        "d_skip": 1.0 + nrm(ks[12], (DEPTH, SSM_HEADS), 0.02),
        "ssd_norm_w": 1.0 + nrm(ks[13], (DEPTH, D_INNER), 0.02),
        "w_ssd_out": nrm(ks[14], (DEPTH, D_INNER, D_MODEL), D_INNER ** -0.5),
        "w_o": nrm(ks[15], (DEPTH, D_MODEL, D_MODEL), D_MODEL ** -0.5),
        "mlp_norm_w": 1.0 + nrm(ks[16], (DEPTH, D_MODEL), 0.02),
        "w_ff1": nrm(ks[17], (DEPTH, D_MODEL, D_FF), D_MODEL ** -0.5),
        "w_ff2": nrm(ks[18], (DEPTH, D_FF, D_MODEL), 0.5 * D_FF ** -0.5),
        "final_norm_w": 1.0 + nrm(ks[19], (D_MODEL,), 0.02),
    }


def _fwd_reference(x, meta_tokens, mix_norm_w, w_in, b_gate, pool_w_group, pool_scale, w_pool_up,
              conv_w, conv_b, dt_bias, a_log, d_skip, ssd_norm_w, w_ssd_out, w_o,
              mlp_norm_w, w_ff1, w_ff2, final_norm_w):
    bsz = x.shape[0]
    meta = jnp.broadcast_to(meta_tokens[None].astype(x.dtype), (bsz, N_META, D_MODEL))
    h = jnp.concatenate([meta, x], axis=1)
    for i in range(DEPTH):
        u = rmsnorm(h, mix_norm_w[i])
        proj = u @ w_in[i]
        u_pool = proj[..., OFF_POOL:OFF_Z]
        z = proj[..., OFF_Z:OFF_XBC]
        xbc = proj[..., OFF_XBC:OFF_DT]
        dt_raw = proj[..., OFF_DT:OFF_GATE]
        gates = jax.nn.sigmoid(proj[..., OFF_GATE:] + b_gate[i])
        gate_pool = gates[..., :D_MODEL]
        gate_ssd = gates[..., D_MODEL:]

        y_pool = pool_mixer(u_pool, pool_w_group[i], pool_scale[i]) @ w_pool_up[i]
        y_ssd = mamba2_branch(z, xbc, dt_raw, conv_w[i], conv_b[i], dt_bias[i], a_log[i],
                              d_skip[i], ssd_norm_w[i]) @ w_ssd_out[i]
        h = h + (gate_pool * y_pool + gate_ssd * y_ssd) @ w_o[i]

        v = rmsnorm(h, mlp_norm_w[i])
        hid = jax.nn.relu(v @ w_ff1[i])
        h = h + (hid * hid) @ w_ff2[i]
    out = rmsnorm(h, final_norm_w)
    return out[:, N_META:]


import jax as _jax
import jax.numpy as _jnp

TWIN_FORMAT = 'train_step'
FWD_PARAMS = ['x', 'meta_tokens', 'mix_norm_w', 'w_in', 'b_gate', 'pool_w_group', 'pool_scale', 'w_pool_up', 'conv_w', 'conv_b', 'dt_bias', 'a_log', 'd_skip', 'ssd_norm_w', 'w_ssd_out', 'w_o', 'mlp_norm_w', 'w_ff1', 'w_ff2', 'final_norm_w']
TWIN_WEIGHTS = ['meta_tokens', 'mix_norm_w', 'w_in', 'b_gate', 'pool_w_group', 'pool_scale', 'w_pool_up', 'conv_w', 'conv_b', 'dt_bias', 'a_log', 'd_skip', 'ssd_norm_w', 'w_ssd_out', 'w_o', 'mlp_norm_w', 'w_ff1', 'w_ff2', 'final_norm_w']
TWIN_DIFF_INPUT = 'x'
TWIN_INPUTS = ['x', 'meta_tokens', 'mix_norm_w', 'w_in', 'b_gate', 'pool_w_group', 'pool_scale', 'w_pool_up', 'conv_w', 'conv_b', 'dt_bias', 'a_log', 'd_skip', 'ssd_norm_w', 'w_ssd_out', 'w_o', 'mlp_norm_w', 'w_ff1', 'w_ff2', 'final_norm_w', 'loss_target', 'm_meta_tokens', 'm_mix_norm_w', 'm_w_in', 'm_b_gate', 'm_pool_w_group', 'm_pool_scale', 'm_w_pool_up', 'm_conv_w', 'm_conv_b', 'm_dt_bias', 'm_a_log', 'm_d_skip', 'm_ssd_norm_w', 'm_w_ssd_out', 'm_w_o', 'm_mlp_norm_w', 'm_w_ff1', 'm_w_ff2', 'm_final_norm_w', 'v_meta_tokens', 'v_mix_norm_w', 'v_w_in', 'v_b_gate', 'v_pool_w_group', 'v_pool_scale', 'v_w_pool_up', 'v_conv_w', 'v_conv_b', 'v_dt_bias', 'v_a_log', 'v_d_skip', 'v_ssd_norm_w', 'v_w_ssd_out', 'v_w_o', 'v_mlp_norm_w', 'v_w_ff1', 'v_w_ff2', 'v_final_norm_w']
TWIN_OUTPUTS = ['loss', 'grad_x', 'grad_meta_tokens', 'grad_mix_norm_w', 'grad_w_in', 'grad_b_gate', 'grad_pool_w_group', 'grad_pool_scale', 'grad_w_pool_up', 'grad_conv_w', 'grad_conv_b', 'grad_dt_bias', 'grad_a_log', 'grad_d_skip', 'grad_ssd_norm_w', 'grad_w_ssd_out', 'grad_w_o', 'grad_mlp_norm_w', 'grad_w_ff1', 'grad_w_ff2', 'grad_final_norm_w', 'delta_meta_tokens', 'delta_mix_norm_w', 'delta_w_in', 'delta_b_gate', 'delta_pool_w_group', 'delta_pool_scale', 'delta_w_pool_up', 'delta_conv_w', 'delta_conv_b', 'delta_dt_bias', 'delta_a_log', 'delta_d_skip', 'delta_ssd_norm_w', 'delta_w_ssd_out', 'delta_w_o', 'delta_mlp_norm_w', 'delta_w_ff1', 'delta_w_ff2', 'delta_final_norm_w', 'new_m_meta_tokens', 'new_m_mix_norm_w', 'new_m_w_in', 'new_m_b_gate', 'new_m_pool_w_group', 'new_m_pool_scale', 'new_m_w_pool_up', 'new_m_conv_w', 'new_m_conv_b', 'new_m_dt_bias', 'new_m_a_log', 'new_m_d_skip', 'new_m_ssd_norm_w', 'new_m_w_ssd_out', 'new_m_w_o', 'new_m_mlp_norm_w', 'new_m_w_ff1', 'new_m_w_ff2', 'new_m_final_norm_w', 'new_v_meta_tokens', 'new_v_mix_norm_w', 'new_v_w_in', 'new_v_b_gate', 'new_v_pool_w_group', 'new_v_pool_scale', 'new_v_w_pool_up', 'new_v_conv_w', 'new_v_conv_b', 'new_v_dt_bias', 'new_v_a_log', 'new_v_d_skip', 'new_v_ssd_norm_w', 'new_v_w_ssd_out', 'new_v_w_o', 'new_v_mlp_norm_w', 'new_v_w_ff1', 'new_v_w_ff2', 'new_v_final_norm_w']
TWIN_LEAF_KINDS = {'loss': 'loss', 'grad_x': 'grad_x', 'grad_meta_tokens': 'grad_w', 'grad_mix_norm_w': 'grad_w', 'grad_w_in': 'grad_w', 'grad_b_gate': 'grad_w', 'grad_pool_w_group': 'grad_w', 'grad_pool_scale': 'grad_w', 'grad_w_pool_up': 'grad_w', 'grad_conv_w': 'grad_w', 'grad_conv_b': 'grad_w', 'grad_dt_bias': 'grad_w', 'grad_a_log': 'grad_w', 'grad_d_skip': 'grad_w', 'grad_ssd_norm_w': 'grad_w', 'grad_w_ssd_out': 'grad_w', 'grad_w_o': 'grad_w', 'grad_mlp_norm_w': 'grad_w', 'grad_w_ff1': 'grad_w', 'grad_w_ff2': 'grad_w', 'grad_final_norm_w': 'grad_w', 'delta_meta_tokens': 'delta_w', 'delta_mix_norm_w': 'delta_w', 'delta_w_in': 'delta_w', 'delta_b_gate': 'delta_w', 'delta_pool_w_group': 'delta_w', 'delta_pool_scale': 'delta_w', 'delta_w_pool_up': 'delta_w', 'delta_conv_w': 'delta_w', 'delta_conv_b': 'delta_w', 'delta_dt_bias': 'delta_w', 'delta_a_log': 'delta_w', 'delta_d_skip': 'delta_w', 'delta_ssd_norm_w': 'delta_w', 'delta_w_ssd_out': 'delta_w', 'delta_w_o': 'delta_w', 'delta_mlp_norm_w': 'delta_w', 'delta_w_ff1': 'delta_w', 'delta_w_ff2': 'delta_w', 'delta_final_norm_w': 'delta_w', 'new_m_meta_tokens': 'new_m', 'new_m_mix_norm_w': 'new_m', 'new_m_w_in': 'new_m', 'new_m_b_gate': 'new_m', 'new_m_pool_w_group': 'new_m', 'new_m_pool_scale': 'new_m', 'new_m_w_pool_up': 'new_m', 'new_m_conv_w': 'new_m', 'new_m_conv_b': 'new_m', 'new_m_dt_bias': 'new_m', 'new_m_a_log': 'new_m', 'new_m_d_skip': 'new_m', 'new_m_ssd_norm_w': 'new_m', 'new_m_w_ssd_out': 'new_m', 'new_m_w_o': 'new_m', 'new_m_mlp_norm_w': 'new_m', 'new_m_w_ff1': 'new_m', 'new_m_w_ff2': 'new_m', 'new_m_final_norm_w': 'new_m', 'new_v_meta_tokens': 'new_v', 'new_v_mix_norm_w': 'new_v', 'new_v_w_in': 'new_v', 'new_v_b_gate': 'new_v', 'new_v_pool_w_group': 'new_v', 'new_v_pool_scale': 'new_v', 'new_v_w_pool_up': 'new_v', 'new_v_conv_w': 'new_v', 'new_v_conv_b': 'new_v', 'new_v_dt_bias': 'new_v', 'new_v_a_log': 'new_v', 'new_v_d_skip': 'new_v', 'new_v_ssd_norm_w': 'new_v', 'new_v_w_ssd_out': 'new_v', 'new_v_w_o': 'new_v', 'new_v_mlp_norm_w': 'new_v', 'new_v_w_ff1': 'new_v', 'new_v_w_ff2': 'new_v', 'new_v_final_norm_w': 'new_v'}


def _forward(args):
    return _fwd_reference(*[args[k] for k in FWD_PARAMS])


def _output_shape():
    def fwd():
        inp = _fwd_setup_inputs(0)
        return _fwd_reference(*[inp[k] for k in FWD_PARAMS])
    out = _jax.eval_shape(fwd)
    return out.shape, out.dtype

N_MICROBATCH = 1
ADAM_LR = 0.001
ADAM_B1 = 0.9
ADAM_B2 = 0.999
ADAM_EPS = 1e-08
ADAM_WD = 0.01
ADAM_STEP = 10
PER_EXAMPLE_BATCH_AXIS = {'x': 0, 'loss_target': 0}
SHARED_INPUTS = []
_WEIGHT_DTYPES = {'meta_tokens': _jnp.float32, 'mix_norm_w': _jnp.float32, 'w_in': _jnp.float32, 'b_gate': _jnp.float32, 'pool_w_group': _jnp.float32, 'pool_scale': _jnp.float32, 'w_pool_up': _jnp.float32, 'conv_w': _jnp.float32, 'conv_b': _jnp.float32, 'dt_bias': _jnp.float32, 'a_log': _jnp.float32, 'd_skip': _jnp.float32, 'ssd_norm_w': _jnp.float32, 'w_ssd_out': _jnp.float32, 'w_o': _jnp.float32, 'mlp_norm_w': _jnp.float32, 'w_ff1': _jnp.float32, 'w_ff2': _jnp.float32, 'final_norm_w': _jnp.float32}
MOMENT_SCALE = {'meta_tokens': 4.994318e-03, 'mix_norm_w': 1.347907e-01, 'w_in': 4.429123e-02, 'b_gate': 2.742746e-02, 'pool_w_group': 6.356403e-02, 'pool_scale': 6.436538e-02, 'w_pool_up': 6.343731e-02, 'conv_w': 3.951717e-02, 'conv_b': 5.580087e-02, 'dt_bias': 9.069682e-02, 'a_log': 2.041543e-01, 'd_skip': 2.781421e-01, 'ssd_norm_w': 5.405998e-02, 'w_ssd_out': 7.456209e-02, 'w_o': 9.781191e-02, 'mlp_norm_w': 8.805734e-02, 'w_ff1': 4.295434e-02, 'w_ff2': 1.619358e-01, 'final_norm_w': 3.216809e+01}


def _to_microbatches(a, axis):
    t = _jnp.moveaxis(a, axis, 0)
    t = t.reshape((N_MICROBATCH, t.shape[0] // N_MICROBATCH) + t.shape[1:])
    return _jnp.moveaxis(t, 1, axis + 1)


def setup_inputs(seed: int = 0) -> dict:
    inp = _fwd_setup_inputs(seed)
    key = _jax.random.fold_in(_jax.random.key(seed), 7919)
    shape, _ = _output_shape()
    out = dict(inp)
    out["loss_target"] = _jax.random.normal(_jax.random.fold_in(key, 0), shape, _jnp.float32)
    for i, name in enumerate(TWIN_WEIGHTS):
        w = inp[name].astype(_jnp.float32)
        if MOMENT_SCALE is None:
            s = _jnp.sqrt(_jnp.mean(_jnp.square(w)) + 1e-30)
        else:
            s = MOMENT_SCALE[name]
        km, kv = _jax.random.split(_jax.random.fold_in(key, i + 1))
        out[name] = w
        out["m_" + name] = s * _jax.random.normal(km, w.shape, _jnp.float32)
        out["v_" + name] = (s * s) * _jax.random.uniform(kv, w.shape, _jnp.float32, 0.5, 1.5)
    if N_MICROBATCH > 1:
        for name, axis in PER_EXAMPLE_BATCH_AXIS.items():
            out[name] = _to_microbatches(out[name], axis)
    return {'x': out['x'], 'meta_tokens': out['meta_tokens'], 'mix_norm_w': out['mix_norm_w'], 'w_in': out['w_in'], 'b_gate': out['b_gate'], 'pool_w_group': out['pool_w_group'], 'pool_scale': out['pool_scale'], 'w_pool_up': out['w_pool_up'], 'conv_w': out['conv_w'], 'conv_b': out['conv_b'], 'dt_bias': out['dt_bias'], 'a_log': out['a_log'], 'd_skip': out['d_skip'], 'ssd_norm_w': out['ssd_norm_w'], 'w_ssd_out': out['w_ssd_out'], 'w_o': out['w_o'], 'mlp_norm_w': out['mlp_norm_w'], 'w_ff1': out['w_ff1'], 'w_ff2': out['w_ff2'], 'final_norm_w': out['final_norm_w'], 'loss_target': out['loss_target'], 'm_meta_tokens': out['m_meta_tokens'], 'm_mix_norm_w': out['m_mix_norm_w'], 'm_w_in': out['m_w_in'], 'm_b_gate': out['m_b_gate'], 'm_pool_w_group': out['m_pool_w_group'], 'm_pool_scale': out['m_pool_scale'], 'm_w_pool_up': out['m_w_pool_up'], 'm_conv_w': out['m_conv_w'], 'm_conv_b': out['m_conv_b'], 'm_dt_bias': out['m_dt_bias'], 'm_a_log': out['m_a_log'], 'm_d_skip': out['m_d_skip'], 'm_ssd_norm_w': out['m_ssd_norm_w'], 'm_w_ssd_out': out['m_w_ssd_out'], 'm_w_o': out['m_w_o'], 'm_mlp_norm_w': out['m_mlp_norm_w'], 'm_w_ff1': out['m_w_ff1'], 'm_w_ff2': out['m_w_ff2'], 'm_final_norm_w': out['m_final_norm_w'], 'v_meta_tokens': out['v_meta_tokens'], 'v_mix_norm_w': out['v_mix_norm_w'], 'v_w_in': out['v_w_in'], 'v_b_gate': out['v_b_gate'], 'v_pool_w_group': out['v_pool_w_group'], 'v_pool_scale': out['v_pool_scale'], 'v_w_pool_up': out['v_w_pool_up'], 'v_conv_w': out['v_conv_w'], 'v_conv_b': out['v_conv_b'], 'v_dt_bias': out['v_dt_bias'], 'v_a_log': out['v_a_log'], 'v_d_skip': out['v_d_skip'], 'v_ssd_norm_w': out['v_ssd_norm_w'], 'v_w_ssd_out': out['v_w_ssd_out'], 'v_w_o': out['v_w_o'], 'v_mlp_norm_w': out['v_mlp_norm_w'], 'v_w_ff1': out['v_w_ff1'], 'v_w_ff2': out['v_w_ff2'], 'v_final_norm_w': out['v_final_norm_w']}


def _loss(weights, diff, rest, loss_target):
    with _jax.named_scope("forward"):
        args = {**rest, TWIN_DIFF_INPUT: diff, **{k: w.astype(_WEIGHT_DTYPES[k]) for k, w in weights.items()}}
        y = _forward(args)
    with _jax.named_scope("loss_head"):
        err = _jnp.square(y.astype(_jnp.float32) - loss_target)
        return 0.5 * _jnp.sum(_jnp.mean(err, axis=-1)) if err.ndim else 0.5 * err


def _adamw(w, g, m, v):
    m = ADAM_B1 * m + (1.0 - ADAM_B1) * g
    v = ADAM_B2 * v + (1.0 - ADAM_B2) * _jnp.square(g)
    m_hat = m / (1.0 - ADAM_B1 ** ADAM_STEP)
    v_hat = v / (1.0 - ADAM_B2 ** ADAM_STEP)
    delta = -ADAM_LR * (m_hat / (_jnp.sqrt(v_hat) + ADAM_EPS) + ADAM_WD * w)
    return delta, m, v


def reference(x, meta_tokens, mix_norm_w, w_in, b_gate, pool_w_group, pool_scale, w_pool_up, conv_w, conv_b, dt_bias, a_log, d_skip, ssd_norm_w, w_ssd_out, w_o, mlp_norm_w, w_ff1, w_ff2, final_norm_w, loss_target, m_meta_tokens, m_mix_norm_w, m_w_in, m_b_gate, m_pool_w_group, m_pool_scale, m_w_pool_up, m_conv_w, m_conv_b, m_dt_bias, m_a_log, m_d_skip, m_ssd_norm_w, m_w_ssd_out, m_w_o, m_mlp_norm_w, m_w_ff1, m_w_ff2, m_final_norm_w, v_meta_tokens, v_mix_norm_w, v_w_in, v_b_gate, v_pool_w_group, v_pool_scale, v_w_pool_up, v_conv_w, v_conv_b, v_dt_bias, v_a_log, v_d_skip, v_ssd_norm_w, v_w_ssd_out, v_w_o, v_mlp_norm_w, v_w_ff1, v_w_ff2, v_final_norm_w):
    given = dict(x=x, meta_tokens=meta_tokens, mix_norm_w=mix_norm_w, w_in=w_in, b_gate=b_gate, pool_w_group=pool_w_group, pool_scale=pool_scale, w_pool_up=w_pool_up, conv_w=conv_w, conv_b=conv_b, dt_bias=dt_bias, a_log=a_log, d_skip=d_skip, ssd_norm_w=ssd_norm_w, w_ssd_out=w_ssd_out, w_o=w_o, mlp_norm_w=mlp_norm_w, w_ff1=w_ff1, w_ff2=w_ff2, final_norm_w=final_norm_w, loss_target=loss_target, m_meta_tokens=m_meta_tokens, m_mix_norm_w=m_mix_norm_w, m_w_in=m_w_in, m_b_gate=m_b_gate, m_pool_w_group=m_pool_w_group, m_pool_scale=m_pool_scale, m_w_pool_up=m_w_pool_up, m_conv_w=m_conv_w, m_conv_b=m_conv_b, m_dt_bias=m_dt_bias, m_a_log=m_a_log, m_d_skip=m_d_skip, m_ssd_norm_w=m_ssd_norm_w, m_w_ssd_out=m_w_ssd_out, m_w_o=m_w_o, m_mlp_norm_w=m_mlp_norm_w, m_w_ff1=m_w_ff1, m_w_ff2=m_w_ff2, m_final_norm_w=m_final_norm_w, v_meta_tokens=v_meta_tokens, v_mix_norm_w=v_mix_norm_w, v_w_in=v_w_in, v_b_gate=v_b_gate, v_pool_w_group=v_pool_w_group, v_pool_scale=v_pool_scale, v_w_pool_up=v_w_pool_up, v_conv_w=v_conv_w, v_conv_b=v_conv_b, v_dt_bias=v_dt_bias, v_a_log=v_a_log, v_d_skip=v_d_skip, v_ssd_norm_w=v_ssd_norm_w, v_w_ssd_out=v_w_ssd_out, v_w_o=v_w_o, v_mlp_norm_w=v_mlp_norm_w, v_w_ff1=v_w_ff1, v_w_ff2=v_w_ff2, v_final_norm_w=v_final_norm_w)
    weights = {n: given[n] for n in TWIN_WEIGHTS}
    shared = {n: given[n] for n in SHARED_INPUTS}
    per_example = {n: given[n] for n in ['x']}
    grad_fn = _jax.value_and_grad(_loss, argnums=(0, 1))

    def one_microbatch(ex, loss_target):
        ex = dict(ex)
        diff = ex.pop(TWIN_DIFF_INPUT)
        return grad_fn(weights, diff, {**shared, **ex}, loss_target)

    if N_MICROBATCH == 1:
        loss, (grad_w, grad_x) = one_microbatch(per_example, given["loss_target"])
    else:
        def body(carry, xs):
            loss_sum, grad_sum = carry
            l_k, (gw_k, gx_k) = one_microbatch(xs[0], xs[1])
            with _jax.named_scope("update"):
                return (loss_sum + l_k, _jax.tree.map(_jnp.add, grad_sum, gw_k)), gx_k

        init = (_jnp.zeros((), _jnp.float32), _jax.tree.map(_jnp.zeros_like, weights))
        (loss, grad_w), grad_x = _jax.lax.scan(body, init, (per_example, given["loss_target"]))
    with _jax.named_scope("update"):
        delta_w, new_m, new_v = {}, {}, {}
        for n in TWIN_WEIGHTS:
            delta_w[n], new_m[n], new_v[n] = _adamw(weights[n], grad_w[n], given["m_" + n], given["v_" + n])
    return (loss, grad_x, *[grad_w[n] for n in TWIN_WEIGHTS], *[delta_w[n] for n in TWIN_WEIGHTS],
            *[new_m[n] for n in TWIN_WEIGHTS], *[new_v[n] for n in TWIN_WEIGHTS])
```

```python
import functools

import jax
import jax.numpy as jnp
from jax import lax
from jax.experimental import pallas as pl
from jax.experimental.pallas import tpu as pltpu

F32 = jnp.float32
BF16 = jnp.bfloat16

D_MODEL = 1024
DEPTH = 4
N_META = 16
N_PAD = 112
ROW_X = N_PAD + N_META
POOL_WINDOWS = (2, 4, 8, 16)
POOL_GDIM = 256
D_INNER = 2048
N_HEADS = 32
HEAD_DIM = 64
N_GROUPS = 8
HEADS_PER_GROUP = 4
GROUP_W = HEADS_PER_GROUP * HEAD_DIM
D_STATE = 128
CHUNK = 128
D_XBC = 4096
D_FF = 4096
EPS = 1e-5
OFF_Z, OFF_XBC, OFF_DT, OFF_GATE, IN_COLS = 1024, 3072, 7168, 7200, 9248
PZ, PPOOL, PGATE, PXBC, PCOLS = 0, 2048, 3072, 5120, 9216
DT_PAD = 128

ADAM_LR, ADAM_B1, ADAM_B2, ADAM_EPS, ADAM_WD, ADAM_STEP = 0.001, 0.9, 0.999, 1e-08, 0.01, 10

VMEM_LIMIT = 56 * 1024 * 1024

_NN = (((1,), (0,)), ((), ()))
_NT = (((1,), (1,)), ((), ()))
_TN = (((0,), (0,)), ((), ()))


def _dot(a, b, dn=_NN):
    return lax.dot_general(a, b, dn, preferred_element_type=F32)


def _cparams(sem):
    return pltpu.CompilerParams(dimension_semantics=sem, vmem_limit_bytes=VMEM_LIMIT)


def _tile(n, cands):
    for c in cands:
        if n % c == 0:
            return c
    raise ValueError(f"no tile for {n} in {cands}")


def _sigmoid(x):
    return 1.0 / (1.0 + jnp.exp(-x))


def _softplus(x):
    return jnp.maximum(x, 0.0) + jnp.log(1.0 + jnp.exp(-jnp.abs(x)))


def _mm(a, b, mode, *, tm, tn, tk, name, extras=(), epilogue=None, out_dtypes=(F32,)):
    if mode == "nn":
        (m, k), (k2, n) = a.shape, b.shape
    elif mode == "nt":
        (m, k), (n, k2) = a.shape, b.shape
    else:
        (k, m), (k2, n) = a.shape, b.shape
    assert k == k2, (a.shape, b.shape, mode)
    tm, tn, tk = min(tm, m), min(tn, n), min(tk, k)
    assert m % tm == 0 and n % tn == 0 and k % tk == 0, (m, n, k, tm, tn, tk)
    nk = k // tk
    dn = {"nn": _NN, "nt": _NT, "tn": _TN}[mode]
    ne, no = len(extras), len(out_dtypes)

    def body(a_ref, b_ref, *rest):
        e_refs, o_refs = rest[:ne], rest[ne:ne + no]
        p = _dot(a_ref[...].astype(BF16), b_ref[...].astype(BF16), dn)

        def finish(acc):
            outs = epilogue(acc, *[e[...] for e in e_refs]) if epilogue is not None else (acc,)
            for o, v in zip(o_refs, outs):
                o[...] = v.astype(o.dtype)

        if nk == 1:
            finish(p)
        else:
            acc_ref = rest[ne + no]
            kk = pl.program_id(2)

            @pl.when(kk == 0)
            def _():
                acc_ref[...] = p

            @pl.when(kk > 0)
            def _():
                acc_ref[...] += p

            @pl.when(kk == nk - 1)
            def _():
                finish(acc_ref[...])

    if mode == "tn":
        a_spec = pl.BlockSpec((tk, tm), lambda i, j, kk: (kk, i))
    else:
        a_spec = pl.BlockSpec((tm, tk), lambda i, j, kk: (i, kk))
    if mode == "nt":
        b_spec = pl.BlockSpec((tn, tk), lambda i, j, kk: (j, kk))
    else:
        b_spec = pl.BlockSpec((tk, tn), lambda i, j, kk: (kk, j))
    e_specs = [pl.BlockSpec((tm, tn), functools.partial(lambda i, j, kk, off: (i, j + off), off=off)) for _, off in extras]
    o_spec = pl.BlockSpec((tm, tn), lambda i, j, kk: (i, j))
    outs = pl.pallas_call(
        body,
        name=name,
        grid=(m // tm, n // tn, nk),
        in_specs=[a_spec, b_spec, *e_specs],
        out_specs=[o_spec] * no,
        out_shape=[jax.ShapeDtypeStruct((m, n), dt) for dt in out_dtypes],
        scratch_shapes=[pltpu.VMEM((tm, tn), F32)] if nk > 1 else [],
        compiler_params=_cparams(("parallel", "parallel", "arbitrary")),
    )(a, b, *[e for e, _ in extras])
    return outs[0] if no == 1 else outs


def _norm_fwd(h, w, name):
    t, d = h.shape
    tm = _tile(t, (1056, 384, 128))

    def body(h_ref, w_ref, u_ref):
        x = h_ref[...]
        r = lax.rsqrt(jnp.mean(x * x, axis=-1, keepdims=True) + EPS)
        u_ref[...] = (x * r * w_ref[...]).astype(BF16)

    return pl.pallas_call(
        body, name=name, grid=(t // tm,),
        in_specs=[pl.BlockSpec((tm, d), lambda i: (i, 0)), pl.BlockSpec((1, d), lambda i: (0, 0))],
        out_specs=pl.BlockSpec((tm, d), lambda i: (i, 0)),
        out_shape=jax.ShapeDtypeStruct((t, d), BF16),
        compiler_params=_cparams(("parallel",)),
    )(h, w.reshape(1, d))


def _norm_bwd(dres, du, h, w, name):
    t, d = h.shape
    tm = _tile(t, (528, 384, 128))

    def body(dres_ref, du_ref, h_ref, w_ref, dh_ref, dw_ref):
        x = h_ref[...]
        r = lax.rsqrt(jnp.mean(x * x, axis=-1, keepdims=True) + EPS)
        xhat = x * r
        du_v = du_ref[...]
        g = du_v * w_ref[...]
        dh_ref[...] = dres_ref[...] + r * (g - xhat * jnp.mean(g * xhat, axis=-1, keepdims=True))

        @pl.when(pl.program_id(0) == 0)
        def _():
            dw_ref[...] = jnp.zeros_like(dw_ref)

        dw_ref[...] += jnp.sum(du_v * xhat, axis=0, keepdims=True)

    row = pl.BlockSpec((tm, d), lambda i: (i, 0))
    vec = pl.BlockSpec((1, d), lambda i: (0, 0))
    dh, dw = pl.pallas_call(
        body, name=name, grid=(t // tm,),
        in_specs=[row, row, row, vec], out_specs=[row, vec],
        out_shape=[jax.ShapeDtypeStruct((t, d), F32), jax.ShapeDtypeStruct((1, d), F32)],
        compiler_params=_cparams(("arbitrary",)),
    )(dres, du, h, w.reshape(1, d))
    return dh, dw.reshape(d)


def _loss_head(h, w, tgt):
    t, d = h.shape
    tm = CHUNK
    nb = ROW_X // tm

    def body(h_ref, w_ref, t_ref, loss_ref, dh_ref, dw_ref):
        i = pl.program_id(0)
        x = h_ref[...]
        r = lax.rsqrt(jnp.mean(x * x, axis=-1, keepdims=True) + EPS)
        xhat = x * r
        wv = w_ref[...]
        live = i >= nb
        err = jnp.where(live, xhat * wv - t_ref[...], 0.0)
        dout = err * (1.0 / d)
        g = dout * wv
        dh_ref[...] = r * (g - xhat * jnp.mean(g * xhat, axis=-1, keepdims=True))

        @pl.when(i == 0)
        def _():
            loss_ref[...] = jnp.zeros_like(loss_ref)
            dw_ref[...] = jnp.zeros_like(dw_ref)

        loss_ref[...] += 0.5 * jnp.sum(jnp.sum(err * err, axis=-1, keepdims=True), axis=0, keepdims=True) * (1.0 / d)
        dw_ref[...] += jnp.sum(dout * xhat, axis=0, keepdims=True)

    row = pl.BlockSpec((tm, d), lambda i: (i, 0))
    vec = pl.BlockSpec((1, d), lambda i: (0, 0))
    loss, dh, dw = pl.pallas_call(
        body, name="loss_head", grid=(t // tm,),
        in_specs=[row, vec, pl.BlockSpec((tm, d), lambda i: (jnp.maximum(i - nb, 0), 0))],
        out_specs=[pl.BlockSpec((1, 1), lambda i: (0, 0)), row, vec],
        out_shape=[jax.ShapeDtypeStruct((1, 1), F32), jax.ShapeDtypeStruct((t, d), F32), jax.ShapeDtypeStruct((1, d), F32)],
        compiler_params=_cparams(("arbitrary",)),
    )(h, w.reshape(1, d), tgt)
    return loss[0, 0], dh, dw.reshape(d)


def _gate_fwd(proj, b_gate, y_pool, y_ssd):
    t = proj.shape[0]
    d = D_MODEL
    tm = _tile(t, (528, 384, 128))

    def body(gp_ref, gs_ref, bp_ref, bs_ref, yp_ref, ys_ref, o_ref):
        gp = _sigmoid(gp_ref[...] + bp_ref[...])
        gs = _sigmoid(gs_ref[...] + bs_ref[...])
        o_ref[...] = (gp * yp_ref[...] + gs * ys_ref[...]).astype(BF16)

    row = pl.BlockSpec((tm, d), lambda i: (i, 0))
    return pl.pallas_call(
        body, name="gate_fwd", grid=(t // tm,),
        in_specs=[pl.BlockSpec((tm, d), lambda i: (i, PGATE // d)), pl.BlockSpec((tm, d), lambda i: (i, PGATE // d + 1)),
                  pl.BlockSpec((1, d), lambda i: (0, 0)), pl.BlockSpec((1, d), lambda i: (0, 1)), row, row],
        out_specs=row, out_shape=jax.ShapeDtypeStruct((t, d), BF16),
        compiler_params=_cparams(("parallel",)),
    )(proj, proj, b_gate.reshape(1, 2 * d), b_gate.reshape(1, 2 * d), y_pool, y_ssd)


def _gate_bwd(dmerged, proj, b_gate, y_pool, y_ssd):
    t = proj.shape[0]
    d = D_MODEL
    tm = _tile(t, (384, 128))

    def body(dm_ref, gp_ref, gs_ref, bp_ref, bs_ref, yp_ref, ys_ref, dyp_ref, dys_ref, dg_ref, db_ref):
        dm = dm_ref[...]
        gp = _sigmoid(gp_ref[...] + bp_ref[...])
        gs = _sigmoid(gs_ref[...] + bs_ref[...])
        dyp_ref[...] = (dm * gp).astype(BF16)
        dys_ref[...] = (dm * gs).astype(BF16)
        dgp = dm * yp_ref[...] * gp * (1.0 - gp)
        dgs = dm * ys_ref[...] * gs * (1.0 - gs)
        dg_ref[:, :d] = dgp.astype(BF16)
        dg_ref[:, d:] = dgs.astype(BF16)

        @pl.when(pl.program_id(0) == 0)
        def _():
            db_ref[...] = jnp.zeros_like(db_ref)

        db_ref[:, :d] += jnp.sum(dgp, axis=0, keepdims=True)
        db_ref[:, d:] += jnp.sum(dgs, axis=0, keepdims=True)

    row = pl.BlockSpec((tm, d), lambda i: (i, 0))
    dyp, dys, dg, db = pl.pallas_call(
        body, name="gate_bwd", grid=(t // tm,),
        in_specs=[row, pl.BlockSpec((tm, d), lambda i: (i, PGATE // d)), pl.BlockSpec((tm, d), lambda i: (i, PGATE // d + 1)),
                  pl.BlockSpec((1, d), lambda i: (0, 0)), pl.BlockSpec((1, d), lambda i: (0, 1)), row, row],
        out_specs=[row, row, pl.BlockSpec((tm, 2 * d), lambda i: (i, 0)), pl.BlockSpec((1, 2 * d), lambda i: (0, 0))],
        out_shape=[jax.ShapeDtypeStruct((t, d), BF16), jax.ShapeDtypeStruct((t, d), BF16),
                   jax.ShapeDtypeStruct((t, 2 * d), BF16), jax.ShapeDtypeStruct((1, 2 * d), F32)],
        compiler_params=_cparams(("arbitrary",)),
    )(dmerged, proj, proj, b_gate.reshape(1, 2 * d), b_gate.reshape(1, 2 * d), y_pool, y_ssd)
    return dyp, dys, dg, db.reshape(2 * d)


POOL_HALO = 16


def _pool_counts(row0, n, win):
    pos1 = row0 + lax.broadcasted_iota(jnp.int32, (n, 1), 0) - (N_PAD - 1)
    return jnp.clip(pos1, 1, win).astype(F32)


def _pool_fwd(proj, wg, scale, wup):
    t = proj.shape[0]
    d = D_MODEL
    tm = _tile(t, (384, 128))
    hb = tm // POOL_HALO

    def body(u_ref, halo_ref, wg_ref, sc_ref, wup_ref, pooled_ref, yg_ref, ypm_ref, yp_ref):
        i = pl.program_id(0)
        x = u_ref[...]
        halo = jnp.where(i > 0, halo_ref[...], 0.0)
        xc = jnp.concatenate([halo, x], axis=0)
        for g, win in enumerate(POOL_WINDOWS):
            sl = slice(g * POOL_GDIM, (g + 1) * POOL_GDIM)
            s = xc[:, sl]
            k = 1
            while k < win:
                s = s + pltpu.roll(s, k, axis=0)
                k *= 2
            pooled = s[POOL_HALO:] / _pool_counts(i * tm, tm, win) - x[:, sl]
            pb = pooled.astype(BF16)
            pooled_ref[:, sl] = pb
            yg_ref[:, sl] = _dot(pb, wg_ref[g])
        ypm = (yg_ref[...] * sc_ref[...]).astype(BF16)
        ypm_ref[...] = ypm
        yp_ref[...] = _dot(ypm, wup_ref[...])

    row = pl.BlockSpec((tm, d), lambda i: (i, 0))
    return pl.pallas_call(
        body, name="pool_fwd", grid=(t // tm,),
        in_specs=[pl.BlockSpec((tm, d), lambda i: (i, PPOOL // d)),
                  pl.BlockSpec((POOL_HALO, d), lambda i: (jnp.maximum(i * hb - 1, 0), PPOOL // d)),
                  pl.BlockSpec((4, POOL_GDIM, POOL_GDIM), lambda i: (0, 0, 0)),
                  pl.BlockSpec((1, d), lambda i: (0, 0)), pl.BlockSpec((d, d), lambda i: (0, 0))],
        out_specs=[row, row, row, row],
        out_shape=[jax.ShapeDtypeStruct((t, d), BF16), jax.ShapeDtypeStruct((t, d), F32),
                   jax.ShapeDtypeStruct((t, d), BF16), jax.ShapeDtypeStruct((t, d), F32)],
        compiler_params=_cparams(("parallel",)),
    )(proj, proj, wg, scale.reshape(1, d), wup)


def _pool_bwd_a(dy_pool, wup, yg, scale, wg):
    t, d = yg.shape
    tm = _tile(t, (384, 128))

    def body(dy_ref, wup_ref, yg_ref, sc_ref, wg_ref, q_ref, dyg_ref, dsc_ref):
        dypm = _dot(dy_ref[...], wup_ref[...], _NT)

        @pl.when(pl.program_id(0) == 0)
        def _():
            dsc_ref[...] = jnp.zeros_like(dsc_ref)

        dsc_ref[...] += jnp.sum(dypm * yg_ref[...], axis=0, keepdims=True)
        dyg = (dypm * sc_ref[...]).astype(BF16)
        dyg_ref[...] = dyg
        for g in range(len(POOL_WINDOWS)):
            sl = slice(g * POOL_GDIM, (g + 1) * POOL_GDIM)
            q_ref[:, sl] = _dot(dyg[:, sl], wg_ref[g], _NT)

    row = pl.BlockSpec((tm, d), lambda i: (i, 0))
    vec = pl.BlockSpec((1, d), lambda i: (0, 0))
    q, dyg, dsc = pl.pallas_call(
        body, name="pool_bwd_a", grid=(t // tm,),
        in_specs=[row, pl.BlockSpec((d, d), lambda i: (0, 0)), row, vec,
                  pl.BlockSpec((4, POOL_GDIM, POOL_GDIM), lambda i: (0, 0, 0))],
        out_specs=[row, row, vec],
        out_shape=[jax.ShapeDtypeStruct((t, d), F32), jax.ShapeDtypeStruct((t, d), BF16), jax.ShapeDtypeStruct((1, d), F32)],
        compiler_params=_cparams(("arbitrary",)),
    )(dy_pool, wup, yg, scale.reshape(1, d), wg)
    return q, dyg, dsc.reshape(d)


def _pool_bwd_b(q):
    t, d = q.shape
    tm = _tile(t, (384, 128))
    hb = tm // POOL_HALO
    nt = t // tm
    n = tm + POOL_HALO

    def body(q_ref, halo_ref, o_ref):
        i = pl.program_id(0)
        qv = q_ref[...]
        halo = jnp.where(i < nt - 1, halo_ref[...], 0.0)
        qc = jnp.concatenate([qv, halo], axis=0)
        for g, win in enumerate(POOL_WINDOWS):
            sl = slice(g * POOL_GDIM, (g + 1) * POOL_GDIM)
            s = qc[:, sl] / _pool_counts(i * tm, n, win)
            k = 1
            while k < win:
                s = s + pltpu.roll(s, n - k, axis=0)
                k *= 2
            o_ref[:, sl] = (s[:tm] - qv[:, sl]).astype(BF16)

    row = pl.BlockSpec((tm, d), lambda i: (i, 0))
    return pl.pallas_call(
        body, name="pool_bwd_b", grid=(nt,),
        in_specs=[row, pl.BlockSpec((POOL_HALO, d), lambda i: (jnp.minimum((i + 1) * hb, t // POOL_HALO - 1), 0))],
        out_specs=row, out_shape=jax.ShapeDtypeStruct((t, d), BF16),
        compiler_params=_cparams(("parallel",)),
    )(q, q)


def _pool_dwg(pooled, dyg):
    t, d = pooled.shape
    tk = _tile(t, (1056, 384, 128))
    nk = t // tk
    gd = POOL_GDIM

    def body(p_ref, g_ref, o_ref):
        @pl.when(pl.program_id(1) == 0)
        def _():
            o_ref[...] = jnp.zeros_like(o_ref)

        o_ref[0] += _dot(p_ref[...], g_ref[...], _TN)

    blk = pl.BlockSpec((tk, gd), lambda g, k: (k, g))
    return pl.pallas_call(
        body, name="pool_dwg", grid=(d // gd, nk),
        in_specs=[blk, blk], out_specs=pl.BlockSpec((1, gd, gd), lambda g, k: (g, 0, 0)),
        out_shape=jax.ShapeDtypeStruct((d // gd, gd, gd), F32),
        compiler_params=_cparams(("parallel", "arbitrary")),
    )(pooled, dyg)


CONV_W = 4
CONV_HALO = 8
XBC_BLK = PXBC // 1024


def _conv_fwd(proj, conv_w, conv_b):
    t = proj.shape[0]
    cw = 1024
    tm = _tile(t, (1056, 384, 128))
    hb = tm // CONV_HALO

    def body(x_ref, halo_ref, w_ref, b_ref, o_ref):
        i = pl.program_id(1)
        x = x_ref[...]
        halo = jnp.where(i > 0, halo_ref[...], 0.0)
        xc = jnp.concatenate([halo, x], axis=0)
        w = w_ref[...]
        acc = b_ref[...] + x * w[CONV_W - 1:CONV_W, :]
        for k in range(CONV_W - 1):
            acc = acc + pltpu.roll(xc, CONV_W - 1 - k, axis=0)[CONV_HALO:] * w[k:k + 1, :]
        row = i * tm + lax.broadcasted_iota(jnp.int32, (tm, 1), 0)
        o_ref[...] = jnp.where(row >= N_PAD, acc * _sigmoid(acc), 0.0)

    return pl.pallas_call(
        body, name="conv_fwd", grid=(D_XBC // cw, t // tm),
        in_specs=[pl.BlockSpec((tm, cw), lambda j, i: (i, XBC_BLK + j)),
                  pl.BlockSpec((CONV_HALO, cw), lambda j, i: (jnp.maximum(i * hb - 1, 0), XBC_BLK + j)),
                  pl.BlockSpec((CONV_W, cw), lambda j, i: (0, j)), pl.BlockSpec((1, cw), lambda j, i: (0, j))],
        out_specs=pl.BlockSpec((tm, cw), lambda j, i: (i, j)),
        out_shape=jax.ShapeDtypeStruct((t, D_XBC), F32),
        compiler_params=_cparams(("parallel", "parallel")),
    )(proj, proj, conv_w, conv_b.reshape(1, D_XBC))


def _conv_bwd(dxa, coff, proj, conv_w, conv_b):
    t, ncols = dxa.shape
    cw = 1024
    tm = _tile(t, (528, 384, 128))
    hb = tm // CONV_HALO
    nt = t // tm
    n = tm + 2 * CONV_HALO

    def body(d_ref, dn_ref, xp_ref, x_ref, xn_ref, w_ref, b_ref, o_ref, dw_ref, db_ref):
        i = pl.program_id(1)
        last = i == nt - 1
        xf = jnp.concatenate([jnp.where(i > 0, xp_ref[...], 0.0), x_ref[...], jnp.where(last, 0.0, xn_ref[...])], axis=0)
        df = jnp.concatenate([jnp.zeros((CONV_HALO, cw), F32), d_ref[...], jnp.where(last, 0.0, dn_ref[...])], axis=0)
        w = w_ref[...]
        sh = [pltpu.roll(xf, CONV_W - 1 - k, axis=0) if k < CONV_W - 1 else xf for k in range(CONV_W)]
        xc = b_ref[...]
        for k in range(CONV_W):
            xc = xc + sh[k] * w[k:k + 1, :]
        sig = _sigmoid(xc)
        row = i * tm - CONV_HALO + lax.broadcasted_iota(jnp.int32, (n, 1), 0)
        dxc = jnp.where((row >= N_PAD) & (row < t), df * (sig * (1.0 + xc * (1.0 - sig))), 0.0)
        acc = dxc * w[CONV_W - 1:CONV_W, :]
        for k in range(CONV_W - 1):
            acc = acc + pltpu.roll(dxc, n - (CONV_W - 1 - k), axis=0) * w[k:k + 1, :]
        o_ref[...] = acc[CONV_HALO:CONV_HALO + tm].astype(BF16)

        @pl.when(i == 0)
        def _():
            dw_ref[...] = jnp.zeros_like(dw_ref)
            db_ref[...] = jnp.zeros_like(db_ref)

        dm = dxc[CONV_HALO:CONV_HALO + tm]
        db_ref[...] += jnp.sum(dm, axis=0, keepdims=True)
        for k in range(CONV_W):
            dw_ref[k:k + 1, :] += jnp.sum(dm * sh[k][CONV_HALO:CONV_HALO + tm], axis=0, keepdims=True)

    def xspec(rows, fn):
        return pl.BlockSpec((rows, cw), lambda j, i: (fn(i), XBC_BLK + coff + j))

    prev = lambda i: jnp.maximum(i * hb - 1, 0)
    nxt = lambda i: jnp.minimum((i + 1) * hb, t // CONV_HALO - 1)
    dxbc, dw, db = pl.pallas_call(
        body, name=f"conv_bwd_{coff}", grid=(ncols // cw, nt),
        in_specs=[pl.BlockSpec((tm, cw), lambda j, i: (i, j)), pl.BlockSpec((CONV_HALO, cw), lambda j, i: (nxt(i), j)),
                  xspec(CONV_HALO, prev), xspec(tm, lambda i: i), xspec(CONV_HALO, nxt),
                  pl.BlockSpec((CONV_W, cw), lambda j, i: (0, coff + j)), pl.BlockSpec((1, cw), lambda j, i: (0, coff + j))],
        out_specs=[pl.BlockSpec((tm, cw), lambda j, i: (i, j)), pl.BlockSpec((CONV_W, cw), lambda j, i: (0, j)),
                   pl.BlockSpec((1, cw), lambda j, i: (0, j))],
        out_shape=[jax.ShapeDtypeStruct((t, ncols), BF16), jax.ShapeDtypeStruct((CONV_W, ncols), F32),
                   jax.ShapeDtypeStruct((1, ncols), F32)],
        compiler_params=_cparams(("parallel", "arbitrary")),
    )(dxa, dxa, proj, proj, proj, conv_w, conv_b.reshape(1, D_XBC))
    return dxbc, dw, db.reshape(ncols)


def _cumsum(x, axis, reverse=False):
    n = x.shape[axis]
    idx = lax.broadcasted_iota(jnp.int32, x.shape, axis)
    k = 1
    while k < n:
        if reverse:
            x = x + jnp.where(idx < n - k, pltpu.roll(x, n - k, axis=axis), 0.0)
        else:
            x = x + jnp.where(idx >= k, pltpu.roll(x, k, axis=axis), 0.0)
        k *= 2
    return x


def _head_masks():
    lane = lax.broadcasted_iota(jnp.int32, (1, GROUP_W), 1)
    return [(lane >= r * HEAD_DIM) & (lane < (r + 1) * HEAD_DIM) for r in range(HEADS_PER_GROUP)]


def _expand_heads(cols, hm):
    out = jnp.where(hm[0], cols[:, 0:1], 0.0)
    for r in range(1, HEADS_PER_GROUP):
        out = out + jnp.where(hm[r], cols[:, r:r + 1], 0.0)
    return out


def _ssd_common(chunk, dtg_ref, dtt_ref, br_ref, bc_ref, ar_ref, ac_ref):
    rowid = chunk * CHUNK + lax.broadcasted_iota(jnp.int32, (CHUNK, 1), 0)
    laneid = chunk * CHUNK + lax.broadcasted_iota(jnp.int32, (1, CHUNK), 1)
    raw_c = dtg_ref[0] + br_ref[0]
    raw_r = dtt_ref[0] + bc_ref[0]
    dtc = jnp.where(rowid >= N_PAD, _softplus(raw_c), 0.0)
    dtr = jnp.where(laneid >= N_PAD, _softplus(raw_r), 0.0)
    a_r = -jnp.exp(ar_ref[0])
    a_c = -jnp.exp(ac_ref[0])
    hm = _head_masks()
    dt_exp = _expand_heads(dtc, hm)
    acs = _cumsum(dt_exp * _expand_heads(a_r, hm), 0)
    acs_r = _cumsum(dtr * a_c, 1)
    atot = acs[CHUNK - 1:CHUNK, :]
    return dict(rowid=rowid, raw_c=raw_c, dtc=dtc, a_r=a_r, hm=hm, dt_exp=dt_exp, acs=acs, acs_r=acs_r,
                ea=jnp.exp(acs), ds=jnp.exp(atot - acs), dec=jnp.exp(atot))


def _ssd_specs(nc, cidx):
    gw = GROUP_W
    return [
        pl.BlockSpec((CHUNK, gw), lambda g, c: (cidx(c), g)),
        pl.BlockSpec((CHUNK, D_STATE), lambda g, c: (cidx(c), D_INNER // D_STATE + g)),
        pl.BlockSpec((CHUNK, D_STATE), lambda g, c: (cidx(c), (D_INNER + 1024) // D_STATE + g)),
        pl.BlockSpec((1, CHUNK, HEADS_PER_GROUP), lambda g, c: (g, cidx(c), 0)),
        pl.BlockSpec((1, HEADS_PER_GROUP, CHUNK), lambda g, c: (g, 0, cidx(c))),
        pl.BlockSpec((1, 1, HEADS_PER_GROUP), lambda g, c: (g, 0, 0)),
        pl.BlockSpec((1, HEADS_PER_GROUP, 1), lambda g, c: (g, 0, 0)),
        pl.BlockSpec((1, 1, HEADS_PER_GROUP), lambda g, c: (g, 0, 0)),
        pl.BlockSpec((1, HEADS_PER_GROUP, 1), lambda g, c: (g, 0, 0)),
    ]


def _ssd_small(dt_raw, dt_bias, a_log):
    t = dt_raw.shape[0]
    dtg = dt_raw[:, :N_HEADS].reshape(t, N_GROUPS, HEADS_PER_GROUP).transpose(1, 0, 2)
    dtt = dtg.transpose(0, 2, 1)
    return (dtg, dtt, dt_bias.reshape(N_GROUPS, 1, HEADS_PER_GROUP), dt_bias.reshape(N_GROUPS, HEADS_PER_GROUP, 1),
            a_log.reshape(N_GROUPS, 1, HEADS_PER_GROUP), a_log.reshape(N_GROUPS, HEADS_PER_GROUP, 1))


def _ssd_fwd(xa, dt_raw, dt_bias, a_log):
    t = xa.shape[0]
    nc = t // CHUNK
    gw = GROUP_W

    def body(xs_ref, b_ref, c_ref, dtg_ref, dtt_ref, br_ref, bc_ref, ar_ref, ac_ref, y_ref, prev_ref, st_ref):
        c = pl.program_id(1)

        @pl.when(c == 0)
        def _():
            st_ref[...] = jnp.zeros_like(st_ref)

        q = _ssd_common(c, dtg_ref, dtt_ref, br_ref, bc_ref, ar_ref, ac_ref)
        hm = q["hm"]
        xdt = xs_ref[...] * q["dt_exp"]
        bm = b_ref[...].astype(BF16)
        cm = c_ref[...].astype(BF16)
        cb = _dot(cm, bm, _NT)
        st = st_ref[...]
        prev_ref[0, 0] = st
        y = _dot(cm, st.astype(BF16)) * q["ea"]
        tril = lax.broadcasted_iota(jnp.int32, (CHUNK, CHUNK), 0) >= lax.broadcasted_iota(jnp.int32, (CHUNK, CHUNK), 1)
        for r in range(HEADS_PER_GROUP):
            a_col = q["acs"][:, r * HEAD_DIM:r * HEAD_DIM + 1]
            a_row = q["acs_r"][r:r + 1, :]
            lm = jnp.exp(jnp.where(tril, a_col - a_row, -jnp.inf))
            y = y + _dot((cb * lm).astype(BF16), jnp.where(hm[r], xdt, 0.0).astype(BF16))
        y_ref[...] = y
        st_ref[...] = q["dec"] * st + _dot(bm, (xdt * q["ds"]).astype(BF16), _TN)

    y, prev = pl.pallas_call(
        body, name="ssd_fwd", grid=(N_GROUPS, nc),
        in_specs=_ssd_specs(nc, lambda c: c),
        out_specs=[pl.BlockSpec((CHUNK, gw), lambda g, c: (c, g)),
                   pl.BlockSpec((1, 1, D_STATE, gw), lambda g, c: (c, g, 0, 0))],
        out_shape=[jax.ShapeDtypeStruct((t, D_INNER), F32), jax.ShapeDtypeStruct((nc, N_GROUPS, D_STATE, gw), F32)],
        scratch_shapes=[pltpu.VMEM((D_STATE, gw), F32)],
        compiler_params=_cparams(("parallel", "arbitrary")),
    )(xa, xa, xa, *_ssd_small(dt_raw, dt_bias, a_log))
    return y, prev


def _ssd_bwd(dy, dxs_skip, xa, prev, dt_raw, dt_bias, a_log):
    t = xa.shape[0]
    nc = t // CHUNK
    gw = GROUP_W
    hpg = HEADS_PER_GROUP

    def body(xs_ref, b_ref, c_ref, dtg_ref, dtt_ref, br_ref, bc_ref, ar_ref, ac_ref, dy_ref, sk_ref, prev_ref,
             dxs_ref, db_ref, dc_ref, ddt_ref, dbias_ref, dalog_ref, dst_ref):
        cc = pl.program_id(1)

        @pl.when(cc == 0)
        def _():
            dst_ref[...] = jnp.zeros_like(dst_ref)
            dbias_ref[...] = jnp.zeros_like(dbias_ref)
            dalog_ref[...] = jnp.zeros_like(dalog_ref)

        q = _ssd_common(nc - 1 - cc, dtg_ref, dtt_ref, br_ref, bc_ref, ar_ref, ac_ref)
        hm, ds, dec, dt_exp = q["hm"], q["ds"], q["dec"], q["dt_exp"]
        xs = xs_ref[...]
        xdt = xs * dt_exp
        xdtb = xdt.astype(BF16)
        bm = b_ref[...].astype(BF16)
        cm = c_ref[...].astype(BF16)
        cb = _dot(cm, bm, _NT)
        bc = _dot(bm, cm, _NT)
        dyv = dy_ref[...]
        dye = (dyv * q["ea"]).astype(BF16)
        pst = prev_ref[0, 0]
        dst = dst_ref[...]
        dstb = dst.astype(BF16)
        dx_state = ds * _dot(bm, dstb)
        dxdt = dx_state
        ri = lax.broadcasted_iota(jnp.int32, (CHUNK, CHUNK), 0)
        ci = lax.broadcasted_iota(jnp.int32, (CHUNK, CHUNK), 1)
        dcb = jnp.zeros((CHUNK, CHUNK), F32)
        dcbt = jnp.zeros((CHUNK, CHUNK), F32)
        onehot = [(lax.broadcasted_iota(jnp.int32, (1, hpg), 1) == r).astype(F32) for r in range(hpg)]
        qa = jnp.zeros((CHUNK, hpg), F32)
        for r in range(hpg):
            a_col = q["acs"][:, r * HEAD_DIM:r * HEAD_DIM + 1]
            a_row = q["acs_r"][r:r + 1, :]
            lm = jnp.exp(jnp.where(ri >= ci, a_col - a_row, -jnp.inf))
            lt = jnp.exp(jnp.where(ri <= ci, a_row - a_col, -jnp.inf))
            dyr = jnp.where(hm[r], dyv, 0.0).astype(BF16)
            gl = _dot(dyr, xdtb, _NT) * lm
            glt = _dot(xdtb, dyr, _NT) * lt
            dcb = dcb + gl
            dcbt = dcbt + glt
            dxdt = dxdt + _dot((bc * lt).astype(BF16), dyr)
            qa = qa + (jnp.sum(gl * cb, axis=1, keepdims=True) - jnp.sum(glt * bc, axis=1, keepdims=True)) * onehot[r]
        pstb = pst.astype(BF16)
        dc_ref[...] = _dot(dcb.astype(BF16), bm) + _dot(dye, pstb, _NT)
        db_ref[...] = _dot(dcbt.astype(BF16), cm) + _dot((xdt * ds).astype(BF16), dstb, _NT)
        dst_ref[...] = dec * dst + _dot(cm, dye, _TN)
        dxs_ref[...] = dxdt * dt_exp + sk_ref[...]

        t2 = xdt * dx_state
        t1 = dyv * (_dot(cm, pstb) * q["ea"]) - t2
        t3 = dst * pst
        t4 = dxdt * xs
        xd = jnp.zeros((CHUNK, hpg), F32)
        dal = jnp.zeros((1, hpg), F32)
        for r in range(hpg):
            qa = qa + jnp.sum(jnp.where(hm[r], t1, 0.0), axis=1, keepdims=True) * onehot[r]
            xd = xd + jnp.sum(jnp.where(hm[r], t4, 0.0), axis=1, keepdims=True) * onehot[r]
            s2 = jnp.sum(jnp.sum(jnp.where(hm[r], t2, 0.0), axis=1, keepdims=True), axis=0, keepdims=True)
            s3 = jnp.sum(jnp.sum(jnp.where(hm[r], t3, 0.0), axis=1, keepdims=True), axis=0, keepdims=True)
            dal = dal + (s2 + dec[:, r * HEAD_DIM:r * HEAD_DIM + 1] * s3) * onehot[r]
        rc = _cumsum(qa, 0, reverse=True) + dal
        ddt = rc * q["a_r"] + xd
        ddt_raw = jnp.where(q["rowid"] >= N_PAD, ddt * _sigmoid(q["raw_c"]), 0.0)
        ddt_ref[0] = ddt_raw
        dbias_ref[0] += jnp.sum(ddt_raw, axis=0, keepdims=True)
        dalog_ref[0] += jnp.sum(rc * q["dtc"], axis=0, keepdims=True) * q["a_r"]

    rev = lambda c: nc - 1 - c
    blk = pl.BlockSpec((CHUNK, gw), lambda g, c: (rev(c), g))
    small = pl.BlockSpec((1, 1, hpg), lambda g, c: (g, 0, 0))
    dxs, db, dc, ddt, dbias, dalog = pl.pallas_call(
        body, name="ssd_bwd", grid=(N_GROUPS, nc),
        in_specs=_ssd_specs(nc, rev) + [blk, blk, pl.BlockSpec((1, 1, D_STATE, gw), lambda g, c: (rev(c), g, 0, 0))],
        out_specs=[blk, pl.BlockSpec((CHUNK, D_STATE), lambda g, c: (rev(c), g)),
                   pl.BlockSpec((CHUNK, D_STATE), lambda g, c: (rev(c), g)),
                   pl.BlockSpec((1, CHUNK, hpg), lambda g, c: (g, rev(c), 0)), small, small],
        out_shape=[jax.ShapeDtypeStruct((t, D_INNER), F32), jax.ShapeDtypeStruct((t, N_GROUPS * D_STATE), F32),
                   jax.ShapeDtypeStruct((t, N_GROUPS * D_STATE), F32), jax.ShapeDtypeStruct((N_GROUPS, t, hpg), F32),
                   jax.ShapeDtypeStruct((N_GROUPS, 1, hpg), F32), jax.ShapeDtypeStruct((N_GROUPS, 1, hpg), F32)],
        scratch_shapes=[pltpu.VMEM((D_STATE, gw), F32)],
        compiler_params=_cparams(("parallel", "arbitrary")),
    )(xa, xa, xa, *_ssd_small(dt_raw, dt_bias, a_log), dy, dxs_skip, prev)
    ddt_raw = ddt.transpose(1, 0, 2).reshape(t, N_HEADS)
    return dxs, db, dc, ddt_raw, dbias.reshape(N_HEADS), dalog.reshape(N_HEADS)


def _ssd_post_fwd(y, xa, proj, dsk, nw):
    t = y.shape[0]
    di = D_INNER
    tm = _tile(t, (384, 128))

    def body(y_ref, xs_ref, z_ref, dsk_ref, nw_ref, o_ref):
        z = z_ref[...]
        yz = (y_ref[...] + xs_ref[...] * dsk_ref[...]) * (z * _sigmoid(z))
        nwv = nw_ref[...]
        for g in range(N_GROUPS):
            sl = slice(g * GROUP_W, (g + 1) * GROUP_W)
            v = yz[:, sl]
            rg = lax.rsqrt(jnp.mean(v * v, axis=-1, keepdims=True) + EPS)
            o_ref[:, sl] = (v * rg * nwv[:, sl]).astype(BF16)

    row = pl.BlockSpec((tm, di), lambda i: (i, 0))
    vec = pl.BlockSpec((1, di), lambda i: (0, 0))
    return pl.pallas_call(
        body, name="ssd_post_fwd", grid=(t // tm,),
        in_specs=[row, row, row, vec, vec], out_specs=row, out_shape=jax.ShapeDtypeStruct((t, di), BF16),
        compiler_params=_cparams(("parallel",)),
    )(y, xa, proj, dsk.reshape(1, di), nw.reshape(1, di))


def _ssd_post_bwd(dyn, y, xa, proj, dsk, nw):
    t = y.shape[0]
    di = D_INNER
    tm = CHUNK

    def body(dyn_ref, y_ref, xs_ref, z_ref, dsk_ref, nw_ref, dy_ref, dsk_x_ref, dz_ref, dnw_ref, ddsk_ref):
        @pl.when(pl.program_id(0) == 0)
        def _():
            dnw_ref[...] = jnp.zeros_like(dnw_ref)
            ddsk_ref[...] = jnp.zeros_like(ddsk_ref)

        for g in range(N_GROUPS):
            sl = slice(g * GROUP_W, (g + 1) * GROUP_W)
            z = z_ref[:, sl]
            sig = _sigmoid(z)
            s = z * sig
            xs = xs_ref[:, sl]
            dskv = dsk_ref[:, sl]
            yt = y_ref[:, sl] + xs * dskv
            yz = yt * s
            rg = lax.rsqrt(jnp.mean(yz * yz, axis=-1, keepdims=True) + EPS)
            xhat = yz * rg
            dynv = dyn_ref[:, sl]
            gg = dynv * nw_ref[:, sl]
            dnw_ref[:, sl] += jnp.sum(dynv * xhat, axis=0, keepdims=True)
            dyz = rg * (gg - xhat * jnp.mean(gg * xhat, axis=-1, keepdims=True))
            dyt = dyz * s
            dy_ref[:, sl] = dyt
            dsk_x_ref[:, sl] = dyt * dskv
            dz_ref[:, sl] = (dyz * yt * (sig * (1.0 + z * (1.0 - sig)))).astype(BF16)
            ddsk_ref[:, sl] += jnp.sum(dyt * xs, axis=0, keepdims=True)

    row = pl.BlockSpec((tm, di), lambda i: (i, 0))
    vec = pl.BlockSpec((1, di), lambda i: (0, 0))
    dy, dxsk, dz, dnw, ddsk = pl.pallas_call(
        body, name="ssd_post_bwd", grid=(t // tm,),
        in_specs=[row, row, row, row, vec, vec], out_specs=[row, row, row, vec, vec],
        out_shape=[jax.ShapeDtypeStruct((t, di), F32), jax.ShapeDtypeStruct((t, di), F32), jax.ShapeDtypeStruct((t, di), BF16),
                   jax.ShapeDtypeStruct((1, di), F32), jax.ShapeDtypeStruct((1, di), F32)],
        compiler_params=_cparams(("arbitrary",)),
    )(dyn, y, xa, proj, dsk.reshape(1, di), nw.reshape(1, di))
    return dy, dxsk, dz, dnw.reshape(di), ddsk.reshape(N_HEADS, HEAD_DIM).sum(axis=1)


def _split_w_in(w_in):
    main = jnp.concatenate([w_in[:, OFF_Z:OFF_XBC], w_in[:, :OFF_Z], w_in[:, OFF_GATE:], w_in[:, OFF_XBC:OFF_DT]], axis=1)
    dt = jnp.pad(w_in[:, OFF_DT:OFF_GATE], ((0, 0), (0, DT_PAD - N_HEADS)))
    return main.astype(BF16), dt.astype(BF16)


def _merge_dw_in(dmain, ddt):
    return jnp.concatenate([dmain[:, PPOOL:PGATE], dmain[:, PZ:PPOOL], dmain[:, PXBC:], ddt[:, :N_HEADS], dmain[:, PGATE:PXBC]], axis=1)


def _relu2_epilogue(acc):
    hid = jnp.maximum(acc, 0.0)
    return acc, hid * hid


def _relu2_bwd_epilogue(acc, pre):
    return (acc * 2.0 * jnp.maximum(pre, 0.0),)


def _add_epilogue(acc, res):
    return (acc + res,)


def _layer_fwd(h, w):
    t = h.shape[0]
    tm = _tile(t, (1408, 384, 128))
    s = {"h": h}
    u = _norm_fwd(h, w["mix_norm_w"], "norm_mix")
    proj = _mm(u, w["w_in_main"], "nn", tm=tm, tn=1024, tk=1024, name="mm_proj")
    dt_raw = _mm(u, w["w_in_dt"], "nn", tm=tm, tn=DT_PAD, tk=1024, name="mm_dt")
    pooled, yg, ypm, y_pool = _pool_fwd(proj, w["pool_w_group"], w["pool_scale"], w["w_pool_up"])
    xa = _conv_fwd(proj, w["conv_w"], w["conv_b"])
    y, prev = _ssd_fwd(xa, dt_raw, w["dt_bias"], w["a_log"])
    yn = _ssd_post_fwd(y, xa, proj, w["d_skip_exp"], w["ssd_norm_w"])
    y_ssd = _mm(yn, w["w_ssd_out"], "nn", tm=tm, tn=1024, tk=2048, name="mm_ssd_out")
    merged = _gate_fwd(proj, w["b_gate"], y_pool, y_ssd)
    h1 = _mm(merged, w["w_o"], "nn", tm=tm, tn=1024, tk=1024, name="mm_o", extras=[(h, 0)], epilogue=_add_epilogue)
    v = _norm_fwd(h1, w["mlp_norm_w"], "norm_mlp")
    pre, act = _mm(v, w["w_ff1"], "nn", tm=tm, tn=1024, tk=1024, name="mm_ff1", epilogue=_relu2_epilogue, out_dtypes=(F32, BF16))
    h2 = _mm(act, w["w_ff2"], "nn", tm=tm, tn=1024, tk=2048, name="mm_ff2", extras=[(h1, 0)], epilogue=_add_epilogue)
    s.update(u=u, proj=proj, dt_raw=dt_raw, pooled=pooled, yg=yg, ypm=ypm, y_pool=y_pool, xa=xa, y=y, prev=prev, yn=yn,
             y_ssd=y_ssd, merged=merged, h1=h1, v=v, pre=pre, act=act)
    return h2, s


def _layer_bwd(dh, s, w):
    t = dh.shape[0]
    tm = _tile(t, (1408, 384, 128))
    tk = tm
    g = {}
    dpre = _mm(dh, w["w_ff2"], "nt", tm=tm, tn=1024, tk=1024, name="mm_dact", extras=[(s["pre"], 0)],
               epilogue=_relu2_bwd_epilogue, out_dtypes=(BF16,))
    g["w_ff2"] = _mm(s["act"], dh, "tn", tm=1024, tn=1024, tk=tk, name="mm_dw_ff2")
    dv = _mm(dpre, w["w_ff1"], "nt", tm=tm, tn=1024, tk=2048, name="mm_dv")
    g["w_ff1"] = _mm(s["v"], dpre, "tn", tm=1024, tn=1024, tk=tk, name="mm_dw_ff1")
    dh1, g["mlp_norm_w"] = _norm_bwd(dh, dv, s["h1"], w["mlp_norm_w"], "norm_mlp_bwd")
    dmerged = _mm(dh1, w["w_o"], "nt", tm=tm, tn=1024, tk=1024, name="mm_dmerged")
    g["w_o"] = _mm(s["merged"], dh1, "tn", tm=1024, tn=1024, tk=tk, name="mm_dw_o")
    dy_pool, dy_ssd, dgate, g["b_gate"] = _gate_bwd(dmerged, s["proj"], w["b_gate"], s["y_pool"], s["y_ssd"])
    dyn = _mm(dy_ssd, w["w_ssd_out"], "nt", tm=tm, tn=1024, tk=1024, name="mm_dyn")
    g["w_ssd_out"] = _mm(s["yn"], dy_ssd, "tn", tm=1024, tn=1024, tk=tk, name="mm_dw_ssd_out")
    dy, dxs_skip, dz, g["ssd_norm_w"], g["d_skip"] = _ssd_post_bwd(dyn, s["y"], s["xa"], s["proj"], w["d_skip_exp"], w["ssd_norm_w"])
    dxs, db, dc, ddt_raw, g["dt_bias"], g["a_log"] = _ssd_bwd(dy, dxs_skip, s["xa"], s["prev"], s["dt_raw"],
                                                              w["dt_bias"], w["a_log"])
    dx1, dcw1, dcb1 = _conv_bwd(dxs, 0, s["proj"], w["conv_w"], w["conv_b"])
    dx2, dcw2, dcb2 = _conv_bwd(db, 2, s["proj"], w["conv_w"], w["conv_b"])
    dx3, dcw3, dcb3 = _conv_bwd(dc, 3, s["proj"], w["conv_w"], w["conv_b"])
    g["conv_w"] = jnp.concatenate([dcw1, dcw2, dcw3], axis=1)
    g["conv_b"] = jnp.concatenate([dcb1, dcb2, dcb3])
    q, dyg, g["pool_scale"] = _pool_bwd_a(dy_pool, w["w_pool_up"], s["yg"], w["pool_scale"], w["pool_w_group"])
    g["w_pool_up"] = _mm(s["ypm"], dy_pool, "tn", tm=1024, tn=1024, tk=tk, name="mm_dw_pool_up")
    g["pool_w_group"] = _pool_dwg(s["pooled"], dyg)
    dup = _pool_bwd_b(q)
    dproj = jnp.concatenate([dz, dup, dgate, dx1, dx2, dx3], axis=1)
    ddt_pad = jnp.pad(ddt_raw, ((0, 0), (0, DT_PAD - N_HEADS)))
    du = _mm(dproj, w["w_in_main"], "nt", tm=tm, tn=1024, tk=1024, name="mm_du")
    du = _mm(ddt_pad, w["w_in_dt"], "nt", tm=tm, tn=1024, tk=DT_PAD, name="mm_du_dt", extras=[(du, 0)], epilogue=_add_epilogue)
    dmain = _mm(s["u"], dproj, "tn", tm=1024, tn=1024, tk=tk, name="mm_dw_in")
    ddtw = _mm(s["u"], ddt_pad, "tn", tm=1024, tn=DT_PAD, tk=tk, name="mm_dw_dt")
    g["w_in"] = _merge_dw_in(dmain, ddtw)
    dh0, g["mix_norm_w"] = _norm_bwd(dh1, du, s["h"], w["mix_norm_w"], "norm_mix_bwd")
    return dh0, g


BIG = ("w_in", "pool_w_group", "w_pool_up", "w_ssd_out", "w_o", "w_ff1", "w_ff2")
PER_LAYER = ("mix_norm_w", "w_in", "b_gate", "pool_w_group", "pool_scale", "w_pool_up", "conv_w", "conv_b", "dt_bias",
             "a_log", "d_skip", "ssd_norm_w", "w_ssd_out", "w_o", "mlp_norm_w", "w_ff1", "w_ff2")


def _layer_weights(full, i):
    w = {n: full[n][i] for n in PER_LAYER}
    w["w_in_main"], w["w_in_dt"] = _split_w_in(w.pop("w_in"))
    for n in ("pool_w_group", "w_pool_up", "w_ssd_out", "w_o", "w_ff1", "w_ff2"):
        w[n] = w[n].astype(BF16)
    w["d_skip_exp"] = jnp.repeat(w["d_skip"], HEAD_DIM)
    return w


def _local_step(x, tgt, full):
    seq = x.shape[0]
    h = jnp.concatenate([jnp.zeros((N_PAD, D_MODEL), F32), full["meta_tokens"].astype(F32), x], axis=0)
    saved, ws = [], []
    for i in range(DEPTH):
        w = _layer_weights(full, i)
        h, s = _layer_fwd(h, w)
        saved.append(s)
        ws.append(w)
    loss, dh, g_final = _loss_head(h, full["final_norm_w"], tgt)
    layer_g = [None] * DEPTH
    for i in reversed(range(DEPTH)):
        dh, layer_g[i] = _layer_bwd(dh, saved[i], ws[i])
    grads = {n: jnp.stack([layer_g[i][n] for i in range(DEPTH)]) for n in PER_LAYER}
    grads["final_norm_w"] = g_final
    grads["meta_tokens"] = dh[N_PAD:ROW_X]
    return loss, dh[ROW_X:ROW_X + seq], grads


MESH = pl.DeviceIdType.MESH
LANES = 128
ANY = pl.BlockSpec(memory_space=pl.ANY)


def _place():
    x, y, c = lax.axis_index("x"), lax.axis_index("y"), lax.axis_index("c")
    chips = [(1 - x, y), (x, 1 - y), (1 - x, 1 - y)]
    return x, y, c, chips


def _remote(src, dst, send_sem, recv_sem, to):
    return pltpu.make_async_remote_copy(src_ref=src, dst_ref=dst, send_sem=send_sem, recv_sem=recv_sem,
                                        device_id=to, device_id_type=MESH)


def _gather_big(shard):
    r = shard.shape[0]
    half = r // 2

    def body(src_ref, out_ref, send_sems, recv_sems, local_sem):
        x, y, c, chips = _place()
        sibling = (x, y, 1 - c)
        me = 2 * x + y

        def rows(shard_id, hf):
            return out_ref.at[shard_id, pl.ds(hf * half, half), :]

        mine = pltpu.make_async_copy(src_ref, out_ref.at[me], local_sem)
        mine.start()
        first = [_remote(src_ref.at[pl.ds(c * half, half), :], rows(me, c), send_sems.at[j], recv_sems.at[j], (*chip, c))
                 for j, chip in enumerate(chips)]
        for cp in first:
            cp.start()
        passed = []
        for j, (cx, cy) in enumerate(chips):
            blk = rows(2 * cx + cy, c)
            _remote(blk, blk, send_sems.at[j], recv_sems.at[j], (cx, cy, c)).wait_recv()
            fwd = _remote(blk, blk, send_sems.at[3 + j], recv_sems.at[3 + j], sibling)
            fwd.start()
            passed.append(fwd)
        for j, (cx, cy) in enumerate(chips):
            blk = rows(2 * cx + cy, 1 - c)
            _remote(blk, blk, send_sems.at[3 + j], recv_sems.at[3 + j], sibling).wait_recv()
        for cp in first + passed:
            cp.wait_send()
        mine.wait()

    return pl.pallas_call(
        body, name="gather_big", in_specs=[ANY], out_specs=ANY,
        out_shape=jax.ShapeDtypeStruct((4, r, LANES), shard.dtype),
        scratch_shapes=[pltpu.SemaphoreType.DMA((6,)), pltpu.SemaphoreType.DMA((6,)), pltpu.SemaphoreType.DMA],
    )(shard)


def _exchange_small(v, reduce, name):
    rows_per = v.shape[0]
    vm = pl.BlockSpec(memory_space=pltpu.VMEM)

    def body(v_ref, out_ref, *scratch):
        if reduce:
            land_ref, send_sems, recv_sems, local_sem = scratch
        else:
            land_ref = out_ref
            send_sems, recv_sems, local_sem = scratch
        x, y, c, chips = _place()
        me, sibling = (x, y, c), (x, y, 1 - c)

        def rows(px, py, pc):
            return land_ref.at[4 * px + 2 * py + pc]

        def copy(k, block, to, src=None):
            return _remote(rows(*block) if src is None else src, rows(*block), send_sems.at[k], recv_sems.at[k], to)

        mine = pltpu.make_async_copy(v_ref, rows(*me), local_sem)
        mine.start()
        first = [copy(0, me, sibling, src=v_ref)]
        first += [copy(1 + j, me, (*chip, c), src=v_ref) for j, chip in enumerate(chips)]
        for cp in first:
            cp.start()
        passed = [copy(4 + j, (*chip, c), sibling) for j, chip in enumerate(chips)]
        for j, chip in enumerate(chips):
            copy(1 + j, (*chip, c), me).wait_recv()
            passed[j].start()
        copy(0, sibling, me).wait_recv()
        for j, chip in enumerate(chips):
            copy(4 + j, (*chip, 1 - c), me).wait_recv()
        for cp in first + passed:
            cp.wait_send()
        mine.wait()
        if reduce:
            acc = land_ref[0]
            for d in range(1, 8):
                acc = acc + land_ref[d]
            out_ref[...] = acc

    sems = [pltpu.SemaphoreType.DMA((7,)), pltpu.SemaphoreType.DMA((7,)), pltpu.SemaphoreType.DMA]
    if reduce:
        out_shape = jax.ShapeDtypeStruct((rows_per, LANES), F32)
        scratch = [pltpu.VMEM((8, rows_per, LANES), F32)] + sems
    else:
        out_shape = jax.ShapeDtypeStruct((8, rows_per, LANES), F32)
        scratch = sems
    return pl.pallas_call(
        body, name=name, in_specs=[vm], out_specs=vm, out_shape=out_shape, scratch_shapes=scratch,
        compiler_params=pltpu.CompilerParams(vmem_limit_bytes=VMEM_LIMIT),
    )(v)


def _rs_swap(g):
    half = g.shape[1] // 2

    def body(g_ref, out_ref, send_sem, recv_sem):
        x, y, c, _ = _place()
        cp = _remote(g_ref.at[:, pl.ds((1 - c) * half, half), :], out_ref, send_sem, recv_sem, (x, y, 1 - c))
        cp.start()
        cp.wait()

    return pl.pallas_call(
        body, name="rs_swap", in_specs=[ANY], out_specs=ANY,
        out_shape=jax.ShapeDtypeStruct((4, half, LANES), F32),
        scratch_shapes=[pltpu.SemaphoreType.DMA, pltpu.SemaphoreType.DMA],
    )(g)


def _rs_tile(half):
    return _tile(half, (5448, 2724, 1816, 1024, 512, 256, 128, 64, 32, 16))


def _rs_add(g, got, c):
    half = got.shape[1]
    tr = _rs_tile(half)
    nb = half // tr

    def body(c_ref, g_ref, got_ref, o_ref):
        o_ref[...] = (g_ref[...] + got_ref[...]).astype(BF16)

    return pl.pallas_call(
        body, name="rs_add",
        grid_spec=pltpu.PrefetchScalarGridSpec(
            num_scalar_prefetch=1, grid=(4, nb),
            in_specs=[pl.BlockSpec((1, tr, LANES), lambda s, i, c_ref: (s, c_ref[0] * nb + i, 0)),
                      pl.BlockSpec((1, tr, LANES), lambda s, i, c_ref: (s, i, 0))],
            out_specs=pl.BlockSpec((1, tr, LANES), lambda s, i, c_ref: (s, i, 0))),
        out_shape=jax.ShapeDtypeStruct((4, half, LANES), BF16),
        compiler_params=_cparams(("parallel", "parallel")),
    )(c.reshape(1).astype(jnp.int32), g, got)


def _rs_ici(p):
    def body(p_ref, out_ref, send_sems, recv_sems, local_sem):
        x, y, c, chips = _place()
        me = 2 * x + y
        mine = pltpu.make_async_copy(p_ref.at[me], out_ref.at[me], local_sem)
        mine.start()
        sends = [_remote(p_ref.at[2 * cx + cy], out_ref.at[me], send_sems.at[j], recv_sems.at[j], (cx, cy, c))
                 for j, (cx, cy) in enumerate(chips)]
        for cp in sends:
            cp.start()
        for j, (cx, cy) in enumerate(chips):
            slot = out_ref.at[2 * cx + cy]
            _remote(slot, slot, send_sems.at[j], recv_sems.at[j], (cx, cy, c)).wait_recv()
        for cp in sends:
            cp.wait_send()
        mine.wait()

    return pl.pallas_call(
        body, name="rs_ici", in_specs=[ANY], out_specs=ANY, out_shape=jax.ShapeDtypeStruct(p.shape, p.dtype),
        scratch_shapes=[pltpu.SemaphoreType.DMA((3,)), pltpu.SemaphoreType.DMA((3,)), pltpu.SemaphoreType.DMA],
    )(p)


def _rs_sum4(parts):
    half = parts.shape[1]
    tr = _rs_tile(half)

    def body(p_ref, o_ref):
        acc = p_ref[0].astype(F32)
        for k in range(1, 4):
            acc = acc + p_ref[k].astype(F32)
        o_ref[...] = acc

    return pl.pallas_call(
        body, name="rs_sum4", grid=(half // tr,),
        in_specs=[pl.BlockSpec((4, tr, LANES), lambda i: (0, i, 0))],
        out_specs=pl.BlockSpec((tr, LANES), lambda i: (i, 0)),
        out_shape=jax.ShapeDtypeStruct((half, LANES), F32),
        compiler_params=_cparams(("parallel",)),
    )(parts)


def _rs_share(mine):
    half = mine.shape[0]

    def body(m_ref, out_ref, send_sem, recv_sem, local_sem):
        x, y, c, _ = _place()
        own = pltpu.make_async_copy(m_ref, out_ref.at[pl.ds(c * half, half), :], local_sem)
        own.start()
        cp = _remote(m_ref, out_ref.at[pl.ds(c * half, half), :], send_sem, recv_sem, (x, y, 1 - c))
        cp.start()
        other = out_ref.at[pl.ds((1 - c) * half, half), :]
        _remote(other, other, send_sem, recv_sem, (x, y, 1 - c)).wait_recv()
        cp.wait_send()
        own.wait()

    return pl.pallas_call(
        body, name="rs_share", in_specs=[ANY], out_specs=ANY,
        out_shape=jax.ShapeDtypeStruct((2 * half, LANES), F32),
        scratch_shapes=[pltpu.SemaphoreType.DMA, pltpu.SemaphoreType.DMA, pltpu.SemaphoreType.DMA],
    )(mine)


def _reduce_scatter(g, c):
    chip_sum = _rs_add(g, _rs_swap(g), c)
    return _rs_share(_rs_sum4(_rs_ici(chip_sum)))


def _adamw(w, g, m, v, name):
    shape = w.shape
    last = shape[-1]
    w2, g2, m2, v2 = (a.reshape(-1, last) for a in (w, g, m, v))
    rows = w2.shape[0]
    tr = _tile(rows, tuple(t for t in (1024, 512, 256, 128, 64, 32, 16, 8) if t * last * 4 <= 2 * 1024 * 1024) + (rows,))

    def body(w_ref, g_ref, m_ref, v_ref, d_ref, nm_ref, nv_ref):
        gv = g_ref[...]
        mn = ADAM_B1 * m_ref[...] + (1.0 - ADAM_B1) * gv
        vn = ADAM_B2 * v_ref[...] + (1.0 - ADAM_B2) * (gv * gv)
        m_hat = mn / (1.0 - ADAM_B1 ** ADAM_STEP)
        v_hat = vn / (1.0 - ADAM_B2 ** ADAM_STEP)
        d_ref[...] = -ADAM_LR * (m_hat / (jnp.sqrt(v_hat) + ADAM_EPS) + ADAM_WD * w_ref[...])
        nm_ref[...] = mn
        nv_ref[...] = vn

    blk = pl.BlockSpec((tr, last), lambda i: (i, 0))
    outs = pl.pallas_call(
        body, name=name, grid=(rows // tr,), in_specs=[blk] * 4, out_specs=[blk] * 3,
        out_shape=[jax.ShapeDtypeStruct((rows, last), F32)] * 3,
        compiler_params=_cparams(("parallel",)),
    )(w2, g2, m2, v2)
    return tuple(o.reshape(shape) for o in outs)


SHARD_AXIS = {"w_in": 2, "pool_w_group": 2, "w_pool_up": 1, "w_ssd_out": 1, "w_o": 1, "w_ff1": 2, "w_ff2": 1,
              "conv_w": 2, "meta_tokens": 1}
SMALL_SHARDED = ("conv_w", "meta_tokens")
REPLICATED = ("mix_norm_w", "b_gate", "pool_scale", "conv_b", "dt_bias", "a_log", "d_skip", "ssd_norm_w", "mlp_norm_w",
              "final_norm_w")
N_CHIPS = 4


def _flatten(arrs, dtype, row_mult):
    flat = jnp.concatenate([a.astype(dtype).reshape(-1) for a in arrs])
    n = flat.shape[0]
    rows = -(-n // (LANES * row_mult)) * row_mult
    return jnp.pad(flat, (0, rows * LANES - n)).reshape(rows, LANES)


def _unflatten(flat2d, shapes):
    flat = flat2d.reshape(-1)
    out, off = [], 0
    for sh in shapes:
        n = 1
        for d in sh:
            n *= d
        out.append(flat[off:off + n].reshape(sh))
        off += n
    return out


def _shard_of(a, axis, k):
    n = a.shape[axis] // N_CHIPS
    return lax.slice_in_dim(a, k * n, (k + 1) * n, axis=axis)


def kernel(x, meta_tokens, mix_norm_w, w_in, b_gate, pool_w_group, pool_scale, w_pool_up, conv_w, conv_b, dt_bias, a_log, d_skip, ssd_norm_w, w_ssd_out, w_o, mlp_norm_w, w_ff1, w_ff2, final_norm_w, loss_target, m_meta_tokens, m_mix_norm_w, m_w_in, m_b_gate, m_pool_w_group, m_pool_scale, m_w_pool_up, m_conv_w, m_conv_b, m_dt_bias, m_a_log, m_d_skip, m_ssd_norm_w, m_w_ssd_out, m_w_o, m_mlp_norm_w, m_w_ff1, m_w_ff2, m_final_norm_w, v_meta_tokens, v_mix_norm_w, v_w_in, v_b_gate, v_pool_w_group, v_pool_scale, v_w_pool_up, v_conv_w, v_conv_b, v_dt_bias, v_a_log, v_d_skip, v_ssd_norm_w, v_w_ssd_out, v_w_o, v_mlp_norm_w, v_w_ff1, v_w_ff2, v_final_norm_w):
    names = ("meta_tokens",) + PER_LAYER + ("final_norm_w",)
    par = dict(meta_tokens=meta_tokens, mix_norm_w=mix_norm_w, w_in=w_in, b_gate=b_gate, pool_w_group=pool_w_group,
               pool_scale=pool_scale, w_pool_up=w_pool_up, conv_w=conv_w, conv_b=conv_b, dt_bias=dt_bias, a_log=a_log,
               d_skip=d_skip, ssd_norm_w=ssd_norm_w, w_ssd_out=w_ssd_out, w_o=w_o, mlp_norm_w=mlp_norm_w, w_ff1=w_ff1,
               w_ff2=w_ff2, final_norm_w=final_norm_w)
    mom = dict(meta_tokens=m_meta_tokens, mix_norm_w=m_mix_norm_w, w_in=m_w_in, b_gate=m_b_gate, pool_w_group=m_pool_w_group,
               pool_scale=m_pool_scale, w_pool_up=m_w_pool_up, conv_w=m_conv_w, conv_b=m_conv_b, dt_bias=m_dt_bias,
               a_log=m_a_log, d_skip=m_d_skip, ssd_norm_w=m_ssd_norm_w, w_ssd_out=m_w_ssd_out, w_o=m_w_o,
               mlp_norm_w=m_mlp_norm_w, w_ff1=m_w_ff1, w_ff2=m_w_ff2, final_norm_w=m_final_norm_w)
    var = dict(meta_tokens=v_meta_tokens, mix_norm_w=v_mix_norm_w, w_in=v_w_in, b_gate=v_b_gate, pool_w_group=v_pool_w_group,
               pool_scale=v_pool_scale, w_pool_up=v_w_pool_up, conv_w=v_conv_w, conv_b=v_conv_b, dt_bias=v_dt_bias,
               a_log=v_a_log, d_skip=v_d_skip, ssd_norm_w=v_ssd_norm_w, w_ssd_out=v_w_ssd_out, w_o=v_w_o,
               mlp_norm_w=v_mlp_norm_w, w_ff1=v_w_ff1, w_ff2=v_w_ff2, final_norm_w=v_final_norm_w)
    chip = 2 * lax.axis_index("x") + lax.axis_index("y")
    core = lax.axis_index("c")

    big_shapes = [par[n].shape for n in BIG]
    got = _gather_big(_flatten([par[n] for n in BIG], BF16, 32))
    full = {n: par[n] for n in REPLICATED}
    pieces = [_unflatten(got[k], big_shapes) for k in range(N_CHIPS)]
    for j, n in enumerate(BIG):
        full[n] = jnp.concatenate([pieces[k][j] for k in range(N_CHIPS)], axis=SHARD_AXIS[n])
    small_shapes = [par[n].shape for n in SMALL_SHARDED]
    got_small = _exchange_small(_flatten([par[n] for n in SMALL_SHARDED], F32, 8), False, "gather_small")
    pieces = [_unflatten(got_small[2 * k], small_shapes) for k in range(N_CHIPS)]
    for j, n in enumerate(SMALL_SHARDED):
        full[n] = jnp.concatenate([pieces[k][j] for k in range(N_CHIPS)], axis=SHARD_AXIS[n])

    loss, dx, grads = _local_step(x[0], loss_target[0], full)
    loss = lax.psum(loss, ("x", "y", "c"))

    small_names = REPLICATED + SMALL_SHARDED
    red = _exchange_small(_flatten([grads[n] for n in small_names], F32, 8), True, "allreduce_small")
    gsum = dict(zip(small_names, _unflatten(red, [grads[n].shape for n in small_names])))
    for n in SMALL_SHARDED:
        width = par[n].shape[SHARD_AXIS[n]]
        gsum[n] = lax.dynamic_slice_in_dim(gsum[n], chip * width, width, axis=SHARD_AXIS[n])

    slabs = [_flatten([_shard_of(grads[n], SHARD_AXIS[n], k) for n in BIG], F32, 32) for k in range(N_CHIPS)]
    gbig = _reduce_scatter(jnp.stack(slabs), core)
    gsum.update(zip(BIG, _unflatten(gbig, big_shapes)))

    delta, new_m, new_v = {}, {}, {}
    for n in BIG:
        delta[n], new_m[n], new_v[n] = _adamw(par[n], gsum[n], mom[n], var[n], "adamw_" + n)
    rest = [n for n in names if n not in BIG]
    shapes = [par[n].shape for n in rest]
    packed = [_flatten([d[n] for n in rest], F32, 8) for d in (par, gsum, mom, var)]
    for d, flat in zip((delta, new_m, new_v), _adamw(*packed, "adamw_small")):
        d.update(zip(rest, _unflatten(flat, shapes)))

    order = ("meta_tokens", "mix_norm_w", "w_in", "b_gate", "pool_w_group", "pool_scale", "w_pool_up", "conv_w", "conv_b",
             "dt_bias", "a_log", "d_skip", "ssd_norm_w", "w_ssd_out", "w_o", "mlp_norm_w", "w_ff1", "w_ff2", "final_norm_w")
    return (loss, dx[None], *[gsum[n] for n in order], *[delta[n] for n in order], *[new_m[n] for n in order],
            *[new_v[n] for n in order])
```

```python
import functools

import jax
import jax.numpy as jnp
from jax import lax
from jax.experimental import pallas as pl
from jax.experimental.pallas import tpu as pltpu

F32 = jnp.float32
BF16 = jnp.bfloat16

D_MODEL = 1024
DEPTH = 4
N_META = 16
N_PAD = 112
ROW_X = N_PAD + N_META
POOL_WINDOWS = (2, 4, 8, 16)
POOL_GDIM = 256
D_INNER = 2048
N_HEADS = 32
HEAD_DIM = 64
N_GROUPS = 8
HEADS_PER_GROUP = 4
GROUP_W = HEADS_PER_GROUP * HEAD_DIM
D_STATE = 128
CHUNK = 128
D_XBC = 4096
D_FF = 4096
EPS = 1e-5
OFF_Z, OFF_XBC, OFF_DT, OFF_GATE, IN_COLS = 1024, 3072, 7168, 7200, 9248
PZ, PPOOL, PGATE, PXBC, PCOLS = 0, 2048, 3072, 5120, 9216
DT_PAD = 128

ADAM_LR, ADAM_B1, ADAM_B2, ADAM_EPS, ADAM_WD, ADAM_STEP = 0.001, 0.9, 0.999, 1e-08, 0.01, 10

VMEM_LIMIT = 56 * 1024 * 1024

_NN = (((1,), (0,)), ((), ()))
_NT = (((1,), (1,)), ((), ()))
_TN = (((0,), (0,)), ((), ()))


def _dot(a, b, dn=_NN):
    return lax.dot_general(a, b, dn, preferred_element_type=F32)


def _cparams(sem):
    return pltpu.CompilerParams(dimension_semantics=sem, vmem_limit_bytes=VMEM_LIMIT)


def _tile(n, cands):
    for c in cands:
        if n % c == 0:
            return c
    raise ValueError(f"no tile for {n} in {cands}")


def _sigmoid(x):
    return 1.0 / (1.0 + jnp.exp(-x))


def _softplus(x):
    return jnp.maximum(x, 0.0) + jnp.log(1.0 + jnp.exp(-jnp.abs(x)))


def _bs(shape, fn):
    return pl.BlockSpec(shape, fn)


def _mm(name, dn, grid, a, a_spec, b, b_spec, outs, *, extras=(), epilogue=None, into=None):
    nk = grid[2]
    ne, no = len(extras), len(outs)
    ni = 0 if into is None else 1
    blk = tuple(d for d in outs[0][2].block_shape if d is not None)

    def body(a_ref, b_ref, *rest):
        e_refs, o_refs = rest[:ne], rest[ne + ni:ne + ni + no]
        p = _dot(a_ref[...].astype(BF16), b_ref[...].astype(BF16), dn)

        def finish(acc):
            outs = epilogue(acc, *[e[...] for e in e_refs]) if epilogue is not None else (acc,)
            for o, v in zip(o_refs, outs):
                o[...] = v.astype(o.dtype)

        if nk == 1:
            finish(p)
        else:
            acc_ref = rest[ne + ni + no]
            kk = pl.program_id(2)

            @pl.when(kk == 0)
            def _():
                acc_ref[...] = p

            @pl.when(kk > 0)
            def _():
                acc_ref[...] += p

            @pl.when(kk == nk - 1)
            def _():
                finish(acc_ref[...])

    res = pl.pallas_call(
        body,
        name=name,
        grid=grid,
        in_specs=[a_spec, b_spec, *[s for _, s in extras]] + ([pl.BlockSpec(memory_space=pl.ANY)] if ni else []),
        out_specs=[o[2] for o in outs],
        out_shape=[jax.ShapeDtypeStruct(o[0], o[1]) for o in outs],
        scratch_shapes=[pltpu.VMEM(blk, F32)] if nk > 1 else [],
        input_output_aliases={2 + ne: 0} if ni else {},
        compiler_params=_cparams(("parallel", "parallel", "arbitrary")),
    )(a, b, *[e for e, _ in extras], *([into] if ni else []))
    return res[0] if no == 1 else res


def _norm_fwd(h, w, name):
    t, d = h.shape
    tm = _tile(t, (1056, 384, 128))

    def body(h_ref, w_ref, u_ref):
        x = h_ref[...]
        r = lax.rsqrt(jnp.mean(x * x, axis=-1, keepdims=True) + EPS)
        u_ref[...] = (x * r * w_ref[...]).astype(BF16)

    return pl.pallas_call(
        body, name=name, grid=(t // tm,),
        in_specs=[pl.BlockSpec((tm, d), lambda i: (i, 0)), pl.BlockSpec((1, d), lambda i: (0, 0))],
        out_specs=pl.BlockSpec((tm, d), lambda i: (i, 0)),
        out_shape=jax.ShapeDtypeStruct((t, d), BF16),
        compiler_params=_cparams(("parallel",)),
    )(h, w.reshape(1, d))


def _norm_bwd(dres, du, h, w, name):
    t, d = h.shape
    tm = _tile(t, (528, 384, 128))

    def body(dres_ref, du_ref, h_ref, w_ref, dh_ref, dw_ref):
        x = h_ref[...]
        r = lax.rsqrt(jnp.mean(x * x, axis=-1, keepdims=True) + EPS)
        xhat = x * r
        du_v = du_ref[...]
        g = du_v * w_ref[...]
        dh_ref[...] = dres_ref[...] + r * (g - xhat * jnp.mean(g * xhat, axis=-1, keepdims=True))

        @pl.when(pl.program_id(0) == 0)
        def _():
            dw_ref[...] = jnp.zeros_like(dw_ref)

        dw_ref[...] += jnp.sum(du_v * xhat, axis=0, keepdims=True)

    row = pl.BlockSpec((tm, d), lambda i: (i, 0))
    vec = pl.BlockSpec((1, d), lambda i: (0, 0))
    dh, dw = pl.pallas_call(
        body, name=name, grid=(t // tm,),
        in_specs=[row, row, row, vec], out_specs=[row, vec],
        out_shape=[jax.ShapeDtypeStruct((t, d), F32), jax.ShapeDtypeStruct((1, d), F32)],
        compiler_params=_cparams(("arbitrary",)),
    )(dres, du, h, w.reshape(1, d))
    return dh, dw.reshape(d)


def _loss_head(h, w, tgt):
    t, d = h.shape
    tm = CHUNK
    nb = ROW_X // tm

    def body(h_ref, w_ref, t_ref, loss_ref, dh_ref, dw_ref):
        i = pl.program_id(0)
        x = h_ref[...]
        r = lax.rsqrt(jnp.mean(x * x, axis=-1, keepdims=True) + EPS)
        xhat = x * r
        wv = w_ref[...]
        live = i >= nb
        err = jnp.where(live, xhat * wv - t_ref[...], 0.0)
        dout = err * (1.0 / d)
        g = dout * wv
        dh_ref[...] = r * (g - xhat * jnp.mean(g * xhat, axis=-1, keepdims=True))

        @pl.when(i == 0)
        def _():
            loss_ref[...] = jnp.zeros_like(loss_ref)
            dw_ref[...] = jnp.zeros_like(dw_ref)

        loss_ref[...] += 0.5 * jnp.sum(jnp.sum(err * err, axis=-1, keepdims=True), axis=0, keepdims=True) * (1.0 / d)
        dw_ref[...] += jnp.sum(dout * xhat, axis=0, keepdims=True)

    row = pl.BlockSpec((tm, d), lambda i: (i, 0))
    vec = pl.BlockSpec((1, d), lambda i: (0, 0))
    loss, dh, dw = pl.pallas_call(
        body, name="loss_head", grid=(t // tm,),
        in_specs=[row, vec, pl.BlockSpec((tm, d), lambda i: (jnp.maximum(i - nb, 0), 0))],
        out_specs=[pl.BlockSpec((1, 1), lambda i: (0, 0)), row, vec],
        out_shape=[jax.ShapeDtypeStruct((1, 1), F32), jax.ShapeDtypeStruct((t, d), F32), jax.ShapeDtypeStruct((1, d), F32)],
        compiler_params=_cparams(("arbitrary",)),
    )(h, w.reshape(1, d), tgt)
    return loss[0, 0], dh, dw.reshape(d)


def _gate_fwd(proj, b_gate, y_pool, y_ssd):
    t = proj.shape[0]
    d = D_MODEL
    tm = _tile(t, (528, 384, 128))

    def body(gp_ref, gs_ref, bp_ref, bs_ref, yp_ref, ys_ref, o_ref):
        gp = _sigmoid(gp_ref[...] + bp_ref[...])
        gs = _sigmoid(gs_ref[...] + bs_ref[...])
        o_ref[...] = (gp * yp_ref[...] + gs * ys_ref[...]).astype(BF16)

    row = pl.BlockSpec((tm, d), lambda i: (i, 0))
    return pl.pallas_call(
        body, name="gate_fwd", grid=(t // tm,),
        in_specs=[pl.BlockSpec((tm, d), lambda i: (i, PGATE // d)), pl.BlockSpec((tm, d), lambda i: (i, PGATE // d + 1)),
                  pl.BlockSpec((1, d), lambda i: (0, 0)), pl.BlockSpec((1, d), lambda i: (0, 1)), row, row],
        out_specs=row, out_shape=jax.ShapeDtypeStruct((t, d), BF16),
        compiler_params=_cparams(("parallel",)),
    )(proj, proj, b_gate.reshape(1, 2 * d), b_gate.reshape(1, 2 * d), y_pool, y_ssd)


def _gate_bwd(dmerged, proj, b_gate, y_pool, y_ssd):
    t = proj.shape[0]
    d = D_MODEL
    tm = _tile(t, (384, 128))

    def body(dm_ref, gp_ref, gs_ref, bp_ref, bs_ref, yp_ref, ys_ref, dyp_ref, dys_ref, dg_ref, db_ref):
        dm = dm_ref[...]
        gp = _sigmoid(gp_ref[...] + bp_ref[...])
        gs = _sigmoid(gs_ref[...] + bs_ref[...])
        dyp_ref[...] = (dm * gp).astype(BF16)
        dys_ref[...] = (dm * gs).astype(BF16)
        dgp = dm * yp_ref[...] * gp * (1.0 - gp)
        dgs = dm * ys_ref[...] * gs * (1.0 - gs)
        dg_ref[:, :d] = dgp.astype(BF16)
        dg_ref[:, d:] = dgs.astype(BF16)

        @pl.when(pl.program_id(0) == 0)
        def _():
            db_ref[...] = jnp.zeros_like(db_ref)

        db_ref[:, :d] += jnp.sum(dgp, axis=0, keepdims=True)
        db_ref[:, d:] += jnp.sum(dgs, axis=0, keepdims=True)

    row = pl.BlockSpec((tm, d), lambda i: (i, 0))
    dyp, dys, dg, db = pl.pallas_call(
        body, name="gate_bwd", grid=(t // tm,),
        in_specs=[row, pl.BlockSpec((tm, d), lambda i: (i, PGATE // d)), pl.BlockSpec((tm, d), lambda i: (i, PGATE // d + 1)),
                  pl.BlockSpec((1, d), lambda i: (0, 0)), pl.BlockSpec((1, d), lambda i: (0, 1)), row, row],
        out_specs=[row, row, pl.BlockSpec((tm, 2 * d), lambda i: (i, 0)), pl.BlockSpec((1, 2 * d), lambda i: (0, 0))],
        out_shape=[jax.ShapeDtypeStruct((t, d), BF16), jax.ShapeDtypeStruct((t, d), BF16),
                   jax.ShapeDtypeStruct((t, 2 * d), BF16), jax.ShapeDtypeStruct((1, 2 * d), F32)],
        compiler_params=_cparams(("arbitrary",)),
    )(dmerged, proj, proj, b_gate.reshape(1, 2 * d), b_gate.reshape(1, 2 * d), y_pool, y_ssd)
    return dyp, dys, dg, db.reshape(2 * d)


POOL_HALO = 16


def _pool_counts(row0, n, win):
    pos1 = row0 + lax.broadcasted_iota(jnp.int32, (n, 1), 0) - (N_PAD - 1)
    return jnp.clip(pos1, 1, win).astype(F32)


N_CHIPS = 4
WG_ROWS = POOL_GDIM // N_CHIPS
WUP_ROWS = D_MODEL // N_CHIPS


def _group_w(wg_ref, g):
    return jnp.concatenate([wg_ref[k, g] for k in range(N_CHIPS)], axis=0)


def _pool_w_specs(layer):
    return [pl.BlockSpec((None, N_CHIPS, len(POOL_WINDOWS), WG_ROWS, POOL_GDIM), lambda i: (layer, 0, 0, 0, 0)),
            pl.BlockSpec((None, N_CHIPS, WUP_ROWS, D_MODEL), lambda i: (layer, 0, 0, 0))]


def _pool_fwd(proj, wg, scale, wup, layer):
    t = proj.shape[0]
    d = D_MODEL
    tm = _tile(t, (384, 128))
    hb = tm // POOL_HALO

    def body(u_ref, halo_ref, sc_ref, wg_ref, wup_ref, pooled_ref, yg_ref, ypm_ref, yp_ref):
        i = pl.program_id(0)
        x = u_ref[...]
        halo = jnp.where(i > 0, halo_ref[...], 0.0)
        xc = jnp.concatenate([halo, x], axis=0)
        for g, win in enumerate(POOL_WINDOWS):
            sl = slice(g * POOL_GDIM, (g + 1) * POOL_GDIM)
            s = xc[:, sl]
            k = 1
            while k < win:
                s = s + pltpu.roll(s, k, axis=0)
                k *= 2
            pooled = s[POOL_HALO:] / _pool_counts(i * tm, tm, win) - x[:, sl]
            pb = pooled.astype(BF16)
            pooled_ref[:, sl] = pb
            yg_ref[:, sl] = _dot(pb, _group_w(wg_ref, g))
        ypm = (yg_ref[...] * sc_ref[...]).astype(BF16)
        ypm_ref[...] = ypm
        acc = _dot(ypm[:, :WUP_ROWS], wup_ref[0])
        for k in range(1, N_CHIPS):
            acc = acc + _dot(ypm[:, k * WUP_ROWS:(k + 1) * WUP_ROWS], wup_ref[k])
        yp_ref[...] = acc

    row = pl.BlockSpec((tm, d), lambda i: (i, 0))
    return pl.pallas_call(
        body, name=f"pool_fwd_{layer}", grid=(t // tm,),
        in_specs=[pl.BlockSpec((tm, d), lambda i: (i, PPOOL // d)),
                  pl.BlockSpec((POOL_HALO, d), lambda i: (jnp.maximum(i * hb - 1, 0), PPOOL // d)),
                  pl.BlockSpec((1, d), lambda i: (0, 0))] + _pool_w_specs(layer),
        out_specs=[row, row, row, row],
        out_shape=[jax.ShapeDtypeStruct((t, d), BF16), jax.ShapeDtypeStruct((t, d), F32),
                   jax.ShapeDtypeStruct((t, d), BF16), jax.ShapeDtypeStruct((t, d), F32)],
        compiler_params=_cparams(("parallel",)),
    )(proj, proj, scale.reshape(1, d), wg, wup)


def _pool_bwd_a(dy_pool, yg, scale, wg, wup, layer):
    t, d = yg.shape
    tm = _tile(t, (384, 128))

    def body(dy_ref, yg_ref, sc_ref, wg_ref, wup_ref, q_ref, dyg_ref, dsc_ref):
        dy = dy_ref[...]
        dypm = jnp.concatenate([_dot(dy, wup_ref[k], _NT) for k in range(N_CHIPS)], axis=1)

        @pl.when(pl.program_id(0) == 0)
        def _():
            dsc_ref[...] = jnp.zeros_like(dsc_ref)

        dsc_ref[...] += jnp.sum(dypm * yg_ref[...], axis=0, keepdims=True)
        dyg = (dypm * sc_ref[...]).astype(BF16)
        dyg_ref[...] = dyg
        for g in range(len(POOL_WINDOWS)):
            sl = slice(g * POOL_GDIM, (g + 1) * POOL_GDIM)
            q_ref[:, sl] = _dot(dyg[:, sl], _group_w(wg_ref, g), _NT)

    row = pl.BlockSpec((tm, d), lambda i: (i, 0))
    vec = pl.BlockSpec((1, d), lambda i: (0, 0))
    q, dyg, dsc = pl.pallas_call(
        body, name=f"pool_bwd_a_{layer}", grid=(t // tm,),
        in_specs=[row, row, vec] + _pool_w_specs(layer),
        out_specs=[row, row, vec],
        out_shape=[jax.ShapeDtypeStruct((t, d), F32), jax.ShapeDtypeStruct((t, d), BF16), jax.ShapeDtypeStruct((1, d), F32)],
        compiler_params=_cparams(("arbitrary",)),
    )(dy_pool, yg, scale.reshape(1, d), wg, wup)
    return q, dyg, dsc.reshape(d)


def _pool_bwd_b(q):
    t, d = q.shape
    tm = _tile(t, (384, 128))
    hb = tm // POOL_HALO
    nt = t // tm
    n = tm + POOL_HALO

    def body(q_ref, halo_ref, o_ref):
        i = pl.program_id(0)
        qv = q_ref[...]
        halo = jnp.where(i < nt - 1, halo_ref[...], 0.0)
        qc = jnp.concatenate([qv, halo], axis=0)
        for g, win in enumerate(POOL_WINDOWS):
            sl = slice(g * POOL_GDIM, (g + 1) * POOL_GDIM)
            s = qc[:, sl] / _pool_counts(i * tm, n, win)
            k = 1
            while k < win:
                s = s + pltpu.roll(s, n - k, axis=0)
                k *= 2
            o_ref[:, sl] = (s[:tm] - qv[:, sl]).astype(BF16)

    row = pl.BlockSpec((tm, d), lambda i: (i, 0))
    return pl.pallas_call(
        body, name="pool_bwd_b", grid=(nt,),
        in_specs=[row, pl.BlockSpec((POOL_HALO, d), lambda i: (jnp.minimum((i + 1) * hb, t // POOL_HALO - 1), 0))],
        out_specs=row, out_shape=jax.ShapeDtypeStruct((t, d), BF16),
        compiler_params=_cparams(("parallel",)),
    )(q, q)


def _pool_dwg(pooled, dyg, layer, into):
    t, d = pooled.shape
    tk = _tile(t, (1056, 384, 128))
    nk = t // tk
    gd = POOL_GDIM
    ng = d // gd

    def body(p_ref, g_ref, *rest):
        o_ref = rest[-1]

        @pl.when(pl.program_id(1) == 0)
        def _():
            o_ref[...] = jnp.zeros_like(o_ref)

        part = _dot(p_ref[...], g_ref[...], _TN)
        for k in range(N_CHIPS):
            o_ref[k] += part[k * WG_ROWS:(k + 1) * WG_ROWS]

    blk = pl.BlockSpec((tk, gd), lambda g, k: (k, g))
    ni = 0 if into is None else 1
    return pl.pallas_call(
        body, name=f"pool_dwg_{layer}", grid=(ng, nk),
        in_specs=[blk, blk] + [pl.BlockSpec(memory_space=pl.ANY)] * ni,
        out_specs=pl.BlockSpec((None, N_CHIPS, None, WG_ROWS, gd), lambda g, k: (layer, 0, g, 0, 0)),
        out_shape=jax.ShapeDtypeStruct((DEPTH, N_CHIPS, ng, WG_ROWS, gd), F32),
        input_output_aliases={2: 0} if ni else {},
        compiler_params=_cparams(("parallel", "arbitrary")),
    )(pooled, dyg, *([into] if ni else []))


CONV_W = 4
CONV_HALO = 8
XBC_BLK = PXBC // 1024


def _conv_fwd(proj, conv_w, conv_b):
    t = proj.shape[0]
    cw = 1024
    tm = _tile(t, (1056, 384, 128))
    hb = tm // CONV_HALO

    def body(x_ref, halo_ref, w_ref, b_ref, o_ref):
        i = pl.program_id(1)
        x = x_ref[...]
        halo = jnp.where(i > 0, halo_ref[...], 0.0)
        xc = jnp.concatenate([halo, x], axis=0)
        w = w_ref[...]
        acc = b_ref[...] + x * w[CONV_W - 1:CONV_W, :]
        for k in range(CONV_W - 1):
            acc = acc + pltpu.roll(xc, CONV_W - 1 - k, axis=0)[CONV_HALO:] * w[k:k + 1, :]
        row = i * tm + lax.broadcasted_iota(jnp.int32, (tm, 1), 0)
        o_ref[...] = jnp.where(row >= N_PAD, acc * _sigmoid(acc), 0.0)

    return pl.pallas_call(
        body, name="conv_fwd", grid=(D_XBC // cw, t // tm),
        in_specs=[pl.BlockSpec((tm, cw), lambda j, i: (i, XBC_BLK + j)),
                  pl.BlockSpec((CONV_HALO, cw), lambda j, i: (jnp.maximum(i * hb - 1, 0), XBC_BLK + j)),
                  pl.BlockSpec((CONV_W, cw), lambda j, i: (0, j)), pl.BlockSpec((1, cw), lambda j, i: (0, j))],
        out_specs=pl.BlockSpec((tm, cw), lambda j, i: (i, j)),
        out_shape=jax.ShapeDtypeStruct((t, D_XBC), F32),
        compiler_params=_cparams(("parallel", "parallel")),
    )(proj, proj, conv_w, conv_b.reshape(1, D_XBC))


def _conv_bwd(dxa, coff, proj, conv_w, conv_b):
    t, ncols = dxa.shape
    cw = 1024
    tm = _tile(t, (528, 384, 128))
    hb = tm // CONV_HALO
    nt = t // tm
    n = tm + 2 * CONV_HALO

    def body(d_ref, dn_ref, xp_ref, x_ref, xn_ref, w_ref, b_ref, o_ref, dw_ref, db_ref):
        i = pl.program_id(1)
        last = i == nt - 1
        xf = jnp.concatenate([jnp.where(i > 0, xp_ref[...], 0.0), x_ref[...], jnp.where(last, 0.0, xn_ref[...])], axis=0)
        df = jnp.concatenate([jnp.zeros((CONV_HALO, cw), F32), d_ref[...], jnp.where(last, 0.0, dn_ref[...])], axis=0)
        w = w_ref[...]
        sh = [pltpu.roll(xf, CONV_W - 1 - k, axis=0) if k < CONV_W - 1 else xf for k in range(CONV_W)]
        xc = b_ref[...]
        for k in range(CONV_W):
            xc = xc + sh[k] * w[k:k + 1, :]
        sig = _sigmoid(xc)
        row = i * tm - CONV_HALO + lax.broadcasted_iota(jnp.int32, (n, 1), 0)
        dxc = jnp.where((row >= N_PAD) & (row < t), df * (sig * (1.0 + xc * (1.0 - sig))), 0.0)
        acc = dxc * w[CONV_W - 1:CONV_W, :]
        for k in range(CONV_W - 1):
            acc = acc + pltpu.roll(dxc, n - (CONV_W - 1 - k), axis=0) * w[k:k + 1, :]
        o_ref[...] = acc[CONV_HALO:CONV_HALO + tm].astype(BF16)

        @pl.when(i == 0)
        def _():
            dw_ref[...] = jnp.zeros_like(dw_ref)
            db_ref[...] = jnp.zeros_like(db_ref)

        dm = dxc[CONV_HALO:CONV_HALO + tm]
        db_ref[...] += jnp.sum(dm, axis=0, keepdims=True)
        for k in range(CONV_W):
            dw_ref[k:k + 1, :] += jnp.sum(dm * sh[k][CONV_HALO:CONV_HALO + tm], axis=0, keepdims=True)

    def xspec(rows, fn):
        return pl.BlockSpec((rows, cw), lambda j, i: (fn(i), XBC_BLK + coff + j))

    prev = lambda i: jnp.maximum(i * hb - 1, 0)
    nxt = lambda i: jnp.minimum((i + 1) * hb, t // CONV_HALO - 1)
    dxbc, dw, db = pl.pallas_call(
        body, name=f"conv_bwd_{coff}", grid=(ncols // cw, nt),
        in_specs=[pl.BlockSpec((tm, cw), lambda j, i: (i, j)), pl.BlockSpec((CONV_HALO, cw), lambda j, i: (nxt(i), j)),
                  xspec(CONV_HALO, prev), xspec(tm, lambda i: i), xspec(CONV_HALO, nxt),
                  pl.BlockSpec((CONV_W, cw), lambda j, i: (0, coff + j)), pl.BlockSpec((1, cw), lambda j, i: (0, coff + j))],
        out_specs=[pl.BlockSpec((tm, cw), lambda j, i: (i, j)), pl.BlockSpec((CONV_W, cw), lambda j, i: (0, j)),
                   pl.BlockSpec((1, cw), lambda j, i: (0, j))],
        out_shape=[jax.ShapeDtypeStruct((t, ncols), BF16), jax.ShapeDtypeStruct((CONV_W, ncols), F32),
                   jax.ShapeDtypeStruct((1, ncols), F32)],
        compiler_params=_cparams(("parallel", "arbitrary")),
    )(dxa, dxa, proj, proj, proj, conv_w, conv_b.reshape(1, D_XBC))
    return dxbc, dw, db.reshape(ncols)


def _cumsum(x, axis, reverse=False):
    n = x.shape[axis]
    idx = lax.broadcasted_iota(jnp.int32, x.shape, axis)
    k = 1
    while k < n:
        if reverse:
            x = x + jnp.where(idx < n - k, pltpu.roll(x, n - k, axis=axis), 0.0)
        else:
            x = x + jnp.where(idx >= k, pltpu.roll(x, k, axis=axis), 0.0)
        k *= 2
    return x


def _head_masks():
    lane = lax.broadcasted_iota(jnp.int32, (1, GROUP_W), 1)
    return [(lane >= r * HEAD_DIM) & (lane < (r + 1) * HEAD_DIM) for r in range(HEADS_PER_GROUP)]


def _expand_heads(cols, hm):
    out = jnp.where(hm[0], cols[:, 0:1], 0.0)
    for r in range(1, HEADS_PER_GROUP):
        out = out + jnp.where(hm[r], cols[:, r:r + 1], 0.0)
    return out


def _ssd_common(chunk, dtg_ref, dtt_ref, br_ref, bc_ref, ar_ref, ac_ref):
    rowid = chunk * CHUNK + lax.broadcasted_iota(jnp.int32, (CHUNK, 1), 0)
    laneid = chunk * CHUNK + lax.broadcasted_iota(jnp.int32, (1, CHUNK), 1)
    raw_c = dtg_ref[0] + br_ref[0]
    raw_r = dtt_ref[0] + bc_ref[0]
    dtc = jnp.where(rowid >= N_PAD, _softplus(raw_c), 0.0)
    dtr = jnp.where(laneid >= N_PAD, _softplus(raw_r), 0.0)
    a_r = -jnp.exp(ar_ref[0])
    a_c = -jnp.exp(ac_ref[0])
    hm = _head_masks()
    dt_exp = _expand_heads(dtc, hm)
    acs = _cumsum(dt_exp * _expand_heads(a_r, hm), 0)
    acs_r = _cumsum(dtr * a_c, 1)
    atot = acs[CHUNK - 1:CHUNK, :]
    return dict(rowid=rowid, raw_c=raw_c, dtc=dtc, a_r=a_r, hm=hm, dt_exp=dt_exp, acs=acs, acs_r=acs_r,
                ea=jnp.exp(acs), ds=jnp.exp(atot - acs), dec=jnp.exp(atot))


def _ssd_specs(nc, cidx):
    gw = GROUP_W
    return [
        pl.BlockSpec((CHUNK, gw), lambda g, c: (cidx(c), g)),
        pl.BlockSpec((CHUNK, D_STATE), lambda g, c: (cidx(c), D_INNER // D_STATE + g)),
        pl.BlockSpec((CHUNK, D_STATE), lambda g, c: (cidx(c), (D_INNER + 1024) // D_STATE + g)),
        pl.BlockSpec((1, CHUNK, HEADS_PER_GROUP), lambda g, c: (g, cidx(c), 0)),
        pl.BlockSpec((1, HEADS_PER_GROUP, CHUNK), lambda g, c: (g, 0, cidx(c))),
        pl.BlockSpec((1, 1, HEADS_PER_GROUP), lambda g, c: (g, 0, 0)),
        pl.BlockSpec((1, HEADS_PER_GROUP, 1), lambda g, c: (g, 0, 0)),
        pl.BlockSpec((1, 1, HEADS_PER_GROUP), lambda g, c: (g, 0, 0)),
        pl.BlockSpec((1, HEADS_PER_GROUP, 1), lambda g, c: (g, 0, 0)),
    ]


def _ssd_small(dt_raw, dt_bias, a_log):
    t = dt_raw.shape[0]
    dtg = dt_raw[:, :N_HEADS].reshape(t, N_GROUPS, HEADS_PER_GROUP).transpose(1, 0, 2)
    dtt = dtg.transpose(0, 2, 1)
    return (dtg, dtt, dt_bias.reshape(N_GROUPS, 1, HEADS_PER_GROUP), dt_bias.reshape(N_GROUPS, HEADS_PER_GROUP, 1),
            a_log.reshape(N_GROUPS, 1, HEADS_PER_GROUP), a_log.reshape(N_GROUPS, HEADS_PER_GROUP, 1))


def _ssd_fwd(xa, dt_raw, dt_bias, a_log):
    t = xa.shape[0]
    nc = t // CHUNK
    gw = GROUP_W

    def body(xs_ref, b_ref, c_ref, dtg_ref, dtt_ref, br_ref, bc_ref, ar_ref, ac_ref, y_ref, prev_ref, st_ref):
        c = pl.program_id(1)

        @pl.when(c == 0)
        def _():
            st_ref[...] = jnp.zeros_like(st_ref)

        q = _ssd_common(c, dtg_ref, dtt_ref, br_ref, bc_ref, ar_ref, ac_ref)
        hm = q["hm"]
        xdt = xs_ref[...] * q["dt_exp"]
        bm = b_ref[...].astype(BF16)
        cm = c_ref[...].astype(BF16)
        cb = _dot(cm, bm, _NT)
        st = st_ref[...]
        prev_ref[0, 0] = st
        y = _dot(cm, st.astype(BF16)) * q["ea"]
        tril = lax.broadcasted_iota(jnp.int32, (CHUNK, CHUNK), 0) >= lax.broadcasted_iota(jnp.int32, (CHUNK, CHUNK), 1)
        for r in range(HEADS_PER_GROUP):
            a_col = q["acs"][:, r * HEAD_DIM:r * HEAD_DIM + 1]
            a_row = q["acs_r"][r:r + 1, :]
            lm = jnp.exp(jnp.where(tril, a_col - a_row, -jnp.inf))
            y = y + _dot((cb * lm).astype(BF16), jnp.where(hm[r], xdt, 0.0).astype(BF16))
        y_ref[...] = y
        st_ref[...] = q["dec"] * st + _dot(bm, (xdt * q["ds"]).astype(BF16), _TN)

    y, prev = pl.pallas_call(
        body, name="ssd_fwd", grid=(N_GROUPS, nc),
        in_specs=_ssd_specs(nc, lambda c: c),
        out_specs=[pl.BlockSpec((CHUNK, gw), lambda g, c: (c, g)),
                   pl.BlockSpec((1, 1, D_STATE, gw), lambda g, c: (c, g, 0, 0))],
        out_shape=[jax.ShapeDtypeStruct((t, D_INNER), F32), jax.ShapeDtypeStruct((nc, N_GROUPS, D_STATE, gw), F32)],
        scratch_shapes=[pltpu.VMEM((D_STATE, gw), F32)],
        compiler_params=_cparams(("parallel", "arbitrary")),
    )(xa, xa, xa, *_ssd_small(dt_raw, dt_bias, a_log))
    return y, prev


def _ssd_bwd(dy, dxs_skip, xa, prev, dt_raw, dt_bias, a_log):
    t = xa.shape[0]
    nc = t // CHUNK
    gw = GROUP_W
    hpg = HEADS_PER_GROUP

    def body(xs_ref, b_ref, c_ref, dtg_ref, dtt_ref, br_ref, bc_ref, ar_ref, ac_ref, dy_ref, sk_ref, prev_ref,
             dxs_ref, db_ref, dc_ref, ddt_ref, dbias_ref, dalog_ref, dst_ref):
        cc = pl.program_id(1)

        @pl.when(cc == 0)
        def _():
            dst_ref[...] = jnp.zeros_like(dst_ref)
            dbias_ref[...] = jnp.zeros_like(dbias_ref)
            dalog_ref[...] = jnp.zeros_like(dalog_ref)

        q = _ssd_common(nc - 1 - cc, dtg_ref, dtt_ref, br_ref, bc_ref, ar_ref, ac_ref)
        hm, ds, dec, dt_exp = q["hm"], q["ds"], q["dec"], q["dt_exp"]
        xs = xs_ref[...]
        xdt = xs * dt_exp
        xdtb = xdt.astype(BF16)
        bm = b_ref[...].astype(BF16)
        cm = c_ref[...].astype(BF16)
        cb = _dot(cm, bm, _NT)
        bc = _dot(bm, cm, _NT)
        dyv = dy_ref[...]
        dye = (dyv * q["ea"]).astype(BF16)
        pst = prev_ref[0, 0]
        dst = dst_ref[...]
        dstb = dst.astype(BF16)
        dx_state = ds * _dot(bm, dstb)
        dxdt = dx_state
        ri = lax.broadcasted_iota(jnp.int32, (CHUNK, CHUNK), 0)
        ci = lax.broadcasted_iota(jnp.int32, (CHUNK, CHUNK), 1)
        dcb = jnp.zeros((CHUNK, CHUNK), F32)
        dcbt = jnp.zeros((CHUNK, CHUNK), F32)
        onehot = [(lax.broadcasted_iota(jnp.int32, (1, hpg), 1) == r).astype(F32) for r in range(hpg)]
        qa = jnp.zeros((CHUNK, hpg), F32)
        for r in range(hpg):
            a_col = q["acs"][:, r * HEAD_DIM:r * HEAD_DIM + 1]
            a_row = q["acs_r"][r:r + 1, :]
            lm = jnp.exp(jnp.where(ri >= ci, a_col - a_row, -jnp.inf))
            lt = jnp.exp(jnp.where(ri <= ci, a_row - a_col, -jnp.inf))
            dyr = jnp.where(hm[r], dyv, 0.0).astype(BF16)
            gl = _dot(dyr, xdtb, _NT) * lm
            glt = _dot(xdtb, dyr, _NT) * lt
            dcb = dcb + gl
            dcbt = dcbt + glt
            dxdt = dxdt + _dot((bc * lt).astype(BF16), dyr)
            qa = qa + (jnp.sum(gl * cb, axis=1, keepdims=True) - jnp.sum(glt * bc, axis=1, keepdims=True)) * onehot[r]
        pstb = pst.astype(BF16)
        dc_ref[...] = _dot(dcb.astype(BF16), bm) + _dot(dye, pstb, _NT)
        db_ref[...] = _dot(dcbt.astype(BF16), cm) + _dot((xdt * ds).astype(BF16), dstb, _NT)
        dst_ref[...] = dec * dst + _dot(cm, dye, _TN)
        dxs_ref[...] = dxdt * dt_exp + sk_ref[...]

        t2 = xdt * dx_state
        t1 = dyv * (_dot(cm, pstb) * q["ea"]) - t2
        t3 = dst * pst
        t4 = dxdt * xs
        xd = jnp.zeros((CHUNK, hpg), F32)
        dal = jnp.zeros((1, hpg), F32)
        for r in range(hpg):
            qa = qa + jnp.sum(jnp.where(hm[r], t1, 0.0), axis=1, keepdims=True) * onehot[r]
            xd = xd + jnp.sum(jnp.where(hm[r], t4, 0.0), axis=1, keepdims=True) * onehot[r]
            s2 = jnp.sum(jnp.sum(jnp.where(hm[r], t2, 0.0), axis=1, keepdims=True), axis=0, keepdims=True)
            s3 = jnp.sum(jnp.sum(jnp.where(hm[r], t3, 0.0), axis=1, keepdims=True), axis=0, keepdims=True)
            dal = dal + (s2 + dec[:, r * HEAD_DIM:r * HEAD_DIM + 1] * s3) * onehot[r]
        rc = _cumsum(qa, 0, reverse=True) + dal
        ddt = rc * q["a_r"] + xd
        ddt_raw = jnp.where(q["rowid"] >= N_PAD, ddt * _sigmoid(q["raw_c"]), 0.0)
        ddt_ref[0] = ddt_raw
        dbias_ref[0] += jnp.sum(ddt_raw, axis=0, keepdims=True)
        dalog_ref[0] += jnp.sum(rc * q["dtc"], axis=0, keepdims=True) * q["a_r"]

    rev = lambda c: nc - 1 - c
    blk = pl.BlockSpec((CHUNK, gw), lambda g, c: (rev(c), g))
    small = pl.BlockSpec((1, 1, hpg), lambda g, c: (g, 0, 0))
    dxs, db, dc, ddt, dbias, dalog = pl.pallas_call(
        body, name="ssd_bwd", grid=(N_GROUPS, nc),
        in_specs=_ssd_specs(nc, rev) + [blk, blk, pl.BlockSpec((1, 1, D_STATE, gw), lambda g, c: (rev(c), g, 0, 0))],
        out_specs=[blk, pl.BlockSpec((CHUNK, D_STATE), lambda g, c: (rev(c), g)),
                   pl.BlockSpec((CHUNK, D_STATE), lambda g, c: (rev(c), g)),
                   pl.BlockSpec((1, CHUNK, hpg), lambda g, c: (g, rev(c), 0)), small, small],
        out_shape=[jax.ShapeDtypeStruct((t, D_INNER), F32), jax.ShapeDtypeStruct((t, N_GROUPS * D_STATE), F32),
                   jax.ShapeDtypeStruct((t, N_GROUPS * D_STATE), F32), jax.ShapeDtypeStruct((N_GROUPS, t, hpg), F32),
                   jax.ShapeDtypeStruct((N_GROUPS, 1, hpg), F32), jax.ShapeDtypeStruct((N_GROUPS, 1, hpg), F32)],
        scratch_shapes=[pltpu.VMEM((D_STATE, gw), F32)],
        compiler_params=_cparams(("parallel", "arbitrary")),
    )(xa, xa, xa, *_ssd_small(dt_raw, dt_bias, a_log), dy, dxs_skip, prev)
    ddt_raw = ddt.transpose(1, 0, 2).reshape(t, N_HEADS)
    return dxs, db, dc, ddt_raw, dbias.reshape(N_HEADS), dalog.reshape(N_HEADS)


def _ssd_post_fwd(y, xa, proj, dsk, nw):
    t = y.shape[0]
    di = D_INNER
    tm = _tile(t, (384, 128))

    def body(y_ref, xs_ref, z_ref, dsk_ref, nw_ref, o_ref):
        z = z_ref[...]
        yz = (y_ref[...] + xs_ref[...] * dsk_ref[...]) * (z * _sigmoid(z))
        nwv = nw_ref[...]
        for g in range(N_GROUPS):
            sl = slice(g * GROUP_W, (g + 1) * GROUP_W)
            v = yz[:, sl]
            rg = lax.rsqrt(jnp.mean(v * v, axis=-1, keepdims=True) + EPS)
            o_ref[:, sl] = (v * rg * nwv[:, sl]).astype(BF16)

    row = pl.BlockSpec((tm, di), lambda i: (i, 0))
    vec = pl.BlockSpec((1, di), lambda i: (0, 0))
    return pl.pallas_call(
        body, name="ssd_post_fwd", grid=(t // tm,),
        in_specs=[row, row, row, vec, vec], out_specs=row, out_shape=jax.ShapeDtypeStruct((t, di), BF16),
        compiler_params=_cparams(("parallel",)),
    )(y, xa, proj, dsk.reshape(1, di), nw.reshape(1, di))


def _ssd_post_bwd(dyn, y, xa, proj, dsk, nw):
    t = y.shape[0]
    di = D_INNER
    tm = CHUNK

    def body(dyn_ref, y_ref, xs_ref, z_ref, dsk_ref, nw_ref, dy_ref, dsk_x_ref, dz_ref, dnw_ref, ddsk_ref):
        @pl.when(pl.program_id(0) == 0)
        def _():
            dnw_ref[...] = jnp.zeros_like(dnw_ref)
            ddsk_ref[...] = jnp.zeros_like(ddsk_ref)

        for g in range(N_GROUPS):
            sl = slice(g * GROUP_W, (g + 1) * GROUP_W)
            z = z_ref[:, sl]
            sig = _sigmoid(z)
            s = z * sig
            xs = xs_ref[:, sl]
            dskv = dsk_ref[:, sl]
            yt = y_ref[:, sl] + xs * dskv
            yz = yt * s
            rg = lax.rsqrt(jnp.mean(yz * yz, axis=-1, keepdims=True) + EPS)
            xhat = yz * rg
            dynv = dyn_ref[:, sl]
            gg = dynv * nw_ref[:, sl]
            dnw_ref[:, sl] += jnp.sum(dynv * xhat, axis=0, keepdims=True)
            dyz = rg * (gg - xhat * jnp.mean(gg * xhat, axis=-1, keepdims=True))
            dyt = dyz * s
            dy_ref[:, sl] = dyt
            dsk_x_ref[:, sl] = dyt * dskv
            dz_ref[:, sl] = (dyz * yt * (sig * (1.0 + z * (1.0 - sig)))).astype(BF16)
            ddsk_ref[:, sl] += jnp.sum(dyt * xs, axis=0, keepdims=True)

    row = pl.BlockSpec((tm, di), lambda i: (i, 0))
    vec = pl.BlockSpec((1, di), lambda i: (0, 0))
    dy, dxsk, dz, dnw, ddsk = pl.pallas_call(
        body, name="ssd_post_bwd", grid=(t // tm,),
        in_specs=[row, row, row, row, vec, vec], out_specs=[row, row, row, vec, vec],
        out_shape=[jax.ShapeDtypeStruct((t, di), F32), jax.ShapeDtypeStruct((t, di), F32), jax.ShapeDtypeStruct((t, di), BF16),
                   jax.ShapeDtypeStruct((1, di), F32), jax.ShapeDtypeStruct((1, di), F32)],
        compiler_params=_cparams(("arbitrary",)),
    )(dyn, y, xa, proj, dsk.reshape(1, di), nw.reshape(1, di))
    return dy, dxsk, dz, dnw.reshape(di), ddsk.reshape(N_HEADS, HEAD_DIM).sum(axis=1)


SHARD_COLS = IN_COLS // N_CHIPS


def _split_w_in(w_in_sh):
    nl = w_in_sh.shape[0]
    w = w_in_sh.transpose(0, 2, 1, 3).reshape(nl, D_MODEL, IN_COLS)
    main = jnp.concatenate([w[..., OFF_Z:OFF_XBC], w[..., :OFF_Z], w[..., OFF_GATE:], w[..., OFF_XBC:OFF_DT]], axis=-1)
    dt = jnp.pad(w[..., OFF_DT:OFF_GATE], ((0, 0), (0, 0), (0, DT_PAD - N_HEADS)))
    return main, dt


def _dproj_by_shard(dup, dz, dxbc, ddt_raw, dgate):
    t = dup.shape[0]
    full = jnp.concatenate([dup, dz, *dxbc, ddt_raw.astype(BF16), dgate], axis=1)
    return full.reshape(t, N_CHIPS, SHARD_COLS).transpose(1, 0, 2)


def _relu2_epilogue(acc):
    hid = jnp.maximum(acc, 0.0)
    return acc, hid * hid


def _relu2_bwd_epilogue(acc, pre):
    return (acc * 2.0 * jnp.maximum(pre, 0.0),)


def _add_epilogue(acc, res):
    return (acc + res,)


def _w_spec(rows, cols, fn):
    return pl.BlockSpec((None, None, rows, cols), fn)


def _layer_fwd(h, w, big, layer):
    t = h.shape[0]
    d = D_MODEL
    tm = _tile(t, (1408, 384, 128))
    nt = t // tm
    row = _bs((tm, d), lambda i, j, k: (i, 0))
    s = {"h": h}
    u = _norm_fwd(h, w["mix_norm_w"], "norm_mix")
    proj = _mm(f"mm_proj_{layer}", _NN, (nt, PCOLS // d, 1), u, row,
               big["w_in_main"], _bs((None, d, d), lambda i, j, k: (layer, 0, j)),
               [((t, PCOLS), F32, _bs((tm, d), lambda i, j, k: (i, j)))])
    dt_raw = _mm(f"mm_dt_{layer}", _NN, (nt, 1, 1), u, row,
                 big["w_in_dt"], _bs((None, d, DT_PAD), lambda i, j, k: (layer, 0, 0)),
                 [((t, DT_PAD), F32, _bs((tm, DT_PAD), lambda i, j, k: (i, 0)))])
    pooled, yg, ypm, y_pool = _pool_fwd(proj, big["pool_w_group"], w["pool_scale"], big["w_pool_up"], layer)
    xa = _conv_fwd(proj, w["conv_w"], w["conv_b"])
    y, prev = _ssd_fwd(xa, dt_raw, w["dt_bias"], w["a_log"])
    yn = _ssd_post_fwd(y, xa, proj, w["d_skip_exp"], w["ssd_norm_w"])
    rs = D_INNER // N_CHIPS
    y_ssd = _mm(f"mm_ssd_out_{layer}", _NN, (nt, 1, N_CHIPS), yn, _bs((tm, rs), lambda i, j, k: (i, k)),
                big["w_ssd_out"], _w_spec(rs, d, lambda i, j, k: (layer, k, 0, 0)), [((t, d), F32, row)])
    merged = _gate_fwd(proj, w["b_gate"], y_pool, y_ssd)
    ro = d // N_CHIPS
    h1 = _mm(f"mm_o_{layer}", _NN, (nt, 1, N_CHIPS), merged, _bs((tm, ro), lambda i, j, k: (i, k)),
             big["w_o"], _w_spec(ro, d, lambda i, j, k: (layer, k, 0, 0)), [((t, d), F32, row)],
             extras=[(h, row)], epilogue=_add_epilogue)
    v = _norm_fwd(h1, w["mlp_norm_w"], "norm_mlp")
    tile = _bs((tm, d), lambda i, j, k: (i, j))
    pre, act = _mm(f"mm_ff1_{layer}", _NN, (nt, N_CHIPS, 1), v, row,
                   big["w_ff1"], _w_spec(d, d, lambda i, j, k: (layer, j, 0, 0)),
                   [((t, D_FF), F32, tile), ((t, D_FF), BF16, tile)], epilogue=_relu2_epilogue)
    h2 = _mm(f"mm_ff2_{layer}", _NN, (nt, 1, N_CHIPS), act, _bs((tm, d), lambda i, j, k: (i, k)),
             big["w_ff2"], _w_spec(d, d, lambda i, j, k: (layer, k, 0, 0)), [((t, d), F32, row)],
             extras=[(h1, row)], epilogue=_add_epilogue)
    s.update(u=u, proj=proj, dt_raw=dt_raw, pooled=pooled, yg=yg, ypm=ypm, y_pool=y_pool, xa=xa, y=y, prev=prev, yn=yn,
             y_ssd=y_ssd, merged=merged, h1=h1, v=v, pre=pre, act=act)
    return h2, s


def _dw(name, layer, a, b, shard_shape, by, gbuf, tk, b_blk=None):
    t = a.shape[0]
    nk = t // tk
    rows, cols = shard_shape
    if by == "rows":
        grid = (N_CHIPS, 1, nk)
        a_spec = _bs((tk, rows), lambda i, j, k: (k, i))
        b_spec = _bs((tk, cols), lambda i, j, k: (k, 0))
        o_spec = _w_spec(rows, cols, lambda i, j, k: (layer, i, 0, 0))
    else:
        grid = (1, N_CHIPS, nk)
        a_spec = _bs((tk, rows), lambda i, j, k: (k, 0))
        b_spec = b_blk if b_blk is not None else _bs((tk, cols), lambda i, j, k: (k, j))
        o_spec = _w_spec(rows, cols, lambda i, j, k: (layer, j, 0, 0))
    return _mm(f"{name}_{layer}", _TN, grid, a, a_spec, b, b_spec,
               [((DEPTH, N_CHIPS, rows, cols), F32, o_spec)], into=gbuf)


def _layer_bwd(dh, s, w, big, layer, gb):
    t = dh.shape[0]
    d = D_MODEL
    tm = _tile(t, (1408, 384, 128))
    tk = _tile(t, (1408, 384, 128))
    nt = t // tm
    row = _bs((tm, d), lambda i, j, k: (i, 0))
    tile = _bs((tm, d), lambda i, j, k: (i, j))
    g = {}
    gb = dict(gb)
    dpre = _mm(f"mm_dact_{layer}", _NT, (nt, N_CHIPS, 1), dh, row,
               big["w_ff2"], _w_spec(d, d, lambda i, j, k: (layer, j, 0, 0)), [((t, D_FF), BF16, tile)],
               extras=[(s["pre"], tile)], epilogue=_relu2_bwd_epilogue)
    gb["w_ff2"] = _dw("mm_dw_ff2", layer, s["act"],dh,(d, d), "rows", gb["w_ff2"], tk)
    dv = _mm(f"mm_dv_{layer}", _NT, (nt, 1, N_CHIPS), dpre, _bs((tm, d), lambda i, j, k: (i, k)),
             big["w_ff1"], _w_spec(d, d, lambda i, j, k: (layer, k, 0, 0)), [((t, d), F32, row)])
    gb["w_ff1"] = _dw("mm_dw_ff1", layer, s["v"],dpre,(d, d), "cols", gb["w_ff1"], tk)
    dh1, g["mlp_norm_w"] = _norm_bwd(dh, dv, s["h1"], w["mlp_norm_w"], "norm_mlp_bwd")
    ro = d // N_CHIPS
    dmerged = _mm(f"mm_dmerged_{layer}", _NT, (nt, N_CHIPS, 1), dh1, row,
                  big["w_o"], _w_spec(ro, d, lambda i, j, k: (layer, j, 0, 0)),
                  [((t, d), F32, _bs((tm, ro), lambda i, j, k: (i, j)))])
    gb["w_o"] = _dw("mm_dw_o", layer, s["merged"],dh1,(ro, d), "rows", gb["w_o"], tk)
    dy_pool, dy_ssd, dgate, g["b_gate"] = _gate_bwd(dmerged, s["proj"], w["b_gate"], s["y_pool"], s["y_ssd"])
    rs = D_INNER // N_CHIPS
    dyn = _mm(f"mm_dyn_{layer}", _NT, (nt, N_CHIPS, 1), dy_ssd, row,
              big["w_ssd_out"], _w_spec(rs, d, lambda i, j, k: (layer, j, 0, 0)),
              [((t, D_INNER), F32, _bs((tm, rs), lambda i, j, k: (i, j)))])
    gb["w_ssd_out"] = _dw("mm_dw_ssd_out", layer, s["yn"],dy_ssd,(rs, d), "rows", gb["w_ssd_out"], tk)
    dy, dxs_skip, dz, g["ssd_norm_w"], g["d_skip"] = _ssd_post_bwd(dyn, s["y"], s["xa"], s["proj"], w["d_skip_exp"], w["ssd_norm_w"])
    dxs, db, dc, ddt_raw, g["dt_bias"], g["a_log"] = _ssd_bwd(dy, dxs_skip, s["xa"], s["prev"], s["dt_raw"],
                                                              w["dt_bias"], w["a_log"])
    dx1, dcw1, dcb1 = _conv_bwd(dxs, 0, s["proj"], w["conv_w"], w["conv_b"])
    dx2, dcw2, dcb2 = _conv_bwd(db, 2, s["proj"], w["conv_w"], w["conv_b"])
    dx3, dcw3, dcb3 = _conv_bwd(dc, 3, s["proj"], w["conv_w"], w["conv_b"])
    g["conv_w"] = jnp.concatenate([dcw1, dcw2, dcw3], axis=1)
    g["conv_b"] = jnp.concatenate([dcb1, dcb2, dcb3])
    q, dyg, g["pool_scale"] = _pool_bwd_a(dy_pool, s["yg"], w["pool_scale"], big["pool_w_group"], big["w_pool_up"], layer)
    gb["w_pool_up"] = _dw("mm_dw_pool_up", layer, s["ypm"],dy_pool,(ro, d), "rows", gb["w_pool_up"], tk)
    gb["pool_w_group"] = _pool_dwg(s["pooled"], dyg, layer, gb["pool_w_group"])
    dup = _pool_bwd_b(q)
    dproj = _dproj_by_shard(dup, dz, (dx1, dx2, dx3), ddt_raw, dgate)
    du = _mm(f"mm_du_{layer}", _NT, (nt, 1, N_CHIPS), dproj, _bs((None, tm, SHARD_COLS), lambda i, j, k: (k, i, 0)),
             big["w_in"], _w_spec(d, SHARD_COLS, lambda i, j, k: (layer, k, 0, 0)), [((t, d), F32, row)])
    tkw = _tile(t, (704, 384, 128))
    gb["w_in"] = _dw("mm_dw_in", layer, s["u"], dproj, (d, SHARD_COLS), "cols", gb["w_in"], tkw,
                     b_blk=_bs((None, tkw, SHARD_COLS), lambda i, j, k: (j, k, 0)))
    dh0, g["mix_norm_w"] = _norm_bwd(dh1, du, s["h"], w["mix_norm_w"], "norm_mix_bwd")
    return dh0, g, gb


BIG = ("w_in", "pool_w_group", "w_pool_up", "w_ssd_out", "w_o", "w_ff1", "w_ff2")
PER_LAYER = ("mix_norm_w", "w_in", "b_gate", "pool_w_group", "pool_scale", "w_pool_up", "conv_w", "conv_b", "dt_bias",
             "a_log", "d_skip", "ssd_norm_w", "w_ssd_out", "w_o", "mlp_norm_w", "w_ff1", "w_ff2")


SMALL_PER_LAYER = tuple(n for n in PER_LAYER if n not in BIG)


def _local_step(x, tgt, big, small):
    seq = x.shape[0]
    big = dict(big)
    big["w_in_main"], big["w_in_dt"] = _split_w_in(big["w_in"])
    h = jnp.concatenate([jnp.zeros((N_PAD, D_MODEL), F32), small["meta_tokens"], x], axis=0)
    saved, ws = [], []
    for i in range(DEPTH):
        w = {n: small[n][i] for n in SMALL_PER_LAYER}
        w["d_skip_exp"] = jnp.repeat(w["d_skip"], HEAD_DIM)
        h, s = _layer_fwd(h, w, big, i)
        saved.append(s)
        ws.append(w)
    loss, dh, g_final = _loss_head(h, small["final_norm_w"], tgt)
    layer_g = [None] * DEPTH
    gb = {n: None for n in BIG}
    for i in reversed(range(DEPTH)):
        dh, layer_g[i], gb = _layer_bwd(dh, saved[i], ws[i], big, i, gb)
    grads = {n: jnp.stack([layer_g[i][n] for i in range(DEPTH)]) for n in SMALL_PER_LAYER}
    grads["final_norm_w"] = g_final
    grads["meta_tokens"] = dh[N_PAD:ROW_X]
    return loss, dh[ROW_X:ROW_X + seq], gb, grads


MESH = pl.DeviceIdType.MESH
LANES = 128
ANY = pl.BlockSpec(memory_space=pl.ANY)


def _place():
    x, y, c = lax.axis_index("x"), lax.axis_index("y"), lax.axis_index("c")
    chips = [(1 - x, y), (x, 1 - y), (1 - x, 1 - y)]
    return x, y, c, chips


def _remote(src, dst, send_sem, recv_sem, to):
    return pltpu.make_async_remote_copy(src_ref=src, dst_ref=dst, send_sem=send_sem, recv_sem=recv_sem,
                                        device_id=to, device_id_type=MESH)


HALF = DEPTH // 2


def _gather_big(shards):
    n = len(shards)
    per = 7

    def body(*refs):
        src, out = refs[:n], refs[n:2 * n]
        send_sems, recv_sems = refs[2 * n:]
        x, y, c, chips = _place()
        sibling = (x, y, 1 - c)
        me = 2 * x + y
        lo = HALF * c
        sends = []

        def go(src_ref, dst_ref, k, to):
            cp = _remote(src_ref, dst_ref, send_sems.at[k], recv_sems.at[k], to)
            cp.start()
            sends.append(cp)

        def landed(ref, k):
            _remote(ref, ref, send_sems.at[k], recv_sems.at[k], sibling).wait_recv()

        for p in range(n):
            go(src[p], out[p].at[:, me], per * p, sibling)
            for j, chip in enumerate(chips):
                go(src[p].at[pl.ds(lo, HALF)], out[p].at[pl.ds(lo, HALF), me], per * p + 1 + j, (*chip, c))
        for j, (cx, cy) in enumerate(chips):
            for p in range(n):
                blk = out[p].at[pl.ds(lo, HALF), 2 * cx + cy]
                landed(blk, per * p + 1 + j)
                go(blk, blk, per * p + 4 + j, sibling)
        for p in range(n):
            landed(out[p].at[:, me], per * p)
        for j, (cx, cy) in enumerate(chips):
            for p in range(n):
                landed(out[p].at[pl.ds(HALF * (1 - c), HALF), 2 * cx + cy], per * p + 4 + j)
        for cp in sends:
            cp.wait_send()

    return pl.pallas_call(
        body, name="gather_big", in_specs=[ANY] * n, out_specs=[ANY] * n,
        out_shape=[jax.ShapeDtypeStruct((a.shape[0], N_CHIPS) + a.shape[1:], a.dtype) for a in shards],
        scratch_shapes=[pltpu.SemaphoreType.DMA((per * n,)), pltpu.SemaphoreType.DMA((per * n,))],
    )(*shards)


def _exchange_small(v, reduce, name):
    rows_per = v.shape[0]
    vm = pl.BlockSpec(memory_space=pltpu.VMEM)

    def body(v_ref, out_ref, *scratch):
        if reduce:
            land_ref, send_sems, recv_sems, local_sem = scratch
        else:
            land_ref = out_ref
            send_sems, recv_sems, local_sem = scratch
        x, y, c, chips = _place()
        me, sibling = (x, y, c), (x, y, 1 - c)

        def rows(px, py, pc):
            return land_ref.at[4 * px + 2 * py + pc]

        def copy(k, block, to, src=None):
            return _remote(rows(*block) if src is None else src, rows(*block), send_sems.at[k], recv_sems.at[k], to)

        mine = pltpu.make_async_copy(v_ref, rows(*me), local_sem)
        mine.start()
        first = [copy(0, me, sibling, src=v_ref)]
        first += [copy(1 + j, me, (*chip, c), src=v_ref) for j, chip in enumerate(chips)]
        for cp in first:
            cp.start()
        passed = [copy(4 + j, (*chip, c), sibling) for j, chip in enumerate(chips)]
        for j, chip in enumerate(chips):
            copy(1 + j, (*chip, c), me).wait_recv()
            passed[j].start()
        copy(0, sibling, me).wait_recv()
        for j, chip in enumerate(chips):
            copy(4 + j, (*chip, 1 - c), me).wait_recv()
        for cp in first + passed:
            cp.wait_send()
        mine.wait()
        if reduce:
            acc = land_ref[0]
            for d in range(1, 8):
                acc = acc + land_ref[d]
            out_ref[...] = acc

    sems = [pltpu.SemaphoreType.DMA((7,)), pltpu.SemaphoreType.DMA((7,)), pltpu.SemaphoreType.DMA]
    if reduce:
        out_shape = jax.ShapeDtypeStruct((rows_per, LANES), F32)
        scratch = [pltpu.VMEM((8, rows_per, LANES), F32)] + sems
    else:
        out_shape = jax.ShapeDtypeStruct((8, rows_per, LANES), F32)
        scratch = sems
    return pl.pallas_call(
        body, name=name, in_specs=[vm], out_specs=vm, out_shape=out_shape, scratch_shapes=scratch,
        compiler_params=pltpu.CompilerParams(vmem_limit_bytes=VMEM_LIMIT),
    )(v)


def _swap_with_sibling(arrs, pick, name):
    n = len(arrs)
    shapes = [jax.eval_shape(lambda a: pick(a, 0), a) for a in arrs]

    def body(*refs):
        src, out = refs[:n], refs[n:2 * n]
        send_sems, recv_sems = refs[2 * n:]
        x, y, c, _ = _place()
        cps = [_remote(pick(src[p], c), out[p], send_sems.at[p], recv_sems.at[p], (x, y, 1 - c)) for p in range(n)]
        for cp in cps:
            cp.start()
        for cp in cps:
            cp.wait()

    return pl.pallas_call(
        body, name=name, in_specs=[ANY] * n, out_specs=[ANY] * n,
        out_shape=[jax.ShapeDtypeStruct(s.shape, s.dtype) for s in shapes],
        scratch_shapes=[pltpu.SemaphoreType.DMA((n,)), pltpu.SemaphoreType.DMA((n,))],
    )(*arrs)


def _layer_half(ref, hf):
    if isinstance(hf, int):
        return ref[HALF * hf:HALF * (hf + 1)]
    return ref.at[pl.ds(HALF * hf, HALF)]


def _row_tile(rows, last, itemsize=4, budget=2 * 1024 * 1024):
    return _tile(rows, tuple(t for t in (2048, 1024, 512, 256, 128, 64, 32, 16) if t * last * itemsize <= budget))


def _rs_add(g, got, c, name):
    last = g.shape[-1]
    g2, got2 = g.reshape(-1, last), got.reshape(-1, last)
    half = got2.shape[0]
    tr = _row_tile(half, last)
    nb = half // tr

    def body(c_ref, g_ref, got_ref, o_ref):
        o_ref[...] = (g_ref[...] + got_ref[...]).astype(BF16)

    out = pl.pallas_call(
        body, name=name,
        grid_spec=pltpu.PrefetchScalarGridSpec(
            num_scalar_prefetch=1, grid=(nb,),
            in_specs=[pl.BlockSpec((tr, last), lambda i, c_ref: (c_ref[0] * nb + i, 0)),
                      pl.BlockSpec((tr, last), lambda i, c_ref: (i, 0))],
            out_specs=pl.BlockSpec((tr, last), lambda i, c_ref: (i, 0))),
        out_shape=jax.ShapeDtypeStruct((half, last), BF16),
        compiler_params=_cparams(("parallel",)),
    )(c.reshape(1).astype(jnp.int32), g2, got2)
    return out.reshape(got.shape)


def _rs_ici(ps):
    n = len(ps)

    def body(*refs):
        src, out = refs[:n], refs[n:2 * n]
        send_sems, recv_sems = refs[2 * n:]
        x, y, c, chips = _place()
        me = 2 * x + y
        sends = []
        for p in range(n):
            for j, (cx, cy) in enumerate(chips):
                cp = _remote(src[p].at[:, 2 * cx + cy], out[p].at[j], send_sems.at[3 * p + j], recv_sems.at[3 * p + j],
                             (cx, cy, c))
                cp.start()
                sends.append(cp)
        for j, (cx, cy) in enumerate(chips):
            for p in range(n):
                slot = out[p].at[j]
                _remote(slot, slot, send_sems.at[3 * p + j], recv_sems.at[3 * p + j], (cx, cy, c)).wait_recv()
        for cp in sends:
            cp.wait_send()

    return pl.pallas_call(
        body, name="rs_ici", in_specs=[ANY] * n, out_specs=[ANY] * n,
        out_shape=[jax.ShapeDtypeStruct((3, a.shape[0]) + a.shape[2:], a.dtype) for a in ps],
        scratch_shapes=[pltpu.SemaphoreType.DMA((3 * n,)), pltpu.SemaphoreType.DMA((3 * n,))],
    )(*ps)


def _rs_sum(own, got, chip, name):
    nl, last = own.shape[0], own.shape[-1]
    shard = own.shape[2:]
    own4 = own.reshape(nl, N_CHIPS, -1, last)
    rows = own4.shape[2]
    got4 = got.reshape(3, nl, rows, last)
    tr = _row_tile(rows, last, budget=1024 * 1024)

    def body(k_ref, own_ref, got_ref, o_ref):
        acc = own_ref[...].astype(F32)
        for j in range(3):
            acc = acc + got_ref[j].astype(F32)
        o_ref[...] = acc

    out = pl.pallas_call(
        body, name=name,
        grid_spec=pltpu.PrefetchScalarGridSpec(
            num_scalar_prefetch=1, grid=(nl, rows // tr),
            in_specs=[pl.BlockSpec((None, None, tr, last), lambda l, i, k_ref: (l, k_ref[0], i, 0)),
                      pl.BlockSpec((3, None, tr, last), lambda l, i, k_ref: (0, l, i, 0))],
            out_specs=pl.BlockSpec((None, tr, last), lambda l, i, k_ref: (l, i, 0))),
        out_shape=jax.ShapeDtypeStruct((nl, rows, last), F32),
        compiler_params=_cparams(("parallel", "parallel")),
    )(chip.reshape(1).astype(jnp.int32), own4, got4)
    return out.reshape((nl,) + shard)


def _reduce_scatter(gs, chip, core):
    names = list(gs)
    got = _swap_with_sibling([gs[n] for n in names], lambda ref, c: _layer_half(ref, 1 - c), "rs_swap")
    chip_sums = [_rs_add(gs[n], a, core, "rs_add_" + n) for n, a in zip(names, got)]
    landed = _rs_ici(chip_sums)
    mine = [_rs_sum(p, a, chip, "rs_sum_" + n) for n, p, a in zip(names, chip_sums, landed)]
    other = _swap_with_sibling(mine, lambda ref, c: ref, "rs_share")
    out = {}
    for n, a, b in zip(names, mine, other):
        out[n] = jnp.where(core == 0, jnp.concatenate([a, b]), jnp.concatenate([b, a]))
    return out


def _adamw(w, g, m, v, name):
    shape = w.shape
    last = shape[-1]
    w2, g2, m2, v2 = (a.reshape(-1, last) for a in (w, g, m, v))
    rows = w2.shape[0]
    tr = _tile(rows, tuple(t for t in (1024, 512, 256, 128, 64, 32, 16, 8) if t * last * 4 <= 2 * 1024 * 1024) + (rows,))

    def body(w_ref, g_ref, m_ref, v_ref, d_ref, nm_ref, nv_ref):
        gv = g_ref[...]
        mn = ADAM_B1 * m_ref[...] + (1.0 - ADAM_B1) * gv
        vn = ADAM_B2 * v_ref[...] + (1.0 - ADAM_B2) * (gv * gv)
        m_hat = mn / (1.0 - ADAM_B1 ** ADAM_STEP)
        v_hat = vn / (1.0 - ADAM_B2 ** ADAM_STEP)
        d_ref[...] = -ADAM_LR * (m_hat / (jnp.sqrt(v_hat) + ADAM_EPS) + ADAM_WD * w_ref[...])
        nm_ref[...] = mn
        nv_ref[...] = vn

    blk = pl.BlockSpec((tr, last), lambda i: (i, 0))
    outs = pl.pallas_call(
        body, name=name, grid=(rows // tr,), in_specs=[blk] * 4, out_specs=[blk] * 3,
        out_shape=[jax.ShapeDtypeStruct((rows, last), F32)] * 3,
        compiler_params=_cparams(("parallel",)),
    )(w2, g2, m2, v2)
    return tuple(o.reshape(shape) for o in outs)


SHARD_AXIS = {"conv_w": 2, "meta_tokens": 1}
SMALL_SHARDED = ("conv_w", "meta_tokens")
REPLICATED = ("mix_norm_w", "b_gate", "pool_scale", "conv_b", "dt_bias", "a_log", "d_skip", "ssd_norm_w", "mlp_norm_w",
              "final_norm_w")


def _flatten(arrs, dtype, row_mult):
    flat = jnp.concatenate([a.astype(dtype).reshape(-1) for a in arrs])
    n = flat.shape[0]
    rows = -(-n // (LANES * row_mult)) * row_mult
    return jnp.pad(flat, (0, rows * LANES - n)).reshape(rows, LANES)


def _unflatten(flat2d, shapes):
    flat = flat2d.reshape(-1)
    out, off = [], 0
    for sh in shapes:
        n = 1
        for d in sh:
            n *= d
        out.append(flat[off:off + n].reshape(sh))
        off += n
    return out


def kernel(x, meta_tokens, mix_norm_w, w_in, b_gate, pool_w_group, pool_scale, w_pool_up, conv_w, conv_b, dt_bias, a_log, d_skip, ssd_norm_w, w_ssd_out, w_o, mlp_norm_w, w_ff1, w_ff2, final_norm_w, loss_target, m_meta_tokens, m_mix_norm_w, m_w_in, m_b_gate, m_pool_w_group, m_pool_scale, m_w_pool_up, m_conv_w, m_conv_b, m_dt_bias, m_a_log, m_d_skip, m_ssd_norm_w, m_w_ssd_out, m_w_o, m_mlp_norm_w, m_w_ff1, m_w_ff2, m_final_norm_w, v_meta_tokens, v_mix_norm_w, v_w_in, v_b_gate, v_pool_w_group, v_pool_scale, v_w_pool_up, v_conv_w, v_conv_b, v_dt_bias, v_a_log, v_d_skip, v_ssd_norm_w, v_w_ssd_out, v_w_o, v_mlp_norm_w, v_w_ff1, v_w_ff2, v_final_norm_w):
    names = ("meta_tokens",) + PER_LAYER + ("final_norm_w",)
    par = dict(meta_tokens=meta_tokens, mix_norm_w=mix_norm_w, w_in=w_in, b_gate=b_gate, pool_w_group=pool_w_group,
               pool_scale=pool_scale, w_pool_up=w_pool_up, conv_w=conv_w, conv_b=conv_b, dt_bias=dt_bias, a_log=a_log,
               d_skip=d_skip, ssd_norm_w=ssd_norm_w, w_ssd_out=w_ssd_out, w_o=w_o, mlp_norm_w=mlp_norm_w, w_ff1=w_ff1,
               w_ff2=w_ff2, final_norm_w=final_norm_w)
    mom = dict(meta_tokens=m_meta_tokens, mix_norm_w=m_mix_norm_w, w_in=m_w_in, b_gate=m_b_gate, pool_w_group=m_pool_w_group,
               pool_scale=m_pool_scale, w_pool_up=m_w_pool_up, conv_w=m_conv_w, conv_b=m_conv_b, dt_bias=m_dt_bias,
               a_log=m_a_log, d_skip=m_d_skip, ssd_norm_w=m_ssd_norm_w, w_ssd_out=m_w_ssd_out, w_o=m_w_o,
               mlp_norm_w=m_mlp_norm_w, w_ff1=m_w_ff1, w_ff2=m_w_ff2, final_norm_w=m_final_norm_w)
    var = dict(meta_tokens=v_meta_tokens, mix_norm_w=v_mix_norm_w, w_in=v_w_in, b_gate=v_b_gate, pool_w_group=v_pool_w_group,
               pool_scale=v_pool_scale, w_pool_up=v_w_pool_up, conv_w=v_conv_w, conv_b=v_conv_b, dt_bias=v_dt_bias,
               a_log=v_a_log, d_skip=v_d_skip, ssd_norm_w=v_ssd_norm_w, w_ssd_out=v_w_ssd_out, w_o=v_w_o,
               mlp_norm_w=v_mlp_norm_w, w_ff1=v_w_ff1, w_ff2=v_w_ff2, final_norm_w=v_final_norm_w)
    chip = 2 * lax.axis_index("x") + lax.axis_index("y")
    core = lax.axis_index("c")

    big = dict(zip(BIG, _gather_big([par[n].astype(BF16) for n in BIG])))
    small = {n: par[n] for n in REPLICATED}
    small_shapes = [par[n].shape for n in SMALL_SHARDED]
    got_small = _exchange_small(_flatten([par[n] for n in SMALL_SHARDED], F32, 8), False, "gather_small")
    pieces = [_unflatten(got_small[2 * k], small_shapes) for k in range(N_CHIPS)]
    for j, n in enumerate(SMALL_SHARDED):
        small[n] = jnp.concatenate([pieces[k][j] for k in range(N_CHIPS)], axis=SHARD_AXIS[n])

    loss, dx, gbig, grads = _local_step(x[0], loss_target[0], big, small)
    loss = lax.psum(loss, ("x", "y", "c"))

    small_names = REPLICATED + SMALL_SHARDED
    red = _exchange_small(_flatten([grads[n] for n in small_names], F32, 8), True, "allreduce_small")
    gsum = dict(zip(small_names, _unflatten(red, [grads[n].shape for n in small_names])))
    for n in SMALL_SHARDED:
        width = par[n].shape[SHARD_AXIS[n]]
        gsum[n] = lax.dynamic_slice_in_dim(gsum[n], chip * width, width, axis=SHARD_AXIS[n])

    gsum.update(_reduce_scatter(gbig, chip, core))

    delta, new_m, new_v = {}, {}, {}
    for n in BIG:
        delta[n], new_m[n], new_v[n] = _adamw(par[n], gsum[n], mom[n], var[n], "adamw_" + n)
    rest = [n for n in names if n not in BIG]
    shapes = [par[n].shape for n in rest]
    packed = [_flatten([d[n] for n in rest], F32, 8) for d in (par, gsum, mom, var)]
    for d, flat in zip((delta, new_m, new_v), _adamw(*packed, "adamw_small")):
        d.update(zip(rest, _unflatten(flat, shapes)))

    order = ("meta_tokens", "mix_norm_w", "w_in", "b_gate", "pool_w_group", "pool_scale", "w_pool_up", "conv_w", "conv_b",
             "dt_bias", "a_log", "d_skip", "ssd_norm_w", "w_ssd_out", "w_o", "mlp_norm_w", "w_ff1", "w_ff2", "final_norm_w")
    return (loss, dx[None], *[gsum[n] for n in order], *[delta[n] for n in order], *[new_m[n] for n in order],
            *[new_v[n] for n in order])
```

```python
import functools

import jax
import jax.numpy as jnp
from jax import lax
from jax.experimental import pallas as pl
from jax.experimental.pallas import tpu as pltpu

F32 = jnp.float32
BF16 = jnp.bfloat16

D_MODEL = 1024
DEPTH = 4
N_META = 16
N_PAD = 112
ROW_X = N_PAD + N_META
POOL_WINDOWS = (2, 4, 8, 16)
POOL_GDIM = 256
D_INNER = 2048
N_HEADS = 32
HEAD_DIM = 64
N_GROUPS = 8
HEADS_PER_GROUP = 4
GROUP_W = HEADS_PER_GROUP * HEAD_DIM
D_STATE = 128
CHUNK = 128
D_XBC = 4096
D_FF = 4096
EPS = 1e-5
OFF_Z, OFF_XBC, OFF_DT, OFF_GATE, IN_COLS = 1024, 3072, 7168, 7200, 9248
PZ, PGATE, PPOOL, PXBC, PCOLS = 0, 2048, 4096, 5120, 9216
DT_PAD = 128

ADAM_LR, ADAM_B1, ADAM_B2, ADAM_EPS, ADAM_WD, ADAM_STEP = 0.001, 0.9, 0.999, 1e-08, 0.01, 10

VMEM_LIMIT = 56 * 1024 * 1024

_NN = (((1,), (0,)), ((), ()))
_NT = (((1,), (1,)), ((), ()))
_TN = (((0,), (0,)), ((), ()))


def _dot(a, b, dn=_NN):
    return lax.dot_general(a, b, dn, preferred_element_type=F32)


def _cparams(sem):
    return pltpu.CompilerParams(dimension_semantics=sem, vmem_limit_bytes=VMEM_LIMIT)


def _tile(n, cands):
    for c in cands:
        if n % c == 0:
            return c
    raise ValueError(f"no tile for {n} in {cands}")


def _sigmoid(x):
    return 1.0 / (1.0 + jnp.exp(-x))


def _softplus(x):
    return jnp.maximum(x, 0.0) + jnp.log(1.0 + jnp.exp(-jnp.abs(x)))


def _bs(shape, fn):
    return pl.BlockSpec(shape, fn)


def _mm(name, dn, grid, a, a_spec, b, b_spec, outs, *, extras=(), epilogue=None, into=None):
    nk = grid[2]
    ne, no = len(extras), len(outs)
    ni = 0 if into is None else 1
    blk = tuple(d for d in outs[0][2].block_shape if d is not None)

    def body(a_ref, b_ref, *rest):
        e_refs, o_refs = rest[:ne], rest[ne + ni:ne + ni + no]
        p = _dot(a_ref[...].astype(BF16), b_ref[...].astype(BF16), dn)

        def finish(acc):
            outs = epilogue(acc, *[e[...] for e in e_refs]) if epilogue is not None else (acc,)
            for o, v in zip(o_refs, outs):
                o[...] = v.astype(o.dtype)

        if nk == 1:
            finish(p)
        else:
            acc_ref = rest[ne + ni + no]
            kk = pl.program_id(2)

            @pl.when(kk == 0)
            def _():
                acc_ref[...] = p

            @pl.when(kk > 0)
            def _():
                acc_ref[...] += p

            @pl.when(kk == nk - 1)
            def _():
                finish(acc_ref[...])

    res = pl.pallas_call(
        body,
        name=name,
        grid=grid,
        in_specs=[a_spec, b_spec, *[s for _, s in extras]] + ([pl.BlockSpec(memory_space=pl.ANY)] if ni else []),
        out_specs=[o[2] for o in outs],
        out_shape=[jax.ShapeDtypeStruct(o[0], o[1]) for o in outs],
        scratch_shapes=[pltpu.VMEM(blk, F32)] if nk > 1 else [],
        input_output_aliases={2 + ne: 0} if ni else {},
        compiler_params=_cparams(("parallel", "parallel", "arbitrary")),
    )(a, b, *[e for e, _ in extras], *([into] if ni else []))
    return res[0] if no == 1 else res


def _norm_fwd(h, w, name):
    t, d = h.shape
    tm = _tile(t, (1056, 384, 128))

    def body(h_ref, w_ref, u_ref):
        x = h_ref[...]
        r = lax.rsqrt(jnp.mean(x * x, axis=-1, keepdims=True) + EPS)
        u_ref[...] = (x * r * w_ref[...]).astype(BF16)

    return pl.pallas_call(
        body, name=name, grid=(t // tm,),
        in_specs=[pl.BlockSpec((tm, d), lambda i: (i, 0)), pl.BlockSpec((1, d), lambda i: (0, 0))],
        out_specs=pl.BlockSpec((tm, d), lambda i: (i, 0)),
        out_shape=jax.ShapeDtypeStruct((t, d), BF16),
        compiler_params=_cparams(("parallel",)),
    )(h, w.reshape(1, d))


def _norm_bwd(dres, du, h, w, name):
    t, d = h.shape
    tm = _tile(t, (528, 384, 128))

    def body(dres_ref, du_ref, h_ref, w_ref, dh_ref, dw_ref):
        x = h_ref[...]
        r = lax.rsqrt(jnp.mean(x * x, axis=-1, keepdims=True) + EPS)
        xhat = x * r
        du_v = du_ref[...]
        g = du_v * w_ref[...]
        dh_ref[...] = dres_ref[...] + r * (g - xhat * jnp.mean(g * xhat, axis=-1, keepdims=True))

        @pl.when(pl.program_id(0) == 0)
        def _():
            dw_ref[...] = jnp.zeros_like(dw_ref)

        dw_ref[...] += jnp.sum(du_v * xhat, axis=0, keepdims=True)

    row = pl.BlockSpec((tm, d), lambda i: (i, 0))
    vec = pl.BlockSpec((1, d), lambda i: (0, 0))
    dh, dw = pl.pallas_call(
        body, name=name, grid=(t // tm,),
        in_specs=[row, row, row, vec], out_specs=[row, vec],
        out_shape=[jax.ShapeDtypeStruct((t, d), F32), jax.ShapeDtypeStruct((1, d), F32)],
        compiler_params=_cparams(("arbitrary",)),
    )(dres, du, h, w.reshape(1, d))
    return dh, dw.reshape(d)


def _loss_head(h, w, tgt):
    t, d = h.shape
    tm = CHUNK
    nb = ROW_X // tm

    def body(h_ref, w_ref, t_ref, loss_ref, dh_ref, dw_ref):
        i = pl.program_id(0)
        x = h_ref[...]
        r = lax.rsqrt(jnp.mean(x * x, axis=-1, keepdims=True) + EPS)
        xhat = x * r
        wv = w_ref[...]
        live = i >= nb
        err = jnp.where(live, xhat * wv - t_ref[...], 0.0)
        dout = err * (1.0 / d)
        g = dout * wv
        dh_ref[...] = r * (g - xhat * jnp.mean(g * xhat, axis=-1, keepdims=True))

        @pl.when(i == 0)
        def _():
            loss_ref[...] = jnp.zeros_like(loss_ref)
            dw_ref[...] = jnp.zeros_like(dw_ref)

        loss_ref[...] += 0.5 * jnp.sum(jnp.sum(err * err, axis=-1, keepdims=True), axis=0, keepdims=True) * (1.0 / d)
        dw_ref[...] += jnp.sum(dout * xhat, axis=0, keepdims=True)

    row = pl.BlockSpec((tm, d), lambda i: (i, 0))
    vec = pl.BlockSpec((1, d), lambda i: (0, 0))
    loss, dh, dw = pl.pallas_call(
        body, name="loss_head", grid=(t // tm,),
        in_specs=[row, vec, pl.BlockSpec((tm, d), lambda i: (jnp.maximum(i - nb, 0), 0))],
        out_specs=[pl.BlockSpec((1, 1), lambda i: (0, 0)), row, vec],
        out_shape=[jax.ShapeDtypeStruct((1, 1), F32), jax.ShapeDtypeStruct((t, d), F32), jax.ShapeDtypeStruct((1, d), F32)],
        compiler_params=_cparams(("arbitrary",)),
    )(h, w.reshape(1, d), tgt)
    return loss[0, 0], dh, dw.reshape(d)


def _gate_fwd(proj, b_gate, y_pool, y_ssd):
    t = proj.shape[0]
    d = D_MODEL
    tm = _tile(t, (528, 384, 128))

    def body(gp_ref, gs_ref, bp_ref, bs_ref, yp_ref, ys_ref, o_ref):
        gp = _sigmoid(gp_ref[...] + bp_ref[...])
        gs = _sigmoid(gs_ref[...] + bs_ref[...])
        o_ref[...] = (gp * yp_ref[...] + gs * ys_ref[...]).astype(BF16)

    row = pl.BlockSpec((tm, d), lambda i: (i, 0))
    return pl.pallas_call(
        body, name="gate_fwd", grid=(t // tm,),
        in_specs=[pl.BlockSpec((tm, d), lambda i: (i, PGATE // d)), pl.BlockSpec((tm, d), lambda i: (i, PGATE // d + 1)),
                  pl.BlockSpec((1, d), lambda i: (0, 0)), pl.BlockSpec((1, d), lambda i: (0, 1)), row, row],
        out_specs=row, out_shape=jax.ShapeDtypeStruct((t, d), BF16),
        compiler_params=_cparams(("parallel",)),
    )(proj, proj, b_gate.reshape(1, 2 * d), b_gate.reshape(1, 2 * d), y_pool, y_ssd)


def _gate_bwd(dmerged, proj, b_gate, y_pool, y_ssd):
    t = proj.shape[0]
    d = D_MODEL
    tm = _tile(t, (384, 128))

    def body(dm_ref, gp_ref, gs_ref, bp_ref, bs_ref, yp_ref, ys_ref, dyp_ref, dys_ref, dg_ref, db_ref):
        dm = dm_ref[...]
        gp = _sigmoid(gp_ref[...] + bp_ref[...])
        gs = _sigmoid(gs_ref[...] + bs_ref[...])
        dyp_ref[...] = (dm * gp).astype(BF16)
        dys_ref[...] = (dm * gs).astype(BF16)
        dgp = dm * yp_ref[...] * gp * (1.0 - gp)
        dgs = dm * ys_ref[...] * gs * (1.0 - gs)
        dg_ref[:, :d] = dgp.astype(BF16)
        dg_ref[:, d:] = dgs.astype(BF16)

        @pl.when(pl.program_id(0) == 0)
        def _():
            db_ref[...] = jnp.zeros_like(db_ref)

        db_ref[:, :d] += jnp.sum(dgp, axis=0, keepdims=True)
        db_ref[:, d:] += jnp.sum(dgs, axis=0, keepdims=True)

    row = pl.BlockSpec((tm, d), lambda i: (i, 0))
    dyp, dys, dg, db = pl.pallas_call(
        body, name="gate_bwd", grid=(t // tm,),
        in_specs=[row, pl.BlockSpec((tm, d), lambda i: (i, PGATE // d)), pl.BlockSpec((tm, d), lambda i: (i, PGATE // d + 1)),
                  pl.BlockSpec((1, d), lambda i: (0, 0)), pl.BlockSpec((1, d), lambda i: (0, 1)), row, row],
        out_specs=[row, row, pl.BlockSpec((tm, 2 * d), lambda i: (i, PGATE // (2 * d))),
                   pl.BlockSpec((1, 2 * d), lambda i: (0, 0))],
        out_shape=[jax.ShapeDtypeStruct((t, d), BF16), jax.ShapeDtypeStruct((t, d), BF16),
                   jax.ShapeDtypeStruct((t, PCOLS), BF16), jax.ShapeDtypeStruct((1, 2 * d), F32)],
        compiler_params=_cparams(("arbitrary",)),
    )(dmerged, proj, proj, b_gate.reshape(1, 2 * d), b_gate.reshape(1, 2 * d), y_pool, y_ssd)
    return dyp, dys, dg, db.reshape(2 * d)


POOL_HALO = 16


def _pool_counts(row0, n, win):
    pos1 = row0 + lax.broadcasted_iota(jnp.int32, (n, 1), 0) - (N_PAD - 1)
    return jnp.clip(pos1, 1, win).astype(F32)


N_CHIPS = 4
WG_ROWS = POOL_GDIM // N_CHIPS
WUP_ROWS = D_MODEL // N_CHIPS


def _group_w(wg_ref, g):
    return jnp.concatenate([wg_ref[k, g] for k in range(N_CHIPS)], axis=0)


def _pool_w_specs(layer):
    return [pl.BlockSpec((None, N_CHIPS, len(POOL_WINDOWS), WG_ROWS, POOL_GDIM), lambda i: (layer, 0, 0, 0, 0)),
            pl.BlockSpec((None, N_CHIPS, WUP_ROWS, D_MODEL), lambda i: (layer, 0, 0, 0))]


def _pool_fwd(proj, wg, scale, wup, layer):
    t = proj.shape[0]
    d = D_MODEL
    tm = _tile(t, (384, 128))
    hb = tm // POOL_HALO

    def body(u_ref, halo_ref, sc_ref, wg_ref, wup_ref, pooled_ref, yg_ref, ypm_ref, yp_ref):
        i = pl.program_id(0)
        x = u_ref[...]
        halo = jnp.where(i > 0, halo_ref[...], 0.0)
        xc = jnp.concatenate([halo, x], axis=0)
        for g, win in enumerate(POOL_WINDOWS):
            sl = slice(g * POOL_GDIM, (g + 1) * POOL_GDIM)
            s = xc[:, sl]
            k = 1
            while k < win:
                s = s + pltpu.roll(s, k, axis=0)
                k *= 2
            pooled = s[POOL_HALO:] / _pool_counts(i * tm, tm, win) - x[:, sl]
            pb = pooled.astype(BF16)
            pooled_ref[:, sl] = pb
            yg_ref[:, sl] = _dot(pb, _group_w(wg_ref, g))
        ypm = (yg_ref[...] * sc_ref[...]).astype(BF16)
        ypm_ref[...] = ypm
        acc = _dot(ypm[:, :WUP_ROWS], wup_ref[0])
        for k in range(1, N_CHIPS):
            acc = acc + _dot(ypm[:, k * WUP_ROWS:(k + 1) * WUP_ROWS], wup_ref[k])
        yp_ref[...] = acc

    row = pl.BlockSpec((tm, d), lambda i: (i, 0))
    return pl.pallas_call(
        body, name=f"pool_fwd_{layer}", grid=(t // tm,),
        in_specs=[pl.BlockSpec((tm, d), lambda i: (i, PPOOL // d)),
                  pl.BlockSpec((POOL_HALO, d), lambda i: (jnp.maximum(i * hb - 1, 0), PPOOL // d)),
                  pl.BlockSpec((1, d), lambda i: (0, 0))] + _pool_w_specs(layer),
        out_specs=[row, row, row, row],
        out_shape=[jax.ShapeDtypeStruct((t, d), BF16), jax.ShapeDtypeStruct((t, d), F32),
                   jax.ShapeDtypeStruct((t, d), BF16), jax.ShapeDtypeStruct((t, d), F32)],
        compiler_params=_cparams(("parallel",)),
    )(proj, proj, scale.reshape(1, d), wg, wup)


def _pool_bwd_a(dy_pool, yg, scale, wg, wup, layer):
    t, d = yg.shape
    tm = _tile(t, (384, 128))

    def body(dy_ref, yg_ref, sc_ref, wg_ref, wup_ref, q_ref, dyg_ref, dsc_ref):
        dy = dy_ref[...]
        dypm = jnp.concatenate([_dot(dy, wup_ref[k], _NT) for k in range(N_CHIPS)], axis=1)

        @pl.when(pl.program_id(0) == 0)
        def _():
            dsc_ref[...] = jnp.zeros_like(dsc_ref)

        dsc_ref[...] += jnp.sum(dypm * yg_ref[...], axis=0, keepdims=True)
        dyg = (dypm * sc_ref[...]).astype(BF16)
        dyg_ref[...] = dyg
        for g in range(len(POOL_WINDOWS)):
            sl = slice(g * POOL_GDIM, (g + 1) * POOL_GDIM)
            q_ref[:, sl] = _dot(dyg[:, sl], _group_w(wg_ref, g), _NT)

    row = pl.BlockSpec((tm, d), lambda i: (i, 0))
    vec = pl.BlockSpec((1, d), lambda i: (0, 0))
    q, dyg, dsc = pl.pallas_call(
        body, name=f"pool_bwd_a_{layer}", grid=(t // tm,),
        in_specs=[row, row, vec] + _pool_w_specs(layer),
        out_specs=[row, row, vec],
        out_shape=[jax.ShapeDtypeStruct((t, d), F32), jax.ShapeDtypeStruct((t, d), BF16), jax.ShapeDtypeStruct((1, d), F32)],
        compiler_params=_cparams(("arbitrary",)),
    )(dy_pool, yg, scale.reshape(1, d), wg, wup)
    return q, dyg, dsc.reshape(d)


def _pool_bwd_b(q, dproj):
    t, d = q.shape
    tm = _tile(t, (384, 128))
    hb = tm // POOL_HALO
    nt = t // tm
    n = tm + POOL_HALO

    def body(q_ref, halo_ref, _, o_ref):
        i = pl.program_id(0)
        qv = q_ref[...]
        halo = jnp.where(i < nt - 1, halo_ref[...], 0.0)
        qc = jnp.concatenate([qv, halo], axis=0)
        for g, win in enumerate(POOL_WINDOWS):
            sl = slice(g * POOL_GDIM, (g + 1) * POOL_GDIM)
            s = qc[:, sl] / _pool_counts(i * tm, n, win)
            k = 1
            while k < win:
                s = s + pltpu.roll(s, n - k, axis=0)
                k *= 2
            o_ref[:, sl] = (s[:tm] - qv[:, sl]).astype(BF16)

    row = pl.BlockSpec((tm, d), lambda i: (i, 0))
    return pl.pallas_call(
        body, name="pool_bwd_b", grid=(nt,),
        in_specs=[row, pl.BlockSpec((POOL_HALO, d), lambda i: (jnp.minimum((i + 1) * hb, t // POOL_HALO - 1), 0)),
                  pl.BlockSpec(memory_space=pl.ANY)],
        out_specs=pl.BlockSpec((tm, d), lambda i: (i, PPOOL // d)), out_shape=jax.ShapeDtypeStruct(dproj.shape, BF16),
        input_output_aliases={2: 0},
        compiler_params=_cparams(("parallel",)),
    )(q, q, dproj)


def _pool_dwg(pooled, dyg, layer, into):
    t, d = pooled.shape
    tk = _tile(t, (1056, 384, 128))
    nk = t // tk
    gd = POOL_GDIM
    ng = d // gd

    def body(p_ref, g_ref, *rest):
        o_ref = rest[-1]

        @pl.when(pl.program_id(1) == 0)
        def _():
            o_ref[...] = jnp.zeros_like(o_ref)

        part = _dot(p_ref[...], g_ref[...], _TN)
        for k in range(N_CHIPS):
            o_ref[k] += part[k * WG_ROWS:(k + 1) * WG_ROWS]

    blk = pl.BlockSpec((tk, gd), lambda g, k: (k, g))
    ni = 0 if into is None else 1
    return pl.pallas_call(
        body, name=f"pool_dwg_{layer}", grid=(ng, nk),
        in_specs=[blk, blk] + [pl.BlockSpec(memory_space=pl.ANY)] * ni,
        out_specs=pl.BlockSpec((None, N_CHIPS, None, WG_ROWS, gd), lambda g, k: (layer, 0, g, 0, 0)),
        out_shape=jax.ShapeDtypeStruct((DEPTH, N_CHIPS, ng, WG_ROWS, gd), F32),
        input_output_aliases={2: 0} if ni else {},
        compiler_params=_cparams(("parallel", "arbitrary")),
    )(pooled, dyg, *([into] if ni else []))


CONV_W = 4
CONV_HALO = 8
XBC_BLK = PXBC // 1024


def _conv_fwd(proj, conv_w, conv_b):
    t = proj.shape[0]
    cw = 1024
    tm = _tile(t, (1056, 384, 128))
    hb = tm // CONV_HALO

    def body(x_ref, halo_ref, w_ref, b_ref, o_ref):
        i = pl.program_id(1)
        x = x_ref[...]
        halo = jnp.where(i > 0, halo_ref[...], 0.0)
        xc = jnp.concatenate([halo, x], axis=0)
        w = w_ref[...]
        acc = b_ref[...] + x * w[CONV_W - 1:CONV_W, :]
        for k in range(CONV_W - 1):
            acc = acc + pltpu.roll(xc, CONV_W - 1 - k, axis=0)[CONV_HALO:] * w[k:k + 1, :]
        row = i * tm + lax.broadcasted_iota(jnp.int32, (tm, 1), 0)
        o_ref[...] = jnp.where(row >= N_PAD, acc * _sigmoid(acc), 0.0)

    return pl.pallas_call(
        body, name="conv_fwd", grid=(D_XBC // cw, t // tm),
        in_specs=[pl.BlockSpec((tm, cw), lambda j, i: (i, XBC_BLK + j)),
                  pl.BlockSpec((CONV_HALO, cw), lambda j, i: (jnp.maximum(i * hb - 1, 0), XBC_BLK + j)),
                  pl.BlockSpec((CONV_W, cw), lambda j, i: (0, j)), pl.BlockSpec((1, cw), lambda j, i: (0, j))],
        out_specs=pl.BlockSpec((tm, cw), lambda j, i: (i, j)),
        out_shape=jax.ShapeDtypeStruct((t, D_XBC), F32),
        compiler_params=_cparams(("parallel", "parallel")),
    )(proj, proj, conv_w, conv_b.reshape(1, D_XBC))


def _conv_bwd(dxa, coff, proj, conv_w, conv_b, dproj):
    t, ncols = dxa.shape
    cw = 1024
    tm = _tile(t, (528, 384, 128))
    hb = tm // CONV_HALO
    nt = t // tm
    n = tm + 2 * CONV_HALO

    def body(d_ref, dn_ref, xp_ref, x_ref, xn_ref, w_ref, b_ref, _, o_ref, dw_ref, db_ref):
        i = pl.program_id(1)
        last = i == nt - 1
        xf = jnp.concatenate([jnp.where(i > 0, xp_ref[...], 0.0), x_ref[...], jnp.where(last, 0.0, xn_ref[...])], axis=0)
        df = jnp.concatenate([jnp.zeros((CONV_HALO, cw), F32), d_ref[...], jnp.where(last, 0.0, dn_ref[...])], axis=0)
        w = w_ref[...]
        sh = [pltpu.roll(xf, CONV_W - 1 - k, axis=0) if k < CONV_W - 1 else xf for k in range(CONV_W)]
        xc = b_ref[...]
        for k in range(CONV_W):
            xc = xc + sh[k] * w[k:k + 1, :]
        sig = _sigmoid(xc)
        row = i * tm - CONV_HALO + lax.broadcasted_iota(jnp.int32, (n, 1), 0)
        dxc = jnp.where((row >= N_PAD) & (row < t), df * (sig * (1.0 + xc * (1.0 - sig))), 0.0)
        acc = dxc * w[CONV_W - 1:CONV_W, :]
        for k in range(CONV_W - 1):
            acc = acc + pltpu.roll(dxc, n - (CONV_W - 1 - k), axis=0) * w[k:k + 1, :]
        o_ref[...] = acc[CONV_HALO:CONV_HALO + tm].astype(BF16)

        @pl.when(i == 0)
        def _():
            dw_ref[...] = jnp.zeros_like(dw_ref)
            db_ref[...] = jnp.zeros_like(db_ref)

        dm = dxc[CONV_HALO:CONV_HALO + tm]
        db_ref[...] += jnp.sum(dm, axis=0, keepdims=True)
        for k in range(CONV_W):
            dw_ref[k:k + 1, :] += jnp.sum(dm * sh[k][CONV_HALO:CONV_HALO + tm], axis=0, keepdims=True)

    def xspec(rows, fn):
        return pl.BlockSpec((rows, cw), lambda j, i: (fn(i), XBC_BLK + coff + j))

    prev = lambda i: jnp.maximum(i * hb - 1, 0)
    nxt = lambda i: jnp.minimum((i + 1) * hb, t // CONV_HALO - 1)
    dxbc, dw, db = pl.pallas_call(
        body, name=f"conv_bwd_{coff}", grid=(ncols // cw, nt),
        in_specs=[pl.BlockSpec((tm, cw), lambda j, i: (i, j)), pl.BlockSpec((CONV_HALO, cw), lambda j, i: (nxt(i), j)),
                  xspec(CONV_HALO, prev), xspec(tm, lambda i: i), xspec(CONV_HALO, nxt),
                  pl.BlockSpec((CONV_W, cw), lambda j, i: (0, coff + j)), pl.BlockSpec((1, cw), lambda j, i: (0, coff + j)),
                  pl.BlockSpec(memory_space=pl.ANY)],
        out_specs=[pl.BlockSpec((tm, cw), lambda j, i: (i, XBC_BLK + coff + j)), pl.BlockSpec((CONV_W, cw), lambda j, i: (0, j)),
                   pl.BlockSpec((1, cw), lambda j, i: (0, j))],
        out_shape=[jax.ShapeDtypeStruct(dproj.shape, BF16), jax.ShapeDtypeStruct((CONV_W, ncols), F32),
                   jax.ShapeDtypeStruct((1, ncols), F32)],
        input_output_aliases={7: 0},
        compiler_params=_cparams(("parallel", "arbitrary")),
    )(dxa, dxa, proj, proj, proj, conv_w, conv_b.reshape(1, D_XBC), dproj)
    return dxbc, dw, db.reshape(ncols)


def _cumsum(x, axis, reverse=False):
    n = x.shape[axis]
    idx = lax.broadcasted_iota(jnp.int32, x.shape, axis)
    k = 1
    while k < n:
        if reverse:
            x = x + jnp.where(idx < n - k, pltpu.roll(x, n - k, axis=axis), 0.0)
        else:
            x = x + jnp.where(idx >= k, pltpu.roll(x, k, axis=axis), 0.0)
        k *= 2
    return x


def _head_masks():
    lane = lax.broadcasted_iota(jnp.int32, (1, GROUP_W), 1)
    return [(lane >= r * HEAD_DIM) & (lane < (r + 1) * HEAD_DIM) for r in range(HEADS_PER_GROUP)]


def _expand_heads(cols, hm):
    out = jnp.where(hm[0], cols[:, 0:1], 0.0)
    for r in range(1, HEADS_PER_GROUP):
        out = out + jnp.where(hm[r], cols[:, r:r + 1], 0.0)
    return out


def _ssd_decay(dt_raw, dt_bias, a_log):
    t = dt_raw.shape[0]

    def body(raw_ref, b_ref, al_ref, dt_ref, acs_ref, sig_ref):
        raw = raw_ref[...] + b_ref[...]
        rowid = pl.program_id(0) * CHUNK + lax.broadcasted_iota(jnp.int32, (CHUNK, 1), 0)
        dt = jnp.where(rowid >= N_PAD, _softplus(raw), 0.0)
        dt_ref[...] = dt
        acs_ref[...] = _cumsum(dt * -jnp.exp(al_ref[...]), 0)
        sig_ref[...] = _sigmoid(raw)

    blk = pl.BlockSpec((CHUNK, DT_PAD), lambda c: (c, 0))
    vec = pl.BlockSpec((1, DT_PAD), lambda c: (0, 0))
    pad = lambda v: jnp.pad(v, (0, DT_PAD - N_HEADS)).reshape(1, DT_PAD)
    dt, acs, sig = pl.pallas_call(
        body, name="ssd_decay", grid=(t // CHUNK,), in_specs=[blk, vec, vec], out_specs=[blk, blk, blk],
        out_shape=[jax.ShapeDtypeStruct((t, DT_PAD), F32)] * 3,
        compiler_params=_cparams(("parallel",)),
    )(dt_raw, pad(dt_bias), pad(a_log))
    grp = lambda v: v[:, :N_HEADS].reshape(t, N_GROUPS, HEADS_PER_GROUP).transpose(1, 0, 2)
    acs_g = grp(acs)
    return dict(dt=grp(dt), acs=acs_g, acs_rows=acs_g.transpose(0, 2, 1), sig=grp(sig),
                a_log=a_log.reshape(N_GROUPS, 1, HEADS_PER_GROUP))


def _ssd_q(dtc, acs4, acs_r):
    hm = _head_masks()
    dt_exp = _expand_heads(dtc, hm)
    acs = _expand_heads(acs4, hm)
    atot = acs[CHUNK - 1:CHUNK, :]
    return dict(dtc=dtc, hm=hm, dt_exp=dt_exp, acs=acs, acs_r=acs_r,
                ea=jnp.exp(acs), ds=jnp.exp(atot - acs), dec=jnp.exp(atot))


def _stack4(x):
    return jnp.concatenate([x] * HEADS_PER_GROUP, axis=0)


def _ssd_decay_stacks(q):
    hpg = HEADS_PER_GROUP
    a_col = jnp.concatenate([q["acs"][:, r * HEAD_DIM:r * HEAD_DIM + 1] for r in range(hpg)], axis=0)
    a_row = jnp.concatenate([jnp.broadcast_to(q["acs_r"][r:r + 1, :], (CHUNK, CHUNK)) for r in range(hpg)], axis=0)
    ri = lax.broadcasted_iota(jnp.int32, (hpg * CHUNK, CHUNK), 0) % CHUNK
    ci = lax.broadcasted_iota(jnp.int32, (hpg * CHUNK, CHUNK), 1)
    diff = a_col - a_row
    lm = jnp.exp(jnp.where(ri >= ci, diff, -jnp.inf))
    lt = jnp.exp(jnp.where(ri <= ci, -diff, -jnp.inf))
    return lm, lt


def _pick_heads(stacked, hm):
    out = jnp.where(hm[0], stacked[:CHUNK], 0.0)
    for r in range(1, HEADS_PER_GROUP):
        out = out + jnp.where(hm[r], stacked[r * CHUNK:(r + 1) * CHUNK], 0.0)
    return out


def _mask_heads(x, hm):
    return jnp.concatenate([jnp.where(hm[r], x, 0.0) for r in range(HEADS_PER_GROUP)], axis=0)


SSD_GP = 4


def _ssd_specs(cidx):
    gp, hpg = SSD_GP, HEADS_PER_GROUP
    return [
        pl.BlockSpec((CHUNK, gp * GROUP_W), lambda g, c: (cidx(c), g)),
        pl.BlockSpec((CHUNK, gp * D_STATE), lambda g, c: (cidx(c), D_INNER // (gp * D_STATE) + g)),
        pl.BlockSpec((CHUNK, gp * D_STATE), lambda g, c: (cidx(c), (D_INNER + 1024) // (gp * D_STATE) + g)),
        pl.BlockSpec((gp, CHUNK, hpg), lambda g, c: (g, cidx(c), 0)),
        pl.BlockSpec((gp, CHUNK, hpg), lambda g, c: (g, cidx(c), 0)),
        pl.BlockSpec((gp, hpg, CHUNK), lambda g, c: (g, 0, cidx(c))),
    ]


def _ssd_fwd(xa, dec):
    t = xa.shape[0]
    nc = t // CHUNK
    gp, gw, ds = SSD_GP, GROUP_W, D_STATE

    def body(xs_ref, b_ref, c_ref, dt_ref, acs_ref, acsr_ref, y_ref, prev_ref, st_ref):
        @pl.when(pl.program_id(1) == 0)
        def _():
            st_ref[...] = jnp.zeros_like(st_ref)

        for k in range(gp):
            q = _ssd_q(dt_ref[k], acs_ref[k], acsr_ref[k])
            xdt = xs_ref[:, k * gw:(k + 1) * gw] * q["dt_exp"]
            bm = b_ref[:, k * ds:(k + 1) * ds].astype(BF16)
            cm = c_ref[:, k * ds:(k + 1) * ds].astype(BF16)
            cb = _dot(cm, bm, _NT)
            st = st_ref[k]
            prev_ref[0, k] = st
            lm, _ = _ssd_decay_stacks(q)
            y_diag = _pick_heads(_dot((_stack4(cb) * lm).astype(BF16), xdt.astype(BF16)), q["hm"])
            y_ref[:, k * gw:(k + 1) * gw] = y_diag + _dot(cm, st.astype(BF16)) * q["ea"]
            st_ref[k] = q["dec"] * st + _dot(bm, (xdt * q["ds"]).astype(BF16), _TN)

    y, prev = pl.pallas_call(
        body, name="ssd_fwd", grid=(N_GROUPS // gp, nc),
        in_specs=_ssd_specs(lambda c: c),
        out_specs=[pl.BlockSpec((CHUNK, gp * gw), lambda g, c: (c, g)),
                   pl.BlockSpec((1, gp, ds, gw), lambda g, c: (c, g, 0, 0))],
        out_shape=[jax.ShapeDtypeStruct((t, D_INNER), F32), jax.ShapeDtypeStruct((nc, N_GROUPS, ds, gw), F32)],
        scratch_shapes=[pltpu.VMEM((gp, ds, gw), F32)],
        compiler_params=_cparams(("parallel", "arbitrary")),
    )(xa, xa, xa, dec["dt"], dec["acs"], dec["acs_rows"])
    return y, prev


def _ssd_bwd(dy, dxs_skip, xa, prev, dec):
    t = xa.shape[0]
    nc = t // CHUNK
    gp, gw, dstate = SSD_GP, GROUP_W, D_STATE
    hpg = HEADS_PER_GROUP

    def body(xs_ref, b_ref, c_ref, dt_ref, acs_ref, acsr_ref, sig_ref, al_ref, dy_ref, sk_ref, prev_ref,
             dxs_ref, db_ref, dc_ref, ddt_ref, dbias_ref, dalog_ref, dst_ref):
        @pl.when(pl.program_id(1) == 0)
        def _():
            dst_ref[...] = jnp.zeros_like(dst_ref)
            dbias_ref[...] = jnp.zeros_like(dbias_ref)
            dalog_ref[...] = jnp.zeros_like(dalog_ref)

        for k in range(gp):
            one_group(k, xs_ref, b_ref, c_ref, dt_ref, acs_ref, acsr_ref, sig_ref, al_ref, dy_ref, sk_ref, prev_ref,
                      dxs_ref, db_ref, dc_ref, ddt_ref, dbias_ref, dalog_ref, dst_ref)

    def one_group(k, xs_ref, b_ref, c_ref, dt_ref, acs_ref, acsr_ref, sig_ref, al_ref, dy_ref, sk_ref, prev_ref,
                  dxs_ref, db_ref, dc_ref, ddt_ref, dbias_ref, dalog_ref, dst_ref):
        wide = slice(k * gw, (k + 1) * gw)
        narrow = slice(k * dstate, (k + 1) * dstate)
        q = _ssd_q(dt_ref[k], acs_ref[k], acsr_ref[k])
        a_r = -jnp.exp(al_ref[k])
        hm, ds, dec, dt_exp = q["hm"], q["ds"], q["dec"], q["dt_exp"]
        xs = xs_ref[:, wide]
        xdt = xs * dt_exp
        xdtb = xdt.astype(BF16)
        bm = b_ref[:, narrow].astype(BF16)
        cm = c_ref[:, narrow].astype(BF16)
        cb = _dot(cm, bm, _NT)
        bc = _dot(bm, cm, _NT)
        dyv = dy_ref[:, wide]
        dye = (dyv * q["ea"]).astype(BF16)
        pst = prev_ref[0, k]
        dst = dst_ref[k]
        dstb = dst.astype(BF16)
        dx_state = ds * _dot(bm, dstb)
        onehot = [(lax.broadcasted_iota(jnp.int32, (1, hpg), 1) == r).astype(F32) for r in range(hpg)]
        lm, lt = _ssd_decay_stacks(q)
        dyb = dyv.astype(BF16)
        gl = _dot(_mask_heads(dyv, hm).astype(BF16), xdtb, _NT) * lm
        glt = _dot(_mask_heads(xdt, hm).astype(BF16), dyb, _NT) * lt
        bc4 = _stack4(bc)
        dxdt = dx_state + _pick_heads(_dot((bc4 * lt).astype(BF16), dyb), hm)
        wd = jnp.sum(gl * _stack4(cb), axis=1, keepdims=True) - jnp.sum(glt * bc4, axis=1, keepdims=True)
        dcb = gl[:CHUNK]
        dcbt = glt[:CHUNK]
        qa = wd[:CHUNK] * onehot[0]
        for r in range(1, hpg):
            dcb = dcb + gl[r * CHUNK:(r + 1) * CHUNK]
            dcbt = dcbt + glt[r * CHUNK:(r + 1) * CHUNK]
            qa = qa + wd[r * CHUNK:(r + 1) * CHUNK] * onehot[r]
        pstb = pst.astype(BF16)
        dc_ref[:, narrow] = _dot(dcb.astype(BF16), bm) + _dot(dye, pstb, _NT)
        db_ref[:, narrow] = _dot(dcbt.astype(BF16), cm) + _dot((xdt * ds).astype(BF16), dstb, _NT)
        dst_ref[k] = dec * dst + _dot(cm, dye, _TN)
        dxs_ref[:, wide] = dxdt * dt_exp + sk_ref[:, wide]

        t2 = xdt * dx_state
        t1 = dyv * (_dot(cm, pstb) * q["ea"]) - t2
        t3 = dst * pst
        t4 = dxdt * xs
        xd = jnp.zeros((CHUNK, hpg), F32)
        dal = jnp.zeros((1, hpg), F32)
        for r in range(hpg):
            qa = qa + jnp.sum(jnp.where(hm[r], t1, 0.0), axis=1, keepdims=True) * onehot[r]
            xd = xd + jnp.sum(jnp.where(hm[r], t4, 0.0), axis=1, keepdims=True) * onehot[r]
            s2 = jnp.sum(jnp.sum(jnp.where(hm[r], t2, 0.0), axis=1, keepdims=True), axis=0, keepdims=True)
            s3 = jnp.sum(jnp.sum(jnp.where(hm[r], t3, 0.0), axis=1, keepdims=True), axis=0, keepdims=True)
            dal = dal + (s2 + dec[:, r * HEAD_DIM:r * HEAD_DIM + 1] * s3) * onehot[r]
        rc = _cumsum(qa, 0, reverse=True) + dal
        rowid = (nc - 1 - pl.program_id(1)) * CHUNK + lax.broadcasted_iota(jnp.int32, (CHUNK, 1), 0)
        ddt_raw = jnp.where(rowid >= N_PAD, (rc * a_r + xd) * sig_ref[k], 0.0)
        ddt_ref[k] = ddt_raw
        dbias_ref[k] += jnp.sum(ddt_raw, axis=0, keepdims=True)
        dalog_ref[k] += jnp.sum(rc * q["dtc"], axis=0, keepdims=True) * a_r

    rev = lambda c: nc - 1 - c
    blk = pl.BlockSpec((CHUNK, gp * gw), lambda g, c: (rev(c), g))
    nblk = pl.BlockSpec((CHUNK, gp * dstate), lambda g, c: (rev(c), g))
    cols = pl.BlockSpec((gp, CHUNK, hpg), lambda g, c: (g, rev(c), 0))
    small = pl.BlockSpec((gp, 1, hpg), lambda g, c: (g, 0, 0))
    dxs, db, dc, ddt, dbias, dalog = pl.pallas_call(
        body, name="ssd_bwd", grid=(N_GROUPS // gp, nc),
        in_specs=_ssd_specs(rev) + [cols, small, blk, blk, pl.BlockSpec((1, gp, dstate, gw), lambda g, c: (rev(c), g, 0, 0))],
        out_specs=[blk, nblk, nblk, cols, small, small],
        out_shape=[jax.ShapeDtypeStruct((t, D_INNER), F32), jax.ShapeDtypeStruct((t, N_GROUPS * dstate), F32),
                   jax.ShapeDtypeStruct((t, N_GROUPS * dstate), F32), jax.ShapeDtypeStruct((N_GROUPS, t, hpg), F32),
                   jax.ShapeDtypeStruct((N_GROUPS, 1, hpg), F32), jax.ShapeDtypeStruct((N_GROUPS, 1, hpg), F32)],
        scratch_shapes=[pltpu.VMEM((gp, dstate, gw), F32)],
        compiler_params=_cparams(("parallel", "arbitrary")),
    )(xa, xa, xa, dec["dt"], dec["acs"], dec["acs_rows"], dec["sig"], dec["a_log"], dy, dxs_skip, prev)
    ddt_raw = ddt.transpose(1, 0, 2).reshape(t, N_HEADS)
    return dxs, db, dc, ddt_raw, dbias.reshape(N_HEADS), dalog.reshape(N_HEADS)


def _ssd_post_fwd(y, xa, proj, dsk, nw):
    t = y.shape[0]
    di = D_INNER
    tm = _tile(t, (384, 128))

    def body(y_ref, xs_ref, z_ref, dsk_ref, nw_ref, o_ref):
        z = z_ref[...]
        yz = (y_ref[...] + xs_ref[...] * dsk_ref[...]) * (z * _sigmoid(z))
        nwv = nw_ref[...]
        for g in range(N_GROUPS):
            sl = slice(g * GROUP_W, (g + 1) * GROUP_W)
            v = yz[:, sl]
            rg = lax.rsqrt(jnp.mean(v * v, axis=-1, keepdims=True) + EPS)
            o_ref[:, sl] = (v * rg * nwv[:, sl]).astype(BF16)

    row = pl.BlockSpec((tm, di), lambda i: (i, 0))
    vec = pl.BlockSpec((1, di), lambda i: (0, 0))
    return pl.pallas_call(
        body, name="ssd_post_fwd", grid=(t // tm,),
        in_specs=[row, row, row, vec, vec], out_specs=row, out_shape=jax.ShapeDtypeStruct((t, di), BF16),
        compiler_params=_cparams(("parallel",)),
    )(y, xa, proj, dsk.reshape(1, di), nw.reshape(1, di))


def _ssd_post_bwd(dyn, y, xa, proj, dsk, nw, dproj):
    t = y.shape[0]
    di = D_INNER
    tm = CHUNK

    def body(dyn_ref, y_ref, xs_ref, z_ref, dsk_ref, nw_ref, _, dy_ref, dsk_x_ref, dz_ref, dnw_ref, ddsk_ref):
        @pl.when(pl.program_id(0) == 0)
        def _():
            dnw_ref[...] = jnp.zeros_like(dnw_ref)
            ddsk_ref[...] = jnp.zeros_like(ddsk_ref)

        for g in range(N_GROUPS):
            sl = slice(g * GROUP_W, (g + 1) * GROUP_W)
            z = z_ref[:, sl]
            sig = _sigmoid(z)
            s = z * sig
            xs = xs_ref[:, sl]
            dskv = dsk_ref[:, sl]
            yt = y_ref[:, sl] + xs * dskv
            yz = yt * s
            rg = lax.rsqrt(jnp.mean(yz * yz, axis=-1, keepdims=True) + EPS)
            xhat = yz * rg
            dynv = dyn_ref[:, sl]
            gg = dynv * nw_ref[:, sl]
            dnw_ref[:, sl] += jnp.sum(dynv * xhat, axis=0, keepdims=True)
            dyz = rg * (gg - xhat * jnp.mean(gg * xhat, axis=-1, keepdims=True))
            dyt = dyz * s
            dy_ref[:, sl] = dyt
            dsk_x_ref[:, sl] = dyt * dskv
            dz_ref[:, sl] = (dyz * yt * (sig * (1.0 + z * (1.0 - sig)))).astype(BF16)
            ddsk_ref[:, sl] += jnp.sum(dyt * xs, axis=0, keepdims=True)

    row = pl.BlockSpec((tm, di), lambda i: (i, 0))
    vec = pl.BlockSpec((1, di), lambda i: (0, 0))
    dy, dxsk, dz, dnw, ddsk = pl.pallas_call(
        body, name="ssd_post_bwd", grid=(t // tm,),
        in_specs=[row, row, row, row, vec, vec, pl.BlockSpec(memory_space=pl.ANY)], out_specs=[row, row, row, vec, vec],
        out_shape=[jax.ShapeDtypeStruct((t, di), F32), jax.ShapeDtypeStruct((t, di), F32), jax.ShapeDtypeStruct(dproj.shape, BF16),
                   jax.ShapeDtypeStruct((1, di), F32), jax.ShapeDtypeStruct((1, di), F32)],
        input_output_aliases={6: 2},
        compiler_params=_cparams(("arbitrary",)),
    )(dyn, y, xa, proj, dsk.reshape(1, di), nw.reshape(1, di), dproj)
    return dy, dxsk, dz, dnw.reshape(di), ddsk.reshape(N_HEADS, HEAD_DIM).sum(axis=1)


SHARD_COLS = IN_COLS // N_CHIPS


def _split_w_in(w_in_sh):
    nl = w_in_sh.shape[0]
    w = w_in_sh.transpose(0, 2, 1, 3).reshape(nl, D_MODEL, IN_COLS)
    main = jnp.concatenate([w[..., OFF_Z:OFF_XBC], w[..., OFF_GATE:], w[..., :OFF_Z], w[..., OFF_XBC:OFF_DT]], axis=-1)
    dt = jnp.pad(w[..., OFF_DT:OFF_GATE], ((0, 0), (0, 0), (0, DT_PAD - N_HEADS)))
    return main, dt


def _merge_dw_in(dmain, ddt):
    full = jnp.concatenate([dmain[:, PPOOL:PXBC], dmain[:, PZ:PGATE], dmain[:, PXBC:], ddt[:, :N_HEADS], dmain[:, PGATE:PPOOL]],
                           axis=1)
    return full.reshape(D_MODEL, N_CHIPS, SHARD_COLS).transpose(1, 0, 2)


def _relu2_epilogue(acc):
    hid = jnp.maximum(acc, 0.0)
    return acc, hid * hid


def _relu2_bwd_epilogue(acc, pre):
    return (acc * 2.0 * jnp.maximum(pre, 0.0),)


def _add_epilogue(acc, res):
    return (acc + res,)


def _w_spec(rows, cols, fn):
    return pl.BlockSpec((None, None, rows, cols), fn)


def _layer_fwd(h, w, big, layer):
    t = h.shape[0]
    d = D_MODEL
    tm = _tile(t, (1408, 384, 128))
    nt = t // tm
    row = _bs((tm, d), lambda i, j, k: (i, 0))
    s = {"h": h}
    u = _norm_fwd(h, w["mix_norm_w"], "norm_mix")
    proj = _mm(f"mm_proj_{layer}", _NN, (nt, PCOLS // d, 1), u, row,
               big["w_in_main"], _bs((None, d, d), lambda i, j, k: (layer, 0, j)),
               [((t, PCOLS), F32, _bs((tm, d), lambda i, j, k: (i, j)))])
    dt_raw = _mm(f"mm_dt_{layer}", _NN, (nt, 1, 1), u, row,
                 big["w_in_dt"], _bs((None, d, DT_PAD), lambda i, j, k: (layer, 0, 0)),
                 [((t, DT_PAD), F32, _bs((tm, DT_PAD), lambda i, j, k: (i, 0)))])
    pooled, yg, ypm, y_pool = _pool_fwd(proj, big["pool_w_group"], w["pool_scale"], big["w_pool_up"], layer)
    xa = _conv_fwd(proj, w["conv_w"], w["conv_b"])
    dec = _ssd_decay(dt_raw, w["dt_bias"], w["a_log"])
    y, prev = _ssd_fwd(xa, dec)
    yn = _ssd_post_fwd(y, xa, proj, w["d_skip_exp"], w["ssd_norm_w"])
    rs = D_INNER // N_CHIPS
    y_ssd = _mm(f"mm_ssd_out_{layer}", _NN, (nt, 1, N_CHIPS), yn, _bs((tm, rs), lambda i, j, k: (i, k)),
                big["w_ssd_out"], _w_spec(rs, d, lambda i, j, k: (layer, k, 0, 0)), [((t, d), F32, row)])
    merged = _gate_fwd(proj, w["b_gate"], y_pool, y_ssd)
    ro = d // N_CHIPS
    h1 = _mm(f"mm_o_{layer}", _NN, (nt, 1, N_CHIPS), merged, _bs((tm, ro), lambda i, j, k: (i, k)),
             big["w_o"], _w_spec(ro, d, lambda i, j, k: (layer, k, 0, 0)), [((t, d), F32, row)],
             extras=[(h, row)], epilogue=_add_epilogue)
    v = _norm_fwd(h1, w["mlp_norm_w"], "norm_mlp")
    tile = _bs((tm, d), lambda i, j, k: (i, j))
    pre, act = _mm(f"mm_ff1_{layer}", _NN, (nt, N_CHIPS, 1), v, row,
                   big["w_ff1"], _w_spec(d, d, lambda i, j, k: (layer, j, 0, 0)),
                   [((t, D_FF), F32, tile), ((t, D_FF), BF16, tile)], epilogue=_relu2_epilogue)
    h2 = _mm(f"mm_ff2_{layer}", _NN, (nt, 1, N_CHIPS), act, _bs((tm, d), lambda i, j, k: (i, k)),
             big["w_ff2"], _w_spec(d, d, lambda i, j, k: (layer, k, 0, 0)), [((t, d), F32, row)],
             extras=[(h1, row)], epilogue=_add_epilogue)
    s.update(u=u, proj=proj, dec=dec, pooled=pooled, yg=yg, ypm=ypm, y_pool=y_pool, xa=xa, y=y, prev=prev, yn=yn,
             y_ssd=y_ssd, merged=merged, h1=h1, v=v, pre=pre, act=act)
    return h2, s


def _dw(name, layer, a, b, shard_shape, by, gbuf, tk, b_blk=None):
    t = a.shape[0]
    nk = t // tk
    rows, cols = shard_shape
    if by == "rows":
        grid = (N_CHIPS, 1, nk)
        a_spec = _bs((tk, rows), lambda i, j, k: (k, i))
        b_spec = _bs((tk, cols), lambda i, j, k: (k, 0))
        o_spec = _w_spec(rows, cols, lambda i, j, k: (layer, i, 0, 0))
    else:
        grid = (1, N_CHIPS, nk)
        a_spec = _bs((tk, rows), lambda i, j, k: (k, 0))
        b_spec = b_blk if b_blk is not None else _bs((tk, cols), lambda i, j, k: (k, j))
        o_spec = _w_spec(rows, cols, lambda i, j, k: (layer, j, 0, 0))
    return _mm(f"{name}_{layer}", _TN, grid, a, a_spec, b, b_spec,
               [((DEPTH, N_CHIPS, rows, cols), F32, o_spec)], into=gbuf)


def _layer_bwd(dh, s, w, big, layer, gb):
    t = dh.shape[0]
    d = D_MODEL
    tm = _tile(t, (1408, 384, 128))
    tk = _tile(t, (1408, 384, 128))
    nt = t // tm
    row = _bs((tm, d), lambda i, j, k: (i, 0))
    tile = _bs((tm, d), lambda i, j, k: (i, j))
    g = {}
    gb = dict(gb)
    dpre = _mm(f"mm_dact_{layer}", _NT, (nt, N_CHIPS, 1), dh, row,
               big["w_ff2"], _w_spec(d, d, lambda i, j, k: (layer, j, 0, 0)), [((t, D_FF), BF16, tile)],
               extras=[(s["pre"], tile)], epilogue=_relu2_bwd_epilogue)
    gb["w_ff2"] = _dw("mm_dw_ff2", layer, s["act"],dh,(d, d), "rows", gb["w_ff2"], tk)
    dv = _mm(f"mm_dv_{layer}", _NT, (nt, 1, N_CHIPS), dpre, _bs((tm, d), lambda i, j, k: (i, k)),
             big["w_ff1"], _w_spec(d, d, lambda i, j, k: (layer, k, 0, 0)), [((t, d), F32, row)])
    gb["w_ff1"] = _dw("mm_dw_ff1", layer, s["v"],dpre,(d, d), "cols", gb["w_ff1"], tk)
    dh1, g["mlp_norm_w"] = _norm_bwd(dh, dv, s["h1"], w["mlp_norm_w"], "norm_mlp_bwd")
    ro = d // N_CHIPS
    dmerged = _mm(f"mm_dmerged_{layer}", _NT, (nt, N_CHIPS, 1), dh1, row,
                  big["w_o"], _w_spec(ro, d, lambda i, j, k: (layer, j, 0, 0)),
                  [((t, d), F32, _bs((tm, ro), lambda i, j, k: (i, j)))])
    gb["w_o"] = _dw("mm_dw_o", layer, s["merged"],dh1,(ro, d), "rows", gb["w_o"], tk)
    dy_pool, dy_ssd, dproj, g["b_gate"] = _gate_bwd(dmerged, s["proj"], w["b_gate"], s["y_pool"], s["y_ssd"])
    rs = D_INNER // N_CHIPS
    dyn = _mm(f"mm_dyn_{layer}", _NT, (nt, N_CHIPS, 1), dy_ssd, row,
              big["w_ssd_out"], _w_spec(rs, d, lambda i, j, k: (layer, j, 0, 0)),
              [((t, D_INNER), F32, _bs((tm, rs), lambda i, j, k: (i, j)))])
    gb["w_ssd_out"] = _dw("mm_dw_ssd_out", layer, s["yn"],dy_ssd,(rs, d), "rows", gb["w_ssd_out"], tk)
    dy, dxs_skip, dproj, g["ssd_norm_w"], g["d_skip"] = _ssd_post_bwd(dyn, s["y"], s["xa"], s["proj"], w["d_skip_exp"],
                                                                      w["ssd_norm_w"], dproj)
    dxs, db, dc, ddt_raw, g["dt_bias"], g["a_log"] = _ssd_bwd(dy, dxs_skip, s["xa"], s["prev"], s["dec"])
    dproj, dcw1, dcb1 = _conv_bwd(dxs, 0, s["proj"], w["conv_w"], w["conv_b"], dproj)
    dproj, dcw2, dcb2 = _conv_bwd(db, 2, s["proj"], w["conv_w"], w["conv_b"], dproj)
    dproj, dcw3, dcb3 = _conv_bwd(dc, 3, s["proj"], w["conv_w"], w["conv_b"], dproj)
    g["conv_w"] = jnp.concatenate([dcw1, dcw2, dcw3], axis=1)
    g["conv_b"] = jnp.concatenate([dcb1, dcb2, dcb3])
    q, dyg, g["pool_scale"] = _pool_bwd_a(dy_pool, s["yg"], w["pool_scale"], big["pool_w_group"], big["w_pool_up"], layer)
    gb["w_pool_up"] = _dw("mm_dw_pool_up", layer, s["ypm"],dy_pool,(ro, d), "rows", gb["w_pool_up"], tk)
    gb["pool_w_group"] = _pool_dwg(s["pooled"], dyg, layer, gb["pool_w_group"])
    dproj = _pool_bwd_b(q, dproj)
    ddt = jnp.pad(ddt_raw.astype(BF16), ((0, 0), (0, DT_PAD - N_HEADS)))
    nk = PCOLS // d
    du = _mm(f"mm_du_{layer}", _NT, (nt, 1, nk), dproj, _bs((tm, d), lambda i, j, k: (i, k)),
             big["w_in_main"], _bs((None, d, d), lambda i, j, k: (layer, 0, k)), [((t, d), F32, row)])
    du = _mm(f"mm_du_dt_{layer}", _NT, (nt, 1, 1), ddt, _bs((tm, DT_PAD), lambda i, j, k: (i, 0)),
             big["w_in_dt"], _bs((None, d, DT_PAD), lambda i, j, k: (layer, 0, 0)), [((t, d), F32, row)],
             extras=[(du, row)], epilogue=_add_epilogue)
    ntk = t // tk
    u_spec = _bs((tk, d), lambda i, j, k: (k, 0))
    dmain = _mm(f"mm_dw_in_{layer}", _TN, (1, nk, ntk), s["u"], u_spec, dproj, _bs((tk, d), lambda i, j, k: (k, j)),
                [((d, PCOLS), F32, _bs((d, d), lambda i, j, k: (0, j)))])
    ddtw = _mm(f"mm_dw_dt_{layer}", _TN, (1, 1, ntk), s["u"], u_spec, ddt, _bs((tk, DT_PAD), lambda i, j, k: (k, 0)),
               [((d, DT_PAD), F32, _bs((d, DT_PAD), lambda i, j, k: (0, 0)))])
    piece = _merge_dw_in(dmain, ddtw)[None]
    if gb["w_in"] is None:
        gb["w_in"] = lax.empty((DEPTH,) + piece.shape[1:], F32)
    gb["w_in"] = lax.dynamic_update_slice(gb["w_in"], piece, (layer, 0, 0, 0))
    dh0, g["mix_norm_w"] = _norm_bwd(dh1, du, s["h"], w["mix_norm_w"], "norm_mix_bwd")
    return dh0, g, gb


BIG = ("w_in", "pool_w_group", "w_pool_up", "w_ssd_out", "w_o", "w_ff1", "w_ff2")
PER_LAYER = ("mix_norm_w", "w_in", "b_gate", "pool_w_group", "pool_scale", "w_pool_up", "conv_w", "conv_b", "dt_bias",
             "a_log", "d_skip", "ssd_norm_w", "w_ssd_out", "w_o", "mlp_norm_w", "w_ff1", "w_ff2")


SMALL_PER_LAYER = tuple(n for n in PER_LAYER if n not in BIG)


def _local_step(x, tgt, big, small):
    seq = x.shape[0]
    big = dict(big)
    big["w_in_main"], big["w_in_dt"] = _split_w_in(big["w_in"])
    h = jnp.concatenate([jnp.zeros((N_PAD, D_MODEL), F32), small["meta_tokens"], x], axis=0)
    saved, ws = [], []
    for i in range(DEPTH):
        w = {n: small[n][i] for n in SMALL_PER_LAYER}
        w["d_skip_exp"] = jnp.repeat(w["d_skip"], HEAD_DIM)
        h, s = _layer_fwd(h, w, big, i)
        saved.append(s)
        ws.append(w)
    loss, dh, g_final = _loss_head(h, small["final_norm_w"], tgt)
    layer_g = [None] * DEPTH
    gb = {n: None for n in BIG}
    for i in reversed(range(DEPTH)):
        dh, layer_g[i], gb = _layer_bwd(dh, saved[i], ws[i], big, i, gb)
    grads = {n: jnp.stack([layer_g[i][n] for i in range(DEPTH)]) for n in SMALL_PER_LAYER}
    grads["final_norm_w"] = g_final
    grads["meta_tokens"] = dh[N_PAD:ROW_X]
    return loss, dh[ROW_X:ROW_X + seq], gb, grads


MESH = pl.DeviceIdType.MESH
LANES = 128
ANY = pl.BlockSpec(memory_space=pl.ANY)


def _place():
    x, y, c = lax.axis_index("x"), lax.axis_index("y"), lax.axis_index("c")
    chips = [(1 - x, y), (x, 1 - y), (1 - x, 1 - y)]
    return x, y, c, chips


def _remote(src, dst, send_sem, recv_sem, to):
    return pltpu.make_async_remote_copy(src_ref=src, dst_ref=dst, send_sem=send_sem, recv_sem=recv_sem,
                                        device_id=to, device_id_type=MESH)


HALF = DEPTH // 2


def _gather_big(shards):
    n = len(shards)
    per = 7

    def body(*refs):
        src, out = refs[:n], refs[n:2 * n]
        send_sems, recv_sems = refs[2 * n:]
        x, y, c, chips = _place()
        sibling = (x, y, 1 - c)
        me = 2 * x + y
        lo = HALF * c
        sends = []

        def go(src_ref, dst_ref, k, to):
            cp = _remote(src_ref, dst_ref, send_sems.at[k], recv_sems.at[k], to)
            cp.start()
            sends.append(cp)

        def landed(ref, k):
            _remote(ref, ref, send_sems.at[k], recv_sems.at[k], sibling).wait_recv()

        for p in range(n):
            go(src[p], out[p].at[:, me], per * p, sibling)
            for j, chip in enumerate(chips):
                go(src[p].at[pl.ds(lo, HALF)], out[p].at[pl.ds(lo, HALF), me], per * p + 1 + j, (*chip, c))
        for j, (cx, cy) in enumerate(chips):
            for p in range(n):
                blk = out[p].at[pl.ds(lo, HALF), 2 * cx + cy]
                landed(blk, per * p + 1 + j)
                go(blk, blk, per * p + 4 + j, sibling)
        for p in range(n):
            landed(out[p].at[:, me], per * p)
        for j, (cx, cy) in enumerate(chips):
            for p in range(n):
                landed(out[p].at[pl.ds(HALF * (1 - c), HALF), 2 * cx + cy], per * p + 4 + j)
        for cp in sends:
            cp.wait_send()

    return pl.pallas_call(
        body, name="gather_big", in_specs=[ANY] * n, out_specs=[ANY] * n,
        out_shape=[jax.ShapeDtypeStruct((a.shape[0], N_CHIPS) + a.shape[1:], a.dtype) for a in shards],
        scratch_shapes=[pltpu.SemaphoreType.DMA((per * n,)), pltpu.SemaphoreType.DMA((per * n,))],
    )(*shards)


def _exchange_small(v, reduce, name):
    rows_per = v.shape[0]
    vm = pl.BlockSpec(memory_space=pltpu.VMEM)

    def body(v_ref, out_ref, *scratch):
        if reduce:
            land_ref, send_sems, recv_sems, local_sem = scratch
        else:
            land_ref = out_ref
            send_sems, recv_sems, local_sem = scratch
        x, y, c, chips = _place()
        me, sibling = (x, y, c), (x, y, 1 - c)

        def rows(px, py, pc):
            return land_ref.at[4 * px + 2 * py + pc]

        def copy(k, block, to, src=None):
            return _remote(rows(*block) if src is None else src, rows(*block), send_sems.at[k], recv_sems.at[k], to)

        mine = pltpu.make_async_copy(v_ref, rows(*me), local_sem)
        mine.start()
        first = [copy(0, me, sibling, src=v_ref)]
        first += [copy(1 + j, me, (*chip, c), src=v_ref) for j, chip in enumerate(chips)]
        for cp in first:
            cp.start()
        passed = [copy(4 + j, (*chip, c), sibling) for j, chip in enumerate(chips)]
        for j, chip in enumerate(chips):
            copy(1 + j, (*chip, c), me).wait_recv()
            passed[j].start()
        copy(0, sibling, me).wait_recv()
        for j, chip in enumerate(chips):
            copy(4 + j, (*chip, 1 - c), me).wait_recv()
        for cp in first + passed:
            cp.wait_send()
        mine.wait()
        if reduce:
            acc = land_ref[0]
            for d in range(1, 8):
                acc = acc + land_ref[d]
            out_ref[...] = acc

    sems = [pltpu.SemaphoreType.DMA((7,)), pltpu.SemaphoreType.DMA((7,)), pltpu.SemaphoreType.DMA]
    if reduce:
        out_shape = jax.ShapeDtypeStruct((rows_per, LANES), F32)
        scratch = [pltpu.VMEM((8, rows_per, LANES), F32)] + sems
    else:
        out_shape = jax.ShapeDtypeStruct((8, rows_per, LANES), F32)
        scratch = sems
    return pl.pallas_call(
        body, name=name, in_specs=[vm], out_specs=vm, out_shape=out_shape, scratch_shapes=scratch,
        compiler_params=pltpu.CompilerParams(vmem_limit_bytes=VMEM_LIMIT),
    )(v)


def _swap_with_sibling(arrs, pick, name):
    n = len(arrs)
    shapes = [jax.eval_shape(lambda a: pick(a, 0), a) for a in arrs]

    def body(*refs):
        src, out = refs[:n], refs[n:2 * n]
        send_sems, recv_sems = refs[2 * n:]
        x, y, c, _ = _place()
        cps = [_remote(pick(src[p], c), out[p], send_sems.at[p], recv_sems.at[p], (x, y, 1 - c)) for p in range(n)]
        for cp in cps:
            cp.start()
        for cp in cps:
            cp.wait()

    return pl.pallas_call(
        body, name=name, in_specs=[ANY] * n, out_specs=[ANY] * n,
        out_shape=[jax.ShapeDtypeStruct(s.shape, s.dtype) for s in shapes],
        scratch_shapes=[pltpu.SemaphoreType.DMA((n,)), pltpu.SemaphoreType.DMA((n,))],
    )(*arrs)


def _layer_half(ref, hf):
    if isinstance(hf, int):
        return ref[HALF * hf:HALF * (hf + 1)]
    return ref.at[pl.ds(HALF * hf, HALF)]


def _row_tile(rows, last, itemsize=4, budget=2 * 1024 * 1024):
    return _tile(rows, tuple(t for t in (2048, 1024, 512, 256, 128, 64, 32, 16) if t * last * itemsize <= budget))


def _rs_add(g, got, c, name):
    last = g.shape[-1]
    g2, got2 = g.reshape(-1, last), got.reshape(-1, last)
    half = got2.shape[0]
    tr = _row_tile(half, last)
    nb = half // tr

    def body(c_ref, g_ref, got_ref, o_ref):
        o_ref[...] = (g_ref[...] + got_ref[...]).astype(BF16)

    out = pl.pallas_call(
        body, name=name,
        grid_spec=pltpu.PrefetchScalarGridSpec(
            num_scalar_prefetch=1, grid=(nb,),
            in_specs=[pl.BlockSpec((tr, last), lambda i, c_ref: (c_ref[0] * nb + i, 0)),
                      pl.BlockSpec((tr, last), lambda i, c_ref: (i, 0))],
            out_specs=pl.BlockSpec((tr, last), lambda i, c_ref: (i, 0))),
        out_shape=jax.ShapeDtypeStruct((half, last), BF16),
        compiler_params=_cparams(("parallel",)),
    )(c.reshape(1).astype(jnp.int32), g2, got2)
    return out.reshape(got.shape)


def _rs_ici(ps):
    n = len(ps)

    def body(*refs):
        src, out = refs[:n], refs[n:2 * n]
        send_sems, recv_sems = refs[2 * n:]
        x, y, c, chips = _place()
        me = 2 * x + y
        sends = []
        for p in range(n):
            for j, (cx, cy) in enumerate(chips):
                cp = _remote(src[p].at[:, 2 * cx + cy], out[p].at[j], send_sems.at[3 * p + j], recv_sems.at[3 * p + j],
                             (cx, cy, c))
                cp.start()
                sends.append(cp)
        for j, (cx, cy) in enumerate(chips):
            for p in range(n):
                slot = out[p].at[j]
                _remote(slot, slot, send_sems.at[3 * p + j], recv_sems.at[3 * p + j], (cx, cy, c)).wait_recv()
        for cp in sends:
            cp.wait_send()

    return pl.pallas_call(
        body, name="rs_ici", in_specs=[ANY] * n, out_specs=[ANY] * n,
        out_shape=[jax.ShapeDtypeStruct((3, a.shape[0]) + a.shape[2:], a.dtype) for a in ps],
        scratch_shapes=[pltpu.SemaphoreType.DMA((3 * n,)), pltpu.SemaphoreType.DMA((3 * n,))],
    )(*ps)


def _rs_sum(own, got, chip, core, name):
    nl, last = own.shape[0], own.shape[-1]
    shard = own.shape[2:]
    own4 = own.reshape(nl, N_CHIPS, -1, last)
    rows = own4.shape[2]
    got4 = got.reshape(3, nl, rows, last)
    tr = _row_tile(rows, last, budget=1024 * 1024)

    def body(k_ref, c_ref, own_ref, got_ref, o_ref):
        acc = own_ref[...].astype(F32)
        for j in range(3):
            acc = acc + got_ref[j].astype(F32)
        o_ref[...] = acc

    out = pl.pallas_call(
        body, name=name,
        grid_spec=pltpu.PrefetchScalarGridSpec(
            num_scalar_prefetch=2, grid=(nl, rows // tr),
            in_specs=[pl.BlockSpec((None, None, tr, last), lambda l, i, k_ref, c_ref: (l, k_ref[0], i, 0)),
                      pl.BlockSpec((3, None, tr, last), lambda l, i, k_ref, c_ref: (0, l, i, 0))],
            out_specs=pl.BlockSpec((None, tr, last), lambda l, i, k_ref, c_ref: (nl * c_ref[0] + l, i, 0))),
        out_shape=jax.ShapeDtypeStruct((2 * nl, rows, last), F32),
        compiler_params=_cparams(("parallel", "parallel")),
    )(chip.reshape(1).astype(jnp.int32), core.reshape(1).astype(jnp.int32), own4, got4)
    return out.reshape((2 * nl,) + shard)


def _rs_share(arrs):
    n = len(arrs)

    def body(*refs):
        out = refs[n:2 * n]
        send_sems, recv_sems = refs[2 * n:]
        x, y, c, _ = _place()
        sibling = (x, y, 1 - c)
        cps = [_remote(_layer_half(out[p], c), _layer_half(out[p], c), send_sems.at[p], recv_sems.at[p], sibling)
               for p in range(n)]
        for cp in cps:
            cp.start()
        for p, cp in enumerate(cps):
            cp.wait_send()
            other = _layer_half(out[p], 1 - c)
            _remote(other, other, send_sems.at[p], recv_sems.at[p], sibling).wait_recv()

    return pl.pallas_call(
        body, name="rs_share", in_specs=[ANY] * n, out_specs=[ANY] * n,
        out_shape=[jax.ShapeDtypeStruct(a.shape, a.dtype) for a in arrs],
        input_output_aliases={p: p for p in range(n)},
        scratch_shapes=[pltpu.SemaphoreType.DMA((n,)), pltpu.SemaphoreType.DMA((n,))],
    )(*arrs)


def _reduce_scatter(gs, chip, core):
    names = list(gs)
    got = _swap_with_sibling([gs[n] for n in names], lambda ref, c: _layer_half(ref, 1 - c), "rs_swap")
    chip_sums = [_rs_add(gs[n], a, core, "rs_add_" + n) for n, a in zip(names, got)]
    landed = _rs_ici(chip_sums)
    mine = [_rs_sum(p, a, chip, core, "rs_sum_" + n) for n, p, a in zip(names, chip_sums, landed)]
    return dict(zip(names, _rs_share(mine)))


def _adamw(w, g, m, v, name):
    shape = w.shape
    if len(shape) == 2:
        shape3 = (1,) + shape
    else:
        shape3 = (-1,) + shape[-2:]
    w2, g2, m2, v2 = (a.reshape(shape3) for a in (w, g, m, v))
    lead, rows, last = w2.shape
    tr = _tile(rows, tuple(t for t in (1024, 512, 256, 128, 64, 32, 16, 8) if t * last * 4 <= 2 * 1024 * 1024) + (rows,))

    def body(w_ref, g_ref, m_ref, v_ref, d_ref, nm_ref, nv_ref):
        gv = g_ref[...]
        mn = ADAM_B1 * m_ref[...] + (1.0 - ADAM_B1) * gv
        vn = ADAM_B2 * v_ref[...] + (1.0 - ADAM_B2) * (gv * gv)
        m_hat = mn / (1.0 - ADAM_B1 ** ADAM_STEP)
        v_hat = vn / (1.0 - ADAM_B2 ** ADAM_STEP)
        d_ref[...] = -ADAM_LR * (m_hat / (jnp.sqrt(v_hat) + ADAM_EPS) + ADAM_WD * w_ref[...])
        nm_ref[...] = mn
        nv_ref[...] = vn

    blk = pl.BlockSpec((None, tr, last), lambda a, i: (a, i, 0))
    outs = pl.pallas_call(
        body, name=name, grid=(lead, rows // tr), in_specs=[blk] * 4, out_specs=[blk] * 3,
        out_shape=[jax.ShapeDtypeStruct((lead, rows, last), F32)] * 3,
        compiler_params=_cparams(("parallel", "parallel")),
    )(w2, g2, m2, v2)
    return tuple(o.reshape(shape) for o in outs)


SHARD_AXIS = {"conv_w": 2, "meta_tokens": 1}
SMALL_SHARDED = ("conv_w", "meta_tokens")
REPLICATED = ("mix_norm_w", "b_gate", "pool_scale", "conv_b", "dt_bias", "a_log", "d_skip", "ssd_norm_w", "mlp_norm_w",
              "final_norm_w")


def _flatten(arrs, dtype, row_mult):
    flat = jnp.concatenate([a.astype(dtype).reshape(-1) for a in arrs])
    n = flat.shape[0]
    rows = -(-n // (LANES * row_mult)) * row_mult
    return jnp.pad(flat, (0, rows * LANES - n)).reshape(rows, LANES)


def _unflatten(flat2d, shapes):
    flat = flat2d.reshape(-1)
    out, off = [], 0
    for sh in shapes:
        n = 1
        for d in sh:
            n *= d
        out.append(flat[off:off + n].reshape(sh))
        off += n
    return out


def kernel(x, meta_tokens, mix_norm_w, w_in, b_gate, pool_w_group, pool_scale, w_pool_up, conv_w, conv_b, dt_bias, a_log, d_skip, ssd_norm_w, w_ssd_out, w_o, mlp_norm_w, w_ff1, w_ff2, final_norm_w, loss_target, m_meta_tokens, m_mix_norm_w, m_w_in, m_b_gate, m_pool_w_group, m_pool_scale, m_w_pool_up, m_conv_w, m_conv_b, m_dt_bias, m_a_log, m_d_skip, m_ssd_norm_w, m_w_ssd_out, m_w_o, m_mlp_norm_w, m_w_ff1, m_w_ff2, m_final_norm_w, v_meta_tokens, v_mix_norm_w, v_w_in, v_b_gate, v_pool_w_group, v_pool_scale, v_w_pool_up, v_conv_w, v_conv_b, v_dt_bias, v_a_log, v_d_skip, v_ssd_norm_w, v_w_ssd_out, v_w_o, v_mlp_norm_w, v_w_ff1, v_w_ff2, v_final_norm_w):
    names = ("meta_tokens",) + PER_LAYER + ("final_norm_w",)
    par = dict(meta_tokens=meta_tokens, mix_norm_w=mix_norm_w, w_in=w_in, b_gate=b_gate, pool_w_group=pool_w_group,
               pool_scale=pool_scale, w_pool_up=w_pool_up, conv_w=conv_w, conv_b=conv_b, dt_bias=dt_bias, a_log=a_log,
               d_skip=d_skip, ssd_norm_w=ssd_norm_w, w_ssd_out=w_ssd_out, w_o=w_o, mlp_norm_w=mlp_norm_w, w_ff1=w_ff1,
               w_ff2=w_ff2, final_norm_w=final_norm_w)
    mom = dict(meta_tokens=m_meta_tokens, mix_norm_w=m_mix_norm_w, w_in=m_w_in, b_gate=m_b_gate, pool_w_group=m_pool_w_group,
               pool_scale=m_pool_scale, w_pool_up=m_w_pool_up, conv_w=m_conv_w, conv_b=m_conv_b, dt_bias=m_dt_bias,
               a_log=m_a_log, d_skip=m_d_skip, ssd_norm_w=m_ssd_norm_w, w_ssd_out=m_w_ssd_out, w_o=m_w_o,
               mlp_norm_w=m_mlp_norm_w, w_ff1=m_w_ff1, w_ff2=m_w_ff2, final_norm_w=m_final_norm_w)
    var = dict(meta_tokens=v_meta_tokens, mix_norm_w=v_mix_norm_w, w_in=v_w_in, b_gate=v_b_gate, pool_w_group=v_pool_w_group,
               pool_scale=v_pool_scale, w_pool_up=v_w_pool_up, conv_w=v_conv_w, conv_b=v_conv_b, dt_bias=v_dt_bias,
               a_log=v_a_log, d_skip=v_d_skip, ssd_norm_w=v_ssd_norm_w, w_ssd_out=v_w_ssd_out, w_o=v_w_o,
               mlp_norm_w=v_mlp_norm_w, w_ff1=v_w_ff1, w_ff2=v_w_ff2, final_norm_w=v_final_norm_w)
    chip = 2 * lax.axis_index("x") + lax.axis_index("y")
    core = lax.axis_index("c")

    big = dict(zip(BIG, _gather_big([par[n].astype(BF16) for n in BIG])))
    small = {n: par[n] for n in REPLICATED}
    small_shapes = [par[n].shape for n in SMALL_SHARDED]
    got_small = _exchange_small(_flatten([par[n] for n in SMALL_SHARDED], F32, 8), False, "gather_small")
    pieces = [_unflatten(got_small[2 * k], small_shapes) for k in range(N_CHIPS)]
    for j, n in enumerate(SMALL_SHARDED):
        small[n] = jnp.concatenate([pieces[k][j] for k in range(N_CHIPS)], axis=SHARD_AXIS[n])

    loss, dx, gbig, grads = _local_step(x[0], loss_target[0], big, small)
    loss = lax.psum(loss, ("x", "y", "c"))

    small_names = REPLICATED + SMALL_SHARDED
    red = _exchange_small(_flatten([grads[n] for n in small_names], F32, 8), True, "allreduce_small")
    gsum = dict(zip(small_names, _unflatten(red, [grads[n].shape for n in small_names])))
    for n in SMALL_SHARDED:
        width = par[n].shape[SHARD_AXIS[n]]
        gsum[n] = lax.dynamic_slice_in_dim(gsum[n], chip * width, width, axis=SHARD_AXIS[n])

    gsum.update(_reduce_scatter(gbig, chip, core))

    delta, new_m, new_v = {}, {}, {}
    for n in BIG:
        delta[n], new_m[n], new_v[n] = _adamw(par[n], gsum[n], mom[n], var[n], "adamw_" + n)
    rest = [n for n in names if n not in BIG]
    shapes = [par[n].shape for n in rest]
    packed = [_flatten([d[n] for n in rest], F32, 8) for d in (par, gsum, mom, var)]
    for d, flat in zip((delta, new_m, new_v), _adamw(*packed, "adamw_small")):
        d.update(zip(rest, _unflatten(flat, shapes)))

    order = ("meta_tokens", "mix_norm_w", "w_in", "b_gate", "pool_w_group", "pool_scale", "w_pool_up", "conv_w", "conv_b",
             "dt_bias", "a_log", "d_skip", "ssd_norm_w", "w_ssd_out", "w_o", "mlp_norm_w", "w_ff1", "w_ff2", "final_norm_w")
    return (loss, dx[None], *[gsum[n] for n in order], *[delta[n] for n in order], *[new_m[n] for n in order],
            *[new_v[n] for n in order])
```

```python
import functools

import jax
import jax.numpy as jnp
from jax import lax
from jax.experimental import pallas as pl
from jax.experimental.pallas import tpu as pltpu

F32 = jnp.float32
BF16 = jnp.bfloat16

D_MODEL = 1024
DEPTH = 4
N_META = 16
N_PAD = 112
ROW_X = N_PAD + N_META
POOL_WINDOWS = (2, 4, 8, 16)
POOL_GDIM = 256
D_INNER = 2048
N_HEADS = 32
HEAD_DIM = 64
N_GROUPS = 8
HEADS_PER_GROUP = 4
GROUP_W = HEADS_PER_GROUP * HEAD_DIM
D_STATE = 128
CHUNK = 128
D_XBC = 4096
D_FF = 4096
EPS = 1e-5
OFF_Z, OFF_XBC, OFF_DT, OFF_GATE, IN_COLS = 1024, 3072, 7168, 7200, 9248
PZ, PGATE, PPOOL, PXBC, PCOLS = 0, 2048, 4096, 5120, 9216
DT_PAD = 128

ADAM_LR, ADAM_B1, ADAM_B2, ADAM_EPS, ADAM_WD, ADAM_STEP = 0.001, 0.9, 0.999, 1e-08, 0.01, 10

VMEM_LIMIT = 56 * 1024 * 1024

_NN = (((1,), (0,)), ((), ()))
_NT = (((1,), (1,)), ((), ()))
_TN = (((0,), (0,)), ((), ()))


def _dot(a, b, dn=_NN):
    return lax.dot_general(a, b, dn, preferred_element_type=F32)


def _cparams(sem):
    return pltpu.CompilerParams(dimension_semantics=sem, vmem_limit_bytes=VMEM_LIMIT)


def _tile(n, cands):
    for c in cands:
        if n % c == 0:
            return c
    raise ValueError(f"no tile for {n} in {cands}")


def _sigmoid(x):
    return 1.0 / (1.0 + jnp.exp(-x))


def _softplus(x):
    return jnp.maximum(x, 0.0) + jnp.log(1.0 + jnp.exp(-jnp.abs(x)))


def _bs(shape, fn):
    return pl.BlockSpec(shape, fn)


class _Carry:
    def __init__(self, ins, outs, n_sems, make):
        self.ins, self.outs, self.n_sems, self.make = list(ins), list(outs), n_sems, make

    def specs(self):
        hbm = pl.BlockSpec(memory_space=pl.ANY)
        return [hbm] * len(self.ins), [hbm] * len(self.outs), [pltpu.SemaphoreType.DMA((self.n_sems,))] * 2

    def at_edges(self, grid, in_refs, out_refs, sems):
        ids = [pl.program_id(a) for a in range(len(grid))]
        first = functools.reduce(jnp.logical_and, [i == 0 for i in ids])
        last = functools.reduce(jnp.logical_and, [i == n - 1 for i, n in zip(ids, grid)])

        def begin():
            @pl.when(first)
            def _():
                for cp in self.make(in_refs, out_refs, *sems):
                    cp.start()

        def end():
            @pl.when(last)
            def _():
                for cp in self.make(in_refs, out_refs, *sems):
                    cp.wait()

        return begin, end


def _mm(name, dn, grid, a, a_spec, b, b_spec, outs, *, extras=(), epilogue=None, carry=None):
    nk = grid[2]
    ne, no = len(extras), len(outs)
    blk = tuple(d for d in outs[0][2].block_shape if d is not None)
    c_in, c_out, c_sem = carry.specs() if carry is not None else ([], [], [])
    nci, nco = len(c_in), len(c_out)

    def body(a_ref, b_ref, *rest):
        e_refs, o_refs = rest[:ne], rest[ne + nci:ne + nci + no]
        if carry is not None:
            begin, end = carry.at_edges(grid, rest[ne:ne + nci], rest[ne + nci + no:ne + nci + no + nco], rest[-2:])
            begin()
        p = _dot(a_ref[...].astype(BF16), b_ref[...].astype(BF16), dn)

        def finish(acc):
            outs = epilogue(acc, *[e[...] for e in e_refs]) if epilogue is not None else (acc,)
            for o, v in zip(o_refs, outs):
                o[...] = v.astype(o.dtype)

        if nk == 1:
            finish(p)
        else:
            acc_ref = rest[ne + nci + no + nco]
            kk = pl.program_id(2)

            @pl.when(kk == 0)
            def _():
                acc_ref[...] = p

            @pl.when(kk > 0)
            def _():
                acc_ref[...] += p

            @pl.when(kk == nk - 1)
            def _():
                finish(acc_ref[...])
        if carry is not None:
            end()

    res = pl.pallas_call(
        body,
        name=name,
        grid=grid,
        in_specs=[a_spec, b_spec, *[s for _, s in extras]] + c_in,
        out_specs=[o[2] for o in outs] + c_out,
        out_shape=[jax.ShapeDtypeStruct(o[0], o[1]) for o in outs] + (carry.outs if carry is not None else []),
        scratch_shapes=([pltpu.VMEM(blk, F32)] if nk > 1 else []) + c_sem,
        compiler_params=_cparams(("arbitrary",) * 3 if carry is not None else ("parallel", "parallel", "arbitrary")),
    )(a, b, *[e for e, _ in extras], *(carry.ins if carry is not None else []))
    if carry is not None:
        return res[:no], res[no:]
    return res[0] if no == 1 else res


def _norm_fwd(h, w, name):
    t, d = h.shape
    tm = _tile(t, (1056, 384, 128))

    def body(h_ref, w_ref, u_ref):
        x = h_ref[...]
        r = lax.rsqrt(jnp.mean(x * x, axis=-1, keepdims=True) + EPS)
        u_ref[...] = (x * r * w_ref[...]).astype(BF16)

    return pl.pallas_call(
        body, name=name, grid=(t // tm,),
        in_specs=[pl.BlockSpec((tm, d), lambda i: (i, 0)), pl.BlockSpec((1, d), lambda i: (0, 0))],
        out_specs=pl.BlockSpec((tm, d), lambda i: (i, 0)),
        out_shape=jax.ShapeDtypeStruct((t, d), BF16),
        compiler_params=_cparams(("parallel",)),
    )(h, w.reshape(1, d))


def _norm_bwd(dres, du, h, w, name):
    t, d = h.shape
    tm = _tile(t, (528, 384, 128))

    def body(dres_ref, du_ref, h_ref, w_ref, dh_ref, dw_ref):
        x = h_ref[...]
        r = lax.rsqrt(jnp.mean(x * x, axis=-1, keepdims=True) + EPS)
        xhat = x * r
        du_v = du_ref[...]
        g = du_v * w_ref[...]
        dh_ref[...] = dres_ref[...] + r * (g - xhat * jnp.mean(g * xhat, axis=-1, keepdims=True))

        @pl.when(pl.program_id(0) == 0)
        def _():
            dw_ref[...] = jnp.zeros_like(dw_ref)

        dw_ref[...] += jnp.sum(du_v * xhat, axis=0, keepdims=True)

    row = pl.BlockSpec((tm, d), lambda i: (i, 0))
    vec = pl.BlockSpec((1, d), lambda i: (0, 0))
    dh, dw = pl.pallas_call(
        body, name=name, grid=(t // tm,),
        in_specs=[row, row, row, vec], out_specs=[row, vec],
        out_shape=[jax.ShapeDtypeStruct((t, d), F32), jax.ShapeDtypeStruct((1, d), F32)],
        compiler_params=_cparams(("arbitrary",)),
    )(dres, du, h, w.reshape(1, d))
    return dh, dw.reshape(d)


def _loss_head(h, w, tgt):
    t, d = h.shape
    tm = CHUNK
    nb = ROW_X // tm

    def body(h_ref, w_ref, t_ref, loss_ref, dh_ref, dw_ref):
        i = pl.program_id(0)
        x = h_ref[...]
        r = lax.rsqrt(jnp.mean(x * x, axis=-1, keepdims=True) + EPS)
        xhat = x * r
        wv = w_ref[...]
        live = i >= nb
        err = jnp.where(live, xhat * wv - t_ref[...], 0.0)
        dout = err * (1.0 / d)
        g = dout * wv
        dh_ref[...] = r * (g - xhat * jnp.mean(g * xhat, axis=-1, keepdims=True))

        @pl.when(i == 0)
        def _():
            loss_ref[...] = jnp.zeros_like(loss_ref)
            dw_ref[...] = jnp.zeros_like(dw_ref)

        loss_ref[...] += 0.5 * jnp.sum(jnp.sum(err * err, axis=-1, keepdims=True), axis=0, keepdims=True) * (1.0 / d)
        dw_ref[...] += jnp.sum(dout * xhat, axis=0, keepdims=True)

    row = pl.BlockSpec((tm, d), lambda i: (i, 0))
    vec = pl.BlockSpec((1, d), lambda i: (0, 0))
    loss, dh, dw = pl.pallas_call(
        body, name="loss_head", grid=(t // tm,),
        in_specs=[row, vec, pl.BlockSpec((tm, d), lambda i: (jnp.maximum(i - nb, 0), 0))],
        out_specs=[pl.BlockSpec((1, 1), lambda i: (0, 0)), row, vec],
        out_shape=[jax.ShapeDtypeStruct((1, 1), F32), jax.ShapeDtypeStruct((t, d), F32), jax.ShapeDtypeStruct((1, d), F32)],
        compiler_params=_cparams(("arbitrary",)),
    )(h, w.reshape(1, d), tgt)
    return loss[0, 0], dh, dw.reshape(d)


def _gate_fwd(proj, b_gate, y_pool, y_ssd):
    t = proj.shape[0]
    d = D_MODEL
    tm = _tile(t, (528, 384, 128))

    def body(gp_ref, gs_ref, bp_ref, bs_ref, yp_ref, ys_ref, o_ref):
        gp = _sigmoid(gp_ref[...] + bp_ref[...])
        gs = _sigmoid(gs_ref[...] + bs_ref[...])
        o_ref[...] = (gp * yp_ref[...] + gs * ys_ref[...]).astype(BF16)

    row = pl.BlockSpec((tm, d), lambda i: (i, 0))
    return pl.pallas_call(
        body, name="gate_fwd", grid=(t // tm,),
        in_specs=[pl.BlockSpec((tm, d), lambda i: (i, PGATE // d)), pl.BlockSpec((tm, d), lambda i: (i, PGATE // d + 1)),
                  pl.BlockSpec((1, d), lambda i: (0, 0)), pl.BlockSpec((1, d), lambda i: (0, 1)), row, row],
        out_specs=row, out_shape=jax.ShapeDtypeStruct((t, d), BF16),
        compiler_params=_cparams(("parallel",)),
    )(proj, proj, b_gate.reshape(1, 2 * d), b_gate.reshape(1, 2 * d), y_pool, y_ssd)


def _gate_bwd(dmerged, proj, b_gate, y_pool, y_ssd):
    t = proj.shape[0]
    d = D_MODEL
    tm = _tile(t, (384, 128))

    def body(dm_ref, gp_ref, gs_ref, bp_ref, bs_ref, yp_ref, ys_ref, dyp_ref, dys_ref, dg_ref, db_ref):
        dm = dm_ref[...]
        gp = _sigmoid(gp_ref[...] + bp_ref[...])
        gs = _sigmoid(gs_ref[...] + bs_ref[...])
        dyp_ref[...] = (dm * gp).astype(BF16)
        dys_ref[...] = (dm * gs).astype(BF16)
        dgp = dm * yp_ref[...] * gp * (1.0 - gp)
        dgs = dm * ys_ref[...] * gs * (1.0 - gs)
        dg_ref[:, :d] = dgp.astype(BF16)
        dg_ref[:, d:] = dgs.astype(BF16)

        @pl.when(pl.program_id(0) == 0)
        def _():
            db_ref[...] = jnp.zeros_like(db_ref)

        db_ref[:, :d] += jnp.sum(dgp, axis=0, keepdims=True)
        db_ref[:, d:] += jnp.sum(dgs, axis=0, keepdims=True)

    row = pl.BlockSpec((tm, d), lambda i: (i, 0))
    dyp, dys, dg, db = pl.pallas_call(
        body, name="gate_bwd", grid=(t // tm,),
        in_specs=[row, pl.BlockSpec((tm, d), lambda i: (i, PGATE // d)), pl.BlockSpec((tm, d), lambda i: (i, PGATE // d + 1)),
                  pl.BlockSpec((1, d), lambda i: (0, 0)), pl.BlockSpec((1, d), lambda i: (0, 1)), row, row],
        out_specs=[row, row, pl.BlockSpec((tm, 2 * d), lambda i: (i, PGATE // (2 * d))),
                   pl.BlockSpec((1, 2 * d), lambda i: (0, 0))],
        out_shape=[jax.ShapeDtypeStruct((t, d), BF16), jax.ShapeDtypeStruct((t, d), BF16),
                   jax.ShapeDtypeStruct((t, PCOLS), BF16), jax.ShapeDtypeStruct((1, 2 * d), F32)],
        compiler_params=_cparams(("arbitrary",)),
    )(dmerged, proj, proj, b_gate.reshape(1, 2 * d), b_gate.reshape(1, 2 * d), y_pool, y_ssd)
    return dyp, dys, dg, db.reshape(2 * d)


POOL_HALO = 16


def _pool_counts(row0, n, win):
    pos1 = row0 + lax.broadcasted_iota(jnp.int32, (n, 1), 0) - (N_PAD - 1)
    return jnp.clip(pos1, 1, win).astype(F32)


N_CHIPS = 4
WG_ROWS = POOL_GDIM // N_CHIPS
WUP_ROWS = D_MODEL // N_CHIPS


def _group_w(wg_ref, g):
    return jnp.concatenate([wg_ref[k, g] for k in range(N_CHIPS)], axis=0)


def _pool_w_specs(layer):
    return [pl.BlockSpec((None, N_CHIPS, len(POOL_WINDOWS), WG_ROWS, POOL_GDIM), lambda i: (layer, 0, 0, 0, 0)),
            pl.BlockSpec((None, N_CHIPS, WUP_ROWS, D_MODEL), lambda i: (layer, 0, 0, 0))]


def _pool_fwd(proj, wg, scale, wup, layer):
    t = proj.shape[0]
    d = D_MODEL
    tm = _tile(t, (384, 128))
    hb = tm // POOL_HALO

    def body(u_ref, halo_ref, sc_ref, wg_ref, wup_ref, pooled_ref, yg_ref, ypm_ref, yp_ref):
        i = pl.program_id(0)
        x = u_ref[...]
        halo = jnp.where(i > 0, halo_ref[...], 0.0)
        xc = jnp.concatenate([halo, x], axis=0)
        for g, win in enumerate(POOL_WINDOWS):
            sl = slice(g * POOL_GDIM, (g + 1) * POOL_GDIM)
            s = xc[:, sl]
            k = 1
            while k < win:
                s = s + pltpu.roll(s, k, axis=0)
                k *= 2
            pooled = s[POOL_HALO:] / _pool_counts(i * tm, tm, win) - x[:, sl]
            pb = pooled.astype(BF16)
            pooled_ref[:, sl] = pb
            yg_ref[:, sl] = _dot(pb, _group_w(wg_ref, g))
        ypm = (yg_ref[...] * sc_ref[...]).astype(BF16)
        ypm_ref[...] = ypm
        acc = _dot(ypm[:, :WUP_ROWS], wup_ref[0])
        for k in range(1, N_CHIPS):
            acc = acc + _dot(ypm[:, k * WUP_ROWS:(k + 1) * WUP_ROWS], wup_ref[k])
        yp_ref[...] = acc

    row = pl.BlockSpec((tm, d), lambda i: (i, 0))
    return pl.pallas_call(
        body, name=f"pool_fwd_{layer}", grid=(t // tm,),
        in_specs=[pl.BlockSpec((tm, d), lambda i: (i, PPOOL // d)),
                  pl.BlockSpec((POOL_HALO, d), lambda i: (jnp.maximum(i * hb - 1, 0), PPOOL // d)),
                  pl.BlockSpec((1, d), lambda i: (0, 0))] + _pool_w_specs(layer),
        out_specs=[row, row, row, row],
        out_shape=[jax.ShapeDtypeStruct((t, d), BF16), jax.ShapeDtypeStruct((t, d), F32),
                   jax.ShapeDtypeStruct((t, d), BF16), jax.ShapeDtypeStruct((t, d), F32)],
        compiler_params=_cparams(("parallel",)),
    )(proj, proj, scale.reshape(1, d), wg, wup)


def _pool_bwd_a(dy_pool, yg, scale, wg, wup, layer):
    t, d = yg.shape
    tm = _tile(t, (384, 128))

    def body(dy_ref, yg_ref, sc_ref, wg_ref, wup_ref, q_ref, dyg_ref, dsc_ref):
        dy = dy_ref[...]
        dypm = jnp.concatenate([_dot(dy, wup_ref[k], _NT) for k in range(N_CHIPS)], axis=1)

        @pl.when(pl.program_id(0) == 0)
        def _():
            dsc_ref[...] = jnp.zeros_like(dsc_ref)

        dsc_ref[...] += jnp.sum(dypm * yg_ref[...], axis=0, keepdims=True)
        dyg = (dypm * sc_ref[...]).astype(BF16)
        dyg_ref[...] = dyg
        for g in range(len(POOL_WINDOWS)):
            sl = slice(g * POOL_GDIM, (g + 1) * POOL_GDIM)
            q_ref[:, sl] = _dot(dyg[:, sl], _group_w(wg_ref, g), _NT)

    row = pl.BlockSpec((tm, d), lambda i: (i, 0))
    vec = pl.BlockSpec((1, d), lambda i: (0, 0))
    q, dyg, dsc = pl.pallas_call(
        body, name=f"pool_bwd_a_{layer}", grid=(t // tm,),
        in_specs=[row, row, vec] + _pool_w_specs(layer),
        out_specs=[row, row, vec],
        out_shape=[jax.ShapeDtypeStruct((t, d), F32), jax.ShapeDtypeStruct((t, d), BF16), jax.ShapeDtypeStruct((1, d), F32)],
        compiler_params=_cparams(("arbitrary",)),
    )(dy_pool, yg, scale.reshape(1, d), wg, wup)
    return q, dyg, dsc.reshape(d)


def _pool_bwd_b(q, dproj):
    t, d = q.shape
    tm = _tile(t, (384, 128))
    hb = tm // POOL_HALO
    nt = t // tm
    n = tm + POOL_HALO

    def body(q_ref, halo_ref, _, o_ref):
        i = pl.program_id(0)
        qv = q_ref[...]
        halo = jnp.where(i < nt - 1, halo_ref[...], 0.0)
        qc = jnp.concatenate([qv, halo], axis=0)
        for g, win in enumerate(POOL_WINDOWS):
            sl = slice(g * POOL_GDIM, (g + 1) * POOL_GDIM)
            s = qc[:, sl] / _pool_counts(i * tm, n, win)
            k = 1
            while k < win:
                s = s + pltpu.roll(s, n - k, axis=0)
                k *= 2
            o_ref[:, sl] = (s[:tm] - qv[:, sl]).astype(BF16)

    row = pl.BlockSpec((tm, d), lambda i: (i, 0))
    return pl.pallas_call(
        body, name="pool_bwd_b", grid=(nt,),
        in_specs=[row, pl.BlockSpec((POOL_HALO, d), lambda i: (jnp.minimum((i + 1) * hb, t // POOL_HALO - 1), 0)),
                  pl.BlockSpec(memory_space=pl.ANY)],
        out_specs=pl.BlockSpec((tm, d), lambda i: (i, PPOOL // d)), out_shape=jax.ShapeDtypeStruct(dproj.shape, BF16),
        input_output_aliases={2: 0},
        compiler_params=_cparams(("parallel",)),
    )(q, q, dproj)


def _pool_dwg(pooled, dyg, layer):
    t, d = pooled.shape
    tk = _tile(t, (1056, 384, 128))
    nk = t // tk
    gd = POOL_GDIM
    ng = d // gd

    def body(p_ref, g_ref, o_ref):
        @pl.when(pl.program_id(1) == 0)
        def _():
            o_ref[...] = jnp.zeros_like(o_ref)

        part = _dot(p_ref[...], g_ref[...], _TN)
        for k in range(N_CHIPS):
            o_ref[k] += part[k * WG_ROWS:(k + 1) * WG_ROWS]

    blk = pl.BlockSpec((tk, gd), lambda g, k: (k, g))
    return pl.pallas_call(
        body, name=f"pool_dwg_{layer}", grid=(ng, nk), in_specs=[blk, blk],
        out_specs=pl.BlockSpec((N_CHIPS, None, WG_ROWS, gd), lambda g, k: (0, g, 0, 0)),
        out_shape=jax.ShapeDtypeStruct((N_CHIPS, ng, WG_ROWS, gd), F32),
        compiler_params=_cparams(("parallel", "arbitrary")),
    )(pooled, dyg)


CONV_W = 4
CONV_HALO = 8
XBC_BLK = PXBC // 1024


def _conv_fwd(proj, conv_w, conv_b):
    t = proj.shape[0]
    cw = 1024
    tm = _tile(t, (1056, 384, 128))
    hb = tm // CONV_HALO

    def body(x_ref, halo_ref, w_ref, b_ref, o_ref):
        i = pl.program_id(1)
        x = x_ref[...]
        halo = jnp.where(i > 0, halo_ref[...], 0.0)
        xc = jnp.concatenate([halo, x], axis=0)
        w = w_ref[...]
        acc = b_ref[...] + x * w[CONV_W - 1:CONV_W, :]
        for k in range(CONV_W - 1):
            acc = acc + pltpu.roll(xc, CONV_W - 1 - k, axis=0)[CONV_HALO:] * w[k:k + 1, :]
        row = i * tm + lax.broadcasted_iota(jnp.int32, (tm, 1), 0)
        o_ref[...] = jnp.where(row >= N_PAD, acc * _sigmoid(acc), 0.0)

    return pl.pallas_call(
        body, name="conv_fwd", grid=(D_XBC // cw, t // tm),
        in_specs=[pl.BlockSpec((tm, cw), lambda j, i: (i, XBC_BLK + j)),
                  pl.BlockSpec((CONV_HALO, cw), lambda j, i: (jnp.maximum(i * hb - 1, 0), XBC_BLK + j)),
                  pl.BlockSpec((CONV_W, cw), lambda j, i: (0, j)), pl.BlockSpec((1, cw), lambda j, i: (0, j))],
        out_specs=pl.BlockSpec((tm, cw), lambda j, i: (i, j)),
        out_shape=jax.ShapeDtypeStruct((t, D_XBC), F32),
        compiler_params=_cparams(("parallel", "parallel")),
    )(proj, proj, conv_w, conv_b.reshape(1, D_XBC))


def _conv_bwd(dxa, coff, proj, conv_w, conv_b, dproj):
    t, ncols = dxa.shape
    cw = 1024
    tm = _tile(t, (528, 384, 128))
    hb = tm // CONV_HALO
    nt = t // tm
    n = tm + 2 * CONV_HALO

    def body(d_ref, dn_ref, xp_ref, x_ref, xn_ref, w_ref, b_ref, _, o_ref, dw_ref, db_ref):
        i = pl.program_id(1)
        last = i == nt - 1
        xf = jnp.concatenate([jnp.where(i > 0, xp_ref[...], 0.0), x_ref[...], jnp.where(last, 0.0, xn_ref[...])], axis=0)
        df = jnp.concatenate([jnp.zeros((CONV_HALO, cw), F32), d_ref[...], jnp.where(last, 0.0, dn_ref[...])], axis=0)
        w = w_ref[...]
        sh = [pltpu.roll(xf, CONV_W - 1 - k, axis=0) if k < CONV_W - 1 else xf for k in range(CONV_W)]
        xc = b_ref[...]
        for k in range(CONV_W):
            xc = xc + sh[k] * w[k:k + 1, :]
        sig = _sigmoid(xc)
        row = i * tm - CONV_HALO + lax.broadcasted_iota(jnp.int32, (n, 1), 0)
        dxc = jnp.where((row >= N_PAD) & (row < t), df * (sig * (1.0 + xc * (1.0 - sig))), 0.0)
        acc = dxc * w[CONV_W - 1:CONV_W, :]
        for k in range(CONV_W - 1):
            acc = acc + pltpu.roll(dxc, n - (CONV_W - 1 - k), axis=0) * w[k:k + 1, :]
        o_ref[...] = acc[CONV_HALO:CONV_HALO + tm].astype(BF16)

        @pl.when(i == 0)
        def _():
            dw_ref[...] = jnp.zeros_like(dw_ref)
            db_ref[...] = jnp.zeros_like(db_ref)

        dm = dxc[CONV_HALO:CONV_HALO + tm]
        db_ref[...] += jnp.sum(dm, axis=0, keepdims=True)
        for k in range(CONV_W):
            dw_ref[k:k + 1, :] += jnp.sum(dm * sh[k][CONV_HALO:CONV_HALO + tm], axis=0, keepdims=True)

    def xspec(rows, fn):
        return pl.BlockSpec((rows, cw), lambda j, i: (fn(i), XBC_BLK + coff + j))

    prev = lambda i: jnp.maximum(i * hb - 1, 0)
    nxt = lambda i: jnp.minimum((i + 1) * hb, t // CONV_HALO - 1)
    dxbc, dw, db = pl.pallas_call(
        body, name=f"conv_bwd_{coff}", grid=(ncols // cw, nt),
        in_specs=[pl.BlockSpec((tm, cw), lambda j, i: (i, j)), pl.BlockSpec((CONV_HALO, cw), lambda j, i: (nxt(i), j)),
                  xspec(CONV_HALO, prev), xspec(tm, lambda i: i), xspec(CONV_HALO, nxt),
                  pl.BlockSpec((CONV_W, cw), lambda j, i: (0, coff + j)), pl.BlockSpec((1, cw), lambda j, i: (0, coff + j)),
                  pl.BlockSpec(memory_space=pl.ANY)],
        out_specs=[pl.BlockSpec((tm, cw), lambda j, i: (i, XBC_BLK + coff + j)), pl.BlockSpec((CONV_W, cw), lambda j, i: (0, j)),
                   pl.BlockSpec((1, cw), lambda j, i: (0, j))],
        out_shape=[jax.ShapeDtypeStruct(dproj.shape, BF16), jax.ShapeDtypeStruct((CONV_W, ncols), F32),
                   jax.ShapeDtypeStruct((1, ncols), F32)],
        input_output_aliases={7: 0},
        compiler_params=_cparams(("parallel", "arbitrary")),
    )(dxa, dxa, proj, proj, proj, conv_w, conv_b.reshape(1, D_XBC), dproj)
    return dxbc, dw, db.reshape(ncols)


def _cumsum(x, axis, reverse=False):
    n = x.shape[axis]
    idx = lax.broadcasted_iota(jnp.int32, x.shape, axis)
    k = 1
    while k < n:
        if reverse:
            x = x + jnp.where(idx < n - k, pltpu.roll(x, n - k, axis=axis), 0.0)
        else:
            x = x + jnp.where(idx >= k, pltpu.roll(x, k, axis=axis), 0.0)
        k *= 2
    return x


def _head_masks():
    lane = lax.broadcasted_iota(jnp.int32, (1, GROUP_W), 1)
    return [(lane >= r * HEAD_DIM) & (lane < (r + 1) * HEAD_DIM) for r in range(HEADS_PER_GROUP)]


def _expand_heads(cols, hm):
    out = jnp.where(hm[0], cols[:, 0:1], 0.0)
    for r in range(1, HEADS_PER_GROUP):
        out = out + jnp.where(hm[r], cols[:, r:r + 1], 0.0)
    return out


def _ssd_decay(dt_raw, dt_bias, a_log):
    t = dt_raw.shape[0]

    def body(raw_ref, b_ref, al_ref, dt_ref, acs_ref, sig_ref):
        raw = raw_ref[...] + b_ref[...]
        rowid = pl.program_id(0) * CHUNK + lax.broadcasted_iota(jnp.int32, (CHUNK, 1), 0)
        dt = jnp.where(rowid >= N_PAD, _softplus(raw), 0.0)
        dt_ref[...] = dt
        acs_ref[...] = _cumsum(dt * -jnp.exp(al_ref[...]), 0)
        sig_ref[...] = _sigmoid(raw)

    blk = pl.BlockSpec((CHUNK, DT_PAD), lambda c: (c, 0))
    vec = pl.BlockSpec((1, DT_PAD), lambda c: (0, 0))
    pad = lambda v: jnp.pad(v, (0, DT_PAD - N_HEADS)).reshape(1, DT_PAD)
    dt, acs, sig = pl.pallas_call(
        body, name="ssd_decay", grid=(t // CHUNK,), in_specs=[blk, vec, vec], out_specs=[blk, blk, blk],
        out_shape=[jax.ShapeDtypeStruct((t, DT_PAD), F32)] * 3,
        compiler_params=_cparams(("parallel",)),
    )(dt_raw, pad(dt_bias), pad(a_log))
    grp = lambda v: v[:, :N_HEADS].reshape(t, N_GROUPS, HEADS_PER_GROUP).transpose(1, 0, 2)
    acs_g = grp(acs)
    return dict(dt=grp(dt), acs=acs_g, acs_rows=acs_g.transpose(0, 2, 1), sig=grp(sig),
                a_log=a_log.reshape(N_GROUPS, 1, HEADS_PER_GROUP))


def _ssd_q(dtc, acs4, acs_r):
    hm = _head_masks()
    dt_exp = _expand_heads(dtc, hm)
    acs = _expand_heads(acs4, hm)
    atot = acs[CHUNK - 1:CHUNK, :]
    return dict(dtc=dtc, hm=hm, dt_exp=dt_exp, acs=acs, acs_r=acs_r,
                ea=jnp.exp(acs), ds=jnp.exp(atot - acs), dec=jnp.exp(atot))


def _stack4(x):
    return jnp.concatenate([x] * HEADS_PER_GROUP, axis=0)


def _ssd_decay_stacks(q):
    hpg = HEADS_PER_GROUP
    a_col = jnp.concatenate([q["acs"][:, r * HEAD_DIM:r * HEAD_DIM + 1] for r in range(hpg)], axis=0)
    a_row = jnp.concatenate([jnp.broadcast_to(q["acs_r"][r:r + 1, :], (CHUNK, CHUNK)) for r in range(hpg)], axis=0)
    ri = lax.broadcasted_iota(jnp.int32, (hpg * CHUNK, CHUNK), 0) % CHUNK
    ci = lax.broadcasted_iota(jnp.int32, (hpg * CHUNK, CHUNK), 1)
    diff = a_col - a_row
    lm = jnp.exp(jnp.where(ri >= ci, diff, -jnp.inf))
    lt = jnp.exp(jnp.where(ri <= ci, -diff, -jnp.inf))
    return lm, lt


def _pick_heads(stacked, hm):
    out = jnp.where(hm[0], stacked[:CHUNK], 0.0)
    for r in range(1, HEADS_PER_GROUP):
        out = out + jnp.where(hm[r], stacked[r * CHUNK:(r + 1) * CHUNK], 0.0)
    return out


def _mask_heads(x, hm):
    return jnp.concatenate([jnp.where(hm[r], x, 0.0) for r in range(HEADS_PER_GROUP)], axis=0)


SSD_GP = 4


def _ssd_specs(cidx):
    gp, hpg = SSD_GP, HEADS_PER_GROUP
    return [
        pl.BlockSpec((CHUNK, gp * GROUP_W), lambda g, c: (cidx(c), g)),
        pl.BlockSpec((CHUNK, gp * D_STATE), lambda g, c: (cidx(c), D_INNER // (gp * D_STATE) + g)),
        pl.BlockSpec((CHUNK, gp * D_STATE), lambda g, c: (cidx(c), (D_INNER + 1024) // (gp * D_STATE) + g)),
        pl.BlockSpec((gp, CHUNK, hpg), lambda g, c: (g, cidx(c), 0)),
        pl.BlockSpec((gp, CHUNK, hpg), lambda g, c: (g, cidx(c), 0)),
        pl.BlockSpec((gp, hpg, CHUNK), lambda g, c: (g, 0, cidx(c))),
    ]


def _ssd_fwd(xa, dec):
    t = xa.shape[0]
    nc = t // CHUNK
    gp, gw, ds = SSD_GP, GROUP_W, D_STATE

    def body(xs_ref, b_ref, c_ref, dt_ref, acs_ref, acsr_ref, y_ref, prev_ref, st_ref):
        @pl.when(pl.program_id(1) == 0)
        def _():
            st_ref[...] = jnp.zeros_like(st_ref)

        for k in range(gp):
            q = _ssd_q(dt_ref[k], acs_ref[k], acsr_ref[k])
            xdt = xs_ref[:, k * gw:(k + 1) * gw] * q["dt_exp"]
            bm = b_ref[:, k * ds:(k + 1) * ds].astype(BF16)
            cm = c_ref[:, k * ds:(k + 1) * ds].astype(BF16)
            cb = _dot(cm, bm, _NT)
            st = st_ref[k]
            prev_ref[0, k] = st
            lm, _ = _ssd_decay_stacks(q)
            y_diag = _pick_heads(_dot((_stack4(cb) * lm).astype(BF16), xdt.astype(BF16)), q["hm"])
            y_ref[:, k * gw:(k + 1) * gw] = y_diag + _dot(cm, st.astype(BF16)) * q["ea"]
            st_ref[k] = q["dec"] * st + _dot(bm, (xdt * q["ds"]).astype(BF16), _TN)

    y, prev = pl.pallas_call(
        body, name="ssd_fwd", grid=(N_GROUPS // gp, nc),
        in_specs=_ssd_specs(lambda c: c),
        out_specs=[pl.BlockSpec((CHUNK, gp * gw), lambda g, c: (c, g)),
                   pl.BlockSpec((1, gp, ds, gw), lambda g, c: (c, g, 0, 0))],
        out_shape=[jax.ShapeDtypeStruct((t, D_INNER), F32), jax.ShapeDtypeStruct((nc, N_GROUPS, ds, gw), F32)],
        scratch_shapes=[pltpu.VMEM((gp, ds, gw), F32)],
        compiler_params=_cparams(("parallel", "arbitrary")),
    )(xa, xa, xa, dec["dt"], dec["acs"], dec["acs_rows"])
    return y, prev


def _ssd_bwd(dy, dxs_skip, xa, prev, dec, carry=None):
    t = xa.shape[0]
    nc = t // CHUNK
    gp, gw, dstate = SSD_GP, GROUP_W, D_STATE
    hpg = HEADS_PER_GROUP

    grid = (N_GROUPS // gp, nc)
    c_in, c_out, c_sem = carry.specs() if carry is not None else ([], [], [])
    nci, nco = len(c_in), len(c_out)

    def body(*refs):
        ins, c_ins = refs[:11], refs[11:11 + nci]
        outs = refs[11 + nci:17 + nci]
        c_outs = refs[17 + nci:17 + nci + nco]
        dst_ref = refs[17 + nci + nco]
        dbias_ref, dalog_ref = outs[4], outs[5]
        if carry is not None:
            begin, end = carry.at_edges(grid, c_ins, c_outs, refs[-2:])
            begin()

        @pl.when(pl.program_id(1) == 0)
        def _():
            dst_ref[...] = jnp.zeros_like(dst_ref)
            dbias_ref[...] = jnp.zeros_like(dbias_ref)
            dalog_ref[...] = jnp.zeros_like(dalog_ref)

        for k in range(gp):
            one_group(k, *ins, *outs, dst_ref)
        if carry is not None:
            end()

    def one_group(k, xs_ref, b_ref, c_ref, dt_ref, acs_ref, acsr_ref, sig_ref, al_ref, dy_ref, sk_ref, prev_ref,
                  dxs_ref, db_ref, dc_ref, ddt_ref, dbias_ref, dalog_ref, dst_ref):
        wide = slice(k * gw, (k + 1) * gw)
        narrow = slice(k * dstate, (k + 1) * dstate)
        q = _ssd_q(dt_ref[k], acs_ref[k], acsr_ref[k])
        a_r = -jnp.exp(al_ref[k])
        hm, ds, dec, dt_exp = q["hm"], q["ds"], q["dec"], q["dt_exp"]
        xs = xs_ref[:, wide]
        xdt = xs * dt_exp
        xdtb = xdt.astype(BF16)
        bm = b_ref[:, narrow].astype(BF16)
        cm = c_ref[:, narrow].astype(BF16)
        cb = _dot(cm, bm, _NT)
        bc = _dot(bm, cm, _NT)
        dyv = dy_ref[:, wide]
        dye = (dyv * q["ea"]).astype(BF16)
        pst = prev_ref[0, k]
        dst = dst_ref[k]
        dstb = dst.astype(BF16)
        dx_state = ds * _dot(bm, dstb)
        onehot = [(lax.broadcasted_iota(jnp.int32, (1, hpg), 1) == r).astype(F32) for r in range(hpg)]
        lm, lt = _ssd_decay_stacks(q)
        dyb = dyv.astype(BF16)
        gl = _dot(_mask_heads(dyv, hm).astype(BF16), xdtb, _NT) * lm
        glt = _dot(_mask_heads(xdt, hm).astype(BF16), dyb, _NT) * lt
        bc4 = _stack4(bc)
        dxdt = dx_state + _pick_heads(_dot((bc4 * lt).astype(BF16), dyb), hm)
        wd = jnp.sum(gl * _stack4(cb), axis=1, keepdims=True) - jnp.sum(glt * bc4, axis=1, keepdims=True)
        dcb = gl[:CHUNK]
        dcbt = glt[:CHUNK]
        qa = wd[:CHUNK] * onehot[0]
        for r in range(1, hpg):
            dcb = dcb + gl[r * CHUNK:(r + 1) * CHUNK]
            dcbt = dcbt + glt[r * CHUNK:(r + 1) * CHUNK]
            qa = qa + wd[r * CHUNK:(r + 1) * CHUNK] * onehot[r]
        pstb = pst.astype(BF16)
        dc_ref[:, narrow] = _dot(dcb.astype(BF16), bm) + _dot(dye, pstb, _NT)
        db_ref[:, narrow] = _dot(dcbt.astype(BF16), cm) + _dot((xdt * ds).astype(BF16), dstb, _NT)
        dst_ref[k] = dec * dst + _dot(cm, dye, _TN)
        dxs_ref[:, wide] = dxdt * dt_exp + sk_ref[:, wide]

        t2 = xdt * dx_state
        t1 = dyv * (_dot(cm, pstb) * q["ea"]) - t2
        t3 = dst * pst
        t4 = dxdt * xs
        xd = jnp.zeros((CHUNK, hpg), F32)
        dal = jnp.zeros((1, hpg), F32)
        for r in range(hpg):
            qa = qa + jnp.sum(jnp.where(hm[r], t1, 0.0), axis=1, keepdims=True) * onehot[r]
            xd = xd + jnp.sum(jnp.where(hm[r], t4, 0.0), axis=1, keepdims=True) * onehot[r]
            s2 = jnp.sum(jnp.sum(jnp.where(hm[r], t2, 0.0), axis=1, keepdims=True), axis=0, keepdims=True)
            s3 = jnp.sum(jnp.sum(jnp.where(hm[r], t3, 0.0), axis=1, keepdims=True), axis=0, keepdims=True)
            dal = dal + (s2 + dec[:, r * HEAD_DIM:r * HEAD_DIM + 1] * s3) * onehot[r]
        rc = _cumsum(qa, 0, reverse=True) + dal
        rowid = (nc - 1 - pl.program_id(1)) * CHUNK + lax.broadcasted_iota(jnp.int32, (CHUNK, 1), 0)
        ddt_raw = jnp.where(rowid >= N_PAD, (rc * a_r + xd) * sig_ref[k], 0.0)
        ddt_ref[k] = ddt_raw
        dbias_ref[k] += jnp.sum(ddt_raw, axis=0, keepdims=True)
        dalog_ref[k] += jnp.sum(rc * q["dtc"], axis=0, keepdims=True) * a_r

    rev = lambda c: nc - 1 - c
    blk = pl.BlockSpec((CHUNK, gp * gw), lambda g, c: (rev(c), g))
    nblk = pl.BlockSpec((CHUNK, gp * dstate), lambda g, c: (rev(c), g))
    cols = pl.BlockSpec((gp, CHUNK, hpg), lambda g, c: (g, rev(c), 0))
    small = pl.BlockSpec((gp, 1, hpg), lambda g, c: (g, 0, 0))
    res = pl.pallas_call(
        body, name="ssd_bwd", grid=grid,
        in_specs=_ssd_specs(rev) + [cols, small, blk, blk, pl.BlockSpec((1, gp, dstate, gw), lambda g, c: (rev(c), g, 0, 0))]
        + c_in,
        out_specs=[blk, nblk, nblk, cols, small, small] + c_out,
        out_shape=[jax.ShapeDtypeStruct((t, D_INNER), F32), jax.ShapeDtypeStruct((t, N_GROUPS * dstate), F32),
                   jax.ShapeDtypeStruct((t, N_GROUPS * dstate), F32), jax.ShapeDtypeStruct((N_GROUPS, t, hpg), F32),
                   jax.ShapeDtypeStruct((N_GROUPS, 1, hpg), F32), jax.ShapeDtypeStruct((N_GROUPS, 1, hpg), F32)]
        + (carry.outs if carry is not None else []),
        scratch_shapes=[pltpu.VMEM((gp, dstate, gw), F32)] + c_sem,
        compiler_params=_cparams(("arbitrary", "arbitrary") if carry is not None else ("parallel", "arbitrary")),
    )(xa, xa, xa, dec["dt"], dec["acs"], dec["acs_rows"], dec["sig"], dec["a_log"], dy, dxs_skip, prev,
      *(carry.ins if carry is not None else []))
    dxs, db, dc, ddt, dbias, dalog = res[:6]
    ddt_raw = ddt.transpose(1, 0, 2).reshape(t, N_HEADS)
    return dxs, db, dc, ddt_raw, dbias.reshape(N_HEADS), dalog.reshape(N_HEADS), res[6:]


def _ssd_post_fwd(y, xa, proj, dsk, nw):
    t = y.shape[0]
    di = D_INNER
    tm = _tile(t, (384, 128))

    def body(y_ref, xs_ref, z_ref, dsk_ref, nw_ref, o_ref):
        z = z_ref[...]
        yz = (y_ref[...] + xs_ref[...] * dsk_ref[...]) * (z * _sigmoid(z))
        nwv = nw_ref[...]
        for g in range(N_GROUPS):
            sl = slice(g * GROUP_W, (g + 1) * GROUP_W)
            v = yz[:, sl]
            rg = lax.rsqrt(jnp.mean(v * v, axis=-1, keepdims=True) + EPS)
            o_ref[:, sl] = (v * rg * nwv[:, sl]).astype(BF16)

    row = pl.BlockSpec((tm, di), lambda i: (i, 0))
    vec = pl.BlockSpec((1, di), lambda i: (0, 0))
    return pl.pallas_call(
        body, name="ssd_post_fwd", grid=(t // tm,),
        in_specs=[row, row, row, vec, vec], out_specs=row, out_shape=jax.ShapeDtypeStruct((t, di), BF16),
        compiler_params=_cparams(("parallel",)),
    )(y, xa, proj, dsk.reshape(1, di), nw.reshape(1, di))


def _ssd_post_bwd(dyn, y, xa, proj, dsk, nw, dproj):
    t = y.shape[0]
    di = D_INNER
    tm = CHUNK

    def body(dyn_ref, y_ref, xs_ref, z_ref, dsk_ref, nw_ref, _, dy_ref, dsk_x_ref, dz_ref, dnw_ref, ddsk_ref):
        @pl.when(pl.program_id(0) == 0)
        def _():
            dnw_ref[...] = jnp.zeros_like(dnw_ref)
            ddsk_ref[...] = jnp.zeros_like(ddsk_ref)

        for g in range(N_GROUPS):
            sl = slice(g * GROUP_W, (g + 1) * GROUP_W)
            z = z_ref[:, sl]
            sig = _sigmoid(z)
            s = z * sig
            xs = xs_ref[:, sl]
            dskv = dsk_ref[:, sl]
            yt = y_ref[:, sl] + xs * dskv
            yz = yt * s
            rg = lax.rsqrt(jnp.mean(yz * yz, axis=-1, keepdims=True) + EPS)
            xhat = yz * rg
            dynv = dyn_ref[:, sl]
            gg = dynv * nw_ref[:, sl]
            dnw_ref[:, sl] += jnp.sum(dynv * xhat, axis=0, keepdims=True)
            dyz = rg * (gg - xhat * jnp.mean(gg * xhat, axis=-1, keepdims=True))
            dyt = dyz * s
            dy_ref[:, sl] = dyt
            dsk_x_ref[:, sl] = dyt * dskv
            dz_ref[:, sl] = (dyz * yt * (sig * (1.0 + z * (1.0 - sig)))).astype(BF16)
            ddsk_ref[:, sl] += jnp.sum(dyt * xs, axis=0, keepdims=True)

    row = pl.BlockSpec((tm, di), lambda i: (i, 0))
    vec = pl.BlockSpec((1, di), lambda i: (0, 0))
    dy, dxsk, dz, dnw, ddsk = pl.pallas_call(
        body, name="ssd_post_bwd", grid=(t // tm,),
        in_specs=[row, row, row, row, vec, vec, pl.BlockSpec(memory_space=pl.ANY)], out_specs=[row, row, row, vec, vec],
        out_shape=[jax.ShapeDtypeStruct((t, di), F32), jax.ShapeDtypeStruct((t, di), F32), jax.ShapeDtypeStruct(dproj.shape, BF16),
                   jax.ShapeDtypeStruct((1, di), F32), jax.ShapeDtypeStruct((1, di), F32)],
        input_output_aliases={6: 2},
        compiler_params=_cparams(("arbitrary",)),
    )(dyn, y, xa, proj, dsk.reshape(1, di), nw.reshape(1, di), dproj)
    return dy, dxsk, dz, dnw.reshape(di), ddsk.reshape(N_HEADS, HEAD_DIM).sum(axis=1)


SHARD_COLS = IN_COLS // N_CHIPS


def _split_w_in(w_in_sh):
    nl = w_in_sh.shape[0]
    w = w_in_sh.transpose(0, 2, 1, 3).reshape(nl, D_MODEL, IN_COLS)
    main = jnp.concatenate([w[..., OFF_Z:OFF_XBC], w[..., OFF_GATE:], w[..., :OFF_Z], w[..., OFF_XBC:OFF_DT]], axis=-1)
    dt = jnp.pad(w[..., OFF_DT:OFF_GATE], ((0, 0), (0, 0), (0, DT_PAD - N_HEADS)))
    return main, dt


def _merge_dw_in(dmain, ddt):
    full = jnp.concatenate([dmain[:, PPOOL:PXBC], dmain[:, PZ:PGATE], dmain[:, PXBC:], ddt[:, :N_HEADS], dmain[:, PGATE:PPOOL]],
                           axis=1)
    return full.reshape(D_MODEL, N_CHIPS, SHARD_COLS).transpose(1, 0, 2)


def _relu2_epilogue(acc):
    hid = jnp.maximum(acc, 0.0)
    return acc, hid * hid


def _relu2_bwd_epilogue(acc, pre):
    return (acc * 2.0 * jnp.maximum(pre, 0.0),)


def _add_epilogue(acc, res):
    return (acc + res,)


def _w_spec(rows, cols, fn):
    return pl.BlockSpec((None, None, rows, cols), fn)


def _layer_fwd(h, w, big, layer):
    t = h.shape[0]
    d = D_MODEL
    tm = _tile(t, (1408, 384, 128))
    nt = t // tm
    row = _bs((tm, d), lambda i, j, k: (i, 0))
    s = {"h": h}
    u = _norm_fwd(h, w["mix_norm_w"], "norm_mix")
    proj = _mm(f"mm_proj_{layer}", _NN, (nt, PCOLS // d, 1), u, row,
               big["w_in_main"], _bs((None, d, d), lambda i, j, k: (layer, 0, j)),
               [((t, PCOLS), F32, _bs((tm, d), lambda i, j, k: (i, j)))])
    dt_raw = _mm(f"mm_dt_{layer}", _NN, (nt, 1, 1), u, row,
                 big["w_in_dt"], _bs((None, d, DT_PAD), lambda i, j, k: (layer, 0, 0)),
                 [((t, DT_PAD), F32, _bs((tm, DT_PAD), lambda i, j, k: (i, 0)))])
    pooled, yg, ypm, y_pool = _pool_fwd(proj, big["pool_w_group"], w["pool_scale"], big["w_pool_up"], layer)
    xa = _conv_fwd(proj, w["conv_w"], w["conv_b"])
    dec = _ssd_decay(dt_raw, w["dt_bias"], w["a_log"])
    y, prev = _ssd_fwd(xa, dec)
    yn = _ssd_post_fwd(y, xa, proj, w["d_skip_exp"], w["ssd_norm_w"])
    rs = D_INNER // N_CHIPS
    y_ssd = _mm(f"mm_ssd_out_{layer}", _NN, (nt, 1, N_CHIPS), yn, _bs((tm, rs), lambda i, j, k: (i, k)),
                big["w_ssd_out"], _w_spec(rs, d, lambda i, j, k: (layer, k, 0, 0)), [((t, d), F32, row)])
    merged = _gate_fwd(proj, w["b_gate"], y_pool, y_ssd)
    ro = d // N_CHIPS
    h1 = _mm(f"mm_o_{layer}", _NN, (nt, 1, N_CHIPS), merged, _bs((tm, ro), lambda i, j, k: (i, k)),
             big["w_o"], _w_spec(ro, d, lambda i, j, k: (layer, k, 0, 0)), [((t, d), F32, row)],
             extras=[(h, row)], epilogue=_add_epilogue)
    v = _norm_fwd(h1, w["mlp_norm_w"], "norm_mlp")
    tile = _bs((tm, d), lambda i, j, k: (i, j))
    pre, act = _mm(f"mm_ff1_{layer}", _NN, (nt, N_CHIPS, 1), v, row,
                   big["w_ff1"], _w_spec(d, d, lambda i, j, k: (layer, j, 0, 0)),
                   [((t, D_FF), F32, tile), ((t, D_FF), BF16, tile)], epilogue=_relu2_epilogue)
    h2 = _mm(f"mm_ff2_{layer}", _NN, (nt, 1, N_CHIPS), act, _bs((tm, d), lambda i, j, k: (i, k)),
             big["w_ff2"], _w_spec(d, d, lambda i, j, k: (layer, k, 0, 0)), [((t, d), F32, row)],
             extras=[(h1, row)], epilogue=_add_epilogue)
    s.update(u=u, proj=proj, dec=dec, pooled=pooled, yg=yg, ypm=ypm, y_pool=y_pool, xa=xa, y=y, prev=prev, yn=yn,
             y_ssd=y_ssd, merged=merged, h1=h1, v=v, pre=pre, act=act)
    return h2, s


def _dw(name, layer, a, b, shard_shape, by, tk):
    t = a.shape[0]
    nk = t // tk
    rows, cols = shard_shape
    if by == "rows":
        grid = (N_CHIPS, 1, nk)
        a_spec = _bs((tk, rows), lambda i, j, k: (k, i))
        b_spec = _bs((tk, cols), lambda i, j, k: (k, 0))
        o_spec = _bs((None, rows, cols), lambda i, j, k: (i, 0, 0))
    else:
        grid = (1, N_CHIPS, nk)
        a_spec = _bs((tk, rows), lambda i, j, k: (k, 0))
        b_spec = _bs((tk, cols), lambda i, j, k: (k, j))
        o_spec = _bs((None, rows, cols), lambda i, j, k: (j, 0, 0))
    return _mm(f"{name}_{layer}", _TN, grid, a, a_spec, b, b_spec, [((N_CHIPS, rows, cols), F32, o_spec)])


def _layer_bwd(dh, s, w, big, layer, red=None):
    t = dh.shape[0]
    d = D_MODEL
    tm = _tile(t, (1408, 384, 128))
    tk = _tile(t, (1408, 384, 128))
    nt = t // tm
    row = _bs((tm, d), lambda i, j, k: (i, 0))
    tile = _bs((tm, d), lambda i, j, k: (i, j))
    g, gb = {}, {}
    carry = red.swap_carry() if red is not None else None
    dpre = _mm(f"mm_dact_{layer}", _NT, (nt, N_CHIPS, 1), dh, row,
               big["w_ff2"], _w_spec(d, d, lambda i, j, k: (layer, j, 0, 0)), [((t, D_FF), BF16, tile)],
               extras=[(s["pre"], tile)], epilogue=_relu2_bwd_epilogue, carry=carry)
    if carry is not None:
        (dpre,), got = dpre
        red.after_swap(got)
    gb["w_ff2"] = _dw("mm_dw_ff2", layer, s["act"], dh, (d, d), "rows", tk)
    dv = _mm(f"mm_dv_{layer}", _NT, (nt, 1, N_CHIPS), dpre, _bs((tm, d), lambda i, j, k: (i, k)),
             big["w_ff1"], _w_spec(d, d, lambda i, j, k: (layer, k, 0, 0)), [((t, d), F32, row)])
    gb["w_ff1"] = _dw("mm_dw_ff1", layer, s["v"], dpre, (d, d), "cols", tk)
    dh1, g["mlp_norm_w"] = _norm_bwd(dh, dv, s["h1"], w["mlp_norm_w"], "norm_mlp_bwd")
    ro = d // N_CHIPS
    dmerged = _mm(f"mm_dmerged_{layer}", _NT, (nt, N_CHIPS, 1), dh1, row,
                  big["w_o"], _w_spec(ro, d, lambda i, j, k: (layer, j, 0, 0)),
                  [((t, d), F32, _bs((tm, ro), lambda i, j, k: (i, j)))])
    gb["w_o"] = _dw("mm_dw_o", layer, s["merged"], dh1, (ro, d), "rows", tk)
    dy_pool, dy_ssd, dproj, g["b_gate"] = _gate_bwd(dmerged, s["proj"], w["b_gate"], s["y_pool"], s["y_ssd"])
    rs = D_INNER // N_CHIPS
    dyn = _mm(f"mm_dyn_{layer}", _NT, (nt, N_CHIPS, 1), dy_ssd, row,
              big["w_ssd_out"], _w_spec(rs, d, lambda i, j, k: (layer, j, 0, 0)),
              [((t, D_INNER), F32, _bs((tm, rs), lambda i, j, k: (i, j)))])
    gb["w_ssd_out"] = _dw("mm_dw_ssd_out", layer, s["yn"], dy_ssd, (rs, d), "rows", tk)
    dy, dxs_skip, dproj, g["ssd_norm_w"], g["d_skip"] = _ssd_post_bwd(dyn, s["y"], s["xa"], s["proj"], w["d_skip_exp"],
                                                                      w["ssd_norm_w"], dproj)
    carry = red.ici_carry() if red is not None else None
    dxs, db, dc, ddt_raw, g["dt_bias"], g["a_log"], got = _ssd_bwd(dy, dxs_skip, s["xa"], s["prev"], s["dec"], carry)
    if carry is not None:
        red.after_ici(got)
    dproj, dcw1, dcb1 = _conv_bwd(dxs, 0, s["proj"], w["conv_w"], w["conv_b"], dproj)
    dproj, dcw2, dcb2 = _conv_bwd(db, 2, s["proj"], w["conv_w"], w["conv_b"], dproj)
    dproj, dcw3, dcb3 = _conv_bwd(dc, 3, s["proj"], w["conv_w"], w["conv_b"], dproj)
    g["conv_w"] = jnp.concatenate([dcw1, dcw2, dcw3], axis=1)
    g["conv_b"] = jnp.concatenate([dcb1, dcb2, dcb3])
    q, dyg, g["pool_scale"] = _pool_bwd_a(dy_pool, s["yg"], w["pool_scale"], big["pool_w_group"], big["w_pool_up"], layer)
    gb["w_pool_up"] = _dw("mm_dw_pool_up", layer, s["ypm"], dy_pool, (ro, d), "rows", tk)
    gb["pool_w_group"] = _pool_dwg(s["pooled"], dyg, layer).reshape(N_CHIPS, POOL_GDIM, POOL_GDIM)
    dproj = _pool_bwd_b(q, dproj)
    ddt = jnp.pad(ddt_raw.astype(BF16), ((0, 0), (0, DT_PAD - N_HEADS)))
    nk = PCOLS // d
    du = _mm(f"mm_du_{layer}", _NT, (nt, 1, nk), dproj, _bs((tm, d), lambda i, j, k: (i, k)),
             big["w_in_main"], _bs((None, d, d), lambda i, j, k: (layer, 0, k)), [((t, d), F32, row)])
    du = _mm(f"mm_du_dt_{layer}", _NT, (nt, 1, 1), ddt, _bs((tm, DT_PAD), lambda i, j, k: (i, 0)),
             big["w_in_dt"], _bs((None, d, DT_PAD), lambda i, j, k: (layer, 0, 0)), [((t, d), F32, row)],
             extras=[(du, row)], epilogue=_add_epilogue)
    ntk = t // tk
    u_spec = _bs((tk, d), lambda i, j, k: (k, 0))
    dmain = _mm(f"mm_dw_in_{layer}", _TN, (1, nk, ntk), s["u"], u_spec, dproj, _bs((tk, d), lambda i, j, k: (k, j)),
                [((d, PCOLS), F32, _bs((d, d), lambda i, j, k: (0, j)))])
    ddtw = _mm(f"mm_dw_dt_{layer}", _TN, (1, 1, ntk), s["u"], u_spec, ddt, _bs((tk, DT_PAD), lambda i, j, k: (k, 0)),
               [((d, DT_PAD), F32, _bs((d, DT_PAD), lambda i, j, k: (0, 0)))])
    gb["w_in"] = _merge_dw_in(dmain, ddtw)
    dh0, g["mix_norm_w"] = _norm_bwd(dh1, du, s["h"], w["mix_norm_w"], "norm_mix_bwd")
    return dh0, g, gb


BIG = ("w_in", "pool_w_group", "w_pool_up", "w_ssd_out", "w_o", "w_ff1", "w_ff2")
PER_LAYER = ("mix_norm_w", "w_in", "b_gate", "pool_w_group", "pool_scale", "w_pool_up", "conv_w", "conv_b", "dt_bias",
             "a_log", "d_skip", "ssd_norm_w", "w_ssd_out", "w_o", "mlp_norm_w", "w_ff1", "w_ff2")


SMALL_PER_LAYER = tuple(n for n in PER_LAYER if n not in BIG)


def _local_step(x, tgt, big, small, red=None):
    seq = x.shape[0]
    big = dict(big)
    big["w_in_main"], big["w_in_dt"] = _split_w_in(big["w_in"])
    h = jnp.concatenate([jnp.zeros((N_PAD, D_MODEL), F32), small["meta_tokens"], x], axis=0)
    saved, ws = [], []
    for i in range(DEPTH):
        w = {n: small[n][i] for n in SMALL_PER_LAYER}
        w["d_skip_exp"] = jnp.repeat(w["d_skip"], HEAD_DIM)
        h, s = _layer_fwd(h, w, big, i)
        saved.append(s)
        ws.append(w)
    loss, dh, g_final = _loss_head(h, small["final_norm_w"], tgt)
    layer_g, layer_gb = [None] * DEPTH, [None] * DEPTH
    for i in reversed(range(DEPTH)):
        dh, layer_g[i], layer_gb[i] = _layer_bwd(dh, saved[i], ws[i], big, i, red)
        if red is not None:
            red.push(i, layer_gb[i])
    grads = {n: jnp.stack([layer_g[i][n] for i in range(DEPTH)]) for n in SMALL_PER_LAYER}
    grads["final_norm_w"] = g_final
    grads["meta_tokens"] = dh[N_PAD:ROW_X]
    return loss, dh[ROW_X:ROW_X + seq], layer_gb, grads


MESH = pl.DeviceIdType.MESH
LANES = 128
ANY = pl.BlockSpec(memory_space=pl.ANY)


def _place():
    x, y, c = lax.axis_index("x"), lax.axis_index("y"), lax.axis_index("c")
    chips = [(1 - x, y), (x, 1 - y), (1 - x, 1 - y)]
    return x, y, c, chips


def _remote(src, dst, send_sem, recv_sem, to):
    return pltpu.make_async_remote_copy(src_ref=src, dst_ref=dst, send_sem=send_sem, recv_sem=recv_sem,
                                        device_id=to, device_id_type=MESH)


HALF = DEPTH // 2


def _gather_big(shards):
    n = len(shards)
    per = 7

    def body(*refs):
        src, out = refs[:n], refs[n:2 * n]
        send_sems, recv_sems = refs[2 * n:]
        x, y, c, chips = _place()
        sibling = (x, y, 1 - c)
        me = 2 * x + y
        lo = HALF * c
        sends = []

        def go(src_ref, dst_ref, k, to):
            cp = _remote(src_ref, dst_ref, send_sems.at[k], recv_sems.at[k], to)
            cp.start()
            sends.append(cp)

        def landed(ref, k):
            _remote(ref, ref, send_sems.at[k], recv_sems.at[k], sibling).wait_recv()

        for p in range(n):
            go(src[p], out[p].at[:, me], per * p, sibling)
            for j, chip in enumerate(chips):
                go(src[p].at[pl.ds(lo, HALF)], out[p].at[pl.ds(lo, HALF), me], per * p + 1 + j, (*chip, c))
        for j, (cx, cy) in enumerate(chips):
            for p in range(n):
                blk = out[p].at[pl.ds(lo, HALF), 2 * cx + cy]
                landed(blk, per * p + 1 + j)
                go(blk, blk, per * p + 4 + j, sibling)
        for p in range(n):
            landed(out[p].at[:, me], per * p)
        for j, (cx, cy) in enumerate(chips):
            for p in range(n):
                landed(out[p].at[pl.ds(HALF * (1 - c), HALF), 2 * cx + cy], per * p + 4 + j)
        for cp in sends:
            cp.wait_send()

    return pl.pallas_call(
        body, name="gather_big", in_specs=[ANY] * n, out_specs=[ANY] * n,
        out_shape=[jax.ShapeDtypeStruct((a.shape[0], N_CHIPS) + a.shape[1:], a.dtype) for a in shards],
        scratch_shapes=[pltpu.SemaphoreType.DMA((per * n,)), pltpu.SemaphoreType.DMA((per * n,))],
    )(*shards)


def _exchange_small(v, reduce, name):
    rows_per = v.shape[0]
    vm = pl.BlockSpec(memory_space=pltpu.VMEM)

    def body(v_ref, out_ref, *scratch):
        if reduce:
            land_ref, send_sems, recv_sems, local_sem = scratch
        else:
            land_ref = out_ref
            send_sems, recv_sems, local_sem = scratch
        x, y, c, chips = _place()
        me, sibling = (x, y, c), (x, y, 1 - c)

        def rows(px, py, pc):
            return land_ref.at[4 * px + 2 * py + pc]

        def copy(k, block, to, src=None):
            return _remote(rows(*block) if src is None else src, rows(*block), send_sems.at[k], recv_sems.at[k], to)

        mine = pltpu.make_async_copy(v_ref, rows(*me), local_sem)
        mine.start()
        first = [copy(0, me, sibling, src=v_ref)]
        first += [copy(1 + j, me, (*chip, c), src=v_ref) for j, chip in enumerate(chips)]
        for cp in first:
            cp.start()
        passed = [copy(4 + j, (*chip, c), sibling) for j, chip in enumerate(chips)]
        for j, chip in enumerate(chips):
            copy(1 + j, (*chip, c), me).wait_recv()
            passed[j].start()
        copy(0, sibling, me).wait_recv()
        for j, chip in enumerate(chips):
            copy(4 + j, (*chip, 1 - c), me).wait_recv()
        for cp in first + passed:
            cp.wait_send()
        mine.wait()
        if reduce:
            acc = land_ref[0]
            for d in range(1, 8):
                acc = acc + land_ref[d]
            out_ref[...] = acc

    sems = [pltpu.SemaphoreType.DMA((7,)), pltpu.SemaphoreType.DMA((7,)), pltpu.SemaphoreType.DMA]
    if reduce:
        out_shape = jax.ShapeDtypeStruct((rows_per, LANES), F32)
        scratch = [pltpu.VMEM((8, rows_per, LANES), F32)] + sems
    else:
        out_shape = jax.ShapeDtypeStruct((8, rows_per, LANES), F32)
        scratch = sems
    return pl.pallas_call(
        body, name=name, in_specs=[vm], out_specs=vm, out_shape=out_shape, scratch_shapes=scratch,
        compiler_params=pltpu.CompilerParams(vmem_limit_bytes=VMEM_LIMIT),
    )(v)


def _run_carry(carry, name):
    c_in, c_out, c_sem = carry.specs()
    n_in, n_out = len(c_in), len(c_out)

    def body(*refs):
        cps = carry.make(refs[:n_in], refs[n_in:n_in + n_out], *refs[n_in + n_out:])
        for cp in cps:
            cp.start()
        for cp in cps:
            cp.wait()

    return pl.pallas_call(body, name=name, in_specs=c_in, out_specs=c_out, out_shape=carry.outs,
                          scratch_shapes=c_sem)(*carry.ins)


def _row_tile(rows, last, itemsize=4, budget=2 * 1024 * 1024):
    return _tile(rows, tuple(t for t in (2048, 1024, 512, 256, 128, 64, 32, 16) if t * last * itemsize <= budget))


def _rs_add(g, got, core, name):
    _, half, last = got.shape
    tr = _row_tile(half, last)
    nb = half // tr

    def body(c_ref, g_ref, got_ref, o_ref):
        o_ref[...] = (g_ref[...] + got_ref[...]).astype(BF16)

    blk = pl.BlockSpec((None, tr, last), lambda s, i, c_ref: (s, i, 0))
    return pl.pallas_call(
        body, name=name,
        grid_spec=pltpu.PrefetchScalarGridSpec(
            num_scalar_prefetch=1, grid=(N_CHIPS, nb),
            in_specs=[pl.BlockSpec((None, tr, last), lambda s, i, c_ref: (s, c_ref[0] * nb + i, 0)), blk],
            out_specs=blk),
        out_shape=jax.ShapeDtypeStruct(got.shape, BF16),
        compiler_params=_cparams(("parallel", "parallel")),
    )(core.reshape(1).astype(jnp.int32), g, got)


def _rs_sum(own, got, acc, layer, chip, core, name):
    _, half, last = own.shape
    tr = _row_tile(half, last, budget=1024 * 1024)
    nb = half // tr
    ni = 0 if acc is None else 1

    def body(k_ref, c_ref, own_ref, got_ref, *rest):
        o_ref = rest[-1]
        tot = own_ref[...].astype(F32)
        for j in range(3):
            tot = tot + got_ref[j].astype(F32)
        o_ref[...] = tot

    return pl.pallas_call(
        body, name=name,
        grid_spec=pltpu.PrefetchScalarGridSpec(
            num_scalar_prefetch=2, grid=(nb,),
            in_specs=[pl.BlockSpec((None, tr, last), lambda i, k_ref, c_ref: (k_ref[0], i, 0)),
                      pl.BlockSpec((3, tr, last), lambda i, k_ref, c_ref: (0, i, 0))]
            + [pl.BlockSpec(memory_space=pl.ANY)] * ni,
            out_specs=pl.BlockSpec((None, tr, last), lambda i, k_ref, c_ref: (layer, c_ref[0] * nb + i, 0))),
        out_shape=jax.ShapeDtypeStruct((DEPTH, 2 * half, last), F32),
        input_output_aliases={4: 0} if ni else {},
        compiler_params=_cparams(("parallel",)),
    )(chip.reshape(1).astype(jnp.int32), core.reshape(1).astype(jnp.int32), own, got, *([acc] if ni else []))


def _rs_share(arrs):
    n = len(arrs)

    def body(*refs):
        out = refs[n:2 * n]
        send_sems, recv_sems = refs[2 * n:]
        x, y, c, _ = _place()
        sibling = (x, y, 1 - c)
        cps = []
        for p in range(n):
            half = out[p].shape[1] // 2
            mine = out[p].at[:, pl.ds(c * half, half), :]
            cps.append(_remote(mine, mine, send_sems.at[p], recv_sems.at[p], sibling))
        for cp in cps:
            cp.start()
        for p, cp in enumerate(cps):
            cp.wait_send()
            half = out[p].shape[1] // 2
            other = out[p].at[:, pl.ds((1 - c) * half, half), :]
            _remote(other, other, send_sems.at[p], recv_sems.at[p], sibling).wait_recv()

    return pl.pallas_call(
        body, name="rs_share", in_specs=[ANY] * n, out_specs=[ANY] * n,
        out_shape=[jax.ShapeDtypeStruct(a.shape, a.dtype) for a in arrs],
        input_output_aliases={p: p for p in range(n)},
        scratch_shapes=[pltpu.SemaphoreType.DMA((n,)), pltpu.SemaphoreType.DMA((n,))],
    )(*arrs)


class _GradReducer:
    def __init__(self, chip, core):
        self.chip, self.core = chip, core
        self.pending = None
        self.sums = None
        self.acc = {n: None for n in BIG}

    def push(self, layer, gb):
        self.pending = (layer, [gb[n] for n in BIG])

    def swap_carry(self):
        if self.pending is None:
            return None
        _, gs = self.pending
        outs = [jax.ShapeDtypeStruct((g.shape[0], g.shape[1] // 2, g.shape[2]), F32) for g in gs]

        def make(ins, out, send_sems, recv_sems):
            x, y, c, _ = _place()
            cps = []
            for p, ref in enumerate(ins):
                half = ref.shape[1] // 2
                cps.append(_remote(ref.at[:, pl.ds((1 - c) * half, half), :], out[p], send_sems.at[p], recv_sems.at[p],
                                   (x, y, 1 - c)))
            return cps

        return _Carry(gs, outs, len(gs), make)

    def after_swap(self, got):
        layer, gs = self.pending
        self.pending = None
        self.sums = (layer, [_rs_add(g, a, self.core, f"rs_add_{n}_{layer}") for n, g, a in zip(BIG, gs, got)])

    def ici_carry(self):
        if self.sums is None:
            return None
        _, ps = self.sums
        outs = [jax.ShapeDtypeStruct((3,) + p.shape[1:], BF16) for p in ps]

        def make(ins, out, send_sems, recv_sems):
            x, y, c, chips = _place()
            return [_remote(ins[p].at[2 * cx + cy], out[p].at[j], send_sems.at[3 * p + j], recv_sems.at[3 * p + j],
                            (cx, cy, c)) for p in range(len(ins)) for j, (cx, cy) in enumerate(chips)]

        return _Carry(ps, outs, 3 * len(ps), make)

    def after_ici(self, got):
        layer, ps = self.sums
        self.sums = None
        for n, p, a in zip(BIG, ps, got):
            self.acc[n] = _rs_sum(p, a, self.acc[n], layer, self.chip, self.core, f"rs_sum_{n}_{layer}")

    def finish(self):
        self.after_swap(_run_carry(self.swap_carry(), "rs_swap_last"))
        self.after_ici(_run_carry(self.ici_carry(), "rs_ici_last"))
        return dict(zip(BIG, _rs_share([self.acc[n] for n in BIG])))


def _adamw(w, g, m, v, name):
    shape = w.shape
    if len(shape) == 2:
        shape3 = (1,) + shape
    else:
        shape3 = (-1,) + shape[-2:]
    w2, g2, m2, v2 = (a.reshape(shape3) for a in (w, g, m, v))
    lead, rows, last = w2.shape
    tr = _tile(rows, tuple(t for t in (1024, 512, 256, 128, 64, 32, 16, 8) if t * last * 4 <= 2 * 1024 * 1024) + (rows,))

    def body(w_ref, g_ref, m_ref, v_ref, d_ref, nm_ref, nv_ref):
        gv = g_ref[...]
        mn = ADAM_B1 * m_ref[...] + (1.0 - ADAM_B1) * gv
        vn = ADAM_B2 * v_ref[...] + (1.0 - ADAM_B2) * (gv * gv)
        m_hat = mn / (1.0 - ADAM_B1 ** ADAM_STEP)
        v_hat = vn / (1.0 - ADAM_B2 ** ADAM_STEP)
        d_ref[...] = -ADAM_LR * (m_hat / (jnp.sqrt(v_hat) + ADAM_EPS) + ADAM_WD * w_ref[...])
        nm_ref[...] = mn
        nv_ref[...] = vn

    blk = pl.BlockSpec((None, tr, last), lambda a, i: (a, i, 0))
    outs = pl.pallas_call(
        body, name=name, grid=(lead, rows // tr), in_specs=[blk] * 4, out_specs=[blk] * 3,
        out_shape=[jax.ShapeDtypeStruct((lead, rows, last), F32)] * 3,
        compiler_params=_cparams(("parallel", "parallel")),
    )(w2, g2, m2, v2)
    return tuple(o.reshape(shape) for o in outs)


SHARD_AXIS = {"conv_w": 2, "meta_tokens": 1}
SMALL_SHARDED = ("conv_w", "meta_tokens")
REPLICATED = ("mix_norm_w", "b_gate", "pool_scale", "conv_b", "dt_bias", "a_log", "d_skip", "ssd_norm_w", "mlp_norm_w",
              "final_norm_w")


def _flatten(arrs, dtype, row_mult):
    flat = jnp.concatenate([a.astype(dtype).reshape(-1) for a in arrs])
    n = flat.shape[0]
    rows = -(-n // (LANES * row_mult)) * row_mult
    return jnp.pad(flat, (0, rows * LANES - n)).reshape(rows, LANES)


def _unflatten(flat2d, shapes):
    flat = flat2d.reshape(-1)
    out, off = [], 0
    for sh in shapes:
        n = 1
        for d in sh:
            n *= d
        out.append(flat[off:off + n].reshape(sh))
        off += n
    return out


def kernel(x, meta_tokens, mix_norm_w, w_in, b_gate, pool_w_group, pool_scale, w_pool_up, conv_w, conv_b, dt_bias, a_log, d_skip, ssd_norm_w, w_ssd_out, w_o, mlp_norm_w, w_ff1, w_ff2, final_norm_w, loss_target, m_meta_tokens, m_mix_norm_w, m_w_in, m_b_gate, m_pool_w_group, m_pool_scale, m_w_pool_up, m_conv_w, m_conv_b, m_dt_bias, m_a_log, m_d_skip, m_ssd_norm_w, m_w_ssd_out, m_w_o, m_mlp_norm_w, m_w_ff1, m_w_ff2, m_final_norm_w, v_meta_tokens, v_mix_norm_w, v_w_in, v_b_gate, v_pool_w_group, v_pool_scale, v_w_pool_up, v_conv_w, v_conv_b, v_dt_bias, v_a_log, v_d_skip, v_ssd_norm_w, v_w_ssd_out, v_w_o, v_mlp_norm_w, v_w_ff1, v_w_ff2, v_final_norm_w):
    names = ("meta_tokens",) + PER_LAYER + ("final_norm_w",)
    par = dict(meta_tokens=meta_tokens, mix_norm_w=mix_norm_w, w_in=w_in, b_gate=b_gate, pool_w_group=pool_w_group,
               pool_scale=pool_scale, w_pool_up=w_pool_up, conv_w=conv_w, conv_b=conv_b, dt_bias=dt_bias, a_log=a_log,
               d_skip=d_skip, ssd_norm_w=ssd_norm_w, w_ssd_out=w_ssd_out, w_o=w_o, mlp_norm_w=mlp_norm_w, w_ff1=w_ff1,
               w_ff2=w_ff2, final_norm_w=final_norm_w)
    mom = dict(meta_tokens=m_meta_tokens, mix_norm_w=m_mix_norm_w, w_in=m_w_in, b_gate=m_b_gate, pool_w_group=m_pool_w_group,
               pool_scale=m_pool_scale, w_pool_up=m_w_pool_up, conv_w=m_conv_w, conv_b=m_conv_b, dt_bias=m_dt_bias,
               a_log=m_a_log, d_skip=m_d_skip, ssd_norm_w=m_ssd_norm_w, w_ssd_out=m_w_ssd_out, w_o=m_w_o,
               mlp_norm_w=m_mlp_norm_w, w_ff1=m_w_ff1, w_ff2=m_w_ff2, final_norm_w=m_final_norm_w)
    var = dict(meta_tokens=v_meta_tokens, mix_norm_w=v_mix_norm_w, w_in=v_w_in, b_gate=v_b_gate, pool_w_group=v_pool_w_group,
               pool_scale=v_pool_scale, w_pool_up=v_w_pool_up, conv_w=v_conv_w, conv_b=v_conv_b, dt_bias=v_dt_bias,
               a_log=v_a_log, d_skip=v_d_skip, ssd_norm_w=v_ssd_norm_w, w_ssd_out=v_w_ssd_out, w_o=v_w_o,
               mlp_norm_w=v_mlp_norm_w, w_ff1=v_w_ff1, w_ff2=v_w_ff2, final_norm_w=v_final_norm_w)
    chip = 2 * lax.axis_index("x") + lax.axis_index("y")
    core = lax.axis_index("c")

    big = dict(zip(BIG, _gather_big([par[n].astype(BF16) for n in BIG])))
    small = {n: par[n] for n in REPLICATED}
    small_shapes = [par[n].shape for n in SMALL_SHARDED]
    got_small = _exchange_small(_flatten([par[n] for n in SMALL_SHARDED], F32, 8), False, "gather_small")
    pieces = [_unflatten(got_small[2 * k], small_shapes) for k in range(N_CHIPS)]
    for j, n in enumerate(SMALL_SHARDED):
        small[n] = jnp.concatenate([pieces[k][j] for k in range(N_CHIPS)], axis=SHARD_AXIS[n])

    reducer = _GradReducer(chip, core)
    loss, dx, _, grads = _local_step(x[0], loss_target[0], big, small, reducer)
    loss = lax.psum(loss, ("x", "y", "c"))

    small_names = REPLICATED + SMALL_SHARDED
    red = _exchange_small(_flatten([grads[n] for n in small_names], F32, 8), True, "allreduce_small")
    gsum = dict(zip(small_names, _unflatten(red, [grads[n].shape for n in small_names])))
    for n in SMALL_SHARDED:
        width = par[n].shape[SHARD_AXIS[n]]
        gsum[n] = lax.dynamic_slice_in_dim(gsum[n], chip * width, width, axis=SHARD_AXIS[n])

    gsum.update({n: g.reshape(par[n].shape) for n, g in reducer.finish().items()})

    delta, new_m, new_v = {}, {}, {}
    for n in BIG:
        delta[n], new_m[n], new_v[n] = _adamw(par[n], gsum[n], mom[n], var[n], "adamw_" + n)
    rest = [n for n in names if n not in BIG]
    shapes = [par[n].shape for n in rest]
    packed = [_flatten([d[n] for n in rest], F32, 8) for d in (par, gsum, mom, var)]
    for d, flat in zip((delta, new_m, new_v), _adamw(*packed, "adamw_small")):
        d.update(zip(rest, _unflatten(flat, shapes)))

    order = ("meta_tokens", "mix_norm_w", "w_in", "b_gate", "pool_w_group", "pool_scale", "w_pool_up", "conv_w", "conv_b",
             "dt_bias", "a_log", "d_skip", "ssd_norm_w", "w_ssd_out", "w_o", "mlp_norm_w", "w_ff1", "w_ff2", "final_norm_w")
    return (loss, dx[None], *[gsum[n] for n in order], *[delta[n] for n in order], *[new_m[n] for n in order],
            *[new_v[n] for n in order])
```

```python
import functools

import jax
import jax.numpy as jnp
from jax import lax
from jax.experimental import pallas as pl
from jax.experimental.pallas import tpu as pltpu

F32 = jnp.float32
BF16 = jnp.bfloat16

D_MODEL = 1024
DEPTH = 4
N_META = 16
N_PAD = 112
ROW_X = N_PAD + N_META
POOL_WINDOWS = (2, 4, 8, 16)
POOL_GDIM = 256
D_INNER = 2048
N_HEADS = 32
HEAD_DIM = 64
N_GROUPS = 8
HEADS_PER_GROUP = 4
GROUP_W = HEADS_PER_GROUP * HEAD_DIM
D_STATE = 128
CHUNK = 128
D_XBC = 4096
D_FF = 4096
EPS = 1e-5
OFF_Z, OFF_XBC, OFF_DT, OFF_GATE, IN_COLS = 1024, 3072, 7168, 7200, 9248
PZ, PGATE, PPOOL, PXBC, PCOLS = 0, 2048, 4096, 5120, 9216
DT_PAD = 128

ADAM_LR, ADAM_B1, ADAM_B2, ADAM_EPS, ADAM_WD, ADAM_STEP = 0.001, 0.9, 0.999, 1e-08, 0.01, 10

VMEM_LIMIT = 56 * 1024 * 1024

_NN = (((1,), (0,)), ((), ()))
_NT = (((1,), (1,)), ((), ()))
_TN = (((0,), (0,)), ((), ()))


def _dot(a, b, dn=_NN):
    return lax.dot_general(a, b, dn, preferred_element_type=F32)


def _cparams(sem):
    return pltpu.CompilerParams(dimension_semantics=sem, vmem_limit_bytes=VMEM_LIMIT)


def _tile(n, cands):
    for c in cands:
        if n % c == 0:
            return c
    raise ValueError(f"no tile for {n} in {cands}")


def _sigmoid(x):
    return 1.0 / (1.0 + jnp.exp(-x))


def _softplus(x):
    return jnp.maximum(x, 0.0) + jnp.log(1.0 + jnp.exp(-jnp.abs(x)))


def _bs(shape, fn):
    return pl.BlockSpec(shape, fn)


class _Carry:
    def __init__(self, ins, outs, n_sems, make, alias=None):
        self.ins, self.outs, self.n_sems, self.make = list(ins), list(outs), n_sems, make
        self.alias = dict(alias or {})

    def aliases(self, first_in, first_out):
        return {first_in + i: first_out + o for i, o in self.alias.items()}

    def specs(self):
        hbm = pl.BlockSpec(memory_space=pl.ANY)
        return [hbm] * len(self.ins), [hbm] * len(self.outs), [pltpu.SemaphoreType.DMA((self.n_sems,))] * 2

    def at_edges(self, grid, in_refs, out_refs, sems):
        ids = [pl.program_id(a) for a in range(len(grid))]
        first = functools.reduce(jnp.logical_and, [i == 0 for i in ids])
        last = functools.reduce(jnp.logical_and, [i == n - 1 for i, n in zip(ids, grid)])

        def begin():
            @pl.when(first)
            def _():
                for cp in self.make(in_refs, out_refs, *sems):
                    cp.start()

        def end():
            @pl.when(last)
            def _():
                for cp in self.make(in_refs, out_refs, *sems):
                    cp.wait()

        return begin, end


def _mm(name, dn, grid, a, a_spec, b, b_spec, outs, *, extras=(), epilogue=None, carry=None):
    nk = grid[2]
    ne, no = len(extras), len(outs)
    blk = tuple(d for d in outs[0][2].block_shape if d is not None)
    c_in, c_out, c_sem = carry.specs() if carry is not None else ([], [], [])
    nci, nco = len(c_in), len(c_out)

    def body(a_ref, b_ref, *rest):
        e_refs, o_refs = rest[:ne], rest[ne + nci:ne + nci + no]
        if carry is not None:
            begin, end = carry.at_edges(grid, rest[ne:ne + nci], rest[ne + nci + no:ne + nci + no + nco], rest[-2:])
            begin()
        p = _dot(a_ref[...].astype(BF16), b_ref[...].astype(BF16), dn)

        def finish(acc):
            outs = epilogue(acc, *[e[...] for e in e_refs]) if epilogue is not None else (acc,)
            for o, v in zip(o_refs, outs):
                o[...] = v.astype(o.dtype)

        if nk == 1:
            finish(p)
        else:
            acc_ref = rest[ne + nci + no + nco]
            kk = pl.program_id(2)

            @pl.when(kk == 0)
            def _():
                acc_ref[...] = p

            @pl.when(kk > 0)
            def _():
                acc_ref[...] += p

            @pl.when(kk == nk - 1)
            def _():
                finish(acc_ref[...])
        if carry is not None:
            end()

    res = pl.pallas_call(
        body,
        name=name,
        grid=grid,
        in_specs=[a_spec, b_spec, *[s for _, s in extras]] + c_in,
        out_specs=[o[2] for o in outs] + c_out,
        out_shape=[jax.ShapeDtypeStruct(o[0], o[1]) for o in outs] + (carry.outs if carry is not None else []),
        scratch_shapes=([pltpu.VMEM(blk, F32)] if nk > 1 else []) + c_sem,
        input_output_aliases=carry.aliases(2 + ne, no) if carry is not None else {},
        compiler_params=_cparams(("arbitrary",) * 3 if carry is not None else ("parallel", "parallel", "arbitrary")),
    )(a, b, *[e for e, _ in extras], *(carry.ins if carry is not None else []))
    if carry is not None:
        return res[:no], res[no:]
    return res[0] if no == 1 else res


def _norm_fwd(h, w, name):
    t, d = h.shape
    tm = _tile(t, (1056, 384, 128))

    def body(h_ref, w_ref, u_ref):
        x = h_ref[...]
        r = lax.rsqrt(jnp.mean(x * x, axis=-1, keepdims=True) + EPS)
        u_ref[...] = (x * r * w_ref[...]).astype(BF16)

    return pl.pallas_call(
        body, name=name, grid=(t // tm,),
        in_specs=[pl.BlockSpec((tm, d), lambda i: (i, 0)), pl.BlockSpec((1, d), lambda i: (0, 0))],
        out_specs=pl.BlockSpec((tm, d), lambda i: (i, 0)),
        out_shape=jax.ShapeDtypeStruct((t, d), BF16),
        compiler_params=_cparams(("parallel",)),
    )(h, w.reshape(1, d))


def _norm_bwd(dres, du, h, w, name):
    t, d = h.shape
    tm = _tile(t, (528, 384, 128))

    def body(dres_ref, du_ref, h_ref, w_ref, dh_ref, dw_ref):
        x = h_ref[...]
        r = lax.rsqrt(jnp.mean(x * x, axis=-1, keepdims=True) + EPS)
        xhat = x * r
        du_v = du_ref[...]
        g = du_v * w_ref[...]
        dh_ref[...] = dres_ref[...] + r * (g - xhat * jnp.mean(g * xhat, axis=-1, keepdims=True))

        @pl.when(pl.program_id(0) == 0)
        def _():
            dw_ref[...] = jnp.zeros_like(dw_ref)

        dw_ref[...] += jnp.sum(du_v * xhat, axis=0, keepdims=True)

    row = pl.BlockSpec((tm, d), lambda i: (i, 0))
    vec = pl.BlockSpec((1, d), lambda i: (0, 0))
    dh, dw = pl.pallas_call(
        body, name=name, grid=(t // tm,),
        in_specs=[row, row, row, vec], out_specs=[row, vec],
        out_shape=[jax.ShapeDtypeStruct((t, d), F32), jax.ShapeDtypeStruct((1, d), F32)],
        compiler_params=_cparams(("arbitrary",)),
    )(dres, du, h, w.reshape(1, d))
    return dh, dw.reshape(d)


def _loss_head(h, w, tgt):
    t, d = h.shape
    tm = CHUNK
    nb = ROW_X // tm

    def body(h_ref, w_ref, t_ref, loss_ref, dh_ref, dw_ref):
        i = pl.program_id(0)
        x = h_ref[...]
        r = lax.rsqrt(jnp.mean(x * x, axis=-1, keepdims=True) + EPS)
        xhat = x * r
        wv = w_ref[...]
        live = i >= nb
        err = jnp.where(live, xhat * wv - t_ref[...], 0.0)
        dout = err * (1.0 / d)
        g = dout * wv
        dh_ref[...] = r * (g - xhat * jnp.mean(g * xhat, axis=-1, keepdims=True))

        @pl.when(i == 0)
        def _():
            loss_ref[...] = jnp.zeros_like(loss_ref)
            dw_ref[...] = jnp.zeros_like(dw_ref)

        loss_ref[...] += 0.5 * jnp.sum(jnp.sum(err * err, axis=-1, keepdims=True), axis=0, keepdims=True) * (1.0 / d)
        dw_ref[...] += jnp.sum(dout * xhat, axis=0, keepdims=True)

    row = pl.BlockSpec((tm, d), lambda i: (i, 0))
    vec = pl.BlockSpec((1, d), lambda i: (0, 0))
    loss, dh, dw = pl.pallas_call(
        body, name="loss_head", grid=(t // tm,),
        in_specs=[row, vec, pl.BlockSpec((tm, d), lambda i: (jnp.maximum(i - nb, 0), 0))],
        out_specs=[pl.BlockSpec((1, 1), lambda i: (0, 0)), row, vec],
        out_shape=[jax.ShapeDtypeStruct((1, 1), F32), jax.ShapeDtypeStruct((t, d), F32), jax.ShapeDtypeStruct((1, d), F32)],
        compiler_params=_cparams(("arbitrary",)),
    )(h, w.reshape(1, d), tgt)
    return loss[0, 0], dh, dw.reshape(d)


def _gate_fwd(proj, b_gate, y_pool, y_ssd):
    t = proj.shape[0]
    d = D_MODEL
    tm = _tile(t, (528, 384, 128))

    def body(gp_ref, gs_ref, bp_ref, bs_ref, yp_ref, ys_ref, o_ref):
        gp = _sigmoid(gp_ref[...] + bp_ref[...])
        gs = _sigmoid(gs_ref[...] + bs_ref[...])
        o_ref[...] = (gp * yp_ref[...] + gs * ys_ref[...]).astype(BF16)

    row = pl.BlockSpec((tm, d), lambda i: (i, 0))
    return pl.pallas_call(
        body, name="gate_fwd", grid=(t // tm,),
        in_specs=[pl.BlockSpec((tm, d), lambda i: (i, PGATE // d)), pl.BlockSpec((tm, d), lambda i: (i, PGATE // d + 1)),
                  pl.BlockSpec((1, d), lambda i: (0, 0)), pl.BlockSpec((1, d), lambda i: (0, 1)), row, row],
        out_specs=row, out_shape=jax.ShapeDtypeStruct((t, d), BF16),
        compiler_params=_cparams(("parallel",)),
    )(proj, proj, b_gate.reshape(1, 2 * d), b_gate.reshape(1, 2 * d), y_pool, y_ssd)


def _gate_bwd(dmerged, proj, b_gate, y_pool, y_ssd):
    t = proj.shape[0]
    d = D_MODEL
    tm = _tile(t, (384, 128))

    def body(dm_ref, gp_ref, gs_ref, bp_ref, bs_ref, yp_ref, ys_ref, dyp_ref, dys_ref, dg_ref, db_ref):
        dm = dm_ref[...]
        gp = _sigmoid(gp_ref[...] + bp_ref[...])
        gs = _sigmoid(gs_ref[...] + bs_ref[...])
        dyp_ref[...] = (dm * gp).astype(BF16)
        dys_ref[...] = (dm * gs).astype(BF16)
        dgp = dm * yp_ref[...] * gp * (1.0 - gp)
        dgs = dm * ys_ref[...] * gs * (1.0 - gs)
        dg_ref[:, :d] = dgp.astype(BF16)
        dg_ref[:, d:] = dgs.astype(BF16)

        @pl.when(pl.program_id(0) == 0)
        def _():
            db_ref[...] = jnp.zeros_like(db_ref)

        db_ref[:, :d] += jnp.sum(dgp, axis=0, keepdims=True)
        db_ref[:, d:] += jnp.sum(dgs, axis=0, keepdims=True)

    row = pl.BlockSpec((tm, d), lambda i: (i, 0))
    dyp, dys, dg, db = pl.pallas_call(
        body, name="gate_bwd", grid=(t // tm,),
        in_specs=[row, pl.BlockSpec((tm, d), lambda i: (i, PGATE // d)), pl.BlockSpec((tm, d), lambda i: (i, PGATE // d + 1)),
                  pl.BlockSpec((1, d), lambda i: (0, 0)), pl.BlockSpec((1, d), lambda i: (0, 1)), row, row],
        out_specs=[row, row, pl.BlockSpec((tm, 2 * d), lambda i: (i, PGATE // (2 * d))),
                   pl.BlockSpec((1, 2 * d), lambda i: (0, 0))],
        out_shape=[jax.ShapeDtypeStruct((t, d), BF16), jax.ShapeDtypeStruct((t, d), BF16),
                   jax.ShapeDtypeStruct((t, PCOLS), BF16), jax.ShapeDtypeStruct((1, 2 * d), F32)],
        compiler_params=_cparams(("arbitrary",)),
    )(dmerged, proj, proj, b_gate.reshape(1, 2 * d), b_gate.reshape(1, 2 * d), y_pool, y_ssd)
    return dyp, dys, dg, db.reshape(2 * d)


POOL_HALO = 16


def _pool_counts(row0, n, win):
    pos1 = row0 + lax.broadcasted_iota(jnp.int32, (n, 1), 0) - (N_PAD - 1)
    return jnp.clip(pos1, 1, win).astype(F32)


N_CHIPS = 4
WG_ROWS = POOL_GDIM // N_CHIPS
WUP_ROWS = D_MODEL // N_CHIPS


def _group_w(wg_ref, g):
    return jnp.concatenate([wg_ref[k, g] for k in range(N_CHIPS)], axis=0)


def _pool_w_specs(layer):
    return [pl.BlockSpec((None, N_CHIPS, len(POOL_WINDOWS), WG_ROWS, POOL_GDIM), lambda i: (0, 0, 0, 0, 0)),
            pl.BlockSpec((None, N_CHIPS, WUP_ROWS, D_MODEL), lambda i: (0, 0, 0, 0))]


def _pool_fwd(proj, wg, scale, wup, layer):
    t = proj.shape[0]
    d = D_MODEL
    tm = _tile(t, (384, 128))
    hb = tm // POOL_HALO

    def body(u_ref, halo_ref, sc_ref, wg_ref, wup_ref, pooled_ref, yg_ref, ypm_ref, yp_ref):
        i = pl.program_id(0)
        x = u_ref[...]
        halo = jnp.where(i > 0, halo_ref[...], 0.0)
        xc = jnp.concatenate([halo, x], axis=0)
        for g, win in enumerate(POOL_WINDOWS):
            sl = slice(g * POOL_GDIM, (g + 1) * POOL_GDIM)
            s = xc[:, sl]
            k = 1
            while k < win:
                s = s + pltpu.roll(s, k, axis=0)
                k *= 2
            pooled = s[POOL_HALO:] / _pool_counts(i * tm, tm, win) - x[:, sl]
            pb = pooled.astype(BF16)
            pooled_ref[:, sl] = pb
            yg_ref[:, sl] = _dot(pb, _group_w(wg_ref, g))
        ypm = (yg_ref[...] * sc_ref[...]).astype(BF16)
        ypm_ref[...] = ypm
        acc = _dot(ypm[:, :WUP_ROWS], wup_ref[0])
        for k in range(1, N_CHIPS):
            acc = acc + _dot(ypm[:, k * WUP_ROWS:(k + 1) * WUP_ROWS], wup_ref[k])
        yp_ref[...] = acc

    row = pl.BlockSpec((tm, d), lambda i: (i, 0))
    return pl.pallas_call(
        body, name=f"pool_fwd_{layer}", grid=(t // tm,),
        in_specs=[pl.BlockSpec((tm, d), lambda i: (i, PPOOL // d)),
                  pl.BlockSpec((POOL_HALO, d), lambda i: (jnp.maximum(i * hb - 1, 0), PPOOL // d)),
                  pl.BlockSpec((1, d), lambda i: (0, 0))] + _pool_w_specs(layer),
        out_specs=[row, row, row, row],
        out_shape=[jax.ShapeDtypeStruct((t, d), BF16), jax.ShapeDtypeStruct((t, d), F32),
                   jax.ShapeDtypeStruct((t, d), BF16), jax.ShapeDtypeStruct((t, d), F32)],
        compiler_params=_cparams(("parallel",)),
    )(proj, proj, scale.reshape(1, d), wg, wup)


def _pool_bwd_a(dy_pool, yg, scale, wg, wup, layer):
    t, d = yg.shape
    tm = _tile(t, (384, 128))

    def body(dy_ref, yg_ref, sc_ref, wg_ref, wup_ref, q_ref, dyg_ref, dsc_ref):
        dy = dy_ref[...]
        dypm = jnp.concatenate([_dot(dy, wup_ref[k], _NT) for k in range(N_CHIPS)], axis=1)

        @pl.when(pl.program_id(0) == 0)
        def _():
            dsc_ref[...] = jnp.zeros_like(dsc_ref)

        dsc_ref[...] += jnp.sum(dypm * yg_ref[...], axis=0, keepdims=True)
        dyg = (dypm * sc_ref[...]).astype(BF16)
        dyg_ref[...] = dyg
        for g in range(len(POOL_WINDOWS)):
            sl = slice(g * POOL_GDIM, (g + 1) * POOL_GDIM)
            q_ref[:, sl] = _dot(dyg[:, sl], _group_w(wg_ref, g), _NT)

    row = pl.BlockSpec((tm, d), lambda i: (i, 0))
    vec = pl.BlockSpec((1, d), lambda i: (0, 0))
    q, dyg, dsc = pl.pallas_call(
        body, name=f"pool_bwd_a_{layer}", grid=(t // tm,),
        in_specs=[row, row, vec] + _pool_w_specs(layer),
        out_specs=[row, row, vec],
        out_shape=[jax.ShapeDtypeStruct((t, d), F32), jax.ShapeDtypeStruct((t, d), BF16), jax.ShapeDtypeStruct((1, d), F32)],
        compiler_params=_cparams(("arbitrary",)),
    )(dy_pool, yg, scale.reshape(1, d), wg, wup)
    return q, dyg, dsc.reshape(d)


def _pool_bwd_b(q, dproj):
    t, d = q.shape
    tm = _tile(t, (384, 128))
    hb = tm // POOL_HALO
    nt = t // tm
    n = tm + POOL_HALO

    def body(q_ref, halo_ref, _, o_ref):
        i = pl.program_id(0)
        qv = q_ref[...]
        halo = jnp.where(i < nt - 1, halo_ref[...], 0.0)
        qc = jnp.concatenate([qv, halo], axis=0)
        for g, win in enumerate(POOL_WINDOWS):
            sl = slice(g * POOL_GDIM, (g + 1) * POOL_GDIM)
            s = qc[:, sl] / _pool_counts(i * tm, n, win)
            k = 1
            while k < win:
                s = s + pltpu.roll(s, n - k, axis=0)
                k *= 2
            o_ref[:, sl] = (s[:tm] - qv[:, sl]).astype(BF16)

    row = pl.BlockSpec((tm, d), lambda i: (i, 0))
    return pl.pallas_call(
        body, name="pool_bwd_b", grid=(nt,),
        in_specs=[row, pl.BlockSpec((POOL_HALO, d), lambda i: (jnp.minimum((i + 1) * hb, t // POOL_HALO - 1), 0)),
                  pl.BlockSpec(memory_space=pl.ANY)],
        out_specs=pl.BlockSpec((tm, d), lambda i: (i, PPOOL // d)), out_shape=jax.ShapeDtypeStruct(dproj.shape, BF16),
        input_output_aliases={2: 0},
        compiler_params=_cparams(("parallel",)),
    )(q, q, dproj)


def _pool_dwg(pooled, dyg, layer):
    t, d = pooled.shape
    tk = _tile(t, (1056, 384, 128))
    nk = t // tk
    gd = POOL_GDIM
    ng = d // gd

    def body(p_ref, g_ref, o_ref):
        @pl.when(pl.program_id(1) == 0)
        def _():
            o_ref[...] = jnp.zeros_like(o_ref)

        part = _dot(p_ref[...], g_ref[...], _TN)
        for k in range(N_CHIPS):
            o_ref[k] += part[k * WG_ROWS:(k + 1) * WG_ROWS]

    blk = pl.BlockSpec((tk, gd), lambda g, k: (k, g))
    return pl.pallas_call(
        body, name=f"pool_dwg_{layer}", grid=(ng, nk), in_specs=[blk, blk],
        out_specs=pl.BlockSpec((N_CHIPS, None, WG_ROWS, gd), lambda g, k: (0, g, 0, 0)),
        out_shape=jax.ShapeDtypeStruct((N_CHIPS, ng, WG_ROWS, gd), F32),
        compiler_params=_cparams(("parallel", "arbitrary")),
    )(pooled, dyg)


CONV_W = 4
CONV_HALO = 8
XBC_BLK = PXBC // 1024


def _conv_fwd(proj, conv_w, conv_b):
    t = proj.shape[0]
    cw = 1024
    tm = _tile(t, (1056, 384, 128))
    hb = tm // CONV_HALO

    def body(x_ref, halo_ref, w_ref, b_ref, o_ref):
        i = pl.program_id(1)
        x = x_ref[...]
        halo = jnp.where(i > 0, halo_ref[...], 0.0)
        xc = jnp.concatenate([halo, x], axis=0)
        w = w_ref[...]
        acc = b_ref[...] + x * w[CONV_W - 1:CONV_W, :]
        for k in range(CONV_W - 1):
            acc = acc + pltpu.roll(xc, CONV_W - 1 - k, axis=0)[CONV_HALO:] * w[k:k + 1, :]
        row = i * tm + lax.broadcasted_iota(jnp.int32, (tm, 1), 0)
        o_ref[...] = jnp.where(row >= N_PAD, acc * _sigmoid(acc), 0.0)

    return pl.pallas_call(
        body, name="conv_fwd", grid=(D_XBC // cw, t // tm),
        in_specs=[pl.BlockSpec((tm, cw), lambda j, i: (i, XBC_BLK + j)),
                  pl.BlockSpec((CONV_HALO, cw), lambda j, i: (jnp.maximum(i * hb - 1, 0), XBC_BLK + j)),
                  pl.BlockSpec((CONV_W, cw), lambda j, i: (0, j)), pl.BlockSpec((1, cw), lambda j, i: (0, j))],
        out_specs=pl.BlockSpec((tm, cw), lambda j, i: (i, j)),
        out_shape=jax.ShapeDtypeStruct((t, D_XBC), F32),
        compiler_params=_cparams(("parallel", "parallel")),
    )(proj, proj, conv_w, conv_b.reshape(1, D_XBC))


def _conv_bwd(dxa, coff, proj, conv_w, conv_b, dproj):
    t, ncols = dxa.shape
    cw = 1024
    tm = _tile(t, (528, 384, 128))
    hb = tm // CONV_HALO
    nt = t // tm
    n = tm + 2 * CONV_HALO

    def body(d_ref, dn_ref, xp_ref, x_ref, xn_ref, w_ref, b_ref, _, o_ref, dw_ref, db_ref):
        i = pl.program_id(1)
        last = i == nt - 1
        xf = jnp.concatenate([jnp.where(i > 0, xp_ref[...], 0.0), x_ref[...], jnp.where(last, 0.0, xn_ref[...])], axis=0)
        df = jnp.concatenate([jnp.zeros((CONV_HALO, cw), F32), d_ref[...], jnp.where(last, 0.0, dn_ref[...])], axis=0)
        w = w_ref[...]
        sh = [pltpu.roll(xf, CONV_W - 1 - k, axis=0) if k < CONV_W - 1 else xf for k in range(CONV_W)]
        xc = b_ref[...]
        for k in range(CONV_W):
            xc = xc + sh[k] * w[k:k + 1, :]
        sig = _sigmoid(xc)
        row = i * tm - CONV_HALO + lax.broadcasted_iota(jnp.int32, (n, 1), 0)
        dxc = jnp.where((row >= N_PAD) & (row < t), df * (sig * (1.0 + xc * (1.0 - sig))), 0.0)
        acc = dxc * w[CONV_W - 1:CONV_W, :]
        for k in range(CONV_W - 1):
            acc = acc + pltpu.roll(dxc, n - (CONV_W - 1 - k), axis=0) * w[k:k + 1, :]
        o_ref[...] = acc[CONV_HALO:CONV_HALO + tm].astype(BF16)

        @pl.when(i == 0)
        def _():
            dw_ref[...] = jnp.zeros_like(dw_ref)
            db_ref[...] = jnp.zeros_like(db_ref)

        dm = dxc[CONV_HALO:CONV_HALO + tm]
        db_ref[...] += jnp.sum(dm, axis=0, keepdims=True)
        for k in range(CONV_W):
            dw_ref[k:k + 1, :] += jnp.sum(dm * sh[k][CONV_HALO:CONV_HALO + tm], axis=0, keepdims=True)

    def xspec(rows, fn):
        return pl.BlockSpec((rows, cw), lambda j, i: (fn(i), XBC_BLK + coff + j))

    prev = lambda i: jnp.maximum(i * hb - 1, 0)
    nxt = lambda i: jnp.minimum((i + 1) * hb, t // CONV_HALO - 1)
    dxbc, dw, db = pl.pallas_call(
        body, name=f"conv_bwd_{coff}", grid=(ncols // cw, nt),
        in_specs=[pl.BlockSpec((tm, cw), lambda j, i: (i, j)), pl.BlockSpec((CONV_HALO, cw), lambda j, i: (nxt(i), j)),
                  xspec(CONV_HALO, prev), xspec(tm, lambda i: i), xspec(CONV_HALO, nxt),
                  pl.BlockSpec((CONV_W, cw), lambda j, i: (0, coff + j)), pl.BlockSpec((1, cw), lambda j, i: (0, coff + j)),
                  pl.BlockSpec(memory_space=pl.ANY)],
        out_specs=[pl.BlockSpec((tm, cw), lambda j, i: (i, XBC_BLK + coff + j)), pl.BlockSpec((CONV_W, cw), lambda j, i: (0, j)),
                   pl.BlockSpec((1, cw), lambda j, i: (0, j))],
        out_shape=[jax.ShapeDtypeStruct(dproj.shape, BF16), jax.ShapeDtypeStruct((CONV_W, ncols), F32),
                   jax.ShapeDtypeStruct((1, ncols), F32)],
        input_output_aliases={7: 0},
        compiler_params=_cparams(("parallel", "arbitrary")),
    )(dxa, dxa, proj, proj, proj, conv_w, conv_b.reshape(1, D_XBC), dproj)
    return dxbc, dw, db.reshape(ncols)


def _cumsum(x, axis, reverse=False):
    n = x.shape[axis]
    idx = lax.broadcasted_iota(jnp.int32, x.shape, axis)
    k = 1
    while k < n:
        if reverse:
            x = x + jnp.where(idx < n - k, pltpu.roll(x, n - k, axis=axis), 0.0)
        else:
            x = x + jnp.where(idx >= k, pltpu.roll(x, k, axis=axis), 0.0)
        k *= 2
    return x


def _head_masks():
    lane = lax.broadcasted_iota(jnp.int32, (1, GROUP_W), 1)
    return [(lane >= r * HEAD_DIM) & (lane < (r + 1) * HEAD_DIM) for r in range(HEADS_PER_GROUP)]


def _expand_heads(cols, hm):
    out = jnp.where(hm[0], cols[:, 0:1], 0.0)
    for r in range(1, HEADS_PER_GROUP):
        out = out + jnp.where(hm[r], cols[:, r:r + 1], 0.0)
    return out


def _ssd_decay(dt_raw, dt_bias, a_log):
    t = dt_raw.shape[0]

    def body(raw_ref, b_ref, al_ref, dt_ref, acs_ref, sig_ref):
        raw = raw_ref[...] + b_ref[...]
        rowid = pl.program_id(0) * CHUNK + lax.broadcasted_iota(jnp.int32, (CHUNK, 1), 0)
        dt = jnp.where(rowid >= N_PAD, _softplus(raw), 0.0)
        dt_ref[...] = dt
        acs_ref[...] = _cumsum(dt * -jnp.exp(al_ref[...]), 0)
        sig_ref[...] = _sigmoid(raw)

    blk = pl.BlockSpec((CHUNK, DT_PAD), lambda c: (c, 0))
    vec = pl.BlockSpec((1, DT_PAD), lambda c: (0, 0))
    pad = lambda v: jnp.pad(v, (0, DT_PAD - N_HEADS)).reshape(1, DT_PAD)
    dt, acs, sig = pl.pallas_call(
        body, name="ssd_decay", grid=(t // CHUNK,), in_specs=[blk, vec, vec], out_specs=[blk, blk, blk],
        out_shape=[jax.ShapeDtypeStruct((t, DT_PAD), F32)] * 3,
        compiler_params=_cparams(("parallel",)),
    )(dt_raw, pad(dt_bias), pad(a_log))
    grp = lambda v: v[:, :N_HEADS].reshape(t, N_GROUPS, HEADS_PER_GROUP).transpose(1, 0, 2)
    acs_g = grp(acs)
    return dict(dt=grp(dt), acs=acs_g, acs_rows=acs_g.transpose(0, 2, 1), sig=grp(sig),
                a_log=a_log.reshape(N_GROUPS, 1, HEADS_PER_GROUP))


def _ssd_q(dtc, acs4, acs_r):
    hm = _head_masks()
    dt_exp = _expand_heads(dtc, hm)
    acs = _expand_heads(acs4, hm)
    atot = acs[CHUNK - 1:CHUNK, :]
    return dict(dtc=dtc, hm=hm, dt_exp=dt_exp, acs=acs, acs_r=acs_r,
                ea=jnp.exp(acs), ds=jnp.exp(atot - acs), dec=jnp.exp(atot))


def _stack4(x):
    return jnp.concatenate([x] * HEADS_PER_GROUP, axis=0)


def _ssd_decay_stacks(q):
    hpg = HEADS_PER_GROUP
    a_col = jnp.concatenate([q["acs"][:, r * HEAD_DIM:r * HEAD_DIM + 1] for r in range(hpg)], axis=0)
    a_row = jnp.concatenate([jnp.broadcast_to(q["acs_r"][r:r + 1, :], (CHUNK, CHUNK)) for r in range(hpg)], axis=0)
    ri = lax.broadcasted_iota(jnp.int32, (hpg * CHUNK, CHUNK), 0) % CHUNK
    ci = lax.broadcasted_iota(jnp.int32, (hpg * CHUNK, CHUNK), 1)
    diff = a_col - a_row
    lm = jnp.exp(jnp.where(ri >= ci, diff, -jnp.inf))
    lt = jnp.exp(jnp.where(ri <= ci, -diff, -jnp.inf))
    return lm, lt


def _pick_heads(stacked, hm):
    out = jnp.where(hm[0], stacked[:CHUNK], 0.0)
    for r in range(1, HEADS_PER_GROUP):
        out = out + jnp.where(hm[r], stacked[r * CHUNK:(r + 1) * CHUNK], 0.0)
    return out


def _mask_heads(x, hm):
    return jnp.concatenate([jnp.where(hm[r], x, 0.0) for r in range(HEADS_PER_GROUP)], axis=0)


SSD_GP = 4


def _ssd_specs(cidx):
    gp, hpg = SSD_GP, HEADS_PER_GROUP
    return [
        pl.BlockSpec((CHUNK, gp * GROUP_W), lambda g, c: (cidx(c), g)),
        pl.BlockSpec((CHUNK, gp * D_STATE), lambda g, c: (cidx(c), D_INNER // (gp * D_STATE) + g)),
        pl.BlockSpec((CHUNK, gp * D_STATE), lambda g, c: (cidx(c), (D_INNER + 1024) // (gp * D_STATE) + g)),
        pl.BlockSpec((gp, CHUNK, hpg), lambda g, c: (g, cidx(c), 0)),
        pl.BlockSpec((gp, CHUNK, hpg), lambda g, c: (g, cidx(c), 0)),
        pl.BlockSpec((gp, hpg, CHUNK), lambda g, c: (g, 0, cidx(c))),
    ]


def _ssd_fwd(xa, dec, carry=None):
    t = xa.shape[0]
    nc = t // CHUNK
    gp, gw, ds = SSD_GP, GROUP_W, D_STATE
    grid = (N_GROUPS // gp, nc)
    c_in, c_out, c_sem = carry.specs() if carry is not None else ([], [], [])
    nci, nco = len(c_in), len(c_out)

    def body(*refs):
        xs_ref, b_ref, c_ref, dt_ref, acs_ref, acsr_ref = refs[:6]
        y_ref, prev_ref = refs[6 + nci:8 + nci]
        st_ref = refs[8 + nci + nco]
        if carry is not None:
            begin, end = carry.at_edges(grid, refs[6:6 + nci], refs[8 + nci:8 + nci + nco], refs[-2:])
            begin()

        @pl.when(pl.program_id(1) == 0)
        def _():
            st_ref[...] = jnp.zeros_like(st_ref)

        for k in range(gp):
            q = _ssd_q(dt_ref[k], acs_ref[k], acsr_ref[k])
            xdt = xs_ref[:, k * gw:(k + 1) * gw] * q["dt_exp"]
            bm = b_ref[:, k * ds:(k + 1) * ds].astype(BF16)
            cm = c_ref[:, k * ds:(k + 1) * ds].astype(BF16)
            cb = _dot(cm, bm, _NT)
            st = st_ref[k]
            prev_ref[0, k] = st
            lm, _ = _ssd_decay_stacks(q)
            y_diag = _pick_heads(_dot((_stack4(cb) * lm).astype(BF16), xdt.astype(BF16)), q["hm"])
            y_ref[:, k * gw:(k + 1) * gw] = y_diag + _dot(cm, st.astype(BF16)) * q["ea"]
            st_ref[k] = q["dec"] * st + _dot(bm, (xdt * q["ds"]).astype(BF16), _TN)
        if carry is not None:
            end()

    res = pl.pallas_call(
        body, name="ssd_fwd", grid=grid,
        in_specs=_ssd_specs(lambda c: c) + c_in,
        out_specs=[pl.BlockSpec((CHUNK, gp * gw), lambda g, c: (c, g)),
                   pl.BlockSpec((1, gp, ds, gw), lambda g, c: (c, g, 0, 0))] + c_out,
        out_shape=[jax.ShapeDtypeStruct((t, D_INNER), F32), jax.ShapeDtypeStruct((nc, N_GROUPS, ds, gw), F32)]
        + (carry.outs if carry is not None else []),
        scratch_shapes=[pltpu.VMEM((gp, ds, gw), F32)] + c_sem,
        input_output_aliases=carry.aliases(6, 2) if carry is not None else {},
        compiler_params=_cparams(("arbitrary", "arbitrary") if carry is not None else ("parallel", "arbitrary")),
    )(xa, xa, xa, dec["dt"], dec["acs"], dec["acs_rows"], *(carry.ins if carry is not None else []))
    return res[0], res[1], res[2:]


def _ssd_bwd(dy, dxs_skip, xa, prev, dec, carry=None):
    t = xa.shape[0]
    nc = t // CHUNK
    gp, gw, dstate = SSD_GP, GROUP_W, D_STATE
    hpg = HEADS_PER_GROUP

    grid = (N_GROUPS // gp, nc)
    c_in, c_out, c_sem = carry.specs() if carry is not None else ([], [], [])
    nci, nco = len(c_in), len(c_out)

    def body(*refs):
        ins, c_ins = refs[:11], refs[11:11 + nci]
        outs = refs[11 + nci:17 + nci]
        c_outs = refs[17 + nci:17 + nci + nco]
        dst_ref = refs[17 + nci + nco]
        dbias_ref, dalog_ref = outs[4], outs[5]
        if carry is not None:
            begin, end = carry.at_edges(grid, c_ins, c_outs, refs[-2:])
            begin()

        @pl.when(pl.program_id(1) == 0)
        def _():
            dst_ref[...] = jnp.zeros_like(dst_ref)
            dbias_ref[...] = jnp.zeros_like(dbias_ref)
            dalog_ref[...] = jnp.zeros_like(dalog_ref)

        for k in range(gp):
            one_group(k, *ins, *outs, dst_ref)
        if carry is not None:
            end()

    def one_group(k, xs_ref, b_ref, c_ref, dt_ref, acs_ref, acsr_ref, sig_ref, al_ref, dy_ref, sk_ref, prev_ref,
                  dxs_ref, db_ref, dc_ref, ddt_ref, dbias_ref, dalog_ref, dst_ref):
        wide = slice(k * gw, (k + 1) * gw)
        narrow = slice(k * dstate, (k + 1) * dstate)
        q = _ssd_q(dt_ref[k], acs_ref[k], acsr_ref[k])
        a_r = -jnp.exp(al_ref[k])
        hm, ds, dec, dt_exp = q["hm"], q["ds"], q["dec"], q["dt_exp"]
        xs = xs_ref[:, wide]
        xdt = xs * dt_exp
        xdtb = xdt.astype(BF16)
        bm = b_ref[:, narrow].astype(BF16)
        cm = c_ref[:, narrow].astype(BF16)
        cb = _dot(cm, bm, _NT)
        bc = _dot(bm, cm, _NT)
        dyv = dy_ref[:, wide]
        dye = (dyv * q["ea"]).astype(BF16)
        pst = prev_ref[0, k]
        dst = dst_ref[k]
        dstb = dst.astype(BF16)
        dx_state = ds * _dot(bm, dstb)
        onehot = [(lax.broadcasted_iota(jnp.int32, (1, hpg), 1) == r).astype(F32) for r in range(hpg)]
        lm, lt = _ssd_decay_stacks(q)
        dyb = dyv.astype(BF16)
        gl = _dot(_mask_heads(dyv, hm).astype(BF16), xdtb, _NT) * lm
        glt = _dot(_mask_heads(xdt, hm).astype(BF16), dyb, _NT) * lt
        bc4 = _stack4(bc)
        dxdt = dx_state + _pick_heads(_dot((bc4 * lt).astype(BF16), dyb), hm)
        wd = jnp.sum(gl * _stack4(cb), axis=1, keepdims=True) - jnp.sum(glt * bc4, axis=1, keepdims=True)
        dcb = gl[:CHUNK]
        dcbt = glt[:CHUNK]
        qa = wd[:CHUNK] * onehot[0]
        for r in range(1, hpg):
            dcb = dcb + gl[r * CHUNK:(r + 1) * CHUNK]
            dcbt = dcbt + glt[r * CHUNK:(r + 1) * CHUNK]
            qa = qa + wd[r * CHUNK:(r + 1) * CHUNK] * onehot[r]
        pstb = pst.astype(BF16)
        dc_ref[:, narrow] = _dot(dcb.astype(BF16), bm) + _dot(dye, pstb, _NT)
        db_ref[:, narrow] = _dot(dcbt.astype(BF16), cm) + _dot((xdt * ds).astype(BF16), dstb, _NT)
        dst_ref[k] = dec * dst + _dot(cm, dye, _TN)
        dxs_ref[:, wide] = dxdt * dt_exp + sk_ref[:, wide]

        t2 = xdt * dx_state
        t1 = dyv * (_dot(cm, pstb) * q["ea"]) - t2
        t3 = dst * pst
        t4 = dxdt * xs
        xd = jnp.zeros((CHUNK, hpg), F32)
        dal = jnp.zeros((1, hpg), F32)
        for r in range(hpg):
            qa = qa + jnp.sum(jnp.where(hm[r], t1, 0.0), axis=1, keepdims=True) * onehot[r]
            xd = xd + jnp.sum(jnp.where(hm[r], t4, 0.0), axis=1, keepdims=True) * onehot[r]
            s2 = jnp.sum(jnp.sum(jnp.where(hm[r], t2, 0.0), axis=1, keepdims=True), axis=0, keepdims=True)
            s3 = jnp.sum(jnp.sum(jnp.where(hm[r], t3, 0.0), axis=1, keepdims=True), axis=0, keepdims=True)
            dal = dal + (s2 + dec[:, r * HEAD_DIM:r * HEAD_DIM + 1] * s3) * onehot[r]
        rc = _cumsum(qa, 0, reverse=True) + dal
        rowid = (nc - 1 - pl.program_id(1)) * CHUNK + lax.broadcasted_iota(jnp.int32, (CHUNK, 1), 0)
        ddt_raw = jnp.where(rowid >= N_PAD, (rc * a_r + xd) * sig_ref[k], 0.0)
        ddt_ref[k] = ddt_raw
        dbias_ref[k] += jnp.sum(ddt_raw, axis=0, keepdims=True)
        dalog_ref[k] += jnp.sum(rc * q["dtc"], axis=0, keepdims=True) * a_r

    rev = lambda c: nc - 1 - c
    blk = pl.BlockSpec((CHUNK, gp * gw), lambda g, c: (rev(c), g))
    nblk = pl.BlockSpec((CHUNK, gp * dstate), lambda g, c: (rev(c), g))
    cols = pl.BlockSpec((gp, CHUNK, hpg), lambda g, c: (g, rev(c), 0))
    small = pl.BlockSpec((gp, 1, hpg), lambda g, c: (g, 0, 0))
    res = pl.pallas_call(
        body, name="ssd_bwd", grid=grid,
        in_specs=_ssd_specs(rev) + [cols, small, blk, blk, pl.BlockSpec((1, gp, dstate, gw), lambda g, c: (rev(c), g, 0, 0))]
        + c_in,
        out_specs=[blk, nblk, nblk, cols, small, small] + c_out,
        out_shape=[jax.ShapeDtypeStruct((t, D_INNER), F32), jax.ShapeDtypeStruct((t, N_GROUPS * dstate), F32),
                   jax.ShapeDtypeStruct((t, N_GROUPS * dstate), F32), jax.ShapeDtypeStruct((N_GROUPS, t, hpg), F32),
                   jax.ShapeDtypeStruct((N_GROUPS, 1, hpg), F32), jax.ShapeDtypeStruct((N_GROUPS, 1, hpg), F32)]
        + (carry.outs if carry is not None else []),
        scratch_shapes=[pltpu.VMEM((gp, dstate, gw), F32)] + c_sem,
        compiler_params=_cparams(("arbitrary", "arbitrary") if carry is not None else ("parallel", "arbitrary")),
    )(xa, xa, xa, dec["dt"], dec["acs"], dec["acs_rows"], dec["sig"], dec["a_log"], dy, dxs_skip, prev,
      *(carry.ins if carry is not None else []))
    dxs, db, dc, ddt, dbias, dalog = res[:6]
    ddt_raw = ddt.transpose(1, 0, 2).reshape(t, N_HEADS)
    return dxs, db, dc, ddt_raw, dbias.reshape(N_HEADS), dalog.reshape(N_HEADS), res[6:]


def _ssd_post_fwd(y, xa, proj, dsk, nw):
    t = y.shape[0]
    di = D_INNER
    tm = _tile(t, (384, 128))

    def body(y_ref, xs_ref, z_ref, dsk_ref, nw_ref, o_ref):
        z = z_ref[...]
        yz = (y_ref[...] + xs_ref[...] * dsk_ref[...]) * (z * _sigmoid(z))
        nwv = nw_ref[...]
        for g in range(N_GROUPS):
            sl = slice(g * GROUP_W, (g + 1) * GROUP_W)
            v = yz[:, sl]
            rg = lax.rsqrt(jnp.mean(v * v, axis=-1, keepdims=True) + EPS)
            o_ref[:, sl] = (v * rg * nwv[:, sl]).astype(BF16)

    row = pl.BlockSpec((tm, di), lambda i: (i, 0))
    vec = pl.BlockSpec((1, di), lambda i: (0, 0))
    return pl.pallas_call(
        body, name="ssd_post_fwd", grid=(t // tm,),
        in_specs=[row, row, row, vec, vec], out_specs=row, out_shape=jax.ShapeDtypeStruct((t, di), BF16),
        compiler_params=_cparams(("parallel",)),
    )(y, xa, proj, dsk.reshape(1, di), nw.reshape(1, di))


def _ssd_post_bwd(dyn, y, xa, proj, dsk, nw, dproj):
    t = y.shape[0]
    di = D_INNER
    tm = CHUNK

    def body(dyn_ref, y_ref, xs_ref, z_ref, dsk_ref, nw_ref, _, dy_ref, dsk_x_ref, dz_ref, dnw_ref, ddsk_ref):
        @pl.when(pl.program_id(0) == 0)
        def _():
            dnw_ref[...] = jnp.zeros_like(dnw_ref)
            ddsk_ref[...] = jnp.zeros_like(ddsk_ref)

        for g in range(N_GROUPS):
            sl = slice(g * GROUP_W, (g + 1) * GROUP_W)
            z = z_ref[:, sl]
            sig = _sigmoid(z)
            s = z * sig
            xs = xs_ref[:, sl]
            dskv = dsk_ref[:, sl]
            yt = y_ref[:, sl] + xs * dskv
            yz = yt * s
            rg = lax.rsqrt(jnp.mean(yz * yz, axis=-1, keepdims=True) + EPS)
            xhat = yz * rg
            dynv = dyn_ref[:, sl]
            gg = dynv * nw_ref[:, sl]
            dnw_ref[:, sl] += jnp.sum(dynv * xhat, axis=0, keepdims=True)
            dyz = rg * (gg - xhat * jnp.mean(gg * xhat, axis=-1, keepdims=True))
            dyt = dyz * s
            dy_ref[:, sl] = dyt
            dsk_x_ref[:, sl] = dyt * dskv
            dz_ref[:, sl] = (dyz * yt * (sig * (1.0 + z * (1.0 - sig)))).astype(BF16)
            ddsk_ref[:, sl] += jnp.sum(dyt * xs, axis=0, keepdims=True)

    row = pl.BlockSpec((tm, di), lambda i: (i, 0))
    vec = pl.BlockSpec((1, di), lambda i: (0, 0))
    dy, dxsk, dz, dnw, ddsk = pl.pallas_call(
        body, name="ssd_post_bwd", grid=(t // tm,),
        in_specs=[row, row, row, row, vec, vec, pl.BlockSpec(memory_space=pl.ANY)], out_specs=[row, row, row, vec, vec],
        out_shape=[jax.ShapeDtypeStruct((t, di), F32), jax.ShapeDtypeStruct((t, di), F32), jax.ShapeDtypeStruct(dproj.shape, BF16),
                   jax.ShapeDtypeStruct((1, di), F32), jax.ShapeDtypeStruct((1, di), F32)],
        input_output_aliases={6: 2},
        compiler_params=_cparams(("arbitrary",)),
    )(dyn, y, xa, proj, dsk.reshape(1, di), nw.reshape(1, di), dproj)
    return dy, dxsk, dz, dnw.reshape(di), ddsk.reshape(N_HEADS, HEAD_DIM).sum(axis=1)


SHARD_COLS = IN_COLS // N_CHIPS


def _split_w_in(w_in_sh):
    nl = w_in_sh.shape[0]
    w = w_in_sh.transpose(0, 2, 1, 3).reshape(nl, D_MODEL, IN_COLS)
    main = jnp.concatenate([w[..., OFF_Z:OFF_XBC], w[..., OFF_GATE:], w[..., :OFF_Z], w[..., OFF_XBC:OFF_DT]], axis=-1)
    dt = jnp.pad(w[..., OFF_DT:OFF_GATE], ((0, 0), (0, 0), (0, DT_PAD - N_HEADS)))
    return main, dt


def _merge_dw_in(dmain, ddt):
    full = jnp.concatenate([dmain[:, PPOOL:PXBC], dmain[:, PZ:PGATE], dmain[:, PXBC:], ddt[:, :N_HEADS], dmain[:, PGATE:PPOOL]],
                           axis=1)
    return full.reshape(D_MODEL, N_CHIPS, SHARD_COLS).transpose(1, 0, 2)


def _relu2_epilogue(acc):
    hid = jnp.maximum(acc, 0.0)
    return acc, hid * hid


def _relu2_bwd_epilogue(acc, pre):
    return (acc * 2.0 * jnp.maximum(pre, 0.0),)


def _add_epilogue(acc, res):
    return (acc + res,)


def _w_spec(rows, cols, fn):
    return pl.BlockSpec((None, None, rows, cols), fn)


def _with_carry(stage, call):
    carry = stage[0]() if stage is not None else None
    res = call(carry)
    if carry is None:
        return res
    outs, got = res
    stage[1](got)
    return outs[0] if len(outs) == 1 else outs


def _layer_fwd(h, w, big, layer, stages=None):
    t = h.shape[0]
    d = D_MODEL
    tm = _tile(t, (1408, 384, 128))
    nt = t // tm
    row = _bs((tm, d), lambda i, j, k: (i, 0))
    stages = stages or {}
    s = {"h": h}
    u = _norm_fwd(h, w["mix_norm_w"], "norm_mix")
    proj = _with_carry(stages.get("a_ici"), lambda carry: _mm(
        f"mm_proj_{layer}", _NN, (nt, PCOLS // d, 1), u, row,
        big["w_in_main"], _bs((None, d, d), lambda i, j, k: (0, 0, j)),
        [((t, PCOLS), F32, _bs((tm, d), lambda i, j, k: (i, j)))], carry=carry))
    dt_raw = _mm(f"mm_dt_{layer}", _NN, (nt, 1, 1), u, row,
                 big["w_in_dt"], _bs((None, d, DT_PAD), lambda i, j, k: (0, 0, 0)),
                 [((t, DT_PAD), F32, _bs((tm, DT_PAD), lambda i, j, k: (i, 0)))])
    pooled, yg, ypm, y_pool = _pool_fwd(proj, big["pool_w_group"], w["pool_scale"], big["w_pool_up"], layer)
    xa = _conv_fwd(proj, w["conv_w"], w["conv_b"])
    dec = _ssd_decay(dt_raw, w["dt_bias"], w["a_log"])
    stage = stages.get("b_ici")
    carry = stage[0]() if stage is not None else None
    y, prev, got = _ssd_fwd(xa, dec, carry)
    if carry is not None:
        stage[1](got)
    yn = _ssd_post_fwd(y, xa, proj, w["d_skip_exp"], w["ssd_norm_w"])
    rs = D_INNER // N_CHIPS
    y_ssd = _with_carry(stages.get("a_d2d"), lambda carry: _mm(
        f"mm_ssd_out_{layer}", _NN, (nt, 1, N_CHIPS), yn, _bs((tm, rs), lambda i, j, k: (i, k)),
        big["w_ssd_out"], _w_spec(rs, d, lambda i, j, k: (0, k, 0, 0)), [((t, d), F32, row)], carry=carry))
    merged = _gate_fwd(proj, w["b_gate"], y_pool, y_ssd)
    ro = d // N_CHIPS
    h1 = _mm(f"mm_o_{layer}", _NN, (nt, 1, N_CHIPS), merged, _bs((tm, ro), lambda i, j, k: (i, k)),
             big["w_o"], _w_spec(ro, d, lambda i, j, k: (0, k, 0, 0)), [((t, d), F32, row)],
             extras=[(h, row)], epilogue=_add_epilogue)
    v = _norm_fwd(h1, w["mlp_norm_w"], "norm_mlp")
    tile = _bs((tm, d), lambda i, j, k: (i, j))
    pre, act = _with_carry(stages.get("b_d2d"), lambda carry: _mm(
        f"mm_ff1_{layer}", _NN, (nt, N_CHIPS, 1), v, row,
        big["w_ff1"], _w_spec(d, d, lambda i, j, k: (0, j, 0, 0)),
        [((t, D_FF), F32, tile), ((t, D_FF), BF16, tile)], epilogue=_relu2_epilogue, carry=carry))
    h2 = _mm(f"mm_ff2_{layer}", _NN, (nt, 1, N_CHIPS), act, _bs((tm, d), lambda i, j, k: (i, k)),
             big["w_ff2"], _w_spec(d, d, lambda i, j, k: (0, k, 0, 0)), [((t, d), F32, row)],
             extras=[(h1, row)], epilogue=_add_epilogue)
    s.update(u=u, proj=proj, dec=dec, pooled=pooled, yg=yg, ypm=ypm, y_pool=y_pool, xa=xa, y=y, prev=prev, yn=yn,
             y_ssd=y_ssd, merged=merged, h1=h1, v=v, pre=pre, act=act)
    return h2, s


def _dw(name, layer, a, b, shard_shape, by, tk):
    t = a.shape[0]
    nk = t // tk
    rows, cols = shard_shape
    if by == "rows":
        grid = (N_CHIPS, 1, nk)
        a_spec = _bs((tk, rows), lambda i, j, k: (k, i))
        b_spec = _bs((tk, cols), lambda i, j, k: (k, 0))
        o_spec = _bs((None, rows, cols), lambda i, j, k: (i, 0, 0))
    else:
        grid = (1, N_CHIPS, nk)
        a_spec = _bs((tk, rows), lambda i, j, k: (k, 0))
        b_spec = _bs((tk, cols), lambda i, j, k: (k, j))
        o_spec = _bs((None, rows, cols), lambda i, j, k: (j, 0, 0))
    return _mm(f"{name}_{layer}", _TN, grid, a, a_spec, b, b_spec, [((N_CHIPS, rows, cols), F32, o_spec)])


def _layer_bwd(dh, s, w, big, layer, red=None):
    t = dh.shape[0]
    d = D_MODEL
    tm = _tile(t, (1408, 384, 128))
    tk = _tile(t, (1408, 384, 128))
    nt = t // tm
    row = _bs((tm, d), lambda i, j, k: (i, 0))
    tile = _bs((tm, d), lambda i, j, k: (i, j))
    g, gb = {}, {}
    carry = red.swap_carry() if red is not None else None
    dpre = _mm(f"mm_dact_{layer}", _NT, (nt, N_CHIPS, 1), dh, row,
               big["w_ff2"], _w_spec(d, d, lambda i, j, k: (0, j, 0, 0)), [((t, D_FF), BF16, tile)],
               extras=[(s["pre"], tile)], epilogue=_relu2_bwd_epilogue, carry=carry)
    if carry is not None:
        (dpre,), got = dpre
        red.after_swap(got)
    gb["w_ff2"] = _dw("mm_dw_ff2", layer, s["act"], dh, (d, d), "rows", tk)
    dv = _mm(f"mm_dv_{layer}", _NT, (nt, 1, N_CHIPS), dpre, _bs((tm, d), lambda i, j, k: (i, k)),
             big["w_ff1"], _w_spec(d, d, lambda i, j, k: (0, k, 0, 0)), [((t, d), F32, row)])
    gb["w_ff1"] = _dw("mm_dw_ff1", layer, s["v"], dpre, (d, d), "cols", tk)
    dh1, g["mlp_norm_w"] = _norm_bwd(dh, dv, s["h1"], w["mlp_norm_w"], "norm_mlp_bwd")
    ro = d // N_CHIPS
    dmerged = _mm(f"mm_dmerged_{layer}", _NT, (nt, N_CHIPS, 1), dh1, row,
                  big["w_o"], _w_spec(ro, d, lambda i, j, k: (0, j, 0, 0)),
                  [((t, d), F32, _bs((tm, ro), lambda i, j, k: (i, j)))])
    gb["w_o"] = _dw("mm_dw_o", layer, s["merged"], dh1, (ro, d), "rows", tk)
    dy_pool, dy_ssd, dproj, g["b_gate"] = _gate_bwd(dmerged, s["proj"], w["b_gate"], s["y_pool"], s["y_ssd"])
    rs = D_INNER // N_CHIPS
    dyn = _mm(f"mm_dyn_{layer}", _NT, (nt, N_CHIPS, 1), dy_ssd, row,
              big["w_ssd_out"], _w_spec(rs, d, lambda i, j, k: (0, j, 0, 0)),
              [((t, D_INNER), F32, _bs((tm, rs), lambda i, j, k: (i, j)))])
    gb["w_ssd_out"] = _dw("mm_dw_ssd_out", layer, s["yn"], dy_ssd, (rs, d), "rows", tk)
    dy, dxs_skip, dproj, g["ssd_norm_w"], g["d_skip"] = _ssd_post_bwd(dyn, s["y"], s["xa"], s["proj"], w["d_skip_exp"],
                                                                      w["ssd_norm_w"], dproj)
    carry = red.ici_carry() if red is not None else None
    dxs, db, dc, ddt_raw, g["dt_bias"], g["a_log"], got = _ssd_bwd(dy, dxs_skip, s["xa"], s["prev"], s["dec"], carry)
    if carry is not None:
        red.after_ici(got)
    dproj, dcw1, dcb1 = _conv_bwd(dxs, 0, s["proj"], w["conv_w"], w["conv_b"], dproj)
    dproj, dcw2, dcb2 = _conv_bwd(db, 2, s["proj"], w["conv_w"], w["conv_b"], dproj)
    dproj, dcw3, dcb3 = _conv_bwd(dc, 3, s["proj"], w["conv_w"], w["conv_b"], dproj)
    g["conv_w"] = jnp.concatenate([dcw1, dcw2, dcw3], axis=1)
    g["conv_b"] = jnp.concatenate([dcb1, dcb2, dcb3])
    q, dyg, g["pool_scale"] = _pool_bwd_a(dy_pool, s["yg"], w["pool_scale"], big["pool_w_group"], big["w_pool_up"], layer)
    gb["w_pool_up"] = _dw("mm_dw_pool_up", layer, s["ypm"], dy_pool, (ro, d), "rows", tk)
    gb["pool_w_group"] = _pool_dwg(s["pooled"], dyg, layer).reshape(N_CHIPS, POOL_GDIM, POOL_GDIM)
    dproj = _pool_bwd_b(q, dproj)
    ddt = jnp.pad(ddt_raw.astype(BF16), ((0, 0), (0, DT_PAD - N_HEADS)))
    nk = PCOLS // d
    du = _mm(f"mm_du_{layer}", _NT, (nt, 1, nk), dproj, _bs((tm, d), lambda i, j, k: (i, k)),
             big["w_in_main"], _bs((None, d, d), lambda i, j, k: (0, 0, k)), [((t, d), F32, row)])
    du = _mm(f"mm_du_dt_{layer}", _NT, (nt, 1, 1), ddt, _bs((tm, DT_PAD), lambda i, j, k: (i, 0)),
             big["w_in_dt"], _bs((None, d, DT_PAD), lambda i, j, k: (0, 0, 0)), [((t, d), F32, row)],
             extras=[(du, row)], epilogue=_add_epilogue)
    ntk = t // tk
    u_spec = _bs((tk, d), lambda i, j, k: (k, 0))
    dmain = _mm(f"mm_dw_in_{layer}", _TN, (1, nk, ntk), s["u"], u_spec, dproj, _bs((tk, d), lambda i, j, k: (k, j)),
                [((d, PCOLS), F32, _bs((d, d), lambda i, j, k: (0, j)))])
    ddtw = _mm(f"mm_dw_dt_{layer}", _TN, (1, 1, ntk), s["u"], u_spec, ddt, _bs((tk, DT_PAD), lambda i, j, k: (k, 0)),
               [((d, DT_PAD), F32, _bs((d, DT_PAD), lambda i, j, k: (0, 0)))])
    gb["w_in"] = _merge_dw_in(dmain, ddtw)
    dh0, g["mix_norm_w"] = _norm_bwd(dh1, du, s["h"], w["mix_norm_w"], "norm_mix_bwd")
    return dh0, g, gb


BIG = ("w_in", "pool_w_group", "w_pool_up", "w_ssd_out", "w_o", "w_ff1", "w_ff2")
PER_LAYER = ("mix_norm_w", "w_in", "b_gate", "pool_w_group", "pool_scale", "w_pool_up", "conv_w", "conv_b", "dt_bias",
             "a_log", "d_skip", "ssd_norm_w", "w_ssd_out", "w_o", "mlp_norm_w", "w_ff1", "w_ff2")


SMALL_PER_LAYER = tuple(n for n in PER_LAYER if n not in BIG)


SHARD_SHAPE = {"w_in": (1024, SHARD_COLS), "pool_w_group": (4, WG_ROWS, POOL_GDIM), "w_pool_up": (WUP_ROWS, D_MODEL),
               "w_ssd_out": (D_INNER // N_CHIPS, D_MODEL), "w_o": (D_MODEL // N_CHIPS, D_MODEL),
               "w_ff1": (D_MODEL, D_FF // N_CHIPS), "w_ff2": (D_FF // N_CHIPS, D_MODEL)}


def _layer_view(mats):
    big = {n: mats[n].reshape((1, N_CHIPS) + SHARD_SHAPE[n]) for n in BIG if n != "w_in"}
    big["w_in_main"], big["w_in_dt"] = _split_w_in(mats["w_in"])
    return big


class _HeldWeights:
    def __init__(self, stacked):
        self.stacked = stacked

    def layer(self, i):
        return _layer_view({n: self.stacked[n][i:i + 1].reshape(1, N_CHIPS, -1, self.stacked[n].shape[-1]) for n in BIG})

    def stages(self, i):
        return None


def _local_step(x, tgt, weights, small, red=None):
    seq = x.shape[0]
    h = jnp.concatenate([jnp.zeros((N_PAD, D_MODEL), F32), small["meta_tokens"], x], axis=0)
    saved, ws, bigs = [], [], []
    for i in range(DEPTH):
        w = {n: small[n][i] for n in SMALL_PER_LAYER}
        w["d_skip_exp"] = jnp.repeat(w["d_skip"], HEAD_DIM)
        big = weights.layer(i)
        h, s = _layer_fwd(h, w, big, i, weights.stages(i))
        saved.append(s)
        ws.append(w)
        bigs.append(big)
    loss, dh, g_final = _loss_head(h, small["final_norm_w"], tgt)
    layer_g, layer_gb = [None] * DEPTH, [None] * DEPTH
    for i in reversed(range(DEPTH)):
        dh, layer_g[i], layer_gb[i] = _layer_bwd(dh, saved[i], ws[i], bigs[i], i, red)
        if red is not None:
            red.push(i, layer_gb[i])
    grads = {n: jnp.stack([layer_g[i][n] for i in range(DEPTH)]) for n in SMALL_PER_LAYER}
    grads["final_norm_w"] = g_final
    grads["meta_tokens"] = dh[N_PAD:ROW_X]
    return loss, dh[ROW_X:ROW_X + seq], layer_gb, grads


MESH = pl.DeviceIdType.MESH
LANES = 128
ANY = pl.BlockSpec(memory_space=pl.ANY)


def _place():
    x, y, c = lax.axis_index("x"), lax.axis_index("y"), lax.axis_index("c")
    chips = [(1 - x, y), (x, 1 - y), (1 - x, 1 - y)]
    return x, y, c, chips


def _remote(src, dst, send_sem, recv_sem, to):
    return pltpu.make_async_remote_copy(src_ref=src, dst_ref=dst, send_sem=send_sem, recv_sem=recv_sem,
                                        device_id=to, device_id_type=MESH)


class _WeightGatherer:
    GROUPS = {"a": ("w_in",), "b": tuple(n for n in BIG if n != "w_in")}

    def __init__(self, mine):
        self.mine = mine
        self.landing = {}
        self.ready = {}

    def _ici(self, layer, group):
        names = self.GROUPS[group]
        srcs = [self.mine[n] for n in names]
        outs = [jax.ShapeDtypeStruct((1, N_CHIPS) + s.shape[1:], BF16) for s in srcs]

        def make(ins, out, send_sems, recv_sems):
            x, y, c, chips = _place()
            me = 2 * x + y
            cps = []
            for p, ref in enumerate(ins):
                half = ref.shape[1] // 2
                rows = pl.ds(c * half, half)
                cps += [_remote(ref.at[layer, rows, :], out[p].at[0, me, rows, :], send_sems.at[3 * p + j],
                                recv_sems.at[3 * p + j], (*chip, c)) for j, chip in enumerate(chips)]
            return cps

        return _Carry(srcs, outs, 3 * len(names), make)

    def _d2d(self, layer, group):
        names = self.GROUPS[group]
        n = len(names)
        bufs = self.landing[(layer, group)]
        outs = [jax.ShapeDtypeStruct(b.shape, b.dtype) for b in bufs]

        def make(ins, out, send_sems, recv_sems):
            x, y, c, chips = _place()
            me = 2 * x + y
            sibling = (x, y, 1 - c)
            cps = []
            for p in range(n):
                half = out[p].shape[2] // 2
                for j, (cx, cy) in enumerate(chips):
                    blk = out[p].at[0, 2 * cx + cy, pl.ds(c * half, half), :]
                    cps.append(_remote(blk, blk, send_sems.at[4 * p + j], recv_sems.at[4 * p + j], sibling))
                cps.append(_remote(ins[n + p].at[layer], out[p].at[0, me], send_sems.at[4 * p + 3],
                                   recv_sems.at[4 * p + 3], sibling))
            return cps

        return _Carry(list(bufs) + [self.mine[m] for m in names], outs, 4 * n, make, alias={p: p for p in range(n)})

    def _landed(self, layer, group, bufs):
        self.landing[(layer, group)] = bufs

    def _done(self, layer, group, bufs):
        self.ready.setdefault(layer, {}).update(zip(self.GROUPS[group], bufs))

    def layer(self, i):
        if i == 0:
            for g in self.GROUPS:
                self._landed(0, g, _run_carry(self._ici(0, g), f"gather_ici_{g}_0"))
            for g in self.GROUPS:
                self._done(0, g, _run_carry(self._d2d(0, g), f"gather_d2d_{g}_0"))
        return _layer_view(self.ready[i])

    def stages(self, i):
        nxt = i + 1
        if nxt == DEPTH:
            return None
        st = {}
        for g in self.GROUPS:
            st[f"{g}_ici"] = (functools.partial(self._ici, nxt, g), functools.partial(self._landed, nxt, g))
            st[f"{g}_d2d"] = (functools.partial(self._d2d, nxt, g), functools.partial(self._done, nxt, g))
        return st


def _exchange_small(v, reduce, name):
    rows_per = v.shape[0]
    vm = pl.BlockSpec(memory_space=pltpu.VMEM)

    def body(v_ref, out_ref, *scratch):
        if reduce:
            land_ref, send_sems, recv_sems, local_sem = scratch
        else:
            land_ref = out_ref
            send_sems, recv_sems, local_sem = scratch
        x, y, c, chips = _place()
        me, sibling = (x, y, c), (x, y, 1 - c)

        def rows(px, py, pc):
            return land_ref.at[4 * px + 2 * py + pc]

        def copy(k, block, to, src=None):
            return _remote(rows(*block) if src is None else src, rows(*block), send_sems.at[k], recv_sems.at[k], to)

        mine = pltpu.make_async_copy(v_ref, rows(*me), local_sem)
        mine.start()
        first = [copy(0, me, sibling, src=v_ref)]
        first += [copy(1 + j, me, (*chip, c), src=v_ref) for j, chip in enumerate(chips)]
        for cp in first:
            cp.start()
        passed = [copy(4 + j, (*chip, c), sibling) for j, chip in enumerate(chips)]
        for j, chip in enumerate(chips):
            copy(1 + j, (*chip, c), me).wait_recv()
            passed[j].start()
        copy(0, sibling, me).wait_recv()
        for j, chip in enumerate(chips):
            copy(4 + j, (*chip, 1 - c), me).wait_recv()
        for cp in first + passed:
            cp.wait_send()
        mine.wait()
        if reduce:
            acc = land_ref[0]
            for d in range(1, 8):
                acc = acc + land_ref[d]
            out_ref[...] = acc

    sems = [pltpu.SemaphoreType.DMA((7,)), pltpu.SemaphoreType.DMA((7,)), pltpu.SemaphoreType.DMA]
    if reduce:
        out_shape = jax.ShapeDtypeStruct((rows_per, LANES), F32)
        scratch = [pltpu.VMEM((8, rows_per, LANES), F32)] + sems
    else:
        out_shape = jax.ShapeDtypeStruct((8, rows_per, LANES), F32)
        scratch = sems
    return pl.pallas_call(
        body, name=name, in_specs=[vm], out_specs=vm, out_shape=out_shape, scratch_shapes=scratch,
        compiler_params=pltpu.CompilerParams(vmem_limit_bytes=VMEM_LIMIT),
    )(v)


def _run_carry(carry, name):
    c_in, c_out, c_sem = carry.specs()
    n_in, n_out = len(c_in), len(c_out)

    def body(*refs):
        cps = carry.make(refs[:n_in], refs[n_in:n_in + n_out], *refs[n_in + n_out:])
        for cp in cps:
            cp.start()
        for cp in cps:
            cp.wait()

    return pl.pallas_call(body, name=name, in_specs=c_in, out_specs=c_out, out_shape=carry.outs,
                          scratch_shapes=c_sem, input_output_aliases=carry.aliases(0, 0))(*carry.ins)


def _row_tile(rows, last, itemsize=4, budget=2 * 1024 * 1024):
    return _tile(rows, tuple(t for t in (2048, 1024, 512, 256, 128, 64, 32, 16) if t * last * itemsize <= budget))


def _rs_add(g, got, core, name):
    _, half, last = got.shape
    tr = _row_tile(half, last)
    nb = half // tr

    def body(c_ref, g_ref, got_ref, o_ref):
        o_ref[...] = (g_ref[...] + got_ref[...]).astype(BF16)

    blk = pl.BlockSpec((None, tr, last), lambda s, i, c_ref: (s, i, 0))
    return pl.pallas_call(
        body, name=name,
        grid_spec=pltpu.PrefetchScalarGridSpec(
            num_scalar_prefetch=1, grid=(N_CHIPS, nb),
            in_specs=[pl.BlockSpec((None, tr, last), lambda s, i, c_ref: (s, c_ref[0] * nb + i, 0)), blk],
            out_specs=blk),
        out_shape=jax.ShapeDtypeStruct(got.shape, BF16),
        compiler_params=_cparams(("parallel", "parallel")),
    )(core.reshape(1).astype(jnp.int32), g, got)


def _rs_sum(own, got, acc, layer, chip, core, name):
    _, half, last = own.shape
    tr = _row_tile(half, last, budget=1024 * 1024)
    nb = half // tr
    ni = 0 if acc is None else 1

    def body(k_ref, c_ref, own_ref, got_ref, *rest):
        o_ref = rest[-1]
        tot = own_ref[...].astype(F32)
        for j in range(3):
            tot = tot + got_ref[j].astype(F32)
        o_ref[...] = tot

    return pl.pallas_call(
        body, name=name,
        grid_spec=pltpu.PrefetchScalarGridSpec(
            num_scalar_prefetch=2, grid=(nb,),
            in_specs=[pl.BlockSpec((None, tr, last), lambda i, k_ref, c_ref: (k_ref[0], i, 0)),
                      pl.BlockSpec((3, tr, last), lambda i, k_ref, c_ref: (0, i, 0))]
            + [pl.BlockSpec(memory_space=pl.ANY)] * ni,
            out_specs=pl.BlockSpec((None, tr, last), lambda i, k_ref, c_ref: (layer, c_ref[0] * nb + i, 0))),
        out_shape=jax.ShapeDtypeStruct((DEPTH, 2 * half, last), F32),
        input_output_aliases={4: 0} if ni else {},
        compiler_params=_cparams(("parallel",)),
    )(chip.reshape(1).astype(jnp.int32), core.reshape(1).astype(jnp.int32), own, got, *([acc] if ni else []))


def _rs_share(arrs):
    n = len(arrs)

    def body(*refs):
        out = refs[n:2 * n]
        send_sems, recv_sems = refs[2 * n:]
        x, y, c, _ = _place()
        sibling = (x, y, 1 - c)
        cps = []
        for p in range(n):
            half = out[p].shape[1] // 2
            mine = out[p].at[:, pl.ds(c * half, half), :]
            cps.append(_remote(mine, mine, send_sems.at[p], recv_sems.at[p], sibling))
        for cp in cps:
            cp.start()
        for p, cp in enumerate(cps):
            cp.wait_send()
            half = out[p].shape[1] // 2
            other = out[p].at[:, pl.ds((1 - c) * half, half), :]
            _remote(other, other, send_sems.at[p], recv_sems.at[p], sibling).wait_recv()

    return pl.pallas_call(
        body, name="rs_share", in_specs=[ANY] * n, out_specs=[ANY] * n,
        out_shape=[jax.ShapeDtypeStruct(a.shape, a.dtype) for a in arrs],
        input_output_aliases={p: p for p in range(n)},
        scratch_shapes=[pltpu.SemaphoreType.DMA((n,)), pltpu.SemaphoreType.DMA((n,))],
    )(*arrs)


class _GradReducer:
    def __init__(self, chip, core):
        self.chip, self.core = chip, core
        self.pending = None
        self.sums = None
        self.acc = {n: None for n in BIG}

    def push(self, layer, gb):
        self.pending = (layer, [gb[n] for n in BIG])

    def swap_carry(self):
        if self.pending is None:
            return None
        _, gs = self.pending
        outs = [jax.ShapeDtypeStruct((g.shape[0], g.shape[1] // 2, g.shape[2]), F32) for g in gs]

        def make(ins, out, send_sems, recv_sems):
            x, y, c, _ = _place()
            cps = []
            for p, ref in enumerate(ins):
                half = ref.shape[1] // 2
                cps.append(_remote(ref.at[:, pl.ds((1 - c) * half, half), :], out[p], send_sems.at[p], recv_sems.at[p],
                                   (x, y, 1 - c)))
            return cps

        return _Carry(gs, outs, len(gs), make)

    def after_swap(self, got):
        layer, gs = self.pending
        self.pending = None
        self.sums = (layer, [_rs_add(g, a, self.core, f"rs_add_{n}_{layer}") for n, g, a in zip(BIG, gs, got)])

    def ici_carry(self):
        if self.sums is None:
            return None
        _, ps = self.sums
        outs = [jax.ShapeDtypeStruct((3,) + p.shape[1:], BF16) for p in ps]

        def make(ins, out, send_sems, recv_sems):
            x, y, c, chips = _place()
            return [_remote(ins[p].at[2 * cx + cy], out[p].at[j], send_sems.at[3 * p + j], recv_sems.at[3 * p + j],
                            (cx, cy, c)) for p in range(len(ins)) for j, (cx, cy) in enumerate(chips)]

        return _Carry(ps, outs, 3 * len(ps), make)

    def after_ici(self, got):
        layer, ps = self.sums
        self.sums = None
        for n, p, a in zip(BIG, ps, got):
            self.acc[n] = _rs_sum(p, a, self.acc[n], layer, self.chip, self.core, f"rs_sum_{n}_{layer}")

    def finish(self):
        self.after_swap(_run_carry(self.swap_carry(), "rs_swap_last"))
        self.after_ici(_run_carry(self.ici_carry(), "rs_ici_last"))
        return dict(zip(BIG, _rs_share([self.acc[n] for n in BIG])))


def _adamw(w, g, m, v, name):
    shape = w.shape
    if len(shape) == 2:
        shape3 = (1,) + shape
    else:
        shape3 = (-1,) + shape[-2:]
    w2, g2, m2, v2 = (a.reshape(shape3) for a in (w, g, m, v))
    lead, rows, last = w2.shape
    tr = _tile(rows, tuple(t for t in (1024, 512, 256, 128, 64, 32, 16, 8) if t * last * 4 <= 2 * 1024 * 1024) + (rows,))

    def body(w_ref, g_ref, m_ref, v_ref, d_ref, nm_ref, nv_ref):
        gv = g_ref[...]
        mn = ADAM_B1 * m_ref[...] + (1.0 - ADAM_B1) * gv
        vn = ADAM_B2 * v_ref[...] + (1.0 - ADAM_B2) * (gv * gv)
        m_hat = mn / (1.0 - ADAM_B1 ** ADAM_STEP)
        v_hat = vn / (1.0 - ADAM_B2 ** ADAM_STEP)
        d_ref[...] = -ADAM_LR * (m_hat / (jnp.sqrt(v_hat) + ADAM_EPS) + ADAM_WD * w_ref[...])
        nm_ref[...] = mn
        nv_ref[...] = vn

    blk = pl.BlockSpec((None, tr, last), lambda a, i: (a, i, 0))
    outs = pl.pallas_call(
        body, name=name, grid=(lead, rows // tr), in_specs=[blk] * 4, out_specs=[blk] * 3,
        out_shape=[jax.ShapeDtypeStruct((lead, rows, last), F32)] * 3,
        compiler_params=_cparams(("parallel", "parallel")),
    )(w2, g2, m2, v2)
    return tuple(o.reshape(shape) for o in outs)


SHARD_AXIS = {"conv_w": 2, "meta_tokens": 1}
SMALL_SHARDED = ("conv_w", "meta_tokens")
REPLICATED = ("mix_norm_w", "b_gate", "pool_scale", "conv_b", "dt_bias", "a_log", "d_skip", "ssd_norm_w", "mlp_norm_w",
              "final_norm_w")


def _flatten(arrs, dtype, row_mult):
    flat = jnp.concatenate([a.astype(dtype).reshape(-1) for a in arrs])
    n = flat.shape[0]
    rows = -(-n // (LANES * row_mult)) * row_mult
    return jnp.pad(flat, (0, rows * LANES - n)).reshape(rows, LANES)


def _unflatten(flat2d, shapes):
    flat = flat2d.reshape(-1)
    out, off = [], 0
    for sh in shapes:
        n = 1
        for d in sh:
            n *= d
        out.append(flat[off:off + n].reshape(sh))
        off += n
    return out


def kernel(x, meta_tokens, mix_norm_w, w_in, b_gate, pool_w_group, pool_scale, w_pool_up, conv_w, conv_b, dt_bias, a_log, d_skip, ssd_norm_w, w_ssd_out, w_o, mlp_norm_w, w_ff1, w_ff2, final_norm_w, loss_target, m_meta_tokens, m_mix_norm_w, m_w_in, m_b_gate, m_pool_w_group, m_pool_scale, m_w_pool_up, m_conv_w, m_conv_b, m_dt_bias, m_a_log, m_d_skip, m_ssd_norm_w, m_w_ssd_out, m_w_o, m_mlp_norm_w, m_w_ff1, m_w_ff2, m_final_norm_w, v_meta_tokens, v_mix_norm_w, v_w_in, v_b_gate, v_pool_w_group, v_pool_scale, v_w_pool_up, v_conv_w, v_conv_b, v_dt_bias, v_a_log, v_d_skip, v_ssd_norm_w, v_w_ssd_out, v_w_o, v_mlp_norm_w, v_w_ff1, v_w_ff2, v_final_norm_w):
    names = ("meta_tokens",) + PER_LAYER + ("final_norm_w",)
    par = dict(meta_tokens=meta_tokens, mix_norm_w=mix_norm_w, w_in=w_in, b_gate=b_gate, pool_w_group=pool_w_group,
               pool_scale=pool_scale, w_pool_up=w_pool_up, conv_w=conv_w, conv_b=conv_b, dt_bias=dt_bias, a_log=a_log,
               d_skip=d_skip, ssd_norm_w=ssd_norm_w, w_ssd_out=w_ssd_out, w_o=w_o, mlp_norm_w=mlp_norm_w, w_ff1=w_ff1,
               w_ff2=w_ff2, final_norm_w=final_norm_w)
    mom = dict(meta_tokens=m_meta_tokens, mix_norm_w=m_mix_norm_w, w_in=m_w_in, b_gate=m_b_gate, pool_w_group=m_pool_w_group,
               pool_scale=m_pool_scale, w_pool_up=m_w_pool_up, conv_w=m_conv_w, conv_b=m_conv_b, dt_bias=m_dt_bias,
               a_log=m_a_log, d_skip=m_d_skip, ssd_norm_w=m_ssd_norm_w, w_ssd_out=m_w_ssd_out, w_o=m_w_o,
               mlp_norm_w=m_mlp_norm_w, w_ff1=m_w_ff1, w_ff2=m_w_ff2, final_norm_w=m_final_norm_w)
    var = dict(meta_tokens=v_meta_tokens, mix_norm_w=v_mix_norm_w, w_in=v_w_in, b_gate=v_b_gate, pool_w_group=v_pool_w_group,
               pool_scale=v_pool_scale, w_pool_up=v_w_pool_up, conv_w=v_conv_w, conv_b=v_conv_b, dt_bias=v_dt_bias,
               a_log=v_a_log, d_skip=v_d_skip, ssd_norm_w=v_ssd_norm_w, w_ssd_out=v_w_ssd_out, w_o=v_w_o,
               mlp_norm_w=v_mlp_norm_w, w_ff1=v_w_ff1, w_ff2=v_w_ff2, final_norm_w=v_final_norm_w)
    chip = 2 * lax.axis_index("x") + lax.axis_index("y")
    core = lax.axis_index("c")

    weights = _WeightGatherer({n: par[n].astype(BF16).reshape(DEPTH, -1, par[n].shape[-1]) for n in BIG})
    small = {n: par[n] for n in REPLICATED}
    small_shapes = [par[n].shape for n in SMALL_SHARDED]
    got_small = _exchange_small(_flatten([par[n] for n in SMALL_SHARDED], F32, 8), False, "gather_small")
    pieces = [_unflatten(got_small[2 * k], small_shapes) for k in range(N_CHIPS)]
    for j, n in enumerate(SMALL_SHARDED):
        small[n] = jnp.concatenate([pieces[k][j] for k in range(N_CHIPS)], axis=SHARD_AXIS[n])

    reducer = _GradReducer(chip, core)
    loss, dx, _, grads = _local_step(x[0], loss_target[0], weights, small, reducer)
    loss = lax.psum(loss, ("x", "y", "c"))

    small_names = REPLICATED + SMALL_SHARDED
    red = _exchange_small(_flatten([grads[n] for n in small_names], F32, 8), True, "allreduce_small")
    gsum = dict(zip(small_names, _unflatten(red, [grads[n].shape for n in small_names])))
    for n in SMALL_SHARDED:
        width = par[n].shape[SHARD_AXIS[n]]
        gsum[n] = lax.dynamic_slice_in_dim(gsum[n], chip * width, width, axis=SHARD_AXIS[n])

    gsum.update({n: g.reshape(par[n].shape) for n, g in reducer.finish().items()})

    delta, new_m, new_v = {}, {}, {}
    for n in BIG:
        delta[n], new_m[n], new_v[n] = _adamw(par[n], gsum[n], mom[n], var[n], "adamw_" + n)
    rest = [n for n in names if n not in BIG]
    shapes = [par[n].shape for n in rest]
    packed = [_flatten([d[n] for n in rest], F32, 8) for d in (par, gsum, mom, var)]
    for d, flat in zip((delta, new_m, new_v), _adamw(*packed, "adamw_small")):
        d.update(zip(rest, _unflatten(flat, shapes)))

    order = ("meta_tokens", "mix_norm_w", "w_in", "b_gate", "pool_w_group", "pool_scale", "w_pool_up", "conv_w", "conv_b",
             "dt_bias", "a_log", "d_skip", "ssd_norm_w", "w_ssd_out", "w_o", "mlp_norm_w", "w_ff1", "w_ff2", "final_norm_w")
    return (loss, dx[None], *[gsum[n] for n in order], *[delta[n] for n in order], *[new_m[n] for n in order],
            *[new_v[n] for n in order])
```

```python
import functools

import jax
import jax.numpy as jnp
from jax import lax
from jax.experimental import pallas as pl
from jax.experimental.pallas import tpu as pltpu

F32 = jnp.float32
BF16 = jnp.bfloat16

D_MODEL = 1024
DEPTH = 4
N_META = 16
N_PAD = 112
ROW_X = N_PAD + N_META
POOL_WINDOWS = (2, 4, 8, 16)
POOL_GDIM = 256
D_INNER = 2048
N_HEADS = 32
HEAD_DIM = 64
N_GROUPS = 8
HEADS_PER_GROUP = 4
GROUP_W = HEADS_PER_GROUP * HEAD_DIM
D_STATE = 128
CHUNK = 128
D_XBC = 4096
D_FF = 4096
EPS = 1e-5
OFF_Z, OFF_XBC, OFF_DT, OFF_GATE, IN_COLS = 1024, 3072, 7168, 7200, 9248
PZ, PGATE, PPOOL, PXBC, PCOLS = 0, 2048, 4096, 5120, 9216
DT_PAD = 128

ADAM_LR, ADAM_B1, ADAM_B2, ADAM_EPS, ADAM_WD, ADAM_STEP = 0.001, 0.9, 0.999, 1e-08, 0.01, 10

VMEM_LIMIT = 56 * 1024 * 1024

_NN = (((1,), (0,)), ((), ()))
_NT = (((1,), (1,)), ((), ()))
_TN = (((0,), (0,)), ((), ()))


def _dot(a, b, dn=_NN):
    return lax.dot_general(a, b, dn, preferred_element_type=F32)


def _cparams(sem):
    return pltpu.CompilerParams(dimension_semantics=sem, vmem_limit_bytes=VMEM_LIMIT)


def _tile(n, cands):
    for c in cands:
        if n % c == 0:
            return c
    raise ValueError(f"no tile for {n} in {cands}")


def _sigmoid(x):
    return 0.5 * jnp.tanh(0.5 * x) + 0.5


def _softplus(x):
    return jnp.maximum(x, 0.0) + jnp.log(1.0 + jnp.exp(-jnp.abs(x)))


def _bs(shape, fn):
    return pl.BlockSpec(shape, fn)


class _Carry:
    def __init__(self, ins, outs, n_sems, make, alias=None):
        self.ins, self.outs, self.n_sems, self.make = list(ins), list(outs), n_sems, make
        self.alias = dict(alias or {})

    def aliases(self, first_in, first_out):
        return {first_in + i: first_out + o for i, o in self.alias.items()}

    def specs(self):
        hbm = pl.BlockSpec(memory_space=pl.ANY)
        return [hbm] * len(self.ins), [hbm] * len(self.outs), [pltpu.SemaphoreType.DMA((self.n_sems,))] * 2

    def at_edges(self, grid, in_refs, out_refs, sems):
        ids = [pl.program_id(a) for a in range(len(grid))]
        first = functools.reduce(jnp.logical_and, [i == 0 for i in ids])
        last = functools.reduce(jnp.logical_and, [i == n - 1 for i, n in zip(ids, grid)])

        def begin():
            @pl.when(first)
            def _():
                for cp in self.make(in_refs, out_refs, *sems):
                    cp.start()

        def end():
            @pl.when(last)
            def _():
                for cp in self.make(in_refs, out_refs, *sems):
                    cp.wait()

        return begin, end


def _mm(name, dn, grid, a, a_spec, b, b_spec, outs, *, extras=(), epilogue=None, carry=None):
    nk = grid[2]
    ne, no = len(extras), len(outs)
    blk = tuple(d for d in outs[0][2].block_shape if d is not None)
    c_in, c_out, c_sem = carry.specs() if carry is not None else ([], [], [])
    nci, nco = len(c_in), len(c_out)

    def body(a_ref, b_ref, *rest):
        e_refs, o_refs = rest[:ne], rest[ne + nci:ne + nci + no]
        if carry is not None:
            begin, end = carry.at_edges(grid, rest[ne:ne + nci], rest[ne + nci + no:ne + nci + no + nco], rest[-2:])
            begin()
        p = _dot(a_ref[...].astype(BF16), b_ref[...].astype(BF16), dn)

        def finish(acc):
            outs = epilogue(acc, *[e[...] for e in e_refs]) if epilogue is not None else (acc,)
            for o, v in zip(o_refs, outs):
                o[...] = v.astype(o.dtype)

        if nk == 1:
            finish(p)
        else:
            acc_ref = rest[ne + nci + no + nco]
            kk = pl.program_id(2)

            @pl.when(kk == 0)
            def _():
                acc_ref[...] = p

            @pl.when(kk > 0)
            def _():
                acc_ref[...] += p

            @pl.when(kk == nk - 1)
            def _():
                finish(acc_ref[...])
        if carry is not None:
            end()

    res = pl.pallas_call(
        body,
        name=name,
        grid=grid,
        in_specs=[a_spec, b_spec, *[s for _, s in extras]] + c_in,
        out_specs=[o[2] for o in outs] + c_out,
        out_shape=[jax.ShapeDtypeStruct(o[0], o[1]) for o in outs] + (carry.outs if carry is not None else []),
        scratch_shapes=([pltpu.VMEM(blk, F32)] if nk > 1 else []) + c_sem,
        input_output_aliases=carry.aliases(2 + ne, no) if carry is not None else {},
        compiler_params=_cparams(("arbitrary",) * 3 if carry is not None else ("parallel", "parallel", "arbitrary")),
    )(a, b, *[e for e, _ in extras], *(carry.ins if carry is not None else []))
    if carry is not None:
        return res[:no], res[no:]
    return res[0] if no == 1 else res


def _norm_fwd(h, w, name):
    t, d = h.shape
    tm = _tile(t, (1056, 384, 128))

    def body(h_ref, w_ref, u_ref):
        x = h_ref[...]
        r = lax.rsqrt(jnp.mean(x * x, axis=-1, keepdims=True) + EPS)
        u_ref[...] = (x * r * w_ref[...]).astype(BF16)

    return pl.pallas_call(
        body, name=name, grid=(t // tm,),
        in_specs=[pl.BlockSpec((tm, d), lambda i: (i, 0)), pl.BlockSpec((1, d), lambda i: (0, 0))],
        out_specs=pl.BlockSpec((tm, d), lambda i: (i, 0)),
        out_shape=jax.ShapeDtypeStruct((t, d), BF16),
        compiler_params=_cparams(("parallel",)),
    )(h, w.reshape(1, d))


def _norm_bwd(dres, du, h, w, name):
    t, d = h.shape
    tm = _tile(t, (528, 384, 128))

    def body(dres_ref, du_ref, h_ref, w_ref, dh_ref, dw_ref):
        x = h_ref[...]
        r = lax.rsqrt(jnp.mean(x * x, axis=-1, keepdims=True) + EPS)
        xhat = x * r
        du_v = du_ref[...]
        g = du_v * w_ref[...]
        dh_ref[...] = dres_ref[...] + r * (g - xhat * jnp.mean(g * xhat, axis=-1, keepdims=True))

        @pl.when(pl.program_id(0) == 0)
        def _():
            dw_ref[...] = jnp.zeros_like(dw_ref)

        dw_ref[...] += jnp.sum(du_v * xhat, axis=0, keepdims=True)

    row = pl.BlockSpec((tm, d), lambda i: (i, 0))
    vec = pl.BlockSpec((1, d), lambda i: (0, 0))
    dh, dw = pl.pallas_call(
        body, name=name, grid=(t // tm,),
        in_specs=[row, row, row, vec], out_specs=[row, vec],
        out_shape=[jax.ShapeDtypeStruct((t, d), F32), jax.ShapeDtypeStruct((1, d), F32)],
        compiler_params=_cparams(("arbitrary",)),
    )(dres, du, h, w.reshape(1, d))
    return dh, dw.reshape(d)


def _loss_head(h, w, tgt):
    t, d = h.shape
    tm = CHUNK
    nb = ROW_X // tm

    def body(h_ref, w_ref, t_ref, loss_ref, dh_ref, dw_ref):
        i = pl.program_id(0)
        x = h_ref[...]
        r = lax.rsqrt(jnp.mean(x * x, axis=-1, keepdims=True) + EPS)
        xhat = x * r
        wv = w_ref[...]
        live = i >= nb
        err = jnp.where(live, xhat * wv - t_ref[...], 0.0)
        dout = err * (1.0 / d)
        g = dout * wv
        dh_ref[...] = r * (g - xhat * jnp.mean(g * xhat, axis=-1, keepdims=True))

        @pl.when(i == 0)
        def _():
            loss_ref[...] = jnp.zeros_like(loss_ref)
            dw_ref[...] = jnp.zeros_like(dw_ref)

        loss_ref[...] += 0.5 * jnp.sum(jnp.sum(err * err, axis=-1, keepdims=True), axis=0, keepdims=True) * (1.0 / d)
        dw_ref[...] += jnp.sum(dout * xhat, axis=0, keepdims=True)

    row = pl.BlockSpec((tm, d), lambda i: (i, 0))
    vec = pl.BlockSpec((1, d), lambda i: (0, 0))
    loss, dh, dw = pl.pallas_call(
        body, name="loss_head", grid=(t // tm,),
        in_specs=[row, vec, pl.BlockSpec((tm, d), lambda i: (jnp.maximum(i - nb, 0), 0))],
        out_specs=[pl.BlockSpec((1, 1), lambda i: (0, 0)), row, vec],
        out_shape=[jax.ShapeDtypeStruct((1, 1), F32), jax.ShapeDtypeStruct((t, d), F32), jax.ShapeDtypeStruct((1, d), F32)],
        compiler_params=_cparams(("arbitrary",)),
    )(h, w.reshape(1, d), tgt)
    return loss[0, 0], dh, dw.reshape(d)


def _gate_fwd(proj, b_gate, y_pool, y_ssd):
    t = proj.shape[0]
    d = D_MODEL
    tm = _tile(t, (528, 384, 128))

    def body(gp_ref, gs_ref, bp_ref, bs_ref, yp_ref, ys_ref, o_ref):
        gp = _sigmoid(gp_ref[...] + bp_ref[...])
        gs = _sigmoid(gs_ref[...] + bs_ref[...])
        o_ref[...] = (gp * yp_ref[...] + gs * ys_ref[...]).astype(BF16)

    row = pl.BlockSpec((tm, d), lambda i: (i, 0))
    return pl.pallas_call(
        body, name="gate_fwd", grid=(t // tm,),
        in_specs=[pl.BlockSpec((tm, d), lambda i: (i, PGATE // d)), pl.BlockSpec((tm, d), lambda i: (i, PGATE // d + 1)),
                  pl.BlockSpec((1, d), lambda i: (0, 0)), pl.BlockSpec((1, d), lambda i: (0, 1)), row, row],
        out_specs=row, out_shape=jax.ShapeDtypeStruct((t, d), BF16),
        compiler_params=_cparams(("parallel",)),
    )(proj, proj, b_gate.reshape(1, 2 * d), b_gate.reshape(1, 2 * d), y_pool, y_ssd)


def _gate_bwd(dmerged, proj, b_gate, y_pool, y_ssd):
    t = proj.shape[0]
    d = D_MODEL
    tm = _tile(t, (384, 128))

    def body(dm_ref, gp_ref, gs_ref, bp_ref, bs_ref, yp_ref, ys_ref, dyp_ref, dys_ref, dg_ref, db_ref):
        dm = dm_ref[...]
        gp = _sigmoid(gp_ref[...] + bp_ref[...])
        gs = _sigmoid(gs_ref[...] + bs_ref[...])
        dyp_ref[...] = (dm * gp).astype(BF16)
        dys_ref[...] = (dm * gs).astype(BF16)
        dgp = dm * yp_ref[...] * gp * (1.0 - gp)
        dgs = dm * ys_ref[...] * gs * (1.0 - gs)
        dg_ref[:, :d] = dgp.astype(BF16)
        dg_ref[:, d:] = dgs.astype(BF16)

        @pl.when(pl.program_id(0) == 0)
        def _():
            db_ref[...] = jnp.zeros_like(db_ref)

        db_ref[:, :d] += jnp.sum(dgp, axis=0, keepdims=True)
        db_ref[:, d:] += jnp.sum(dgs, axis=0, keepdims=True)

    row = pl.BlockSpec((tm, d), lambda i: (i, 0))
    dyp, dys, dg, db = pl.pallas_call(
        body, name="gate_bwd", grid=(t // tm,),
        in_specs=[row, pl.BlockSpec((tm, d), lambda i: (i, PGATE // d)), pl.BlockSpec((tm, d), lambda i: (i, PGATE // d + 1)),
                  pl.BlockSpec((1, d), lambda i: (0, 0)), pl.BlockSpec((1, d), lambda i: (0, 1)), row, row],
        out_specs=[row, row, pl.BlockSpec((tm, 2 * d), lambda i: (i, PGATE // (2 * d))),
                   pl.BlockSpec((1, 2 * d), lambda i: (0, 0))],
        out_shape=[jax.ShapeDtypeStruct((t, d), BF16), jax.ShapeDtypeStruct((t, d), BF16),
                   jax.ShapeDtypeStruct((t, PCOLS), BF16), jax.ShapeDtypeStruct((1, 2 * d), F32)],
        compiler_params=_cparams(("arbitrary",)),
    )(dmerged, proj, proj, b_gate.reshape(1, 2 * d), b_gate.reshape(1, 2 * d), y_pool, y_ssd)
    return dyp, dys, dg, db.reshape(2 * d)


POOL_HALO = 16


def _pool_counts(row0, n, win):
    pos1 = row0 + lax.broadcasted_iota(jnp.int32, (n, 1), 0) - (N_PAD - 1)
    return jnp.clip(pos1, 1, win).astype(F32)


N_CHIPS = 4
WG_ROWS = POOL_GDIM // N_CHIPS
WUP_ROWS = D_MODEL // N_CHIPS


def _group_w(wg_ref, g):
    return jnp.concatenate([wg_ref[k, g] for k in range(N_CHIPS)], axis=0)


def _pool_w_specs(layer):
    return [pl.BlockSpec((None, N_CHIPS, len(POOL_WINDOWS), WG_ROWS, POOL_GDIM), lambda i: (0, 0, 0, 0, 0)),
            pl.BlockSpec((None, N_CHIPS, WUP_ROWS, D_MODEL), lambda i: (0, 0, 0, 0))]


def _pool_fwd(proj, wg, scale, wup, layer):
    t = proj.shape[0]
    d = D_MODEL
    tm = _tile(t, (384, 128))
    hb = tm // POOL_HALO

    def body(u_ref, halo_ref, sc_ref, wg_ref, wup_ref, pooled_ref, yg_ref, ypm_ref, yp_ref):
        i = pl.program_id(0)
        x = u_ref[...]
        halo = jnp.where(i > 0, halo_ref[...], 0.0)
        xc = jnp.concatenate([halo, x], axis=0)
        for g, win in enumerate(POOL_WINDOWS):
            sl = slice(g * POOL_GDIM, (g + 1) * POOL_GDIM)
            s = xc[:, sl]
            k = 1
            while k < win:
                s = s + pltpu.roll(s, k, axis=0)
                k *= 2
            pooled = s[POOL_HALO:] / _pool_counts(i * tm, tm, win) - x[:, sl]
            pb = pooled.astype(BF16)
            pooled_ref[:, sl] = pb
            yg_ref[:, sl] = _dot(pb, _group_w(wg_ref, g))
        ypm = (yg_ref[...] * sc_ref[...]).astype(BF16)
        ypm_ref[...] = ypm
        acc = _dot(ypm[:, :WUP_ROWS], wup_ref[0])
        for k in range(1, N_CHIPS):
            acc = acc + _dot(ypm[:, k * WUP_ROWS:(k + 1) * WUP_ROWS], wup_ref[k])
        yp_ref[...] = acc

    row = pl.BlockSpec((tm, d), lambda i: (i, 0))
    return pl.pallas_call(
        body, name=f"pool_fwd_{layer}", grid=(t // tm,),
        in_specs=[pl.BlockSpec((tm, d), lambda i: (i, PPOOL // d)),
                  pl.BlockSpec((POOL_HALO, d), lambda i: (jnp.maximum(i * hb - 1, 0), PPOOL // d)),
                  pl.BlockSpec((1, d), lambda i: (0, 0))] + _pool_w_specs(layer),
        out_specs=[row, row, row, row],
        out_shape=[jax.ShapeDtypeStruct((t, d), BF16), jax.ShapeDtypeStruct((t, d), F32),
                   jax.ShapeDtypeStruct((t, d), BF16), jax.ShapeDtypeStruct((t, d), F32)],
        compiler_params=_cparams(("parallel",)),
    )(proj, proj, scale.reshape(1, d), wg, wup)


def _pool_bwd_a(dy_pool, yg, scale, wg, wup, layer):
    t, d = yg.shape
    tm = _tile(t, (384, 128))

    def body(dy_ref, yg_ref, sc_ref, wg_ref, wup_ref, q_ref, dyg_ref, dsc_ref):
        dy = dy_ref[...]
        dypm = jnp.concatenate([_dot(dy, wup_ref[k], _NT) for k in range(N_CHIPS)], axis=1)

        @pl.when(pl.program_id(0) == 0)
        def _():
            dsc_ref[...] = jnp.zeros_like(dsc_ref)

        dsc_ref[...] += jnp.sum(dypm * yg_ref[...], axis=0, keepdims=True)
        dyg = (dypm * sc_ref[...]).astype(BF16)
        dyg_ref[...] = dyg
        for g in range(len(POOL_WINDOWS)):
            sl = slice(g * POOL_GDIM, (g + 1) * POOL_GDIM)
            q_ref[:, sl] = _dot(dyg[:, sl], _group_w(wg_ref, g), _NT)

    row = pl.BlockSpec((tm, d), lambda i: (i, 0))
    vec = pl.BlockSpec((1, d), lambda i: (0, 0))
    q, dyg, dsc = pl.pallas_call(
        body, name=f"pool_bwd_a_{layer}", grid=(t // tm,),
        in_specs=[row, row, vec] + _pool_w_specs(layer),
        out_specs=[row, row, vec],
        out_shape=[jax.ShapeDtypeStruct((t, d), F32), jax.ShapeDtypeStruct((t, d), BF16), jax.ShapeDtypeStruct((1, d), F32)],
        compiler_params=_cparams(("arbitrary",)),
    )(dy_pool, yg, scale.reshape(1, d), wg, wup)
    return q, dyg, dsc.reshape(d)


def _pool_bwd_b(q, dproj):
    t, d = q.shape
    tm = _tile(t, (384, 128))
    hb = tm // POOL_HALO
    nt = t // tm
    n = tm + POOL_HALO

    def body(q_ref, halo_ref, _, o_ref):
        i = pl.program_id(0)
        qv = q_ref[...]
        halo = jnp.where(i < nt - 1, halo_ref[...], 0.0)
        qc = jnp.concatenate([qv, halo], axis=0)
        for g, win in enumerate(POOL_WINDOWS):
            sl = slice(g * POOL_GDIM, (g + 1) * POOL_GDIM)
            s = qc[:, sl] / _pool_counts(i * tm, n, win)
            k = 1
            while k < win:
                s = s + pltpu.roll(s, n - k, axis=0)
                k *= 2
            o_ref[:, sl] = (s[:tm] - qv[:, sl]).astype(BF16)

    row = pl.BlockSpec((tm, d), lambda i: (i, 0))
    return pl.pallas_call(
        body, name="pool_bwd_b", grid=(nt,),
        in_specs=[row, pl.BlockSpec((POOL_HALO, d), lambda i: (jnp.minimum((i + 1) * hb, t // POOL_HALO - 1), 0)),
                  pl.BlockSpec(memory_space=pl.ANY)],
        out_specs=pl.BlockSpec((tm, d), lambda i: (i, PPOOL // d)), out_shape=jax.ShapeDtypeStruct(dproj.shape, BF16),
        input_output_aliases={2: 0},
        compiler_params=_cparams(("parallel",)),
    )(q, q, dproj)


def _pool_dwg(pooled, dyg, layer):
    t, d = pooled.shape
    tk = _tile(t, (1056, 384, 128))
    nk = t // tk
    gd = POOL_GDIM
    ng = d // gd

    def body(p_ref, g_ref, o_ref):
        @pl.when(pl.program_id(1) == 0)
        def _():
            o_ref[...] = jnp.zeros_like(o_ref)

        part = _dot(p_ref[...], g_ref[...], _TN)
        for k in range(N_CHIPS):
            o_ref[k] += part[k * WG_ROWS:(k + 1) * WG_ROWS]

    blk = pl.BlockSpec((tk, gd), lambda g, k: (k, g))
    return pl.pallas_call(
        body, name=f"pool_dwg_{layer}", grid=(ng, nk), in_specs=[blk, blk],
        out_specs=pl.BlockSpec((N_CHIPS, None, WG_ROWS, gd), lambda g, k: (0, g, 0, 0)),
        out_shape=jax.ShapeDtypeStruct((N_CHIPS, ng, WG_ROWS, gd), F32),
        compiler_params=_cparams(("parallel", "arbitrary")),
    )(pooled, dyg)


CONV_W = 4
CONV_HALO = 8
XBC_BLK = PXBC // 1024


def _conv_fwd(proj, conv_w, conv_b):
    t = proj.shape[0]
    cw = 1024
    tm = _tile(t, (1056, 384, 128))
    hb = tm // CONV_HALO

    def body(x_ref, halo_ref, w_ref, b_ref, o_ref):
        i = pl.program_id(1)
        x = x_ref[...]
        halo = jnp.where(i > 0, halo_ref[...], 0.0)
        xc = jnp.concatenate([halo, x], axis=0)
        w = w_ref[...]
        acc = b_ref[...] + x * w[CONV_W - 1:CONV_W, :]
        for k in range(CONV_W - 1):
            acc = acc + pltpu.roll(xc, CONV_W - 1 - k, axis=0)[CONV_HALO:] * w[k:k + 1, :]
        row = i * tm + lax.broadcasted_iota(jnp.int32, (tm, 1), 0)
        o_ref[...] = jnp.where(row >= N_PAD, acc * _sigmoid(acc), 0.0)

    return pl.pallas_call(
        body, name="conv_fwd", grid=(D_XBC // cw, t // tm),
        in_specs=[pl.BlockSpec((tm, cw), lambda j, i: (i, XBC_BLK + j)),
                  pl.BlockSpec((CONV_HALO, cw), lambda j, i: (jnp.maximum(i * hb - 1, 0), XBC_BLK + j)),
                  pl.BlockSpec((CONV_W, cw), lambda j, i: (0, j)), pl.BlockSpec((1, cw), lambda j, i: (0, j))],
        out_specs=pl.BlockSpec((tm, cw), lambda j, i: (i, j)),
        out_shape=jax.ShapeDtypeStruct((t, D_XBC), F32),
        compiler_params=_cparams(("parallel", "parallel")),
    )(proj, proj, conv_w, conv_b.reshape(1, D_XBC))


def _conv_bwd(dxa, coff, proj, conv_w, conv_b, dproj):
    t, ncols = dxa.shape
    cw = 1024
    tm = _tile(t, (528, 384, 128))
    hb = tm // CONV_HALO
    nt = t // tm
    n = tm + 2 * CONV_HALO

    def body(d_ref, dn_ref, xp_ref, x_ref, xn_ref, w_ref, b_ref, _, o_ref, dw_ref, db_ref):
        i = pl.program_id(1)
        last = i == nt - 1
        xf = jnp.concatenate([jnp.where(i > 0, xp_ref[...], 0.0), x_ref[...], jnp.where(last, 0.0, xn_ref[...])], axis=0)
        df = jnp.concatenate([jnp.zeros((CONV_HALO, cw), F32), d_ref[...], jnp.where(last, 0.0, dn_ref[...])], axis=0)
        w = w_ref[...]
        sh = [pltpu.roll(xf, CONV_W - 1 - k, axis=0) if k < CONV_W - 1 else xf for k in range(CONV_W)]
        xc = b_ref[...]
        for k in range(CONV_W):
            xc = xc + sh[k] * w[k:k + 1, :]
        sig = _sigmoid(xc)
        row = i * tm - CONV_HALO + lax.broadcasted_iota(jnp.int32, (n, 1), 0)
        dxc = jnp.where((row >= N_PAD) & (row < t), df * (sig * (1.0 + xc * (1.0 - sig))), 0.0)
        acc = dxc * w[CONV_W - 1:CONV_W, :]
        for k in range(CONV_W - 1):
            acc = acc + pltpu.roll(dxc, n - (CONV_W - 1 - k), axis=0) * w[k:k + 1, :]
        o_ref[...] = acc[CONV_HALO:CONV_HALO + tm].astype(BF16)

        @pl.when(i == 0)
        def _():
            dw_ref[...] = jnp.zeros_like(dw_ref)
            db_ref[...] = jnp.zeros_like(db_ref)

        dm = dxc[CONV_HALO:CONV_HALO + tm]
        db_ref[...] += jnp.sum(dm, axis=0, keepdims=True)
        for k in range(CONV_W):
            dw_ref[k:k + 1, :] += jnp.sum(dm * sh[k][CONV_HALO:CONV_HALO + tm], axis=0, keepdims=True)

    def xspec(rows, fn):
        return pl.BlockSpec((rows, cw), lambda j, i: (fn(i), XBC_BLK + coff + j))

    prev = lambda i: jnp.maximum(i * hb - 1, 0)
    nxt = lambda i: jnp.minimum((i + 1) * hb, t // CONV_HALO - 1)
    dxbc, dw, db = pl.pallas_call(
        body, name=f"conv_bwd_{coff}", grid=(ncols // cw, nt),
        in_specs=[pl.BlockSpec((tm, cw), lambda j, i: (i, j)), pl.BlockSpec((CONV_HALO, cw), lambda j, i: (nxt(i), j)),
                  xspec(CONV_HALO, prev), xspec(tm, lambda i: i), xspec(CONV_HALO, nxt),
                  pl.BlockSpec((CONV_W, cw), lambda j, i: (0, coff + j)), pl.BlockSpec((1, cw), lambda j, i: (0, coff + j)),
                  pl.BlockSpec(memory_space=pl.ANY)],
        out_specs=[pl.BlockSpec((tm, cw), lambda j, i: (i, XBC_BLK + coff + j)), pl.BlockSpec((CONV_W, cw), lambda j, i: (0, j)),
                   pl.BlockSpec((1, cw), lambda j, i: (0, j))],
        out_shape=[jax.ShapeDtypeStruct(dproj.shape, BF16), jax.ShapeDtypeStruct((CONV_W, ncols), F32),
                   jax.ShapeDtypeStruct((1, ncols), F32)],
        input_output_aliases={7: 0},
        compiler_params=_cparams(("parallel", "arbitrary")),
    )(dxa, dxa, proj, proj, proj, conv_w, conv_b.reshape(1, D_XBC), dproj)
    return dxbc, dw, db.reshape(ncols)


def _cumsum(x, axis, reverse=False):
    n = x.shape[axis]
    idx = lax.broadcasted_iota(jnp.int32, x.shape, axis)
    k = 1
    while k < n:
        if reverse:
            x = x + jnp.where(idx < n - k, pltpu.roll(x, n - k, axis=axis), 0.0)
        else:
            x = x + jnp.where(idx >= k, pltpu.roll(x, k, axis=axis), 0.0)
        k *= 2
    return x


def _head_masks():
    lane = lax.broadcasted_iota(jnp.int32, (1, GROUP_W), 1)
    return [(lane >= r * HEAD_DIM) & (lane < (r + 1) * HEAD_DIM) for r in range(HEADS_PER_GROUP)]


def _expand_heads(cols, hm):
    out = jnp.where(hm[0], cols[:, 0:1], 0.0)
    for r in range(1, HEADS_PER_GROUP):
        out = out + jnp.where(hm[r], cols[:, r:r + 1], 0.0)
    return out


def _ssd_decay(dt_raw, dt_bias, a_log):
    t = dt_raw.shape[0]

    def body(raw_ref, b_ref, al_ref, dt_ref, acs_ref, sig_ref):
        raw = raw_ref[...] + b_ref[...]
        rowid = pl.program_id(0) * CHUNK + lax.broadcasted_iota(jnp.int32, (CHUNK, 1), 0)
        dt = jnp.where(rowid >= N_PAD, _softplus(raw), 0.0)
        dt_ref[...] = dt
        acs_ref[...] = _cumsum(dt * -jnp.exp(al_ref[...]), 0)
        sig_ref[...] = _sigmoid(raw)

    blk = pl.BlockSpec((CHUNK, DT_PAD), lambda c: (c, 0))
    vec = pl.BlockSpec((1, DT_PAD), lambda c: (0, 0))
    pad = lambda v: jnp.pad(v, (0, DT_PAD - N_HEADS)).reshape(1, DT_PAD)
    dt, acs, sig = pl.pallas_call(
        body, name="ssd_decay", grid=(t // CHUNK,), in_specs=[blk, vec, vec], out_specs=[blk, blk, blk],
        out_shape=[jax.ShapeDtypeStruct((t, DT_PAD), F32)] * 3,
        compiler_params=_cparams(("parallel",)),
    )(dt_raw, pad(dt_bias), pad(a_log))
    grp = lambda v: v[:, :N_HEADS].reshape(t, N_GROUPS, HEADS_PER_GROUP).transpose(1, 0, 2)
    acs_g = grp(acs)
    return dict(dt=grp(dt), acs=acs_g, acs_rows=acs_g.transpose(0, 2, 1), sig=grp(sig),
                a_log=a_log.reshape(N_GROUPS, 1, HEADS_PER_GROUP))


def _ssd_q(dtc, acs4, acs_r):
    hm = _head_masks()
    dt_exp = _expand_heads(dtc, hm)
    acs = _expand_heads(acs4, hm)
    atot = acs[CHUNK - 1:CHUNK, :]
    return dict(dtc=dtc, hm=hm, dt_exp=dt_exp, acs=acs, acs_r=acs_r,
                ea=jnp.exp(acs), ds=jnp.exp(atot - acs), dec=jnp.exp(atot))


def _stack4(x):
    return jnp.concatenate([x] * HEADS_PER_GROUP, axis=0)


def _ssd_decay_stacks(q):
    hpg = HEADS_PER_GROUP
    a_col = jnp.concatenate([q["acs"][:, r * HEAD_DIM:r * HEAD_DIM + 1] for r in range(hpg)], axis=0)
    a_row = jnp.concatenate([jnp.broadcast_to(q["acs_r"][r:r + 1, :], (CHUNK, CHUNK)) for r in range(hpg)], axis=0)
    ri = lax.broadcasted_iota(jnp.int32, (hpg * CHUNK, CHUNK), 0) % CHUNK
    ci = lax.broadcasted_iota(jnp.int32, (hpg * CHUNK, CHUNK), 1)
    diff = a_col - a_row
    lm = jnp.exp(jnp.where(ri >= ci, diff, -jnp.inf))
    lt = jnp.exp(jnp.where(ri <= ci, -diff, -jnp.inf))
    return lm, lt


def _pick_heads(stacked, hm):
    out = jnp.where(hm[0], stacked[:CHUNK], 0.0)
    for r in range(1, HEADS_PER_GROUP):
        out = out + jnp.where(hm[r], stacked[r * CHUNK:(r + 1) * CHUNK], 0.0)
    return out


def _mask_heads(x, hm):
    return jnp.concatenate([jnp.where(hm[r], x, 0.0) for r in range(HEADS_PER_GROUP)], axis=0)


SSD_GP = 4


def _ssd_specs(cidx):
    gp, hpg = SSD_GP, HEADS_PER_GROUP
    return [
        pl.BlockSpec((CHUNK, gp * GROUP_W), lambda g, c: (cidx(c), g)),
        pl.BlockSpec((CHUNK, gp * D_STATE), lambda g, c: (cidx(c), D_INNER // (gp * D_STATE) + g)),
        pl.BlockSpec((CHUNK, gp * D_STATE), lambda g, c: (cidx(c), (D_INNER + 1024) // (gp * D_STATE) + g)),
        pl.BlockSpec((gp, CHUNK, hpg), lambda g, c: (g, cidx(c), 0)),
        pl.BlockSpec((gp, CHUNK, hpg), lambda g, c: (g, cidx(c), 0)),
        pl.BlockSpec((gp, hpg, CHUNK), lambda g, c: (g, 0, cidx(c))),
    ]


def _ssd_fwd(xa, dec, carry=None):
    t = xa.shape[0]
    nc = t // CHUNK
    gp, gw, ds = SSD_GP, GROUP_W, D_STATE
    grid = (N_GROUPS // gp, nc)
    c_in, c_out, c_sem = carry.specs() if carry is not None else ([], [], [])
    nci, nco = len(c_in), len(c_out)

    def body(*refs):
        xs_ref, b_ref, c_ref, dt_ref, acs_ref, acsr_ref = refs[:6]
        y_ref, prev_ref = refs[6 + nci:8 + nci]
        st_ref = refs[8 + nci + nco]
        if carry is not None:
            begin, end = carry.at_edges(grid, refs[6:6 + nci], refs[8 + nci:8 + nci + nco], refs[-2:])
            begin()

        @pl.when(pl.program_id(1) == 0)
        def _():
            st_ref[...] = jnp.zeros_like(st_ref)

        for k in range(gp):
            q = _ssd_q(dt_ref[k], acs_ref[k], acsr_ref[k])
            xdt = xs_ref[:, k * gw:(k + 1) * gw] * q["dt_exp"]
            bm = b_ref[:, k * ds:(k + 1) * ds].astype(BF16)
            cm = c_ref[:, k * ds:(k + 1) * ds].astype(BF16)
            cb = _dot(cm, bm, _NT)
            st = st_ref[k]
            prev_ref[0, k] = st
            lm, _ = _ssd_decay_stacks(q)
            y_diag = _pick_heads(_dot((_stack4(cb) * lm).astype(BF16), xdt.astype(BF16)), q["hm"])
            y_ref[:, k * gw:(k + 1) * gw] = y_diag + _dot(cm, st.astype(BF16)) * q["ea"]
            st_ref[k] = q["dec"] * st + _dot(bm, (xdt * q["ds"]).astype(BF16), _TN)
        if carry is not None:
            end()

    res = pl.pallas_call(
        body, name="ssd_fwd", grid=grid,
        in_specs=_ssd_specs(lambda c: c) + c_in,
        out_specs=[pl.BlockSpec((CHUNK, gp * gw), lambda g, c: (c, g)),
                   pl.BlockSpec((1, gp, ds, gw), lambda g, c: (c, g, 0, 0))] + c_out,
        out_shape=[jax.ShapeDtypeStruct((t, D_INNER), F32), jax.ShapeDtypeStruct((nc, N_GROUPS, ds, gw), F32)]
        + (carry.outs if carry is not None else []),
        scratch_shapes=[pltpu.VMEM((gp, ds, gw), F32)] + c_sem,
        input_output_aliases=carry.aliases(6, 2) if carry is not None else {},
        compiler_params=_cparams(("arbitrary", "arbitrary") if carry is not None else ("parallel", "arbitrary")),
    )(xa, xa, xa, dec["dt"], dec["acs"], dec["acs_rows"], *(carry.ins if carry is not None else []))
    return res[0], res[1], res[2:]


def _ssd_bwd(dy, dsk, xa, prev, dec, carry=None):
    t = xa.shape[0]
    nc = t // CHUNK
    gp, gw, dstate = SSD_GP, GROUP_W, D_STATE
    hpg = HEADS_PER_GROUP

    grid = (N_GROUPS // gp, nc)
    c_in, c_out, c_sem = carry.specs() if carry is not None else ([], [], [])
    nci, nco = len(c_in), len(c_out)

    def body(*refs):
        ins, c_ins = refs[:11], refs[11:11 + nci]
        outs = refs[11 + nci:17 + nci]
        c_outs = refs[17 + nci:17 + nci + nco]
        dst_ref = refs[17 + nci + nco]
        dbias_ref, dalog_ref = outs[4], outs[5]
        if carry is not None:
            begin, end = carry.at_edges(grid, c_ins, c_outs, refs[-2:])
            begin()

        @pl.when(pl.program_id(1) == 0)
        def _():
            dst_ref[...] = jnp.zeros_like(dst_ref)
            dbias_ref[...] = jnp.zeros_like(dbias_ref)
            dalog_ref[...] = jnp.zeros_like(dalog_ref)

        for k in range(gp):
            one_group(k, *ins, *outs, dst_ref)
        if carry is not None:
            end()

    def one_group(k, xs_ref, b_ref, c_ref, dt_ref, acs_ref, acsr_ref, sig_ref, al_ref, dy_ref, sk_ref, prev_ref,
                  dxs_ref, db_ref, dc_ref, ddt_ref, dbias_ref, dalog_ref, dst_ref):
        wide = slice(k * gw, (k + 1) * gw)
        narrow = slice(k * dstate, (k + 1) * dstate)
        q = _ssd_q(dt_ref[k], acs_ref[k], acsr_ref[k])
        a_r = -jnp.exp(al_ref[k])
        hm, ds, dec, dt_exp = q["hm"], q["ds"], q["dec"], q["dt_exp"]
        xs = xs_ref[:, wide]
        xdt = xs * dt_exp
        xdtb = xdt.astype(BF16)
        bm = b_ref[:, narrow].astype(BF16)
        cm = c_ref[:, narrow].astype(BF16)
        cb = _dot(cm, bm, _NT)
        bc = _dot(bm, cm, _NT)
        dyv = dy_ref[:, wide]
        dye = (dyv * q["ea"]).astype(BF16)
        pst = prev_ref[0, k]
        dst = dst_ref[k]
        dstb = dst.astype(BF16)
        dx_state = ds * _dot(bm, dstb)
        onehot = [(lax.broadcasted_iota(jnp.int32, (1, hpg), 1) == r).astype(F32) for r in range(hpg)]
        lm, lt = _ssd_decay_stacks(q)
        dyb = dyv.astype(BF16)
        gl = _dot(_mask_heads(dyv, hm).astype(BF16), xdtb, _NT) * lm
        glt = _dot(_mask_heads(xdt, hm).astype(BF16), dyb, _NT) * lt
        bc4 = _stack4(bc)
        dxdt = dx_state + _pick_heads(_dot((bc4 * lt).astype(BF16), dyb), hm)
        wd = jnp.sum(gl * _stack4(cb), axis=1, keepdims=True) - jnp.sum(glt * bc4, axis=1, keepdims=True)
        dcb = gl[:CHUNK]
        dcbt = glt[:CHUNK]
        qa = wd[:CHUNK] * onehot[0]
        for r in range(1, hpg):
            dcb = dcb + gl[r * CHUNK:(r + 1) * CHUNK]
            dcbt = dcbt + glt[r * CHUNK:(r + 1) * CHUNK]
            qa = qa + wd[r * CHUNK:(r + 1) * CHUNK] * onehot[r]
        pstb = pst.astype(BF16)
        dc_ref[:, narrow] = _dot(dcb.astype(BF16), bm) + _dot(dye, pstb, _NT)
        db_ref[:, narrow] = _dot(dcbt.astype(BF16), cm) + _dot((xdt * ds).astype(BF16), dstb, _NT)
        dst_ref[k] = dec * dst + _dot(cm, dye, _TN)
        dxs_ref[:, wide] = dxdt * dt_exp + dyv * sk_ref[:, wide]

        t2 = xdt * dx_state
        t1 = dyv * (_dot(cm, pstb) * q["ea"]) - t2
        t3 = dst * pst
        t4 = dxdt * xs
        xd = jnp.zeros((CHUNK, hpg), F32)
        dal = jnp.zeros((1, hpg), F32)
        for r in range(hpg):
            qa = qa + jnp.sum(jnp.where(hm[r], t1, 0.0), axis=1, keepdims=True) * onehot[r]
            xd = xd + jnp.sum(jnp.where(hm[r], t4, 0.0), axis=1, keepdims=True) * onehot[r]
            s2 = jnp.sum(jnp.sum(jnp.where(hm[r], t2, 0.0), axis=1, keepdims=True), axis=0, keepdims=True)
            s3 = jnp.sum(jnp.sum(jnp.where(hm[r], t3, 0.0), axis=1, keepdims=True), axis=0, keepdims=True)
            dal = dal + (s2 + dec[:, r * HEAD_DIM:r * HEAD_DIM + 1] * s3) * onehot[r]
        rc = _cumsum(qa, 0, reverse=True) + dal
        rowid = (nc - 1 - pl.program_id(1)) * CHUNK + lax.broadcasted_iota(jnp.int32, (CHUNK, 1), 0)
        ddt_raw = jnp.where(rowid >= N_PAD, (rc * a_r + xd) * sig_ref[k], 0.0)
        ddt_ref[k] = ddt_raw
        dbias_ref[k] += jnp.sum(ddt_raw, axis=0, keepdims=True)
        dalog_ref[k] += jnp.sum(rc * q["dtc"], axis=0, keepdims=True) * a_r

    rev = lambda c: nc - 1 - c
    blk = pl.BlockSpec((CHUNK, gp * gw), lambda g, c: (rev(c), g))
    nblk = pl.BlockSpec((CHUNK, gp * dstate), lambda g, c: (rev(c), g))
    cols = pl.BlockSpec((gp, CHUNK, hpg), lambda g, c: (g, rev(c), 0))
    small = pl.BlockSpec((gp, 1, hpg), lambda g, c: (g, 0, 0))
    res = pl.pallas_call(
        body, name="ssd_bwd", grid=grid,
        in_specs=_ssd_specs(rev) + [cols, small, blk, pl.BlockSpec((1, gp * gw), lambda g, c: (0, g)),
                                    pl.BlockSpec((1, gp, dstate, gw), lambda g, c: (rev(c), g, 0, 0))]
        + c_in,
        out_specs=[blk, nblk, nblk, cols, small, small] + c_out,
        out_shape=[jax.ShapeDtypeStruct((t, D_INNER), F32), jax.ShapeDtypeStruct((t, N_GROUPS * dstate), F32),
                   jax.ShapeDtypeStruct((t, N_GROUPS * dstate), F32), jax.ShapeDtypeStruct((N_GROUPS, t, hpg), F32),
                   jax.ShapeDtypeStruct((N_GROUPS, 1, hpg), F32), jax.ShapeDtypeStruct((N_GROUPS, 1, hpg), F32)]
        + (carry.outs if carry is not None else []),
        scratch_shapes=[pltpu.VMEM((gp, dstate, gw), F32)] + c_sem,
        compiler_params=_cparams(("arbitrary", "arbitrary") if carry is not None else ("parallel", "arbitrary")),
    )(xa, xa, xa, dec["dt"], dec["acs"], dec["acs_rows"], dec["sig"], dec["a_log"], dy, dsk.reshape(1, D_INNER), prev,
      *(carry.ins if carry is not None else []))
    dxs, db, dc, ddt, dbias, dalog = res[:6]
    ddt_raw = ddt.transpose(1, 0, 2).reshape(t, N_HEADS)
    return dxs, db, dc, ddt_raw, dbias.reshape(N_HEADS), dalog.reshape(N_HEADS), res[6:]


def _ssd_post_fwd(y, xa, proj, dsk, nw):
    t = y.shape[0]
    di = D_INNER
    tm = _tile(t, (384, 128))

    def body(y_ref, xs_ref, z_ref, dsk_ref, nw_ref, o_ref):
        z = z_ref[...]
        yz = (y_ref[...] + xs_ref[...] * dsk_ref[...]) * (z * _sigmoid(z))
        nwv = nw_ref[...]
        for g in range(N_GROUPS):
            sl = slice(g * GROUP_W, (g + 1) * GROUP_W)
            v = yz[:, sl]
            rg = lax.rsqrt(jnp.mean(v * v, axis=-1, keepdims=True) + EPS)
            o_ref[:, sl] = (v * rg * nwv[:, sl]).astype(BF16)

    row = pl.BlockSpec((tm, di), lambda i: (i, 0))
    vec = pl.BlockSpec((1, di), lambda i: (0, 0))
    return pl.pallas_call(
        body, name="ssd_post_fwd", grid=(t // tm,),
        in_specs=[row, row, row, vec, vec], out_specs=row, out_shape=jax.ShapeDtypeStruct((t, di), BF16),
        compiler_params=_cparams(("parallel",)),
    )(y, xa, proj, dsk.reshape(1, di), nw.reshape(1, di))


def _ssd_post_bwd(dyn, y, xa, proj, dsk, nw, dproj):
    t = y.shape[0]
    di = D_INNER
    tm = CHUNK

    def body(dyn_ref, y_ref, xs_ref, z_ref, dsk_ref, nw_ref, _, dy_ref, dz_ref, dnw_ref, ddsk_ref):
        @pl.when(pl.program_id(0) == 0)
        def _():
            dnw_ref[...] = jnp.zeros_like(dnw_ref)
            ddsk_ref[...] = jnp.zeros_like(ddsk_ref)

        for g in range(N_GROUPS):
            sl = slice(g * GROUP_W, (g + 1) * GROUP_W)
            z = z_ref[:, sl]
            sig = _sigmoid(z)
            s = z * sig
            xs = xs_ref[:, sl]
            dskv = dsk_ref[:, sl]
            yt = y_ref[:, sl] + xs * dskv
            yz = yt * s
            rg = lax.rsqrt(jnp.mean(yz * yz, axis=-1, keepdims=True) + EPS)
            xhat = yz * rg
            dynv = dyn_ref[:, sl]
            gg = dynv * nw_ref[:, sl]
            dnw_ref[:, sl] += jnp.sum(dynv * xhat, axis=0, keepdims=True)
            dyz = rg * (gg - xhat * jnp.mean(gg * xhat, axis=-1, keepdims=True))
            dyt = dyz * s
            dy_ref[:, sl] = dyt
            dz_ref[:, sl] = (dyz * yt * (sig * (1.0 + z * (1.0 - sig)))).astype(BF16)
            ddsk_ref[:, sl] += jnp.sum(dyt * xs, axis=0, keepdims=True)

    row = pl.BlockSpec((tm, di), lambda i: (i, 0))
    vec = pl.BlockSpec((1, di), lambda i: (0, 0))
    dy, dz, dnw, ddsk = pl.pallas_call(
        body, name="ssd_post_bwd", grid=(t // tm,),
        in_specs=[row, row, row, row, vec, vec, pl.BlockSpec(memory_space=pl.ANY)], out_specs=[row, row, vec, vec],
        out_shape=[jax.ShapeDtypeStruct((t, di), F32), jax.ShapeDtypeStruct(dproj.shape, BF16),
                   jax.ShapeDtypeStruct((1, di), F32), jax.ShapeDtypeStruct((1, di), F32)],
        input_output_aliases={6: 1},
        compiler_params=_cparams(("arbitrary",)),
    )(dyn, y, xa, proj, dsk.reshape(1, di), nw.reshape(1, di), dproj)
    return dy, dz, dnw.reshape(di), ddsk.reshape(N_HEADS, HEAD_DIM).sum(axis=1)


SHARD_COLS = IN_COLS // N_CHIPS


def _split_w_in(w_in_sh):
    nl = w_in_sh.shape[0]
    w = w_in_sh.transpose(0, 2, 1, 3).reshape(nl, D_MODEL, IN_COLS)
    main = jnp.concatenate([w[..., OFF_Z:OFF_XBC], w[..., OFF_GATE:], w[..., :OFF_Z], w[..., OFF_XBC:OFF_DT]], axis=-1)
    dt = jnp.pad(w[..., OFF_DT:OFF_GATE], ((0, 0), (0, 0), (0, DT_PAD - N_HEADS)))
    return main, dt


def _merge_dw_in(dmain, ddt):
    full = jnp.concatenate([dmain[:, PPOOL:PXBC], dmain[:, PZ:PGATE], dmain[:, PXBC:], ddt[:, :N_HEADS], dmain[:, PGATE:PPOOL]],
                           axis=1)
    return full.reshape(D_MODEL, N_CHIPS, SHARD_COLS).transpose(1, 0, 2)


def _relu2_epilogue(acc):
    hid = jnp.maximum(acc, 0.0)
    return acc, hid * hid


def _relu2_bwd_epilogue(acc, pre):
    return (acc * 2.0 * jnp.maximum(pre, 0.0),)


def _add_epilogue(acc, res):
    return (acc + res,)


def _w_spec(rows, cols, fn):
    return pl.BlockSpec((None, None, rows, cols), fn)


def _with_carry(stage, call):
    carry = stage[0]() if stage is not None else None
    res = call(carry)
    if carry is None:
        return res
    outs, got = res
    stage[1](got)
    return outs[0] if len(outs) == 1 else outs


def _layer_fwd(h, w, big, layer, stages=None):
    t = h.shape[0]
    d = D_MODEL
    tm = _tile(t, (1408, 384, 128))
    nt = t // tm
    row = _bs((tm, d), lambda i, j, k: (i, 0))
    stages = stages or {}
    s = {"h": h}
    u = _norm_fwd(h, w["mix_norm_w"], "norm_mix")
    proj = _with_carry(stages.get("a_ici"), lambda carry: _mm(
        f"mm_proj_{layer}", _NN, (nt, PCOLS // d, 1), u, row,
        big["w_in_main"], _bs((None, d, d), lambda i, j, k: (0, 0, j)),
        [((t, PCOLS), F32, _bs((tm, d), lambda i, j, k: (i, j)))], carry=carry))
    dt_raw = _mm(f"mm_dt_{layer}", _NN, (nt, 1, 1), u, row,
                 big["w_in_dt"], _bs((None, d, DT_PAD), lambda i, j, k: (0, 0, 0)),
                 [((t, DT_PAD), F32, _bs((tm, DT_PAD), lambda i, j, k: (i, 0)))])
    pooled, yg, ypm, y_pool = _pool_fwd(proj, big["pool_w_group"], w["pool_scale"], big["w_pool_up"], layer)
    xa = _conv_fwd(proj, w["conv_w"], w["conv_b"])
    dec = _ssd_decay(dt_raw, w["dt_bias"], w["a_log"])
    stage = stages.get("b_ici")
    carry = stage[0]() if stage is not None else None
    y, prev, got = _ssd_fwd(xa, dec, carry)
    if carry is not None:
        stage[1](got)
    yn = _ssd_post_fwd(y, xa, proj, w["d_skip_exp"], w["ssd_norm_w"])
    rs = D_INNER // N_CHIPS
    y_ssd = _with_carry(stages.get("a_d2d"), lambda carry: _mm(
        f"mm_ssd_out_{layer}", _NN, (nt, 1, N_CHIPS), yn, _bs((tm, rs), lambda i, j, k: (i, k)),
        big["w_ssd_out"], _w_spec(rs, d, lambda i, j, k: (0, k, 0, 0)), [((t, d), F32, row)], carry=carry))
    merged = _gate_fwd(proj, w["b_gate"], y_pool, y_ssd)
    ro = d // N_CHIPS
    h1 = _mm(f"mm_o_{layer}", _NN, (nt, 1, N_CHIPS), merged, _bs((tm, ro), lambda i, j, k: (i, k)),
             big["w_o"], _w_spec(ro, d, lambda i, j, k: (0, k, 0, 0)), [((t, d), F32, row)],
             extras=[(h, row)], epilogue=_add_epilogue)
    v = _norm_fwd(h1, w["mlp_norm_w"], "norm_mlp")
    tile = _bs((tm, d), lambda i, j, k: (i, j))
    pre, act = _with_carry(stages.get("b_d2d"), lambda carry: _mm(
        f"mm_ff1_{layer}", _NN, (nt, N_CHIPS, 1), v, row,
        big["w_ff1"], _w_spec(d, d, lambda i, j, k: (0, j, 0, 0)),
        [((t, D_FF), F32, tile), ((t, D_FF), BF16, tile)], epilogue=_relu2_epilogue, carry=carry))
    h2 = _mm(f"mm_ff2_{layer}", _NN, (nt, 1, N_CHIPS), act, _bs((tm, d), lambda i, j, k: (i, k)),
             big["w_ff2"], _w_spec(d, d, lambda i, j, k: (0, k, 0, 0)), [((t, d), F32, row)],
             extras=[(h1, row)], epilogue=_add_epilogue)
    s.update(u=u, proj=proj, dec=dec, pooled=pooled, yg=yg, ypm=ypm, y_pool=y_pool, xa=xa, y=y, prev=prev, yn=yn,
             y_ssd=y_ssd, merged=merged, h1=h1, v=v, pre=pre, act=act)
    return h2, s


def _dw(name, layer, a, b, shard_shape, by, tk, carry=None):
    t = a.shape[0]
    nk = t // tk
    rows, cols = shard_shape
    if by == "rows":
        grid = (N_CHIPS, 1, nk)
        a_spec = _bs((tk, rows), lambda i, j, k: (k, i))
        b_spec = _bs((tk, cols), lambda i, j, k: (k, 0))
        o_spec = _bs((None, rows, cols), lambda i, j, k: (i, 0, 0))
    else:
        grid = (1, N_CHIPS, nk)
        a_spec = _bs((tk, rows), lambda i, j, k: (k, 0))
        b_spec = _bs((tk, cols), lambda i, j, k: (k, j))
        o_spec = _bs((None, rows, cols), lambda i, j, k: (j, 0, 0))
    return _mm(f"{name}_{layer}", _TN, grid, a, a_spec, b, b_spec, [((N_CHIPS, rows, cols), F32, o_spec)], carry=carry)


EARLY = ("w_ff2", "w_ff1", "w_o", "w_ssd_out")


def _layer_bwd(dh, s, w, big, layer, red=None, last=False):
    t = dh.shape[0]
    d = D_MODEL
    tm = _tile(t, (1408, 384, 128))
    tk = _tile(t, (1408, 384, 128))
    nt = t // tm
    row = _bs((tm, d), lambda i, j, k: (i, 0))
    tile = _bs((tm, d), lambda i, j, k: (i, j))
    g, gb = {}, {}
    carry = red.swap_carry() if red is not None else None
    dpre = _mm(f"mm_dact_{layer}", _NT, (nt, N_CHIPS, 1), dh, row,
               big["w_ff2"], _w_spec(d, d, lambda i, j, k: (0, j, 0, 0)), [((t, D_FF), BF16, tile)],
               extras=[(s["pre"], tile)], epilogue=_relu2_bwd_epilogue, carry=carry)
    if carry is not None:
        (dpre,), got = dpre
        red.after_swap(got)
    gb["w_ff2"] = _dw("mm_dw_ff2", layer, s["act"], dh, (d, d), "rows", tk)
    dv = _mm(f"mm_dv_{layer}", _NT, (nt, 1, N_CHIPS), dpre, _bs((tm, d), lambda i, j, k: (i, k)),
             big["w_ff1"], _w_spec(d, d, lambda i, j, k: (0, k, 0, 0)), [((t, d), F32, row)])
    gb["w_ff1"] = _dw("mm_dw_ff1", layer, s["v"], dpre, (d, d), "cols", tk)
    dh1, g["mlp_norm_w"] = _norm_bwd(dh, dv, s["h1"], w["mlp_norm_w"], "norm_mlp_bwd")
    ro = d // N_CHIPS
    dmerged = _mm(f"mm_dmerged_{layer}", _NT, (nt, N_CHIPS, 1), dh1, row,
                  big["w_o"], _w_spec(ro, d, lambda i, j, k: (0, j, 0, 0)),
                  [((t, d), F32, _bs((tm, ro), lambda i, j, k: (i, j)))])
    gb["w_o"] = _dw("mm_dw_o", layer, s["merged"], dh1, (ro, d), "rows", tk)
    dy_pool, dy_ssd, dproj, g["b_gate"] = _gate_bwd(dmerged, s["proj"], w["b_gate"], s["y_pool"], s["y_ssd"])
    rs = D_INNER // N_CHIPS
    dyn = _mm(f"mm_dyn_{layer}", _NT, (nt, N_CHIPS, 1), dy_ssd, row,
              big["w_ssd_out"], _w_spec(rs, d, lambda i, j, k: (0, j, 0, 0)),
              [((t, D_INNER), F32, _bs((tm, rs), lambda i, j, k: (i, j)))])
    gb["w_ssd_out"] = _dw("mm_dw_ssd_out", layer, s["yn"], dy_ssd, (rs, d), "rows", tk)
    dy, dproj, g["ssd_norm_w"], g["d_skip"] = _ssd_post_bwd(dyn, s["y"], s["xa"], s["proj"], w["d_skip_exp"],
                                                            w["ssd_norm_w"], dproj)
    carry = red.ici_carry() if red is not None else None
    dxs, db, dc, ddt_raw, g["dt_bias"], g["a_log"], got = _ssd_bwd(dy, w["d_skip_exp"], s["xa"], s["prev"], s["dec"], carry)
    if carry is not None:
        red.after_ici(got)
    own = red is not None and last
    if own:
        red.push(layer, {n: gb[n] for n in EARLY})
    dproj, dcw1, dcb1 = _conv_bwd(dxs, 0, s["proj"], w["conv_w"], w["conv_b"], dproj)
    dproj, dcw2, dcb2 = _conv_bwd(db, 2, s["proj"], w["conv_w"], w["conv_b"], dproj)
    dproj, dcw3, dcb3 = _conv_bwd(dc, 3, s["proj"], w["conv_w"], w["conv_b"], dproj)
    g["conv_w"] = jnp.concatenate([dcw1, dcw2, dcw3], axis=1)
    g["conv_b"] = jnp.concatenate([dcb1, dcb2, dcb3])
    q, dyg, g["pool_scale"] = _pool_bwd_a(dy_pool, s["yg"], w["pool_scale"], big["pool_w_group"], big["w_pool_up"], layer)
    gb["w_pool_up"] = _with_carry((red.swap_carry, red.after_swap) if own else None, lambda carry: _dw(
        "mm_dw_pool_up", layer, s["ypm"], dy_pool, (ro, d), "rows", tk, carry))
    gb["pool_w_group"] = _pool_dwg(s["pooled"], dyg, layer).reshape(N_CHIPS, POOL_GDIM, POOL_GDIM)
    dproj = _pool_bwd_b(q, dproj)
    ddt = jnp.pad(ddt_raw.astype(BF16), ((0, 0), (0, DT_PAD - N_HEADS)))
    nk = PCOLS // d
    du = _with_carry((red.ici_carry, red.after_ici) if own else None, lambda carry: _mm(
        f"mm_du_{layer}", _NT, (nt, 1, nk), dproj, _bs((tm, d), lambda i, j, k: (i, k)),
        big["w_in_main"], _bs((None, d, d), lambda i, j, k: (0, 0, k)), [((t, d), F32, row)], carry=carry))
    du = _mm(f"mm_du_dt_{layer}", _NT, (nt, 1, 1), ddt, _bs((tm, DT_PAD), lambda i, j, k: (i, 0)),
             big["w_in_dt"], _bs((None, d, DT_PAD), lambda i, j, k: (0, 0, 0)), [((t, d), F32, row)],
             extras=[(du, row)], epilogue=_add_epilogue)
    ntk = t // tk
    u_spec = _bs((tk, d), lambda i, j, k: (k, 0))
    dmain = _mm(f"mm_dw_in_{layer}", _TN, (1, nk, ntk), s["u"], u_spec, dproj, _bs((tk, d), lambda i, j, k: (k, j)),
                [((d, PCOLS), F32, _bs((d, d), lambda i, j, k: (0, j)))])
    ddtw = _mm(f"mm_dw_dt_{layer}", _TN, (1, 1, ntk), s["u"], u_spec, ddt, _bs((tk, DT_PAD), lambda i, j, k: (k, 0)),
               [((d, DT_PAD), F32, _bs((d, DT_PAD), lambda i, j, k: (0, 0)))])
    gb["w_in"] = _merge_dw_in(dmain, ddtw)
    dh0, g["mix_norm_w"] = _norm_bwd(dh1, du, s["h"], w["mix_norm_w"], "norm_mix_bwd")
    if red is not None:
        red.push(layer, {n: gb[n] for n in BIG if not (own and n in EARLY)})
    return dh0, g, gb


BIG = ("w_in", "pool_w_group", "w_pool_up", "w_ssd_out", "w_o", "w_ff1", "w_ff2")
PER_LAYER = ("mix_norm_w", "w_in", "b_gate", "pool_w_group", "pool_scale", "w_pool_up", "conv_w", "conv_b", "dt_bias",
             "a_log", "d_skip", "ssd_norm_w", "w_ssd_out", "w_o", "mlp_norm_w", "w_ff1", "w_ff2")


SMALL_PER_LAYER = tuple(n for n in PER_LAYER if n not in BIG)


SHARD_SHAPE = {"w_in": (1024, SHARD_COLS), "pool_w_group": (4, WG_ROWS, POOL_GDIM), "w_pool_up": (WUP_ROWS, D_MODEL),
               "w_ssd_out": (D_INNER // N_CHIPS, D_MODEL), "w_o": (D_MODEL // N_CHIPS, D_MODEL),
               "w_ff1": (D_MODEL, D_FF // N_CHIPS), "w_ff2": (D_FF // N_CHIPS, D_MODEL)}


def _layer_view(mats):
    big = {n: mats[n].reshape((1, N_CHIPS) + SHARD_SHAPE[n]) for n in BIG if n != "w_in"}
    big["w_in_main"], big["w_in_dt"] = _split_w_in(mats["w_in"])
    return big


class _HeldWeights:
    def __init__(self, stacked):
        self.stacked = stacked

    def layer(self, i):
        return _layer_view({n: self.stacked[n][i:i + 1].reshape(1, N_CHIPS, -1, self.stacked[n].shape[-1]) for n in BIG})

    def stages(self, i):
        return None


def _local_step(x, tgt, weights, small, red=None):
    seq = x.shape[0]
    h = jnp.concatenate([jnp.zeros((N_PAD, D_MODEL), F32), small["meta_tokens"], x], axis=0)
    saved, ws, bigs = [], [], []
    for i in range(DEPTH):
        w = {n: small[n][i] for n in SMALL_PER_LAYER}
        w["d_skip_exp"] = jnp.repeat(w["d_skip"], HEAD_DIM)
        big = weights.layer(i)
        h, s = _layer_fwd(h, w, big, i, weights.stages(i))
        saved.append(s)
        ws.append(w)
        bigs.append(big)
    loss, dh, g_final = _loss_head(h, small["final_norm_w"], tgt)
    layer_g, layer_gb = [None] * DEPTH, [None] * DEPTH
    for i in reversed(range(DEPTH)):
        dh, layer_g[i], layer_gb[i] = _layer_bwd(dh, saved[i], ws[i], bigs[i], i, red, last=(i == 0))
    grads = {n: jnp.stack([layer_g[i][n] for i in range(DEPTH)]) for n in SMALL_PER_LAYER}
    grads["final_norm_w"] = g_final
    grads["meta_tokens"] = dh[N_PAD:ROW_X]
    return loss, dh[ROW_X:ROW_X + seq], layer_gb, grads


MESH = pl.DeviceIdType.MESH
LANES = 128
ANY = pl.BlockSpec(memory_space=pl.ANY)


def _place():
    x, y, c = lax.axis_index("x"), lax.axis_index("y"), lax.axis_index("c")
    chips = [(1 - x, y), (x, 1 - y), (1 - x, 1 - y)]
    return x, y, c, chips


def _remote(src, dst, send_sem, recv_sem, to):
    return pltpu.make_async_remote_copy(src_ref=src, dst_ref=dst, send_sem=send_sem, recv_sem=recv_sem,
                                        device_id=to, device_id_type=MESH)


class _WeightGatherer:
    GROUPS = {"a": ("w_in",), "b": tuple(n for n in BIG if n != "w_in")}

    def __init__(self, mine):
        self.mine = mine
        self.landing = {}
        self.ready = {}

    def _ici(self, layer, group):
        names = self.GROUPS[group]
        srcs = [self.mine[n] for n in names]
        outs = [jax.ShapeDtypeStruct((1, N_CHIPS) + s.shape[1:], BF16) for s in srcs]

        def make(ins, out, send_sems, recv_sems):
            x, y, c, chips = _place()
            me = 2 * x + y
            cps = []
            for p, ref in enumerate(ins):
                half = ref.shape[1] // 2
                rows = pl.ds(c * half, half)
                cps += [_remote(ref.at[layer, rows, :], out[p].at[0, me, rows, :], send_sems.at[3 * p + j],
                                recv_sems.at[3 * p + j], (*chip, c)) for j, chip in enumerate(chips)]
            return cps

        return _Carry(srcs, outs, 3 * len(names), make)

    def _d2d(self, layer, group):
        names = self.GROUPS[group]
        n = len(names)
        bufs = self.landing[(layer, group)]
        outs = [jax.ShapeDtypeStruct(b.shape, b.dtype) for b in bufs]

        def make(ins, out, send_sems, recv_sems):
            x, y, c, chips = _place()
            me = 2 * x + y
            sibling = (x, y, 1 - c)
            cps = []
            for p in range(n):
                half = out[p].shape[2] // 2
                for j, (cx, cy) in enumerate(chips):
                    blk = out[p].at[0, 2 * cx + cy, pl.ds(c * half, half), :]
                    cps.append(_remote(blk, blk, send_sems.at[4 * p + j], recv_sems.at[4 * p + j], sibling))
                cps.append(_remote(ins[n + p].at[layer], out[p].at[0, me], send_sems.at[4 * p + 3],
                                   recv_sems.at[4 * p + 3], sibling))
            return cps

        return _Carry(list(bufs) + [self.mine[m] for m in names], outs, 4 * n, make, alias={p: p for p in range(n)})

    def _landed(self, layer, group, bufs):
        self.landing[(layer, group)] = bufs

    def _done(self, layer, group, bufs):
        self.ready.setdefault(layer, {}).update(zip(self.GROUPS[group], bufs))

    def layer(self, i):
        if i == 0:
            for g in self.GROUPS:
                self._landed(0, g, _run_carry(self._ici(0, g), f"gather_ici_{g}_0"))
            for g in self.GROUPS:
                self._done(0, g, _run_carry(self._d2d(0, g), f"gather_d2d_{g}_0"))
        return _layer_view(self.ready[i])

    def stages(self, i):
        nxt = i + 1
        if nxt == DEPTH:
            return None
        st = {}
        for g in self.GROUPS:
            st[f"{g}_ici"] = (functools.partial(self._ici, nxt, g), functools.partial(self._landed, nxt, g))
            st[f"{g}_d2d"] = (functools.partial(self._d2d, nxt, g), functools.partial(self._done, nxt, g))
        return st


def _exchange_small(v, reduce, name):
    rows_per = v.shape[0]
    vm = pl.BlockSpec(memory_space=pltpu.VMEM)

    def body(v_ref, out_ref, *scratch):
        if reduce:
            land_ref, send_sems, recv_sems, local_sem = scratch
        else:
            land_ref = out_ref
            send_sems, recv_sems, local_sem = scratch
        x, y, c, chips = _place()
        me, sibling = (x, y, c), (x, y, 1 - c)

        def rows(px, py, pc):
            return land_ref.at[4 * px + 2 * py + pc]

        def copy(k, block, to, src=None):
            return _remote(rows(*block) if src is None else src, rows(*block), send_sems.at[k], recv_sems.at[k], to)

        mine = pltpu.make_async_copy(v_ref, rows(*me), local_sem)
        mine.start()
        first = [copy(0, me, sibling, src=v_ref)]
        first += [copy(1 + j, me, (*chip, c), src=v_ref) for j, chip in enumerate(chips)]
        for cp in first:
            cp.start()
        passed = [copy(4 + j, (*chip, c), sibling) for j, chip in enumerate(chips)]
        for j, chip in enumerate(chips):
            copy(1 + j, (*chip, c), me).wait_recv()
            passed[j].start()
        copy(0, sibling, me).wait_recv()
        for j, chip in enumerate(chips):
            copy(4 + j, (*chip, 1 - c), me).wait_recv()
        for cp in first + passed:
            cp.wait_send()
        mine.wait()
        if reduce:
            acc = land_ref[0]
            for d in range(1, 8):
                acc = acc + land_ref[d]
            out_ref[...] = acc

    sems = [pltpu.SemaphoreType.DMA((7,)), pltpu.SemaphoreType.DMA((7,)), pltpu.SemaphoreType.DMA]
    if reduce:
        out_shape = jax.ShapeDtypeStruct((rows_per, LANES), F32)
        scratch = [pltpu.VMEM((8, rows_per, LANES), F32)] + sems
    else:
        out_shape = jax.ShapeDtypeStruct((8, rows_per, LANES), F32)
        scratch = sems
    return pl.pallas_call(
        body, name=name, in_specs=[vm], out_specs=vm, out_shape=out_shape, scratch_shapes=scratch,
        compiler_params=pltpu.CompilerParams(vmem_limit_bytes=VMEM_LIMIT),
    )(v)


def _run_carry(carry, name):
    c_in, c_out, c_sem = carry.specs()
    n_in, n_out = len(c_in), len(c_out)

    def body(*refs):
        cps = carry.make(refs[:n_in], refs[n_in:n_in + n_out], *refs[n_in + n_out:])
        for cp in cps:
            cp.start()
        for cp in cps:
            cp.wait()

    return pl.pallas_call(body, name=name, in_specs=c_in, out_specs=c_out, out_shape=carry.outs,
                          scratch_shapes=c_sem, input_output_aliases=carry.aliases(0, 0))(*carry.ins)


def _row_tile(rows, last, itemsize=4, budget=2 * 1024 * 1024):
    return _tile(rows, tuple(t for t in (2048, 1024, 512, 256, 128, 64, 32, 16) if t * last * itemsize <= budget))


def _rs_add(g, got, core, name):
    _, half, last = got.shape
    tr = _row_tile(half, last)
    nb = half // tr

    def body(c_ref, g_ref, got_ref, o_ref):
        o_ref[...] = (g_ref[...] + got_ref[...]).astype(BF16)

    blk = pl.BlockSpec((None, tr, last), lambda s, i, c_ref: (s, i, 0))
    return pl.pallas_call(
        body, name=name,
        grid_spec=pltpu.PrefetchScalarGridSpec(
            num_scalar_prefetch=1, grid=(N_CHIPS, nb),
            in_specs=[pl.BlockSpec((None, tr, last), lambda s, i, c_ref: (s, c_ref[0] * nb + i, 0)), blk],
            out_specs=blk),
        out_shape=jax.ShapeDtypeStruct(got.shape, BF16),
        compiler_params=_cparams(("parallel", "parallel")),
    )(core.reshape(1).astype(jnp.int32), g, got)


def _rs_sum(own, got, acc, layer, chip, core, name):
    _, half, last = own.shape
    tr = _row_tile(half, last, budget=1024 * 1024)
    nb = half // tr
    ni = 0 if acc is None else 1

    def body(k_ref, c_ref, own_ref, got_ref, *rest):
        o_ref = rest[-1]
        tot = own_ref[...].astype(F32)
        for j in range(3):
            tot = tot + got_ref[j].astype(F32)
        o_ref[...] = tot

    return pl.pallas_call(
        body, name=name,
        grid_spec=pltpu.PrefetchScalarGridSpec(
            num_scalar_prefetch=2, grid=(nb,),
            in_specs=[pl.BlockSpec((None, tr, last), lambda i, k_ref, c_ref: (k_ref[0], i, 0)),
                      pl.BlockSpec((3, tr, last), lambda i, k_ref, c_ref: (0, i, 0))]
            + [pl.BlockSpec(memory_space=pl.ANY)] * ni,
            out_specs=pl.BlockSpec((None, tr, last), lambda i, k_ref, c_ref: (layer, c_ref[0] * nb + i, 0))),
        out_shape=jax.ShapeDtypeStruct((DEPTH, 2 * half, last), F32),
        input_output_aliases={4: 0} if ni else {},
        compiler_params=_cparams(("parallel",)),
    )(chip.reshape(1).astype(jnp.int32), core.reshape(1).astype(jnp.int32), own, got, *([acc] if ni else []))


def _rs_share(arrs):
    n = len(arrs)

    def body(*refs):
        out = refs[n:2 * n]
        send_sems, recv_sems = refs[2 * n:]
        x, y, c, _ = _place()
        sibling = (x, y, 1 - c)
        cps = []
        for p in range(n):
            half = out[p].shape[1] // 2
            mine = out[p].at[:, pl.ds(c * half, half), :]
            cps.append(_remote(mine, mine, send_sems.at[p], recv_sems.at[p], sibling))
        for cp in cps:
            cp.start()
        for p, cp in enumerate(cps):
            cp.wait_send()
            half = out[p].shape[1] // 2
            other = out[p].at[:, pl.ds((1 - c) * half, half), :]
            _remote(other, other, send_sems.at[p], recv_sems.at[p], sibling).wait_recv()

    return pl.pallas_call(
        body, name="rs_share", in_specs=[ANY] * n, out_specs=[ANY] * n,
        out_shape=[jax.ShapeDtypeStruct(a.shape, a.dtype) for a in arrs],
        input_output_aliases={p: p for p in range(n)},
        scratch_shapes=[pltpu.SemaphoreType.DMA((n,)), pltpu.SemaphoreType.DMA((n,))],
    )(*arrs)


class _GradReducer:
    def __init__(self, chip, core):
        self.chip, self.core = chip, core
        self.pending = None
        self.sums = None
        self.acc = {n: None for n in BIG}

    def push(self, layer, gb):
        assert self.pending is None
        self.pending = (layer, list(gb), list(gb.values()))

    def swap_carry(self):
        if self.pending is None:
            return None
        _, _, gs = self.pending
        outs = [jax.ShapeDtypeStruct((g.shape[0], g.shape[1] // 2, g.shape[2]), F32) for g in gs]

        def make(ins, out, send_sems, recv_sems):
            x, y, c, _ = _place()
            cps = []
            for p, ref in enumerate(ins):
                half = ref.shape[1] // 2
                cps.append(_remote(ref.at[:, pl.ds((1 - c) * half, half), :], out[p], send_sems.at[p], recv_sems.at[p],
                                   (x, y, 1 - c)))
            return cps

        return _Carry(gs, outs, len(gs), make)

    def after_swap(self, got):
        layer, names, gs = self.pending
        self.pending = None
        self.sums = (layer, names, [_rs_add(g, a, self.core, f"rs_add_{n}_{layer}") for n, g, a in zip(names, gs, got)])

    def ici_carry(self):
        if self.sums is None:
            return None
        _, _, ps = self.sums
        outs = [jax.ShapeDtypeStruct((3,) + p.shape[1:], BF16) for p in ps]

        def make(ins, out, send_sems, recv_sems):
            x, y, c, chips = _place()
            return [_remote(ins[p].at[2 * cx + cy], out[p].at[j], send_sems.at[3 * p + j], recv_sems.at[3 * p + j],
                            (cx, cy, c)) for p in range(len(ins)) for j, (cx, cy) in enumerate(chips)]

        return _Carry(ps, outs, 3 * len(ps), make)

    def after_ici(self, got):
        layer, names, ps = self.sums
        self.sums = None
        for n, p, a in zip(names, ps, got):
            self.acc[n] = _rs_sum(p, a, self.acc[n], layer, self.chip, self.core, f"rs_sum_{n}_{layer}")

    def finish(self):
        if self.pending is not None:
            self.after_swap(_run_carry(self.swap_carry(), "rs_swap_last"))
        if self.sums is not None:
            self.after_ici(_run_carry(self.ici_carry(), "rs_ici_last"))
        return dict(zip(BIG, _rs_share([self.acc[n] for n in BIG])))


def _adamw(w, g, m, v, name):
    shape = w.shape
    if len(shape) == 2:
        shape3 = (1,) + shape
    else:
        shape3 = (-1,) + shape[-2:]
    w2, g2, m2, v2 = (a.reshape(shape3) for a in (w, g, m, v))
    lead, rows, last = w2.shape
    tr = max([t for t in range(8, rows + 1, 8) if rows % t == 0 and t * last * 4 <= 2 * 1024 * 1024] or [rows])

    def body(w_ref, g_ref, m_ref, v_ref, d_ref, nm_ref, nv_ref):
        gv = g_ref[...]
        mn = ADAM_B1 * m_ref[...] + (1.0 - ADAM_B1) * gv
        vn = ADAM_B2 * v_ref[...] + (1.0 - ADAM_B2) * (gv * gv)
        m_hat = mn / (1.0 - ADAM_B1 ** ADAM_STEP)
        v_hat = vn / (1.0 - ADAM_B2 ** ADAM_STEP)
        d_ref[...] = -ADAM_LR * (m_hat / (jnp.sqrt(v_hat) + ADAM_EPS) + ADAM_WD * w_ref[...])
        nm_ref[...] = mn
        nv_ref[...] = vn

    blk = pl.BlockSpec((None, tr, last), lambda a, i: (a, i, 0))
    outs = pl.pallas_call(
        body, name=name, grid=(lead, rows // tr), in_specs=[blk] * 4, out_specs=[blk] * 3,
        out_shape=[jax.ShapeDtypeStruct((lead, rows, last), F32)] * 3,
        compiler_params=_cparams(("parallel", "parallel")),
    )(w2, g2, m2, v2)
    return tuple(o.reshape(shape) for o in outs)


SHARD_AXIS = {"conv_w": 2, "meta_tokens": 1}
SMALL_SHARDED = ("conv_w", "meta_tokens")
REPLICATED = ("mix_norm_w", "b_gate", "pool_scale", "conv_b", "dt_bias", "a_log", "d_skip", "ssd_norm_w", "mlp_norm_w",
              "final_norm_w")


def _flatten(arrs, dtype, row_mult):
    flat = jnp.concatenate([a.astype(dtype).reshape(-1) for a in arrs])
    n = flat.shape[0]
    rows = -(-n // (LANES * row_mult)) * row_mult
    return jnp.pad(flat, (0, rows * LANES - n)).reshape(rows, LANES)


def _unflatten(flat2d, shapes):
    flat = flat2d.reshape(-1)
    out, off = [], 0
    for sh in shapes:
        n = 1
        for d in sh:
            n *= d
        out.append(flat[off:off + n].reshape(sh))
        off += n
    return out


def kernel(x, meta_tokens, mix_norm_w, w_in, b_gate, pool_w_group, pool_scale, w_pool_up, conv_w, conv_b, dt_bias, a_log, d_skip, ssd_norm_w, w_ssd_out, w_o, mlp_norm_w, w_ff1, w_ff2, final_norm_w, loss_target, m_meta_tokens, m_mix_norm_w, m_w_in, m_b_gate, m_pool_w_group, m_pool_scale, m_w_pool_up, m_conv_w, m_conv_b, m_dt_bias, m_a_log, m_d_skip, m_ssd_norm_w, m_w_ssd_out, m_w_o, m_mlp_norm_w, m_w_ff1, m_w_ff2, m_final_norm_w, v_meta_tokens, v_mix_norm_w, v_w_in, v_b_gate, v_pool_w_group, v_pool_scale, v_w_pool_up, v_conv_w, v_conv_b, v_dt_bias, v_a_log, v_d_skip, v_ssd_norm_w, v_w_ssd_out, v_w_o, v_mlp_norm_w, v_w_ff1, v_w_ff2, v_final_norm_w):
    names = ("meta_tokens",) + PER_LAYER + ("final_norm_w",)
    par = dict(meta_tokens=meta_tokens, mix_norm_w=mix_norm_w, w_in=w_in, b_gate=b_gate, pool_w_group=pool_w_group,
               pool_scale=pool_scale, w_pool_up=w_pool_up, conv_w=conv_w, conv_b=conv_b, dt_bias=dt_bias, a_log=a_log,
               d_skip=d_skip, ssd_norm_w=ssd_norm_w, w_ssd_out=w_ssd_out, w_o=w_o, mlp_norm_w=mlp_norm_w, w_ff1=w_ff1,
               w_ff2=w_ff2, final_norm_w=final_norm_w)
    mom = dict(meta_tokens=m_meta_tokens, mix_norm_w=m_mix_norm_w, w_in=m_w_in, b_gate=m_b_gate, pool_w_group=m_pool_w_group,
               pool_scale=m_pool_scale, w_pool_up=m_w_pool_up, conv_w=m_conv_w, conv_b=m_conv_b, dt_bias=m_dt_bias,
               a_log=m_a_log, d_skip=m_d_skip, ssd_norm_w=m_ssd_norm_w, w_ssd_out=m_w_ssd_out, w_o=m_w_o,
               mlp_norm_w=m_mlp_norm_w, w_ff1=m_w_ff1, w_ff2=m_w_ff2, final_norm_w=m_final_norm_w)
    var = dict(meta_tokens=v_meta_tokens, mix_norm_w=v_mix_norm_w, w_in=v_w_in, b_gate=v_b_gate, pool_w_group=v_pool_w_group,
               pool_scale=v_pool_scale, w_pool_up=v_w_pool_up, conv_w=v_conv_w, conv_b=v_conv_b, dt_bias=v_dt_bias,
               a_log=v_a_log, d_skip=v_d_skip, ssd_norm_w=v_ssd_norm_w, w_ssd_out=v_w_ssd_out, w_o=v_w_o,
               mlp_norm_w=v_mlp_norm_w, w_ff1=v_w_ff1, w_ff2=v_w_ff2, final_norm_w=v_final_norm_w)
    chip = 2 * lax.axis_index("x") + lax.axis_index("y")
    core = lax.axis_index("c")

    weights = _WeightGatherer({n: par[n].astype(BF16).reshape(DEPTH, -1, par[n].shape[-1]) for n in BIG})
    small = {n: par[n] for n in REPLICATED}
    small_shapes = [par[n].shape for n in SMALL_SHARDED]
    got_small = _exchange_small(_flatten([par[n] for n in SMALL_SHARDED], F32, 8), False, "gather_small")
    pieces = [_unflatten(got_small[2 * k], small_shapes) for k in range(N_CHIPS)]
    for j, n in enumerate(SMALL_SHARDED):
        small[n] = jnp.concatenate([pieces[k][j] for k in range(N_CHIPS)], axis=SHARD_AXIS[n])

    reducer = _GradReducer(chip, core)
    loss, dx, _, grads = _local_step(x[0], loss_target[0], weights, small, reducer)
    loss = lax.psum(loss, ("x", "y", "c"))

    small_names = REPLICATED + SMALL_SHARDED
    red = _exchange_small(_flatten([grads[n] for n in small_names], F32, 8), True, "allreduce_small")
    gsum = dict(zip(small_names, _unflatten(red, [grads[n].shape for n in small_names])))
    for n in SMALL_SHARDED:
        width = par[n].shape[SHARD_AXIS[n]]
        gsum[n] = lax.dynamic_slice_in_dim(gsum[n], chip * width, width, axis=SHARD_AXIS[n])

    gsum.update({n: g.reshape(par[n].shape) for n, g in reducer.finish().items()})

    delta, new_m, new_v = {}, {}, {}
    for n in BIG:
        if n == "w_in":
            tr = lambda a: jnp.swapaxes(a, 1, 2)
            outs = _adamw(tr(par[n]), tr(gsum[n]), tr(mom[n]), tr(var[n]), "adamw_" + n)
            delta[n], new_m[n], new_v[n] = (tr(o) for o in outs)
        else:
            delta[n], new_m[n], new_v[n] = _adamw(par[n], gsum[n], mom[n], var[n], "adamw_" + n)
    rest = [n for n in names if n not in BIG]
    shapes = [par[n].shape for n in rest]
    packed = [_flatten([d[n] for n in rest], F32, 8) for d in (par, gsum, mom, var)]
    for d, flat in zip((delta, new_m, new_v), _adamw(*packed, "adamw_small")):
        d.update(zip(rest, _unflatten(flat, shapes)))

    order = ("meta_tokens", "mix_norm_w", "w_in", "b_gate", "pool_w_group", "pool_scale", "w_pool_up", "conv_w", "conv_b",
             "dt_bias", "a_log", "d_skip", "ssd_norm_w", "w_ssd_out", "w_o", "mlp_norm_w", "w_ff1", "w_ff2", "final_norm_w")
    return (loss, dx[None], *[gsum[n] for n in order], *[delta[n] for n in order], *[new_m[n] for n in order],
            *[new_v[n] for n in order])
```

```python
import functools

import jax
import jax.numpy as jnp
from jax import lax
from jax.experimental import pallas as pl
from jax.experimental.pallas import tpu as pltpu

F32 = jnp.float32
BF16 = jnp.bfloat16

D_MODEL = 1024
DEPTH = 4
N_META = 16
N_PAD = 112
ROW_X = N_PAD + N_META
POOL_WINDOWS = (2, 4, 8, 16)
POOL_GDIM = 256
D_INNER = 2048
N_HEADS = 32
HEAD_DIM = 64
N_GROUPS = 8
HEADS_PER_GROUP = 4
GROUP_W = HEADS_PER_GROUP * HEAD_DIM
D_STATE = 128
CHUNK = 128
D_XBC = 4096
D_FF = 4096
EPS = 1e-5
OFF_Z, OFF_XBC, OFF_DT, OFF_GATE, IN_COLS = 1024, 3072, 7168, 7200, 9248
PZ, PGATE, PPOOL, PXBC, PCOLS = 0, 2048, 4096, 5120, 9216
DT_PAD = 128

ADAM_LR, ADAM_B1, ADAM_B2, ADAM_EPS, ADAM_WD, ADAM_STEP = 0.001, 0.9, 0.999, 1e-08, 0.01, 10

VMEM_LIMIT = 56 * 1024 * 1024

_NN = (((1,), (0,)), ((), ()))
_NT = (((1,), (1,)), ((), ()))
_TN = (((0,), (0,)), ((), ()))


def _dot(a, b, dn=_NN):
    return lax.dot_general(a, b, dn, preferred_element_type=F32)


def _cparams(sem):
    return pltpu.CompilerParams(dimension_semantics=sem, vmem_limit_bytes=VMEM_LIMIT)


def _tile(n, cands):
    for c in cands:
        if n % c == 0:
            return c
    raise ValueError(f"no tile for {n} in {cands}")


def _sigmoid(x):
    return 0.5 * jnp.tanh(0.5 * x) + 0.5


def _softplus(x):
    return jnp.maximum(x, 0.0) + jnp.log(1.0 + jnp.exp(-jnp.abs(x)))


def _bs(shape, fn):
    return pl.BlockSpec(shape, fn)


class _Carry:
    def __init__(self, ins, outs, n_sems, make, alias=None):
        self.ins, self.outs, self.n_sems, self.make = list(ins), list(outs), n_sems, make
        self.alias = dict(alias or {})

    def aliases(self, first_in, first_out):
        return {first_in + i: first_out + o for i, o in self.alias.items()}

    def specs(self):
        hbm = pl.BlockSpec(memory_space=pl.ANY)
        return [hbm] * len(self.ins), [hbm] * len(self.outs), [pltpu.SemaphoreType.DMA((self.n_sems,))] * 2

    def at_edges(self, grid, in_refs, out_refs, sems):
        ids = [pl.program_id(a) for a in range(len(grid))]
        first = functools.reduce(jnp.logical_and, [i == 0 for i in ids])
        last = functools.reduce(jnp.logical_and, [i == n - 1 for i, n in zip(ids, grid)])

        def begin():
            @pl.when(first)
            def _():
                for cp in self.make(in_refs, out_refs, *sems):
                    cp.start()

        def end():
            @pl.when(last)
            def _():
                for cp in self.make(in_refs, out_refs, *sems):
                    cp.wait()

        return begin, end


def _mm(name, dn, grid, a, a_spec, b, b_spec, outs, *, extras=(), epilogue=None, carry=None):
    nk = grid[2]
    ne, no = len(extras), len(outs)
    blk = tuple(d for d in outs[0][2].block_shape if d is not None)
    c_in, c_out, c_sem = carry.specs() if carry is not None else ([], [], [])
    nci, nco = len(c_in), len(c_out)

    def body(a_ref, b_ref, *rest):
        e_refs, o_refs = rest[:ne], rest[ne + nci:ne + nci + no]
        if carry is not None:
            begin, end = carry.at_edges(grid, rest[ne:ne + nci], rest[ne + nci + no:ne + nci + no + nco], rest[-2:])
            begin()
        p = _dot(a_ref[...].astype(BF16), b_ref[...].astype(BF16), dn)

        def finish(acc):
            outs = epilogue(acc, *[e[...] for e in e_refs]) if epilogue is not None else (acc,)
            for o, v in zip(o_refs, outs):
                o[...] = v.astype(o.dtype)

        if nk == 1:
            finish(p)
        else:
            acc_ref = rest[ne + nci + no + nco]
            kk = pl.program_id(2)

            @pl.when(kk == 0)
            def _():
                acc_ref[...] = p

            @pl.when(kk > 0)
            def _():
                acc_ref[...] += p

            @pl.when(kk == nk - 1)
            def _():
                finish(acc_ref[...])
        if carry is not None:
            end()

    res = pl.pallas_call(
        body,
        name=name,
        grid=grid,
        in_specs=[a_spec, b_spec, *[s for _, s in extras]] + c_in,
        out_specs=[o[2] for o in outs] + c_out,
        out_shape=[jax.ShapeDtypeStruct(o[0], o[1]) for o in outs] + (carry.outs if carry is not None else []),
        scratch_shapes=([pltpu.VMEM(blk, F32)] if nk > 1 else []) + c_sem,
        input_output_aliases=carry.aliases(2 + ne, no) if carry is not None else {},
        compiler_params=_cparams(("arbitrary",) * 3 if carry is not None else ("parallel", "parallel", "arbitrary")),
    )(a, b, *[e for e, _ in extras], *(carry.ins if carry is not None else []))
    if carry is not None:
        return res[:no], res[no:]
    return res[0] if no == 1 else res


def _norm_fwd(h, w, name):
    t, d = h.shape
    tm = _tile(t, (1056, 384, 128))

    def body(h_ref, w_ref, u_ref):
        x = h_ref[...]
        r = lax.rsqrt(jnp.mean(x * x, axis=-1, keepdims=True) + EPS)
        u_ref[...] = (x * r * w_ref[...]).astype(BF16)

    return pl.pallas_call(
        body, name=name, grid=(t // tm,),
        in_specs=[pl.BlockSpec((tm, d), lambda i: (i, 0)), pl.BlockSpec((1, d), lambda i: (0, 0))],
        out_specs=pl.BlockSpec((tm, d), lambda i: (i, 0)),
        out_shape=jax.ShapeDtypeStruct((t, d), BF16),
        compiler_params=_cparams(("parallel",)),
    )(h, w.reshape(1, d))


def _norm_bwd(dres, du, h, w, name):
    t, d = h.shape
    tm = _tile(t, (528, 384, 128))

    def body(dres_ref, du_ref, h_ref, w_ref, dh_ref, dw_ref):
        x = h_ref[...]
        r = lax.rsqrt(jnp.mean(x * x, axis=-1, keepdims=True) + EPS)
        xhat = x * r
        du_v = du_ref[...]
        g = du_v * w_ref[...]
        dh_ref[...] = dres_ref[...] + r * (g - xhat * jnp.mean(g * xhat, axis=-1, keepdims=True))

        @pl.when(pl.program_id(0) == 0)
        def _():
            dw_ref[...] = jnp.zeros_like(dw_ref)

        dw_ref[...] += jnp.sum(du_v * xhat, axis=0, keepdims=True)

    row = pl.BlockSpec((tm, d), lambda i: (i, 0))
    vec = pl.BlockSpec((1, d), lambda i: (0, 0))
    dh, dw = pl.pallas_call(
        body, name=name, grid=(t // tm,),
        in_specs=[row, row, row, vec], out_specs=[row, vec],
        out_shape=[jax.ShapeDtypeStruct((t, d), F32), jax.ShapeDtypeStruct((1, d), F32)],
        compiler_params=_cparams(("arbitrary",)),
    )(dres, du, h, w.reshape(1, d))
    return dh, dw.reshape(d)


def _loss_head(h, w, tgt):
    t, d = h.shape
    tm = CHUNK
    nb = ROW_X // tm

    def body(h_ref, w_ref, t_ref, loss_ref, dh_ref, dw_ref):
        i = pl.program_id(0)
        x = h_ref[...]
        r = lax.rsqrt(jnp.mean(x * x, axis=-1, keepdims=True) + EPS)
        xhat = x * r
        wv = w_ref[...]
        live = i >= nb
        err = jnp.where(live, xhat * wv - t_ref[...], 0.0)
        dout = err * (1.0 / d)
        g = dout * wv
        dh_ref[...] = r * (g - xhat * jnp.mean(g * xhat, axis=-1, keepdims=True))

        @pl.when(i == 0)
        def _():
            loss_ref[...] = jnp.zeros_like(loss_ref)
            dw_ref[...] = jnp.zeros_like(dw_ref)

        loss_ref[...] += 0.5 * jnp.sum(jnp.sum(err * err, axis=-1, keepdims=True), axis=0, keepdims=True) * (1.0 / d)
        dw_ref[...] += jnp.sum(dout * xhat, axis=0, keepdims=True)

    row = pl.BlockSpec((tm, d), lambda i: (i, 0))
    vec = pl.BlockSpec((1, d), lambda i: (0, 0))
    loss, dh, dw = pl.pallas_call(
        body, name="loss_head", grid=(t // tm,),
        in_specs=[row, vec, pl.BlockSpec((tm, d), lambda i: (jnp.maximum(i - nb, 0), 0))],
        out_specs=[pl.BlockSpec((1, 1), lambda i: (0, 0)), row, vec],
        out_shape=[jax.ShapeDtypeStruct((1, 1), F32), jax.ShapeDtypeStruct((t, d), F32), jax.ShapeDtypeStruct((1, d), F32)],
        compiler_params=_cparams(("arbitrary",)),
    )(h, w.reshape(1, d), tgt)
    return loss[0, 0], dh, dw.reshape(d)


def _gate_fwd(proj, b_gate, y_pool, y_ssd):
    t = proj.shape[0]
    d = D_MODEL
    tm = _tile(t, (528, 384, 128))

    def body(gp_ref, gs_ref, bp_ref, bs_ref, yp_ref, ys_ref, o_ref):
        gp = _sigmoid(gp_ref[...] + bp_ref[...])
        gs = _sigmoid(gs_ref[...] + bs_ref[...])
        o_ref[...] = (gp * yp_ref[...] + gs * ys_ref[...]).astype(BF16)

    row = pl.BlockSpec((tm, d), lambda i: (i, 0))
    return pl.pallas_call(
        body, name="gate_fwd", grid=(t // tm,),
        in_specs=[pl.BlockSpec((tm, d), lambda i: (i, PGATE // d)), pl.BlockSpec((tm, d), lambda i: (i, PGATE // d + 1)),
                  pl.BlockSpec((1, d), lambda i: (0, 0)), pl.BlockSpec((1, d), lambda i: (0, 1)), row, row],
        out_specs=row, out_shape=jax.ShapeDtypeStruct((t, d), BF16),
        compiler_params=_cparams(("parallel",)),
    )(proj, proj, b_gate.reshape(1, 2 * d), b_gate.reshape(1, 2 * d), y_pool, y_ssd)


def _gate_bwd(dmerged, proj, b_gate, y_pool, y_ssd):
    t = proj.shape[0]
    d = D_MODEL
    tm = _tile(t, (384, 128))

    def body(dm_ref, gp_ref, gs_ref, bp_ref, bs_ref, yp_ref, ys_ref, dyp_ref, dys_ref, dg_ref, db_ref):
        dm = dm_ref[...]
        gp = _sigmoid(gp_ref[...] + bp_ref[...])
        gs = _sigmoid(gs_ref[...] + bs_ref[...])
        dyp_ref[...] = (dm * gp).astype(BF16)
        dys_ref[...] = (dm * gs).astype(BF16)
        dgp = dm * yp_ref[...] * gp * (1.0 - gp)
        dgs = dm * ys_ref[...] * gs * (1.0 - gs)
        dg_ref[:, :d] = dgp.astype(BF16)
        dg_ref[:, d:] = dgs.astype(BF16)

        @pl.when(pl.program_id(0) == 0)
        def _():
            db_ref[...] = jnp.zeros_like(db_ref)

        db_ref[:, :d] += jnp.sum(dgp, axis=0, keepdims=True)
        db_ref[:, d:] += jnp.sum(dgs, axis=0, keepdims=True)

    row = pl.BlockSpec((tm, d), lambda i: (i, 0))
    dyp, dys, dg, db = pl.pallas_call(
        body, name="gate_bwd", grid=(t // tm,),
        in_specs=[row, pl.BlockSpec((tm, d), lambda i: (i, PGATE // d)), pl.BlockSpec((tm, d), lambda i: (i, PGATE // d + 1)),
                  pl.BlockSpec((1, d), lambda i: (0, 0)), pl.BlockSpec((1, d), lambda i: (0, 1)), row, row],
        out_specs=[row, row, pl.BlockSpec((tm, 2 * d), lambda i: (i, PGATE // (2 * d))),
                   pl.BlockSpec((1, 2 * d), lambda i: (0, 0))],
        out_shape=[jax.ShapeDtypeStruct((t, d), BF16), jax.ShapeDtypeStruct((t, d), BF16),
                   jax.ShapeDtypeStruct((t, PCOLS), BF16), jax.ShapeDtypeStruct((1, 2 * d), F32)],
        compiler_params=_cparams(("arbitrary",)),
    )(dmerged, proj, proj, b_gate.reshape(1, 2 * d), b_gate.reshape(1, 2 * d), y_pool, y_ssd)
    return dyp, dys, dg, db.reshape(2 * d)


POOL_HALO = 16


def _pool_counts(row0, n, win):
    pos1 = row0 + lax.broadcasted_iota(jnp.int32, (n, 1), 0) - (N_PAD - 1)
    return jnp.clip(pos1, 1, win).astype(F32)


N_CHIPS = 4
WG_ROWS = POOL_GDIM // N_CHIPS
WUP_ROWS = D_MODEL // N_CHIPS


def _group_w(wg_ref, g):
    return jnp.concatenate([wg_ref[k, g] for k in range(N_CHIPS)], axis=0)


def _pool_w_specs(layer):
    return [pl.BlockSpec((None, N_CHIPS, len(POOL_WINDOWS), WG_ROWS, POOL_GDIM), lambda i: (0, 0, 0, 0, 0)),
            pl.BlockSpec((None, N_CHIPS, WUP_ROWS, D_MODEL), lambda i: (0, 0, 0, 0))]


def _pool_fwd(proj, wg, scale, wup, layer):
    t = proj.shape[0]
    d = D_MODEL
    tm = _tile(t, (384, 128))
    hb = tm // POOL_HALO

    def body(u_ref, halo_ref, sc_ref, wg_ref, wup_ref, pooled_ref, yg_ref, ypm_ref, yp_ref):
        i = pl.program_id(0)
        x = u_ref[...]
        halo = jnp.where(i > 0, halo_ref[...], 0.0)
        xc = jnp.concatenate([halo, x], axis=0)
        for g, win in enumerate(POOL_WINDOWS):
            sl = slice(g * POOL_GDIM, (g + 1) * POOL_GDIM)
            s = xc[:, sl]
            k = 1
            while k < win:
                s = s + pltpu.roll(s, k, axis=0)
                k *= 2
            pooled = s[POOL_HALO:] / _pool_counts(i * tm, tm, win) - x[:, sl]
            pb = pooled.astype(BF16)
            pooled_ref[:, sl] = pb
            yg_ref[:, sl] = _dot(pb, _group_w(wg_ref, g))
        ypm = (yg_ref[...] * sc_ref[...]).astype(BF16)
        ypm_ref[...] = ypm
        acc = _dot(ypm[:, :WUP_ROWS], wup_ref[0])
        for k in range(1, N_CHIPS):
            acc = acc + _dot(ypm[:, k * WUP_ROWS:(k + 1) * WUP_ROWS], wup_ref[k])
        yp_ref[...] = acc

    row = pl.BlockSpec((tm, d), lambda i: (i, 0))
    return pl.pallas_call(
        body, name=f"pool_fwd_{layer}", grid=(t // tm,),
        in_specs=[pl.BlockSpec((tm, d), lambda i: (i, PPOOL // d)),
                  pl.BlockSpec((POOL_HALO, d), lambda i: (jnp.maximum(i * hb - 1, 0), PPOOL // d)),
                  pl.BlockSpec((1, d), lambda i: (0, 0))] + _pool_w_specs(layer),
        out_specs=[row, row, row, row],
        out_shape=[jax.ShapeDtypeStruct((t, d), BF16), jax.ShapeDtypeStruct((t, d), F32),
                   jax.ShapeDtypeStruct((t, d), BF16), jax.ShapeDtypeStruct((t, d), F32)],
        compiler_params=_cparams(("parallel",)),
    )(proj, proj, scale.reshape(1, d), wg, wup)


def _pool_bwd_a(dy_pool, yg, scale, wg, wup, layer):
    t, d = yg.shape
    tm = _tile(t, (384, 128))

    def body(dy_ref, yg_ref, sc_ref, wg_ref, wup_ref, q_ref, dyg_ref, dsc_ref):
        dy = dy_ref[...]
        dypm = jnp.concatenate([_dot(dy, wup_ref[k], _NT) for k in range(N_CHIPS)], axis=1)

        @pl.when(pl.program_id(0) == 0)
        def _():
            dsc_ref[...] = jnp.zeros_like(dsc_ref)

        dsc_ref[...] += jnp.sum(dypm * yg_ref[...], axis=0, keepdims=True)
        dyg = (dypm * sc_ref[...]).astype(BF16)
        dyg_ref[...] = dyg
        for g in range(len(POOL_WINDOWS)):
            sl = slice(g * POOL_GDIM, (g + 1) * POOL_GDIM)
            q_ref[:, sl] = _dot(dyg[:, sl], _group_w(wg_ref, g), _NT)

    row = pl.BlockSpec((tm, d), lambda i: (i, 0))
    vec = pl.BlockSpec((1, d), lambda i: (0, 0))
    q, dyg, dsc = pl.pallas_call(
        body, name=f"pool_bwd_a_{layer}", grid=(t // tm,),
        in_specs=[row, row, vec] + _pool_w_specs(layer),
        out_specs=[row, row, vec],
        out_shape=[jax.ShapeDtypeStruct((t, d), F32), jax.ShapeDtypeStruct((t, d), BF16), jax.ShapeDtypeStruct((1, d), F32)],
        compiler_params=_cparams(("arbitrary",)),
    )(dy_pool, yg, scale.reshape(1, d), wg, wup)
    return q, dyg, dsc.reshape(d)


def _pool_bwd_b(q, dproj):
    t, d = q.shape
    tm = _tile(t, (384, 128))
    hb = tm // POOL_HALO
    nt = t // tm
    n = tm + POOL_HALO

    def body(q_ref, halo_ref, _, o_ref):
        i = pl.program_id(0)
        qv = q_ref[...]
        halo = jnp.where(i < nt - 1, halo_ref[...], 0.0)
        qc = jnp.concatenate([qv, halo], axis=0)
        for g, win in enumerate(POOL_WINDOWS):
            sl = slice(g * POOL_GDIM, (g + 1) * POOL_GDIM)
            s = qc[:, sl] / _pool_counts(i * tm, n, win)
            k = 1
            while k < win:
                s = s + pltpu.roll(s, n - k, axis=0)
                k *= 2
            o_ref[:, sl] = (s[:tm] - qv[:, sl]).astype(BF16)

    row = pl.BlockSpec((tm, d), lambda i: (i, 0))
    return pl.pallas_call(
        body, name="pool_bwd_b", grid=(nt,),
        in_specs=[row, pl.BlockSpec((POOL_HALO, d), lambda i: (jnp.minimum((i + 1) * hb, t // POOL_HALO - 1), 0)),
                  pl.BlockSpec(memory_space=pl.ANY)],
        out_specs=pl.BlockSpec((tm, d), lambda i: (i, PPOOL // d)), out_shape=jax.ShapeDtypeStruct(dproj.shape, BF16),
        input_output_aliases={2: 0},
        compiler_params=_cparams(("parallel",)),
    )(q, q, dproj)


def _pool_dwg(pooled, dyg, layer):
    t, d = pooled.shape
    tk = _tile(t, (1056, 384, 128))
    nk = t // tk
    gd = POOL_GDIM
    ng = d // gd

    def body(p_ref, g_ref, o_ref):
        @pl.when(pl.program_id(1) == 0)
        def _():
            o_ref[...] = jnp.zeros_like(o_ref)

        part = _dot(p_ref[...], g_ref[...], _TN)
        for k in range(N_CHIPS):
            o_ref[k] += part[k * WG_ROWS:(k + 1) * WG_ROWS]

    blk = pl.BlockSpec((tk, gd), lambda g, k: (k, g))
    return pl.pallas_call(
        body, name=f"pool_dwg_{layer}", grid=(ng, nk), in_specs=[blk, blk],
        out_specs=pl.BlockSpec((N_CHIPS, None, WG_ROWS, gd), lambda g, k: (0, g, 0, 0)),
        out_shape=jax.ShapeDtypeStruct((N_CHIPS, ng, WG_ROWS, gd), F32),
        compiler_params=_cparams(("parallel", "arbitrary")),
    )(pooled, dyg)


CONV_W = 4
CONV_HALO = 8
XBC_BLK = PXBC // 1024


def _conv_fwd(proj, conv_w, conv_b):
    t = proj.shape[0]
    cw = 1024
    tm = _tile(t, (1056, 384, 128))
    hb = tm // CONV_HALO

    def body(x_ref, halo_ref, w_ref, b_ref, o_ref, pre_ref):
        i = pl.program_id(1)
        x = x_ref[...]
        halo = jnp.where(i > 0, halo_ref[...], 0.0)
        xc = jnp.concatenate([halo, x], axis=0)
        w = w_ref[...]
        acc = b_ref[...] + x * w[CONV_W - 1:CONV_W, :]
        for k in range(CONV_W - 1):
            acc = acc + pltpu.roll(xc, CONV_W - 1 - k, axis=0)[CONV_HALO:] * w[k:k + 1, :]
        row = i * tm + lax.broadcasted_iota(jnp.int32, (tm, 1), 0)
        pre_ref[...] = acc
        o_ref[...] = jnp.where(row >= N_PAD, acc * _sigmoid(acc), 0.0)

    blk = pl.BlockSpec((tm, cw), lambda j, i: (i, j))
    return pl.pallas_call(
        body, name="conv_fwd", grid=(D_XBC // cw, t // tm),
        in_specs=[pl.BlockSpec((tm, cw), lambda j, i: (i, XBC_BLK + j)),
                  pl.BlockSpec((CONV_HALO, cw), lambda j, i: (jnp.maximum(i * hb - 1, 0), XBC_BLK + j)),
                  pl.BlockSpec((CONV_W, cw), lambda j, i: (0, j)), pl.BlockSpec((1, cw), lambda j, i: (0, j))],
        out_specs=[blk, blk],
        out_shape=[jax.ShapeDtypeStruct((t, D_XBC), F32), jax.ShapeDtypeStruct((t, D_XBC), F32)],
        compiler_params=_cparams(("parallel", "parallel")),
    )(proj, proj, conv_w, conv_b.reshape(1, D_XBC))


def _conv_bwd(dxa, coff, proj, pre, conv_w, dproj):
    t, ncols = dxa.shape
    cw = 1024
    tm = _tile(t, (528, 384, 128))
    hb = tm // CONV_HALO
    nt = t // tm
    n = tm + CONV_HALO

    def body(d_ref, dn_ref, p_ref, pn_ref, x_ref, w_ref, _, o_ref, dw_ref, db_ref):
        i = pl.program_id(1)
        last = i == nt - 1
        xc = jnp.concatenate([p_ref[...], pn_ref[...]], axis=0)
        df = jnp.concatenate([d_ref[...], dn_ref[...]], axis=0)
        w = w_ref[...]
        sig = _sigmoid(xc)
        row = i * tm + lax.broadcasted_iota(jnp.int32, (n, 1), 0)
        live = (row >= N_PAD) & ((row < (i + 1) * tm) | jnp.logical_not(last))
        dxc = jnp.where(live, df * (sig * (1.0 + xc * (1.0 - sig))), 0.0)
        ahead = [pltpu.roll(dxc, n - (CONV_W - 1 - k), axis=0)[:tm] for k in range(CONV_W - 1)] + [dxc[:tm]]
        acc = ahead[0] * w[0:1, :]
        for k in range(1, CONV_W):
            acc = acc + ahead[k] * w[k:k + 1, :]
        o_ref[...] = acc.astype(BF16)

        @pl.when(i == 0)
        def _():
            dw_ref[...] = jnp.zeros_like(dw_ref)
            db_ref[...] = jnp.zeros_like(db_ref)

        x = x_ref[...]
        db_ref[...] += jnp.sum(ahead[CONV_W - 1], axis=0, keepdims=True)
        for k in range(CONV_W):
            dw_ref[k:k + 1, :] += jnp.sum(ahead[k] * x, axis=0, keepdims=True)

    def pspec(rows, fn):
        return pl.BlockSpec((rows, cw), lambda j, i: (fn(i), coff + j))

    nxt = lambda i: jnp.minimum((i + 1) * hb, t // CONV_HALO - 1)
    dxbc, dw, db = pl.pallas_call(
        body, name=f"conv_bwd_{coff}", grid=(ncols // cw, nt),
        in_specs=[pl.BlockSpec((tm, cw), lambda j, i: (i, j)), pl.BlockSpec((CONV_HALO, cw), lambda j, i: (nxt(i), j)),
                  pspec(tm, lambda i: i), pspec(CONV_HALO, nxt),
                  pl.BlockSpec((tm, cw), lambda j, i: (i, XBC_BLK + coff + j)),
                  pl.BlockSpec((CONV_W, cw), lambda j, i: (0, coff + j)),
                  pl.BlockSpec(memory_space=pl.ANY)],
        out_specs=[pl.BlockSpec((tm, cw), lambda j, i: (i, XBC_BLK + coff + j)), pl.BlockSpec((CONV_W, cw), lambda j, i: (0, j)),
                   pl.BlockSpec((1, cw), lambda j, i: (0, j))],
        out_shape=[jax.ShapeDtypeStruct(dproj.shape, BF16), jax.ShapeDtypeStruct((CONV_W, ncols), F32),
                   jax.ShapeDtypeStruct((1, ncols), F32)],
        input_output_aliases={6: 0},
        compiler_params=_cparams(("parallel", "arbitrary")),
    )(dxa, dxa, pre, pre, proj, conv_w, dproj)
    return dxbc, dw, db.reshape(ncols)


def _cumsum(x, axis, reverse=False):
    n = x.shape[axis]
    idx = lax.broadcasted_iota(jnp.int32, x.shape, axis)
    k = 1
    while k < n:
        if reverse:
            x = x + jnp.where(idx < n - k, pltpu.roll(x, n - k, axis=axis), 0.0)
        else:
            x = x + jnp.where(idx >= k, pltpu.roll(x, k, axis=axis), 0.0)
        k *= 2
    return x


def _head_masks():
    lane = lax.broadcasted_iota(jnp.int32, (1, GROUP_W), 1)
    return [(lane >= r * HEAD_DIM) & (lane < (r + 1) * HEAD_DIM) for r in range(HEADS_PER_GROUP)]


def _expand_heads(cols, hm):
    out = jnp.where(hm[0], cols[:, 0:1], 0.0)
    for r in range(1, HEADS_PER_GROUP):
        out = out + jnp.where(hm[r], cols[:, r:r + 1], 0.0)
    return out


def _ssd_decay(dt_raw, dt_bias, a_log):
    t = dt_raw.shape[0]

    def body(raw_ref, b_ref, al_ref, dt_ref, acs_ref, sig_ref):
        raw = raw_ref[...] + b_ref[...]
        rowid = pl.program_id(0) * CHUNK + lax.broadcasted_iota(jnp.int32, (CHUNK, 1), 0)
        dt = jnp.where(rowid >= N_PAD, _softplus(raw), 0.0)
        dt_ref[...] = dt
        acs_ref[...] = _cumsum(dt * -jnp.exp(al_ref[...]), 0)
        sig_ref[...] = _sigmoid(raw)

    blk = pl.BlockSpec((CHUNK, DT_PAD), lambda c: (c, 0))
    vec = pl.BlockSpec((1, DT_PAD), lambda c: (0, 0))
    pad = lambda v: jnp.pad(v, (0, DT_PAD - N_HEADS)).reshape(1, DT_PAD)
    dt, acs, sig = pl.pallas_call(
        body, name="ssd_decay", grid=(t // CHUNK,), in_specs=[blk, vec, vec], out_specs=[blk, blk, blk],
        out_shape=[jax.ShapeDtypeStruct((t, DT_PAD), F32)] * 3,
        compiler_params=_cparams(("parallel",)),
    )(dt_raw, pad(dt_bias), pad(a_log))
    grp = lambda v: v[:, :N_HEADS].reshape(t, N_GROUPS, HEADS_PER_GROUP).transpose(1, 0, 2)
    acs_g = grp(acs)
    return dict(dt=grp(dt), acs=acs_g, acs_rows=acs_g.transpose(0, 2, 1), sig=grp(sig),
                a_log=a_log.reshape(N_GROUPS, 1, HEADS_PER_GROUP))


def _ssd_q(dtc, acs4, acs_r):
    hm = _head_masks()
    dt_exp = _expand_heads(dtc, hm)
    acs = _expand_heads(acs4, hm)
    atot = acs[CHUNK - 1:CHUNK, :]
    return dict(dtc=dtc, hm=hm, dt_exp=dt_exp, acs=acs, acs_r=acs_r,
                ea=jnp.exp(acs), ds=jnp.exp(atot - acs), dec=jnp.exp(atot))


def _stack4(x):
    return jnp.concatenate([x] * HEADS_PER_GROUP, axis=0)


def _ssd_decay_stacks(q):
    hpg = HEADS_PER_GROUP
    a_col = jnp.concatenate([q["acs"][:, r * HEAD_DIM:r * HEAD_DIM + 1] for r in range(hpg)], axis=0)
    a_row = jnp.concatenate([jnp.broadcast_to(q["acs_r"][r:r + 1, :], (CHUNK, CHUNK)) for r in range(hpg)], axis=0)
    ri = lax.broadcasted_iota(jnp.int32, (hpg * CHUNK, CHUNK), 0) % CHUNK
    ci = lax.broadcasted_iota(jnp.int32, (hpg * CHUNK, CHUNK), 1)
    diff = a_col - a_row
    lm = jnp.exp(jnp.where(ri >= ci, diff, -jnp.inf))
    lt = jnp.exp(jnp.where(ri <= ci, -diff, -jnp.inf))
    return lm, lt


def _pick_heads(stacked, hm):
    out = jnp.where(hm[0], stacked[:CHUNK], 0.0)
    for r in range(1, HEADS_PER_GROUP):
        out = out + jnp.where(hm[r], stacked[r * CHUNK:(r + 1) * CHUNK], 0.0)
    return out


def _mask_heads(x, hm):
    return jnp.concatenate([jnp.where(hm[r], x, 0.0) for r in range(HEADS_PER_GROUP)], axis=0)


SSD_GP = 4


def _ssd_specs(cidx):
    gp, hpg = SSD_GP, HEADS_PER_GROUP
    return [
        pl.BlockSpec((CHUNK, gp * GROUP_W), lambda g, c: (cidx(c), g)),
        pl.BlockSpec((CHUNK, gp * D_STATE), lambda g, c: (cidx(c), D_INNER // (gp * D_STATE) + g)),
        pl.BlockSpec((CHUNK, gp * D_STATE), lambda g, c: (cidx(c), (D_INNER + 1024) // (gp * D_STATE) + g)),
        pl.BlockSpec((gp, CHUNK, hpg), lambda g, c: (g, cidx(c), 0)),
        pl.BlockSpec((gp, CHUNK, hpg), lambda g, c: (g, cidx(c), 0)),
        pl.BlockSpec((gp, hpg, CHUNK), lambda g, c: (g, 0, cidx(c))),
    ]


def _ssd_fwd(xa, dec, carry=None):
    t = xa.shape[0]
    nc = t // CHUNK
    gp, gw, ds = SSD_GP, GROUP_W, D_STATE
    grid = (N_GROUPS // gp, nc)
    c_in, c_out, c_sem = carry.specs() if carry is not None else ([], [], [])
    nci, nco = len(c_in), len(c_out)

    def body(*refs):
        xs_ref, b_ref, c_ref, dt_ref, acs_ref, acsr_ref = refs[:6]
        y_ref, prev_ref = refs[6 + nci:8 + nci]
        st_ref = refs[8 + nci + nco]
        if carry is not None:
            begin, end = carry.at_edges(grid, refs[6:6 + nci], refs[8 + nci:8 + nci + nco], refs[-2:])
            begin()

        @pl.when(pl.program_id(1) == 0)
        def _():
            st_ref[...] = jnp.zeros_like(st_ref)

        for k in range(gp):
            q = _ssd_q(dt_ref[k], acs_ref[k], acsr_ref[k])
            xdt = xs_ref[:, k * gw:(k + 1) * gw] * q["dt_exp"]
            bm = b_ref[:, k * ds:(k + 1) * ds].astype(BF16)
            cm = c_ref[:, k * ds:(k + 1) * ds].astype(BF16)
            cb = _dot(cm, bm, _NT)
            st = st_ref[k]
            prev_ref[0, k] = st
            lm, _ = _ssd_decay_stacks(q)
            y_diag = _pick_heads(_dot((_stack4(cb) * lm).astype(BF16), xdt.astype(BF16)), q["hm"])
            y_ref[:, k * gw:(k + 1) * gw] = y_diag + _dot(cm, st.astype(BF16)) * q["ea"]
            st_ref[k] = q["dec"] * st + _dot(bm, (xdt * q["ds"]).astype(BF16), _TN)
        if carry is not None:
            end()

    res = pl.pallas_call(
        body, name="ssd_fwd", grid=grid,
        in_specs=_ssd_specs(lambda c: c) + c_in,
        out_specs=[pl.BlockSpec((CHUNK, gp * gw), lambda g, c: (c, g)),
                   pl.BlockSpec((1, gp, ds, gw), lambda g, c: (c, g, 0, 0))] + c_out,
        out_shape=[jax.ShapeDtypeStruct((t, D_INNER), F32), jax.ShapeDtypeStruct((nc, N_GROUPS, ds, gw), F32)]
        + (carry.outs if carry is not None else []),
        scratch_shapes=[pltpu.VMEM((gp, ds, gw), F32)] + c_sem,
        input_output_aliases=carry.aliases(6, 2) if carry is not None else {},
        compiler_params=_cparams(("arbitrary", "arbitrary") if carry is not None else ("parallel", "arbitrary")),
    )(xa, xa, xa, dec["dt"], dec["acs"], dec["acs_rows"], *(carry.ins if carry is not None else []))
    return res[0], res[1], res[2:]


def _ssd_bwd(dy, dsk, xa, prev, dec, carry=None):
    t = xa.shape[0]
    nc = t // CHUNK
    gp, gw, dstate = SSD_GP, GROUP_W, D_STATE
    hpg = HEADS_PER_GROUP

    grid = (N_GROUPS // gp, nc)
    c_in, c_out, c_sem = carry.specs() if carry is not None else ([], [], [])
    nci, nco = len(c_in), len(c_out)

    def body(*refs):
        ins, c_ins = refs[:11], refs[11:11 + nci]
        outs = refs[11 + nci:17 + nci]
        c_outs = refs[17 + nci:17 + nci + nco]
        dst_ref = refs[17 + nci + nco]
        dbias_ref, dalog_ref = outs[4], outs[5]
        if carry is not None:
            begin, end = carry.at_edges(grid, c_ins, c_outs, refs[-2:])
            begin()

        @pl.when(pl.program_id(1) == 0)
        def _():
            dst_ref[...] = jnp.zeros_like(dst_ref)
            dbias_ref[...] = jnp.zeros_like(dbias_ref)
            dalog_ref[...] = jnp.zeros_like(dalog_ref)

        for k in range(gp):
            one_group(k, *ins, *outs, dst_ref)
        if carry is not None:
            end()

    def one_group(k, xs_ref, b_ref, c_ref, dt_ref, acs_ref, acsr_ref, sig_ref, al_ref, dy_ref, sk_ref, prev_ref,
                  dxs_ref, db_ref, dc_ref, ddt_ref, dbias_ref, dalog_ref, dst_ref):
        wide = slice(k * gw, (k + 1) * gw)
        narrow = slice(k * dstate, (k + 1) * dstate)
        q = _ssd_q(dt_ref[k], acs_ref[k], acsr_ref[k])
        a_r = -jnp.exp(al_ref[k])
        hm, ds, dec, dt_exp = q["hm"], q["ds"], q["dec"], q["dt_exp"]
        xs = xs_ref[:, wide]
        xdt = xs * dt_exp
        xdtb = xdt.astype(BF16)
        bm = b_ref[:, narrow].astype(BF16)
        cm = c_ref[:, narrow].astype(BF16)
        cb = _dot(cm, bm, _NT)
        bc = _dot(bm, cm, _NT)
        dyv = dy_ref[:, wide]
        dye = (dyv * q["ea"]).astype(BF16)
        pst = prev_ref[0, k]
        dst = dst_ref[k]
        dstb = dst.astype(BF16)
        dx_state = ds * _dot(bm, dstb)
        onehot = [(lax.broadcasted_iota(jnp.int32, (1, hpg), 1) == r).astype(F32) for r in range(hpg)]
        lm, lt = _ssd_decay_stacks(q)
        dyb = dyv.astype(BF16)
        gl = _dot(_mask_heads(dyv, hm).astype(BF16), xdtb, _NT) * lm
        glt = _dot(_mask_heads(xdt, hm).astype(BF16), dyb, _NT) * lt
        bc4 = _stack4(bc)
        dxdt = dx_state + _pick_heads(_dot((bc4 * lt).astype(BF16), dyb), hm)
        wd = jnp.sum(gl * _stack4(cb), axis=1, keepdims=True) - jnp.sum(glt * bc4, axis=1, keepdims=True)
        dcb = gl[:CHUNK]
        dcbt = glt[:CHUNK]
        qa = wd[:CHUNK] * onehot[0]
        for r in range(1, hpg):
            dcb = dcb + gl[r * CHUNK:(r + 1) * CHUNK]
            dcbt = dcbt + glt[r * CHUNK:(r + 1) * CHUNK]
            qa = qa + wd[r * CHUNK:(r + 1) * CHUNK] * onehot[r]
        pstb = pst.astype(BF16)
        dc_ref[:, narrow] = _dot(dcb.astype(BF16), bm) + _dot(dye, pstb, _NT)
        db_ref[:, narrow] = _dot(dcbt.astype(BF16), cm) + _dot((xdt * ds).astype(BF16), dstb, _NT)
        dst_ref[k] = dec * dst + _dot(cm, dye, _TN)
        dxs_ref[:, wide] = dxdt * dt_exp + dyv * sk_ref[:, wide]

        t2 = xdt * dx_state
        t1 = dyv * (_dot(cm, pstb) * q["ea"]) - t2
        t4 = dxdt * xs
        last_row = jnp.sum(t2, axis=0, keepdims=True) + dec * jnp.sum(dst * pst, axis=0, keepdims=True)
        xd = jnp.zeros((CHUNK, hpg), F32)
        dal = jnp.zeros((1, hpg), F32)
        for r in range(hpg):
            qa = qa + jnp.sum(jnp.where(hm[r], t1, 0.0), axis=1, keepdims=True) * onehot[r]
            xd = xd + jnp.sum(jnp.where(hm[r], t4, 0.0), axis=1, keepdims=True) * onehot[r]
            dal = dal + jnp.sum(jnp.where(hm[r], last_row, 0.0), axis=1, keepdims=True) * onehot[r]
        rc = _cumsum(qa, 0, reverse=True) + dal
        rowid = (nc - 1 - pl.program_id(1)) * CHUNK + lax.broadcasted_iota(jnp.int32, (CHUNK, 1), 0)
        ddt_raw = jnp.where(rowid >= N_PAD, (rc * a_r + xd) * sig_ref[k], 0.0)
        ddt_ref[k] = ddt_raw
        dbias_ref[k] += jnp.sum(ddt_raw, axis=0, keepdims=True)
        dalog_ref[k] += jnp.sum(rc * q["dtc"], axis=0, keepdims=True) * a_r

    rev = lambda c: nc - 1 - c
    blk = pl.BlockSpec((CHUNK, gp * gw), lambda g, c: (rev(c), g))
    nblk = pl.BlockSpec((CHUNK, gp * dstate), lambda g, c: (rev(c), g))
    cols = pl.BlockSpec((gp, CHUNK, hpg), lambda g, c: (g, rev(c), 0))
    small = pl.BlockSpec((gp, 1, hpg), lambda g, c: (g, 0, 0))
    res = pl.pallas_call(
        body, name="ssd_bwd", grid=grid,
        in_specs=_ssd_specs(rev) + [cols, small, blk, pl.BlockSpec((1, gp * gw), lambda g, c: (0, g)),
                                    pl.BlockSpec((1, gp, dstate, gw), lambda g, c: (rev(c), g, 0, 0))]
        + c_in,
        out_specs=[blk, nblk, nblk, cols, small, small] + c_out,
        out_shape=[jax.ShapeDtypeStruct((t, D_INNER), F32), jax.ShapeDtypeStruct((t, N_GROUPS * dstate), F32),
                   jax.ShapeDtypeStruct((t, N_GROUPS * dstate), F32), jax.ShapeDtypeStruct((N_GROUPS, t, hpg), F32),
                   jax.ShapeDtypeStruct((N_GROUPS, 1, hpg), F32), jax.ShapeDtypeStruct((N_GROUPS, 1, hpg), F32)]
        + (carry.outs if carry is not None else []),
        scratch_shapes=[pltpu.VMEM((gp, dstate, gw), F32)] + c_sem,
        compiler_params=_cparams(("arbitrary", "arbitrary") if carry is not None else ("parallel", "arbitrary")),
    )(xa, xa, xa, dec["dt"], dec["acs"], dec["acs_rows"], dec["sig"], dec["a_log"], dy, dsk.reshape(1, D_INNER), prev,
      *(carry.ins if carry is not None else []))
    dxs, db, dc, ddt, dbias, dalog = res[:6]
    ddt_raw = ddt.transpose(1, 0, 2).reshape(t, N_HEADS)
    return dxs, db, dc, ddt_raw, dbias.reshape(N_HEADS), dalog.reshape(N_HEADS), res[6:]


def _ssd_post_fwd(y, xa, proj, dsk, nw):
    t = y.shape[0]
    di = D_INNER
    tm = _tile(t, (384, 128))

    def body(y_ref, xs_ref, z_ref, dsk_ref, nw_ref, o_ref):
        z = z_ref[...]
        yz = (y_ref[...] + xs_ref[...] * dsk_ref[...]) * (z * _sigmoid(z))
        nwv = nw_ref[...]
        for g in range(N_GROUPS):
            sl = slice(g * GROUP_W, (g + 1) * GROUP_W)
            v = yz[:, sl]
            rg = lax.rsqrt(jnp.mean(v * v, axis=-1, keepdims=True) + EPS)
            o_ref[:, sl] = (v * rg * nwv[:, sl]).astype(BF16)

    row = pl.BlockSpec((tm, di), lambda i: (i, 0))
    vec = pl.BlockSpec((1, di), lambda i: (0, 0))
    return pl.pallas_call(
        body, name="ssd_post_fwd", grid=(t // tm,),
        in_specs=[row, row, row, vec, vec], out_specs=row, out_shape=jax.ShapeDtypeStruct((t, di), BF16),
        compiler_params=_cparams(("parallel",)),
    )(y, xa, proj, dsk.reshape(1, di), nw.reshape(1, di))


def _ssd_post_bwd(dyn, y, xa, proj, dsk, nw, dproj):
    t = y.shape[0]
    di = D_INNER
    tm = CHUNK

    def body(dyn_ref, y_ref, xs_ref, z_ref, dsk_ref, nw_ref, _, dy_ref, dz_ref, dnw_ref, ddsk_ref):
        @pl.when(pl.program_id(0) == 0)
        def _():
            dnw_ref[...] = jnp.zeros_like(dnw_ref)
            ddsk_ref[...] = jnp.zeros_like(ddsk_ref)

        for g in range(N_GROUPS):
            sl = slice(g * GROUP_W, (g + 1) * GROUP_W)
            z = z_ref[:, sl]
            sig = _sigmoid(z)
            s = z * sig
            xs = xs_ref[:, sl]
            dskv = dsk_ref[:, sl]
            yt = y_ref[:, sl] + xs * dskv
            yz = yt * s
            rg = lax.rsqrt(jnp.mean(yz * yz, axis=-1, keepdims=True) + EPS)
            xhat = yz * rg
            dynv = dyn_ref[:, sl]
            gg = dynv * nw_ref[:, sl]
            dnw_ref[:, sl] += jnp.sum(dynv * xhat, axis=0, keepdims=True)
            dyz = rg * (gg - xhat * jnp.mean(gg * xhat, axis=-1, keepdims=True))
            dyt = dyz * s
            dy_ref[:, sl] = dyt
            dz_ref[:, sl] = (dyz * yt * (sig * (1.0 + z * (1.0 - sig)))).astype(BF16)
            ddsk_ref[:, sl] += jnp.sum(dyt * xs, axis=0, keepdims=True)

    row = pl.BlockSpec((tm, di), lambda i: (i, 0))
    vec = pl.BlockSpec((1, di), lambda i: (0, 0))
    dy, dz, dnw, ddsk = pl.pallas_call(
        body, name="ssd_post_bwd", grid=(t // tm,),
        in_specs=[row, row, row, row, vec, vec, pl.BlockSpec(memory_space=pl.ANY)], out_specs=[row, row, vec, vec],
        out_shape=[jax.ShapeDtypeStruct((t, di), F32), jax.ShapeDtypeStruct(dproj.shape, BF16),
                   jax.ShapeDtypeStruct((1, di), F32), jax.ShapeDtypeStruct((1, di), F32)],
        input_output_aliases={6: 1},
        compiler_params=_cparams(("arbitrary",)),
    )(dyn, y, xa, proj, dsk.reshape(1, di), nw.reshape(1, di), dproj)
    return dy, dz, dnw.reshape(di), ddsk.reshape(N_HEADS, HEAD_DIM).sum(axis=1)


SHARD_COLS = IN_COLS // N_CHIPS


def _split_w_in(w_in_sh):
    nl = w_in_sh.shape[0]
    w = w_in_sh.transpose(0, 2, 1, 3).reshape(nl, D_MODEL, IN_COLS)
    main = jnp.concatenate([w[..., OFF_Z:OFF_XBC], w[..., OFF_GATE:], w[..., :OFF_Z], w[..., OFF_XBC:OFF_DT]], axis=-1)
    dt = jnp.pad(w[..., OFF_DT:OFF_GATE], ((0, 0), (0, 0), (0, DT_PAD - N_HEADS)))
    return main, dt


def _merge_dw_in(dmain, ddt):
    full = jnp.concatenate([dmain[:, PPOOL:PXBC], dmain[:, PZ:PGATE], dmain[:, PXBC:], ddt[:, :N_HEADS], dmain[:, PGATE:PPOOL]],
                           axis=1)
    return full.reshape(D_MODEL, N_CHIPS, SHARD_COLS).transpose(1, 0, 2)


def _relu2_epilogue(acc):
    hid = jnp.maximum(acc, 0.0)
    return acc, hid * hid


def _relu2_bwd_epilogue(acc, pre):
    return (acc * 2.0 * jnp.maximum(pre, 0.0),)


def _add_epilogue(acc, res):
    return (acc + res,)


def _w_spec(rows, cols, fn):
    return pl.BlockSpec((None, None, rows, cols), fn)


def _with_carry(stage, call):
    carry = stage[0]() if stage is not None else None
    res = call(carry)
    if carry is None:
        return res
    outs, got = res
    stage[1](got)
    return outs[0] if len(outs) == 1 else outs


def _layer_fwd(h, w, big, layer, stages=None):
    t = h.shape[0]
    d = D_MODEL
    tm = _tile(t, (1408, 384, 128))
    nt = t // tm
    row = _bs((tm, d), lambda i, j, k: (i, 0))
    stages = stages or {}
    s = {"h": h}
    u = _norm_fwd(h, w["mix_norm_w"], "norm_mix")
    proj = _with_carry(stages.get("a_ici"), lambda carry: _mm(
        f"mm_proj_{layer}", _NN, (nt, PCOLS // d, 1), u, row,
        big["w_in_main"], _bs((None, d, d), lambda i, j, k: (0, 0, j)),
        [((t, PCOLS), F32, _bs((tm, d), lambda i, j, k: (i, j)))], carry=carry))
    dt_raw = _mm(f"mm_dt_{layer}", _NN, (nt, 1, 1), u, row,
                 big["w_in_dt"], _bs((None, d, DT_PAD), lambda i, j, k: (0, 0, 0)),
                 [((t, DT_PAD), F32, _bs((tm, DT_PAD), lambda i, j, k: (i, 0)))])
    pooled, yg, ypm, y_pool = _pool_fwd(proj, big["pool_w_group"], w["pool_scale"], big["w_pool_up"], layer)
    xa, conv_pre = _conv_fwd(proj, w["conv_w"], w["conv_b"])
    dec = _ssd_decay(dt_raw, w["dt_bias"], w["a_log"])
    stage = stages.get("b_ici")
    carry = stage[0]() if stage is not None else None
    y, prev, got = _ssd_fwd(xa, dec, carry)
    if carry is not None:
        stage[1](got)
    yn = _ssd_post_fwd(y, xa, proj, w["d_skip_exp"], w["ssd_norm_w"])
    rs = D_INNER // N_CHIPS
    y_ssd = _with_carry(stages.get("a_d2d"), lambda carry: _mm(
        f"mm_ssd_out_{layer}", _NN, (nt, 1, N_CHIPS), yn, _bs((tm, rs), lambda i, j, k: (i, k)),
        big["w_ssd_out"], _w_spec(rs, d, lambda i, j, k: (0, k, 0, 0)), [((t, d), F32, row)], carry=carry))
    merged = _gate_fwd(proj, w["b_gate"], y_pool, y_ssd)
    ro = d // N_CHIPS
    h1 = _mm(f"mm_o_{layer}", _NN, (nt, 1, N_CHIPS), merged, _bs((tm, ro), lambda i, j, k: (i, k)),
             big["w_o"], _w_spec(ro, d, lambda i, j, k: (0, k, 0, 0)), [((t, d), F32, row)],
             extras=[(h, row)], epilogue=_add_epilogue)
    v = _norm_fwd(h1, w["mlp_norm_w"], "norm_mlp")
    tile = _bs((tm, d), lambda i, j, k: (i, j))
    pre, act = _with_carry(stages.get("b_d2d"), lambda carry: _mm(
        f"mm_ff1_{layer}", _NN, (nt, N_CHIPS, 1), v, row,
        big["w_ff1"], _w_spec(d, d, lambda i, j, k: (0, j, 0, 0)),
        [((t, D_FF), F32, tile), ((t, D_FF), BF16, tile)], epilogue=_relu2_epilogue, carry=carry))
    h2 = _mm(f"mm_ff2_{layer}", _NN, (nt, 1, N_CHIPS), act, _bs((tm, d), lambda i, j, k: (i, k)),
             big["w_ff2"], _w_spec(d, d, lambda i, j, k: (0, k, 0, 0)), [((t, d), F32, row)],
             extras=[(h1, row)], epilogue=_add_epilogue)
    s.update(u=u, proj=proj, dec=dec, conv_pre=conv_pre, pooled=pooled, yg=yg, ypm=ypm, y_pool=y_pool, xa=xa, y=y, prev=prev, yn=yn,
             y_ssd=y_ssd, merged=merged, h1=h1, v=v, pre=pre, act=act)
    return h2, s


def _dw(name, layer, a, b, shard_shape, by, tk, carry=None):
    t = a.shape[0]
    nk = t // tk
    rows, cols = shard_shape
    if by == "rows":
        grid = (N_CHIPS, 1, nk)
        a_spec = _bs((tk, rows), lambda i, j, k: (k, i))
        b_spec = _bs((tk, cols), lambda i, j, k: (k, 0))
        o_spec = _bs((None, rows, cols), lambda i, j, k: (i, 0, 0))
    else:
        grid = (1, N_CHIPS, nk)
        a_spec = _bs((tk, rows), lambda i, j, k: (k, 0))
        b_spec = _bs((tk, cols), lambda i, j, k: (k, j))
        o_spec = _bs((None, rows, cols), lambda i, j, k: (j, 0, 0))
    return _mm(f"{name}_{layer}", _TN, grid, a, a_spec, b, b_spec, [((N_CHIPS, rows, cols), F32, o_spec)], carry=carry)


EARLY = ("w_ff2", "w_ff1", "w_o", "w_ssd_out")


def _layer_bwd(dh, s, w, big, layer, red=None, last=False):
    t = dh.shape[0]
    d = D_MODEL
    tm = _tile(t, (1408, 384, 128))
    tk = _tile(t, (1408, 384, 128))
    nt = t // tm
    row = _bs((tm, d), lambda i, j, k: (i, 0))
    tile = _bs((tm, d), lambda i, j, k: (i, j))
    g, gb = {}, {}
    carry = red.swap_carry() if red is not None else None
    dpre = _mm(f"mm_dact_{layer}", _NT, (nt, N_CHIPS, 1), dh, row,
               big["w_ff2"], _w_spec(d, d, lambda i, j, k: (0, j, 0, 0)), [((t, D_FF), BF16, tile)],
               extras=[(s["pre"], tile)], epilogue=_relu2_bwd_epilogue, carry=carry)
    if carry is not None:
        (dpre,), got = dpre
        red.after_swap(got)
    gb["w_ff2"] = _dw("mm_dw_ff2", layer, s["act"], dh, (d, d), "rows", tk)
    dv = _mm(f"mm_dv_{layer}", _NT, (nt, 1, N_CHIPS), dpre, _bs((tm, d), lambda i, j, k: (i, k)),
             big["w_ff1"], _w_spec(d, d, lambda i, j, k: (0, k, 0, 0)), [((t, d), F32, row)])
    gb["w_ff1"] = _dw("mm_dw_ff1", layer, s["v"], dpre, (d, d), "cols", tk)
    dh1, g["mlp_norm_w"] = _norm_bwd(dh, dv, s["h1"], w["mlp_norm_w"], "norm_mlp_bwd")
    ro = d // N_CHIPS
    dmerged = _mm(f"mm_dmerged_{layer}", _NT, (nt, N_CHIPS, 1), dh1, row,
                  big["w_o"], _w_spec(ro, d, lambda i, j, k: (0, j, 0, 0)),
                  [((t, d), F32, _bs((tm, ro), lambda i, j, k: (i, j)))])
    gb["w_o"] = _dw("mm_dw_o", layer, s["merged"], dh1, (ro, d), "rows", tk)
    dy_pool, dy_ssd, dproj, g["b_gate"] = _gate_bwd(dmerged, s["proj"], w["b_gate"], s["y_pool"], s["y_ssd"])
    rs = D_INNER // N_CHIPS
    dyn = _mm(f"mm_dyn_{layer}", _NT, (nt, N_CHIPS, 1), dy_ssd, row,
              big["w_ssd_out"], _w_spec(rs, d, lambda i, j, k: (0, j, 0, 0)),
              [((t, D_INNER), F32, _bs((tm, rs), lambda i, j, k: (i, j)))])
    gb["w_ssd_out"] = _dw("mm_dw_ssd_out", layer, s["yn"], dy_ssd, (rs, d), "rows", tk)
    dy, dproj, g["ssd_norm_w"], g["d_skip"] = _ssd_post_bwd(dyn, s["y"], s["xa"], s["proj"], w["d_skip_exp"],
                                                            w["ssd_norm_w"], dproj)
    carry = red.ici_carry() if red is not None else None
    dxs, db, dc, ddt_raw, g["dt_bias"], g["a_log"], got = _ssd_bwd(dy, w["d_skip_exp"], s["xa"], s["prev"], s["dec"], carry)
    if carry is not None:
        red.after_ici(got)
    own = red is not None and last
    if own:
        red.push(layer, {n: gb[n] for n in EARLY})
    dproj, dcw1, dcb1 = _conv_bwd(dxs, 0, s["proj"], s["conv_pre"], w["conv_w"], dproj)
    dproj, dcw2, dcb2 = _conv_bwd(db, 2, s["proj"], s["conv_pre"], w["conv_w"], dproj)
    dproj, dcw3, dcb3 = _conv_bwd(dc, 3, s["proj"], s["conv_pre"], w["conv_w"], dproj)
    g["conv_w"] = jnp.concatenate([dcw1, dcw2, dcw3], axis=1)
    g["conv_b"] = jnp.concatenate([dcb1, dcb2, dcb3])
    q, dyg, g["pool_scale"] = _pool_bwd_a(dy_pool, s["yg"], w["pool_scale"], big["pool_w_group"], big["w_pool_up"], layer)
    gb["w_pool_up"] = _with_carry((red.swap_carry, red.after_swap) if own else None, lambda carry: _dw(
        "mm_dw_pool_up", layer, s["ypm"], dy_pool, (ro, d), "rows", tk, carry))
    gb["pool_w_group"] = _pool_dwg(s["pooled"], dyg, layer).reshape(N_CHIPS, POOL_GDIM, POOL_GDIM)
    dproj = _pool_bwd_b(q, dproj)
    ddt = jnp.pad(ddt_raw.astype(BF16), ((0, 0), (0, DT_PAD - N_HEADS)))
    nk = PCOLS // d
    du = _with_carry((red.ici_carry, red.after_ici) if own else None, lambda carry: _mm(
        f"mm_du_{layer}", _NT, (nt, 1, nk), dproj, _bs((tm, d), lambda i, j, k: (i, k)),
        big["w_in_main"], _bs((None, d, d), lambda i, j, k: (0, 0, k)), [((t, d), F32, row)], carry=carry))
    du = _mm(f"mm_du_dt_{layer}", _NT, (nt, 1, 1), ddt, _bs((tm, DT_PAD), lambda i, j, k: (i, 0)),
             big["w_in_dt"], _bs((None, d, DT_PAD), lambda i, j, k: (0, 0, 0)), [((t, d), F32, row)],
             extras=[(du, row)], epilogue=_add_epilogue)
    ntk = t // tk
    u_spec = _bs((tk, d), lambda i, j, k: (k, 0))
    dmain = _mm(f"mm_dw_in_{layer}", _TN, (1, nk, ntk), s["u"], u_spec, dproj, _bs((tk, d), lambda i, j, k: (k, j)),
                [((d, PCOLS), F32, _bs((d, d), lambda i, j, k: (0, j)))])
    ddtw = _mm(f"mm_dw_dt_{layer}", _TN, (1, 1, ntk), s["u"], u_spec, ddt, _bs((tk, DT_PAD), lambda i, j, k: (k, 0)),
               [((d, DT_PAD), F32, _bs((d, DT_PAD), lambda i, j, k: (0, 0)))])
    gb["w_in"] = _merge_dw_in(dmain, ddtw)
    dh0, g["mix_norm_w"] = _norm_bwd(dh1, du, s["h"], w["mix_norm_w"], "norm_mix_bwd")
    if red is not None:
        red.push(layer, {n: gb[n] for n in BIG if not (own and n in EARLY)})
    return dh0, g, gb


BIG = ("w_in", "pool_w_group", "w_pool_up", "w_ssd_out", "w_o", "w_ff1", "w_ff2")
PER_LAYER = ("mix_norm_w", "w_in", "b_gate", "pool_w_group", "pool_scale", "w_pool_up", "conv_w", "conv_b", "dt_bias",
             "a_log", "d_skip", "ssd_norm_w", "w_ssd_out", "w_o", "mlp_norm_w", "w_ff1", "w_ff2")


SMALL_PER_LAYER = tuple(n for n in PER_LAYER if n not in BIG)


SHARD_SHAPE = {"w_in": (1024, SHARD_COLS), "pool_w_group": (4, WG_ROWS, POOL_GDIM), "w_pool_up": (WUP_ROWS, D_MODEL),
               "w_ssd_out": (D_INNER // N_CHIPS, D_MODEL), "w_o": (D_MODEL // N_CHIPS, D_MODEL),
               "w_ff1": (D_MODEL, D_FF // N_CHIPS), "w_ff2": (D_FF // N_CHIPS, D_MODEL)}


def _layer_view(mats):
    big = {n: mats[n].reshape((1, N_CHIPS) + SHARD_SHAPE[n]) for n in BIG if n != "w_in"}
    big["w_in_main"], big["w_in_dt"] = _split_w_in(mats["w_in"])
    return big


class _HeldWeights:
    def __init__(self, stacked):
        self.stacked = stacked

    def layer(self, i):
        return _layer_view({n: self.stacked[n][i:i + 1].reshape(1, N_CHIPS, -1, self.stacked[n].shape[-1]) for n in BIG})

    def stages(self, i):
        return None


def _local_step(x, tgt, weights, small, red=None):
    seq = x.shape[0]
    h = jnp.concatenate([jnp.zeros((N_PAD, D_MODEL), F32), small["meta_tokens"], x], axis=0)
    saved, ws, bigs = [], [], []
    for i in range(DEPTH):
        w = {n: small[n][i] for n in SMALL_PER_LAYER}
        w["d_skip_exp"] = jnp.repeat(w["d_skip"], HEAD_DIM)
        big = weights.layer(i)
        h, s = _layer_fwd(h, w, big, i, weights.stages(i))
        saved.append(s)
        ws.append(w)
        bigs.append(big)
    loss, dh, g_final = _loss_head(h, small["final_norm_w"], tgt)
    layer_g, layer_gb = [None] * DEPTH, [None] * DEPTH
    for i in reversed(range(DEPTH)):
        dh, layer_g[i], layer_gb[i] = _layer_bwd(dh, saved[i], ws[i], bigs[i], i, red, last=(i == 0))
    grads = {n: jnp.stack([layer_g[i][n] for i in range(DEPTH)]) for n in SMALL_PER_LAYER}
    grads["final_norm_w"] = g_final
    grads["meta_tokens"] = dh[N_PAD:ROW_X]
    return loss, dh[ROW_X:ROW_X + seq], layer_gb, grads


MESH = pl.DeviceIdType.MESH
LANES = 128
ANY = pl.BlockSpec(memory_space=pl.ANY)


def _place():
    x, y, c = lax.axis_index("x"), lax.axis_index("y"), lax.axis_index("c")
    chips = [(1 - x, y), (x, 1 - y), (1 - x, 1 - y)]
    return x, y, c, chips


def _remote(src, dst, send_sem, recv_sem, to):
    return pltpu.make_async_remote_copy(src_ref=src, dst_ref=dst, send_sem=send_sem, recv_sem=recv_sem,
                                        device_id=to, device_id_type=MESH)


class _WeightGatherer:
    GROUPS = {"a": ("w_in",), "b": tuple(n for n in BIG if n != "w_in")}

    def __init__(self, mine):
        self.mine = mine
        self.landing = {}
        self.ready = {}

    def _ici(self, layer, group):
        names = self.GROUPS[group]
        srcs = [self.mine[n] for n in names]
        outs = [jax.ShapeDtypeStruct((1, N_CHIPS) + s.shape[1:], BF16) for s in srcs]

        def make(ins, out, send_sems, recv_sems):
            x, y, c, chips = _place()
            me = 2 * x + y
            cps = []
            for p, ref in enumerate(ins):
                half = ref.shape[1] // 2
                rows = pl.ds(c * half, half)
                cps += [_remote(ref.at[layer, rows, :], out[p].at[0, me, rows, :], send_sems.at[3 * p + j],
                                recv_sems.at[3 * p + j], (*chip, c)) for j, chip in enumerate(chips)]
            return cps

        return _Carry(srcs, outs, 3 * len(names), make)

    def _d2d(self, layer, group):
        names = self.GROUPS[group]
        n = len(names)
        bufs = self.landing[(layer, group)]
        outs = [jax.ShapeDtypeStruct(b.shape, b.dtype) for b in bufs]

        def make(ins, out, send_sems, recv_sems):
            x, y, c, chips = _place()
            me = 2 * x + y
            sibling = (x, y, 1 - c)
            cps = []
            for p in range(n):
                half = out[p].shape[2] // 2
                for j, (cx, cy) in enumerate(chips):
                    blk = out[p].at[0, 2 * cx + cy, pl.ds(c * half, half), :]
                    cps.append(_remote(blk, blk, send_sems.at[4 * p + j], recv_sems.at[4 * p + j], sibling))
                cps.append(_remote(ins[n + p].at[layer], out[p].at[0, me], send_sems.at[4 * p + 3],
                                   recv_sems.at[4 * p + 3], sibling))
            return cps

        return _Carry(list(bufs) + [self.mine[m] for m in names], outs, 4 * n, make, alias={p: p for p in range(n)})

    def _landed(self, layer, group, bufs):
        self.landing[(layer, group)] = bufs

    def _done(self, layer, group, bufs):
        self.ready.setdefault(layer, {}).update(zip(self.GROUPS[group], bufs))

    def layer(self, i):
        if i == 0:
            for g in self.GROUPS:
                self._landed(0, g, _run_carry(self._ici(0, g), f"gather_ici_{g}_0"))
            for g in self.GROUPS:
                self._done(0, g, _run_carry(self._d2d(0, g), f"gather_d2d_{g}_0"))
        return _layer_view(self.ready[i])

    def stages(self, i):
        nxt = i + 1
        if nxt == DEPTH:
            return None
        st = {}
        for g in self.GROUPS:
            st[f"{g}_ici"] = (functools.partial(self._ici, nxt, g), functools.partial(self._landed, nxt, g))
            st[f"{g}_d2d"] = (functools.partial(self._d2d, nxt, g), functools.partial(self._done, nxt, g))
        return st


def _exchange_small(v, reduce, name):
    rows_per = v.shape[0]
    vm = pl.BlockSpec(memory_space=pltpu.VMEM)

    def body(v_ref, out_ref, *scratch):
        if reduce:
            land_ref, send_sems, recv_sems, local_sem = scratch
        else:
            land_ref = out_ref
            send_sems, recv_sems, local_sem = scratch
        x, y, c, chips = _place()
        me, sibling = (x, y, c), (x, y, 1 - c)

        def rows(px, py, pc):
            return land_ref.at[4 * px + 2 * py + pc]

        def copy(k, block, to, src=None):
            return _remote(rows(*block) if src is None else src, rows(*block), send_sems.at[k], recv_sems.at[k], to)

        mine = pltpu.make_async_copy(v_ref, rows(*me), local_sem)
        mine.start()
        first = [copy(0, me, sibling, src=v_ref)]
        first += [copy(1 + j, me, (*chip, c), src=v_ref) for j, chip in enumerate(chips)]
        for cp in first:
            cp.start()
        passed = [copy(4 + j, (*chip, c), sibling) for j, chip in enumerate(chips)]
        for j, chip in enumerate(chips):
            copy(1 + j, (*chip, c), me).wait_recv()
            passed[j].start()
        copy(0, sibling, me).wait_recv()
        for j, chip in enumerate(chips):
            copy(4 + j, (*chip, 1 - c), me).wait_recv()
        for cp in first + passed:
            cp.wait_send()
        mine.wait()
        if reduce:
            acc = land_ref[0]
            for d in range(1, 8):
                acc = acc + land_ref[d]
            out_ref[...] = acc

    sems = [pltpu.SemaphoreType.DMA((7,)), pltpu.SemaphoreType.DMA((7,)), pltpu.SemaphoreType.DMA]
    if reduce:
        out_shape = jax.ShapeDtypeStruct((rows_per, LANES), F32)
        scratch = [pltpu.VMEM((8, rows_per, LANES), F32)] + sems
    else:
        out_shape = jax.ShapeDtypeStruct((8, rows_per, LANES), F32)
        scratch = sems
    return pl.pallas_call(
        body, name=name, in_specs=[vm], out_specs=vm, out_shape=out_shape, scratch_shapes=scratch,
        compiler_params=pltpu.CompilerParams(vmem_limit_bytes=VMEM_LIMIT),
    )(v)


def _run_carry(carry, name):
    c_in, c_out, c_sem = carry.specs()
    n_in, n_out = len(c_in), len(c_out)

    def body(*refs):
        cps = carry.make(refs[:n_in], refs[n_in:n_in + n_out], *refs[n_in + n_out:])
        for cp in cps:
            cp.start()
        for cp in cps:
            cp.wait()

    return pl.pallas_call(body, name=name, in_specs=c_in, out_specs=c_out, out_shape=carry.outs,
                          scratch_shapes=c_sem, input_output_aliases=carry.aliases(0, 0))(*carry.ins)


def _row_tile(rows, last, itemsize=4, budget=2 * 1024 * 1024):
    return _tile(rows, tuple(t for t in (2048, 1024, 512, 256, 128, 64, 32, 16) if t * last * itemsize <= budget))


def _rs_add(g, got, core, name):
    _, half, last = got.shape
    tr = _row_tile(half, last)
    nb = half // tr

    def body(c_ref, g_ref, got_ref, o_ref):
        o_ref[...] = (g_ref[...] + got_ref[...]).astype(BF16)

    blk = pl.BlockSpec((None, tr, last), lambda s, i, c_ref: (s, i, 0))
    return pl.pallas_call(
        body, name=name,
        grid_spec=pltpu.PrefetchScalarGridSpec(
            num_scalar_prefetch=1, grid=(N_CHIPS, nb),
            in_specs=[pl.BlockSpec((None, tr, last), lambda s, i, c_ref: (s, c_ref[0] * nb + i, 0)), blk],
            out_specs=blk),
        out_shape=jax.ShapeDtypeStruct(got.shape, BF16),
        compiler_params=_cparams(("parallel", "parallel")),
    )(core.reshape(1).astype(jnp.int32), g, got)


def _rs_sum(own, got, acc, layer, chip, core, name):
    _, half, last = own.shape
    tr = _row_tile(half, last, budget=1024 * 1024)
    nb = half // tr
    ni = 0 if acc is None else 1

    def body(k_ref, c_ref, own_ref, got_ref, *rest):
        o_ref = rest[-1]
        tot = own_ref[...].astype(F32)
        for j in range(3):
            tot = tot + got_ref[j].astype(F32)
        o_ref[...] = tot

    return pl.pallas_call(
        body, name=name,
        grid_spec=pltpu.PrefetchScalarGridSpec(
            num_scalar_prefetch=2, grid=(nb,),
            in_specs=[pl.BlockSpec((None, tr, last), lambda i, k_ref, c_ref: (k_ref[0], i, 0)),
                      pl.BlockSpec((3, tr, last), lambda i, k_ref, c_ref: (0, i, 0))]
            + [pl.BlockSpec(memory_space=pl.ANY)] * ni,
            out_specs=pl.BlockSpec((None, tr, last), lambda i, k_ref, c_ref: (layer, c_ref[0] * nb + i, 0))),
        out_shape=jax.ShapeDtypeStruct((DEPTH, 2 * half, last), F32),
        input_output_aliases={4: 0} if ni else {},
        compiler_params=_cparams(("parallel",)),
    )(chip.reshape(1).astype(jnp.int32), core.reshape(1).astype(jnp.int32), own, got, *([acc] if ni else []))


def _rs_share(arrs):
    n = len(arrs)

    def body(*refs):
        out = refs[n:2 * n]
        send_sems, recv_sems = refs[2 * n:]
        x, y, c, _ = _place()
        sibling = (x, y, 1 - c)
        cps = []
        for p in range(n):
            half = out[p].shape[1] // 2
            mine = out[p].at[:, pl.ds(c * half, half), :]
            cps.append(_remote(mine, mine, send_sems.at[p], recv_sems.at[p], sibling))
        for cp in cps:
            cp.start()
        for p, cp in enumerate(cps):
            cp.wait_send()
            half = out[p].shape[1] // 2
            other = out[p].at[:, pl.ds((1 - c) * half, half), :]
            _remote(other, other, send_sems.at[p], recv_sems.at[p], sibling).wait_recv()

    return pl.pallas_call(
        body, name="rs_share", in_specs=[ANY] * n, out_specs=[ANY] * n,
        out_shape=[jax.ShapeDtypeStruct(a.shape, a.dtype) for a in arrs],
        input_output_aliases={p: p for p in range(n)},
        scratch_shapes=[pltpu.SemaphoreType.DMA((n,)), pltpu.SemaphoreType.DMA((n,))],
    )(*arrs)


class _GradReducer:
    def __init__(self, chip, core):
        self.chip, self.core = chip, core
        self.pending = None
        self.sums = None
        self.acc = {n: None for n in BIG}

    def push(self, layer, gb):
        assert self.pending is None
        self.pending = (layer, list(gb), list(gb.values()))

    def swap_carry(self):
        if self.pending is None:
            return None
        _, _, gs = self.pending
        outs = [jax.ShapeDtypeStruct((g.shape[0], g.shape[1] // 2, g.shape[2]), F32) for g in gs]

        def make(ins, out, send_sems, recv_sems):
            x, y, c, _ = _place()
            cps = []
            for p, ref in enumerate(ins):
                half = ref.shape[1] // 2
                cps.append(_remote(ref.at[:, pl.ds((1 - c) * half, half), :], out[p], send_sems.at[p], recv_sems.at[p],
                                   (x, y, 1 - c)))
            return cps

        return _Carry(gs, outs, len(gs), make)

    def after_swap(self, got):
        layer, names, gs = self.pending
        self.pending = None
        self.sums = (layer, names, [_rs_add(g, a, self.core, f"rs_add_{n}_{layer}") for n, g, a in zip(names, gs, got)])

    def ici_carry(self):
        if self.sums is None:
            return None
        _, _, ps = self.sums
        outs = [jax.ShapeDtypeStruct((3,) + p.shape[1:], BF16) for p in ps]

        def make(ins, out, send_sems, recv_sems):
            x, y, c, chips = _place()
            return [_remote(ins[p].at[2 * cx + cy], out[p].at[j], send_sems.at[3 * p + j], recv_sems.at[3 * p + j],
                            (cx, cy, c)) for p in range(len(ins)) for j, (cx, cy) in enumerate(chips)]

        return _Carry(ps, outs, 3 * len(ps), make)

    def after_ici(self, got):
        layer, names, ps = self.sums
        self.sums = None
        for n, p, a in zip(names, ps, got):
            self.acc[n] = _rs_sum(p, a, self.acc[n], layer, self.chip, self.core, f"rs_sum_{n}_{layer}")

    def finish(self):
        if self.pending is not None:
            self.after_swap(_run_carry(self.swap_carry(), "rs_swap_last"))
        if self.sums is not None:
            self.after_ici(_run_carry(self.ici_carry(), "rs_ici_last"))
        return dict(zip(BIG, _rs_share([self.acc[n] for n in BIG])))


def _adamw(w, g, m, v, name):
    shape = w.shape
    if len(shape) == 2:
        shape3 = (1,) + shape
    else:
        shape3 = (-1,) + shape[-2:]
    w2, g2, m2, v2 = (a.reshape(shape3) for a in (w, g, m, v))
    lead, rows, last = w2.shape
    tr = max([t for t in range(8, rows + 1, 8) if rows % t == 0 and t * last * 4 <= 2 * 1024 * 1024] or [rows])

    def body(w_ref, g_ref, m_ref, v_ref, d_ref, nm_ref, nv_ref):
        gv = g_ref[...]
        mn = ADAM_B1 * m_ref[...] + (1.0 - ADAM_B1) * gv
        vn = ADAM_B2 * v_ref[...] + (1.0 - ADAM_B2) * (gv * gv)
        m_hat = mn / (1.0 - ADAM_B1 ** ADAM_STEP)
        v_hat = vn / (1.0 - ADAM_B2 ** ADAM_STEP)
        d_ref[...] = -ADAM_LR * (m_hat / (jnp.sqrt(v_hat) + ADAM_EPS) + ADAM_WD * w_ref[...])
        nm_ref[...] = mn
        nv_ref[...] = vn

    blk = pl.BlockSpec((None, tr, last), lambda a, i: (a, i, 0))
    outs = pl.pallas_call(
        body, name=name, grid=(lead, rows // tr), in_specs=[blk] * 4, out_specs=[blk] * 3,
        out_shape=[jax.ShapeDtypeStruct((lead, rows, last), F32)] * 3,
        compiler_params=_cparams(("parallel", "parallel")),
    )(w2, g2, m2, v2)
    return tuple(o.reshape(shape) for o in outs)


SHARD_AXIS = {"conv_w": 2, "meta_tokens": 1}
SMALL_SHARDED = ("conv_w", "meta_tokens")
REPLICATED = ("mix_norm_w", "b_gate", "pool_scale", "conv_b", "dt_bias", "a_log", "d_skip", "ssd_norm_w", "mlp_norm_w",
              "final_norm_w")


def _flatten(arrs, dtype, row_mult):
    flat = jnp.concatenate([a.astype(dtype).reshape(-1) for a in arrs])
    n = flat.shape[0]
    rows = -(-n // (LANES * row_mult)) * row_mult
    return jnp.pad(flat, (0, rows * LANES - n)).reshape(rows, LANES)


def _unflatten(flat2d, shapes):
    flat = flat2d.reshape(-1)
    out, off = [], 0
    for sh in shapes:
        n = 1
        for d in sh:
            n *= d
        out.append(flat[off:off + n].reshape(sh))
        off += n
    return out


def kernel(x, meta_tokens, mix_norm_w, w_in, b_gate, pool_w_group, pool_scale, w_pool_up, conv_w, conv_b, dt_bias, a_log, d_skip, ssd_norm_w, w_ssd_out, w_o, mlp_norm_w, w_ff1, w_ff2, final_norm_w, loss_target, m_meta_tokens, m_mix_norm_w, m_w_in, m_b_gate, m_pool_w_group, m_pool_scale, m_w_pool_up, m_conv_w, m_conv_b, m_dt_bias, m_a_log, m_d_skip, m_ssd_norm_w, m_w_ssd_out, m_w_o, m_mlp_norm_w, m_w_ff1, m_w_ff2, m_final_norm_w, v_meta_tokens, v_mix_norm_w, v_w_in, v_b_gate, v_pool_w_group, v_pool_scale, v_w_pool_up, v_conv_w, v_conv_b, v_dt_bias, v_a_log, v_d_skip, v_ssd_norm_w, v_w_ssd_out, v_w_o, v_mlp_norm_w, v_w_ff1, v_w_ff2, v_final_norm_w):
    names = ("meta_tokens",) + PER_LAYER + ("final_norm_w",)
    par = dict(meta_tokens=meta_tokens, mix_norm_w=mix_norm_w, w_in=w_in, b_gate=b_gate, pool_w_group=pool_w_group,
               pool_scale=pool_scale, w_pool_up=w_pool_up, conv_w=conv_w, conv_b=conv_b, dt_bias=dt_bias, a_log=a_log,
               d_skip=d_skip, ssd_norm_w=ssd_norm_w, w_ssd_out=w_ssd_out, w_o=w_o, mlp_norm_w=mlp_norm_w, w_ff1=w_ff1,
               w_ff2=w_ff2, final_norm_w=final_norm_w)
    mom = dict(meta_tokens=m_meta_tokens, mix_norm_w=m_mix_norm_w, w_in=m_w_in, b_gate=m_b_gate, pool_w_group=m_pool_w_group,
               pool_scale=m_pool_scale, w_pool_up=m_w_pool_up, conv_w=m_conv_w, conv_b=m_conv_b, dt_bias=m_dt_bias,
               a_log=m_a_log, d_skip=m_d_skip, ssd_norm_w=m_ssd_norm_w, w_ssd_out=m_w_ssd_out, w_o=m_w_o,
               mlp_norm_w=m_mlp_norm_w, w_ff1=m_w_ff1, w_ff2=m_w_ff2, final_norm_w=m_final_norm_w)
    var = dict(meta_tokens=v_meta_tokens, mix_norm_w=v_mix_norm_w, w_in=v_w_in, b_gate=v_b_gate, pool_w_group=v_pool_w_group,
               pool_scale=v_pool_scale, w_pool_up=v_w_pool_up, conv_w=v_conv_w, conv_b=v_conv_b, dt_bias=v_dt_bias,
               a_log=v_a_log, d_skip=v_d_skip, ssd_norm_w=v_ssd_norm_w, w_ssd_out=v_w_ssd_out, w_o=v_w_o,
               mlp_norm_w=v_mlp_norm_w, w_ff1=v_w_ff1, w_ff2=v_w_ff2, final_norm_w=v_final_norm_w)
    chip = 2 * lax.axis_index("x") + lax.axis_index("y")
    core = lax.axis_index("c")

    weights = _WeightGatherer({n: par[n].astype(BF16).reshape(DEPTH, -1, par[n].shape[-1]) for n in BIG})
    small = {n: par[n] for n in REPLICATED}
    small_shapes = [par[n].shape for n in SMALL_SHARDED]
    got_small = _exchange_small(_flatten([par[n] for n in SMALL_SHARDED], F32, 8), False, "gather_small")
    pieces = [_unflatten(got_small[2 * k], small_shapes) for k in range(N_CHIPS)]
    for j, n in enumerate(SMALL_SHARDED):
        small[n] = jnp.concatenate([pieces[k][j] for k in range(N_CHIPS)], axis=SHARD_AXIS[n])

    reducer = _GradReducer(chip, core)
    loss, dx, _, grads = _local_step(x[0], loss_target[0], weights, small, reducer)
    loss = lax.psum(loss, ("x", "y", "c"))

    small_names = REPLICATED + SMALL_SHARDED
    red = _exchange_small(_flatten([grads[n] for n in small_names], F32, 8), True, "allreduce_small")
    gsum = dict(zip(small_names, _unflatten(red, [grads[n].shape for n in small_names])))
    for n in SMALL_SHARDED:
        width = par[n].shape[SHARD_AXIS[n]]
        gsum[n] = lax.dynamic_slice_in_dim(gsum[n], chip * width, width, axis=SHARD_AXIS[n])

    gsum.update({n: g.reshape(par[n].shape) for n, g in reducer.finish().items()})

    delta, new_m, new_v = {}, {}, {}
    for n in BIG:
        if n == "w_in":
            tr = lambda a: jnp.swapaxes(a, 1, 2)
            outs = _adamw(tr(par[n]), tr(gsum[n]), tr(mom[n]), tr(var[n]), "adamw_" + n)
            delta[n], new_m[n], new_v[n] = (tr(o) for o in outs)
        else:
            delta[n], new_m[n], new_v[n] = _adamw(par[n], gsum[n], mom[n], var[n], "adamw_" + n)
    rest = [n for n in names if n not in BIG]
    shapes = [par[n].shape for n in rest]
    packed = [_flatten([d[n] for n in rest], F32, 8) for d in (par, gsum, mom, var)]
    for d, flat in zip((delta, new_m, new_v), _adamw(*packed, "adamw_small")):
        d.update(zip(rest, _unflatten(flat, shapes)))

    order = ("meta_tokens", "mix_norm_w", "w_in", "b_gate", "pool_w_group", "pool_scale", "w_pool_up", "conv_w", "conv_b",
             "dt_bias", "a_log", "d_skip", "ssd_norm_w", "w_ssd_out", "w_o", "mlp_norm_w", "w_ff1", "w_ff2", "final_norm_w")
    return (loss, dx[None], *[gsum[n] for n in order], *[delta[n] for n in order], *[new_m[n] for n in order],
            *[new_v[n] for n in order])
```

```python
import functools

import jax
import jax.numpy as jnp
from jax import lax
from jax.experimental import pallas as pl
from jax.experimental.pallas import tpu as pltpu

F32 = jnp.float32
BF16 = jnp.bfloat16

D_MODEL = 1024
DEPTH = 4
N_META = 16
N_PAD = 112
ROW_X = N_PAD + N_META
POOL_WINDOWS = (2, 4, 8, 16)
POOL_GDIM = 256
D_INNER = 2048
N_HEADS = 32
HEAD_DIM = 64
N_GROUPS = 8
HEADS_PER_GROUP = 4
GROUP_W = HEADS_PER_GROUP * HEAD_DIM
D_STATE = 128
CHUNK = 128
D_XBC = 4096
D_FF = 4096
EPS = 1e-5
OFF_Z, OFF_XBC, OFF_DT, OFF_GATE, IN_COLS = 1024, 3072, 7168, 7200, 9248
PZ, PGATE, PPOOL, PXBC, PCOLS = 0, 2048, 4096, 5120, 9216
DT_PAD = 128

ADAM_LR, ADAM_B1, ADAM_B2, ADAM_EPS, ADAM_WD, ADAM_STEP = 0.001, 0.9, 0.999, 1e-08, 0.01, 10

VMEM_LIMIT = 56 * 1024 * 1024

_NN = (((1,), (0,)), ((), ()))
_NT = (((1,), (1,)), ((), ()))
_TN = (((0,), (0,)), ((), ()))


def _dot(a, b, dn=_NN):
    return lax.dot_general(a, b, dn, preferred_element_type=F32)


def _cparams(sem):
    return pltpu.CompilerParams(dimension_semantics=sem, vmem_limit_bytes=VMEM_LIMIT)


def _tile(n, cands):
    for c in cands:
        if n % c == 0:
            return c
    raise ValueError(f"no tile for {n} in {cands}")


def _sigmoid(x):
    return 0.5 * jnp.tanh(0.5 * x) + 0.5


def _softplus(x):
    return jnp.maximum(x, 0.0) + jnp.log(1.0 + jnp.exp(-jnp.abs(x)))


def _bs(shape, fn):
    return pl.BlockSpec(shape, fn)


class _Carry:
    def __init__(self, ins, outs, n_sems, make, alias=None):
        self.ins, self.outs, self.n_sems, self.make = list(ins), list(outs), n_sems, make
        self.alias = dict(alias or {})

    def aliases(self, first_in, first_out):
        return {first_in + i: first_out + o for i, o in self.alias.items()}

    def specs(self):
        hbm = pl.BlockSpec(memory_space=pl.ANY)
        return [hbm] * len(self.ins), [hbm] * len(self.outs), [pltpu.SemaphoreType.DMA((self.n_sems,))] * 2

    def at_edges(self, grid, in_refs, out_refs, sems):
        ids = [pl.program_id(a) for a in range(len(grid))]
        first = functools.reduce(jnp.logical_and, [i == 0 for i in ids])
        last = functools.reduce(jnp.logical_and, [i == n - 1 for i, n in zip(ids, grid)])

        def begin():
            @pl.when(first)
            def _():
                for cp in self.make(in_refs, out_refs, *sems):
                    cp.start()

        def end():
            @pl.when(last)
            def _():
                for cp in self.make(in_refs, out_refs, *sems):
                    cp.wait()

        return begin, end


def _mm(name, dn, grid, a, a_spec, b, b_spec, outs, *, extras=(), epilogue=None, carry=None):
    nk = grid[2]
    ne, no = len(extras), len(outs)
    blk = tuple(d for d in outs[0][2].block_shape if d is not None)
    c_in, c_out, c_sem = carry.specs() if carry is not None else ([], [], [])
    nci, nco = len(c_in), len(c_out)

    def body(a_ref, b_ref, *rest):
        e_refs, o_refs = rest[:ne], rest[ne + nci:ne + nci + no]
        if carry is not None:
            begin, end = carry.at_edges(grid, rest[ne:ne + nci], rest[ne + nci + no:ne + nci + no + nco], rest[-2:])
            begin()
        p = _dot(a_ref[...].astype(BF16), b_ref[...].astype(BF16), dn)

        def finish(acc):
            outs = epilogue(acc, *[e[...] for e in e_refs]) if epilogue is not None else (acc,)
            for o, v in zip(o_refs, outs):
                o[...] = v.astype(o.dtype)

        if nk == 1:
            finish(p)
        else:
            acc_ref = rest[ne + nci + no + nco]
            kk = pl.program_id(2)

            @pl.when(kk == 0)
            def _():
                acc_ref[...] = p

            @pl.when(kk > 0)
            def _():
                acc_ref[...] += p

            @pl.when(kk == nk - 1)
            def _():
                finish(acc_ref[...])
        if carry is not None:
            end()

    res = pl.pallas_call(
        body,
        name=name,
        grid=grid,
        in_specs=[a_spec, b_spec, *[s for _, s in extras]] + c_in,
        out_specs=[o[2] for o in outs] + c_out,
        out_shape=[jax.ShapeDtypeStruct(o[0], o[1]) for o in outs] + (carry.outs if carry is not None else []),
        scratch_shapes=([pltpu.VMEM(blk, F32)] if nk > 1 else []) + c_sem,
        input_output_aliases=carry.aliases(2 + ne, no) if carry is not None else {},
        compiler_params=_cparams(("arbitrary",) * 3 if carry is not None else ("parallel", "parallel", "arbitrary")),
    )(a, b, *[e for e, _ in extras], *(carry.ins if carry is not None else []))
    if carry is not None:
        return res[:no], res[no:]
    return res[0] if no == 1 else res


def _norm_fwd(h, w, name):
    t, d = h.shape
    tm = _tile(t, (1056, 384, 128))

    def body(h_ref, w_ref, u_ref):
        x = h_ref[...]
        r = lax.rsqrt(jnp.mean(x * x, axis=-1, keepdims=True) + EPS)
        u_ref[...] = (x * r * w_ref[...]).astype(BF16)

    return pl.pallas_call(
        body, name=name, grid=(t // tm,),
        in_specs=[pl.BlockSpec((tm, d), lambda i: (i, 0)), pl.BlockSpec((1, d), lambda i: (0, 0))],
        out_specs=pl.BlockSpec((tm, d), lambda i: (i, 0)),
        out_shape=jax.ShapeDtypeStruct((t, d), BF16),
        compiler_params=_cparams(("parallel",)),
    )(h, w.reshape(1, d))


def _norm_bwd(dres, du, h, w, name):
    t, d = h.shape
    tm = _tile(t, (528, 384, 128))

    def body(dres_ref, du_ref, h_ref, w_ref, dh_ref, dw_ref):
        x = h_ref[...]
        r = lax.rsqrt(jnp.mean(x * x, axis=-1, keepdims=True) + EPS)
        xhat = x * r
        du_v = du_ref[...]
        g = du_v * w_ref[...]
        dh_ref[...] = dres_ref[...] + r * (g - xhat * jnp.mean(g * xhat, axis=-1, keepdims=True))

        @pl.when(pl.program_id(0) == 0)
        def _():
            dw_ref[...] = jnp.zeros_like(dw_ref)

        dw_ref[...] += jnp.sum(du_v * xhat, axis=0, keepdims=True)

    row = pl.BlockSpec((tm, d), lambda i: (i, 0))
    vec = pl.BlockSpec((1, d), lambda i: (0, 0))
    dh, dw = pl.pallas_call(
        body, name=name, grid=(t // tm,),
        in_specs=[row, row, row, vec], out_specs=[row, vec],
        out_shape=[jax.ShapeDtypeStruct((t, d), F32), jax.ShapeDtypeStruct((1, d), F32)],
        compiler_params=_cparams(("arbitrary",)),
    )(dres, du, h, w.reshape(1, d))
    return dh, dw.reshape(d)


def _loss_head(h, w, tgt):
    t, d = h.shape
    tm = CHUNK
    nb = ROW_X // tm

    def body(h_ref, w_ref, t_ref, loss_ref, dh_ref, dw_ref):
        i = pl.program_id(0)
        x = h_ref[...]
        r = lax.rsqrt(jnp.mean(x * x, axis=-1, keepdims=True) + EPS)
        xhat = x * r
        wv = w_ref[...]
        live = i >= nb
        err = jnp.where(live, xhat * wv - t_ref[...], 0.0)
        dout = err * (1.0 / d)
        g = dout * wv
        dh_ref[...] = r * (g - xhat * jnp.mean(g * xhat, axis=-1, keepdims=True))

        @pl.when(i == 0)
        def _():
            loss_ref[...] = jnp.zeros_like(loss_ref)
            dw_ref[...] = jnp.zeros_like(dw_ref)

        loss_ref[...] += 0.5 * jnp.sum(jnp.sum(err * err, axis=-1, keepdims=True), axis=0, keepdims=True) * (1.0 / d)
        dw_ref[...] += jnp.sum(dout * xhat, axis=0, keepdims=True)

    row = pl.BlockSpec((tm, d), lambda i: (i, 0))
    vec = pl.BlockSpec((1, d), lambda i: (0, 0))
    loss, dh, dw = pl.pallas_call(
        body, name="loss_head", grid=(t // tm,),
        in_specs=[row, vec, pl.BlockSpec((tm, d), lambda i: (jnp.maximum(i - nb, 0), 0))],
        out_specs=[pl.BlockSpec((1, 1), lambda i: (0, 0)), row, vec],
        out_shape=[jax.ShapeDtypeStruct((1, 1), F32), jax.ShapeDtypeStruct((t, d), F32), jax.ShapeDtypeStruct((1, d), F32)],
        compiler_params=_cparams(("arbitrary",)),
    )(h, w.reshape(1, d), tgt)
    return loss[0, 0], dh, dw.reshape(d)


def _gate_fwd(proj, b_gate, y_pool, y_ssd):
    t = proj.shape[0]
    d = D_MODEL
    tm = _tile(t, (528, 384, 128))

    def body(gp_ref, gs_ref, bp_ref, bs_ref, yp_ref, ys_ref, o_ref):
        gp = _sigmoid(gp_ref[...] + bp_ref[...])
        gs = _sigmoid(gs_ref[...] + bs_ref[...])
        o_ref[...] = (gp * yp_ref[...] + gs * ys_ref[...]).astype(BF16)

    row = pl.BlockSpec((tm, d), lambda i: (i, 0))
    return pl.pallas_call(
        body, name="gate_fwd", grid=(t // tm,),
        in_specs=[pl.BlockSpec((tm, d), lambda i: (i, PGATE // d)), pl.BlockSpec((tm, d), lambda i: (i, PGATE // d + 1)),
                  pl.BlockSpec((1, d), lambda i: (0, 0)), pl.BlockSpec((1, d), lambda i: (0, 1)), row, row],
        out_specs=row, out_shape=jax.ShapeDtypeStruct((t, d), BF16),
        compiler_params=_cparams(("parallel",)),
    )(proj, proj, b_gate.reshape(1, 2 * d), b_gate.reshape(1, 2 * d), y_pool, y_ssd)


def _gate_bwd(dmerged, proj, b_gate, y_pool, y_ssd):
    t = proj.shape[0]
    d = D_MODEL
    tm = _tile(t, (384, 128))

    def body(dm_ref, gp_ref, gs_ref, bp_ref, bs_ref, yp_ref, ys_ref, dyp_ref, dys_ref, dg_ref, db_ref):
        dm = dm_ref[...]
        gp = _sigmoid(gp_ref[...] + bp_ref[...])
        gs = _sigmoid(gs_ref[...] + bs_ref[...])
        dyp_ref[...] = (dm * gp).astype(BF16)
        dys_ref[...] = (dm * gs).astype(BF16)
        dgp = dm * yp_ref[...] * gp * (1.0 - gp)
        dgs = dm * ys_ref[...] * gs * (1.0 - gs)
        dg_ref[:, :d] = dgp.astype(BF16)
        dg_ref[:, d:] = dgs.astype(BF16)

        @pl.when(pl.program_id(0) == 0)
        def _():
            db_ref[...] = jnp.zeros_like(db_ref)

        db_ref[:, :d] += jnp.sum(dgp, axis=0, keepdims=True)
        db_ref[:, d:] += jnp.sum(dgs, axis=0, keepdims=True)

    row = pl.BlockSpec((tm, d), lambda i: (i, 0))
    dyp, dys, dg, db = pl.pallas_call(
        body, name="gate_bwd", grid=(t // tm,),
        in_specs=[row, pl.BlockSpec((tm, d), lambda i: (i, PGATE // d)), pl.BlockSpec((tm, d), lambda i: (i, PGATE // d + 1)),
                  pl.BlockSpec((1, d), lambda i: (0, 0)), pl.BlockSpec((1, d), lambda i: (0, 1)), row, row],
        out_specs=[row, row, pl.BlockSpec((tm, 2 * d), lambda i: (i, PGATE // (2 * d))),
                   pl.BlockSpec((1, 2 * d), lambda i: (0, 0))],
        out_shape=[jax.ShapeDtypeStruct((t, d), BF16), jax.ShapeDtypeStruct((t, d), BF16),
                   jax.ShapeDtypeStruct((t, PCOLS), BF16), jax.ShapeDtypeStruct((1, 2 * d), F32)],
        compiler_params=_cparams(("arbitrary",)),
    )(dmerged, proj, proj, b_gate.reshape(1, 2 * d), b_gate.reshape(1, 2 * d), y_pool, y_ssd)
    return dyp, dys, dg, db.reshape(2 * d)


POOL_HALO = 16


def _pool_counts(row0, n, win):
    pos1 = row0 + lax.broadcasted_iota(jnp.int32, (n, 1), 0) - (N_PAD - 1)
    return jnp.clip(pos1, 1, win).astype(F32)


N_CHIPS = 4
WG_ROWS = POOL_GDIM // N_CHIPS
WUP_ROWS = D_MODEL // N_CHIPS


def _group_w(wg_ref, g):
    return jnp.concatenate([wg_ref[k, g] for k in range(N_CHIPS)], axis=0)


def _pool_w_specs(layer):
    return [pl.BlockSpec((None, N_CHIPS, len(POOL_WINDOWS), WG_ROWS, POOL_GDIM), lambda i: (0, 0, 0, 0, 0)),
            pl.BlockSpec((None, N_CHIPS, WUP_ROWS, D_MODEL), lambda i: (0, 0, 0, 0))]


def _pool_fwd(proj, wg, scale, wup, layer):
    t = proj.shape[0]
    d = D_MODEL
    tm = _tile(t, (384, 128))
    hb = tm // POOL_HALO

    def body(u_ref, halo_ref, sc_ref, wg_ref, wup_ref, pooled_ref, yg_ref, ypm_ref, yp_ref):
        i = pl.program_id(0)
        x = u_ref[...]
        halo = jnp.where(i > 0, halo_ref[...], 0.0)
        xc = jnp.concatenate([halo, x], axis=0)
        for g, win in enumerate(POOL_WINDOWS):
            sl = slice(g * POOL_GDIM, (g + 1) * POOL_GDIM)
            s = xc[:, sl]
            k = 1
            while k < win:
                s = s + pltpu.roll(s, k, axis=0)
                k *= 2
            pooled = s[POOL_HALO:] / _pool_counts(i * tm, tm, win) - x[:, sl]
            pb = pooled.astype(BF16)
            pooled_ref[:, sl] = pb
            yg_ref[:, sl] = _dot(pb, _group_w(wg_ref, g))
        ypm = (yg_ref[...] * sc_ref[...]).astype(BF16)
        ypm_ref[...] = ypm
        acc = _dot(ypm[:, :WUP_ROWS], wup_ref[0])
        for k in range(1, N_CHIPS):
            acc = acc + _dot(ypm[:, k * WUP_ROWS:(k + 1) * WUP_ROWS], wup_ref[k])
        yp_ref[...] = acc

    row = pl.BlockSpec((tm, d), lambda i: (i, 0))
    return pl.pallas_call(
        body, name=f"pool_fwd_{layer}", grid=(t // tm,),
        in_specs=[pl.BlockSpec((tm, d), lambda i: (i, PPOOL // d)),
                  pl.BlockSpec((POOL_HALO, d), lambda i: (jnp.maximum(i * hb - 1, 0), PPOOL // d)),
                  pl.BlockSpec((1, d), lambda i: (0, 0))] + _pool_w_specs(layer),
        out_specs=[row, row, row, row],
        out_shape=[jax.ShapeDtypeStruct((t, d), BF16), jax.ShapeDtypeStruct((t, d), F32),
                   jax.ShapeDtypeStruct((t, d), BF16), jax.ShapeDtypeStruct((t, d), F32)],
        compiler_params=_cparams(("parallel",)),
    )(proj, proj, scale.reshape(1, d), wg, wup)


def _pool_bwd_a(dy_pool, yg, scale, wg, wup, layer):
    t, d = yg.shape
    tm = _tile(t, (384, 128))

    def body(dy_ref, yg_ref, sc_ref, wg_ref, wup_ref, q_ref, dyg_ref, dsc_ref):
        dy = dy_ref[...]
        dypm = jnp.concatenate([_dot(dy, wup_ref[k], _NT) for k in range(N_CHIPS)], axis=1)

        @pl.when(pl.program_id(0) == 0)
        def _():
            dsc_ref[...] = jnp.zeros_like(dsc_ref)

        dsc_ref[...] += jnp.sum(dypm * yg_ref[...], axis=0, keepdims=True)
        dyg = (dypm * sc_ref[...]).astype(BF16)
        dyg_ref[...] = dyg
        for g in range(len(POOL_WINDOWS)):
            sl = slice(g * POOL_GDIM, (g + 1) * POOL_GDIM)
            q_ref[:, sl] = _dot(dyg[:, sl], _group_w(wg_ref, g), _NT)

    row = pl.BlockSpec((tm, d), lambda i: (i, 0))
    vec = pl.BlockSpec((1, d), lambda i: (0, 0))
    q, dyg, dsc = pl.pallas_call(
        body, name=f"pool_bwd_a_{layer}", grid=(t // tm,),
        in_specs=[row, row, vec] + _pool_w_specs(layer),
        out_specs=[row, row, vec],
        out_shape=[jax.ShapeDtypeStruct((t, d), F32), jax.ShapeDtypeStruct((t, d), BF16), jax.ShapeDtypeStruct((1, d), F32)],
        compiler_params=_cparams(("arbitrary",)),
    )(dy_pool, yg, scale.reshape(1, d), wg, wup)
    return q, dyg, dsc.reshape(d)


def _pool_bwd_b(q, dproj):
    t, d = q.shape
    tm = _tile(t, (384, 128))
    hb = tm // POOL_HALO
    nt = t // tm
    n = tm + POOL_HALO

    def body(q_ref, halo_ref, _, o_ref):
        i = pl.program_id(0)
        qv = q_ref[...]
        halo = jnp.where(i < nt - 1, halo_ref[...], 0.0)
        qc = jnp.concatenate([qv, halo], axis=0)
        for g, win in enumerate(POOL_WINDOWS):
            sl = slice(g * POOL_GDIM, (g + 1) * POOL_GDIM)
            s = qc[:, sl] / _pool_counts(i * tm, n, win)
            k = 1
            while k < win:
                s = s + pltpu.roll(s, n - k, axis=0)
                k *= 2
            o_ref[:, sl] = (s[:tm] - qv[:, sl]).astype(BF16)

    row = pl.BlockSpec((tm, d), lambda i: (i, 0))
    return pl.pallas_call(
        body, name="pool_bwd_b", grid=(nt,),
        in_specs=[row, pl.BlockSpec((POOL_HALO, d), lambda i: (jnp.minimum((i + 1) * hb, t // POOL_HALO - 1), 0)),
                  pl.BlockSpec(memory_space=pl.ANY)],
        out_specs=pl.BlockSpec((tm, d), lambda i: (i, PPOOL // d)), out_shape=jax.ShapeDtypeStruct(dproj.shape, BF16),
        input_output_aliases={2: 0},
        compiler_params=_cparams(("parallel",)),
    )(q, q, dproj)


def _pool_dwg(pooled, dyg, layer):
    t, d = pooled.shape
    tk = _tile(t, (1056, 384, 128))
    nk = t // tk
    gd = POOL_GDIM
    ng = d // gd

    def body(p_ref, g_ref, o_ref):
        @pl.when(pl.program_id(1) == 0)
        def _():
            o_ref[...] = jnp.zeros_like(o_ref)

        part = _dot(p_ref[...], g_ref[...], _TN)
        for k in range(N_CHIPS):
            o_ref[k] += part[k * WG_ROWS:(k + 1) * WG_ROWS]

    blk = pl.BlockSpec((tk, gd), lambda g, k: (k, g))
    return pl.pallas_call(
        body, name=f"pool_dwg_{layer}", grid=(ng, nk), in_specs=[blk, blk],
        out_specs=pl.BlockSpec((N_CHIPS, None, WG_ROWS, gd), lambda g, k: (0, g, 0, 0)),
        out_shape=jax.ShapeDtypeStruct((N_CHIPS, ng, WG_ROWS, gd), F32),
        compiler_params=_cparams(("parallel", "arbitrary")),
    )(pooled, dyg)


CONV_W = 4
CONV_HALO = 8
XBC_BLK = PXBC // 1024


def _conv_fwd(proj, conv_w, conv_b):
    t = proj.shape[0]
    cw = 1024
    tm = _tile(t, (1056, 384, 128))
    hb = tm // CONV_HALO

    def body(x_ref, halo_ref, w_ref, b_ref, o_ref, pre_ref):
        i = pl.program_id(1)
        x = x_ref[...]
        halo = jnp.where(i > 0, halo_ref[...], 0.0)
        xc = jnp.concatenate([halo, x], axis=0)
        w = w_ref[...]
        acc = b_ref[...] + x * w[CONV_W - 1:CONV_W, :]
        for k in range(CONV_W - 1):
            acc = acc + pltpu.roll(xc, CONV_W - 1 - k, axis=0)[CONV_HALO:] * w[k:k + 1, :]
        row = i * tm + lax.broadcasted_iota(jnp.int32, (tm, 1), 0)
        pre_ref[...] = acc
        o_ref[...] = jnp.where(row >= N_PAD, acc * _sigmoid(acc), 0.0)

    blk = pl.BlockSpec((tm, cw), lambda j, i: (i, j))
    return pl.pallas_call(
        body, name="conv_fwd", grid=(D_XBC // cw, t // tm),
        in_specs=[pl.BlockSpec((tm, cw), lambda j, i: (i, XBC_BLK + j)),
                  pl.BlockSpec((CONV_HALO, cw), lambda j, i: (jnp.maximum(i * hb - 1, 0), XBC_BLK + j)),
                  pl.BlockSpec((CONV_W, cw), lambda j, i: (0, j)), pl.BlockSpec((1, cw), lambda j, i: (0, j))],
        out_specs=[blk, blk],
        out_shape=[jax.ShapeDtypeStruct((t, D_XBC), F32), jax.ShapeDtypeStruct((t, D_XBC), F32)],
        compiler_params=_cparams(("parallel", "parallel")),
    )(proj, proj, conv_w, conv_b.reshape(1, D_XBC))


def _conv_bwd(dxa, coff, proj, pre, conv_w, dproj):
    t, ncols = dxa.shape
    cw = 1024
    tm = _tile(t, (528, 384, 128))
    hb = tm // CONV_HALO
    nt = t // tm
    n = tm + CONV_HALO

    def body(d_ref, dn_ref, p_ref, pn_ref, x_ref, w_ref, _, o_ref, dw_ref, db_ref):
        i = pl.program_id(1)
        last = i == nt - 1
        xc = jnp.concatenate([p_ref[...], pn_ref[...]], axis=0)
        df = jnp.concatenate([d_ref[...], dn_ref[...]], axis=0)
        w = w_ref[...]
        sig = _sigmoid(xc)
        row = i * tm + lax.broadcasted_iota(jnp.int32, (n, 1), 0)
        live = (row >= N_PAD) & ((row < (i + 1) * tm) | jnp.logical_not(last))
        dxc = jnp.where(live, df * (sig * (1.0 + xc * (1.0 - sig))), 0.0)
        ahead = [pltpu.roll(dxc, n - (CONV_W - 1 - k), axis=0)[:tm] for k in range(CONV_W - 1)] + [dxc[:tm]]
        acc = ahead[0] * w[0:1, :]
        for k in range(1, CONV_W):
            acc = acc + ahead[k] * w[k:k + 1, :]
        o_ref[...] = acc.astype(BF16)

        @pl.when(i == 0)
        def _():
            dw_ref[...] = jnp.zeros_like(dw_ref)
            db_ref[...] = jnp.zeros_like(db_ref)

        x = x_ref[...]
        db_ref[...] += jnp.sum(ahead[CONV_W - 1], axis=0, keepdims=True)
        for k in range(CONV_W):
            dw_ref[k:k + 1, :] += jnp.sum(ahead[k] * x, axis=0, keepdims=True)

    def pspec(rows, fn):
        return pl.BlockSpec((rows, cw), lambda j, i: (fn(i), coff + j))

    nxt = lambda i: jnp.minimum((i + 1) * hb, t // CONV_HALO - 1)
    dxbc, dw, db = pl.pallas_call(
        body, name=f"conv_bwd_{coff}", grid=(ncols // cw, nt),
        in_specs=[pl.BlockSpec((tm, cw), lambda j, i: (i, j)), pl.BlockSpec((CONV_HALO, cw), lambda j, i: (nxt(i), j)),
                  pspec(tm, lambda i: i), pspec(CONV_HALO, nxt),
                  pl.BlockSpec((tm, cw), lambda j, i: (i, XBC_BLK + coff + j)),
                  pl.BlockSpec((CONV_W, cw), lambda j, i: (0, coff + j)),
                  pl.BlockSpec(memory_space=pl.ANY)],
        out_specs=[pl.BlockSpec((tm, cw), lambda j, i: (i, XBC_BLK + coff + j)), pl.BlockSpec((CONV_W, cw), lambda j, i: (0, j)),
                   pl.BlockSpec((1, cw), lambda j, i: (0, j))],
        out_shape=[jax.ShapeDtypeStruct(dproj.shape, BF16), jax.ShapeDtypeStruct((CONV_W, ncols), F32),
                   jax.ShapeDtypeStruct((1, ncols), F32)],
        input_output_aliases={6: 0},
        compiler_params=_cparams(("parallel", "arbitrary")),
    )(dxa, dxa, pre, pre, proj, conv_w, dproj)
    return dxbc, dw, db.reshape(ncols)


def _cumsum(x, axis, reverse=False):
    n = x.shape[axis]
    idx = lax.broadcasted_iota(jnp.int32, x.shape, axis)
    k = 1
    while k < n:
        if reverse:
            x = x + jnp.where(idx < n - k, pltpu.roll(x, n - k, axis=axis), 0.0)
        else:
            x = x + jnp.where(idx >= k, pltpu.roll(x, k, axis=axis), 0.0)
        k *= 2
    return x


def _head_masks():
    lane = lax.broadcasted_iota(jnp.int32, (1, GROUP_W), 1)
    return [(lane >= r * HEAD_DIM) & (lane < (r + 1) * HEAD_DIM) for r in range(HEADS_PER_GROUP)]


def _expand_heads(cols, hm):
    out = jnp.where(hm[0], cols[:, 0:1], 0.0)
    for r in range(1, HEADS_PER_GROUP):
        out = out + jnp.where(hm[r], cols[:, r:r + 1], 0.0)
    return out


def _ssd_decay(dt_raw, dt_bias, a_log):
    t = dt_raw.shape[0]

    hpg = HEADS_PER_GROUP

    def body(raw_ref, b_ref, al_ref, dt_ref, acs_ref, sig_ref, rows_ref):
        raw = raw_ref[...] + b_ref[...]
        rowid = pl.program_id(0) * CHUNK + lax.broadcasted_iota(jnp.int32, (CHUNK, 1), 0)
        dt = jnp.where(rowid >= N_PAD, _softplus(raw), 0.0)
        acs = _cumsum(dt * -jnp.exp(al_ref[...]), 0)
        sig = _sigmoid(raw)
        acs_t = acs.T
        for g in range(N_GROUPS):
            heads = slice(g * hpg, (g + 1) * hpg)
            dt_ref[g] = dt[:, heads]
            acs_ref[g] = acs[:, heads]
            sig_ref[g] = sig[:, heads]
            rows_ref[g] = acs_t[heads, :]

    blk = pl.BlockSpec((CHUNK, DT_PAD), lambda c: (c, 0))
    vec = pl.BlockSpec((1, DT_PAD), lambda c: (0, 0))
    cols = pl.BlockSpec((N_GROUPS, CHUNK, hpg), lambda c: (0, c, 0))
    pad = lambda v: jnp.pad(v, (0, DT_PAD - N_HEADS)).reshape(1, DT_PAD)
    dt, acs, sig, rows = pl.pallas_call(
        body, name="ssd_decay", grid=(t // CHUNK,), in_specs=[blk, vec, vec],
        out_specs=[cols, cols, cols, pl.BlockSpec((N_GROUPS, hpg, CHUNK), lambda c: (0, 0, c))],
        out_shape=[jax.ShapeDtypeStruct((N_GROUPS, t, hpg), F32)] * 3 + [jax.ShapeDtypeStruct((N_GROUPS, hpg, t), F32)],
        compiler_params=_cparams(("parallel",)),
    )(dt_raw, pad(dt_bias), pad(a_log))
    return dict(dt=dt, acs=acs, acs_rows=rows, sig=sig, a_log=a_log.reshape(N_GROUPS, 1, hpg))


def _ssd_q(dtc, acs4, acs_r):
    hm = _head_masks()
    dt_exp = _expand_heads(dtc, hm)
    acs = _expand_heads(acs4, hm)
    atot = acs[CHUNK - 1:CHUNK, :]
    return dict(dtc=dtc, hm=hm, dt_exp=dt_exp, acs=acs, acs_r=acs_r,
                ea=jnp.exp(acs), ds=jnp.exp(atot - acs), dec=jnp.exp(atot))


def _stack4(x):
    return jnp.concatenate([x] * HEADS_PER_GROUP, axis=0)


def _ssd_decay_stacks(q):
    hpg = HEADS_PER_GROUP
    a_col = jnp.concatenate([q["acs"][:, r * HEAD_DIM:r * HEAD_DIM + 1] for r in range(hpg)], axis=0)
    a_row = jnp.concatenate([jnp.broadcast_to(q["acs_r"][r:r + 1, :], (CHUNK, CHUNK)) for r in range(hpg)], axis=0)
    ri = lax.broadcasted_iota(jnp.int32, (hpg * CHUNK, CHUNK), 0) % CHUNK
    ci = lax.broadcasted_iota(jnp.int32, (hpg * CHUNK, CHUNK), 1)
    diff = a_col - a_row
    lm = jnp.exp(jnp.where(ri >= ci, diff, -jnp.inf))
    lt = jnp.exp(jnp.where(ri <= ci, -diff, -jnp.inf))
    return lm, lt


def _pick_heads(stacked, hm):
    out = jnp.where(hm[0], stacked[:CHUNK], 0.0)
    for r in range(1, HEADS_PER_GROUP):
        out = out + jnp.where(hm[r], stacked[r * CHUNK:(r + 1) * CHUNK], 0.0)
    return out


def _mask_heads(x, hm):
    return jnp.concatenate([jnp.where(hm[r], x, 0.0) for r in range(HEADS_PER_GROUP)], axis=0)


SSD_GP_FWD, SSD_GP_BWD = 8, 4


def _ssd_specs(cidx, gp):
    hpg = HEADS_PER_GROUP
    return [
        pl.BlockSpec((CHUNK, gp * GROUP_W), lambda g, c: (cidx(c), g)),
        pl.BlockSpec((CHUNK, gp * D_STATE), lambda g, c: (cidx(c), D_INNER // (gp * D_STATE) + g)),
        pl.BlockSpec((CHUNK, gp * D_STATE), lambda g, c: (cidx(c), (D_INNER + 1024) // (gp * D_STATE) + g)),
        pl.BlockSpec((gp, CHUNK, hpg), lambda g, c: (g, cidx(c), 0)),
        pl.BlockSpec((gp, CHUNK, hpg), lambda g, c: (g, cidx(c), 0)),
        pl.BlockSpec((gp, hpg, CHUNK), lambda g, c: (g, 0, cidx(c))),
    ]


def _ssd_fwd(xa, dec, carry=None):
    t = xa.shape[0]
    nc = t // CHUNK
    gp, gw, ds = SSD_GP_FWD, GROUP_W, D_STATE
    grid = (N_GROUPS // gp, nc)
    c_in, c_out, c_sem = carry.specs() if carry is not None else ([], [], [])
    nci, nco = len(c_in), len(c_out)

    def body(*refs):
        xs_ref, b_ref, c_ref, dt_ref, acs_ref, acsr_ref = refs[:6]
        y_ref, prev_ref = refs[6 + nci:8 + nci]
        st_ref = refs[8 + nci + nco]
        if carry is not None:
            begin, end = carry.at_edges(grid, refs[6:6 + nci], refs[8 + nci:8 + nci + nco], refs[-2:])
            begin()

        @pl.when(pl.program_id(1) == 0)
        def _():
            st_ref[...] = jnp.zeros_like(st_ref)

        for k in range(gp):
            q = _ssd_q(dt_ref[k], acs_ref[k], acsr_ref[k])
            xdt = xs_ref[:, k * gw:(k + 1) * gw] * q["dt_exp"]
            bm = b_ref[:, k * ds:(k + 1) * ds].astype(BF16)
            cm = c_ref[:, k * ds:(k + 1) * ds].astype(BF16)
            cb = _dot(cm, bm, _NT)
            st = st_ref[k]
            prev_ref[0, k] = st
            lm, _ = _ssd_decay_stacks(q)
            y_diag = _pick_heads(_dot((_stack4(cb) * lm).astype(BF16), xdt.astype(BF16)), q["hm"])
            y_ref[:, k * gw:(k + 1) * gw] = y_diag + _dot(cm, st.astype(BF16)) * q["ea"]
            st_ref[k] = q["dec"] * st + _dot(bm, (xdt * q["ds"]).astype(BF16), _TN)
        if carry is not None:
            end()

    res = pl.pallas_call(
        body, name="ssd_fwd", grid=grid,
        in_specs=_ssd_specs(lambda c: c, gp) + c_in,
        out_specs=[pl.BlockSpec((CHUNK, gp * gw), lambda g, c: (c, g)),
                   pl.BlockSpec((1, gp, ds, gw), lambda g, c: (c, g, 0, 0))] + c_out,
        out_shape=[jax.ShapeDtypeStruct((t, D_INNER), F32), jax.ShapeDtypeStruct((nc, N_GROUPS, ds, gw), F32)]
        + (carry.outs if carry is not None else []),
        scratch_shapes=[pltpu.VMEM((gp, ds, gw), F32)] + c_sem,
        input_output_aliases=carry.aliases(6, 2) if carry is not None else {},
        compiler_params=_cparams(("arbitrary", "arbitrary") if carry is not None else ("parallel", "arbitrary")),
    )(xa, xa, xa, dec["dt"], dec["acs"], dec["acs_rows"], *(carry.ins if carry is not None else []))
    return res[0], res[1], res[2:]


def _ssd_bwd(dy, dsk, xa, prev, dec, carry=None):
    t = xa.shape[0]
    nc = t // CHUNK
    gp, gw, dstate = SSD_GP_BWD, GROUP_W, D_STATE
    hpg = HEADS_PER_GROUP

    grid = (N_GROUPS // gp, nc)
    c_in, c_out, c_sem = carry.specs() if carry is not None else ([], [], [])
    nci, nco = len(c_in), len(c_out)

    def body(*refs):
        ins, c_ins = refs[:11], refs[11:11 + nci]
        outs = refs[11 + nci:17 + nci]
        c_outs = refs[17 + nci:17 + nci + nco]
        dst_ref = refs[17 + nci + nco]
        dbias_ref, dalog_ref = outs[4], outs[5]
        if carry is not None:
            begin, end = carry.at_edges(grid, c_ins, c_outs, refs[-2:])
            begin()

        @pl.when(pl.program_id(1) == 0)
        def _():
            dst_ref[...] = jnp.zeros_like(dst_ref)
            dbias_ref[...] = jnp.zeros_like(dbias_ref)
            dalog_ref[...] = jnp.zeros_like(dalog_ref)

        for k in range(gp):
            one_group(k, *ins, *outs, dst_ref)
        if carry is not None:
            end()

    def one_group(k, xs_ref, b_ref, c_ref, dt_ref, acs_ref, acsr_ref, sig_ref, al_ref, dy_ref, sk_ref, prev_ref,
                  dxs_ref, db_ref, dc_ref, ddt_ref, dbias_ref, dalog_ref, dst_ref):
        wide = slice(k * gw, (k + 1) * gw)
        narrow = slice(k * dstate, (k + 1) * dstate)
        q = _ssd_q(dt_ref[k], acs_ref[k], acsr_ref[k])
        a_r = -jnp.exp(al_ref[k])
        hm, ds, dec, dt_exp = q["hm"], q["ds"], q["dec"], q["dt_exp"]
        xs = xs_ref[:, wide]
        xdt = xs * dt_exp
        xdtb = xdt.astype(BF16)
        bm = b_ref[:, narrow].astype(BF16)
        cm = c_ref[:, narrow].astype(BF16)
        cb = _dot(cm, bm, _NT)
        bc = _dot(bm, cm, _NT)
        dyv = dy_ref[:, wide]
        dye = (dyv * q["ea"]).astype(BF16)
        pst = prev_ref[0, k]
        dst = dst_ref[k]
        dstb = dst.astype(BF16)
        dx_state = ds * _dot(bm, dstb)
        onehot = [(lax.broadcasted_iota(jnp.int32, (1, hpg), 1) == r).astype(F32) for r in range(hpg)]
        lm, lt = _ssd_decay_stacks(q)
        dyb = dyv.astype(BF16)
        gl = _dot(_mask_heads(dyv, hm).astype(BF16), xdtb, _NT) * lm
        glt = _dot(_mask_heads(xdt, hm).astype(BF16), dyb, _NT) * lt
        bc4 = _stack4(bc)
        dxdt = dx_state + _pick_heads(_dot((bc4 * lt).astype(BF16), dyb), hm)
        wd = jnp.sum(gl * _stack4(cb), axis=1, keepdims=True) - jnp.sum(glt * bc4, axis=1, keepdims=True)
        dcb = gl[:CHUNK]
        dcbt = glt[:CHUNK]
        qa = wd[:CHUNK] * onehot[0]
        for r in range(1, hpg):
            dcb = dcb + gl[r * CHUNK:(r + 1) * CHUNK]
            dcbt = dcbt + glt[r * CHUNK:(r + 1) * CHUNK]
            qa = qa + wd[r * CHUNK:(r + 1) * CHUNK] * onehot[r]
        pstb = pst.astype(BF16)
        dc_ref[:, narrow] = _dot(dcb.astype(BF16), bm) + _dot(dye, pstb, _NT)
        db_ref[:, narrow] = _dot(dcbt.astype(BF16), cm) + _dot((xdt * ds).astype(BF16), dstb, _NT)
        dst_ref[k] = dec * dst + _dot(cm, dye, _TN)
        dxs_ref[:, wide] = dxdt * dt_exp + dyv * sk_ref[:, wide]

        t2 = xdt * dx_state
        t1 = dyv * (_dot(cm, pstb) * q["ea"]) - t2
        t4 = dxdt * xs
        last_row = jnp.sum(t2, axis=0, keepdims=True) + dec * jnp.sum(dst * pst, axis=0, keepdims=True)
        xd = jnp.zeros((CHUNK, hpg), F32)
        dal = jnp.zeros((1, hpg), F32)
        for r in range(hpg):
            qa = qa + jnp.sum(jnp.where(hm[r], t1, 0.0), axis=1, keepdims=True) * onehot[r]
            xd = xd + jnp.sum(jnp.where(hm[r], t4, 0.0), axis=1, keepdims=True) * onehot[r]
            dal = dal + jnp.sum(jnp.where(hm[r], last_row, 0.0), axis=1, keepdims=True) * onehot[r]
        rc = _cumsum(qa, 0, reverse=True) + dal
        rowid = (nc - 1 - pl.program_id(1)) * CHUNK + lax.broadcasted_iota(jnp.int32, (CHUNK, 1), 0)
        ddt_raw = jnp.where(rowid >= N_PAD, (rc * a_r + xd) * sig_ref[k], 0.0)
        ddt_ref[k] = ddt_raw
        dbias_ref[k] += jnp.sum(ddt_raw, axis=0, keepdims=True)
        dalog_ref[k] += jnp.sum(rc * q["dtc"], axis=0, keepdims=True) * a_r

    rev = lambda c: nc - 1 - c
    blk = pl.BlockSpec((CHUNK, gp * gw), lambda g, c: (rev(c), g))
    nblk = pl.BlockSpec((CHUNK, gp * dstate), lambda g, c: (rev(c), g))
    cols = pl.BlockSpec((gp, CHUNK, hpg), lambda g, c: (g, rev(c), 0))
    small = pl.BlockSpec((gp, 1, hpg), lambda g, c: (g, 0, 0))
    res = pl.pallas_call(
        body, name="ssd_bwd", grid=grid,
        in_specs=_ssd_specs(rev, gp) + [cols, small, blk, pl.BlockSpec((1, gp * gw), lambda g, c: (0, g)),
                                    pl.BlockSpec((1, gp, dstate, gw), lambda g, c: (rev(c), g, 0, 0))]
        + c_in,
        out_specs=[blk, nblk, nblk, cols, small, small] + c_out,
        out_shape=[jax.ShapeDtypeStruct((t, D_INNER), F32), jax.ShapeDtypeStruct((t, N_GROUPS * dstate), F32),
                   jax.ShapeDtypeStruct((t, N_GROUPS * dstate), F32), jax.ShapeDtypeStruct((N_GROUPS, t, hpg), F32),
                   jax.ShapeDtypeStruct((N_GROUPS, 1, hpg), F32), jax.ShapeDtypeStruct((N_GROUPS, 1, hpg), F32)]
        + (carry.outs if carry is not None else []),
        scratch_shapes=[pltpu.VMEM((gp, dstate, gw), F32)] + c_sem,
        compiler_params=_cparams(("arbitrary", "arbitrary") if carry is not None else ("parallel", "arbitrary")),
    )(xa, xa, xa, dec["dt"], dec["acs"], dec["acs_rows"], dec["sig"], dec["a_log"], dy, dsk.reshape(1, D_INNER), prev,
      *(carry.ins if carry is not None else []))
    dxs, db, dc, ddt, dbias, dalog = res[:6]
    ddt_raw = ddt.transpose(1, 0, 2).reshape(t, N_HEADS)
    return dxs, db, dc, ddt_raw, dbias.reshape(N_HEADS), dalog.reshape(N_HEADS), res[6:]


def _ssd_post_fwd(y, xa, proj, dsk, nw):
    t = y.shape[0]
    di = D_INNER
    tm = _tile(t, (384, 128))

    def body(y_ref, xs_ref, z_ref, dsk_ref, nw_ref, o_ref):
        z = z_ref[...]
        yz = (y_ref[...] + xs_ref[...] * dsk_ref[...]) * (z * _sigmoid(z))
        nwv = nw_ref[...]
        for g in range(N_GROUPS):
            sl = slice(g * GROUP_W, (g + 1) * GROUP_W)
            v = yz[:, sl]
            rg = lax.rsqrt(jnp.mean(v * v, axis=-1, keepdims=True) + EPS)
            o_ref[:, sl] = (v * rg * nwv[:, sl]).astype(BF16)

    row = pl.BlockSpec((tm, di), lambda i: (i, 0))
    vec = pl.BlockSpec((1, di), lambda i: (0, 0))
    return pl.pallas_call(
        body, name="ssd_post_fwd", grid=(t // tm,),
        in_specs=[row, row, row, vec, vec], out_specs=row, out_shape=jax.ShapeDtypeStruct((t, di), BF16),
        compiler_params=_cparams(("parallel",)),
    )(y, xa, proj, dsk.reshape(1, di), nw.reshape(1, di))


def _ssd_post_bwd(dyn, y, xa, proj, dsk, nw, dproj):
    t = y.shape[0]
    di = D_INNER
    tm = CHUNK

    def body(dyn_ref, y_ref, xs_ref, z_ref, dsk_ref, nw_ref, _, dy_ref, dz_ref, dnw_ref, ddsk_ref):
        @pl.when(pl.program_id(0) == 0)
        def _():
            dnw_ref[...] = jnp.zeros_like(dnw_ref)
            ddsk_ref[...] = jnp.zeros_like(ddsk_ref)

        for g in range(N_GROUPS):
            sl = slice(g * GROUP_W, (g + 1) * GROUP_W)
            z = z_ref[:, sl]
            sig = _sigmoid(z)
            s = z * sig
            xs = xs_ref[:, sl]
            dskv = dsk_ref[:, sl]
            yt = y_ref[:, sl] + xs * dskv
            yz = yt * s
            rg = lax.rsqrt(jnp.mean(yz * yz, axis=-1, keepdims=True) + EPS)
            xhat = yz * rg
            dynv = dyn_ref[:, sl]
            gg = dynv * nw_ref[:, sl]
            dnw_ref[:, sl] += jnp.sum(dynv * xhat, axis=0, keepdims=True)
            dyz = rg * (gg - xhat * jnp.mean(gg * xhat, axis=-1, keepdims=True))
            dyt = dyz * s
            dy_ref[:, sl] = dyt
            dz_ref[:, sl] = (dyz * yt * (sig * (1.0 + z * (1.0 - sig)))).astype(BF16)
            ddsk_ref[:, sl] += jnp.sum(dyt * xs, axis=0, keepdims=True)

    row = pl.BlockSpec((tm, di), lambda i: (i, 0))
    vec = pl.BlockSpec((1, di), lambda i: (0, 0))
    dy, dz, dnw, ddsk = pl.pallas_call(
        body, name="ssd_post_bwd", grid=(t // tm,),
        in_specs=[row, row, row, row, vec, vec, pl.BlockSpec(memory_space=pl.ANY)], out_specs=[row, row, vec, vec],
        out_shape=[jax.ShapeDtypeStruct((t, di), F32), jax.ShapeDtypeStruct(dproj.shape, BF16),
                   jax.ShapeDtypeStruct((1, di), F32), jax.ShapeDtypeStruct((1, di), F32)],
        input_output_aliases={6: 1},
        compiler_params=_cparams(("arbitrary",)),
    )(dyn, y, xa, proj, dsk.reshape(1, di), nw.reshape(1, di), dproj)
    return dy, dz, dnw.reshape(di), ddsk.reshape(N_HEADS, HEAD_DIM).sum(axis=1)


SHARD_COLS = IN_COLS // N_CHIPS


def _split_w_in(w_in_sh):
    nl = w_in_sh.shape[0]
    w = w_in_sh.transpose(0, 2, 1, 3).reshape(nl, D_MODEL, IN_COLS)
    main = jnp.concatenate([w[..., OFF_Z:OFF_XBC], w[..., OFF_GATE:], w[..., :OFF_Z], w[..., OFF_XBC:OFF_DT]], axis=-1)
    dt = jnp.pad(w[..., OFF_DT:OFF_GATE], ((0, 0), (0, 0), (0, DT_PAD - N_HEADS)))
    return main, dt


def _merge_dw_in(dmain, ddt):
    full = jnp.concatenate([dmain[:, PPOOL:PXBC], dmain[:, PZ:PGATE], dmain[:, PXBC:], ddt[:, :N_HEADS], dmain[:, PGATE:PPOOL]],
                           axis=1)
    return full.reshape(D_MODEL, N_CHIPS, SHARD_COLS).transpose(1, 0, 2)


def _relu2_epilogue(acc):
    hid = jnp.maximum(acc, 0.0)
    return acc, hid * hid


def _relu2_bwd_epilogue(acc, pre):
    return (acc * 2.0 * jnp.maximum(pre, 0.0),)


def _add_epilogue(acc, res):
    return (acc + res,)


def _w_spec(rows, cols, fn):
    return pl.BlockSpec((None, None, rows, cols), fn)


def _with_carry(stage, call):
    carry = stage[0]() if stage is not None else None
    res = call(carry)
    if carry is None:
        return res
    outs, got = res
    stage[1](got)
    return outs[0] if len(outs) == 1 else outs


def _layer_fwd(h, w, big, layer, stages=None):
    t = h.shape[0]
    d = D_MODEL
    tm = _tile(t, (1408, 384, 128))
    nt = t // tm
    row = _bs((tm, d), lambda i, j, k: (i, 0))
    stages = stages or {}
    s = {"h": h}
    u = _norm_fwd(h, w["mix_norm_w"], "norm_mix")
    proj = _with_carry(stages.get("a_ici"), lambda carry: _mm(
        f"mm_proj_{layer}", _NN, (nt, PCOLS // d, 1), u, row,
        big["w_in_main"], _bs((None, d, d), lambda i, j, k: (0, 0, j)),
        [((t, PCOLS), F32, _bs((tm, d), lambda i, j, k: (i, j)))], carry=carry))
    dt_raw = _mm(f"mm_dt_{layer}", _NN, (nt, 1, 1), u, row,
                 big["w_in_dt"], _bs((None, d, DT_PAD), lambda i, j, k: (0, 0, 0)),
                 [((t, DT_PAD), F32, _bs((tm, DT_PAD), lambda i, j, k: (i, 0)))])
    pooled, yg, ypm, y_pool = _pool_fwd(proj, big["pool_w_group"], w["pool_scale"], big["w_pool_up"], layer)
    xa, conv_pre = _conv_fwd(proj, w["conv_w"], w["conv_b"])
    dec = _ssd_decay(dt_raw, w["dt_bias"], w["a_log"])
    stage = stages.get("b_ici")
    carry = stage[0]() if stage is not None else None
    y, prev, got = _ssd_fwd(xa, dec, carry)
    if carry is not None:
        stage[1](got)
    yn = _ssd_post_fwd(y, xa, proj, w["d_skip_exp"], w["ssd_norm_w"])
    rs = D_INNER // N_CHIPS
    y_ssd = _with_carry(stages.get("a_d2d"), lambda carry: _mm(
        f"mm_ssd_out_{layer}", _NN, (nt, 1, N_CHIPS), yn, _bs((tm, rs), lambda i, j, k: (i, k)),
        big["w_ssd_out"], _w_spec(rs, d, lambda i, j, k: (0, k, 0, 0)), [((t, d), F32, row)], carry=carry))
    merged = _gate_fwd(proj, w["b_gate"], y_pool, y_ssd)
    ro = d // N_CHIPS
    h1 = _mm(f"mm_o_{layer}", _NN, (nt, 1, N_CHIPS), merged, _bs((tm, ro), lambda i, j, k: (i, k)),
             big["w_o"], _w_spec(ro, d, lambda i, j, k: (0, k, 0, 0)), [((t, d), F32, row)],
             extras=[(h, row)], epilogue=_add_epilogue)
    v = _norm_fwd(h1, w["mlp_norm_w"], "norm_mlp")
    tile = _bs((tm, d), lambda i, j, k: (i, j))
    pre, act = _with_carry(stages.get("b_d2d"), lambda carry: _mm(
        f"mm_ff1_{layer}", _NN, (nt, N_CHIPS, 1), v, row,
        big["w_ff1"], _w_spec(d, d, lambda i, j, k: (0, j, 0, 0)),
        [((t, D_FF), F32, tile), ((t, D_FF), BF16, tile)], epilogue=_relu2_epilogue, carry=carry))
    h2 = _mm(f"mm_ff2_{layer}", _NN, (nt, 1, N_CHIPS), act, _bs((tm, d), lambda i, j, k: (i, k)),
             big["w_ff2"], _w_spec(d, d, lambda i, j, k: (0, k, 0, 0)), [((t, d), F32, row)],
             extras=[(h1, row)], epilogue=_add_epilogue)
    s.update(u=u, proj=proj, dec=dec, conv_pre=conv_pre, pooled=pooled, yg=yg, ypm=ypm, y_pool=y_pool, xa=xa, y=y, prev=prev, yn=yn,
             y_ssd=y_ssd, merged=merged, h1=h1, v=v, pre=pre, act=act)
    return h2, s


def _dw(name, layer, a, b, shard_shape, by, tk, carry=None):
    t = a.shape[0]
    nk = t // tk
    rows, cols = shard_shape
    if by == "rows":
        grid = (N_CHIPS, 1, nk)
        a_spec = _bs((tk, rows), lambda i, j, k: (k, i))
        b_spec = _bs((tk, cols), lambda i, j, k: (k, 0))
        o_spec = _bs((None, rows, cols), lambda i, j, k: (i, 0, 0))
    else:
        grid = (1, N_CHIPS, nk)
        a_spec = _bs((tk, rows), lambda i, j, k: (k, 0))
        b_spec = _bs((tk, cols), lambda i, j, k: (k, j))
        o_spec = _bs((None, rows, cols), lambda i, j, k: (j, 0, 0))
    return _mm(f"{name}_{layer}", _TN, grid, a, a_spec, b, b_spec, [((N_CHIPS, rows, cols), F32, o_spec)], carry=carry)


EARLY = ("w_ff2", "w_ff1", "w_o", "w_ssd_out")


def _layer_bwd(dh, s, w, big, layer, red=None, last=False):
    t = dh.shape[0]
    d = D_MODEL
    tm = _tile(t, (1408, 384, 128))
    tk = _tile(t, (1408, 384, 128))
    nt = t // tm
    row = _bs((tm, d), lambda i, j, k: (i, 0))
    tile = _bs((tm, d), lambda i, j, k: (i, j))
    g, gb = {}, {}
    carry = red.swap_carry() if red is not None else None
    dpre = _mm(f"mm_dact_{layer}", _NT, (nt, N_CHIPS, 1), dh, row,
               big["w_ff2"], _w_spec(d, d, lambda i, j, k: (0, j, 0, 0)), [((t, D_FF), BF16, tile)],
               extras=[(s["pre"], tile)], epilogue=_relu2_bwd_epilogue, carry=carry)
    if carry is not None:
        (dpre,), got = dpre
        red.after_swap(got)
    gb["w_ff2"] = _dw("mm_dw_ff2", layer, s["act"], dh, (d, d), "rows", tk)
    dv = _mm(f"mm_dv_{layer}", _NT, (nt, 1, N_CHIPS), dpre, _bs((tm, d), lambda i, j, k: (i, k)),
             big["w_ff1"], _w_spec(d, d, lambda i, j, k: (0, k, 0, 0)), [((t, d), F32, row)])
    gb["w_ff1"] = _dw("mm_dw_ff1", layer, s["v"], dpre, (d, d), "cols", tk)
    dh1, g["mlp_norm_w"] = _norm_bwd(dh, dv, s["h1"], w["mlp_norm_w"], "norm_mlp_bwd")
    ro = d // N_CHIPS
    dmerged = _mm(f"mm_dmerged_{layer}", _NT, (nt, N_CHIPS, 1), dh1, row,
                  big["w_o"], _w_spec(ro, d, lambda i, j, k: (0, j, 0, 0)),
                  [((t, d), F32, _bs((tm, ro), lambda i, j, k: (i, j)))])
    gb["w_o"] = _dw("mm_dw_o", layer, s["merged"], dh1, (ro, d), "rows", tk)
    dy_pool, dy_ssd, dproj, g["b_gate"] = _gate_bwd(dmerged, s["proj"], w["b_gate"], s["y_pool"], s["y_ssd"])
    rs = D_INNER // N_CHIPS
    dyn = _mm(f"mm_dyn_{layer}", _NT, (nt, N_CHIPS, 1), dy_ssd, row,
              big["w_ssd_out"], _w_spec(rs, d, lambda i, j, k: (0, j, 0, 0)),
              [((t, D_INNER), F32, _bs((tm, rs), lambda i, j, k: (i, j)))])
    gb["w_ssd_out"] = _dw("mm_dw_ssd_out", layer, s["yn"], dy_ssd, (rs, d), "rows", tk)
    dy, dproj, g["ssd_norm_w"], g["d_skip"] = _ssd_post_bwd(dyn, s["y"], s["xa"], s["proj"], w["d_skip_exp"],
                                                            w["ssd_norm_w"], dproj)
    carry = red.ici_carry() if red is not None else None
    dxs, db, dc, ddt_raw, g["dt_bias"], g["a_log"], got = _ssd_bwd(dy, w["d_skip_exp"], s["xa"], s["prev"], s["dec"], carry)
    if carry is not None:
        red.after_ici(got)
    own = red is not None and last
    if own:
        red.push(layer, {n: gb[n] for n in EARLY})
    dproj, dcw1, dcb1 = _conv_bwd(dxs, 0, s["proj"], s["conv_pre"], w["conv_w"], dproj)
    dproj, dcw2, dcb2 = _conv_bwd(db, 2, s["proj"], s["conv_pre"], w["conv_w"], dproj)
    dproj, dcw3, dcb3 = _conv_bwd(dc, 3, s["proj"], s["conv_pre"], w["conv_w"], dproj)
    g["conv_w"] = jnp.concatenate([dcw1, dcw2, dcw3], axis=1)
    g["conv_b"] = jnp.concatenate([dcb1, dcb2, dcb3])
    q, dyg, g["pool_scale"] = _pool_bwd_a(dy_pool, s["yg"], w["pool_scale"], big["pool_w_group"], big["w_pool_up"], layer)
    gb["w_pool_up"] = _with_carry((red.swap_carry, red.after_swap) if own else None, lambda carry: _dw(
        "mm_dw_pool_up", layer, s["ypm"], dy_pool, (ro, d), "rows", tk, carry))
    gb["pool_w_group"] = _pool_dwg(s["pooled"], dyg, layer).reshape(N_CHIPS, POOL_GDIM, POOL_GDIM)
    dproj = _pool_bwd_b(q, dproj)
    ddt = jnp.pad(ddt_raw.astype(BF16), ((0, 0), (0, DT_PAD - N_HEADS)))
    nk = PCOLS // d
    du = _with_carry((red.ici_carry, red.after_ici) if own else None, lambda carry: _mm(
        f"mm_du_{layer}", _NT, (nt, 1, nk), dproj, _bs((tm, d), lambda i, j, k: (i, k)),
        big["w_in_main"], _bs((None, d, d), lambda i, j, k: (0, 0, k)), [((t, d), F32, row)], carry=carry))
    du = _mm(f"mm_du_dt_{layer}", _NT, (nt, 1, 1), ddt, _bs((tm, DT_PAD), lambda i, j, k: (i, 0)),
             big["w_in_dt"], _bs((None, d, DT_PAD), lambda i, j, k: (0, 0, 0)), [((t, d), F32, row)],
             extras=[(du, row)], epilogue=_add_epilogue)
    ntk = t // tk
    u_spec = _bs((tk, d), lambda i, j, k: (k, 0))
    dmain = _mm(f"mm_dw_in_{layer}", _TN, (1, nk, ntk), s["u"], u_spec, dproj, _bs((tk, d), lambda i, j, k: (k, j)),
                [((d, PCOLS), F32, _bs((d, d), lambda i, j, k: (0, j)))])
    ddtw = _mm(f"mm_dw_dt_{layer}", _TN, (1, 1, ntk), s["u"], u_spec, ddt, _bs((tk, DT_PAD), lambda i, j, k: (k, 0)),
               [((d, DT_PAD), F32, _bs((d, DT_PAD), lambda i, j, k: (0, 0)))])
    gb["w_in"] = _merge_dw_in(dmain, ddtw)
    dh0, g["mix_norm_w"] = _norm_bwd(dh1, du, s["h"], w["mix_norm_w"], "norm_mix_bwd")
    if red is not None:
        red.push(layer, {n: gb[n] for n in BIG if not (own and n in EARLY)})
    return dh0, g, gb


BIG = ("w_in", "pool_w_group", "w_pool_up", "w_ssd_out", "w_o", "w_ff1", "w_ff2")
PER_LAYER = ("mix_norm_w", "w_in", "b_gate", "pool_w_group", "pool_scale", "w_pool_up", "conv_w", "conv_b", "dt_bias",
             "a_log", "d_skip", "ssd_norm_w", "w_ssd_out", "w_o", "mlp_norm_w", "w_ff1", "w_ff2")


SMALL_PER_LAYER = tuple(n for n in PER_LAYER if n not in BIG)


SHARD_SHAPE = {"w_in": (1024, SHARD_COLS), "pool_w_group": (4, WG_ROWS, POOL_GDIM), "w_pool_up": (WUP_ROWS, D_MODEL),
               "w_ssd_out": (D_INNER // N_CHIPS, D_MODEL), "w_o": (D_MODEL // N_CHIPS, D_MODEL),
               "w_ff1": (D_MODEL, D_FF // N_CHIPS), "w_ff2": (D_FF // N_CHIPS, D_MODEL)}


def _layer_view(mats):
    big = {n: mats[n].reshape((1, N_CHIPS) + SHARD_SHAPE[n]) for n in BIG if n != "w_in"}
    big["w_in_main"], big["w_in_dt"] = _split_w_in(mats["w_in"])
    return big


class _HeldWeights:
    def __init__(self, stacked):
        self.stacked = stacked

    def layer(self, i):
        return _layer_view({n: self.stacked[n][i:i + 1].reshape(1, N_CHIPS, -1, self.stacked[n].shape[-1]) for n in BIG})

    def stages(self, i):
        return None


def _local_step(x, tgt, weights, small, red=None):
    seq = x.shape[0]
    h = jnp.concatenate([jnp.zeros((N_PAD, D_MODEL), F32), small["meta_tokens"], x], axis=0)
    saved, ws, bigs = [], [], []
    for i in range(DEPTH):
        w = {n: small[n][i] for n in SMALL_PER_LAYER}
        w["d_skip_exp"] = jnp.repeat(w["d_skip"], HEAD_DIM)
        big = weights.layer(i)
        h, s = _layer_fwd(h, w, big, i, weights.stages(i))
        saved.append(s)
        ws.append(w)
        bigs.append(big)
    loss, dh, g_final = _loss_head(h, small["final_norm_w"], tgt)
    layer_g, layer_gb = [None] * DEPTH, [None] * DEPTH
    for i in reversed(range(DEPTH)):
        dh, layer_g[i], layer_gb[i] = _layer_bwd(dh, saved[i], ws[i], bigs[i], i, red, last=(i == 0))
    grads = {n: jnp.stack([layer_g[i][n] for i in range(DEPTH)]) for n in SMALL_PER_LAYER}
    grads["final_norm_w"] = g_final
    grads["meta_tokens"] = dh[N_PAD:ROW_X]
    return loss, dh[ROW_X:ROW_X + seq], layer_gb, grads


MESH = pl.DeviceIdType.MESH
LANES = 128
ANY = pl.BlockSpec(memory_space=pl.ANY)


def _place():
    x, y, c = lax.axis_index("x"), lax.axis_index("y"), lax.axis_index("c")
    chips = [(1 - x, y), (x, 1 - y), (1 - x, 1 - y)]
    return x, y, c, chips


def _remote(src, dst, send_sem, recv_sem, to):
    return pltpu.make_async_remote_copy(src_ref=src, dst_ref=dst, send_sem=send_sem, recv_sem=recv_sem,
                                        device_id=to, device_id_type=MESH)


class _WeightGatherer:
    GROUPS = {"a": ("w_in",), "b": tuple(n for n in BIG if n != "w_in")}

    def __init__(self, mine):
        self.mine = mine
        self.landing = {}
        self.ready = {}

    def _ici(self, layer, group):
        names = self.GROUPS[group]
        srcs = [self.mine[n] for n in names]
        outs = [jax.ShapeDtypeStruct((1, N_CHIPS) + s.shape[1:], BF16) for s in srcs]

        def make(ins, out, send_sems, recv_sems):
            x, y, c, chips = _place()
            me = 2 * x + y
            cps = []
            for p, ref in enumerate(ins):
                half = ref.shape[1] // 2
                rows = pl.ds(c * half, half)
                cps += [_remote(ref.at[layer, rows, :], out[p].at[0, me, rows, :], send_sems.at[3 * p + j],
                                recv_sems.at[3 * p + j], (*chip, c)) for j, chip in enumerate(chips)]
            return cps

        return _Carry(srcs, outs, 3 * len(names), make)

    def _d2d(self, layer, group):
        names = self.GROUPS[group]
        n = len(names)
        bufs = self.landing[(layer, group)]
        outs = [jax.ShapeDtypeStruct(b.shape, b.dtype) for b in bufs]

        def make(ins, out, send_sems, recv_sems):
            x, y, c, chips = _place()
            me = 2 * x + y
            sibling = (x, y, 1 - c)
            cps = []
            for p in range(n):
                half = out[p].shape[2] // 2
                for j, (cx, cy) in enumerate(chips):
                    blk = out[p].at[0, 2 * cx + cy, pl.ds(c * half, half), :]
                    cps.append(_remote(blk, blk, send_sems.at[4 * p + j], recv_sems.at[4 * p + j], sibling))
                cps.append(_remote(ins[n + p].at[layer], out[p].at[0, me], send_sems.at[4 * p + 3],
                                   recv_sems.at[4 * p + 3], sibling))
            return cps

        return _Carry(list(bufs) + [self.mine[m] for m in names], outs, 4 * n, make, alias={p: p for p in range(n)})

    def _landed(self, layer, group, bufs):
        self.landing[(layer, group)] = bufs

    def _done(self, layer, group, bufs):
        self.ready.setdefault(layer, {}).update(zip(self.GROUPS[group], bufs))

    def layer(self, i):
        if i == 0:
            for g in self.GROUPS:
                self._landed(0, g, _run_carry(self._ici(0, g), f"gather_ici_{g}_0"))
            for g in self.GROUPS:
                self._done(0, g, _run_carry(self._d2d(0, g), f"gather_d2d_{g}_0"))
        return _layer_view(self.ready[i])

    def stages(self, i):
        nxt = i + 1
        if nxt == DEPTH:
            return None
        st = {}
        for g in self.GROUPS:
            st[f"{g}_ici"] = (functools.partial(self._ici, nxt, g), functools.partial(self._landed, nxt, g))
            st[f"{g}_d2d"] = (functools.partial(self._d2d, nxt, g), functools.partial(self._done, nxt, g))
        return st


def _exchange_small(v, reduce, name):
    rows_per = v.shape[0]
    vm = pl.BlockSpec(memory_space=pltpu.VMEM)

    def body(v_ref, out_ref, *scratch):
        if reduce:
            land_ref, send_sems, recv_sems, local_sem = scratch
        else:
            land_ref = out_ref
            send_sems, recv_sems, local_sem = scratch
        x, y, c, chips = _place()
        me, sibling = (x, y, c), (x, y, 1 - c)

        def rows(px, py, pc):
            return land_ref.at[4 * px + 2 * py + pc]

        def copy(k, block, to, src=None):
            return _remote(rows(*block) if src is None else src, rows(*block), send_sems.at[k], recv_sems.at[k], to)

        mine = pltpu.make_async_copy(v_ref, rows(*me), local_sem)
        mine.start()
        first = [copy(0, me, sibling, src=v_ref)]
        first += [copy(1 + j, me, (*chip, c), src=v_ref) for j, chip in enumerate(chips)]
        for cp in first:
            cp.start()
        passed = [copy(4 + j, (*chip, c), sibling) for j, chip in enumerate(chips)]
        for j, chip in enumerate(chips):
            copy(1 + j, (*chip, c), me).wait_recv()
            passed[j].start()
        copy(0, sibling, me).wait_recv()
        for j, chip in enumerate(chips):
            copy(4 + j, (*chip, 1 - c), me).wait_recv()
        for cp in first + passed:
            cp.wait_send()
        mine.wait()
        if reduce:
            acc = land_ref[0]
            for d in range(1, 8):
                acc = acc + land_ref[d]
            out_ref[...] = acc

    sems = [pltpu.SemaphoreType.DMA((7,)), pltpu.SemaphoreType.DMA((7,)), pltpu.SemaphoreType.DMA]
    if reduce:
        out_shape = jax.ShapeDtypeStruct((rows_per, LANES), F32)
        scratch = [pltpu.VMEM((8, rows_per, LANES), F32)] + sems
    else:
        out_shape = jax.ShapeDtypeStruct((8, rows_per, LANES), F32)
        scratch = sems
    return pl.pallas_call(
        body, name=name, in_specs=[vm], out_specs=vm, out_shape=out_shape, scratch_shapes=scratch,
        compiler_params=pltpu.CompilerParams(vmem_limit_bytes=VMEM_LIMIT),
    )(v)


def _run_carry(carry, name):
    c_in, c_out, c_sem = carry.specs()
    n_in, n_out = len(c_in), len(c_out)

    def body(*refs):
        cps = carry.make(refs[:n_in], refs[n_in:n_in + n_out], *refs[n_in + n_out:])
        for cp in cps:
            cp.start()
        for cp in cps:
            cp.wait()

    return pl.pallas_call(body, name=name, in_specs=c_in, out_specs=c_out, out_shape=carry.outs,
                          scratch_shapes=c_sem, input_output_aliases=carry.aliases(0, 0))(*carry.ins)


def _row_tile(rows, last, itemsize=4, budget=2 * 1024 * 1024):
    return _tile(rows, tuple(t for t in (2048, 1024, 512, 256, 128, 64, 32, 16) if t * last * itemsize <= budget))


def _rs_add(g, got, core, name):
    _, half, last = got.shape
    tr = _row_tile(half, last)
    nb = half // tr

    def body(c_ref, g_ref, got_ref, o_ref):
        o_ref[...] = (g_ref[...] + got_ref[...]).astype(BF16)

    blk = pl.BlockSpec((None, tr, last), lambda s, i, c_ref: (s, i, 0))
    return pl.pallas_call(
        body, name=name,
        grid_spec=pltpu.PrefetchScalarGridSpec(
            num_scalar_prefetch=1, grid=(N_CHIPS, nb),
            in_specs=[pl.BlockSpec((None, tr, last), lambda s, i, c_ref: (s, c_ref[0] * nb + i, 0)), blk],
            out_specs=blk),
        out_shape=jax.ShapeDtypeStruct(got.shape, BF16),
        compiler_params=_cparams(("parallel", "parallel")),
    )(core.reshape(1).astype(jnp.int32), g, got)


def _rs_sum(own, got, acc, layer, chip, core, name):
    _, half, last = own.shape
    tr = _row_tile(half, last, budget=1024 * 1024)
    nb = half // tr
    ni = 0 if acc is None else 1

    def body(k_ref, c_ref, own_ref, got_ref, *rest):
        o_ref = rest[-1]
        tot = own_ref[...].astype(F32)
        for j in range(3):
            tot = tot + got_ref[j].astype(F32)
        o_ref[...] = tot

    return pl.pallas_call(
        body, name=name,
        grid_spec=pltpu.PrefetchScalarGridSpec(
            num_scalar_prefetch=2, grid=(nb,),
            in_specs=[pl.BlockSpec((None, tr, last), lambda i, k_ref, c_ref: (k_ref[0], i, 0)),
                      pl.BlockSpec((3, tr, last), lambda i, k_ref, c_ref: (0, i, 0))]
            + [pl.BlockSpec(memory_space=pl.ANY)] * ni,
            out_specs=pl.BlockSpec((None, tr, last), lambda i, k_ref, c_ref: (layer, c_ref[0] * nb + i, 0))),
        out_shape=jax.ShapeDtypeStruct((DEPTH, 2 * half, last), F32),
        input_output_aliases={4: 0} if ni else {},
        compiler_params=_cparams(("parallel",)),
    )(chip.reshape(1).astype(jnp.int32), core.reshape(1).astype(jnp.int32), own, got, *([acc] if ni else []))


def _rs_share(arrs):
    n = len(arrs)

    def body(*refs):
        out = refs[n:2 * n]
        send_sems, recv_sems = refs[2 * n:]
        x, y, c, _ = _place()
        sibling = (x, y, 1 - c)
        cps = []
        for p in range(n):
            half = out[p].shape[1] // 2
            mine = out[p].at[:, pl.ds(c * half, half), :]
            cps.append(_remote(mine, mine, send_sems.at[p], recv_sems.at[p], sibling))
        for cp in cps:
            cp.start()
        for p, cp in enumerate(cps):
            cp.wait_send()
            half = out[p].shape[1] // 2
            other = out[p].at[:, pl.ds((1 - c) * half, half), :]
            _remote(other, other, send_sems.at[p], recv_sems.at[p], sibling).wait_recv()

    return pl.pallas_call(
        body, name="rs_share", in_specs=[ANY] * n, out_specs=[ANY] * n,
        out_shape=[jax.ShapeDtypeStruct(a.shape, a.dtype) for a in arrs],
        input_output_aliases={p: p for p in range(n)},
        scratch_shapes=[pltpu.SemaphoreType.DMA((n,)), pltpu.SemaphoreType.DMA((n,))],
    )(*arrs)


class _GradReducer:
    def __init__(self, chip, core):
        self.chip, self.core = chip, core
        self.pending = None
        self.sums = None
        self.acc = {n: None for n in BIG}

    def push(self, layer, gb):
        assert self.pending is None
        self.pending = (layer, list(gb), list(gb.values()))

    def swap_carry(self):
        if self.pending is None:
            return None
        _, _, gs = self.pending
        outs = [jax.ShapeDtypeStruct((g.shape[0], g.shape[1] // 2, g.shape[2]), F32) for g in gs]

        def make(ins, out, send_sems, recv_sems):
            x, y, c, _ = _place()
            cps = []
            for p, ref in enumerate(ins):
                half = ref.shape[1] // 2
                cps.append(_remote(ref.at[:, pl.ds((1 - c) * half, half), :], out[p], send_sems.at[p], recv_sems.at[p],
                                   (x, y, 1 - c)))
            return cps

        return _Carry(gs, outs, len(gs), make)

    def after_swap(self, got):
        layer, names, gs = self.pending
        self.pending = None
        self.sums = (layer, names, [_rs_add(g, a, self.core, f"rs_add_{n}_{layer}") for n, g, a in zip(names, gs, got)])

    def ici_carry(self):
        if self.sums is None:
            return None
        _, _, ps = self.sums
        outs = [jax.ShapeDtypeStruct((3,) + p.shape[1:], BF16) for p in ps]

        def make(ins, out, send_sems, recv_sems):
            x, y, c, chips = _place()
            return [_remote(ins[p].at[2 * cx + cy], out[p].at[j], send_sems.at[3 * p + j], recv_sems.at[3 * p + j],
                            (cx, cy, c)) for p in range(len(ins)) for j, (cx, cy) in enumerate(chips)]

        return _Carry(ps, outs, 3 * len(ps), make)

    def after_ici(self, got):
        layer, names, ps = self.sums
        self.sums = None
        for n, p, a in zip(names, ps, got):
            self.acc[n] = _rs_sum(p, a, self.acc[n], layer, self.chip, self.core, f"rs_sum_{n}_{layer}")

    def finish(self):
        if self.pending is not None:
            self.after_swap(_run_carry(self.swap_carry(), "rs_swap_last"))
        if self.sums is not None:
            self.after_ici(_run_carry(self.ici_carry(), "rs_ici_last"))
        return dict(zip(BIG, _rs_share([self.acc[n] for n in BIG])))


def _adamw(w, g, m, v, name):
    shape = w.shape
    if len(shape) == 2:
        shape3 = (1,) + shape
    else:
        shape3 = (-1,) + shape[-2:]
    w2, g2, m2, v2 = (a.reshape(shape3) for a in (w, g, m, v))
    lead, rows, last = w2.shape
    tr = max([t for t in range(8, rows + 1, 8) if rows % t == 0 and t * last * 4 <= 2 * 1024 * 1024] or [rows])

    def body(w_ref, g_ref, m_ref, v_ref, d_ref, nm_ref, nv_ref):
        gv = g_ref[...]
        mn = ADAM_B1 * m_ref[...] + (1.0 - ADAM_B1) * gv
        vn = ADAM_B2 * v_ref[...] + (1.0 - ADAM_B2) * (gv * gv)
        m_hat = mn / (1.0 - ADAM_B1 ** ADAM_STEP)
        v_hat = vn / (1.0 - ADAM_B2 ** ADAM_STEP)
        d_ref[...] = -ADAM_LR * (m_hat / (jnp.sqrt(v_hat) + ADAM_EPS) + ADAM_WD * w_ref[...])
        nm_ref[...] = mn
        nv_ref[...] = vn

    blk = pl.BlockSpec((None, tr, last), lambda a, i: (a, i, 0))
    outs = pl.pallas_call(
        body, name=name, grid=(lead, rows // tr), in_specs=[blk] * 4, out_specs=[blk] * 3,
        out_shape=[jax.ShapeDtypeStruct((lead, rows, last), F32)] * 3,
        compiler_params=_cparams(("parallel", "parallel")),
    )(w2, g2, m2, v2)
    return tuple(o.reshape(shape) for o in outs)


SHARD_AXIS = {"conv_w": 2, "meta_tokens": 1}
SMALL_SHARDED = ("conv_w", "meta_tokens")
REPLICATED = ("mix_norm_w", "b_gate", "pool_scale", "conv_b", "dt_bias", "a_log", "d_skip", "ssd_norm_w", "mlp_norm_w",
              "final_norm_w")


def _flatten(arrs, dtype, row_mult):
    flat = jnp.concatenate([a.astype(dtype).reshape(-1) for a in arrs])
    n = flat.shape[0]
    rows = -(-n // (LANES * row_mult)) * row_mult
    return jnp.pad(flat, (0, rows * LANES - n)).reshape(rows, LANES)


def _unflatten(flat2d, shapes):
    flat = flat2d.reshape(-1)
    out, off = [], 0
    for sh in shapes:
        n = 1
        for d in sh:
            n *= d
        out.append(flat[off:off + n].reshape(sh))
        off += n
    return out


def kernel(x, meta_tokens, mix_norm_w, w_in, b_gate, pool_w_group, pool_scale, w_pool_up, conv_w, conv_b, dt_bias, a_log, d_skip, ssd_norm_w, w_ssd_out, w_o, mlp_norm_w, w_ff1, w_ff2, final_norm_w, loss_target, m_meta_tokens, m_mix_norm_w, m_w_in, m_b_gate, m_pool_w_group, m_pool_scale, m_w_pool_up, m_conv_w, m_conv_b, m_dt_bias, m_a_log, m_d_skip, m_ssd_norm_w, m_w_ssd_out, m_w_o, m_mlp_norm_w, m_w_ff1, m_w_ff2, m_final_norm_w, v_meta_tokens, v_mix_norm_w, v_w_in, v_b_gate, v_pool_w_group, v_pool_scale, v_w_pool_up, v_conv_w, v_conv_b, v_dt_bias, v_a_log, v_d_skip, v_ssd_norm_w, v_w_ssd_out, v_w_o, v_mlp_norm_w, v_w_ff1, v_w_ff2, v_final_norm_w):
    names = ("meta_tokens",) + PER_LAYER + ("final_norm_w",)
    par = dict(meta_tokens=meta_tokens, mix_norm_w=mix_norm_w, w_in=w_in, b_gate=b_gate, pool_w_group=pool_w_group,
               pool_scale=pool_scale, w_pool_up=w_pool_up, conv_w=conv_w, conv_b=conv_b, dt_bias=dt_bias, a_log=a_log,
               d_skip=d_skip, ssd_norm_w=ssd_norm_w, w_ssd_out=w_ssd_out, w_o=w_o, mlp_norm_w=mlp_norm_w, w_ff1=w_ff1,
               w_ff2=w_ff2, final_norm_w=final_norm_w)
    mom = dict(meta_tokens=m_meta_tokens, mix_norm_w=m_mix_norm_w, w_in=m_w_in, b_gate=m_b_gate, pool_w_group=m_pool_w_group,
               pool_scale=m_pool_scale, w_pool_up=m_w_pool_up, conv_w=m_conv_w, conv_b=m_conv_b, dt_bias=m_dt_bias,
               a_log=m_a_log, d_skip=m_d_skip, ssd_norm_w=m_ssd_norm_w, w_ssd_out=m_w_ssd_out, w_o=m_w_o,
               mlp_norm_w=m_mlp_norm_w, w_ff1=m_w_ff1, w_ff2=m_w_ff2, final_norm_w=m_final_norm_w)
    var = dict(meta_tokens=v_meta_tokens, mix_norm_w=v_mix_norm_w, w_in=v_w_in, b_gate=v_b_gate, pool_w_group=v_pool_w_group,
               pool_scale=v_pool_scale, w_pool_up=v_w_pool_up, conv_w=v_conv_w, conv_b=v_conv_b, dt_bias=v_dt_bias,
               a_log=v_a_log, d_skip=v_d_skip, ssd_norm_w=v_ssd_norm_w, w_ssd_out=v_w_ssd_out, w_o=v_w_o,
               mlp_norm_w=v_mlp_norm_w, w_ff1=v_w_ff1, w_ff2=v_w_ff2, final_norm_w=v_final_norm_w)
    chip = 2 * lax.axis_index("x") + lax.axis_index("y")
    core = lax.axis_index("c")

    weights = _WeightGatherer({n: par[n].astype(BF16).reshape(DEPTH, -1, par[n].shape[-1]) for n in BIG})
    small = {n: par[n] for n in REPLICATED}
    small_shapes = [par[n].shape for n in SMALL_SHARDED]
    got_small = _exchange_small(_flatten([par[n] for n in SMALL_SHARDED], F32, 8), False, "gather_small")
    pieces = [_unflatten(got_small[2 * k], small_shapes) for k in range(N_CHIPS)]
    for j, n in enumerate(SMALL_SHARDED):
        small[n] = jnp.concatenate([pieces[k][j] for k in range(N_CHIPS)], axis=SHARD_AXIS[n])

    reducer = _GradReducer(chip, core)
    loss, dx, _, grads = _local_step(x[0], loss_target[0], weights, small, reducer)

    small_names = REPLICATED + SMALL_SHARDED
    red = _exchange_small(_flatten([grads[n] for n in small_names] + [loss.reshape(1)], F32, 8), True, "allreduce_small")
    *parts, loss = _unflatten(red, [grads[n].shape for n in small_names] + [(1,)])
    loss = loss[0]
    gsum = dict(zip(small_names, parts))
    for n in SMALL_SHARDED:
        width = par[n].shape[SHARD_AXIS[n]]
        gsum[n] = lax.dynamic_slice_in_dim(gsum[n], chip * width, width, axis=SHARD_AXIS[n])

    gsum.update({n: g.reshape(par[n].shape) for n, g in reducer.finish().items()})

    delta, new_m, new_v = {}, {}, {}
    for n in BIG:
        if n == "w_in":
            tr = lambda a: jnp.swapaxes(a, 1, 2)
            outs = _adamw(tr(par[n]), tr(gsum[n]), tr(mom[n]), tr(var[n]), "adamw_" + n)
            delta[n], new_m[n], new_v[n] = (tr(o) for o in outs)
        else:
            delta[n], new_m[n], new_v[n] = _adamw(par[n], gsum[n], mom[n], var[n], "adamw_" + n)
    rest = [n for n in names if n not in BIG]
    shapes = [par[n].shape for n in rest]
    packed = [_flatten([d[n] for n in rest], F32, 8) for d in (par, gsum, mom, var)]
    for d, flat in zip((delta, new_m, new_v), _adamw(*packed, "adamw_small")):
        d.update(zip(rest, _unflatten(flat, shapes)))

    order = ("meta_tokens", "mix_norm_w", "w_in", "b_gate", "pool_w_group", "pool_scale", "w_pool_up", "conv_w", "conv_b",
             "dt_bias", "a_log", "d_skip", "ssd_norm_w", "w_ssd_out", "w_o", "mlp_norm_w", "w_ff1", "w_ff2", "final_norm_w")
    return (loss, dx[None], *[gsum[n] for n in order], *[delta[n] for n in order], *[new_m[n] for n in order],
            *[new_v[n] for n in order])
```

```python
import functools

import jax
import jax.numpy as jnp
from jax import lax
from jax.experimental import pallas as pl
from jax.experimental.pallas import tpu as pltpu

F32 = jnp.float32
BF16 = jnp.bfloat16

D_MODEL = 1024
DEPTH = 4
N_META = 16
N_PAD = 112
ROW_X = N_PAD + N_META
POOL_WINDOWS = (2, 4, 8, 16)
POOL_GDIM = 256
D_INNER = 2048
N_HEADS = 32
HEAD_DIM = 64
N_GROUPS = 8
HEADS_PER_GROUP = 4
GROUP_W = HEADS_PER_GROUP * HEAD_DIM
D_STATE = 128
CHUNK = 128
D_XBC = 4096
D_FF = 4096
EPS = 1e-5
OFF_Z, OFF_XBC, OFF_DT, OFF_GATE, IN_COLS = 1024, 3072, 7168, 7200, 9248
PZ, PGATE, PPOOL, PXBC, PCOLS = 0, 2048, 4096, 5120, 9216
DT_PAD = 128

ADAM_LR, ADAM_B1, ADAM_B2, ADAM_EPS, ADAM_WD, ADAM_STEP = 0.001, 0.9, 0.999, 1e-08, 0.01, 10

VMEM_LIMIT = 56 * 1024 * 1024

_NN = (((1,), (0,)), ((), ()))
_NT = (((1,), (1,)), ((), ()))
_TN = (((0,), (0,)), ((), ()))


def _dot(a, b, dn=_NN):
    return lax.dot_general(a, b, dn, preferred_element_type=F32)


def _cparams(sem):
    return pltpu.CompilerParams(dimension_semantics=sem, vmem_limit_bytes=VMEM_LIMIT)


def _tile(n, cands):
    for c in cands:
        if n % c == 0:
            return c
    raise ValueError(f"no tile for {n} in {cands}")


def _sigmoid(x):
    return 0.5 * jnp.tanh(0.5 * x) + 0.5


def _softplus(x):
    return jnp.maximum(x, 0.0) + jnp.log(1.0 + jnp.exp(-jnp.abs(x)))


def _bs(shape, fn):
    return pl.BlockSpec(shape, fn)


class _Carry:
    def __init__(self, ins, outs, n_sems, make, alias=None):
        self.ins, self.outs, self.n_sems, self.make = list(ins), list(outs), n_sems, make
        self.alias = dict(alias or {})

    def aliases(self, first_in, first_out):
        return {first_in + i: first_out + o for i, o in self.alias.items()}

    def specs(self):
        hbm = pl.BlockSpec(memory_space=pl.ANY)
        return [hbm] * len(self.ins), [hbm] * len(self.outs), [pltpu.SemaphoreType.DMA((self.n_sems,))] * 2

    def at_edges(self, grid, in_refs, out_refs, sems):
        ids = [pl.program_id(a) for a in range(len(grid))]
        first = functools.reduce(jnp.logical_and, [i == 0 for i in ids])
        last = functools.reduce(jnp.logical_and, [i == n - 1 for i, n in zip(ids, grid)])

        def begin():
            @pl.when(first)
            def _():
                for cp in self.make(in_refs, out_refs, *sems):
                    cp.start()

        def end():
            @pl.when(last)
            def _():
                for cp in self.make(in_refs, out_refs, *sems):
                    cp.wait()

        return begin, end


def _mm(name, dn, grid, a, a_spec, b, b_spec, outs, *, extras=(), epilogue=None, carry=None):
    nk = grid[2]
    ne, no = len(extras), len(outs)
    blk = tuple(d for d in outs[0][2].block_shape if d is not None)
    c_in, c_out, c_sem = carry.specs() if carry is not None else ([], [], [])
    nci, nco = len(c_in), len(c_out)

    def body(a_ref, b_ref, *rest):
        e_refs, o_refs = rest[:ne], rest[ne + nci:ne + nci + no]
        if carry is not None:
            begin, end = carry.at_edges(grid, rest[ne:ne + nci], rest[ne + nci + no:ne + nci + no + nco], rest[-2:])
            begin()
        if len(b_ref.shape) == 3:
            rows = b_ref.shape[1]
            p = _dot(a_ref[:, :rows].astype(BF16), b_ref[0].astype(BF16), dn)
            for s in range(1, b_ref.shape[0]):
                p = p + _dot(a_ref[:, s * rows:(s + 1) * rows].astype(BF16), b_ref[s].astype(BF16), dn)
        else:
            p = _dot(a_ref[...].astype(BF16), b_ref[...].astype(BF16), dn)

        def finish(acc):
            outs = epilogue(acc, *[e[...] for e in e_refs]) if epilogue is not None else (acc,)
            for o, v in zip(o_refs, outs):
                o[...] = v.astype(o.dtype)

        if nk == 1:
            finish(p)
        else:
            acc_ref = rest[ne + nci + no + nco]
            kk = pl.program_id(2)

            @pl.when(kk == 0)
            def _():
                acc_ref[...] = p

            @pl.when(kk > 0)
            def _():
                acc_ref[...] += p

            @pl.when(kk == nk - 1)
            def _():
                finish(acc_ref[...])
        if carry is not None:
            end()

    res = pl.pallas_call(
        body,
        name=name,
        grid=grid,
        in_specs=[a_spec, b_spec, *[s for _, s in extras]] + c_in,
        out_specs=[o[2] for o in outs] + c_out,
        out_shape=[jax.ShapeDtypeStruct(o[0], o[1]) for o in outs] + (carry.outs if carry is not None else []),
        scratch_shapes=([pltpu.VMEM(blk, F32)] if nk > 1 else []) + c_sem,
        input_output_aliases=carry.aliases(2 + ne, no) if carry is not None else {},
        compiler_params=_cparams(("arbitrary",) * 3 if carry is not None else ("parallel", "parallel", "arbitrary")),
    )(a, b, *[e for e, _ in extras], *(carry.ins if carry is not None else []))
    if carry is not None:
        return res[:no], res[no:]
    return res[0] if no == 1 else res


def _norm_fwd(h, w, name):
    t, d = h.shape
    tm = _tile(t, (1056, 384, 128))

    def body(h_ref, w_ref, u_ref):
        x = h_ref[...]
        r = lax.rsqrt(jnp.mean(x * x, axis=-1, keepdims=True) + EPS)
        u_ref[...] = (x * r * w_ref[...]).astype(BF16)

    return pl.pallas_call(
        body, name=name, grid=(t // tm,),
        in_specs=[pl.BlockSpec((tm, d), lambda i: (i, 0)), pl.BlockSpec((1, d), lambda i: (0, 0))],
        out_specs=pl.BlockSpec((tm, d), lambda i: (i, 0)),
        out_shape=jax.ShapeDtypeStruct((t, d), BF16),
        compiler_params=_cparams(("parallel",)),
    )(h, w.reshape(1, d))


def _norm_bwd(dres, du, h, w, name):
    t, d = h.shape
    tm = _tile(t, (528, 384, 128))

    def body(dres_ref, du_ref, h_ref, w_ref, dh_ref, dw_ref):
        x = h_ref[...]
        r = lax.rsqrt(jnp.mean(x * x, axis=-1, keepdims=True) + EPS)
        xhat = x * r
        du_v = du_ref[...]
        g = du_v * w_ref[...]
        dh_ref[...] = dres_ref[...] + r * (g - xhat * jnp.mean(g * xhat, axis=-1, keepdims=True))

        @pl.when(pl.program_id(0) == 0)
        def _():
            dw_ref[...] = jnp.zeros_like(dw_ref)

        dw_ref[...] += jnp.sum(du_v * xhat, axis=0, keepdims=True)

    row = pl.BlockSpec((tm, d), lambda i: (i, 0))
    vec = pl.BlockSpec((1, d), lambda i: (0, 0))
    dh, dw = pl.pallas_call(
        body, name=name, grid=(t // tm,),
        in_specs=[row, row, row, vec], out_specs=[row, vec],
        out_shape=[jax.ShapeDtypeStruct((t, d), F32), jax.ShapeDtypeStruct((1, d), F32)],
        compiler_params=_cparams(("arbitrary",)),
    )(dres, du, h, w.reshape(1, d))
    return dh, dw.reshape(d)


def _loss_head(h, w, tgt):
    t, d = h.shape
    tm = CHUNK
    nb = ROW_X // tm

    def body(h_ref, w_ref, t_ref, loss_ref, dh_ref, dw_ref):
        i = pl.program_id(0)
        x = h_ref[...]
        r = lax.rsqrt(jnp.mean(x * x, axis=-1, keepdims=True) + EPS)
        xhat = x * r
        wv = w_ref[...]
        live = i >= nb
        err = jnp.where(live, xhat * wv - t_ref[...], 0.0)
        dout = err * (1.0 / d)
        g = dout * wv
        dh_ref[...] = r * (g - xhat * jnp.mean(g * xhat, axis=-1, keepdims=True))

        @pl.when(i == 0)
        def _():
            loss_ref[...] = jnp.zeros_like(loss_ref)
            dw_ref[...] = jnp.zeros_like(dw_ref)

        loss_ref[...] += 0.5 * jnp.sum(jnp.sum(err * err, axis=-1, keepdims=True), axis=0, keepdims=True) * (1.0 / d)
        dw_ref[...] += jnp.sum(dout * xhat, axis=0, keepdims=True)

    row = pl.BlockSpec((tm, d), lambda i: (i, 0))
    vec = pl.BlockSpec((1, d), lambda i: (0, 0))
    loss, dh, dw = pl.pallas_call(
        body, name="loss_head", grid=(t // tm,),
        in_specs=[row, vec, pl.BlockSpec((tm, d), lambda i: (jnp.maximum(i - nb, 0), 0))],
        out_specs=[pl.BlockSpec((1, 1), lambda i: (0, 0)), row, vec],
        out_shape=[jax.ShapeDtypeStruct((1, 1), F32), jax.ShapeDtypeStruct((t, d), F32), jax.ShapeDtypeStruct((1, d), F32)],
        compiler_params=_cparams(("arbitrary",)),
    )(h, w.reshape(1, d), tgt)
    return loss[0, 0], dh, dw.reshape(d)


def _gate_fwd(proj, b_gate, y_pool, y_ssd):
    t = proj.shape[0]
    d = D_MODEL
    tm = _tile(t, (528, 384, 128))

    def body(gp_ref, gs_ref, bp_ref, bs_ref, yp_ref, ys_ref, o_ref):
        gp = _sigmoid(gp_ref[...] + bp_ref[...])
        gs = _sigmoid(gs_ref[...] + bs_ref[...])
        o_ref[...] = (gp * yp_ref[...] + gs * ys_ref[...]).astype(BF16)

    row = pl.BlockSpec((tm, d), lambda i: (i, 0))
    return pl.pallas_call(
        body, name="gate_fwd", grid=(t // tm,),
        in_specs=[pl.BlockSpec((tm, d), lambda i: (i, PGATE // d)), pl.BlockSpec((tm, d), lambda i: (i, PGATE // d + 1)),
                  pl.BlockSpec((1, d), lambda i: (0, 0)), pl.BlockSpec((1, d), lambda i: (0, 1)), row, row],
        out_specs=row, out_shape=jax.ShapeDtypeStruct((t, d), BF16),
        compiler_params=_cparams(("parallel",)),
    )(proj, proj, b_gate.reshape(1, 2 * d), b_gate.reshape(1, 2 * d), y_pool, y_ssd)


def _gate_bwd(dmerged, proj, b_gate, y_pool, y_ssd):
    t = proj.shape[0]
    d = D_MODEL
    tm = _tile(t, (384, 128))

    def body(dm_ref, gp_ref, gs_ref, bp_ref, bs_ref, yp_ref, ys_ref, dyp_ref, dys_ref, dg_ref, db_ref):
        dm = dm_ref[...]
        gp = _sigmoid(gp_ref[...] + bp_ref[...])
        gs = _sigmoid(gs_ref[...] + bs_ref[...])
        dyp_ref[...] = (dm * gp).astype(BF16)
        dys_ref[...] = (dm * gs).astype(BF16)
        dgp = dm * yp_ref[...] * gp * (1.0 - gp)
        dgs = dm * ys_ref[...] * gs * (1.0 - gs)
        dg_ref[:, :d] = dgp.astype(BF16)
        dg_ref[:, d:] = dgs.astype(BF16)

        @pl.when(pl.program_id(0) == 0)
        def _():
            db_ref[...] = jnp.zeros_like(db_ref)

        db_ref[:, :d] += jnp.sum(dgp, axis=0, keepdims=True)
        db_ref[:, d:] += jnp.sum(dgs, axis=0, keepdims=True)

    row = pl.BlockSpec((tm, d), lambda i: (i, 0))
    dyp, dys, dg, db = pl.pallas_call(
        body, name="gate_bwd", grid=(t // tm,),
        in_specs=[row, pl.BlockSpec((tm, d), lambda i: (i, PGATE // d)), pl.BlockSpec((tm, d), lambda i: (i, PGATE // d + 1)),
                  pl.BlockSpec((1, d), lambda i: (0, 0)), pl.BlockSpec((1, d), lambda i: (0, 1)), row, row],
        out_specs=[row, row, pl.BlockSpec((tm, 2 * d), lambda i: (i, PGATE // (2 * d))),
                   pl.BlockSpec((1, 2 * d), lambda i: (0, 0))],
        out_shape=[jax.ShapeDtypeStruct((t, d), BF16), jax.ShapeDtypeStruct((t, d), BF16),
                   jax.ShapeDtypeStruct((t, PCOLS), BF16), jax.ShapeDtypeStruct((1, 2 * d), F32)],
        compiler_params=_cparams(("arbitrary",)),
    )(dmerged, proj, proj, b_gate.reshape(1, 2 * d), b_gate.reshape(1, 2 * d), y_pool, y_ssd)
    return dyp, dys, dg, db.reshape(2 * d)


POOL_HALO = 16


def _pool_counts(row0, n, win):
    pos1 = row0 + lax.broadcasted_iota(jnp.int32, (n, 1), 0) - (N_PAD - 1)
    return jnp.clip(pos1, 1, win).astype(F32)


N_CHIPS = 4
WG_ROWS = POOL_GDIM // N_CHIPS
WUP_ROWS = D_MODEL // N_CHIPS


def _group_w(wg_ref, g):
    return jnp.concatenate([wg_ref[k, g] for k in range(N_CHIPS)], axis=0)


def _pool_w_specs(layer):
    return [pl.BlockSpec((None, N_CHIPS, len(POOL_WINDOWS), WG_ROWS, POOL_GDIM), lambda i: (0, 0, 0, 0, 0)),
            pl.BlockSpec((None, N_CHIPS, WUP_ROWS, D_MODEL), lambda i: (0, 0, 0, 0))]


def _pool_fwd(proj, wg, scale, wup, layer):
    t = proj.shape[0]
    d = D_MODEL
    tm = _tile(t, (384, 128))
    hb = tm // POOL_HALO

    def body(u_ref, halo_ref, sc_ref, wg_ref, wup_ref, pooled_ref, yg_ref, ypm_ref, yp_ref):
        i = pl.program_id(0)
        x = u_ref[...]
        halo = jnp.where(i > 0, halo_ref[...], 0.0)
        xc = jnp.concatenate([halo, x], axis=0)
        for g, win in enumerate(POOL_WINDOWS):
            sl = slice(g * POOL_GDIM, (g + 1) * POOL_GDIM)
            s = xc[:, sl]
            k = 1
            while k < win:
                s = s + pltpu.roll(s, k, axis=0)
                k *= 2
            pooled = s[POOL_HALO:] / _pool_counts(i * tm, tm, win) - x[:, sl]
            pb = pooled.astype(BF16)
            pooled_ref[:, sl] = pb
            yg_ref[:, sl] = _dot(pb, _group_w(wg_ref, g))
        ypm = (yg_ref[...] * sc_ref[...]).astype(BF16)
        ypm_ref[...] = ypm
        acc = _dot(ypm[:, :WUP_ROWS], wup_ref[0])
        for k in range(1, N_CHIPS):
            acc = acc + _dot(ypm[:, k * WUP_ROWS:(k + 1) * WUP_ROWS], wup_ref[k])
        yp_ref[...] = acc

    row = pl.BlockSpec((tm, d), lambda i: (i, 0))
    return pl.pallas_call(
        body, name=f"pool_fwd_{layer}", grid=(t // tm,),
        in_specs=[pl.BlockSpec((tm, d), lambda i: (i, PPOOL // d)),
                  pl.BlockSpec((POOL_HALO, d), lambda i: (jnp.maximum(i * hb - 1, 0), PPOOL // d)),
                  pl.BlockSpec((1, d), lambda i: (0, 0))] + _pool_w_specs(layer),
        out_specs=[row, row, row, row],
        out_shape=[jax.ShapeDtypeStruct((t, d), BF16), jax.ShapeDtypeStruct((t, d), F32),
                   jax.ShapeDtypeStruct((t, d), BF16), jax.ShapeDtypeStruct((t, d), F32)],
        compiler_params=_cparams(("parallel",)),
    )(proj, proj, scale.reshape(1, d), wg, wup)


def _pool_bwd_a(dy_pool, yg, scale, wg, wup, layer):
    t, d = yg.shape
    tm = _tile(t, (384, 128))

    def body(dy_ref, yg_ref, sc_ref, wg_ref, wup_ref, q_ref, dyg_ref, dsc_ref):
        dy = dy_ref[...]
        dypm = jnp.concatenate([_dot(dy, wup_ref[k], _NT) for k in range(N_CHIPS)], axis=1)

        @pl.when(pl.program_id(0) == 0)
        def _():
            dsc_ref[...] = jnp.zeros_like(dsc_ref)

        dsc_ref[...] += jnp.sum(dypm * yg_ref[...], axis=0, keepdims=True)
        dyg = (dypm * sc_ref[...]).astype(BF16)
        dyg_ref[...] = dyg
        for g in range(len(POOL_WINDOWS)):
            sl = slice(g * POOL_GDIM, (g + 1) * POOL_GDIM)
            q_ref[:, sl] = _dot(dyg[:, sl], _group_w(wg_ref, g), _NT)

    row = pl.BlockSpec((tm, d), lambda i: (i, 0))
    vec = pl.BlockSpec((1, d), lambda i: (0, 0))
    q, dyg, dsc = pl.pallas_call(
        body, name=f"pool_bwd_a_{layer}", grid=(t // tm,),
        in_specs=[row, row, vec] + _pool_w_specs(layer),
        out_specs=[row, row, vec],
        out_shape=[jax.ShapeDtypeStruct((t, d), F32), jax.ShapeDtypeStruct((t, d), BF16), jax.ShapeDtypeStruct((1, d), F32)],
        compiler_params=_cparams(("arbitrary",)),
    )(dy_pool, yg, scale.reshape(1, d), wg, wup)
    return q, dyg, dsc.reshape(d)


def _pool_bwd_b(q, dproj):
    t, d = q.shape
    tm = _tile(t, (384, 128))
    hb = tm // POOL_HALO
    nt = t // tm
    n = tm + POOL_HALO

    def body(q_ref, halo_ref, _, o_ref):
        i = pl.program_id(0)
        qv = q_ref[...]
        halo = jnp.where(i < nt - 1, halo_ref[...], 0.0)
        qc = jnp.concatenate([qv, halo], axis=0)
        for g, win in enumerate(POOL_WINDOWS):
            sl = slice(g * POOL_GDIM, (g + 1) * POOL_GDIM)
            s = qc[:, sl] / _pool_counts(i * tm, n, win)
            k = 1
            while k < win:
                s = s + pltpu.roll(s, n - k, axis=0)
                k *= 2
            o_ref[:, sl] = (s[:tm] - qv[:, sl]).astype(BF16)

    row = pl.BlockSpec((tm, d), lambda i: (i, 0))
    return pl.pallas_call(
        body, name="pool_bwd_b", grid=(nt,),
        in_specs=[row, pl.BlockSpec((POOL_HALO, d), lambda i: (jnp.minimum((i + 1) * hb, t // POOL_HALO - 1), 0)),
                  pl.BlockSpec(memory_space=pl.ANY)],
        out_specs=pl.BlockSpec((tm, d), lambda i: (i, PPOOL // d)), out_shape=jax.ShapeDtypeStruct(dproj.shape, BF16),
        input_output_aliases={2: 0},
        compiler_params=_cparams(("parallel",)),
    )(q, q, dproj)


def _pool_dwg(pooled, dyg, layer):
    t, d = pooled.shape
    tk = _tile(t, (1056, 384, 128))
    nk = t // tk
    gd = POOL_GDIM
    ng = d // gd

    def body(p_ref, g_ref, o_ref):
        @pl.when(pl.program_id(1) == 0)
        def _():
            o_ref[...] = jnp.zeros_like(o_ref)

        part = _dot(p_ref[...], g_ref[...], _TN)
        for k in range(N_CHIPS):
            o_ref[k] += part[k * WG_ROWS:(k + 1) * WG_ROWS]

    blk = pl.BlockSpec((tk, gd), lambda g, k: (k, g))
    return pl.pallas_call(
        body, name=f"pool_dwg_{layer}", grid=(ng, nk), in_specs=[blk, blk],
        out_specs=pl.BlockSpec((N_CHIPS, None, WG_ROWS, gd), lambda g, k: (0, g, 0, 0)),
        out_shape=jax.ShapeDtypeStruct((N_CHIPS, ng, WG_ROWS, gd), F32),
        compiler_params=_cparams(("parallel", "arbitrary")),
    )(pooled, dyg)


CONV_W = 4
CONV_HALO = 8
XBC_BLK = PXBC // 1024


def _conv_fwd(proj, conv_w, conv_b):
    t = proj.shape[0]
    cw = 1024
    tm = _tile(t, (1056, 384, 128))
    hb = tm // CONV_HALO

    def body(x_ref, halo_ref, w_ref, b_ref, o_ref, pre_ref):
        i = pl.program_id(1)
        x = x_ref[...]
        halo = jnp.where(i > 0, halo_ref[...], 0.0)
        xc = jnp.concatenate([halo, x], axis=0)
        w = w_ref[...]
        acc = b_ref[...] + x * w[CONV_W - 1:CONV_W, :]
        for k in range(CONV_W - 1):
            acc = acc + pltpu.roll(xc, CONV_W - 1 - k, axis=0)[CONV_HALO:] * w[k:k + 1, :]
        row = i * tm + lax.broadcasted_iota(jnp.int32, (tm, 1), 0)
        pre_ref[...] = acc
        o_ref[...] = jnp.where(row >= N_PAD, acc * _sigmoid(acc), 0.0)

    blk = pl.BlockSpec((tm, cw), lambda j, i: (i, j))
    return pl.pallas_call(
        body, name="conv_fwd", grid=(D_XBC // cw, t // tm),
        in_specs=[pl.BlockSpec((tm, cw), lambda j, i: (i, XBC_BLK + j)),
                  pl.BlockSpec((CONV_HALO, cw), lambda j, i: (jnp.maximum(i * hb - 1, 0), XBC_BLK + j)),
                  pl.BlockSpec((CONV_W, cw), lambda j, i: (0, j)), pl.BlockSpec((1, cw), lambda j, i: (0, j))],
        out_specs=[blk, blk],
        out_shape=[jax.ShapeDtypeStruct((t, D_XBC), F32), jax.ShapeDtypeStruct((t, D_XBC), F32)],
        compiler_params=_cparams(("parallel", "parallel")),
    )(proj, proj, conv_w, conv_b.reshape(1, D_XBC))


def _conv_bwd(dxa, coff, proj, pre, conv_w, dproj):
    t, ncols = dxa.shape
    cw = 1024
    tm = _tile(t, (528, 384, 128))
    hb = tm // CONV_HALO
    nt = t // tm
    n = tm + CONV_HALO

    def body(d_ref, dn_ref, p_ref, pn_ref, x_ref, w_ref, _, o_ref, dw_ref, db_ref):
        i = pl.program_id(1)
        last = i == nt - 1
        xc = jnp.concatenate([p_ref[...], pn_ref[...]], axis=0)
        df = jnp.concatenate([d_ref[...], dn_ref[...]], axis=0)
        w = w_ref[...]
        sig = _sigmoid(xc)
        row = i * tm + lax.broadcasted_iota(jnp.int32, (n, 1), 0)
        live = (row >= N_PAD) & ((row < (i + 1) * tm) | jnp.logical_not(last))
        dxc = jnp.where(live, df * (sig * (1.0 + xc * (1.0 - sig))), 0.0)
        ahead = [pltpu.roll(dxc, n - (CONV_W - 1 - k), axis=0)[:tm] for k in range(CONV_W - 1)] + [dxc[:tm]]
        acc = ahead[0] * w[0:1, :]
        for k in range(1, CONV_W):
            acc = acc + ahead[k] * w[k:k + 1, :]
        o_ref[...] = acc.astype(BF16)

        @pl.when(i == 0)
        def _():
            dw_ref[...] = jnp.zeros_like(dw_ref)
            db_ref[...] = jnp.zeros_like(db_ref)

        x = x_ref[...]
        db_ref[...] += jnp.sum(ahead[CONV_W - 1], axis=0, keepdims=True)
        for k in range(CONV_W):
            dw_ref[k:k + 1, :] += jnp.sum(ahead[k] * x, axis=0, keepdims=True)

    def pspec(rows, fn):
        return pl.BlockSpec((rows, cw), lambda j, i: (fn(i), coff + j))

    nxt = lambda i: jnp.minimum((i + 1) * hb, t // CONV_HALO - 1)
    dxbc, dw, db = pl.pallas_call(
        body, name=f"conv_bwd_{coff}", grid=(ncols // cw, nt),
        in_specs=[pl.BlockSpec((tm, cw), lambda j, i: (i, j)), pl.BlockSpec((CONV_HALO, cw), lambda j, i: (nxt(i), j)),
                  pspec(tm, lambda i: i), pspec(CONV_HALO, nxt),
                  pl.BlockSpec((tm, cw), lambda j, i: (i, XBC_BLK + coff + j)),
                  pl.BlockSpec((CONV_W, cw), lambda j, i: (0, coff + j)),
                  pl.BlockSpec(memory_space=pl.ANY)],
        out_specs=[pl.BlockSpec((tm, cw), lambda j, i: (i, XBC_BLK + coff + j)), pl.BlockSpec((CONV_W, cw), lambda j, i: (0, j)),
                   pl.BlockSpec((1, cw), lambda j, i: (0, j))],
        out_shape=[jax.ShapeDtypeStruct(dproj.shape, BF16), jax.ShapeDtypeStruct((CONV_W, ncols), F32),
                   jax.ShapeDtypeStruct((1, ncols), F32)],
        input_output_aliases={6: 0},
        compiler_params=_cparams(("parallel", "arbitrary")),
    )(dxa, dxa, pre, pre, proj, conv_w, dproj)
    return dxbc, dw, db.reshape(ncols)


def _cumsum(x, axis, reverse=False):
    n = x.shape[axis]
    idx = lax.broadcasted_iota(jnp.int32, x.shape, axis)
    k = 1
    while k < n:
        if reverse:
            x = x + jnp.where(idx < n - k, pltpu.roll(x, n - k, axis=axis), 0.0)
        else:
            x = x + jnp.where(idx >= k, pltpu.roll(x, k, axis=axis), 0.0)
        k *= 2
    return x


def _head_masks():
    lane = lax.broadcasted_iota(jnp.int32, (1, GROUP_W), 1)
    return [(lane >= r * HEAD_DIM) & (lane < (r + 1) * HEAD_DIM) for r in range(HEADS_PER_GROUP)]


def _expand_heads(cols, hm):
    out = jnp.where(hm[0], cols[:, 0:1], 0.0)
    for r in range(1, HEADS_PER_GROUP):
        out = out + jnp.where(hm[r], cols[:, r:r + 1], 0.0)
    return out


def _ssd_decay(dt_raw, dt_bias, a_log):
    t = dt_raw.shape[0]

    hpg = HEADS_PER_GROUP

    def body(raw_ref, b_ref, al_ref, dt_ref, acs_ref, sig_ref, rows_ref):
        raw = raw_ref[...] + b_ref[...]
        rowid = pl.program_id(0) * CHUNK + lax.broadcasted_iota(jnp.int32, (CHUNK, 1), 0)
        dt = jnp.where(rowid >= N_PAD, _softplus(raw), 0.0)
        acs = _cumsum(dt * -jnp.exp(al_ref[...]), 0)
        sig = _sigmoid(raw)
        acs_t = acs.T
        for g in range(N_GROUPS):
            heads = slice(g * hpg, (g + 1) * hpg)
            dt_ref[g] = dt[:, heads]
            acs_ref[g] = acs[:, heads]
            sig_ref[g] = sig[:, heads]
            rows_ref[g] = acs_t[heads, :]

    blk = pl.BlockSpec((CHUNK, DT_PAD), lambda c: (c, 0))
    vec = pl.BlockSpec((1, DT_PAD), lambda c: (0, 0))
    cols = pl.BlockSpec((N_GROUPS, CHUNK, hpg), lambda c: (0, c, 0))
    pad = lambda v: jnp.pad(v, (0, DT_PAD - N_HEADS)).reshape(1, DT_PAD)
    dt, acs, sig, rows = pl.pallas_call(
        body, name="ssd_decay", grid=(t // CHUNK,), in_specs=[blk, vec, vec],
        out_specs=[cols, cols, cols, pl.BlockSpec((N_GROUPS, hpg, CHUNK), lambda c: (0, 0, c))],
        out_shape=[jax.ShapeDtypeStruct((N_GROUPS, t, hpg), F32)] * 3 + [jax.ShapeDtypeStruct((N_GROUPS, hpg, t), F32)],
        compiler_params=_cparams(("parallel",)),
    )(dt_raw, pad(dt_bias), pad(a_log))
    return dict(dt=dt, acs=acs, acs_rows=rows, sig=sig, a_log=a_log.reshape(N_GROUPS, 1, hpg))


def _ssd_q(dtc, acs4, acs_r):
    hm = _head_masks()
    dt_exp = _expand_heads(dtc, hm)
    acs = _expand_heads(acs4, hm)
    atot = acs[CHUNK - 1:CHUNK, :]
    return dict(dtc=dtc, hm=hm, dt_exp=dt_exp, acs=acs, acs_r=acs_r,
                ea=jnp.exp(acs), ds=jnp.exp(atot - acs), dec=jnp.exp(atot))


def _stack4(x):
    return jnp.concatenate([x] * HEADS_PER_GROUP, axis=0)


def _ssd_decay_stacks(q):
    hpg = HEADS_PER_GROUP
    a_col = jnp.concatenate([q["acs"][:, r * HEAD_DIM:r * HEAD_DIM + 1] for r in range(hpg)], axis=0)
    a_row = jnp.concatenate([jnp.broadcast_to(q["acs_r"][r:r + 1, :], (CHUNK, CHUNK)) for r in range(hpg)], axis=0)
    ri = lax.broadcasted_iota(jnp.int32, (hpg * CHUNK, CHUNK), 0) % CHUNK
    ci = lax.broadcasted_iota(jnp.int32, (hpg * CHUNK, CHUNK), 1)
    diff = a_col - a_row
    lm = jnp.exp(jnp.where(ri >= ci, diff, -jnp.inf))
    lt = jnp.exp(jnp.where(ri <= ci, -diff, -jnp.inf))
    return lm, lt


def _pick_heads(stacked, hm):
    out = jnp.where(hm[0], stacked[:CHUNK], 0.0)
    for r in range(1, HEADS_PER_GROUP):
        out = out + jnp.where(hm[r], stacked[r * CHUNK:(r + 1) * CHUNK], 0.0)
    return out


def _mask_heads(x, hm):
    return jnp.concatenate([jnp.where(hm[r], x, 0.0) for r in range(HEADS_PER_GROUP)], axis=0)


SSD_GP_FWD, SSD_GP_BWD = 8, 4


def _ssd_specs(cidx, gp):
    hpg = HEADS_PER_GROUP
    return [
        pl.BlockSpec((CHUNK, gp * GROUP_W), lambda g, c: (cidx(c), g)),
        pl.BlockSpec((CHUNK, gp * D_STATE), lambda g, c: (cidx(c), D_INNER // (gp * D_STATE) + g)),
        pl.BlockSpec((CHUNK, gp * D_STATE), lambda g, c: (cidx(c), (D_INNER + 1024) // (gp * D_STATE) + g)),
        pl.BlockSpec((gp, CHUNK, hpg), lambda g, c: (g, cidx(c), 0)),
        pl.BlockSpec((gp, CHUNK, hpg), lambda g, c: (g, cidx(c), 0)),
        pl.BlockSpec((gp, hpg, CHUNK), lambda g, c: (g, 0, cidx(c))),
    ]


def _ssd_fwd(xa, dec, carry=None):
    t = xa.shape[0]
    nc = t // CHUNK
    gp, gw, ds = SSD_GP_FWD, GROUP_W, D_STATE
    grid = (N_GROUPS // gp, nc)
    c_in, c_out, c_sem = carry.specs() if carry is not None else ([], [], [])
    nci, nco = len(c_in), len(c_out)

    def body(*refs):
        xs_ref, b_ref, c_ref, dt_ref, acs_ref, acsr_ref = refs[:6]
        y_ref, prev_ref = refs[6 + nci:8 + nci]
        st_ref = refs[8 + nci + nco]
        if carry is not None:
            begin, end = carry.at_edges(grid, refs[6:6 + nci], refs[8 + nci:8 + nci + nco], refs[-2:])
            begin()

        @pl.when(pl.program_id(1) == 0)
        def _():
            st_ref[...] = jnp.zeros_like(st_ref)

        for k in range(gp):
            q = _ssd_q(dt_ref[k], acs_ref[k], acsr_ref[k])
            xdt = xs_ref[:, k * gw:(k + 1) * gw] * q["dt_exp"]
            bm = b_ref[:, k * ds:(k + 1) * ds].astype(BF16)
            cm = c_ref[:, k * ds:(k + 1) * ds].astype(BF16)
            cb = _dot(cm, bm, _NT)
            st = st_ref[k]
            prev_ref[0, k] = st
            lm, _ = _ssd_decay_stacks(q)
            y_diag = _pick_heads(_dot((_stack4(cb) * lm).astype(BF16), xdt.astype(BF16)), q["hm"])
            y_ref[:, k * gw:(k + 1) * gw] = y_diag + _dot(cm, st.astype(BF16)) * q["ea"]
            st_ref[k] = q["dec"] * st + _dot(bm, (xdt * q["ds"]).astype(BF16), _TN)
        if carry is not None:
            end()

    res = pl.pallas_call(
        body, name="ssd_fwd", grid=grid,
        in_specs=_ssd_specs(lambda c: c, gp) + c_in,
        out_specs=[pl.BlockSpec((CHUNK, gp * gw), lambda g, c: (c, g)),
                   pl.BlockSpec((1, gp, ds, gw), lambda g, c: (c, g, 0, 0))] + c_out,
        out_shape=[jax.ShapeDtypeStruct((t, D_INNER), F32), jax.ShapeDtypeStruct((nc, N_GROUPS, ds, gw), F32)]
        + (carry.outs if carry is not None else []),
        scratch_shapes=[pltpu.VMEM((gp, ds, gw), F32)] + c_sem,
        input_output_aliases=carry.aliases(6, 2) if carry is not None else {},
        compiler_params=_cparams(("arbitrary", "arbitrary") if carry is not None else ("parallel", "arbitrary")),
    )(xa, xa, xa, dec["dt"], dec["acs"], dec["acs_rows"], *(carry.ins if carry is not None else []))
    return res[0], res[1], res[2:]


def _ssd_bwd(dy, dsk, xa, prev, dec, carry=None):
    t = xa.shape[0]
    nc = t // CHUNK
    gp, gw, dstate = SSD_GP_BWD, GROUP_W, D_STATE
    hpg = HEADS_PER_GROUP

    grid = (N_GROUPS // gp, nc)
    c_in, c_out, c_sem = carry.specs() if carry is not None else ([], [], [])
    nci, nco = len(c_in), len(c_out)

    def body(*refs):
        ins, c_ins = refs[:11], refs[11:11 + nci]
        outs = refs[11 + nci:17 + nci]
        c_outs = refs[17 + nci:17 + nci + nco]
        dst_ref = refs[17 + nci + nco]
        dbias_ref, dalog_ref = outs[4], outs[5]
        if carry is not None:
            begin, end = carry.at_edges(grid, c_ins, c_outs, refs[-2:])
            begin()

        @pl.when(pl.program_id(1) == 0)
        def _():
            dst_ref[...] = jnp.zeros_like(dst_ref)
            dbias_ref[...] = jnp.zeros_like(dbias_ref)
            dalog_ref[...] = jnp.zeros_like(dalog_ref)

        for k in range(gp):
            one_group(k, *ins, *outs, dst_ref)
        if carry is not None:
            end()

    def one_group(k, xs_ref, b_ref, c_ref, dt_ref, acs_ref, acsr_ref, sig_ref, al_ref, dy_ref, sk_ref, prev_ref,
                  dxs_ref, db_ref, dc_ref, ddt_ref, dbias_ref, dalog_ref, dst_ref):
        wide = slice(k * gw, (k + 1) * gw)
        narrow = slice(k * dstate, (k + 1) * dstate)
        q = _ssd_q(dt_ref[k], acs_ref[k], acsr_ref[k])
        a_r = -jnp.exp(al_ref[k])
        hm, ds, dec, dt_exp = q["hm"], q["ds"], q["dec"], q["dt_exp"]
        xs = xs_ref[:, wide]
        xdt = xs * dt_exp
        xdtb = xdt.astype(BF16)
        bm = b_ref[:, narrow].astype(BF16)
        cm = c_ref[:, narrow].astype(BF16)
        cb = _dot(cm, bm, _NT)
        bc = _dot(bm, cm, _NT)
        dyv = dy_ref[:, wide]
        dye = (dyv * q["ea"]).astype(BF16)
        pst = prev_ref[0, k]
        dst = dst_ref[k]
        dstb = dst.astype(BF16)
        dx_state = ds * _dot(bm, dstb)
        onehot = [(lax.broadcasted_iota(jnp.int32, (1, hpg), 1) == r).astype(F32) for r in range(hpg)]
        lm, lt = _ssd_decay_stacks(q)
        dyb = dyv.astype(BF16)
        gl = _dot(_mask_heads(dyv, hm).astype(BF16), xdtb, _NT) * lm
        glt = _dot(_mask_heads(xdt, hm).astype(BF16), dyb, _NT) * lt
        bc4 = _stack4(bc)
        dxdt = dx_state + _pick_heads(_dot((bc4 * lt).astype(BF16), dyb), hm)
        wd = jnp.sum(gl * _stack4(cb), axis=1, keepdims=True) - jnp.sum(glt * bc4, axis=1, keepdims=True)
        dcb = gl[:CHUNK]
        dcbt = glt[:CHUNK]
        qa = wd[:CHUNK] * onehot[0]
        for r in range(1, hpg):
            dcb = dcb + gl[r * CHUNK:(r + 1) * CHUNK]
            dcbt = dcbt + glt[r * CHUNK:(r + 1) * CHUNK]
            qa = qa + wd[r * CHUNK:(r + 1) * CHUNK] * onehot[r]
        pstb = pst.astype(BF16)
        dc_ref[:, narrow] = _dot(dcb.astype(BF16), bm) + _dot(dye, pstb, _NT)
        db_ref[:, narrow] = _dot(dcbt.astype(BF16), cm) + _dot((xdt * ds).astype(BF16), dstb, _NT)
        dst_ref[k] = dec * dst + _dot(cm, dye, _TN)
        dxs_ref[:, wide] = dxdt * dt_exp + dyv * sk_ref[:, wide]

        t2 = xdt * dx_state
        t1 = dyv * (_dot(cm, pstb) * q["ea"]) - t2
        t4 = dxdt * xs
        last_row = jnp.sum(t2, axis=0, keepdims=True) + dec * jnp.sum(dst * pst, axis=0, keepdims=True)
        xd = jnp.zeros((CHUNK, hpg), F32)
        dal = jnp.zeros((1, hpg), F32)
        for r in range(hpg):
            qa = qa + jnp.sum(jnp.where(hm[r], t1, 0.0), axis=1, keepdims=True) * onehot[r]
            xd = xd + jnp.sum(jnp.where(hm[r], t4, 0.0), axis=1, keepdims=True) * onehot[r]
            dal = dal + jnp.sum(jnp.where(hm[r], last_row, 0.0), axis=1, keepdims=True) * onehot[r]
        rc = _cumsum(qa, 0, reverse=True) + dal
        rowid = (nc - 1 - pl.program_id(1)) * CHUNK + lax.broadcasted_iota(jnp.int32, (CHUNK, 1), 0)
        ddt_raw = jnp.where(rowid >= N_PAD, (rc * a_r + xd) * sig_ref[k], 0.0)
        ddt_ref[k] = ddt_raw
        dbias_ref[k] += jnp.sum(ddt_raw, axis=0, keepdims=True)
        dalog_ref[k] += jnp.sum(rc * q["dtc"], axis=0, keepdims=True) * a_r

    rev = lambda c: nc - 1 - c
    blk = pl.BlockSpec((CHUNK, gp * gw), lambda g, c: (rev(c), g))
    nblk = pl.BlockSpec((CHUNK, gp * dstate), lambda g, c: (rev(c), g))
    cols = pl.BlockSpec((gp, CHUNK, hpg), lambda g, c: (g, rev(c), 0))
    small = pl.BlockSpec((gp, 1, hpg), lambda g, c: (g, 0, 0))
    res = pl.pallas_call(
        body, name="ssd_bwd", grid=grid,
        in_specs=_ssd_specs(rev, gp) + [cols, small, blk, pl.BlockSpec((1, gp * gw), lambda g, c: (0, g)),
                                    pl.BlockSpec((1, gp, dstate, gw), lambda g, c: (rev(c), g, 0, 0))]
        + c_in,
        out_specs=[blk, nblk, nblk, cols, small, small] + c_out,
        out_shape=[jax.ShapeDtypeStruct((t, D_INNER), F32), jax.ShapeDtypeStruct((t, N_GROUPS * dstate), F32),
                   jax.ShapeDtypeStruct((t, N_GROUPS * dstate), F32), jax.ShapeDtypeStruct((N_GROUPS, t, hpg), F32),
                   jax.ShapeDtypeStruct((N_GROUPS, 1, hpg), F32), jax.ShapeDtypeStruct((N_GROUPS, 1, hpg), F32)]
        + (carry.outs if carry is not None else []),
        scratch_shapes=[pltpu.VMEM((gp, dstate, gw), F32)] + c_sem,
        compiler_params=_cparams(("arbitrary", "arbitrary") if carry is not None else ("parallel", "arbitrary")),
    )(xa, xa, xa, dec["dt"], dec["acs"], dec["acs_rows"], dec["sig"], dec["a_log"], dy, dsk.reshape(1, D_INNER), prev,
      *(carry.ins if carry is not None else []))
    dxs, db, dc, ddt, dbias, dalog = res[:6]
    ddt_raw = ddt.transpose(1, 0, 2).reshape(t, N_HEADS)
    return dxs, db, dc, ddt_raw, dbias.reshape(N_HEADS), dalog.reshape(N_HEADS), res[6:]


def _ssd_post_fwd(y, xa, proj, dsk, nw):
    t = y.shape[0]
    di = D_INNER
    tm = _tile(t, (384, 128))

    def body(y_ref, xs_ref, z_ref, dsk_ref, nw_ref, o_ref):
        z = z_ref[...]
        yz = (y_ref[...] + xs_ref[...] * dsk_ref[...]) * (z * _sigmoid(z))
        nwv = nw_ref[...]
        for g in range(N_GROUPS):
            sl = slice(g * GROUP_W, (g + 1) * GROUP_W)
            v = yz[:, sl]
            rg = lax.rsqrt(jnp.mean(v * v, axis=-1, keepdims=True) + EPS)
            o_ref[:, sl] = (v * rg * nwv[:, sl]).astype(BF16)

    row = pl.BlockSpec((tm, di), lambda i: (i, 0))
    vec = pl.BlockSpec((1, di), lambda i: (0, 0))
    return pl.pallas_call(
        body, name="ssd_post_fwd", grid=(t // tm,),
        in_specs=[row, row, row, vec, vec], out_specs=row, out_shape=jax.ShapeDtypeStruct((t, di), BF16),
        compiler_params=_cparams(("parallel",)),
    )(y, xa, proj, dsk.reshape(1, di), nw.reshape(1, di))


def _ssd_post_bwd(dyn, y, xa, proj, dsk, nw, dproj):
    t = y.shape[0]
    di = D_INNER
    tm = CHUNK

    def body(dyn_ref, y_ref, xs_ref, z_ref, dsk_ref, nw_ref, _, dy_ref, dz_ref, dnw_ref, ddsk_ref):
        @pl.when(pl.program_id(0) == 0)
        def _():
            dnw_ref[...] = jnp.zeros_like(dnw_ref)
            ddsk_ref[...] = jnp.zeros_like(ddsk_ref)

        for g in range(N_GROUPS):
            sl = slice(g * GROUP_W, (g + 1) * GROUP_W)
            z = z_ref[:, sl]
            sig = _sigmoid(z)
            s = z * sig
            xs = xs_ref[:, sl]
            dskv = dsk_ref[:, sl]
            yt = y_ref[:, sl] + xs * dskv
            yz = yt * s
            rg = lax.rsqrt(jnp.mean(yz * yz, axis=-1, keepdims=True) + EPS)
            xhat = yz * rg
            dynv = dyn_ref[:, sl]
            gg = dynv * nw_ref[:, sl]
            dnw_ref[:, sl] += jnp.sum(dynv * xhat, axis=0, keepdims=True)
            dyz = rg * (gg - xhat * jnp.mean(gg * xhat, axis=-1, keepdims=True))
            dyt = dyz * s
            dy_ref[:, sl] = dyt
            dz_ref[:, sl] = (dyz * yt * (sig * (1.0 + z * (1.0 - sig)))).astype(BF16)
            ddsk_ref[:, sl] += jnp.sum(dyt * xs, axis=0, keepdims=True)

    row = pl.BlockSpec((tm, di), lambda i: (i, 0))
    vec = pl.BlockSpec((1, di), lambda i: (0, 0))
    dy, dz, dnw, ddsk = pl.pallas_call(
        body, name="ssd_post_bwd", grid=(t // tm,),
        in_specs=[row, row, row, row, vec, vec, pl.BlockSpec(memory_space=pl.ANY)], out_specs=[row, row, vec, vec],
        out_shape=[jax.ShapeDtypeStruct((t, di), F32), jax.ShapeDtypeStruct(dproj.shape, BF16),
                   jax.ShapeDtypeStruct((1, di), F32), jax.ShapeDtypeStruct((1, di), F32)],
        input_output_aliases={6: 1},
        compiler_params=_cparams(("arbitrary",)),
    )(dyn, y, xa, proj, dsk.reshape(1, di), nw.reshape(1, di), dproj)
    return dy, dz, dnw.reshape(di), ddsk.reshape(N_HEADS, HEAD_DIM).sum(axis=1)


SHARD_COLS = IN_COLS // N_CHIPS


def _split_w_in(w_in_sh):
    nl = w_in_sh.shape[0]
    w = w_in_sh.transpose(0, 2, 1, 3).reshape(nl, D_MODEL, IN_COLS)
    main = jnp.concatenate([w[..., OFF_Z:OFF_XBC], w[..., OFF_GATE:], w[..., :OFF_Z], w[..., OFF_XBC:OFF_DT]], axis=-1)
    dt = jnp.pad(w[..., OFF_DT:OFF_GATE], ((0, 0), (0, 0), (0, DT_PAD - N_HEADS)))
    return main, dt


def _merge_dw_in(dmain, ddt):
    full = jnp.concatenate([dmain[:, PPOOL:PXBC], dmain[:, PZ:PGATE], dmain[:, PXBC:], ddt[:, :N_HEADS], dmain[:, PGATE:PPOOL]],
                           axis=1)
    return full.reshape(D_MODEL, N_CHIPS, SHARD_COLS).transpose(1, 0, 2)


def _relu2_epilogue(acc):
    hid = jnp.maximum(acc, 0.0)
    return acc, hid * hid


def _relu2_bwd_epilogue(acc, pre):
    return (acc * 2.0 * jnp.maximum(pre, 0.0),)


def _add_epilogue(acc, res):
    return (acc + res,)


def _w_spec(rows, cols, fn):
    return pl.BlockSpec((None, None, rows, cols), fn)


def _with_carry(stage, call):
    carry = stage[0]() if stage is not None else None
    res = call(carry)
    if carry is None:
        return res
    outs, got = res
    stage[1](got)
    return outs[0] if len(outs) == 1 else outs


def _layer_fwd(h, w, big, layer, stages=None):
    t = h.shape[0]
    d = D_MODEL
    tm = _tile(t, (1408, 384, 128))
    nt = t // tm
    row = _bs((tm, d), lambda i, j, k: (i, 0))
    stages = stages or {}
    s = {"h": h}
    u = _norm_fwd(h, w["mix_norm_w"], "norm_mix")
    proj = _with_carry(stages.get("a_ici"), lambda carry: _mm(
        f"mm_proj_{layer}", _NN, (nt, PCOLS // d, 1), u, row,
        big["w_in_main"], _bs((None, d, d), lambda i, j, k: (0, 0, j)),
        [((t, PCOLS), F32, _bs((tm, d), lambda i, j, k: (i, j)))], carry=carry))
    dt_raw = _mm(f"mm_dt_{layer}", _NN, (nt, 1, 1), u, row,
                 big["w_in_dt"], _bs((None, d, DT_PAD), lambda i, j, k: (0, 0, 0)),
                 [((t, DT_PAD), F32, _bs((tm, DT_PAD), lambda i, j, k: (i, 0)))])
    pooled, yg, ypm, y_pool = _pool_fwd(proj, big["pool_w_group"], w["pool_scale"], big["w_pool_up"], layer)
    xa, conv_pre = _conv_fwd(proj, w["conv_w"], w["conv_b"])
    dec = _ssd_decay(dt_raw, w["dt_bias"], w["a_log"])
    stage = stages.get("b_ici")
    carry = stage[0]() if stage is not None else None
    y, prev, got = _ssd_fwd(xa, dec, carry)
    if carry is not None:
        stage[1](got)
    yn = _ssd_post_fwd(y, xa, proj, w["d_skip_exp"], w["ssd_norm_w"])
    rs = D_INNER // N_CHIPS
    y_ssd = _with_carry(stages.get("a_d2d"), lambda carry: _mm(
        f"mm_ssd_out_{layer}", _NN, (nt, 1, 1), yn, _bs((tm, D_INNER), lambda i, j, k: (i, 0)),
        big["w_ssd_out"], _bs((None, N_CHIPS, rs, d), lambda i, j, k: (0, 0, 0, 0)), [((t, d), F32, row)], carry=carry))
    merged = _gate_fwd(proj, w["b_gate"], y_pool, y_ssd)
    ro = d // N_CHIPS
    h1 = _mm(f"mm_o_{layer}", _NN, (nt, 1, 1), merged, row,
             big["w_o"], _bs((None, N_CHIPS, ro, d), lambda i, j, k: (0, 0, 0, 0)), [((t, d), F32, row)],
             extras=[(h, row)], epilogue=_add_epilogue)
    v = _norm_fwd(h1, w["mlp_norm_w"], "norm_mlp")
    tile = _bs((tm, d), lambda i, j, k: (i, j))
    pre, act = _with_carry(stages.get("b_d2d"), lambda carry: _mm(
        f"mm_ff1_{layer}", _NN, (nt, N_CHIPS, 1), v, row,
        big["w_ff1"], _w_spec(d, d, lambda i, j, k: (0, j, 0, 0)),
        [((t, D_FF), F32, tile), ((t, D_FF), BF16, tile)], epilogue=_relu2_epilogue, carry=carry))
    h2 = _mm(f"mm_ff2_{layer}", _NN, (nt, 1, N_CHIPS), act, _bs((tm, d), lambda i, j, k: (i, k)),
             big["w_ff2"], _w_spec(d, d, lambda i, j, k: (0, k, 0, 0)), [((t, d), F32, row)],
             extras=[(h1, row)], epilogue=_add_epilogue)
    s.update(u=u, proj=proj, dec=dec, conv_pre=conv_pre, pooled=pooled, yg=yg, ypm=ypm, y_pool=y_pool, xa=xa, y=y, prev=prev, yn=yn,
             y_ssd=y_ssd, merged=merged, h1=h1, v=v, pre=pre, act=act)
    return h2, s


def _dw(name, layer, a, b, shard_shape, by, tk, carry=None):
    t = a.shape[0]
    nk = t // tk
    rows, cols = shard_shape
    if by == "rows":
        grid = (N_CHIPS, 1, nk)
        a_spec = _bs((tk, rows), lambda i, j, k: (k, i))
        b_spec = _bs((tk, cols), lambda i, j, k: (k, 0))
        o_spec = _bs((None, rows, cols), lambda i, j, k: (i, 0, 0))
    else:
        grid = (1, N_CHIPS, nk)
        a_spec = _bs((tk, rows), lambda i, j, k: (k, 0))
        b_spec = _bs((tk, cols), lambda i, j, k: (k, j))
        o_spec = _bs((None, rows, cols), lambda i, j, k: (j, 0, 0))
    return _mm(f"{name}_{layer}", _TN, grid, a, a_spec, b, b_spec, [((N_CHIPS, rows, cols), F32, o_spec)], carry=carry)


EARLY = ("w_ff2", "w_ff1", "w_o", "w_ssd_out", "w_pool_up", "pool_w_group")


def _layer_bwd(dh, s, w, big, layer, red=None, last=False):
    t = dh.shape[0]
    d = D_MODEL
    tm = _tile(t, (1408, 384, 128))
    tk = _tile(t, (1408, 384, 128))
    nt = t // tm
    row = _bs((tm, d), lambda i, j, k: (i, 0))
    tile = _bs((tm, d), lambda i, j, k: (i, j))
    g, gb = {}, {}
    carry = red.swap_carry() if red is not None else None
    dpre = _mm(f"mm_dact_{layer}", _NT, (nt, N_CHIPS, 1), dh, row,
               big["w_ff2"], _w_spec(d, d, lambda i, j, k: (0, j, 0, 0)), [((t, D_FF), BF16, tile)],
               extras=[(s["pre"], tile)], epilogue=_relu2_bwd_epilogue, carry=carry)
    if carry is not None:
        (dpre,), got = dpre
        red.after_swap(got)
    gb["w_ff2"] = _dw("mm_dw_ff2", layer, s["act"], dh, (d, d), "rows", tk)
    dv = _mm(f"mm_dv_{layer}", _NT, (nt, 1, N_CHIPS), dpre, _bs((tm, d), lambda i, j, k: (i, k)),
             big["w_ff1"], _w_spec(d, d, lambda i, j, k: (0, k, 0, 0)), [((t, d), F32, row)])
    gb["w_ff1"] = _dw("mm_dw_ff1", layer, s["v"], dpre, (d, d), "cols", tk)
    dh1, g["mlp_norm_w"] = _norm_bwd(dh, dv, s["h1"], w["mlp_norm_w"], "norm_mlp_bwd")
    ro = d // N_CHIPS
    dmerged = _mm(f"mm_dmerged_{layer}", _NT, (nt, N_CHIPS, 1), dh1, row,
                  big["w_o"], _w_spec(ro, d, lambda i, j, k: (0, j, 0, 0)),
                  [((t, d), F32, _bs((tm, ro), lambda i, j, k: (i, j)))])
    gb["w_o"] = _dw("mm_dw_o", layer, s["merged"], dh1, (ro, d), "rows", tk)
    dy_pool, dy_ssd, dproj, g["b_gate"] = _gate_bwd(dmerged, s["proj"], w["b_gate"], s["y_pool"], s["y_ssd"])
    rs = D_INNER // N_CHIPS
    dyn = _mm(f"mm_dyn_{layer}", _NT, (nt, N_CHIPS, 1), dy_ssd, row,
              big["w_ssd_out"], _w_spec(rs, d, lambda i, j, k: (0, j, 0, 0)),
              [((t, D_INNER), F32, _bs((tm, rs), lambda i, j, k: (i, j)))])
    gb["w_ssd_out"] = _dw("mm_dw_ssd_out", layer, s["yn"], dy_ssd, (rs, d), "rows", tk)
    dy, dproj, g["ssd_norm_w"], g["d_skip"] = _ssd_post_bwd(dyn, s["y"], s["xa"], s["proj"], w["d_skip_exp"],
                                                            w["ssd_norm_w"], dproj)
    carry = red.ici_carry() if red is not None else None
    dxs, db, dc, ddt_raw, g["dt_bias"], g["a_log"], got = _ssd_bwd(dy, w["d_skip_exp"], s["xa"], s["prev"], s["dec"], carry)
    if carry is not None:
        red.after_ici(got)
    own = red is not None and last
    dproj, dcw1, dcb1 = _conv_bwd(dxs, 0, s["proj"], s["conv_pre"], w["conv_w"], dproj)
    dproj, dcw2, dcb2 = _conv_bwd(db, 2, s["proj"], s["conv_pre"], w["conv_w"], dproj)
    dproj, dcw3, dcb3 = _conv_bwd(dc, 3, s["proj"], s["conv_pre"], w["conv_w"], dproj)
    g["conv_w"] = jnp.concatenate([dcw1, dcw2, dcw3], axis=1)
    g["conv_b"] = jnp.concatenate([dcb1, dcb2, dcb3])
    q, dyg, g["pool_scale"] = _pool_bwd_a(dy_pool, s["yg"], w["pool_scale"], big["pool_w_group"], big["w_pool_up"], layer)
    gb["w_pool_up"] = _dw("mm_dw_pool_up", layer, s["ypm"], dy_pool, (ro, d), "rows", tk)
    gb["pool_w_group"] = _pool_dwg(s["pooled"], dyg, layer).reshape(N_CHIPS, POOL_GDIM, POOL_GDIM)
    dproj = _pool_bwd_b(q, dproj)
    if own:
        red.push(layer, {n: gb[n] for n in EARLY})
    ddt = jnp.pad(ddt_raw.astype(BF16), ((0, 0), (0, DT_PAD - N_HEADS)))
    nk = PCOLS // d
    du = _with_carry((red.swap_carry, red.after_swap) if own else None, lambda carry: _mm(
        f"mm_du_{layer}", _NT, (nt, 1, nk), dproj, _bs((tm, d), lambda i, j, k: (i, k)),
        big["w_in_main"], _bs((None, d, d), lambda i, j, k: (0, 0, k)), [((t, d), F32, row)], carry=carry))
    du = _mm(f"mm_du_dt_{layer}", _NT, (nt, 1, 1), ddt, _bs((tm, DT_PAD), lambda i, j, k: (i, 0)),
             big["w_in_dt"], _bs((None, d, DT_PAD), lambda i, j, k: (0, 0, 0)), [((t, d), F32, row)],
             extras=[(du, row)], epilogue=_add_epilogue)
    ntk = t // tk
    u_spec = _bs((tk, d), lambda i, j, k: (k, 0))
    dmain = _with_carry((red.ici_carry, red.after_ici) if own else None, lambda carry: _mm(
        f"mm_dw_in_{layer}", _TN, (1, nk, ntk), s["u"], u_spec, dproj, _bs((tk, d), lambda i, j, k: (k, j)),
        [((d, PCOLS), F32, _bs((d, d), lambda i, j, k: (0, j)))], carry=carry))
    ddtw = _mm(f"mm_dw_dt_{layer}", _TN, (1, 1, ntk), s["u"], u_spec, ddt, _bs((tk, DT_PAD), lambda i, j, k: (k, 0)),
               [((d, DT_PAD), F32, _bs((d, DT_PAD), lambda i, j, k: (0, 0)))])
    gb["w_in"] = _merge_dw_in(dmain, ddtw)
    dh0, g["mix_norm_w"] = _norm_bwd(dh1, du, s["h"], w["mix_norm_w"], "norm_mix_bwd")
    if red is not None:
        red.push(layer, {n: gb[n] for n in BIG if not (own and n in EARLY)})
    return dh0, g, gb


BIG = ("w_in", "pool_w_group", "w_pool_up", "w_ssd_out", "w_o", "w_ff1", "w_ff2")
PER_LAYER = ("mix_norm_w", "w_in", "b_gate", "pool_w_group", "pool_scale", "w_pool_up", "conv_w", "conv_b", "dt_bias",
             "a_log", "d_skip", "ssd_norm_w", "w_ssd_out", "w_o", "mlp_norm_w", "w_ff1", "w_ff2")


SMALL_PER_LAYER = tuple(n for n in PER_LAYER if n not in BIG)


SHARD_SHAPE = {"w_in": (1024, SHARD_COLS), "pool_w_group": (4, WG_ROWS, POOL_GDIM), "w_pool_up": (WUP_ROWS, D_MODEL),
               "w_ssd_out": (D_INNER // N_CHIPS, D_MODEL), "w_o": (D_MODEL // N_CHIPS, D_MODEL),
               "w_ff1": (D_MODEL, D_FF // N_CHIPS), "w_ff2": (D_FF // N_CHIPS, D_MODEL)}


def _layer_view(mats):
    big = {n: mats[n].reshape((1, N_CHIPS) + SHARD_SHAPE[n]) for n in BIG if n != "w_in"}
    big["w_in_main"], big["w_in_dt"] = _split_w_in(mats["w_in"])
    return big


def _local_step(x, tgt, weights, small, red=None):
    seq = x.shape[0]
    h = jnp.concatenate([jnp.zeros((N_PAD, D_MODEL), F32), small["meta_tokens"], x], axis=0)
    saved, ws, bigs = [], [], []
    for i in range(DEPTH):
        w = {n: small[n][i] for n in SMALL_PER_LAYER}
        w["d_skip_exp"] = jnp.repeat(w["d_skip"], HEAD_DIM)
        big = weights.layer(i)
        h, s = _layer_fwd(h, w, big, i, weights.stages(i))
        saved.append(s)
        ws.append(w)
        bigs.append(big)
    loss, dh, g_final = _loss_head(h, small["final_norm_w"], tgt)
    layer_g, layer_gb = [None] * DEPTH, [None] * DEPTH
    for i in reversed(range(DEPTH)):
        dh, layer_g[i], layer_gb[i] = _layer_bwd(dh, saved[i], ws[i], bigs[i], i, red, last=(i == 0))
    grads = {n: jnp.stack([layer_g[i][n] for i in range(DEPTH)]) for n in SMALL_PER_LAYER}
    grads["final_norm_w"] = g_final
    grads["meta_tokens"] = dh[N_PAD:ROW_X]
    return loss, dh[ROW_X:ROW_X + seq], layer_gb, grads


MESH = pl.DeviceIdType.MESH
LANES = 128
ANY = pl.BlockSpec(memory_space=pl.ANY)


def _place():
    x, y, c = lax.axis_index("x"), lax.axis_index("y"), lax.axis_index("c")
    chips = [(1 - x, y), (x, 1 - y), (1 - x, 1 - y)]
    return x, y, c, chips


def _remote(src, dst, send_sem, recv_sem, to):
    return pltpu.make_async_remote_copy(src_ref=src, dst_ref=dst, send_sem=send_sem, recv_sem=recv_sem,
                                        device_id=to, device_id_type=MESH)


class _WeightGatherer:
    GROUPS = {"a": ("w_in",), "b": tuple(n for n in BIG if n != "w_in")}

    def __init__(self, mine):
        self.mine = mine
        self.landing = {}
        self.ready = {}

    def _ici(self, layer, group):
        names = self.GROUPS[group]
        srcs = [self.mine[n] for n in names]
        outs = [jax.ShapeDtypeStruct((1, N_CHIPS) + s.shape[1:], BF16) for s in srcs]

        def make(ins, out, send_sems, recv_sems):
            x, y, c, chips = _place()
            me = 2 * x + y
            cps = []
            for p, ref in enumerate(ins):
                half = ref.shape[1] // 2
                rows = pl.ds(c * half, half)
                cps += [_remote(ref.at[layer, rows, :], out[p].at[0, me, rows, :], send_sems.at[3 * p + j],
                                recv_sems.at[3 * p + j], (*chip, c)) for j, chip in enumerate(chips)]
            return cps

        return _Carry(srcs, outs, 3 * len(names), make)

    def _d2d(self, layer, group):
        names = self.GROUPS[group]
        n = len(names)
        bufs = self.landing[(layer, group)]
        outs = [jax.ShapeDtypeStruct(b.shape, b.dtype) for b in bufs]

        def make(ins, out, send_sems, recv_sems):
            x, y, c, chips = _place()
            me = 2 * x + y
            sibling = (x, y, 1 - c)
            cps = []
            for p in range(n):
                half = out[p].shape[2] // 2
                for j, (cx, cy) in enumerate(chips):
                    blk = out[p].at[0, 2 * cx + cy, pl.ds(c * half, half), :]
                    cps.append(_remote(blk, blk, send_sems.at[4 * p + j], recv_sems.at[4 * p + j], sibling))
                cps.append(_remote(ins[n + p].at[layer], out[p].at[0, me], send_sems.at[4 * p + 3],
                                   recv_sems.at[4 * p + 3], sibling))
            return cps

        return _Carry(list(bufs) + [self.mine[m] for m in names], outs, 4 * n, make, alias={p: p for p in range(n)})

    def _landed(self, layer, group, bufs):
        self.landing[(layer, group)] = bufs

    def _done(self, layer, group, bufs):
        self.ready.setdefault(layer, {}).update(zip(self.GROUPS[group], bufs))

    def layer(self, i):
        if i == 0:
            for g in self.GROUPS:
                self._landed(0, g, _run_carry(self._ici(0, g), f"gather_ici_{g}_0"))
            for g in self.GROUPS:
                self._done(0, g, _run_carry(self._d2d(0, g), f"gather_d2d_{g}_0"))
        return _layer_view(self.ready[i])

    def stages(self, i):
        nxt = i + 1
        if nxt == DEPTH:
            return None
        st = {}
        for g in self.GROUPS:
            st[f"{g}_ici"] = (functools.partial(self._ici, nxt, g), functools.partial(self._landed, nxt, g))
            st[f"{g}_d2d"] = (functools.partial(self._d2d, nxt, g), functools.partial(self._done, nxt, g))
        return st


def _exchange_small(v, reduce, name):
    rows_per = v.shape[0]
    vm = pl.BlockSpec(memory_space=pltpu.VMEM)

    def body(v_ref, out_ref, *scratch):
        if reduce:
            land_ref, send_sems, recv_sems, local_sem = scratch
        else:
            land_ref = out_ref
            send_sems, recv_sems, local_sem = scratch
        x, y, c, chips = _place()
        me, sibling = (x, y, c), (x, y, 1 - c)

        def rows(px, py, pc):
            return land_ref.at[4 * px + 2 * py + pc]

        def copy(k, block, to, src=None):
            return _remote(rows(*block) if src is None else src, rows(*block), send_sems.at[k], recv_sems.at[k], to)

        mine = pltpu.make_async_copy(v_ref, rows(*me), local_sem)
        mine.start()
        first = [copy(0, me, sibling, src=v_ref)]
        first += [copy(1 + j, me, (*chip, c), src=v_ref) for j, chip in enumerate(chips)]
        for cp in first:
            cp.start()
        passed = [copy(4 + j, (*chip, c), sibling) for j, chip in enumerate(chips)]
        for j, chip in enumerate(chips):
            copy(1 + j, (*chip, c), me).wait_recv()
            passed[j].start()
        copy(0, sibling, me).wait_recv()
        for j, chip in enumerate(chips):
            copy(4 + j, (*chip, 1 - c), me).wait_recv()
        for cp in first + passed:
            cp.wait_send()
        mine.wait()
        if reduce:
            acc = land_ref[0]
            for d in range(1, 8):
                acc = acc + land_ref[d]
            out_ref[...] = acc

    sems = [pltpu.SemaphoreType.DMA((7,)), pltpu.SemaphoreType.DMA((7,)), pltpu.SemaphoreType.DMA]
    if reduce:
        out_shape = jax.ShapeDtypeStruct((rows_per, LANES), F32)
        scratch = [pltpu.VMEM((8, rows_per, LANES), F32)] + sems
    else:
        out_shape = jax.ShapeDtypeStruct((8, rows_per, LANES), F32)
        scratch = sems
    return pl.pallas_call(
        body, name=name, in_specs=[vm], out_specs=vm, out_shape=out_shape, scratch_shapes=scratch,
        compiler_params=pltpu.CompilerParams(vmem_limit_bytes=VMEM_LIMIT),
    )(v)


def _run_carry(carry, name):
    c_in, c_out, c_sem = carry.specs()
    n_in, n_out = len(c_in), len(c_out)

    def body(*refs):
        cps = carry.make(refs[:n_in], refs[n_in:n_in + n_out], *refs[n_in + n_out:])
        for cp in cps:
            cp.start()
        for cp in cps:
            cp.wait()

    return pl.pallas_call(body, name=name, in_specs=c_in, out_specs=c_out, out_shape=carry.outs,
                          scratch_shapes=c_sem, input_output_aliases=carry.aliases(0, 0))(*carry.ins)


def _row_tile(rows, last, itemsize=4, budget=2 * 1024 * 1024):
    return _tile(rows, tuple(t for t in (2048, 1024, 512, 256, 128, 64, 32, 16) if t * last * itemsize <= budget))


def _rs_add(g, got, core, name):
    _, half, last = got.shape
    tr = _row_tile(half, last)
    nb = half // tr

    def body(c_ref, g_ref, got_ref, o_ref):
        o_ref[...] = (g_ref[...] + got_ref[...]).astype(BF16)

    blk = pl.BlockSpec((None, tr, last), lambda s, i, c_ref: (s, i, 0))
    return pl.pallas_call(
        body, name=name,
        grid_spec=pltpu.PrefetchScalarGridSpec(
            num_scalar_prefetch=1, grid=(N_CHIPS, nb),
            in_specs=[pl.BlockSpec((None, tr, last), lambda s, i, c_ref: (s, c_ref[0] * nb + i, 0)), blk],
            out_specs=blk),
        out_shape=jax.ShapeDtypeStruct(got.shape, BF16),
        compiler_params=_cparams(("parallel", "parallel")),
    )(core.reshape(1).astype(jnp.int32), g, got)


def _rs_sum(own, got, acc, layer, chip, core, name):
    _, half, last = own.shape
    tr = _row_tile(half, last, budget=1024 * 1024)
    nb = half // tr
    ni = 0 if acc is None else 1

    def body(k_ref, c_ref, own_ref, got_ref, *rest):
        o_ref = rest[-1]
        tot = own_ref[...].astype(F32)
        for j in range(3):
            tot = tot + got_ref[j].astype(F32)
        o_ref[...] = tot

    return pl.pallas_call(
        body, name=name,
        grid_spec=pltpu.PrefetchScalarGridSpec(
            num_scalar_prefetch=2, grid=(nb,),
            in_specs=[pl.BlockSpec((None, tr, last), lambda i, k_ref, c_ref: (k_ref[0], i, 0)),
                      pl.BlockSpec((3, tr, last), lambda i, k_ref, c_ref: (0, i, 0))]
            + [pl.BlockSpec(memory_space=pl.ANY)] * ni,
            out_specs=pl.BlockSpec((None, tr, last), lambda i, k_ref, c_ref: (layer, c_ref[0] * nb + i, 0))),
        out_shape=jax.ShapeDtypeStruct((DEPTH, 2 * half, last), F32),
        input_output_aliases={4: 0} if ni else {},
        compiler_params=_cparams(("parallel",)),
    )(chip.reshape(1).astype(jnp.int32), core.reshape(1).astype(jnp.int32), own, got, *([acc] if ni else []))


def _rs_share(arrs):
    n = len(arrs)

    def body(*refs):
        out = refs[n:2 * n]
        send_sems, recv_sems = refs[2 * n:]
        x, y, c, _ = _place()
        sibling = (x, y, 1 - c)
        cps = []
        for p in range(n):
            half = out[p].shape[1] // 2
            mine = out[p].at[:, pl.ds(c * half, half), :]
            cps.append(_remote(mine, mine, send_sems.at[p], recv_sems.at[p], sibling))
        for cp in cps:
            cp.start()
        for p, cp in enumerate(cps):
            cp.wait_send()
            half = out[p].shape[1] // 2
            other = out[p].at[:, pl.ds((1 - c) * half, half), :]
            _remote(other, other, send_sems.at[p], recv_sems.at[p], sibling).wait_recv()

    return pl.pallas_call(
        body, name="rs_share", in_specs=[ANY] * n, out_specs=[ANY] * n,
        out_shape=[jax.ShapeDtypeStruct(a.shape, a.dtype) for a in arrs],
        input_output_aliases={p: p for p in range(n)},
        scratch_shapes=[pltpu.SemaphoreType.DMA((n,)), pltpu.SemaphoreType.DMA((n,))],
    )(*arrs)


class _GradReducer:
    def __init__(self, chip, core):
        self.chip, self.core = chip, core
        self.pending = None
        self.sums = None
        self.acc = {n: None for n in BIG}

    def push(self, layer, gb):
        assert self.pending is None
        self.pending = (layer, list(gb), list(gb.values()))

    def swap_carry(self):
        if self.pending is None:
            return None
        _, _, gs = self.pending
        outs = [jax.ShapeDtypeStruct((g.shape[0], g.shape[1] // 2, g.shape[2]), F32) for g in gs]

        def make(ins, out, send_sems, recv_sems):
            x, y, c, _ = _place()
            cps = []
            for p, ref in enumerate(ins):
                half = ref.shape[1] // 2
                cps.append(_remote(ref.at[:, pl.ds((1 - c) * half, half), :], out[p], send_sems.at[p], recv_sems.at[p],
                                   (x, y, 1 - c)))
            return cps

        return _Carry(gs, outs, len(gs), make)

    def after_swap(self, got):
        layer, names, gs = self.pending
        self.pending = None
        self.sums = (layer, names, [_rs_add(g, a, self.core, f"rs_add_{n}_{layer}") for n, g, a in zip(names, gs, got)])

    def ici_carry(self):
        if self.sums is None:
            return None
        _, _, ps = self.sums
        outs = [jax.ShapeDtypeStruct((3,) + p.shape[1:], BF16) for p in ps]

        def make(ins, out, send_sems, recv_sems):
            x, y, c, chips = _place()
            return [_remote(ins[p].at[2 * cx + cy], out[p].at[j], send_sems.at[3 * p + j], recv_sems.at[3 * p + j],
                            (cx, cy, c)) for p in range(len(ins)) for j, (cx, cy) in enumerate(chips)]

        return _Carry(ps, outs, 3 * len(ps), make)

    def after_ici(self, got):
        layer, names, ps = self.sums
        self.sums = None
        for n, p, a in zip(names, ps, got):
            self.acc[n] = _rs_sum(p, a, self.acc[n], layer, self.chip, self.core, f"rs_sum_{n}_{layer}")

    def finish(self):
        if self.pending is not None:
            self.after_swap(_run_carry(self.swap_carry(), "rs_swap_last"))
        if self.sums is not None:
            self.after_ici(_run_carry(self.ici_carry(), "rs_ici_last"))
        return dict(zip(BIG, _rs_share([self.acc[n] for n in BIG])))


def _adamw(w, g, m, v, name):
    shape = w.shape
    if len(shape) == 2:
        shape3 = (1,) + shape
    else:
        shape3 = (-1,) + shape[-2:]
    w2, g2, m2, v2 = (a.reshape(shape3) for a in (w, g, m, v))
    lead, rows, last = w2.shape
    tr = max([t for t in range(8, rows + 1, 8) if rows % t == 0 and t * last * 4 <= 2 * 1024 * 1024] or [rows])

    def body(w_ref, g_ref, m_ref, v_ref, d_ref, nm_ref, nv_ref):
        gv = g_ref[...]
        mn = ADAM_B1 * m_ref[...] + (1.0 - ADAM_B1) * gv
        vn = ADAM_B2 * v_ref[...] + (1.0 - ADAM_B2) * (gv * gv)
        m_hat = mn / (1.0 - ADAM_B1 ** ADAM_STEP)
        v_hat = vn / (1.0 - ADAM_B2 ** ADAM_STEP)
        d_ref[...] = -ADAM_LR * (m_hat / (jnp.sqrt(v_hat) + ADAM_EPS) + ADAM_WD * w_ref[...])
        nm_ref[...] = mn
        nv_ref[...] = vn

    blk = pl.BlockSpec((None, tr, last), lambda a, i: (a, i, 0))
    outs = pl.pallas_call(
        body, name=name, grid=(lead, rows // tr), in_specs=[blk] * 4, out_specs=[blk] * 3,
        out_shape=[jax.ShapeDtypeStruct((lead, rows, last), F32)] * 3,
        compiler_params=_cparams(("parallel", "parallel")),
    )(w2, g2, m2, v2)
    return tuple(o.reshape(shape) for o in outs)


SHARD_AXIS = {"conv_w": 2, "meta_tokens": 1}
SMALL_SHARDED = ("conv_w", "meta_tokens")
REPLICATED = ("mix_norm_w", "b_gate", "pool_scale", "conv_b", "dt_bias", "a_log", "d_skip", "ssd_norm_w", "mlp_norm_w",
              "final_norm_w")


def _flatten(arrs, dtype, row_mult):
    flat = jnp.concatenate([a.astype(dtype).reshape(-1) for a in arrs])
    n = flat.shape[0]
    rows = -(-n // (LANES * row_mult)) * row_mult
    return jnp.pad(flat, (0, rows * LANES - n)).reshape(rows, LANES)


def _unflatten(flat2d, shapes):
    flat = flat2d.reshape(-1)
    out, off = [], 0
    for sh in shapes:
        n = 1
        for d in sh:
            n *= d
        out.append(flat[off:off + n].reshape(sh))
        off += n
    return out


def kernel(x, meta_tokens, mix_norm_w, w_in, b_gate, pool_w_group, pool_scale, w_pool_up, conv_w, conv_b, dt_bias, a_log, d_skip, ssd_norm_w, w_ssd_out, w_o, mlp_norm_w, w_ff1, w_ff2, final_norm_w, loss_target, m_meta_tokens, m_mix_norm_w, m_w_in, m_b_gate, m_pool_w_group, m_pool_scale, m_w_pool_up, m_conv_w, m_conv_b, m_dt_bias, m_a_log, m_d_skip, m_ssd_norm_w, m_w_ssd_out, m_w_o, m_mlp_norm_w, m_w_ff1, m_w_ff2, m_final_norm_w, v_meta_tokens, v_mix_norm_w, v_w_in, v_b_gate, v_pool_w_group, v_pool_scale, v_w_pool_up, v_conv_w, v_conv_b, v_dt_bias, v_a_log, v_d_skip, v_ssd_norm_w, v_w_ssd_out, v_w_o, v_mlp_norm_w, v_w_ff1, v_w_ff2, v_final_norm_w):
    names = ("meta_tokens",) + PER_LAYER + ("final_norm_w",)
    par = dict(meta_tokens=meta_tokens, mix_norm_w=mix_norm_w, w_in=w_in, b_gate=b_gate, pool_w_group=pool_w_group,
               pool_scale=pool_scale, w_pool_up=w_pool_up, conv_w=conv_w, conv_b=conv_b, dt_bias=dt_bias, a_log=a_log,
               d_skip=d_skip, ssd_norm_w=ssd_norm_w, w_ssd_out=w_ssd_out, w_o=w_o, mlp_norm_w=mlp_norm_w, w_ff1=w_ff1,
               w_ff2=w_ff2, final_norm_w=final_norm_w)
    mom = dict(meta_tokens=m_meta_tokens, mix_norm_w=m_mix_norm_w, w_in=m_w_in, b_gate=m_b_gate, pool_w_group=m_pool_w_group,
               pool_scale=m_pool_scale, w_pool_up=m_w_pool_up, conv_w=m_conv_w, conv_b=m_conv_b, dt_bias=m_dt_bias,
               a_log=m_a_log, d_skip=m_d_skip, ssd_norm_w=m_ssd_norm_w, w_ssd_out=m_w_ssd_out, w_o=m_w_o,
               mlp_norm_w=m_mlp_norm_w, w_ff1=m_w_ff1, w_ff2=m_w_ff2, final_norm_w=m_final_norm_w)
    var = dict(meta_tokens=v_meta_tokens, mix_norm_w=v_mix_norm_w, w_in=v_w_in, b_gate=v_b_gate, pool_w_group=v_pool_w_group,
               pool_scale=v_pool_scale, w_pool_up=v_w_pool_up, conv_w=v_conv_w, conv_b=v_conv_b, dt_bias=v_dt_bias,
               a_log=v_a_log, d_skip=v_d_skip, ssd_norm_w=v_ssd_norm_w, w_ssd_out=v_w_ssd_out, w_o=v_w_o,
               mlp_norm_w=v_mlp_norm_w, w_ff1=v_w_ff1, w_ff2=v_w_ff2, final_norm_w=v_final_norm_w)
    chip = 2 * lax.axis_index("x") + lax.axis_index("y")
    core = lax.axis_index("c")

    weights = _WeightGatherer({n: par[n].astype(BF16).reshape(DEPTH, -1, par[n].shape[-1]) for n in BIG})
    small = {n: par[n] for n in REPLICATED}
    small_shapes = [par[n].shape for n in SMALL_SHARDED]
    got_small = _exchange_small(_flatten([par[n] for n in SMALL_SHARDED], F32, 8), False, "gather_small")
    pieces = [_unflatten(got_small[2 * k], small_shapes) for k in range(N_CHIPS)]
    for j, n in enumerate(SMALL_SHARDED):
        small[n] = jnp.concatenate([pieces[k][j] for k in range(N_CHIPS)], axis=SHARD_AXIS[n])

    reducer = _GradReducer(chip, core)
    loss, dx, _, grads = _local_step(x[0], loss_target[0], weights, small, reducer)

    small_names = REPLICATED + SMALL_SHARDED
    red = _exchange_small(_flatten([grads[n] for n in small_names] + [loss.reshape(1)], F32, 8), True, "allreduce_small")
    *parts, loss = _unflatten(red, [grads[n].shape for n in small_names] + [(1,)])
    loss = loss[0]
    gsum = dict(zip(small_names, parts))
    for n in SMALL_SHARDED:
        width = par[n].shape[SHARD_AXIS[n]]
        gsum[n] = lax.dynamic_slice_in_dim(gsum[n], chip * width, width, axis=SHARD_AXIS[n])

    gsum.update({n: g.reshape(par[n].shape) for n, g in reducer.finish().items()})

    delta, new_m, new_v = {}, {}, {}
    for n in BIG:
        if n == "w_in":
            tr = lambda a: jnp.swapaxes(a, 1, 2)
            outs = _adamw(tr(par[n]), tr(gsum[n]), tr(mom[n]), tr(var[n]), "adamw_" + n)
            delta[n], new_m[n], new_v[n] = (tr(o) for o in outs)
        else:
            delta[n], new_m[n], new_v[n] = _adamw(par[n], gsum[n], mom[n], var[n], "adamw_" + n)
    rest = [n for n in names if n not in BIG]
    shapes = [par[n].shape for n in rest]
    packed = [_flatten([d[n] for n in rest], F32, 8) for d in (par, gsum, mom, var)]
    for d, flat in zip((delta, new_m, new_v), _adamw(*packed, "adamw_small")):
        d.update(zip(rest, _unflatten(flat, shapes)))

    order = ("meta_tokens", "mix_norm_w", "w_in", "b_gate", "pool_w_group", "pool_scale", "w_pool_up", "conv_w", "conv_b",
             "dt_bias", "a_log", "d_skip", "ssd_norm_w", "w_ssd_out", "w_o", "mlp_norm_w", "w_ff1", "w_ff2", "final_norm_w")
    return (loss, dx[None], *[gsum[n] for n in order], *[delta[n] for n in order], *[new_m[n] for n in order],
            *[new_v[n] for n in order])
```

```python
import functools

import jax
import jax.numpy as jnp
from jax import lax
from jax.experimental import pallas as pl
from jax.experimental.pallas import tpu as pltpu

F32 = jnp.float32
BF16 = jnp.bfloat16

D_MODEL = 1024
DEPTH = 4
N_META = 16
N_PAD = 112
ROW_X = N_PAD + N_META
POOL_WINDOWS = (2, 4, 8, 16)
POOL_GDIM = 256
D_INNER = 2048
N_HEADS = 32
HEAD_DIM = 64
N_GROUPS = 8
HEADS_PER_GROUP = 4
GROUP_W = HEADS_PER_GROUP * HEAD_DIM
D_STATE = 128
CHUNK = 128
D_XBC = 4096
D_FF = 4096
EPS = 1e-5
OFF_Z, OFF_XBC, OFF_DT, OFF_GATE, IN_COLS = 1024, 3072, 7168, 7200, 9248
PZ, PGATE, PPOOL, PXBC, PCOLS = 0, 2048, 4096, 5120, 9216
DT_PAD = 128

ADAM_LR, ADAM_B1, ADAM_B2, ADAM_EPS, ADAM_WD, ADAM_STEP = 0.001, 0.9, 0.999, 1e-08, 0.01, 10

VMEM_LIMIT = 56 * 1024 * 1024

_NN = (((1,), (0,)), ((), ()))
_NT = (((1,), (1,)), ((), ()))
_TN = (((0,), (0,)), ((), ()))


def _dot(a, b, dn=_NN):
    return lax.dot_general(a, b, dn, preferred_element_type=F32)


def _cparams(sem):
    return pltpu.CompilerParams(dimension_semantics=sem, vmem_limit_bytes=VMEM_LIMIT)


def _tile(n, cands):
    for c in cands:
        if n % c == 0:
            return c
    raise ValueError(f"no tile for {n} in {cands}")


def _sigmoid(x):
    return 0.5 * jnp.tanh(0.5 * x) + 0.5


def _softplus(x):
    return jnp.maximum(x, 0.0) + jnp.log(1.0 + jnp.exp(-jnp.abs(x)))


def _bs(shape, fn):
    return pl.BlockSpec(shape, fn)


class _Carry:
    def __init__(self, ins, outs, n_sems, make, alias=None):
        self.ins, self.outs, self.n_sems, self.make = list(ins), list(outs), n_sems, make
        self.alias = dict(alias or {})

    def aliases(self, first_in, first_out):
        return {first_in + i: first_out + o for i, o in self.alias.items()}

    def specs(self):
        hbm = pl.BlockSpec(memory_space=pl.ANY)
        return [hbm] * len(self.ins), [hbm] * len(self.outs), [pltpu.SemaphoreType.DMA((self.n_sems,))] * 2

    def at_edges(self, grid, in_refs, out_refs, sems):
        ids = [pl.program_id(a) for a in range(len(grid))]
        first = functools.reduce(jnp.logical_and, [i == 0 for i in ids])
        last = functools.reduce(jnp.logical_and, [i == n - 1 for i, n in zip(ids, grid)])

        def begin():
            @pl.when(first)
            def _():
                for cp in self.make(in_refs, out_refs, *sems):
                    cp.start()

        def end():
            @pl.when(last)
            def _():
                for cp in self.make(in_refs, out_refs, *sems):
                    cp.wait()

        return begin, end


def _mm(name, dn, grid, a, a_spec, b, b_spec, outs, *, extras=(), epilogue=None, carry=None):
    nk = grid[2]
    ne, no = len(extras), len(outs)
    blk = tuple(d for d in outs[0][2].block_shape if d is not None)
    c_in, c_out, c_sem = carry.specs() if carry is not None else ([], [], [])
    nci, nco = len(c_in), len(c_out)

    def body(a_ref, b_ref, *rest):
        e_refs, o_refs = rest[:ne], rest[ne + nci:ne + nci + no]
        if carry is not None:
            begin, end = carry.at_edges(grid, rest[ne:ne + nci], rest[ne + nci + no:ne + nci + no + nco], rest[-2:])
            begin()
        if len(b_ref.shape) == 3:
            rows = b_ref.shape[1]
            p = _dot(a_ref[:, :rows].astype(BF16), b_ref[0].astype(BF16), dn)
            for s in range(1, b_ref.shape[0]):
                p = p + _dot(a_ref[:, s * rows:(s + 1) * rows].astype(BF16), b_ref[s].astype(BF16), dn)
        else:
            p = _dot(a_ref[...].astype(BF16), b_ref[...].astype(BF16), dn)

        def finish(acc):
            outs = epilogue(acc, *[e[...] for e in e_refs]) if epilogue is not None else (acc,)
            for o, v in zip(o_refs, outs):
                o[...] = v.astype(o.dtype)

        if nk == 1:
            finish(p)
        else:
            acc_ref = rest[ne + nci + no + nco]
            kk = pl.program_id(2)

            @pl.when(kk == 0)
            def _():
                acc_ref[...] = p

            @pl.when(kk > 0)
            def _():
                acc_ref[...] += p

            @pl.when(kk == nk - 1)
            def _():
                finish(acc_ref[...])
        if carry is not None:
            end()

    res = pl.pallas_call(
        body,
        name=name,
        grid=grid,
        in_specs=[a_spec, b_spec, *[s for _, s in extras]] + c_in,
        out_specs=[o[2] for o in outs] + c_out,
        out_shape=[jax.ShapeDtypeStruct(o[0], o[1]) for o in outs] + (carry.outs if carry is not None else []),
        scratch_shapes=([pltpu.VMEM(blk, F32)] if nk > 1 else []) + c_sem,
        input_output_aliases=carry.aliases(2 + ne, no) if carry is not None else {},
        compiler_params=_cparams(("arbitrary",) * 3 if carry is not None else ("parallel", "parallel", "arbitrary")),
    )(a, b, *[e for e, _ in extras], *(carry.ins if carry is not None else []))
    if carry is not None:
        return res[:no], res[no:]
    return res[0] if no == 1 else res


def _norm_fwd(h, w, name):
    t, d = h.shape
    tm = _tile(t, (1056, 384, 128))

    def body(h_ref, w_ref, u_ref):
        x = h_ref[...]
        r = lax.rsqrt(jnp.mean(x * x, axis=-1, keepdims=True) + EPS)
        u_ref[...] = (x * r * w_ref[...]).astype(BF16)

    return pl.pallas_call(
        body, name=name, grid=(t // tm,),
        in_specs=[pl.BlockSpec((tm, d), lambda i: (i, 0)), pl.BlockSpec((1, d), lambda i: (0, 0))],
        out_specs=pl.BlockSpec((tm, d), lambda i: (i, 0)),
        out_shape=jax.ShapeDtypeStruct((t, d), BF16),
        compiler_params=_cparams(("parallel",)),
    )(h, w.reshape(1, d))


def _norm_bwd(dres, du, h, w, name):
    t, d = h.shape
    tm = _tile(t, (528, 384, 128))

    def body(dres_ref, du_ref, h_ref, w_ref, dh_ref, dw_ref):
        x = h_ref[...]
        r = lax.rsqrt(jnp.mean(x * x, axis=-1, keepdims=True) + EPS)
        xhat = x * r
        du_v = du_ref[...]
        g = du_v * w_ref[...]
        dh_ref[...] = dres_ref[...] + r * (g - xhat * jnp.mean(g * xhat, axis=-1, keepdims=True))

        @pl.when(pl.program_id(0) == 0)
        def _():
            dw_ref[...] = jnp.zeros_like(dw_ref)

        dw_ref[...] += jnp.sum(du_v * xhat, axis=0, keepdims=True)

    row = pl.BlockSpec((tm, d), lambda i: (i, 0))
    vec = pl.BlockSpec((1, d), lambda i: (0, 0))
    dh, dw = pl.pallas_call(
        body, name=name, grid=(t // tm,),
        in_specs=[row, row, row, vec], out_specs=[row, vec],
        out_shape=[jax.ShapeDtypeStruct((t, d), F32), jax.ShapeDtypeStruct((1, d), F32)],
        compiler_params=_cparams(("arbitrary",)),
    )(dres, du, h, w.reshape(1, d))
    return dh, dw.reshape(d)


def _loss_head(h, w, tgt):
    t, d = h.shape
    tm = CHUNK
    nb = ROW_X // tm

    def body(h_ref, w_ref, t_ref, loss_ref, dh_ref, dw_ref):
        i = pl.program_id(0)
        x = h_ref[...]
        r = lax.rsqrt(jnp.mean(x * x, axis=-1, keepdims=True) + EPS)
        xhat = x * r
        wv = w_ref[...]
        live = i >= nb
        err = jnp.where(live, xhat * wv - t_ref[...], 0.0)
        dout = err * (1.0 / d)
        g = dout * wv
        dh_ref[...] = r * (g - xhat * jnp.mean(g * xhat, axis=-1, keepdims=True))

        @pl.when(i == 0)
        def _():
            loss_ref[...] = jnp.zeros_like(loss_ref)
            dw_ref[...] = jnp.zeros_like(dw_ref)

        loss_ref[...] += 0.5 * jnp.sum(jnp.sum(err * err, axis=-1, keepdims=True), axis=0, keepdims=True) * (1.0 / d)
        dw_ref[...] += jnp.sum(dout * xhat, axis=0, keepdims=True)

    row = pl.BlockSpec((tm, d), lambda i: (i, 0))
    vec = pl.BlockSpec((1, d), lambda i: (0, 0))
    loss, dh, dw = pl.pallas_call(
        body, name="loss_head", grid=(t // tm,),
        in_specs=[row, vec, pl.BlockSpec((tm, d), lambda i: (jnp.maximum(i - nb, 0), 0))],
        out_specs=[pl.BlockSpec((1, 1), lambda i: (0, 0)), row, vec],
        out_shape=[jax.ShapeDtypeStruct((1, 1), F32), jax.ShapeDtypeStruct((t, d), F32), jax.ShapeDtypeStruct((1, d), F32)],
        compiler_params=_cparams(("arbitrary",)),
    )(h, w.reshape(1, d), tgt)
    return loss[0, 0], dh, dw.reshape(d)


def _gate_fwd(proj, b_gate, y_pool, y_ssd):
    t = proj.shape[0]
    d = D_MODEL
    tm = _tile(t, (528, 384, 128))

    def body(gp_ref, gs_ref, bp_ref, bs_ref, yp_ref, ys_ref, o_ref):
        gp = _sigmoid(gp_ref[...] + bp_ref[...])
        gs = _sigmoid(gs_ref[...] + bs_ref[...])
        o_ref[...] = (gp * yp_ref[...] + gs * ys_ref[...]).astype(BF16)

    row = pl.BlockSpec((tm, d), lambda i: (i, 0))
    return pl.pallas_call(
        body, name="gate_fwd", grid=(t // tm,),
        in_specs=[pl.BlockSpec((tm, d), lambda i: (i, PGATE // d)), pl.BlockSpec((tm, d), lambda i: (i, PGATE // d + 1)),
                  pl.BlockSpec((1, d), lambda i: (0, 0)), pl.BlockSpec((1, d), lambda i: (0, 1)), row, row],
        out_specs=row, out_shape=jax.ShapeDtypeStruct((t, d), BF16),
        compiler_params=_cparams(("parallel",)),
    )(proj, proj, b_gate.reshape(1, 2 * d), b_gate.reshape(1, 2 * d), y_pool, y_ssd)


def _gate_bwd(dmerged, proj, b_gate, y_pool, y_ssd):
    t = proj.shape[0]
    d = D_MODEL
    tm = _tile(t, (384, 128))

    def body(dm_ref, gp_ref, gs_ref, bp_ref, bs_ref, yp_ref, ys_ref, dyp_ref, dys_ref, dg_ref, db_ref):
        dm = dm_ref[...]
        gp = _sigmoid(gp_ref[...] + bp_ref[...])
        gs = _sigmoid(gs_ref[...] + bs_ref[...])
        dyp_ref[...] = (dm * gp).astype(BF16)
        dys_ref[...] = (dm * gs).astype(BF16)
        dgp = dm * yp_ref[...] * gp * (1.0 - gp)
        dgs = dm * ys_ref[...] * gs * (1.0 - gs)
        dg_ref[:, :d] = dgp.astype(BF16)
        dg_ref[:, d:] = dgs.astype(BF16)

        @pl.when(pl.program_id(0) == 0)
        def _():
            db_ref[...] = jnp.zeros_like(db_ref)

        db_ref[:, :d] += jnp.sum(dgp, axis=0, keepdims=True)
        db_ref[:, d:] += jnp.sum(dgs, axis=0, keepdims=True)

    row = pl.BlockSpec((tm, d), lambda i: (i, 0))
    dyp, dys, dg, db = pl.pallas_call(
        body, name="gate_bwd", grid=(t // tm,),
        in_specs=[row, pl.BlockSpec((tm, d), lambda i: (i, PGATE // d)), pl.BlockSpec((tm, d), lambda i: (i, PGATE // d + 1)),
                  pl.BlockSpec((1, d), lambda i: (0, 0)), pl.BlockSpec((1, d), lambda i: (0, 1)), row, row],
        out_specs=[row, row, pl.BlockSpec((tm, 2 * d), lambda i: (i, PGATE // (2 * d))),
                   pl.BlockSpec((1, 2 * d), lambda i: (0, 0))],
        out_shape=[jax.ShapeDtypeStruct((t, d), BF16), jax.ShapeDtypeStruct((t, d), BF16),
                   jax.ShapeDtypeStruct((t, PCOLS), BF16), jax.ShapeDtypeStruct((1, 2 * d), F32)],
        compiler_params=_cparams(("arbitrary",)),
    )(dmerged, proj, proj, b_gate.reshape(1, 2 * d), b_gate.reshape(1, 2 * d), y_pool, y_ssd)
    return dyp, dys, dg, db.reshape(2 * d)


POOL_HALO = 16


def _pool_counts(row0, n, win):
    pos1 = row0 + lax.broadcasted_iota(jnp.int32, (n, 1), 0) - (N_PAD - 1)
    return jnp.clip(pos1, 1, win).astype(F32)


N_CHIPS = 4
WG_ROWS = POOL_GDIM // N_CHIPS
WUP_ROWS = D_MODEL // N_CHIPS


def _group_w(wg_ref, g):
    return jnp.concatenate([wg_ref[k, g] for k in range(N_CHIPS)], axis=0)


def _pool_w_specs(layer):
    return [pl.BlockSpec((None, N_CHIPS, len(POOL_WINDOWS), WG_ROWS, POOL_GDIM), lambda i: (0, 0, 0, 0, 0)),
            pl.BlockSpec((None, N_CHIPS, WUP_ROWS, D_MODEL), lambda i: (0, 0, 0, 0))]


def _pool_fwd(proj, wg, scale, wup, layer):
    t = proj.shape[0]
    d = D_MODEL
    tm = _tile(t, (384, 128))
    hb = tm // POOL_HALO

    def body(u_ref, halo_ref, sc_ref, wg_ref, wup_ref, pooled_ref, yg_ref, ypm_ref, yp_ref):
        i = pl.program_id(0)
        x = u_ref[...]
        halo = jnp.where(i > 0, halo_ref[...], 0.0)
        xc = jnp.concatenate([halo, x], axis=0)
        for g, win in enumerate(POOL_WINDOWS):
            sl = slice(g * POOL_GDIM, (g + 1) * POOL_GDIM)
            s = xc[:, sl]
            k = 1
            while k < win:
                s = s + pltpu.roll(s, k, axis=0)
                k *= 2
            pooled = s[POOL_HALO:] / _pool_counts(i * tm, tm, win) - x[:, sl]
            pb = pooled.astype(BF16)
            pooled_ref[:, sl] = pb
            yg_ref[:, sl] = _dot(pb, _group_w(wg_ref, g))
        ypm = (yg_ref[...] * sc_ref[...]).astype(BF16)
        ypm_ref[...] = ypm
        acc = _dot(ypm[:, :WUP_ROWS], wup_ref[0])
        for k in range(1, N_CHIPS):
            acc = acc + _dot(ypm[:, k * WUP_ROWS:(k + 1) * WUP_ROWS], wup_ref[k])
        yp_ref[...] = acc

    row = pl.BlockSpec((tm, d), lambda i: (i, 0))
    return pl.pallas_call(
        body, name=f"pool_fwd_{layer}", grid=(t // tm,),
        in_specs=[pl.BlockSpec((tm, d), lambda i: (i, PPOOL // d)),
                  pl.BlockSpec((POOL_HALO, d), lambda i: (jnp.maximum(i * hb - 1, 0), PPOOL // d)),
                  pl.BlockSpec((1, d), lambda i: (0, 0))] + _pool_w_specs(layer),
        out_specs=[row, row, row, row],
        out_shape=[jax.ShapeDtypeStruct((t, d), BF16), jax.ShapeDtypeStruct((t, d), F32),
                   jax.ShapeDtypeStruct((t, d), BF16), jax.ShapeDtypeStruct((t, d), F32)],
        compiler_params=_cparams(("parallel",)),
    )(proj, proj, scale.reshape(1, d), wg, wup)


def _pool_bwd_a(dy_pool, yg, scale, wg, wup, layer):
    t, d = yg.shape
    tm = _tile(t, (384, 128))

    def body(dy_ref, yg_ref, sc_ref, wg_ref, wup_ref, q_ref, dyg_ref, dsc_ref):
        dy = dy_ref[...]
        dypm = jnp.concatenate([_dot(dy, wup_ref[k], _NT) for k in range(N_CHIPS)], axis=1)

        @pl.when(pl.program_id(0) == 0)
        def _():
            dsc_ref[...] = jnp.zeros_like(dsc_ref)

        dsc_ref[...] += jnp.sum(dypm * yg_ref[...], axis=0, keepdims=True)
        dyg = (dypm * sc_ref[...]).astype(BF16)
        dyg_ref[...] = dyg
        for g in range(len(POOL_WINDOWS)):
            sl = slice(g * POOL_GDIM, (g + 1) * POOL_GDIM)
            q_ref[:, sl] = _dot(dyg[:, sl], _group_w(wg_ref, g), _NT)

    row = pl.BlockSpec((tm, d), lambda i: (i, 0))
    vec = pl.BlockSpec((1, d), lambda i: (0, 0))
    q, dyg, dsc = pl.pallas_call(
        body, name=f"pool_bwd_a_{layer}", grid=(t // tm,),
        in_specs=[row, row, vec] + _pool_w_specs(layer),
        out_specs=[row, row, vec],
        out_shape=[jax.ShapeDtypeStruct((t, d), F32), jax.ShapeDtypeStruct((t, d), BF16), jax.ShapeDtypeStruct((1, d), F32)],
        compiler_params=_cparams(("arbitrary",)),
    )(dy_pool, yg, scale.reshape(1, d), wg, wup)
    return q, dyg, dsc.reshape(d)


def _pool_bwd_b(q, dproj):
    t, d = q.shape
    tm = _tile(t, (384, 128))
    hb = tm // POOL_HALO
    nt = t // tm
    n = tm + POOL_HALO

    def body(q_ref, halo_ref, _, o_ref):
        i = pl.program_id(0)
        qv = q_ref[...]
        halo = jnp.where(i < nt - 1, halo_ref[...], 0.0)
        qc = jnp.concatenate([qv, halo], axis=0)
        for g, win in enumerate(POOL_WINDOWS):
            sl = slice(g * POOL_GDIM, (g + 1) * POOL_GDIM)
            s = qc[:, sl] / _pool_counts(i * tm, n, win)
            k = 1
            while k < win:
                s = s + pltpu.roll(s, n - k, axis=0)
                k *= 2
            o_ref[:, sl] = (s[:tm] - qv[:, sl]).astype(BF16)

    row = pl.BlockSpec((tm, d), lambda i: (i, 0))
    return pl.pallas_call(
        body, name="pool_bwd_b", grid=(nt,),
        in_specs=[row, pl.BlockSpec((POOL_HALO, d), lambda i: (jnp.minimum((i + 1) * hb, t // POOL_HALO - 1), 0)),
                  pl.BlockSpec(memory_space=pl.ANY)],
        out_specs=pl.BlockSpec((tm, d), lambda i: (i, PPOOL // d)), out_shape=jax.ShapeDtypeStruct(dproj.shape, BF16),
        input_output_aliases={2: 0},
        compiler_params=_cparams(("parallel",)),
    )(q, q, dproj)


def _pool_dwg(pooled, dyg, layer):
    t, d = pooled.shape
    tk = _tile(t, (1056, 384, 128))
    nk = t // tk
    gd = POOL_GDIM
    ng = d // gd

    def body(p_ref, g_ref, o_ref):
        @pl.when(pl.program_id(1) == 0)
        def _():
            o_ref[...] = jnp.zeros_like(o_ref)

        part = _dot(p_ref[...], g_ref[...], _TN)
        for k in range(N_CHIPS):
            o_ref[k] += part[k * WG_ROWS:(k + 1) * WG_ROWS]

    blk = pl.BlockSpec((tk, gd), lambda g, k: (k, g))
    return pl.pallas_call(
        body, name=f"pool_dwg_{layer}", grid=(ng, nk), in_specs=[blk, blk],
        out_specs=pl.BlockSpec((N_CHIPS, None, WG_ROWS, gd), lambda g, k: (0, g, 0, 0)),
        out_shape=jax.ShapeDtypeStruct((N_CHIPS, ng, WG_ROWS, gd), F32),
        compiler_params=_cparams(("parallel", "arbitrary")),
    )(pooled, dyg)


CONV_W = 4
CONV_HALO = 8
XBC_BLK = PXBC // 1024


def _conv_fwd(proj, conv_w, conv_b):
    t = proj.shape[0]
    cw = 1024
    tm = _tile(t, (1056, 384, 128))
    hb = tm // CONV_HALO

    def body(x_ref, halo_ref, w_ref, b_ref, o_ref, pre_ref):
        i = pl.program_id(1)
        x = x_ref[...]
        halo = jnp.where(i > 0, halo_ref[...], 0.0)
        xc = jnp.concatenate([halo, x], axis=0)
        w = w_ref[...]
        acc = b_ref[...] + x * w[CONV_W - 1:CONV_W, :]
        for k in range(CONV_W - 1):
            acc = acc + pltpu.roll(xc, CONV_W - 1 - k, axis=0)[CONV_HALO:] * w[k:k + 1, :]
        row = i * tm + lax.broadcasted_iota(jnp.int32, (tm, 1), 0)
        pre_ref[...] = acc
        o_ref[...] = jnp.where(row >= N_PAD, acc * _sigmoid(acc), 0.0)

    blk = pl.BlockSpec((tm, cw), lambda j, i: (i, j))
    return pl.pallas_call(
        body, name="conv_fwd", grid=(D_XBC // cw, t // tm),
        in_specs=[pl.BlockSpec((tm, cw), lambda j, i: (i, XBC_BLK + j)),
                  pl.BlockSpec((CONV_HALO, cw), lambda j, i: (jnp.maximum(i * hb - 1, 0), XBC_BLK + j)),
                  pl.BlockSpec((CONV_W, cw), lambda j, i: (0, j)), pl.BlockSpec((1, cw), lambda j, i: (0, j))],
        out_specs=[blk, blk],
        out_shape=[jax.ShapeDtypeStruct((t, D_XBC), F32), jax.ShapeDtypeStruct((t, D_XBC), F32)],
        compiler_params=_cparams(("parallel", "parallel")),
    )(proj, proj, conv_w, conv_b.reshape(1, D_XBC))


def _conv_bwd(dxa, coff, proj, pre, conv_w, dproj):
    t, ncols = dxa.shape
    cw = 1024
    tm = _tile(t, (528, 384, 128))
    hb = tm // CONV_HALO
    nt = t // tm
    n = tm + CONV_HALO

    def body(d_ref, dn_ref, p_ref, pn_ref, x_ref, w_ref, _, o_ref, dw_ref, db_ref):
        i = pl.program_id(1)
        last = i == nt - 1
        xc = jnp.concatenate([p_ref[...], pn_ref[...]], axis=0)
        df = jnp.concatenate([d_ref[...], dn_ref[...]], axis=0)
        w = w_ref[...]
        sig = _sigmoid(xc)
        row = i * tm + lax.broadcasted_iota(jnp.int32, (n, 1), 0)
        live = (row >= N_PAD) & ((row < (i + 1) * tm) | jnp.logical_not(last))
        dxc = jnp.where(live, df * (sig * (1.0 + xc * (1.0 - sig))), 0.0)
        ahead = [pltpu.roll(dxc, n - (CONV_W - 1 - k), axis=0)[:tm] for k in range(CONV_W - 1)] + [dxc[:tm]]
        acc = ahead[0] * w[0:1, :]
        for k in range(1, CONV_W):
            acc = acc + ahead[k] * w[k:k + 1, :]
        o_ref[...] = acc.astype(BF16)

        @pl.when(i == 0)
        def _():
            dw_ref[...] = jnp.zeros_like(dw_ref)
            db_ref[...] = jnp.zeros_like(db_ref)

        x = x_ref[...]
        db_ref[...] += jnp.sum(ahead[CONV_W - 1], axis=0, keepdims=True)
        for k in range(CONV_W):
            dw_ref[k:k + 1, :] += jnp.sum(ahead[k] * x, axis=0, keepdims=True)

    def pspec(rows, fn):
        return pl.BlockSpec((rows, cw), lambda j, i: (fn(i), coff + j))

    nxt = lambda i: jnp.minimum((i + 1) * hb, t // CONV_HALO - 1)
    dxbc, dw, db = pl.pallas_call(
        body, name=f"conv_bwd_{coff}", grid=(ncols // cw, nt),
        in_specs=[pl.BlockSpec((tm, cw), lambda j, i: (i, j)), pl.BlockSpec((CONV_HALO, cw), lambda j, i: (nxt(i), j)),
                  pspec(tm, lambda i: i), pspec(CONV_HALO, nxt),
                  pl.BlockSpec((tm, cw), lambda j, i: (i, XBC_BLK + coff + j)),
                  pl.BlockSpec((CONV_W, cw), lambda j, i: (0, coff + j)),
                  pl.BlockSpec(memory_space=pl.ANY)],
        out_specs=[pl.BlockSpec((tm, cw), lambda j, i: (i, XBC_BLK + coff + j)), pl.BlockSpec((CONV_W, cw), lambda j, i: (0, j)),
                   pl.BlockSpec((1, cw), lambda j, i: (0, j))],
        out_shape=[jax.ShapeDtypeStruct(dproj.shape, BF16), jax.ShapeDtypeStruct((CONV_W, ncols), F32),
                   jax.ShapeDtypeStruct((1, ncols), F32)],
        input_output_aliases={6: 0},
        compiler_params=_cparams(("parallel", "arbitrary")),
    )(dxa, dxa, pre, pre, proj, conv_w, dproj)
    return dxbc, dw, db.reshape(ncols)


def _cumsum(x, axis, reverse=False):
    n = x.shape[axis]
    idx = lax.broadcasted_iota(jnp.int32, x.shape, axis)
    k = 1
    while k < n:
        if reverse:
            x = x + jnp.where(idx < n - k, pltpu.roll(x, n - k, axis=axis), 0.0)
        else:
            x = x + jnp.where(idx >= k, pltpu.roll(x, k, axis=axis), 0.0)
        k *= 2
    return x


def _head_masks():
    lane = lax.broadcasted_iota(jnp.int32, (1, GROUP_W), 1)
    return [(lane >= r * HEAD_DIM) & (lane < (r + 1) * HEAD_DIM) for r in range(HEADS_PER_GROUP)]


def _expand_heads(cols, hm):
    out = jnp.where(hm[0], cols[:, 0:1], 0.0)
    for r in range(1, HEADS_PER_GROUP):
        out = out + jnp.where(hm[r], cols[:, r:r + 1], 0.0)
    return out


def _ssd_decay(dt_raw, dt_bias, a_log):
    t = dt_raw.shape[0]

    hpg = HEADS_PER_GROUP

    def body(raw_ref, b_ref, al_ref, dt_ref, acs_ref, sig_ref, rows_ref):
        raw = raw_ref[...] + b_ref[...]
        rowid = pl.program_id(0) * CHUNK + lax.broadcasted_iota(jnp.int32, (CHUNK, 1), 0)
        dt = jnp.where(rowid >= N_PAD, _softplus(raw), 0.0)
        acs = _cumsum(dt * -jnp.exp(al_ref[...]), 0)
        sig = _sigmoid(raw)
        acs_t = acs.T
        for g in range(N_GROUPS):
            heads = slice(g * hpg, (g + 1) * hpg)
            dt_ref[g] = dt[:, heads]
            acs_ref[g] = acs[:, heads]
            sig_ref[g] = sig[:, heads]
            rows_ref[g] = acs_t[heads, :]

    blk = pl.BlockSpec((CHUNK, DT_PAD), lambda c: (c, 0))
    vec = pl.BlockSpec((1, DT_PAD), lambda c: (0, 0))
    cols = pl.BlockSpec((N_GROUPS, CHUNK, hpg), lambda c: (0, c, 0))
    pad = lambda v: jnp.pad(v, (0, DT_PAD - N_HEADS)).reshape(1, DT_PAD)
    dt, acs, sig, rows = pl.pallas_call(
        body, name="ssd_decay", grid=(t // CHUNK,), in_specs=[blk, vec, vec],
        out_specs=[cols, cols, cols, pl.BlockSpec((N_GROUPS, hpg, CHUNK), lambda c: (0, 0, c))],
        out_shape=[jax.ShapeDtypeStruct((N_GROUPS, t, hpg), F32)] * 3 + [jax.ShapeDtypeStruct((N_GROUPS, hpg, t), F32)],
        compiler_params=_cparams(("parallel",)),
    )(dt_raw, pad(dt_bias), pad(a_log))
    return dict(dt=dt, acs=acs, acs_rows=rows, sig=sig, a_log=a_log.reshape(N_GROUPS, 1, hpg))


def _ssd_q(dtc, acs4, acs_r):
    hm = _head_masks()
    dt_exp = _expand_heads(dtc, hm)
    acs = _expand_heads(acs4, hm)
    atot = acs[CHUNK - 1:CHUNK, :]
    return dict(dtc=dtc, hm=hm, dt_exp=dt_exp, acs=acs, acs_r=acs_r,
                ea=jnp.exp(acs), ds=jnp.exp(atot - acs), dec=jnp.exp(atot))


def _stack4(x):
    return jnp.concatenate([x] * HEADS_PER_GROUP, axis=0)


def _ssd_decay_stacks(q):
    hpg = HEADS_PER_GROUP
    a_col = jnp.concatenate([q["acs"][:, r * HEAD_DIM:r * HEAD_DIM + 1] for r in range(hpg)], axis=0)
    a_row = jnp.concatenate([jnp.broadcast_to(q["acs_r"][r:r + 1, :], (CHUNK, CHUNK)) for r in range(hpg)], axis=0)
    ri = lax.broadcasted_iota(jnp.int32, (hpg * CHUNK, CHUNK), 0) % CHUNK
    ci = lax.broadcasted_iota(jnp.int32, (hpg * CHUNK, CHUNK), 1)
    diff = a_col - a_row
    lm = jnp.exp(jnp.where(ri >= ci, diff, -jnp.inf))
    lt = jnp.exp(jnp.where(ri <= ci, -diff, -jnp.inf))
    return lm, lt


def _pick_heads(stacked, hm):
    out = jnp.where(hm[0], stacked[:CHUNK], 0.0)
    for r in range(1, HEADS_PER_GROUP):
        out = out + jnp.where(hm[r], stacked[r * CHUNK:(r + 1) * CHUNK], 0.0)
    return out


def _mask_heads(x, hm):
    return jnp.concatenate([jnp.where(hm[r], x, 0.0) for r in range(HEADS_PER_GROUP)], axis=0)


SSD_GP_FWD, SSD_GP_BWD = 8, 4


def _ssd_specs(cidx, gp):
    hpg = HEADS_PER_GROUP
    return [
        pl.BlockSpec((CHUNK, gp * GROUP_W), lambda g, c: (cidx(c), g)),
        pl.BlockSpec((CHUNK, gp * D_STATE), lambda g, c: (cidx(c), D_INNER // (gp * D_STATE) + g)),
        pl.BlockSpec((CHUNK, gp * D_STATE), lambda g, c: (cidx(c), (D_INNER + 1024) // (gp * D_STATE) + g)),
        pl.BlockSpec((gp, CHUNK, hpg), lambda g, c: (g, cidx(c), 0)),
        pl.BlockSpec((gp, CHUNK, hpg), lambda g, c: (g, cidx(c), 0)),
        pl.BlockSpec((gp, hpg, CHUNK), lambda g, c: (g, 0, cidx(c))),
    ]


def _ssd_fwd(xa, dec, carry=None):
    t = xa.shape[0]
    nc = t // CHUNK
    gp, gw, ds = SSD_GP_FWD, GROUP_W, D_STATE
    grid = (N_GROUPS // gp, nc)
    c_in, c_out, c_sem = carry.specs() if carry is not None else ([], [], [])
    nci, nco = len(c_in), len(c_out)

    def body(*refs):
        xs_ref, b_ref, c_ref, dt_ref, acs_ref, acsr_ref = refs[:6]
        y_ref, prev_ref = refs[6 + nci:8 + nci]
        st_ref = refs[8 + nci + nco]
        if carry is not None:
            begin, end = carry.at_edges(grid, refs[6:6 + nci], refs[8 + nci:8 + nci + nco], refs[-2:])
            begin()

        @pl.when(pl.program_id(1) == 0)
        def _():
            st_ref[...] = jnp.zeros_like(st_ref)

        for k in range(gp):
            q = _ssd_q(dt_ref[k], acs_ref[k], acsr_ref[k])
            xdt = xs_ref[:, k * gw:(k + 1) * gw] * q["dt_exp"]
            bm = b_ref[:, k * ds:(k + 1) * ds].astype(BF16)
            cm = c_ref[:, k * ds:(k + 1) * ds].astype(BF16)
            cb = _dot(cm, bm, _NT)
            st = st_ref[k]
            prev_ref[0, k] = st
            lm, _ = _ssd_decay_stacks(q)
            y_diag = _pick_heads(_dot((_stack4(cb) * lm).astype(BF16), xdt.astype(BF16)), q["hm"])
            y_ref[:, k * gw:(k + 1) * gw] = y_diag + _dot(cm, st.astype(BF16)) * q["ea"]
            st_ref[k] = q["dec"] * st + _dot(bm, (xdt * q["ds"]).astype(BF16), _TN)
        if carry is not None:
            end()

    res = pl.pallas_call(
        body, name="ssd_fwd", grid=grid,
        in_specs=_ssd_specs(lambda c: c, gp) + c_in,
        out_specs=[pl.BlockSpec((CHUNK, gp * gw), lambda g, c: (c, g)),
                   pl.BlockSpec((1, gp, ds, gw), lambda g, c: (c, g, 0, 0))] + c_out,
        out_shape=[jax.ShapeDtypeStruct((t, D_INNER), F32), jax.ShapeDtypeStruct((nc, N_GROUPS, ds, gw), F32)]
        + (carry.outs if carry is not None else []),
        scratch_shapes=[pltpu.VMEM((gp, ds, gw), F32)] + c_sem,
        input_output_aliases=carry.aliases(6, 2) if carry is not None else {},
        compiler_params=_cparams(("arbitrary", "arbitrary") if carry is not None else ("parallel", "arbitrary")),
    )(xa, xa, xa, dec["dt"], dec["acs"], dec["acs_rows"], *(carry.ins if carry is not None else []))
    return res[0], res[1], res[2:]


def _ssd_bwd(dy, dsk, xa, prev, dec, carry=None):
    t = xa.shape[0]
    nc = t // CHUNK
    gp, gw, dstate = SSD_GP_BWD, GROUP_W, D_STATE
    hpg = HEADS_PER_GROUP

    grid = (N_GROUPS // gp, nc)
    c_in, c_out, c_sem = carry.specs() if carry is not None else ([], [], [])
    nci, nco = len(c_in), len(c_out)

    def body(*refs):
        ins, c_ins = refs[:11], refs[11:11 + nci]
        outs = refs[11 + nci:17 + nci]
        c_outs = refs[17 + nci:17 + nci + nco]
        dst_ref = refs[17 + nci + nco]
        dbias_ref, dalog_ref = outs[4], outs[5]
        if carry is not None:
            begin, end = carry.at_edges(grid, c_ins, c_outs, refs[-2:])
            begin()

        @pl.when(pl.program_id(1) == 0)
        def _():
            dst_ref[...] = jnp.zeros_like(dst_ref)
            dbias_ref[...] = jnp.zeros_like(dbias_ref)
            dalog_ref[...] = jnp.zeros_like(dalog_ref)

        for k in range(gp):
            one_group(k, *ins, *outs, dst_ref)
        if carry is not None:
            end()

    def one_group(k, xs_ref, b_ref, c_ref, dt_ref, acs_ref, acsr_ref, sig_ref, al_ref, dy_ref, sk_ref, prev_ref,
                  dxs_ref, db_ref, dc_ref, ddt_ref, dbias_ref, dalog_ref, dst_ref):
        wide = slice(k * gw, (k + 1) * gw)
        narrow = slice(k * dstate, (k + 1) * dstate)
        q = _ssd_q(dt_ref[k], acs_ref[k], acsr_ref[k])
        a_r = -jnp.exp(al_ref[k])
        hm, ds, dec, dt_exp = q["hm"], q["ds"], q["dec"], q["dt_exp"]
        xs = xs_ref[:, wide]
        xdt = xs * dt_exp
        xdtb = xdt.astype(BF16)
        bm = b_ref[:, narrow].astype(BF16)
        cm = c_ref[:, narrow].astype(BF16)
        cb = _dot(cm, bm, _NT)
        bc = _dot(bm, cm, _NT)
        dyv = dy_ref[:, wide]
        dye = (dyv * q["ea"]).astype(BF16)
        pst = prev_ref[0, k]
        dst = dst_ref[k]
        dstb = dst.astype(BF16)
        dx_state = ds * _dot(bm, dstb)
        onehot = [(lax.broadcasted_iota(jnp.int32, (1, hpg), 1) == r).astype(F32) for r in range(hpg)]
        lm, lt = _ssd_decay_stacks(q)
        dyb = dyv.astype(BF16)
        gl = _dot(_mask_heads(dyv, hm).astype(BF16), xdtb, _NT) * lm
        glt = _dot(_mask_heads(xdt, hm).astype(BF16), dyb, _NT) * lt
        bc4 = _stack4(bc)
        dxdt = dx_state + _pick_heads(_dot((bc4 * lt).astype(BF16), dyb), hm)
        wd = jnp.sum(gl * _stack4(cb), axis=1, keepdims=True) - jnp.sum(glt * bc4, axis=1, keepdims=True)
        dcb = gl[:CHUNK]
        dcbt = glt[:CHUNK]
        qa = wd[:CHUNK] * onehot[0]
        for r in range(1, hpg):
            dcb = dcb + gl[r * CHUNK:(r + 1) * CHUNK]
            dcbt = dcbt + glt[r * CHUNK:(r + 1) * CHUNK]
            qa = qa + wd[r * CHUNK:(r + 1) * CHUNK] * onehot[r]
        pstb = pst.astype(BF16)
        dc_ref[:, narrow] = _dot(dcb.astype(BF16), bm) + _dot(dye, pstb, _NT)
        db_ref[:, narrow] = _dot(dcbt.astype(BF16), cm) + _dot((xdt * ds).astype(BF16), dstb, _NT)
        dst_ref[k] = dec * dst + _dot(cm, dye, _TN)
        dxs_ref[:, wide] = dxdt * dt_exp + dyv * sk_ref[:, wide]

        t2 = xdt * dx_state
        t1 = dyv * (_dot(cm, pstb) * q["ea"]) - t2
        t4 = dxdt * xs
        last_row = jnp.sum(t2, axis=0, keepdims=True) + dec * jnp.sum(dst * pst, axis=0, keepdims=True)
        xd = jnp.zeros((CHUNK, hpg), F32)
        dal = jnp.zeros((1, hpg), F32)
        for r in range(hpg):
            qa = qa + jnp.sum(jnp.where(hm[r], t1, 0.0), axis=1, keepdims=True) * onehot[r]
            xd = xd + jnp.sum(jnp.where(hm[r], t4, 0.0), axis=1, keepdims=True) * onehot[r]
            dal = dal + jnp.sum(jnp.where(hm[r], last_row, 0.0), axis=1, keepdims=True) * onehot[r]
        rc = _cumsum(qa, 0, reverse=True) + dal
        rowid = (nc - 1 - pl.program_id(1)) * CHUNK + lax.broadcasted_iota(jnp.int32, (CHUNK, 1), 0)
        ddt_raw = jnp.where(rowid >= N_PAD, (rc * a_r + xd) * sig_ref[k], 0.0)
        ddt_ref[k] = ddt_raw
        dbias_ref[k] += jnp.sum(ddt_raw, axis=0, keepdims=True)
        dalog_ref[k] += jnp.sum(rc * q["dtc"], axis=0, keepdims=True) * a_r

    rev = lambda c: nc - 1 - c
    blk = pl.BlockSpec((CHUNK, gp * gw), lambda g, c: (rev(c), g))
    nblk = pl.BlockSpec((CHUNK, gp * dstate), lambda g, c: (rev(c), g))
    cols = pl.BlockSpec((gp, CHUNK, hpg), lambda g, c: (g, rev(c), 0))
    small = pl.BlockSpec((gp, 1, hpg), lambda g, c: (g, 0, 0))
    res = pl.pallas_call(
        body, name="ssd_bwd", grid=grid,
        in_specs=_ssd_specs(rev, gp) + [cols, small, blk, pl.BlockSpec((1, gp * gw), lambda g, c: (0, g)),
                                    pl.BlockSpec((1, gp, dstate, gw), lambda g, c: (rev(c), g, 0, 0))]
        + c_in,
        out_specs=[blk, nblk, nblk, cols, small, small] + c_out,
        out_shape=[jax.ShapeDtypeStruct((t, D_INNER), F32), jax.ShapeDtypeStruct((t, N_GROUPS * dstate), F32),
                   jax.ShapeDtypeStruct((t, N_GROUPS * dstate), F32), jax.ShapeDtypeStruct((N_GROUPS, t, hpg), F32),
                   jax.ShapeDtypeStruct((N_GROUPS, 1, hpg), F32), jax.ShapeDtypeStruct((N_GROUPS, 1, hpg), F32)]
        + (carry.outs if carry is not None else []),
        scratch_shapes=[pltpu.VMEM((gp, dstate, gw), F32)] + c_sem,
        compiler_params=_cparams(("arbitrary", "arbitrary") if carry is not None else ("parallel", "arbitrary")),
    )(xa, xa, xa, dec["dt"], dec["acs"], dec["acs_rows"], dec["sig"], dec["a_log"], dy, dsk.reshape(1, D_INNER), prev,
      *(carry.ins if carry is not None else []))
    dxs, db, dc, ddt, dbias, dalog = res[:6]
    ddt_raw = ddt.transpose(1, 0, 2).reshape(t, N_HEADS)
    return dxs, db, dc, ddt_raw, dbias.reshape(N_HEADS), dalog.reshape(N_HEADS), res[6:]


def _ssd_post_fwd(y, xa, proj, dsk, nw):
    t = y.shape[0]
    di = D_INNER
    tm = _tile(t, (384, 128))

    def body(y_ref, xs_ref, z_ref, dsk_ref, nw_ref, o_ref):
        z = z_ref[...]
        yz = (y_ref[...] + xs_ref[...] * dsk_ref[...]) * (z * _sigmoid(z))
        nwv = nw_ref[...]
        for g in range(N_GROUPS):
            sl = slice(g * GROUP_W, (g + 1) * GROUP_W)
            v = yz[:, sl]
            rg = lax.rsqrt(jnp.mean(v * v, axis=-1, keepdims=True) + EPS)
            o_ref[:, sl] = (v * rg * nwv[:, sl]).astype(BF16)

    row = pl.BlockSpec((tm, di), lambda i: (i, 0))
    vec = pl.BlockSpec((1, di), lambda i: (0, 0))
    return pl.pallas_call(
        body, name="ssd_post_fwd", grid=(t // tm,),
        in_specs=[row, row, row, vec, vec], out_specs=row, out_shape=jax.ShapeDtypeStruct((t, di), BF16),
        compiler_params=_cparams(("parallel",)),
    )(y, xa, proj, dsk.reshape(1, di), nw.reshape(1, di))


def _ssd_post_bwd(dyn, y, xa, proj, dsk, nw, dproj):
    t = y.shape[0]
    di = D_INNER
    tm = CHUNK

    def body(dyn_ref, y_ref, xs_ref, z_ref, dsk_ref, nw_ref, _, dy_ref, dz_ref, dnw_ref, ddsk_ref):
        @pl.when(pl.program_id(0) == 0)
        def _():
            dnw_ref[...] = jnp.zeros_like(dnw_ref)
            ddsk_ref[...] = jnp.zeros_like(ddsk_ref)

        for g in range(N_GROUPS):
            sl = slice(g * GROUP_W, (g + 1) * GROUP_W)
            z = z_ref[:, sl]
            sig = _sigmoid(z)
            s = z * sig
            xs = xs_ref[:, sl]
            dskv = dsk_ref[:, sl]
            yt = y_ref[:, sl] + xs * dskv
            yz = yt * s
            rg = lax.rsqrt(jnp.mean(yz * yz, axis=-1, keepdims=True) + EPS)
            xhat = yz * rg
            dynv = dyn_ref[:, sl]
            gg = dynv * nw_ref[:, sl]
            dnw_ref[:, sl] += jnp.sum(dynv * xhat, axis=0, keepdims=True)
            dyz = rg * (gg - xhat * jnp.mean(gg * xhat, axis=-1, keepdims=True))
            dyt = dyz * s
            dy_ref[:, sl] = dyt
            dz_ref[:, sl] = (dyz * yt * (sig * (1.0 + z * (1.0 - sig)))).astype(BF16)
            ddsk_ref[:, sl] += jnp.sum(dyt * xs, axis=0, keepdims=True)

    row = pl.BlockSpec((tm, di), lambda i: (i, 0))
    vec = pl.BlockSpec((1, di), lambda i: (0, 0))
    dy, dz, dnw, ddsk = pl.pallas_call(
        body, name="ssd_post_bwd", grid=(t // tm,),
        in_specs=[row, row, row, row, vec, vec, pl.BlockSpec(memory_space=pl.ANY)], out_specs=[row, row, vec, vec],
        out_shape=[jax.ShapeDtypeStruct((t, di), F32), jax.ShapeDtypeStruct(dproj.shape, BF16),
                   jax.ShapeDtypeStruct((1, di), F32), jax.ShapeDtypeStruct((1, di), F32)],
        input_output_aliases={6: 1},
        compiler_params=_cparams(("arbitrary",)),
    )(dyn, y, xa, proj, dsk.reshape(1, di), nw.reshape(1, di), dproj)
    return dy, dz, dnw.reshape(di), ddsk.reshape(N_HEADS, HEAD_DIM).sum(axis=1)


SHARD_COLS = IN_COLS // N_CHIPS


def _split_w_in(w_in_sh):
    nl = w_in_sh.shape[0]
    w = w_in_sh.transpose(0, 2, 1, 3).reshape(nl, D_MODEL, IN_COLS)
    main = jnp.concatenate([w[..., OFF_Z:OFF_XBC], w[..., OFF_GATE:], w[..., :OFF_Z], w[..., OFF_XBC:OFF_DT]], axis=-1)
    dt = jnp.pad(w[..., OFF_DT:OFF_GATE], ((0, 0), (0, 0), (0, DT_PAD - N_HEADS)))
    return main, dt


def _merge_dw_in(dmain, ddt):
    full = jnp.concatenate([dmain[:, PPOOL:PXBC], dmain[:, PZ:PGATE], dmain[:, PXBC:], ddt[:, :N_HEADS], dmain[:, PGATE:PPOOL]],
                           axis=1)
    return full.reshape(D_MODEL, N_CHIPS, SHARD_COLS).transpose(1, 0, 2)


def _relu2_epilogue(acc):
    hid = jnp.maximum(acc, 0.0)
    return acc, hid * hid


def _relu2_bwd_epilogue(acc, pre):
    return (acc * 2.0 * jnp.maximum(pre, 0.0),)


def _add_epilogue(acc, res):
    return (acc + res,)


def _add_norm_epilogue(acc, res, w):
    x = acc + res
    r = lax.rsqrt(jnp.mean(x * x, axis=-1, keepdims=True) + EPS)
    return x, x * r * w


def _w_spec(rows, cols, fn):
    return pl.BlockSpec((None, None, rows, cols), fn)


def _with_carry(stage, call):
    carry = stage[0]() if stage is not None else None
    res = call(carry)
    if carry is None:
        return res
    outs, got = res
    stage[1](got)
    return outs[0] if len(outs) == 1 else outs


def _layer_fwd(h, u, w, big, layer, stages=None, next_norm_w=None):
    t = h.shape[0]
    d = D_MODEL
    tm = _tile(t, (1408, 384, 128))
    nt = t // tm
    row = _bs((tm, d), lambda i, j, k: (i, 0))
    vec = _bs((1, d), lambda i, j, k: (0, 0))
    stages = stages or {}
    s = {"h": h}
    if u is None:
        u = _norm_fwd(h, w["mix_norm_w"], "norm_mix")
    proj = _with_carry(stages.get("a_ici"), lambda carry: _mm(
        f"mm_proj_{layer}", _NN, (nt, PCOLS // d, 1), u, row,
        big["w_in_main"], _bs((None, d, d), lambda i, j, k: (0, 0, j)),
        [((t, PCOLS), F32, _bs((tm, d), lambda i, j, k: (i, j)))], carry=carry))
    dt_raw = _mm(f"mm_dt_{layer}", _NN, (nt, 1, 1), u, row,
                 big["w_in_dt"], _bs((None, d, DT_PAD), lambda i, j, k: (0, 0, 0)),
                 [((t, DT_PAD), F32, _bs((tm, DT_PAD), lambda i, j, k: (i, 0)))])
    pooled, yg, ypm, y_pool = _pool_fwd(proj, big["pool_w_group"], w["pool_scale"], big["w_pool_up"], layer)
    xa, conv_pre = _conv_fwd(proj, w["conv_w"], w["conv_b"])
    dec = _ssd_decay(dt_raw, w["dt_bias"], w["a_log"])
    stage = stages.get("b_ici")
    carry = stage[0]() if stage is not None else None
    y, prev, got = _ssd_fwd(xa, dec, carry)
    if carry is not None:
        stage[1](got)
    yn = _ssd_post_fwd(y, xa, proj, w["d_skip_exp"], w["ssd_norm_w"])
    rs = D_INNER // N_CHIPS
    y_ssd = _with_carry(stages.get("a_d2d"), lambda carry: _mm(
        f"mm_ssd_out_{layer}", _NN, (nt, 1, 1), yn, _bs((tm, D_INNER), lambda i, j, k: (i, 0)),
        big["w_ssd_out"], _bs((None, N_CHIPS, rs, d), lambda i, j, k: (0, 0, 0, 0)), [((t, d), F32, row)], carry=carry))
    merged = _gate_fwd(proj, w["b_gate"], y_pool, y_ssd)
    ro = d // N_CHIPS
    h1, v = _mm(f"mm_o_{layer}", _NN, (nt, 1, 1), merged, row,
                big["w_o"], _bs((None, N_CHIPS, ro, d), lambda i, j, k: (0, 0, 0, 0)),
                [((t, d), F32, row), ((t, d), BF16, row)],
                extras=[(h, row), (w["mlp_norm_w"].reshape(1, d), vec)], epilogue=_add_norm_epilogue)
    tile = _bs((tm, d), lambda i, j, k: (i, j))
    pre, act = _with_carry(stages.get("b_d2d"), lambda carry: _mm(
        f"mm_ff1_{layer}", _NN, (nt, N_CHIPS, 1), v, row,
        big["w_ff1"], _w_spec(d, d, lambda i, j, k: (0, j, 0, 0)),
        [((t, D_FF), F32, tile), ((t, D_FF), BF16, tile)], epilogue=_relu2_epilogue, carry=carry))
    a_spec = _bs((tm, d), lambda i, j, k: (i, k))
    b_spec = _w_spec(d, d, lambda i, j, k: (0, k, 0, 0))
    if next_norm_w is None:
        h2 = _mm(f"mm_ff2_{layer}", _NN, (nt, 1, N_CHIPS), act, a_spec, big["w_ff2"], b_spec, [((t, d), F32, row)],
                 extras=[(h1, row)], epilogue=_add_epilogue)
        u_next = None
    else:
        h2, u_next = _mm(f"mm_ff2_{layer}", _NN, (nt, 1, N_CHIPS), act, a_spec, big["w_ff2"], b_spec,
                         [((t, d), F32, row), ((t, d), BF16, row)],
                         extras=[(h1, row), (next_norm_w.reshape(1, d), vec)], epilogue=_add_norm_epilogue)
    s.update(u=u, proj=proj, dec=dec, conv_pre=conv_pre, pooled=pooled, yg=yg, ypm=ypm, y_pool=y_pool, xa=xa, y=y, prev=prev, yn=yn,
             y_ssd=y_ssd, merged=merged, h1=h1, v=v, pre=pre, act=act)
    return h2, u_next, s


def _dw(name, layer, a, b, shard_shape, by, tk, carry=None):
    t = a.shape[0]
    nk = t // tk
    rows, cols = shard_shape
    if by == "rows":
        grid = (N_CHIPS, 1, nk)
        a_spec = _bs((tk, rows), lambda i, j, k: (k, i))
        b_spec = _bs((tk, cols), lambda i, j, k: (k, 0))
        o_spec = _bs((None, rows, cols), lambda i, j, k: (i, 0, 0))
    else:
        grid = (1, N_CHIPS, nk)
        a_spec = _bs((tk, rows), lambda i, j, k: (k, 0))
        b_spec = _bs((tk, cols), lambda i, j, k: (k, j))
        o_spec = _bs((None, rows, cols), lambda i, j, k: (j, 0, 0))
    return _mm(f"{name}_{layer}", _TN, grid, a, a_spec, b, b_spec, [((N_CHIPS, rows, cols), F32, o_spec)], carry=carry)


EARLY = ("w_ff2", "w_ff1", "w_o", "w_ssd_out", "w_pool_up", "pool_w_group")


def _layer_bwd(dh, s, w, big, layer, red=None, last=False):
    t = dh.shape[0]
    d = D_MODEL
    tm = _tile(t, (1408, 384, 128))
    tk = _tile(t, (1408, 384, 128))
    nt = t // tm
    row = _bs((tm, d), lambda i, j, k: (i, 0))
    tile = _bs((tm, d), lambda i, j, k: (i, j))
    g, gb = {}, {}
    carry = red.swap_carry() if red is not None else None
    dpre = _mm(f"mm_dact_{layer}", _NT, (nt, N_CHIPS, 1), dh, row,
               big["w_ff2"], _w_spec(d, d, lambda i, j, k: (0, j, 0, 0)), [((t, D_FF), BF16, tile)],
               extras=[(s["pre"], tile)], epilogue=_relu2_bwd_epilogue, carry=carry)
    if carry is not None:
        (dpre,), got = dpre
        red.after_swap(got)
    gb["w_ff2"] = _dw("mm_dw_ff2", layer, s["act"], dh, (d, d), "rows", tk)
    dv = _mm(f"mm_dv_{layer}", _NT, (nt, 1, N_CHIPS), dpre, _bs((tm, d), lambda i, j, k: (i, k)),
             big["w_ff1"], _w_spec(d, d, lambda i, j, k: (0, k, 0, 0)), [((t, d), F32, row)])
    gb["w_ff1"] = _dw("mm_dw_ff1", layer, s["v"], dpre, (d, d), "cols", tk)
    dh1, g["mlp_norm_w"] = _norm_bwd(dh, dv, s["h1"], w["mlp_norm_w"], "norm_mlp_bwd")
    ro = d // N_CHIPS
    dmerged = _mm(f"mm_dmerged_{layer}", _NT, (nt, N_CHIPS, 1), dh1, row,
                  big["w_o"], _w_spec(ro, d, lambda i, j, k: (0, j, 0, 0)),
                  [((t, d), F32, _bs((tm, ro), lambda i, j, k: (i, j)))])
    gb["w_o"] = _dw("mm_dw_o", layer, s["merged"], dh1, (ro, d), "rows", tk)
    dy_pool, dy_ssd, dproj, g["b_gate"] = _gate_bwd(dmerged, s["proj"], w["b_gate"], s["y_pool"], s["y_ssd"])
    rs = D_INNER // N_CHIPS
    dyn = _mm(f"mm_dyn_{layer}", _NT, (nt, N_CHIPS, 1), dy_ssd, row,
              big["w_ssd_out"], _w_spec(rs, d, lambda i, j, k: (0, j, 0, 0)),
              [((t, D_INNER), F32, _bs((tm, rs), lambda i, j, k: (i, j)))])
    gb["w_ssd_out"] = _dw("mm_dw_ssd_out", layer, s["yn"], dy_ssd, (rs, d), "rows", tk)
    dy, dproj, g["ssd_norm_w"], g["d_skip"] = _ssd_post_bwd(dyn, s["y"], s["xa"], s["proj"], w["d_skip_exp"],
                                                            w["ssd_norm_w"], dproj)
    carry = red.ici_carry() if red is not None else None
    dxs, db, dc, ddt_raw, g["dt_bias"], g["a_log"], got = _ssd_bwd(dy, w["d_skip_exp"], s["xa"], s["prev"], s["dec"], carry)
    if carry is not None:
        red.after_ici(got)
    own = red is not None and last
    dproj, dcw1, dcb1 = _conv_bwd(dxs, 0, s["proj"], s["conv_pre"], w["conv_w"], dproj)
    dproj, dcw2, dcb2 = _conv_bwd(db, 2, s["proj"], s["conv_pre"], w["conv_w"], dproj)
    dproj, dcw3, dcb3 = _conv_bwd(dc, 3, s["proj"], s["conv_pre"], w["conv_w"], dproj)
    g["conv_w"] = jnp.concatenate([dcw1, dcw2, dcw3], axis=1)
    g["conv_b"] = jnp.concatenate([dcb1, dcb2, dcb3])
    q, dyg, g["pool_scale"] = _pool_bwd_a(dy_pool, s["yg"], w["pool_scale"], big["pool_w_group"], big["w_pool_up"], layer)
    gb["w_pool_up"] = _dw("mm_dw_pool_up", layer, s["ypm"], dy_pool, (ro, d), "rows", tk)
    gb["pool_w_group"] = _pool_dwg(s["pooled"], dyg, layer).reshape(N_CHIPS, POOL_GDIM, POOL_GDIM)
    dproj = _pool_bwd_b(q, dproj)
    if own:
        red.push(layer, {n: gb[n] for n in EARLY})
    ddt = jnp.pad(ddt_raw.astype(BF16), ((0, 0), (0, DT_PAD - N_HEADS)))
    nk = PCOLS // d
    du = _with_carry((red.swap_carry, red.after_swap) if own else None, lambda carry: _mm(
        f"mm_du_{layer}", _NT, (nt, 1, nk), dproj, _bs((tm, d), lambda i, j, k: (i, k)),
        big["w_in_main"], _bs((None, d, d), lambda i, j, k: (0, 0, k)), [((t, d), F32, row)], carry=carry))
    du = _mm(f"mm_du_dt_{layer}", _NT, (nt, 1, 1), ddt, _bs((tm, DT_PAD), lambda i, j, k: (i, 0)),
             big["w_in_dt"], _bs((None, d, DT_PAD), lambda i, j, k: (0, 0, 0)), [((t, d), F32, row)],
             extras=[(du, row)], epilogue=_add_epilogue)
    ntk = t // tk
    u_spec = _bs((tk, d), lambda i, j, k: (k, 0))
    dmain = _with_carry((red.ici_carry, red.after_ici) if own else None, lambda carry: _mm(
        f"mm_dw_in_{layer}", _TN, (1, nk, ntk), s["u"], u_spec, dproj, _bs((tk, d), lambda i, j, k: (k, j)),
        [((d, PCOLS), F32, _bs((d, d), lambda i, j, k: (0, j)))], carry=carry))
    ddtw = _mm(f"mm_dw_dt_{layer}", _TN, (1, 1, ntk), s["u"], u_spec, ddt, _bs((tk, DT_PAD), lambda i, j, k: (k, 0)),
               [((d, DT_PAD), F32, _bs((d, DT_PAD), lambda i, j, k: (0, 0)))])
    gb["w_in"] = _merge_dw_in(dmain, ddtw)
    dh0, g["mix_norm_w"] = _norm_bwd(dh1, du, s["h"], w["mix_norm_w"], "norm_mix_bwd")
    if red is not None:
        red.push(layer, {n: gb[n] for n in BIG if not (own and n in EARLY)})
    return dh0, g, gb


BIG = ("w_in", "pool_w_group", "w_pool_up", "w_ssd_out", "w_o", "w_ff1", "w_ff2")
PER_LAYER = ("mix_norm_w", "w_in", "b_gate", "pool_w_group", "pool_scale", "w_pool_up", "conv_w", "conv_b", "dt_bias",
             "a_log", "d_skip", "ssd_norm_w", "w_ssd_out", "w_o", "mlp_norm_w", "w_ff1", "w_ff2")


SMALL_PER_LAYER = tuple(n for n in PER_LAYER if n not in BIG)


SHARD_SHAPE = {"w_in": (1024, SHARD_COLS), "pool_w_group": (4, WG_ROWS, POOL_GDIM), "w_pool_up": (WUP_ROWS, D_MODEL),
               "w_ssd_out": (D_INNER // N_CHIPS, D_MODEL), "w_o": (D_MODEL // N_CHIPS, D_MODEL),
               "w_ff1": (D_MODEL, D_FF // N_CHIPS), "w_ff2": (D_FF // N_CHIPS, D_MODEL)}


def _layer_view(mats):
    big = {n: mats[n].reshape((1, N_CHIPS) + SHARD_SHAPE[n]) for n in BIG if n != "w_in"}
    big["w_in_main"], big["w_in_dt"] = _split_w_in(mats["w_in"])
    return big


def _local_step(x, tgt, weights, small, red=None):
    seq = x.shape[0]
    h = jnp.concatenate([jnp.zeros((N_PAD, D_MODEL), F32), small["meta_tokens"], x], axis=0)
    saved, ws, bigs = [], [], []
    u = None
    for i in range(DEPTH):
        w = {n: small[n][i] for n in SMALL_PER_LAYER}
        w["d_skip_exp"] = jnp.repeat(w["d_skip"], HEAD_DIM)
        big = weights.layer(i)
        next_norm_w = small["mix_norm_w"][i + 1] if i + 1 < DEPTH else None
        h, u, s = _layer_fwd(h, u, w, big, i, weights.stages(i), next_norm_w)
        saved.append(s)
        ws.append(w)
        bigs.append(big)
    loss, dh, g_final = _loss_head(h, small["final_norm_w"], tgt)
    layer_g, layer_gb = [None] * DEPTH, [None] * DEPTH
    for i in reversed(range(DEPTH)):
        dh, layer_g[i], layer_gb[i] = _layer_bwd(dh, saved[i], ws[i], bigs[i], i, red, last=(i == 0))
    grads = {n: jnp.stack([layer_g[i][n] for i in range(DEPTH)]) for n in SMALL_PER_LAYER}
    grads["final_norm_w"] = g_final
    grads["meta_tokens"] = dh[N_PAD:ROW_X]
    return loss, dh[ROW_X:ROW_X + seq], layer_gb, grads


MESH = pl.DeviceIdType.MESH
LANES = 128
ANY = pl.BlockSpec(memory_space=pl.ANY)


def _place():
    x, y, c = lax.axis_index("x"), lax.axis_index("y"), lax.axis_index("c")
    chips = [(1 - x, y), (x, 1 - y), (1 - x, 1 - y)]
    return x, y, c, chips


def _remote(src, dst, send_sem, recv_sem, to):
    return pltpu.make_async_remote_copy(src_ref=src, dst_ref=dst, send_sem=send_sem, recv_sem=recv_sem,
                                        device_id=to, device_id_type=MESH)


class _WeightGatherer:
    GROUPS = {"a": ("w_in",), "b": tuple(n for n in BIG if n != "w_in")}

    def __init__(self, mine):
        self.mine = mine
        self.landing = {}
        self.ready = {}

    def _ici(self, layer, group):
        names = self.GROUPS[group]
        srcs = [self.mine[n] for n in names]
        outs = [jax.ShapeDtypeStruct((1, N_CHIPS) + s.shape[1:], BF16) for s in srcs]

        def make(ins, out, send_sems, recv_sems):
            x, y, c, chips = _place()
            me = 2 * x + y
            cps = []
            for p, ref in enumerate(ins):
                half = ref.shape[1] // 2
                rows = pl.ds(c * half, half)
                cps += [_remote(ref.at[layer, rows, :], out[p].at[0, me, rows, :], send_sems.at[3 * p + j],
                                recv_sems.at[3 * p + j], (*chip, c)) for j, chip in enumerate(chips)]
            return cps

        return _Carry(srcs, outs, 3 * len(names), make)

    def _d2d(self, layer, group):
        names = self.GROUPS[group]
        n = len(names)
        bufs = self.landing[(layer, group)]
        outs = [jax.ShapeDtypeStruct(b.shape, b.dtype) for b in bufs]

        def make(ins, out, send_sems, recv_sems):
            x, y, c, chips = _place()
            me = 2 * x + y
            sibling = (x, y, 1 - c)
            cps = []
            for p in range(n):
                half = out[p].shape[2] // 2
                for j, (cx, cy) in enumerate(chips):
                    blk = out[p].at[0, 2 * cx + cy, pl.ds(c * half, half), :]
                    cps.append(_remote(blk, blk, send_sems.at[4 * p + j], recv_sems.at[4 * p + j], sibling))
                cps.append(_remote(ins[n + p].at[layer], out[p].at[0, me], send_sems.at[4 * p + 3],
                                   recv_sems.at[4 * p + 3], sibling))
            return cps

        return _Carry(list(bufs) + [self.mine[m] for m in names], outs, 4 * n, make, alias={p: p for p in range(n)})

    def _landed(self, layer, group, bufs):
        self.landing[(layer, group)] = bufs

    def _done(self, layer, group, bufs):
        self.ready.setdefault(layer, {}).update(zip(self.GROUPS[group], bufs))

    def layer(self, i):
        if i == 0:
            for g in self.GROUPS:
                self._landed(0, g, _run_carry(self._ici(0, g), f"gather_ici_{g}_0"))
            for g in self.GROUPS:
                self._done(0, g, _run_carry(self._d2d(0, g), f"gather_d2d_{g}_0"))
        return _layer_view(self.ready[i])

    def stages(self, i):
        nxt = i + 1
        if nxt == DEPTH:
            return None
        st = {}
        for g in self.GROUPS:
            st[f"{g}_ici"] = (functools.partial(self._ici, nxt, g), functools.partial(self._landed, nxt, g))
            st[f"{g}_d2d"] = (functools.partial(self._d2d, nxt, g), functools.partial(self._done, nxt, g))
        return st


def _exchange_small(v, reduce, name):
    rows_per = v.shape[0]
    vm = pl.BlockSpec(memory_space=pltpu.VMEM)

    def body(v_ref, out_ref, *scratch):
        if reduce:
            land_ref, send_sems, recv_sems, local_sem = scratch
        else:
            land_ref = out_ref
            send_sems, recv_sems, local_sem = scratch
        x, y, c, chips = _place()
        me, sibling = (x, y, c), (x, y, 1 - c)

        def rows(px, py, pc):
            return land_ref.at[4 * px + 2 * py + pc]

        def copy(k, block, to, src=None):
            return _remote(rows(*block) if src is None else src, rows(*block), send_sems.at[k], recv_sems.at[k], to)

        mine = pltpu.make_async_copy(v_ref, rows(*me), local_sem)
        mine.start()
        first = [copy(0, me, sibling, src=v_ref)]
        first += [copy(1 + j, me, (*chip, c), src=v_ref) for j, chip in enumerate(chips)]
        for cp in first:
            cp.start()
        passed = [copy(4 + j, (*chip, c), sibling) for j, chip in enumerate(chips)]
        for j, chip in enumerate(chips):
            copy(1 + j, (*chip, c), me).wait_recv()
            passed[j].start()
        copy(0, sibling, me).wait_recv()
        for j, chip in enumerate(chips):
            copy(4 + j, (*chip, 1 - c), me).wait_recv()
        for cp in first + passed:
            cp.wait_send()
        mine.wait()
        if reduce:
            acc = land_ref[0]
            for d in range(1, 8):
                acc = acc + land_ref[d]
            out_ref[...] = acc

    sems = [pltpu.SemaphoreType.DMA((7,)), pltpu.SemaphoreType.DMA((7,)), pltpu.SemaphoreType.DMA]
    if reduce:
        out_shape = jax.ShapeDtypeStruct((rows_per, LANES), F32)
        scratch = [pltpu.VMEM((8, rows_per, LANES), F32)] + sems
    else:
        out_shape = jax.ShapeDtypeStruct((8, rows_per, LANES), F32)
        scratch = sems
    return pl.pallas_call(
        body, name=name, in_specs=[vm], out_specs=vm, out_shape=out_shape, scratch_shapes=scratch,
        compiler_params=pltpu.CompilerParams(vmem_limit_bytes=VMEM_LIMIT),
    )(v)


def _run_carry(carry, name):
    c_in, c_out, c_sem = carry.specs()
    n_in, n_out = len(c_in), len(c_out)

    def body(*refs):
        cps = carry.make(refs[:n_in], refs[n_in:n_in + n_out], *refs[n_in + n_out:])
        for cp in cps:
            cp.start()
        for cp in cps:
            cp.wait()

    return pl.pallas_call(body, name=name, in_specs=c_in, out_specs=c_out, out_shape=carry.outs,
                          scratch_shapes=c_sem, input_output_aliases=carry.aliases(0, 0))(*carry.ins)


def _row_tile(rows, last, itemsize=4, budget=2 * 1024 * 1024):
    return _tile(rows, tuple(t for t in (2048, 1024, 512, 256, 128, 64, 32, 16) if t * last * itemsize <= budget))


def _rs_add(g, got, core, name):
    _, half, last = got.shape
    tr = _row_tile(half, last)
    nb = half // tr

    def body(c_ref, g_ref, got_ref, o_ref):
        o_ref[...] = (g_ref[...] + got_ref[...]).astype(BF16)

    blk = pl.BlockSpec((None, tr, last), lambda s, i, c_ref: (s, i, 0))
    return pl.pallas_call(
        body, name=name,
        grid_spec=pltpu.PrefetchScalarGridSpec(
            num_scalar_prefetch=1, grid=(N_CHIPS, nb),
            in_specs=[pl.BlockSpec((None, tr, last), lambda s, i, c_ref: (s, c_ref[0] * nb + i, 0)), blk],
            out_specs=blk),
        out_shape=jax.ShapeDtypeStruct(got.shape, BF16),
        compiler_params=_cparams(("parallel", "parallel")),
    )(core.reshape(1).astype(jnp.int32), g, got)


def _rs_sum(own, got, acc, layer, chip, core, name):
    _, half, last = own.shape
    tr = _row_tile(half, last, budget=1024 * 1024)
    nb = half // tr
    ni = 0 if acc is None else 1

    def body(k_ref, c_ref, own_ref, got_ref, *rest):
        o_ref = rest[-1]
        tot = own_ref[...].astype(F32)
        for j in range(3):
            tot = tot + got_ref[j].astype(F32)
        o_ref[...] = tot

    return pl.pallas_call(
        body, name=name,
        grid_spec=pltpu.PrefetchScalarGridSpec(
            num_scalar_prefetch=2, grid=(nb,),
            in_specs=[pl.BlockSpec((None, tr, last), lambda i, k_ref, c_ref: (k_ref[0], i, 0)),
                      pl.BlockSpec((3, tr, last), lambda i, k_ref, c_ref: (0, i, 0))]
            + [pl.BlockSpec(memory_space=pl.ANY)] * ni,
            out_specs=pl.BlockSpec((None, tr, last), lambda i, k_ref, c_ref: (layer, c_ref[0] * nb + i, 0))),
        out_shape=jax.ShapeDtypeStruct((DEPTH, 2 * half, last), F32),
        input_output_aliases={4: 0} if ni else {},
        compiler_params=_cparams(("parallel",)),
    )(chip.reshape(1).astype(jnp.int32), core.reshape(1).astype(jnp.int32), own, got, *([acc] if ni else []))


def _rs_share(arrs):
    n = len(arrs)

    def body(*refs):
        out = refs[n:2 * n]
        send_sems, recv_sems = refs[2 * n:]
        x, y, c, _ = _place()
        sibling = (x, y, 1 - c)
        cps = []
        for p in range(n):
            half = out[p].shape[1] // 2
            mine = out[p].at[:, pl.ds(c * half, half), :]
            cps.append(_remote(mine, mine, send_sems.at[p], recv_sems.at[p], sibling))
        for cp in cps:
            cp.start()
        for p, cp in enumerate(cps):
            cp.wait_send()
            half = out[p].shape[1] // 2
            other = out[p].at[:, pl.ds((1 - c) * half, half), :]
            _remote(other, other, send_sems.at[p], recv_sems.at[p], sibling).wait_recv()

    return pl.pallas_call(
        body, name="rs_share", in_specs=[ANY] * n, out_specs=[ANY] * n,
        out_shape=[jax.ShapeDtypeStruct(a.shape, a.dtype) for a in arrs],
        input_output_aliases={p: p for p in range(n)},
        scratch_shapes=[pltpu.SemaphoreType.DMA((n,)), pltpu.SemaphoreType.DMA((n,))],
    )(*arrs)


class _GradReducer:
    def __init__(self, chip, core):
        self.chip, self.core = chip, core
        self.pending = None
        self.sums = None
        self.acc = {n: None for n in BIG}

    def push(self, layer, gb):
        assert self.pending is None
        self.pending = (layer, list(gb), list(gb.values()))

    def swap_carry(self):
        if self.pending is None:
            return None
        _, _, gs = self.pending
        outs = [jax.ShapeDtypeStruct((g.shape[0], g.shape[1] // 2, g.shape[2]), F32) for g in gs]

        def make(ins, out, send_sems, recv_sems):
            x, y, c, _ = _place()
            cps = []
            for p, ref in enumerate(ins):
                half = ref.shape[1] // 2
                cps.append(_remote(ref.at[:, pl.ds((1 - c) * half, half), :], out[p], send_sems.at[p], recv_sems.at[p],
                                   (x, y, 1 - c)))
            return cps

        return _Carry(gs, outs, len(gs), make)

    def after_swap(self, got):
        layer, names, gs = self.pending
        self.pending = None
        self.sums = (layer, names, [_rs_add(g, a, self.core, f"rs_add_{n}_{layer}") for n, g, a in zip(names, gs, got)])

    def ici_carry(self):
        if self.sums is None:
            return None
        _, _, ps = self.sums
        outs = [jax.ShapeDtypeStruct((3,) + p.shape[1:], BF16) for p in ps]

        def make(ins, out, send_sems, recv_sems):
            x, y, c, chips = _place()
            return [_remote(ins[p].at[2 * cx + cy], out[p].at[j], send_sems.at[3 * p + j], recv_sems.at[3 * p + j],
                            (cx, cy, c)) for p in range(len(ins)) for j, (cx, cy) in enumerate(chips)]

        return _Carry(ps, outs, 3 * len(ps), make)

    def after_ici(self, got):
        layer, names, ps = self.sums
        self.sums = None
        for n, p, a in zip(names, ps, got):
            self.acc[n] = _rs_sum(p, a, self.acc[n], layer, self.chip, self.core, f"rs_sum_{n}_{layer}")

    def finish(self):
        if self.pending is not None:
            self.after_swap(_run_carry(self.swap_carry(), "rs_swap_last"))
        if self.sums is not None:
            self.after_ici(_run_carry(self.ici_carry(), "rs_ici_last"))
        return dict(zip(BIG, _rs_share([self.acc[n] for n in BIG])))


def _adamw(w, g, m, v, name):
    shape = w.shape
    if len(shape) == 2:
        shape3 = (1,) + shape
    else:
        shape3 = (-1,) + shape[-2:]
    w2, g2, m2, v2 = (a.reshape(shape3) for a in (w, g, m, v))
    lead, rows, last = w2.shape
    tr = max([t for t in range(8, rows + 1, 8) if rows % t == 0 and t * last * 4 <= 2 * 1024 * 1024] or [rows])

    def body(w_ref, g_ref, m_ref, v_ref, d_ref, nm_ref, nv_ref):
        gv = g_ref[...]
        mn = ADAM_B1 * m_ref[...] + (1.0 - ADAM_B1) * gv
        vn = ADAM_B2 * v_ref[...] + (1.0 - ADAM_B2) * (gv * gv)
        m_hat = mn / (1.0 - ADAM_B1 ** ADAM_STEP)
        v_hat = vn / (1.0 - ADAM_B2 ** ADAM_STEP)
        d_ref[...] = -ADAM_LR * (m_hat / (jnp.sqrt(v_hat) + ADAM_EPS) + ADAM_WD * w_ref[...])
        nm_ref[...] = mn
        nv_ref[...] = vn

    blk = pl.BlockSpec((None, tr, last), lambda a, i: (a, i, 0))
    outs = pl.pallas_call(
        body, name=name, grid=(lead, rows // tr), in_specs=[blk] * 4, out_specs=[blk] * 3,
        out_shape=[jax.ShapeDtypeStruct((lead, rows, last), F32)] * 3,
        compiler_params=_cparams(("parallel", "parallel")),
    )(w2, g2, m2, v2)
    return tuple(o.reshape(shape) for o in outs)


SHARD_AXIS = {"conv_w": 2, "meta_tokens": 1}
SMALL_SHARDED = ("conv_w", "meta_tokens")
REPLICATED = ("mix_norm_w", "b_gate", "pool_scale", "conv_b", "dt_bias", "a_log", "d_skip", "ssd_norm_w", "mlp_norm_w",
              "final_norm_w")


def _flatten(arrs, dtype, row_mult):
    flat = jnp.concatenate([a.astype(dtype).reshape(-1) for a in arrs])
    n = flat.shape[0]
    rows = -(-n // (LANES * row_mult)) * row_mult
    return jnp.pad(flat, (0, rows * LANES - n)).reshape(rows, LANES)


def _unflatten(flat2d, shapes):
    flat = flat2d.reshape(-1)
    out, off = [], 0
    for sh in shapes:
        n = 1
        for d in sh:
            n *= d
        out.append(flat[off:off + n].reshape(sh))
        off += n
    return out


def kernel(x, meta_tokens, mix_norm_w, w_in, b_gate, pool_w_group, pool_scale, w_pool_up, conv_w, conv_b, dt_bias, a_log, d_skip, ssd_norm_w, w_ssd_out, w_o, mlp_norm_w, w_ff1, w_ff2, final_norm_w, loss_target, m_meta_tokens, m_mix_norm_w, m_w_in, m_b_gate, m_pool_w_group, m_pool_scale, m_w_pool_up, m_conv_w, m_conv_b, m_dt_bias, m_a_log, m_d_skip, m_ssd_norm_w, m_w_ssd_out, m_w_o, m_mlp_norm_w, m_w_ff1, m_w_ff2, m_final_norm_w, v_meta_tokens, v_mix_norm_w, v_w_in, v_b_gate, v_pool_w_group, v_pool_scale, v_w_pool_up, v_conv_w, v_conv_b, v_dt_bias, v_a_log, v_d_skip, v_ssd_norm_w, v_w_ssd_out, v_w_o, v_mlp_norm_w, v_w_ff1, v_w_ff2, v_final_norm_w):
    names = ("meta_tokens",) + PER_LAYER + ("final_norm_w",)
    par = dict(meta_tokens=meta_tokens, mix_norm_w=mix_norm_w, w_in=w_in, b_gate=b_gate, pool_w_group=pool_w_group,
               pool_scale=pool_scale, w_pool_up=w_pool_up, conv_w=conv_w, conv_b=conv_b, dt_bias=dt_bias, a_log=a_log,
               d_skip=d_skip, ssd_norm_w=ssd_norm_w, w_ssd_out=w_ssd_out, w_o=w_o, mlp_norm_w=mlp_norm_w, w_ff1=w_ff1,
               w_ff2=w_ff2, final_norm_w=final_norm_w)
    mom = dict(meta_tokens=m_meta_tokens, mix_norm_w=m_mix_norm_w, w_in=m_w_in, b_gate=m_b_gate, pool_w_group=m_pool_w_group,
               pool_scale=m_pool_scale, w_pool_up=m_w_pool_up, conv_w=m_conv_w, conv_b=m_conv_b, dt_bias=m_dt_bias,
               a_log=m_a_log, d_skip=m_d_skip, ssd_norm_w=m_ssd_norm_w, w_ssd_out=m_w_ssd_out, w_o=m_w_o,
               mlp_norm_w=m_mlp_norm_w, w_ff1=m_w_ff1, w_ff2=m_w_ff2, final_norm_w=m_final_norm_w)
    var = dict(meta_tokens=v_meta_tokens, mix_norm_w=v_mix_norm_w, w_in=v_w_in, b_gate=v_b_gate, pool_w_group=v_pool_w_group,
               pool_scale=v_pool_scale, w_pool_up=v_w_pool_up, conv_w=v_conv_w, conv_b=v_conv_b, dt_bias=v_dt_bias,
               a_log=v_a_log, d_skip=v_d_skip, ssd_norm_w=v_ssd_norm_w, w_ssd_out=v_w_ssd_out, w_o=v_w_o,
               mlp_norm_w=v_mlp_norm_w, w_ff1=v_w_ff1, w_ff2=v_w_ff2, final_norm_w=v_final_norm_w)
    chip = 2 * lax.axis_index("x") + lax.axis_index("y")
    core = lax.axis_index("c")

    weights = _WeightGatherer({n: par[n].astype(BF16).reshape(DEPTH, -1, par[n].shape[-1]) for n in BIG})
    small = {n: par[n] for n in REPLICATED}
    small_shapes = [par[n].shape for n in SMALL_SHARDED]
    got_small = _exchange_small(_flatten([par[n] for n in SMALL_SHARDED], F32, 8), False, "gather_small")
    pieces = [_unflatten(got_small[2 * k], small_shapes) for k in range(N_CHIPS)]
    for j, n in enumerate(SMALL_SHARDED):
        small[n] = jnp.concatenate([pieces[k][j] for k in range(N_CHIPS)], axis=SHARD_AXIS[n])

    reducer = _GradReducer(chip, core)
    loss, dx, _, grads = _local_step(x[0], loss_target[0], weights, small, reducer)

    small_names = REPLICATED + SMALL_SHARDED
    red = _exchange_small(_flatten([grads[n] for n in small_names] + [loss.reshape(1)], F32, 8), True, "allreduce_small")
    *parts, loss = _unflatten(red, [grads[n].shape for n in small_names] + [(1,)])
    loss = loss[0]
    gsum = dict(zip(small_names, parts))
    for n in SMALL_SHARDED:
        width = par[n].shape[SHARD_AXIS[n]]
        gsum[n] = lax.dynamic_slice_in_dim(gsum[n], chip * width, width, axis=SHARD_AXIS[n])

    gsum.update({n: g.reshape(par[n].shape) for n, g in reducer.finish().items()})

    delta, new_m, new_v = {}, {}, {}
    for n in BIG:
        if n == "w_in":
            tr = lambda a: jnp.swapaxes(a, 1, 2)
            outs = _adamw(tr(par[n]), tr(gsum[n]), tr(mom[n]), tr(var[n]), "adamw_" + n)
            delta[n], new_m[n], new_v[n] = (tr(o) for o in outs)
        else:
            delta[n], new_m[n], new_v[n] = _adamw(par[n], gsum[n], mom[n], var[n], "adamw_" + n)
    rest = [n for n in names if n not in BIG]
    shapes = [par[n].shape for n in rest]
    packed = [_flatten([d[n] for n in rest], F32, 8) for d in (par, gsum, mom, var)]
    for d, flat in zip((delta, new_m, new_v), _adamw(*packed, "adamw_small")):
        d.update(zip(rest, _unflatten(flat, shapes)))

    order = ("meta_tokens", "mix_norm_w", "w_in", "b_gate", "pool_w_group", "pool_scale", "w_pool_up", "conv_w", "conv_b",
             "dt_bias", "a_log", "d_skip", "ssd_norm_w", "w_ssd_out", "w_o", "mlp_norm_w", "w_ff1", "w_ff2", "final_norm_w")
    return (loss, dx[None], *[gsum[n] for n in order], *[delta[n] for n in order], *[new_m[n] for n in order],
            *[new_v[n] for n in order])
```

```python
import functools

import jax
import jax.numpy as jnp
from jax import lax
from jax.experimental import pallas as pl
from jax.experimental.pallas import tpu as pltpu

F32 = jnp.float32
BF16 = jnp.bfloat16

D_MODEL = 1024
DEPTH = 4
N_META = 16
N_PAD = 112
ROW_X = N_PAD + N_META
POOL_WINDOWS = (2, 4, 8, 16)
POOL_GDIM = 256
D_INNER = 2048
N_HEADS = 32
HEAD_DIM = 64
N_GROUPS = 8
HEADS_PER_GROUP = 4
GROUP_W = HEADS_PER_GROUP * HEAD_DIM
D_STATE = 128
CHUNK = 128
D_XBC = 4096
D_FF = 4096
EPS = 1e-5
OFF_Z, OFF_XBC, OFF_DT, OFF_GATE, IN_COLS = 1024, 3072, 7168, 7200, 9248
PZ, PGATE, PPOOL, PXBC, PCOLS = 0, 2048, 4096, 5120, 9216
DT_PAD = 128

ADAM_LR, ADAM_B1, ADAM_B2, ADAM_EPS, ADAM_WD, ADAM_STEP = 0.001, 0.9, 0.999, 1e-08, 0.01, 10

VMEM_LIMIT = 56 * 1024 * 1024

_NN = (((1,), (0,)), ((), ()))
_NT = (((1,), (1,)), ((), ()))
_TN = (((0,), (0,)), ((), ()))


def _dot(a, b, dn=_NN):
    return lax.dot_general(a, b, dn, preferred_element_type=F32)


def _cparams(sem):
    return pltpu.CompilerParams(dimension_semantics=sem, vmem_limit_bytes=VMEM_LIMIT)


def _tile(n, cands):
    for c in cands:
        if n % c == 0:
            return c
    raise ValueError(f"no tile for {n} in {cands}")


def _sigmoid(x):
    return 0.5 * jnp.tanh(0.5 * x) + 0.5


def _softplus(x):
    return jnp.maximum(x, 0.0) + jnp.log(1.0 + jnp.exp(-jnp.abs(x)))


def _bs(shape, fn):
    return pl.BlockSpec(shape, fn)


class _Carry:
    def __init__(self, ins, outs, n_sems, make, alias=None):
        self.ins, self.outs, self.n_sems, self.make = list(ins), list(outs), n_sems, make
        self.alias = dict(alias or {})

    def aliases(self, first_in, first_out):
        return {first_in + i: first_out + o for i, o in self.alias.items()}

    def specs(self):
        hbm = pl.BlockSpec(memory_space=pl.ANY)
        return [hbm] * len(self.ins), [hbm] * len(self.outs), [pltpu.SemaphoreType.DMA((self.n_sems,))] * 2

    def at_edges(self, grid, in_refs, out_refs, sems):
        ids = [pl.program_id(a) for a in range(len(grid))]
        first = functools.reduce(jnp.logical_and, [i == 0 for i in ids])
        last = functools.reduce(jnp.logical_and, [i == n - 1 for i, n in zip(ids, grid)])

        def begin():
            @pl.when(first)
            def _():
                for cp in self.make(in_refs, out_refs, *sems):
                    cp.start()

        def end():
            @pl.when(last)
            def _():
                for cp in self.make(in_refs, out_refs, *sems):
                    cp.wait()

        return begin, end


def _sum_into(o_ref, v, first):
    @pl.when(first)
    def _():
        o_ref[...] = v

    @pl.when(jnp.logical_not(first))
    def _():
        o_ref[...] += v


def _mm(name, dn, grid, a, a_spec, b, b_spec, outs, *, extras=(), epilogue=None, carry=None, summed=()):
    nk = grid[2]
    ne, no = len(extras), len(outs)
    blk = tuple(d for d in outs[0][2].block_shape if d is not None)
    c_in, c_out, c_sem = carry.specs() if carry is not None else ([], [], [])
    nci, nco = len(c_in), len(c_out)

    def body(a_ref, b_ref, *rest):
        e_refs, o_refs = rest[:ne], rest[ne + nci:ne + nci + no]
        first_row_tile = pl.program_id(0) == 0
        if carry is not None:
            begin, end = carry.at_edges(grid, rest[ne:ne + nci], rest[ne + nci + no:ne + nci + no + nco], rest[-2:])
            begin()
        if len(b_ref.shape) == 3:
            rows = b_ref.shape[1]
            p = _dot(a_ref[:, :rows].astype(BF16), b_ref[0].astype(BF16), dn)
            for s in range(1, b_ref.shape[0]):
                p = p + _dot(a_ref[:, s * rows:(s + 1) * rows].astype(BF16), b_ref[s].astype(BF16), dn)
        else:
            p = _dot(a_ref[...].astype(BF16), b_ref[...].astype(BF16), dn)

        def finish(acc):
            outs = epilogue(acc, *[e[...] for e in e_refs]) if epilogue is not None else (acc,)
            for idx, (o, v) in enumerate(zip(o_refs, outs)):
                if idx in summed:
                    _sum_into(o, v, first_row_tile)
                else:
                    o[...] = v.astype(o.dtype)

        if nk == 1:
            finish(p)
        else:
            acc_ref = rest[ne + nci + no + nco]
            kk = pl.program_id(2)

            @pl.when(kk == 0)
            def _():
                acc_ref[...] = p

            @pl.when(kk > 0)
            def _():
                acc_ref[...] += p

            @pl.when(kk == nk - 1)
            def _():
                finish(acc_ref[...])
        if carry is not None:
            end()

    res = pl.pallas_call(
        body,
        name=name,
        grid=grid,
        in_specs=[a_spec, b_spec, *[s for _, s in extras]] + c_in,
        out_specs=[o[2] for o in outs] + c_out,
        out_shape=[jax.ShapeDtypeStruct(o[0], o[1]) for o in outs] + (carry.outs if carry is not None else []),
        scratch_shapes=([pltpu.VMEM(blk, F32)] if nk > 1 else []) + c_sem,
        input_output_aliases=carry.aliases(2 + ne, no) if carry is not None else {},
        compiler_params=_cparams(("arbitrary",) * 3 if carry is not None or summed else ("parallel", "parallel", "arbitrary")),
    )(a, b, *[e for e, _ in extras], *(carry.ins if carry is not None else []))
    if carry is not None:
        return res[:no], res[no:]
    return res[0] if no == 1 else res


def _norm_fwd(h, w, name):
    t, d = h.shape
    tm = _tile(t, (1056, 384, 128))

    def body(h_ref, w_ref, u_ref):
        x = h_ref[...]
        r = lax.rsqrt(jnp.mean(x * x, axis=-1, keepdims=True) + EPS)
        u_ref[...] = (x * r * w_ref[...]).astype(BF16)

    return pl.pallas_call(
        body, name=name, grid=(t // tm,),
        in_specs=[pl.BlockSpec((tm, d), lambda i: (i, 0)), pl.BlockSpec((1, d), lambda i: (0, 0))],
        out_specs=pl.BlockSpec((tm, d), lambda i: (i, 0)),
        out_shape=jax.ShapeDtypeStruct((t, d), BF16),
        compiler_params=_cparams(("parallel",)),
    )(h, w.reshape(1, d))


def _loss_head(h, w, tgt):
    t, d = h.shape
    tm = CHUNK
    nb = ROW_X // tm

    def body(h_ref, w_ref, t_ref, loss_ref, dh_ref, dw_ref):
        i = pl.program_id(0)
        x = h_ref[...]
        r = lax.rsqrt(jnp.mean(x * x, axis=-1, keepdims=True) + EPS)
        xhat = x * r
        wv = w_ref[...]
        live = i >= nb
        err = jnp.where(live, xhat * wv - t_ref[...], 0.0)
        dout = err * (1.0 / d)
        g = dout * wv
        dh_ref[...] = r * (g - xhat * jnp.mean(g * xhat, axis=-1, keepdims=True))

        @pl.when(i == 0)
        def _():
            loss_ref[...] = jnp.zeros_like(loss_ref)
            dw_ref[...] = jnp.zeros_like(dw_ref)

        loss_ref[...] += 0.5 * jnp.sum(jnp.sum(err * err, axis=-1, keepdims=True), axis=0, keepdims=True) * (1.0 / d)
        dw_ref[...] += jnp.sum(dout * xhat, axis=0, keepdims=True)

    row = pl.BlockSpec((tm, d), lambda i: (i, 0))
    vec = pl.BlockSpec((1, d), lambda i: (0, 0))
    loss, dh, dw = pl.pallas_call(
        body, name="loss_head", grid=(t // tm,),
        in_specs=[row, vec, pl.BlockSpec((tm, d), lambda i: (jnp.maximum(i - nb, 0), 0))],
        out_specs=[pl.BlockSpec((1, 1), lambda i: (0, 0)), row, vec],
        out_shape=[jax.ShapeDtypeStruct((1, 1), F32), jax.ShapeDtypeStruct((t, d), F32), jax.ShapeDtypeStruct((1, d), F32)],
        compiler_params=_cparams(("arbitrary",)),
    )(h, w.reshape(1, d), tgt)
    return loss[0, 0], dh, dw.reshape(d)


def _gate_fwd(proj, b_gate, y_pool, y_ssd):
    t = proj.shape[0]
    d = D_MODEL
    tm = _tile(t, (528, 384, 128))

    def body(gp_ref, gs_ref, bp_ref, bs_ref, yp_ref, ys_ref, o_ref):
        gp = _sigmoid(gp_ref[...] + bp_ref[...])
        gs = _sigmoid(gs_ref[...] + bs_ref[...])
        o_ref[...] = (gp * yp_ref[...] + gs * ys_ref[...]).astype(BF16)

    row = pl.BlockSpec((tm, d), lambda i: (i, 0))
    return pl.pallas_call(
        body, name="gate_fwd", grid=(t // tm,),
        in_specs=[pl.BlockSpec((tm, d), lambda i: (i, PGATE // d)), pl.BlockSpec((tm, d), lambda i: (i, PGATE // d + 1)),
                  pl.BlockSpec((1, d), lambda i: (0, 0)), pl.BlockSpec((1, d), lambda i: (0, 1)), row, row],
        out_specs=row, out_shape=jax.ShapeDtypeStruct((t, d), BF16),
        compiler_params=_cparams(("parallel",)),
    )(proj, proj, b_gate.reshape(1, 2 * d), b_gate.reshape(1, 2 * d), y_pool, y_ssd)


def _gate_bwd(dmerged, proj, b_gate, y_pool, y_ssd):
    t = proj.shape[0]
    d = D_MODEL
    tm = _tile(t, (384, 128))

    def body(dm_ref, gp_ref, gs_ref, bp_ref, bs_ref, yp_ref, ys_ref, dyp_ref, dys_ref, dg_ref, db_ref):
        dm = dm_ref[...]
        gp = _sigmoid(gp_ref[...] + bp_ref[...])
        gs = _sigmoid(gs_ref[...] + bs_ref[...])
        dyp_ref[...] = (dm * gp).astype(BF16)
        dys_ref[...] = (dm * gs).astype(BF16)
        dgp = dm * yp_ref[...] * gp * (1.0 - gp)
        dgs = dm * ys_ref[...] * gs * (1.0 - gs)
        dg_ref[:, :d] = dgp.astype(BF16)
        dg_ref[:, d:] = dgs.astype(BF16)

        @pl.when(pl.program_id(0) == 0)
        def _():
            db_ref[...] = jnp.zeros_like(db_ref)

        db_ref[:, :d] += jnp.sum(dgp, axis=0, keepdims=True)
        db_ref[:, d:] += jnp.sum(dgs, axis=0, keepdims=True)

    row = pl.BlockSpec((tm, d), lambda i: (i, 0))
    dyp, dys, dg, db = pl.pallas_call(
        body, name="gate_bwd", grid=(t // tm,),
        in_specs=[row, pl.BlockSpec((tm, d), lambda i: (i, PGATE // d)), pl.BlockSpec((tm, d), lambda i: (i, PGATE // d + 1)),
                  pl.BlockSpec((1, d), lambda i: (0, 0)), pl.BlockSpec((1, d), lambda i: (0, 1)), row, row],
        out_specs=[row, row, pl.BlockSpec((tm, 2 * d), lambda i: (i, PGATE // (2 * d))),
                   pl.BlockSpec((1, 2 * d), lambda i: (0, 0))],
        out_shape=[jax.ShapeDtypeStruct((t, d), BF16), jax.ShapeDtypeStruct((t, d), BF16),
                   jax.ShapeDtypeStruct((t, PCOLS), BF16), jax.ShapeDtypeStruct((1, 2 * d), F32)],
        compiler_params=_cparams(("arbitrary",)),
    )(dmerged, proj, proj, b_gate.reshape(1, 2 * d), b_gate.reshape(1, 2 * d), y_pool, y_ssd)
    return dyp, dys, dg, db.reshape(2 * d)


POOL_HALO = 16


def _pool_counts(row0, n, win):
    pos1 = row0 + lax.broadcasted_iota(jnp.int32, (n, 1), 0) - (N_PAD - 1)
    return jnp.clip(pos1, 1, win).astype(F32)


N_CHIPS = 4
WG_ROWS = POOL_GDIM // N_CHIPS
WUP_ROWS = D_MODEL // N_CHIPS


def _group_w(wg_ref, g):
    return jnp.concatenate([wg_ref[k, g] for k in range(N_CHIPS)], axis=0)


def _pool_w_specs(layer):
    return [pl.BlockSpec((None, N_CHIPS, len(POOL_WINDOWS), WG_ROWS, POOL_GDIM), lambda i: (0, 0, 0, 0, 0)),
            pl.BlockSpec((None, N_CHIPS, WUP_ROWS, D_MODEL), lambda i: (0, 0, 0, 0))]


def _pool_fwd(proj, wg, scale, wup, layer):
    t = proj.shape[0]
    d = D_MODEL
    tm = _tile(t, (384, 128))
    hb = tm // POOL_HALO

    def body(u_ref, halo_ref, sc_ref, wg_ref, wup_ref, pooled_ref, yg_ref, ypm_ref, yp_ref):
        i = pl.program_id(0)
        x = u_ref[...]
        halo = jnp.where(i > 0, halo_ref[...], 0.0)
        xc = jnp.concatenate([halo, x], axis=0)
        for g, win in enumerate(POOL_WINDOWS):
            sl = slice(g * POOL_GDIM, (g + 1) * POOL_GDIM)
            s = xc[:, sl]
            k = 1
            while k < win:
                s = s + pltpu.roll(s, k, axis=0)
                k *= 2
            pooled = s[POOL_HALO:] / _pool_counts(i * tm, tm, win) - x[:, sl]
            pb = pooled.astype(BF16)
            pooled_ref[:, sl] = pb
            yg_ref[:, sl] = _dot(pb, _group_w(wg_ref, g))
        ypm = (yg_ref[...] * sc_ref[...]).astype(BF16)
        ypm_ref[...] = ypm
        acc = _dot(ypm[:, :WUP_ROWS], wup_ref[0])
        for k in range(1, N_CHIPS):
            acc = acc + _dot(ypm[:, k * WUP_ROWS:(k + 1) * WUP_ROWS], wup_ref[k])
        yp_ref[...] = acc

    row = pl.BlockSpec((tm, d), lambda i: (i, 0))
    return pl.pallas_call(
        body, name=f"pool_fwd_{layer}", grid=(t // tm,),
        in_specs=[pl.BlockSpec((tm, d), lambda i: (i, PPOOL // d)),
                  pl.BlockSpec((POOL_HALO, d), lambda i: (jnp.maximum(i * hb - 1, 0), PPOOL // d)),
                  pl.BlockSpec((1, d), lambda i: (0, 0))] + _pool_w_specs(layer),
        out_specs=[row, row, row, row],
        out_shape=[jax.ShapeDtypeStruct((t, d), BF16), jax.ShapeDtypeStruct((t, d), F32),
                   jax.ShapeDtypeStruct((t, d), BF16), jax.ShapeDtypeStruct((t, d), F32)],
        compiler_params=_cparams(("parallel",)),
    )(proj, proj, scale.reshape(1, d), wg, wup)


def _pool_bwd_a(dy_pool, yg, scale, wg, wup, layer):
    t, d = yg.shape
    tm = _tile(t, (384, 128))

    def body(dy_ref, yg_ref, sc_ref, wg_ref, wup_ref, q_ref, dyg_ref, dsc_ref):
        dy = dy_ref[...]
        dypm = jnp.concatenate([_dot(dy, wup_ref[k], _NT) for k in range(N_CHIPS)], axis=1)

        @pl.when(pl.program_id(0) == 0)
        def _():
            dsc_ref[...] = jnp.zeros_like(dsc_ref)

        dsc_ref[...] += jnp.sum(dypm * yg_ref[...], axis=0, keepdims=True)
        dyg = (dypm * sc_ref[...]).astype(BF16)
        dyg_ref[...] = dyg
        for g in range(len(POOL_WINDOWS)):
            sl = slice(g * POOL_GDIM, (g + 1) * POOL_GDIM)
            q_ref[:, sl] = _dot(dyg[:, sl], _group_w(wg_ref, g), _NT)

    row = pl.BlockSpec((tm, d), lambda i: (i, 0))
    vec = pl.BlockSpec((1, d), lambda i: (0, 0))
    q, dyg, dsc = pl.pallas_call(
        body, name=f"pool_bwd_a_{layer}", grid=(t // tm,),
        in_specs=[row, row, vec] + _pool_w_specs(layer),
        out_specs=[row, row, vec],
        out_shape=[jax.ShapeDtypeStruct((t, d), F32), jax.ShapeDtypeStruct((t, d), BF16), jax.ShapeDtypeStruct((1, d), F32)],
        compiler_params=_cparams(("arbitrary",)),
    )(dy_pool, yg, scale.reshape(1, d), wg, wup)
    return q, dyg, dsc.reshape(d)


def _pool_bwd_b(q, dproj):
    t, d = q.shape
    tm = _tile(t, (384, 128))
    hb = tm // POOL_HALO
    nt = t // tm
    n = tm + POOL_HALO

    def body(q_ref, halo_ref, _, o_ref):
        i = pl.program_id(0)
        qv = q_ref[...]
        halo = jnp.where(i < nt - 1, halo_ref[...], 0.0)
        qc = jnp.concatenate([qv, halo], axis=0)
        for g, win in enumerate(POOL_WINDOWS):
            sl = slice(g * POOL_GDIM, (g + 1) * POOL_GDIM)
            s = qc[:, sl] / _pool_counts(i * tm, n, win)
            k = 1
            while k < win:
                s = s + pltpu.roll(s, n - k, axis=0)
                k *= 2
            o_ref[:, sl] = (s[:tm] - qv[:, sl]).astype(BF16)

    row = pl.BlockSpec((tm, d), lambda i: (i, 0))
    return pl.pallas_call(
        body, name="pool_bwd_b", grid=(nt,),
        in_specs=[row, pl.BlockSpec((POOL_HALO, d), lambda i: (jnp.minimum((i + 1) * hb, t // POOL_HALO - 1), 0)),
                  pl.BlockSpec(memory_space=pl.ANY)],
        out_specs=pl.BlockSpec((tm, d), lambda i: (i, PPOOL // d)), out_shape=jax.ShapeDtypeStruct(dproj.shape, BF16),
        input_output_aliases={2: 0},
        compiler_params=_cparams(("parallel",)),
    )(q, q, dproj)


def _pool_dwg(pooled, dyg, layer):
    t, d = pooled.shape
    tk = _tile(t, (1056, 384, 128))
    nk = t // tk
    gd = POOL_GDIM
    ng = d // gd

    def body(p_ref, g_ref, o_ref):
        @pl.when(pl.program_id(1) == 0)
        def _():
            o_ref[...] = jnp.zeros_like(o_ref)

        part = _dot(p_ref[...], g_ref[...], _TN)
        for k in range(N_CHIPS):
            o_ref[k] += part[k * WG_ROWS:(k + 1) * WG_ROWS]

    blk = pl.BlockSpec((tk, gd), lambda g, k: (k, g))
    return pl.pallas_call(
        body, name=f"pool_dwg_{layer}", grid=(ng, nk), in_specs=[blk, blk],
        out_specs=pl.BlockSpec((N_CHIPS, None, WG_ROWS, gd), lambda g, k: (0, g, 0, 0)),
        out_shape=jax.ShapeDtypeStruct((N_CHIPS, ng, WG_ROWS, gd), F32),
        compiler_params=_cparams(("parallel", "arbitrary")),
    )(pooled, dyg)


CONV_W = 4
CONV_HALO = 8
XBC_BLK = PXBC // 1024


def _conv_fwd(proj, conv_w, conv_b):
    t = proj.shape[0]
    cw = 1024
    tm = _tile(t, (1056, 384, 128))
    hb = tm // CONV_HALO

    def body(x_ref, halo_ref, w_ref, b_ref, o_ref, pre_ref):
        i = pl.program_id(1)
        x = x_ref[...]
        halo = jnp.where(i > 0, halo_ref[...], 0.0)
        xc = jnp.concatenate([halo, x], axis=0)
        w = w_ref[...]
        acc = b_ref[...] + x * w[CONV_W - 1:CONV_W, :]
        for k in range(CONV_W - 1):
            acc = acc + pltpu.roll(xc, CONV_W - 1 - k, axis=0)[CONV_HALO:] * w[k:k + 1, :]
        row = i * tm + lax.broadcasted_iota(jnp.int32, (tm, 1), 0)
        pre_ref[...] = acc
        o_ref[...] = jnp.where(row >= N_PAD, acc * _sigmoid(acc), 0.0)

    blk = pl.BlockSpec((tm, cw), lambda j, i: (i, j))
    return pl.pallas_call(
        body, name="conv_fwd", grid=(D_XBC // cw, t // tm),
        in_specs=[pl.BlockSpec((tm, cw), lambda j, i: (i, XBC_BLK + j)),
                  pl.BlockSpec((CONV_HALO, cw), lambda j, i: (jnp.maximum(i * hb - 1, 0), XBC_BLK + j)),
                  pl.BlockSpec((CONV_W, cw), lambda j, i: (0, j)), pl.BlockSpec((1, cw), lambda j, i: (0, j))],
        out_specs=[blk, blk],
        out_shape=[jax.ShapeDtypeStruct((t, D_XBC), F32), jax.ShapeDtypeStruct((t, D_XBC), F32)],
        compiler_params=_cparams(("parallel", "parallel")),
    )(proj, proj, conv_w, conv_b.reshape(1, D_XBC))


def _conv_bwd(dxa, coff, proj, pre, conv_w, dproj):
    t, ncols = dxa.shape
    cw = 1024
    tm = _tile(t, (528, 384, 128))
    hb = tm // CONV_HALO
    nt = t // tm
    n = tm + CONV_HALO

    def body(d_ref, dn_ref, p_ref, pn_ref, x_ref, w_ref, _, o_ref, dw_ref, db_ref):
        i = pl.program_id(1)
        last = i == nt - 1
        xc = jnp.concatenate([p_ref[...], pn_ref[...]], axis=0)
        df = jnp.concatenate([d_ref[...], dn_ref[...]], axis=0)
        w = w_ref[...]
        sig = _sigmoid(xc)
        row = i * tm + lax.broadcasted_iota(jnp.int32, (n, 1), 0)
        live = (row >= N_PAD) & ((row < (i + 1) * tm) | jnp.logical_not(last))
        dxc = jnp.where(live, df * (sig * (1.0 + xc * (1.0 - sig))), 0.0)
        ahead = [pltpu.roll(dxc, n - (CONV_W - 1 - k), axis=0)[:tm] for k in range(CONV_W - 1)] + [dxc[:tm]]
        acc = ahead[0] * w[0:1, :]
        for k in range(1, CONV_W):
            acc = acc + ahead[k] * w[k:k + 1, :]
        o_ref[...] = acc.astype(BF16)

        @pl.when(i == 0)
        def _():
            dw_ref[...] = jnp.zeros_like(dw_ref)
            db_ref[...] = jnp.zeros_like(db_ref)

        x = x_ref[...]
        db_ref[...] += jnp.sum(ahead[CONV_W - 1], axis=0, keepdims=True)
        for k in range(CONV_W):
            dw_ref[k:k + 1, :] += jnp.sum(ahead[k] * x, axis=0, keepdims=True)

    def pspec(rows, fn):
        return pl.BlockSpec((rows, cw), lambda j, i: (fn(i), coff + j))

    nxt = lambda i: jnp.minimum((i + 1) * hb, t // CONV_HALO - 1)
    dxbc, dw, db = pl.pallas_call(
        body, name=f"conv_bwd_{coff}", grid=(ncols // cw, nt),
        in_specs=[pl.BlockSpec((tm, cw), lambda j, i: (i, j)), pl.BlockSpec((CONV_HALO, cw), lambda j, i: (nxt(i), j)),
                  pspec(tm, lambda i: i), pspec(CONV_HALO, nxt),
                  pl.BlockSpec((tm, cw), lambda j, i: (i, XBC_BLK + coff + j)),
                  pl.BlockSpec((CONV_W, cw), lambda j, i: (0, coff + j)),
                  pl.BlockSpec(memory_space=pl.ANY)],
        out_specs=[pl.BlockSpec((tm, cw), lambda j, i: (i, XBC_BLK + coff + j)), pl.BlockSpec((CONV_W, cw), lambda j, i: (0, j)),
                   pl.BlockSpec((1, cw), lambda j, i: (0, j))],
        out_shape=[jax.ShapeDtypeStruct(dproj.shape, BF16), jax.ShapeDtypeStruct((CONV_W, ncols), F32),
                   jax.ShapeDtypeStruct((1, ncols), F32)],
        input_output_aliases={6: 0},
        compiler_params=_cparams(("parallel", "arbitrary")),
    )(dxa, dxa, pre, pre, proj, conv_w, dproj)
    return dxbc, dw, db.reshape(ncols)


def _cumsum(x, axis, reverse=False):
    n = x.shape[axis]
    idx = lax.broadcasted_iota(jnp.int32, x.shape, axis)
    k = 1
    while k < n:
        if reverse:
            x = x + jnp.where(idx < n - k, pltpu.roll(x, n - k, axis=axis), 0.0)
        else:
            x = x + jnp.where(idx >= k, pltpu.roll(x, k, axis=axis), 0.0)
        k *= 2
    return x


def _head_masks():
    lane = lax.broadcasted_iota(jnp.int32, (1, GROUP_W), 1)
    return [(lane >= r * HEAD_DIM) & (lane < (r + 1) * HEAD_DIM) for r in range(HEADS_PER_GROUP)]


def _expand_heads(cols, hm):
    out = jnp.where(hm[0], cols[:, 0:1], 0.0)
    for r in range(1, HEADS_PER_GROUP):
        out = out + jnp.where(hm[r], cols[:, r:r + 1], 0.0)
    return out


def _ssd_decay(dt_raw, dt_bias, a_log):
    t = dt_raw.shape[0]

    hpg = HEADS_PER_GROUP

    def body(raw_ref, b_ref, al_ref, dt_ref, acs_ref, sig_ref, rows_ref):
        raw = raw_ref[...] + b_ref[...]
        rowid = pl.program_id(0) * CHUNK + lax.broadcasted_iota(jnp.int32, (CHUNK, 1), 0)
        dt = jnp.where(rowid >= N_PAD, _softplus(raw), 0.0)
        acs = _cumsum(dt * -jnp.exp(al_ref[...]), 0)
        sig = _sigmoid(raw)
        acs_t = acs.T
        for g in range(N_GROUPS):
            heads = slice(g * hpg, (g + 1) * hpg)
            dt_ref[g] = dt[:, heads]
            acs_ref[g] = acs[:, heads]
            sig_ref[g] = sig[:, heads]
            rows_ref[g] = acs_t[heads, :]

    blk = pl.BlockSpec((CHUNK, DT_PAD), lambda c: (c, 0))
    vec = pl.BlockSpec((1, DT_PAD), lambda c: (0, 0))
    cols = pl.BlockSpec((N_GROUPS, CHUNK, hpg), lambda c: (0, c, 0))
    pad = lambda v: jnp.pad(v, (0, DT_PAD - N_HEADS)).reshape(1, DT_PAD)
    dt, acs, sig, rows = pl.pallas_call(
        body, name="ssd_decay", grid=(t // CHUNK,), in_specs=[blk, vec, vec],
        out_specs=[cols, cols, cols, pl.BlockSpec((N_GROUPS, hpg, CHUNK), lambda c: (0, 0, c))],
        out_shape=[jax.ShapeDtypeStruct((N_GROUPS, t, hpg), F32)] * 3 + [jax.ShapeDtypeStruct((N_GROUPS, hpg, t), F32)],
        compiler_params=_cparams(("parallel",)),
    )(dt_raw, pad(dt_bias), pad(a_log))
    return dict(dt=dt, acs=acs, acs_rows=rows, sig=sig, a_log=a_log.reshape(N_GROUPS, 1, hpg))


def _ssd_q(dtc, acs4, acs_r):
    hm = _head_masks()
    dt_exp = _expand_heads(dtc, hm)
    acs = _expand_heads(acs4, hm)
    atot = acs[CHUNK - 1:CHUNK, :]
    return dict(dtc=dtc, hm=hm, dt_exp=dt_exp, acs=acs, acs_r=acs_r,
                ea=jnp.exp(acs), ds=jnp.exp(atot - acs), dec=jnp.exp(atot))


def _stack4(x):
    return jnp.concatenate([x] * HEADS_PER_GROUP, axis=0)


def _ssd_decay_stacks(q):
    hpg = HEADS_PER_GROUP
    a_col = jnp.concatenate([q["acs"][:, r * HEAD_DIM:r * HEAD_DIM + 1] for r in range(hpg)], axis=0)
    a_row = jnp.concatenate([jnp.broadcast_to(q["acs_r"][r:r + 1, :], (CHUNK, CHUNK)) for r in range(hpg)], axis=0)
    ri = lax.broadcasted_iota(jnp.int32, (hpg * CHUNK, CHUNK), 0) % CHUNK
    ci = lax.broadcasted_iota(jnp.int32, (hpg * CHUNK, CHUNK), 1)
    diff = a_col - a_row
    lm = jnp.exp(jnp.where(ri >= ci, diff, -jnp.inf))
    lt = jnp.exp(jnp.where(ri <= ci, -diff, -jnp.inf))
    return lm, lt


def _pick_heads(stacked, hm):
    out = jnp.where(hm[0], stacked[:CHUNK], 0.0)
    for r in range(1, HEADS_PER_GROUP):
        out = out + jnp.where(hm[r], stacked[r * CHUNK:(r + 1) * CHUNK], 0.0)
    return out


def _mask_heads(x, hm):
    return jnp.concatenate([jnp.where(hm[r], x, 0.0) for r in range(HEADS_PER_GROUP)], axis=0)


SSD_GP_FWD, SSD_GP_BWD = 8, 8


def _ssd_specs(cidx, gp):
    hpg = HEADS_PER_GROUP
    return [
        pl.BlockSpec((CHUNK, gp * GROUP_W), lambda g, c: (cidx(c), g)),
        pl.BlockSpec((CHUNK, gp * D_STATE), lambda g, c: (cidx(c), D_INNER // (gp * D_STATE) + g)),
        pl.BlockSpec((CHUNK, gp * D_STATE), lambda g, c: (cidx(c), (D_INNER + 1024) // (gp * D_STATE) + g)),
        pl.BlockSpec((gp, CHUNK, hpg), lambda g, c: (g, cidx(c), 0)),
        pl.BlockSpec((gp, CHUNK, hpg), lambda g, c: (g, cidx(c), 0)),
        pl.BlockSpec((gp, hpg, CHUNK), lambda g, c: (g, 0, cidx(c))),
    ]


def _ssd_fwd(xa, dec, carry=None):
    t = xa.shape[0]
    nc = t // CHUNK
    gp, gw, ds = SSD_GP_FWD, GROUP_W, D_STATE
    grid = (N_GROUPS // gp, nc)
    c_in, c_out, c_sem = carry.specs() if carry is not None else ([], [], [])
    nci, nco = len(c_in), len(c_out)

    def body(*refs):
        xs_ref, b_ref, c_ref, dt_ref, acs_ref, acsr_ref = refs[:6]
        y_ref, prev_ref = refs[6 + nci:8 + nci]
        st_ref = refs[8 + nci + nco]
        if carry is not None:
            begin, end = carry.at_edges(grid, refs[6:6 + nci], refs[8 + nci:8 + nci + nco], refs[-2:])
            begin()

        @pl.when(pl.program_id(1) == 0)
        def _():
            st_ref[...] = jnp.zeros_like(st_ref)

        for k in range(gp):
            q = _ssd_q(dt_ref[k], acs_ref[k], acsr_ref[k])
            xdt = xs_ref[:, k * gw:(k + 1) * gw] * q["dt_exp"]
            bm = b_ref[:, k * ds:(k + 1) * ds].astype(BF16)
            cm = c_ref[:, k * ds:(k + 1) * ds].astype(BF16)
            cb = _dot(cm, bm, _NT)
            st = st_ref[k]
            prev_ref[0, k] = st
            lm, _ = _ssd_decay_stacks(q)
            y_diag = _pick_heads(_dot((_stack4(cb) * lm).astype(BF16), xdt.astype(BF16)), q["hm"])
            y_ref[:, k * gw:(k + 1) * gw] = y_diag + _dot(cm, st.astype(BF16)) * q["ea"]
            st_ref[k] = q["dec"] * st + _dot(bm, (xdt * q["ds"]).astype(BF16), _TN)
        if carry is not None:
            end()

    res = pl.pallas_call(
        body, name="ssd_fwd", grid=grid,
        in_specs=_ssd_specs(lambda c: c, gp) + c_in,
        out_specs=[pl.BlockSpec((CHUNK, gp * gw), lambda g, c: (c, g)),
                   pl.BlockSpec((1, gp, ds, gw), lambda g, c: (c, g, 0, 0))] + c_out,
        out_shape=[jax.ShapeDtypeStruct((t, D_INNER), F32), jax.ShapeDtypeStruct((nc, N_GROUPS, ds, gw), F32)]
        + (carry.outs if carry is not None else []),
        scratch_shapes=[pltpu.VMEM((gp, ds, gw), F32)] + c_sem,
        input_output_aliases=carry.aliases(6, 2) if carry is not None else {},
        compiler_params=_cparams(("arbitrary", "arbitrary") if carry is not None else ("parallel", "arbitrary")),
    )(xa, xa, xa, dec["dt"], dec["acs"], dec["acs_rows"], *(carry.ins if carry is not None else []))
    return res[0], res[1], res[2:]


def _ssd_bwd(dy, dsk, xa, prev, dec, carry=None):
    t = xa.shape[0]
    nc = t // CHUNK
    gp, gw, dstate = SSD_GP_BWD, GROUP_W, D_STATE
    hpg = HEADS_PER_GROUP

    grid = (N_GROUPS // gp, nc)
    c_in, c_out, c_sem = carry.specs() if carry is not None else ([], [], [])
    nci, nco = len(c_in), len(c_out)

    def body(*refs):
        ins, c_ins = refs[:11], refs[11:11 + nci]
        outs = refs[11 + nci:17 + nci]
        c_outs = refs[17 + nci:17 + nci + nco]
        dst_ref = refs[17 + nci + nco]
        dbias_ref, dalog_ref = outs[4], outs[5]
        if carry is not None:
            begin, end = carry.at_edges(grid, c_ins, c_outs, refs[-2:])
            begin()

        @pl.when(pl.program_id(1) == 0)
        def _():
            dst_ref[...] = jnp.zeros_like(dst_ref)
            dbias_ref[...] = jnp.zeros_like(dbias_ref)
            dalog_ref[...] = jnp.zeros_like(dalog_ref)

        for k in range(gp):
            one_group(k, *ins, *outs, dst_ref)
        if carry is not None:
            end()

    def one_group(k, xs_ref, b_ref, c_ref, dt_ref, acs_ref, acsr_ref, sig_ref, al_ref, dy_ref, sk_ref, prev_ref,
                  dxs_ref, db_ref, dc_ref, ddt_ref, dbias_ref, dalog_ref, dst_ref):
        wide = slice(k * gw, (k + 1) * gw)
        narrow = slice(k * dstate, (k + 1) * dstate)
        q = _ssd_q(dt_ref[k], acs_ref[k], acsr_ref[k])
        a_r = -jnp.exp(al_ref[k])
        hm, ds, dec, dt_exp = q["hm"], q["ds"], q["dec"], q["dt_exp"]
        xs = xs_ref[:, wide]
        xdt = xs * dt_exp
        xdtb = xdt.astype(BF16)
        bm = b_ref[:, narrow].astype(BF16)
        cm = c_ref[:, narrow].astype(BF16)
        cb = _dot(cm, bm, _NT)
        bc = _dot(bm, cm, _NT)
        dyv = dy_ref[:, wide]
        dye = (dyv * q["ea"]).astype(BF16)
        pst = prev_ref[0, k]
        dst = dst_ref[k]
        dstb = dst.astype(BF16)
        dx_state = ds * _dot(bm, dstb)
        onehot = [(lax.broadcasted_iota(jnp.int32, (1, hpg), 1) == r).astype(F32) for r in range(hpg)]
        lm, lt = _ssd_decay_stacks(q)
        dyb = dyv.astype(BF16)
        gl = _dot(_mask_heads(dyv, hm).astype(BF16), xdtb, _NT) * lm
        glt = _dot(_mask_heads(xdt, hm).astype(BF16), dyb, _NT) * lt
        bc4 = _stack4(bc)
        dxdt = dx_state + _pick_heads(_dot((bc4 * lt).astype(BF16), dyb), hm)
        wd = jnp.sum(gl * _stack4(cb), axis=1, keepdims=True) - jnp.sum(glt * bc4, axis=1, keepdims=True)
        dcb = gl[:CHUNK]
        dcbt = glt[:CHUNK]
        qa = wd[:CHUNK] * onehot[0]
        for r in range(1, hpg):
            dcb = dcb + gl[r * CHUNK:(r + 1) * CHUNK]
            dcbt = dcbt + glt[r * CHUNK:(r + 1) * CHUNK]
            qa = qa + wd[r * CHUNK:(r + 1) * CHUNK] * onehot[r]
        pstb = pst.astype(BF16)
        dc_ref[:, narrow] = _dot(dcb.astype(BF16), bm) + _dot(dye, pstb, _NT)
        db_ref[:, narrow] = _dot(dcbt.astype(BF16), cm) + _dot((xdt * ds).astype(BF16), dstb, _NT)
        dst_ref[k] = dec * dst + _dot(cm, dye, _TN)
        dxs_ref[:, wide] = dxdt * dt_exp + dyv * sk_ref[:, wide]

        t2 = xdt * dx_state
        t1 = dyv * (_dot(cm, pstb) * q["ea"]) - t2
        t4 = dxdt * xs
        last_row = jnp.sum(t2, axis=0, keepdims=True) + dec * jnp.sum(dst * pst, axis=0, keepdims=True)
        xd = jnp.zeros((CHUNK, hpg), F32)
        dal = jnp.zeros((1, hpg), F32)
        for r in range(hpg):
            qa = qa + jnp.sum(jnp.where(hm[r], t1, 0.0), axis=1, keepdims=True) * onehot[r]
            xd = xd + jnp.sum(jnp.where(hm[r], t4, 0.0), axis=1, keepdims=True) * onehot[r]
            dal = dal + jnp.sum(jnp.where(hm[r], last_row, 0.0), axis=1, keepdims=True) * onehot[r]
        rc = _cumsum(qa, 0, reverse=True) + dal
        rowid = (nc - 1 - pl.program_id(1)) * CHUNK + lax.broadcasted_iota(jnp.int32, (CHUNK, 1), 0)
        ddt_raw = jnp.where(rowid >= N_PAD, (rc * a_r + xd) * sig_ref[k], 0.0)
        ddt_ref[k] = ddt_raw
        dbias_ref[k] += jnp.sum(ddt_raw, axis=0, keepdims=True)
        dalog_ref[k] += jnp.sum(rc * q["dtc"], axis=0, keepdims=True) * a_r

    rev = lambda c: nc - 1 - c
    blk = pl.BlockSpec((CHUNK, gp * gw), lambda g, c: (rev(c), g))
    nblk = pl.BlockSpec((CHUNK, gp * dstate), lambda g, c: (rev(c), g))
    cols = pl.BlockSpec((gp, CHUNK, hpg), lambda g, c: (g, rev(c), 0))
    small = pl.BlockSpec((gp, 1, hpg), lambda g, c: (g, 0, 0))
    res = pl.pallas_call(
        body, name="ssd_bwd", grid=grid,
        in_specs=_ssd_specs(rev, gp) + [cols, small, blk, pl.BlockSpec((1, gp * gw), lambda g, c: (0, g)),
                                    pl.BlockSpec((1, gp, dstate, gw), lambda g, c: (rev(c), g, 0, 0))]
        + c_in,
        out_specs=[blk, nblk, nblk, cols, small, small] + c_out,
        out_shape=[jax.ShapeDtypeStruct((t, D_INNER), F32), jax.ShapeDtypeStruct((t, N_GROUPS * dstate), F32),
                   jax.ShapeDtypeStruct((t, N_GROUPS * dstate), F32), jax.ShapeDtypeStruct((N_GROUPS, t, hpg), F32),
                   jax.ShapeDtypeStruct((N_GROUPS, 1, hpg), F32), jax.ShapeDtypeStruct((N_GROUPS, 1, hpg), F32)]
        + (carry.outs if carry is not None else []),
        scratch_shapes=[pltpu.VMEM((gp, dstate, gw), F32)] + c_sem,
        compiler_params=_cparams(("arbitrary", "arbitrary") if carry is not None else ("parallel", "arbitrary")),
    )(xa, xa, xa, dec["dt"], dec["acs"], dec["acs_rows"], dec["sig"], dec["a_log"], dy, dsk.reshape(1, D_INNER), prev,
      *(carry.ins if carry is not None else []))
    dxs, db, dc, ddt, dbias, dalog = res[:6]
    ddt_raw = ddt.transpose(1, 0, 2).reshape(t, N_HEADS)
    return dxs, db, dc, ddt_raw, dbias.reshape(N_HEADS), dalog.reshape(N_HEADS), res[6:]


def _ssd_post_fwd(y, xa, proj, dsk, nw):
    t = y.shape[0]
    di = D_INNER
    tm = _tile(t, (384, 128))

    def body(y_ref, xs_ref, z_ref, dsk_ref, nw_ref, o_ref):
        z = z_ref[...]
        yz = (y_ref[...] + xs_ref[...] * dsk_ref[...]) * (z * _sigmoid(z))
        nwv = nw_ref[...]
        for g in range(N_GROUPS):
            sl = slice(g * GROUP_W, (g + 1) * GROUP_W)
            v = yz[:, sl]
            rg = lax.rsqrt(jnp.mean(v * v, axis=-1, keepdims=True) + EPS)
            o_ref[:, sl] = (v * rg * nwv[:, sl]).astype(BF16)

    row = pl.BlockSpec((tm, di), lambda i: (i, 0))
    vec = pl.BlockSpec((1, di), lambda i: (0, 0))
    return pl.pallas_call(
        body, name="ssd_post_fwd", grid=(t // tm,),
        in_specs=[row, row, row, vec, vec], out_specs=row, out_shape=jax.ShapeDtypeStruct((t, di), BF16),
        compiler_params=_cparams(("parallel",)),
    )(y, xa, proj, dsk.reshape(1, di), nw.reshape(1, di))


def _ssd_post_bwd(dyn, y, xa, proj, dsk, nw, dproj):
    t = y.shape[0]
    di = D_INNER
    tm = CHUNK

    def body(dyn_ref, y_ref, xs_ref, z_ref, dsk_ref, nw_ref, _, dy_ref, dz_ref, dnw_ref, ddsk_ref):
        @pl.when(pl.program_id(0) == 0)
        def _():
            dnw_ref[...] = jnp.zeros_like(dnw_ref)
            ddsk_ref[...] = jnp.zeros_like(ddsk_ref)

        for g in range(N_GROUPS):
            sl = slice(g * GROUP_W, (g + 1) * GROUP_W)
            z = z_ref[:, sl]
            sig = _sigmoid(z)
            s = z * sig
            xs = xs_ref[:, sl]
            dskv = dsk_ref[:, sl]
            yt = y_ref[:, sl] + xs * dskv
            yz = yt * s
            rg = lax.rsqrt(jnp.mean(yz * yz, axis=-1, keepdims=True) + EPS)
            xhat = yz * rg
            dynv = dyn_ref[:, sl]
            gg = dynv * nw_ref[:, sl]
            dnw_ref[:, sl] += jnp.sum(dynv * xhat, axis=0, keepdims=True)
            dyz = rg * (gg - xhat * jnp.mean(gg * xhat, axis=-1, keepdims=True))
            dyt = dyz * s
            dy_ref[:, sl] = dyt
            dz_ref[:, sl] = (dyz * yt * (sig * (1.0 + z * (1.0 - sig)))).astype(BF16)
            ddsk_ref[:, sl] += jnp.sum(dyt * xs, axis=0, keepdims=True)

    row = pl.BlockSpec((tm, di), lambda i: (i, 0))
    vec = pl.BlockSpec((1, di), lambda i: (0, 0))
    dy, dz, dnw, ddsk = pl.pallas_call(
        body, name="ssd_post_bwd", grid=(t // tm,),
        in_specs=[row, row, row, row, vec, vec, pl.BlockSpec(memory_space=pl.ANY)], out_specs=[row, row, vec, vec],
        out_shape=[jax.ShapeDtypeStruct((t, di), F32), jax.ShapeDtypeStruct(dproj.shape, BF16),
                   jax.ShapeDtypeStruct((1, di), F32), jax.ShapeDtypeStruct((1, di), F32)],
        input_output_aliases={6: 1},
        compiler_params=_cparams(("arbitrary",)),
    )(dyn, y, xa, proj, dsk.reshape(1, di), nw.reshape(1, di), dproj)
    return dy, dz, dnw.reshape(di), ddsk.reshape(N_HEADS, HEAD_DIM).sum(axis=1)


SHARD_COLS = IN_COLS // N_CHIPS


def _split_w_in(w_in_sh):
    nl = w_in_sh.shape[0]
    w = w_in_sh.transpose(0, 2, 1, 3).reshape(nl, D_MODEL, IN_COLS)
    main = jnp.concatenate([w[..., OFF_Z:OFF_XBC], w[..., OFF_GATE:], w[..., :OFF_Z], w[..., OFF_XBC:OFF_DT]], axis=-1)
    dt = jnp.pad(w[..., OFF_DT:OFF_GATE], ((0, 0), (0, 0), (0, DT_PAD - N_HEADS)))
    return main, dt


def _merge_dw_in(dmain, ddt):
    full = jnp.concatenate([dmain[:, PPOOL:PXBC], dmain[:, PZ:PGATE], dmain[:, PXBC:], ddt[:, :N_HEADS], dmain[:, PGATE:PPOOL]],
                           axis=1)
    return full.reshape(D_MODEL, N_CHIPS, SHARD_COLS).transpose(1, 0, 2)


def _relu2_epilogue(acc):
    hid = jnp.maximum(acc, 0.0)
    return acc, hid * hid


def _relu2_bwd_epilogue(acc, pre):
    return (acc * 2.0 * jnp.maximum(pre, 0.0),)


def _add_epilogue(acc, res):
    return (acc + res,)


def _norm_bwd_epilogue(acc, dres, h, w):
    r = lax.rsqrt(jnp.mean(h * h, axis=-1, keepdims=True) + EPS)
    xhat = h * r
    g = acc * w
    dh = dres + r * (g - xhat * jnp.mean(g * xhat, axis=-1, keepdims=True))
    return dh, jnp.sum(acc * xhat, axis=0, keepdims=True)


def _add_norm_bwd_epilogue(acc, part, dres, h, w):
    return _norm_bwd_epilogue(acc + part, dres, h, w)


def _add_norm_epilogue(acc, res, w):
    x = acc + res
    r = lax.rsqrt(jnp.mean(x * x, axis=-1, keepdims=True) + EPS)
    return x, x * r * w


def _w_spec(rows, cols, fn):
    return pl.BlockSpec((None, None, rows, cols), fn)


def _with_carry(stage, call):
    carry = stage[0]() if stage is not None else None
    res = call(carry)
    if carry is None:
        return res
    outs, got = res
    stage[1](got)
    return outs[0] if len(outs) == 1 else outs


def _layer_fwd(h, u, w, big, layer, stages=None, next_norm_w=None):
    t = h.shape[0]
    d = D_MODEL
    tm = _tile(t, (1408, 384, 128))
    nt = t // tm
    row = _bs((tm, d), lambda i, j, k: (i, 0))
    vec = _bs((1, d), lambda i, j, k: (0, 0))
    stages = stages or {}
    s = {"h": h}
    if u is None:
        u = _norm_fwd(h, w["mix_norm_w"], "norm_mix")
    proj = _with_carry(stages.get("a_ici"), lambda carry: _mm(
        f"mm_proj_{layer}", _NN, (nt, PCOLS // d, 1), u, row,
        big["w_in_main"], _bs((None, d, d), lambda i, j, k: (0, 0, j)),
        [((t, PCOLS), F32, _bs((tm, d), lambda i, j, k: (i, j)))], carry=carry))
    dt_raw = _mm(f"mm_dt_{layer}", _NN, (nt, 1, 1), u, row,
                 big["w_in_dt"], _bs((None, d, DT_PAD), lambda i, j, k: (0, 0, 0)),
                 [((t, DT_PAD), F32, _bs((tm, DT_PAD), lambda i, j, k: (i, 0)))])
    pooled, yg, ypm, y_pool = _pool_fwd(proj, big["pool_w_group"], w["pool_scale"], big["w_pool_up"], layer)
    xa, conv_pre = _conv_fwd(proj, w["conv_w"], w["conv_b"])
    dec = _ssd_decay(dt_raw, w["dt_bias"], w["a_log"])
    stage = stages.get("b_ici")
    carry = stage[0]() if stage is not None else None
    y, prev, got = _ssd_fwd(xa, dec, carry)
    if carry is not None:
        stage[1](got)
    yn = _ssd_post_fwd(y, xa, proj, w["d_skip_exp"], w["ssd_norm_w"])
    rs = D_INNER // N_CHIPS
    y_ssd = _with_carry(stages.get("a_d2d"), lambda carry: _mm(
        f"mm_ssd_out_{layer}", _NN, (nt, 1, 1), yn, _bs((tm, D_INNER), lambda i, j, k: (i, 0)),
        big["w_ssd_out"], _bs((None, N_CHIPS, rs, d), lambda i, j, k: (0, 0, 0, 0)), [((t, d), F32, row)], carry=carry))
    merged = _gate_fwd(proj, w["b_gate"], y_pool, y_ssd)
    ro = d // N_CHIPS
    h1, v = _mm(f"mm_o_{layer}", _NN, (nt, 1, 1), merged, row,
                big["w_o"], _bs((None, N_CHIPS, ro, d), lambda i, j, k: (0, 0, 0, 0)),
                [((t, d), F32, row), ((t, d), BF16, row)],
                extras=[(h, row), (w["mlp_norm_w"].reshape(1, d), vec)], epilogue=_add_norm_epilogue)
    tile = _bs((tm, d), lambda i, j, k: (i, j))
    pre, act = _with_carry(stages.get("b_d2d"), lambda carry: _mm(
        f"mm_ff1_{layer}", _NN, (nt, N_CHIPS, 1), v, row,
        big["w_ff1"], _w_spec(d, d, lambda i, j, k: (0, j, 0, 0)),
        [((t, D_FF), F32, tile), ((t, D_FF), BF16, tile)], epilogue=_relu2_epilogue, carry=carry))
    a_spec = _bs((tm, d), lambda i, j, k: (i, k))
    b_spec = _w_spec(d, d, lambda i, j, k: (0, k, 0, 0))
    if next_norm_w is None:
        h2 = _mm(f"mm_ff2_{layer}", _NN, (nt, 1, N_CHIPS), act, a_spec, big["w_ff2"], b_spec, [((t, d), F32, row)],
                 extras=[(h1, row)], epilogue=_add_epilogue)
        u_next = None
    else:
        h2, u_next = _mm(f"mm_ff2_{layer}", _NN, (nt, 1, N_CHIPS), act, a_spec, big["w_ff2"], b_spec,
                         [((t, d), F32, row), ((t, d), BF16, row)],
                         extras=[(h1, row), (next_norm_w.reshape(1, d), vec)], epilogue=_add_norm_epilogue)
    s.update(u=u, proj=proj, dec=dec, conv_pre=conv_pre, pooled=pooled, yg=yg, ypm=ypm, y_pool=y_pool, xa=xa, y=y, prev=prev, yn=yn,
             y_ssd=y_ssd, merged=merged, h1=h1, v=v, pre=pre, act=act)
    return h2, u_next, s


def _dw(name, layer, a, b, shard_shape, by, tk, carry=None):
    t = a.shape[0]
    nk = t // tk
    rows, cols = shard_shape
    if by == "rows":
        grid = (N_CHIPS, 1, nk)
        a_spec = _bs((tk, rows), lambda i, j, k: (k, i))
        b_spec = _bs((tk, cols), lambda i, j, k: (k, 0))
        o_spec = _bs((None, rows, cols), lambda i, j, k: (i, 0, 0))
    else:
        grid = (1, N_CHIPS, nk)
        a_spec = _bs((tk, rows), lambda i, j, k: (k, 0))
        b_spec = _bs((tk, cols), lambda i, j, k: (k, j))
        o_spec = _bs((None, rows, cols), lambda i, j, k: (j, 0, 0))
    return _mm(f"{name}_{layer}", _TN, grid, a, a_spec, b, b_spec, [((N_CHIPS, rows, cols), F32, o_spec)], carry=carry)


EARLY = ("w_ff2", "w_ff1", "w_o", "w_ssd_out", "w_pool_up", "pool_w_group")


def _layer_bwd(dh, s, w, big, layer, red=None, last=False):
    t = dh.shape[0]
    d = D_MODEL
    tm = _tile(t, (1408, 384, 128))
    tk = _tile(t, (1408, 384, 128))
    nt = t // tm
    row = _bs((tm, d), lambda i, j, k: (i, 0))
    tile = _bs((tm, d), lambda i, j, k: (i, j))
    g, gb = {}, {}
    carry = red.swap_carry() if red is not None else None
    dpre = _mm(f"mm_dact_{layer}", _NT, (nt, N_CHIPS, 1), dh, row,
               big["w_ff2"], _w_spec(d, d, lambda i, j, k: (0, j, 0, 0)), [((t, D_FF), BF16, tile)],
               extras=[(s["pre"], tile)], epilogue=_relu2_bwd_epilogue, carry=carry)
    if carry is not None:
        (dpre,), got = dpre
        red.after_swap(got)
    gb["w_ff2"] = _dw("mm_dw_ff2", layer, s["act"], dh, (d, d), "rows", tk)
    vec = _bs((1, d), lambda i, j, k: (0, 0))
    th = _tile(t, (704, 384, 128))
    half_row = _bs((th, d), lambda i, j, k: (i, 0))
    dh1, dnw = _mm(f"mm_dv_{layer}", _NT, (t // th, 1, N_CHIPS), dpre, _bs((th, d), lambda i, j, k: (i, k)),
                   big["w_ff1"], _w_spec(d, d, lambda i, j, k: (0, k, 0, 0)), [((t, d), F32, half_row), ((1, d), F32, vec)],
                   extras=[(dh, half_row), (s["h1"], half_row), (w["mlp_norm_w"].reshape(1, d), vec)],
                   epilogue=_norm_bwd_epilogue, summed=(1,))
    g["mlp_norm_w"] = dnw.reshape(d)
    gb["w_ff1"] = _dw("mm_dw_ff1", layer, s["v"], dpre, (d, d), "cols", tk)
    ro = d // N_CHIPS
    dmerged = _mm(f"mm_dmerged_{layer}", _NT, (nt, N_CHIPS, 1), dh1, row,
                  big["w_o"], _w_spec(ro, d, lambda i, j, k: (0, j, 0, 0)),
                  [((t, d), F32, _bs((tm, ro), lambda i, j, k: (i, j)))])
    gb["w_o"] = _dw("mm_dw_o", layer, s["merged"], dh1, (ro, d), "rows", tk)
    dy_pool, dy_ssd, dproj, g["b_gate"] = _gate_bwd(dmerged, s["proj"], w["b_gate"], s["y_pool"], s["y_ssd"])
    rs = D_INNER // N_CHIPS
    dyn = _mm(f"mm_dyn_{layer}", _NT, (nt, N_CHIPS, 1), dy_ssd, row,
              big["w_ssd_out"], _w_spec(rs, d, lambda i, j, k: (0, j, 0, 0)),
              [((t, D_INNER), F32, _bs((tm, rs), lambda i, j, k: (i, j)))])
    gb["w_ssd_out"] = _dw("mm_dw_ssd_out", layer, s["yn"], dy_ssd, (rs, d), "rows", tk)
    dy, dproj, g["ssd_norm_w"], g["d_skip"] = _ssd_post_bwd(dyn, s["y"], s["xa"], s["proj"], w["d_skip_exp"],
                                                            w["ssd_norm_w"], dproj)
    carry = red.ici_carry() if red is not None else None
    dxs, db, dc, ddt_raw, g["dt_bias"], g["a_log"], got = _ssd_bwd(dy, w["d_skip_exp"], s["xa"], s["prev"], s["dec"], carry)
    if carry is not None:
        red.after_ici(got)
    own = red is not None and last
    dproj, dcw1, dcb1 = _conv_bwd(dxs, 0, s["proj"], s["conv_pre"], w["conv_w"], dproj)
    dproj, dcw2, dcb2 = _conv_bwd(db, 2, s["proj"], s["conv_pre"], w["conv_w"], dproj)
    dproj, dcw3, dcb3 = _conv_bwd(dc, 3, s["proj"], s["conv_pre"], w["conv_w"], dproj)
    g["conv_w"] = jnp.concatenate([dcw1, dcw2, dcw3], axis=1)
    g["conv_b"] = jnp.concatenate([dcb1, dcb2, dcb3])
    q, dyg, g["pool_scale"] = _pool_bwd_a(dy_pool, s["yg"], w["pool_scale"], big["pool_w_group"], big["w_pool_up"], layer)
    gb["w_pool_up"] = _dw("mm_dw_pool_up", layer, s["ypm"], dy_pool, (ro, d), "rows", tk)
    gb["pool_w_group"] = _pool_dwg(s["pooled"], dyg, layer).reshape(N_CHIPS, POOL_GDIM, POOL_GDIM)
    dproj = _pool_bwd_b(q, dproj)
    if own:
        red.push(layer, {n: gb[n] for n in EARLY})
    ddt = jnp.pad(ddt_raw.astype(BF16), ((0, 0), (0, DT_PAD - N_HEADS)))
    nk = PCOLS // d
    du = _with_carry((red.swap_carry, red.after_swap) if own else None, lambda carry: _mm(
        f"mm_du_{layer}", _NT, (nt, 1, nk), dproj, _bs((tm, d), lambda i, j, k: (i, k)),
        big["w_in_main"], _bs((None, d, d), lambda i, j, k: (0, 0, k)), [((t, d), F32, row)], carry=carry))
    dh0, dnw = _mm(f"mm_du_dt_{layer}", _NT, (t // th, 1, 1), ddt, _bs((th, DT_PAD), lambda i, j, k: (i, 0)),
                   big["w_in_dt"], _bs((None, d, DT_PAD), lambda i, j, k: (0, 0, 0)),
                   [((t, d), F32, half_row), ((1, d), F32, vec)],
                   extras=[(du, half_row), (dh1, half_row), (s["h"], half_row), (w["mix_norm_w"].reshape(1, d), vec)],
                   epilogue=_add_norm_bwd_epilogue, summed=(1,))
    g["mix_norm_w"] = dnw.reshape(d)
    ntk = t // tk
    u_spec = _bs((tk, d), lambda i, j, k: (k, 0))
    dmain = _with_carry((red.ici_carry, red.after_ici) if own else None, lambda carry: _mm(
        f"mm_dw_in_{layer}", _TN, (1, nk, ntk), s["u"], u_spec, dproj, _bs((tk, d), lambda i, j, k: (k, j)),
        [((d, PCOLS), F32, _bs((d, d), lambda i, j, k: (0, j)))], carry=carry))
    ddtw = _mm(f"mm_dw_dt_{layer}", _TN, (1, 1, ntk), s["u"], u_spec, ddt, _bs((tk, DT_PAD), lambda i, j, k: (k, 0)),
               [((d, DT_PAD), F32, _bs((d, DT_PAD), lambda i, j, k: (0, 0)))])
    gb["w_in"] = _merge_dw_in(dmain, ddtw)
    if red is not None:
        red.push(layer, {n: gb[n] for n in BIG if not (own and n in EARLY)})
    return dh0, g, gb


BIG = ("w_in", "pool_w_group", "w_pool_up", "w_ssd_out", "w_o", "w_ff1", "w_ff2")
PER_LAYER = ("mix_norm_w", "w_in", "b_gate", "pool_w_group", "pool_scale", "w_pool_up", "conv_w", "conv_b", "dt_bias",
             "a_log", "d_skip", "ssd_norm_w", "w_ssd_out", "w_o", "mlp_norm_w", "w_ff1", "w_ff2")


SMALL_PER_LAYER = tuple(n for n in PER_LAYER if n not in BIG)


SHARD_SHAPE = {"w_in": (1024, SHARD_COLS), "pool_w_group": (4, WG_ROWS, POOL_GDIM), "w_pool_up": (WUP_ROWS, D_MODEL),
               "w_ssd_out": (D_INNER // N_CHIPS, D_MODEL), "w_o": (D_MODEL // N_CHIPS, D_MODEL),
               "w_ff1": (D_MODEL, D_FF // N_CHIPS), "w_ff2": (D_FF // N_CHIPS, D_MODEL)}


def _layer_view(mats):
    big = {n: mats[n].reshape((1, N_CHIPS) + SHARD_SHAPE[n]) for n in BIG if n != "w_in"}
    big["w_in_main"], big["w_in_dt"] = _split_w_in(mats["w_in"])
    return big


def _local_step(x, tgt, weights, small, red=None):
    seq = x.shape[0]
    h = jnp.concatenate([jnp.zeros((N_PAD, D_MODEL), F32), small["meta_tokens"], x], axis=0)
    saved, ws, bigs = [], [], []
    u = None
    for i in range(DEPTH):
        w = {n: small[n][i] for n in SMALL_PER_LAYER}
        w["d_skip_exp"] = jnp.repeat(w["d_skip"], HEAD_DIM)
        big = weights.layer(i)
        next_norm_w = small["mix_norm_w"][i + 1] if i + 1 < DEPTH else None
        h, u, s = _layer_fwd(h, u, w, big, i, weights.stages(i), next_norm_w)
        saved.append(s)
        ws.append(w)
        bigs.append(big)
    loss, dh, g_final = _loss_head(h, small["final_norm_w"], tgt)
    layer_g, layer_gb = [None] * DEPTH, [None] * DEPTH
    for i in reversed(range(DEPTH)):
        dh, layer_g[i], layer_gb[i] = _layer_bwd(dh, saved[i], ws[i], bigs[i], i, red, last=(i == 0))
    grads = {n: jnp.stack([layer_g[i][n] for i in range(DEPTH)]) for n in SMALL_PER_LAYER}
    grads["final_norm_w"] = g_final
    grads["meta_tokens"] = dh[N_PAD:ROW_X]
    return loss, dh[ROW_X:ROW_X + seq], layer_gb, grads


MESH = pl.DeviceIdType.MESH
LANES = 128
ANY = pl.BlockSpec(memory_space=pl.ANY)


def _place():
    x, y, c = lax.axis_index("x"), lax.axis_index("y"), lax.axis_index("c")
    chips = [(1 - x, y), (x, 1 - y), (1 - x, 1 - y)]
    return x, y, c, chips


def _remote(src, dst, send_sem, recv_sem, to):
    return pltpu.make_async_remote_copy(src_ref=src, dst_ref=dst, send_sem=send_sem, recv_sem=recv_sem,
                                        device_id=to, device_id_type=MESH)


class _WeightGatherer:
    GROUPS = {"a": ("w_in",), "b": tuple(n for n in BIG if n != "w_in")}

    def __init__(self, mine):
        self.mine = mine
        self.landing = {}
        self.ready = {}

    def _ici(self, layer, group):
        names = self.GROUPS[group]
        srcs = [self.mine[n] for n in names]
        outs = [jax.ShapeDtypeStruct((1, N_CHIPS) + s.shape[1:], BF16) for s in srcs]

        def make(ins, out, send_sems, recv_sems):
            x, y, c, chips = _place()
            me = 2 * x + y
            cps = []
            for p, ref in enumerate(ins):
                half = ref.shape[1] // 2
                rows = pl.ds(c * half, half)
                cps += [_remote(ref.at[layer, rows, :], out[p].at[0, me, rows, :], send_sems.at[3 * p + j],
                                recv_sems.at[3 * p + j], (*chip, c)) for j, chip in enumerate(chips)]
            return cps

        return _Carry(srcs, outs, 3 * len(names), make)

    def _d2d(self, layer, group):
        names = self.GROUPS[group]
        n = len(names)
        bufs = self.landing[(layer, group)]
        outs = [jax.ShapeDtypeStruct(b.shape, b.dtype) for b in bufs]

        def make(ins, out, send_sems, recv_sems):
            x, y, c, chips = _place()
            me = 2 * x + y
            sibling = (x, y, 1 - c)
            cps = []
            for p in range(n):
                half = out[p].shape[2] // 2
                for j, (cx, cy) in enumerate(chips):
                    blk = out[p].at[0, 2 * cx + cy, pl.ds(c * half, half), :]
                    cps.append(_remote(blk, blk, send_sems.at[4 * p + j], recv_sems.at[4 * p + j], sibling))
                cps.append(_remote(ins[n + p].at[layer], out[p].at[0, me], send_sems.at[4 * p + 3],
                                   recv_sems.at[4 * p + 3], sibling))
            return cps

        return _Carry(list(bufs) + [self.mine[m] for m in names], outs, 4 * n, make, alias={p: p for p in range(n)})

    def _landed(self, layer, group, bufs):
        self.landing[(layer, group)] = bufs

    def _done(self, layer, group, bufs):
        self.ready.setdefault(layer, {}).update(zip(self.GROUPS[group], bufs))

    def layer(self, i):
        if i == 0:
            for g in self.GROUPS:
                self._landed(0, g, _run_carry(self._ici(0, g), f"gather_ici_{g}_0"))
            for g in self.GROUPS:
                self._done(0, g, _run_carry(self._d2d(0, g), f"gather_d2d_{g}_0"))
        return _layer_view(self.ready[i])

    def stages(self, i):
        nxt = i + 1
        if nxt == DEPTH:
            return None
        st = {}
        for g in self.GROUPS:
            st[f"{g}_ici"] = (functools.partial(self._ici, nxt, g), functools.partial(self._landed, nxt, g))
            st[f"{g}_d2d"] = (functools.partial(self._d2d, nxt, g), functools.partial(self._done, nxt, g))
        return st


def _exchange_small(v, reduce, name):
    rows_per = v.shape[0]
    vm = pl.BlockSpec(memory_space=pltpu.VMEM)

    def body(v_ref, out_ref, *scratch):
        if reduce:
            land_ref, send_sems, recv_sems, local_sem = scratch
        else:
            land_ref = out_ref
            send_sems, recv_sems, local_sem = scratch
        x, y, c, chips = _place()
        me, sibling = (x, y, c), (x, y, 1 - c)

        def rows(px, py, pc):
            return land_ref.at[4 * px + 2 * py + pc]

        def copy(k, block, to, src=None):
            return _remote(rows(*block) if src is None else src, rows(*block), send_sems.at[k], recv_sems.at[k], to)

        mine = pltpu.make_async_copy(v_ref, rows(*me), local_sem)
        mine.start()
        first = [copy(0, me, sibling, src=v_ref)]
        first += [copy(1 + j, me, (*chip, c), src=v_ref) for j, chip in enumerate(chips)]
        for cp in first:
            cp.start()
        passed = [copy(4 + j, (*chip, c), sibling) for j, chip in enumerate(chips)]
        for j, chip in enumerate(chips):
            copy(1 + j, (*chip, c), me).wait_recv()
            passed[j].start()
        copy(0, sibling, me).wait_recv()
        for j, chip in enumerate(chips):
            copy(4 + j, (*chip, 1 - c), me).wait_recv()
        for cp in first + passed:
            cp.wait_send()
        mine.wait()
        if reduce:
            acc = land_ref[0]
            for d in range(1, 8):
                acc = acc + land_ref[d]
            out_ref[...] = acc

    sems = [pltpu.SemaphoreType.DMA((7,)), pltpu.SemaphoreType.DMA((7,)), pltpu.SemaphoreType.DMA]
    if reduce:
        out_shape = jax.ShapeDtypeStruct((rows_per, LANES), F32)
        scratch = [pltpu.VMEM((8, rows_per, LANES), F32)] + sems
    else:
        out_shape = jax.ShapeDtypeStruct((8, rows_per, LANES), F32)
        scratch = sems
    return pl.pallas_call(
        body, name=name, in_specs=[vm], out_specs=vm, out_shape=out_shape, scratch_shapes=scratch,
        compiler_params=pltpu.CompilerParams(vmem_limit_bytes=VMEM_LIMIT),
    )(v)


def _run_carry(carry, name):
    c_in, c_out, c_sem = carry.specs()
    n_in, n_out = len(c_in), len(c_out)

    def body(*refs):
        cps = carry.make(refs[:n_in], refs[n_in:n_in + n_out], *refs[n_in + n_out:])
        for cp in cps:
            cp.start()
        for cp in cps:
            cp.wait()

    return pl.pallas_call(body, name=name, in_specs=c_in, out_specs=c_out, out_shape=carry.outs,
                          scratch_shapes=c_sem, input_output_aliases=carry.aliases(0, 0))(*carry.ins)


def _row_tile(rows, last, itemsize=4, budget=2 * 1024 * 1024):
    return _tile(rows, tuple(t for t in (2048, 1024, 512, 256, 128, 64, 32, 16) if t * last * itemsize <= budget))


def _rs_add(g, got, core, name):
    _, half, last = got.shape
    tr = _row_tile(half, last)
    nb = half // tr

    def body(c_ref, g_ref, got_ref, o_ref):
        o_ref[...] = (g_ref[...] + got_ref[...]).astype(BF16)

    blk = pl.BlockSpec((None, tr, last), lambda s, i, c_ref: (s, i, 0))
    return pl.pallas_call(
        body, name=name,
        grid_spec=pltpu.PrefetchScalarGridSpec(
            num_scalar_prefetch=1, grid=(N_CHIPS, nb),
            in_specs=[pl.BlockSpec((None, tr, last), lambda s, i, c_ref: (s, c_ref[0] * nb + i, 0)), blk],
            out_specs=blk),
        out_shape=jax.ShapeDtypeStruct(got.shape, BF16),
        compiler_params=_cparams(("parallel", "parallel")),
    )(core.reshape(1).astype(jnp.int32), g, got)


def _rs_sum(own, got, acc, layer, chip, core, name):
    _, half, last = own.shape
    tr = _row_tile(half, last, budget=1024 * 1024)
    nb = half // tr
    ni = 0 if acc is None else 1

    def body(k_ref, c_ref, own_ref, got_ref, *rest):
        o_ref = rest[-1]
        tot = own_ref[...].astype(F32)
        for j in range(3):
            tot = tot + got_ref[j].astype(F32)
        o_ref[...] = tot

    return pl.pallas_call(
        body, name=name,
        grid_spec=pltpu.PrefetchScalarGridSpec(
            num_scalar_prefetch=2, grid=(nb,),
            in_specs=[pl.BlockSpec((None, tr, last), lambda i, k_ref, c_ref: (k_ref[0], i, 0)),
                      pl.BlockSpec((3, tr, last), lambda i, k_ref, c_ref: (0, i, 0))]
            + [pl.BlockSpec(memory_space=pl.ANY)] * ni,
            out_specs=pl.BlockSpec((None, tr, last), lambda i, k_ref, c_ref: (layer, c_ref[0] * nb + i, 0))),
        out_shape=jax.ShapeDtypeStruct((DEPTH, 2 * half, last), F32),
        input_output_aliases={4: 0} if ni else {},
        compiler_params=_cparams(("parallel",)),
    )(chip.reshape(1).astype(jnp.int32), core.reshape(1).astype(jnp.int32), own, got, *([acc] if ni else []))


def _rs_share(arrs):
    n = len(arrs)

    def body(*refs):
        out = refs[n:2 * n]
        send_sems, recv_sems = refs[2 * n:]
        x, y, c, _ = _place()
        sibling = (x, y, 1 - c)
        cps = []
        for p in range(n):
            half = out[p].shape[1] // 2
            mine = out[p].at[:, pl.ds(c * half, half), :]
            cps.append(_remote(mine, mine, send_sems.at[p], recv_sems.at[p], sibling))
        for cp in cps:
            cp.start()
        for p, cp in enumerate(cps):
            cp.wait_send()
            half = out[p].shape[1] // 2
            other = out[p].at[:, pl.ds((1 - c) * half, half), :]
            _remote(other, other, send_sems.at[p], recv_sems.at[p], sibling).wait_recv()

    return pl.pallas_call(
        body, name="rs_share", in_specs=[ANY] * n, out_specs=[ANY] * n,
        out_shape=[jax.ShapeDtypeStruct(a.shape, a.dtype) for a in arrs],
        input_output_aliases={p: p for p in range(n)},
        scratch_shapes=[pltpu.SemaphoreType.DMA((n,)), pltpu.SemaphoreType.DMA((n,))],
    )(*arrs)


class _GradReducer:
    def __init__(self, chip, core):
        self.chip, self.core = chip, core
        self.pending = None
        self.sums = None
        self.acc = {n: None for n in BIG}

    def push(self, layer, gb):
        assert self.pending is None
        self.pending = (layer, list(gb), list(gb.values()))

    def swap_carry(self):
        if self.pending is None:
            return None
        _, _, gs = self.pending
        outs = [jax.ShapeDtypeStruct((g.shape[0], g.shape[1] // 2, g.shape[2]), F32) for g in gs]

        def make(ins, out, send_sems, recv_sems):
            x, y, c, _ = _place()
            cps = []
            for p, ref in enumerate(ins):
                half = ref.shape[1] // 2
                cps.append(_remote(ref.at[:, pl.ds((1 - c) * half, half), :], out[p], send_sems.at[p], recv_sems.at[p],
                                   (x, y, 1 - c)))
            return cps

        return _Carry(gs, outs, len(gs), make)

    def after_swap(self, got):
        layer, names, gs = self.pending
        self.pending = None
        self.sums = (layer, names, [_rs_add(g, a, self.core, f"rs_add_{n}_{layer}") for n, g, a in zip(names, gs, got)])

    def ici_carry(self):
        if self.sums is None:
            return None
        _, _, ps = self.sums
        outs = [jax.ShapeDtypeStruct((3,) + p.shape[1:], BF16) for p in ps]

        def make(ins, out, send_sems, recv_sems):
            x, y, c, chips = _place()
            return [_remote(ins[p].at[2 * cx + cy], out[p].at[j], send_sems.at[3 * p + j], recv_sems.at[3 * p + j],
                            (cx, cy, c)) for p in range(len(ins)) for j, (cx, cy) in enumerate(chips)]

        return _Carry(ps, outs, 3 * len(ps), make)

    def after_ici(self, got):
        layer, names, ps = self.sums
        self.sums = None
        for n, p, a in zip(names, ps, got):
            self.acc[n] = _rs_sum(p, a, self.acc[n], layer, self.chip, self.core, f"rs_sum_{n}_{layer}")

    def finish(self):
        if self.pending is not None:
            self.after_swap(_run_carry(self.swap_carry(), "rs_swap_last"))
        if self.sums is not None:
            self.after_ici(_run_carry(self.ici_carry(), "rs_ici_last"))
        return dict(zip(BIG, _rs_share([self.acc[n] for n in BIG])))


def _adamw(w, g, m, v, name):
    shape = w.shape
    if len(shape) == 2:
        shape3 = (1,) + shape
    else:
        shape3 = (-1,) + shape[-2:]
    w2, g2, m2, v2 = (a.reshape(shape3) for a in (w, g, m, v))
    lead, rows, last = w2.shape
    tr = max([t for t in range(8, rows + 1, 8) if rows % t == 0 and t * last * 4 <= 2 * 1024 * 1024] or [rows])

    def body(w_ref, g_ref, m_ref, v_ref, d_ref, nm_ref, nv_ref):
        gv = g_ref[...]
        mn = ADAM_B1 * m_ref[...] + (1.0 - ADAM_B1) * gv
        vn = ADAM_B2 * v_ref[...] + (1.0 - ADAM_B2) * (gv * gv)
        m_hat = mn / (1.0 - ADAM_B1 ** ADAM_STEP)
        v_hat = vn / (1.0 - ADAM_B2 ** ADAM_STEP)
        d_ref[...] = -ADAM_LR * (m_hat / (jnp.sqrt(v_hat) + ADAM_EPS) + ADAM_WD * w_ref[...])
        nm_ref[...] = mn
        nv_ref[...] = vn

    blk = pl.BlockSpec((None, tr, last), lambda a, i: (a, i, 0))
    outs = pl.pallas_call(
        body, name=name, grid=(lead, rows // tr), in_specs=[blk] * 4, out_specs=[blk] * 3,
        out_shape=[jax.ShapeDtypeStruct((lead, rows, last), F32)] * 3,
        compiler_params=_cparams(("parallel", "parallel")),
    )(w2, g2, m2, v2)
    return tuple(o.reshape(shape) for o in outs)


SHARD_AXIS = {"conv_w": 2, "meta_tokens": 1}
SMALL_SHARDED = ("conv_w", "meta_tokens")
REPLICATED = ("mix_norm_w", "b_gate", "pool_scale", "conv_b", "dt_bias", "a_log", "d_skip", "ssd_norm_w", "mlp_norm_w",
              "final_norm_w")


def _flatten(arrs, dtype, row_mult):
    flat = jnp.concatenate([a.astype(dtype).reshape(-1) for a in arrs])
    n = flat.shape[0]
    rows = -(-n // (LANES * row_mult)) * row_mult
    return jnp.pad(flat, (0, rows * LANES - n)).reshape(rows, LANES)


def _unflatten(flat2d, shapes):
    flat = flat2d.reshape(-1)
    out, off = [], 0
    for sh in shapes:
        n = 1
        for d in sh:
            n *= d
        out.append(flat[off:off + n].reshape(sh))
        off += n
    return out


def kernel(x, meta_tokens, mix_norm_w, w_in, b_gate, pool_w_group, pool_scale, w_pool_up, conv_w, conv_b, dt_bias, a_log, d_skip, ssd_norm_w, w_ssd_out, w_o, mlp_norm_w, w_ff1, w_ff2, final_norm_w, loss_target, m_meta_tokens, m_mix_norm_w, m_w_in, m_b_gate, m_pool_w_group, m_pool_scale, m_w_pool_up, m_conv_w, m_conv_b, m_dt_bias, m_a_log, m_d_skip, m_ssd_norm_w, m_w_ssd_out, m_w_o, m_mlp_norm_w, m_w_ff1, m_w_ff2, m_final_norm_w, v_meta_tokens, v_mix_norm_w, v_w_in, v_b_gate, v_pool_w_group, v_pool_scale, v_w_pool_up, v_conv_w, v_conv_b, v_dt_bias, v_a_log, v_d_skip, v_ssd_norm_w, v_w_ssd_out, v_w_o, v_mlp_norm_w, v_w_ff1, v_w_ff2, v_final_norm_w):
    names = ("meta_tokens",) + PER_LAYER + ("final_norm_w",)
    par = dict(meta_tokens=meta_tokens, mix_norm_w=mix_norm_w, w_in=w_in, b_gate=b_gate, pool_w_group=pool_w_group,
               pool_scale=pool_scale, w_pool_up=w_pool_up, conv_w=conv_w, conv_b=conv_b, dt_bias=dt_bias, a_log=a_log,
               d_skip=d_skip, ssd_norm_w=ssd_norm_w, w_ssd_out=w_ssd_out, w_o=w_o, mlp_norm_w=mlp_norm_w, w_ff1=w_ff1,
               w_ff2=w_ff2, final_norm_w=final_norm_w)
    mom = dict(meta_tokens=m_meta_tokens, mix_norm_w=m_mix_norm_w, w_in=m_w_in, b_gate=m_b_gate, pool_w_group=m_pool_w_group,
               pool_scale=m_pool_scale, w_pool_up=m_w_pool_up, conv_w=m_conv_w, conv_b=m_conv_b, dt_bias=m_dt_bias,
               a_log=m_a_log, d_skip=m_d_skip, ssd_norm_w=m_ssd_norm_w, w_ssd_out=m_w_ssd_out, w_o=m_w_o,
               mlp_norm_w=m_mlp_norm_w, w_ff1=m_w_ff1, w_ff2=m_w_ff2, final_norm_w=m_final_norm_w)
    var = dict(meta_tokens=v_meta_tokens, mix_norm_w=v_mix_norm_w, w_in=v_w_in, b_gate=v_b_gate, pool_w_group=v_pool_w_group,
               pool_scale=v_pool_scale, w_pool_up=v_w_pool_up, conv_w=v_conv_w, conv_b=v_conv_b, dt_bias=v_dt_bias,
               a_log=v_a_log, d_skip=v_d_skip, ssd_norm_w=v_ssd_norm_w, w_ssd_out=v_w_ssd_out, w_o=v_w_o,
               mlp_norm_w=v_mlp_norm_w, w_ff1=v_w_ff1, w_ff2=v_w_ff2, final_norm_w=v_final_norm_w)
    chip = 2 * lax.axis_index("x") + lax.axis_index("y")
    core = lax.axis_index("c")

    weights = _WeightGatherer({n: par[n].astype(BF16).reshape(DEPTH, -1, par[n].shape[-1]) for n in BIG})
    small = {n: par[n] for n in REPLICATED}
    small_shapes = [par[n].shape for n in SMALL_SHARDED]
    got_small = _exchange_small(_flatten([par[n] for n in SMALL_SHARDED], F32, 8), False, "gather_small")
    pieces = [_unflatten(got_small[2 * k], small_shapes) for k in range(N_CHIPS)]
    for j, n in enumerate(SMALL_SHARDED):
        small[n] = jnp.concatenate([pieces[k][j] for k in range(N_CHIPS)], axis=SHARD_AXIS[n])

    reducer = _GradReducer(chip, core)
    loss, dx, _, grads = _local_step(x[0], loss_target[0], weights, small, reducer)

    small_names = REPLICATED + SMALL_SHARDED
    red = _exchange_small(_flatten([grads[n] for n in small_names] + [loss.reshape(1)], F32, 8), True, "allreduce_small")
    *parts, loss = _unflatten(red, [grads[n].shape for n in small_names] + [(1,)])
    loss = loss[0]
    gsum = dict(zip(small_names, parts))
    for n in SMALL_SHARDED:
        width = par[n].shape[SHARD_AXIS[n]]
        gsum[n] = lax.dynamic_slice_in_dim(gsum[n], chip * width, width, axis=SHARD_AXIS[n])

    gsum.update({n: g.reshape(par[n].shape) for n, g in reducer.finish().items()})

    delta, new_m, new_v = {}, {}, {}
    for n in BIG:
        if n == "w_in":
            tr = lambda a: jnp.swapaxes(a, 1, 2)
            outs = _adamw(tr(par[n]), tr(gsum[n]), tr(mom[n]), tr(var[n]), "adamw_" + n)
            delta[n], new_m[n], new_v[n] = (tr(o) for o in outs)
        else:
            delta[n], new_m[n], new_v[n] = _adamw(par[n], gsum[n], mom[n], var[n], "adamw_" + n)
    rest = [n for n in names if n not in BIG]
    shapes = [par[n].shape for n in rest]
    packed = [_flatten([d[n] for n in rest], F32, 8) for d in (par, gsum, mom, var)]
    for d, flat in zip((delta, new_m, new_v), _adamw(*packed, "adamw_small")):
        d.update(zip(rest, _unflatten(flat, shapes)))

    order = ("meta_tokens", "mix_norm_w", "w_in", "b_gate", "pool_w_group", "pool_scale", "w_pool_up", "conv_w", "conv_b",
             "dt_bias", "a_log", "d_skip", "ssd_norm_w", "w_ssd_out", "w_o", "mlp_norm_w", "w_ff1", "w_ff2", "final_norm_w")
    return (loss, dx[None], *[gsum[n] for n in order], *[delta[n] for n in order], *[new_m[n] for n in order],
            *[new_v[n] for n in order])
```

```python
import functools

import jax
import jax.numpy as jnp
from jax import lax
from jax.experimental import pallas as pl
from jax.experimental.pallas import tpu as pltpu

F32 = jnp.float32
BF16 = jnp.bfloat16

D_MODEL = 1024
DEPTH = 4
N_META = 16
N_PAD = 112
ROW_X = N_PAD + N_META
POOL_WINDOWS = (2, 4, 8, 16)
POOL_GDIM = 256
D_INNER = 2048
N_HEADS = 32
HEAD_DIM = 64
N_GROUPS = 8
HEADS_PER_GROUP = 4
GROUP_W = HEADS_PER_GROUP * HEAD_DIM
D_STATE = 128
CHUNK = 128
D_XBC = 4096
D_FF = 4096
EPS = 1e-5
OFF_Z, OFF_XBC, OFF_DT, OFF_GATE, IN_COLS = 1024, 3072, 7168, 7200, 9248
PZ, PGATE, PPOOL, PXBC, PCOLS = 0, 2048, 4096, 5120, 9216
DT_PAD = 128

ADAM_LR, ADAM_B1, ADAM_B2, ADAM_EPS, ADAM_WD, ADAM_STEP = 0.001, 0.9, 0.999, 1e-08, 0.01, 10

VMEM_LIMIT = 56 * 1024 * 1024

_NN = (((1,), (0,)), ((), ()))
_NT = (((1,), (1,)), ((), ()))
_TN = (((0,), (0,)), ((), ()))


def _dot(a, b, dn=_NN):
    return lax.dot_general(a, b, dn, preferred_element_type=F32)


def _cparams(sem):
    return pltpu.CompilerParams(dimension_semantics=sem, vmem_limit_bytes=VMEM_LIMIT)


def _tile(n, cands):
    for c in cands:
        if n % c == 0:
            return c
    raise ValueError(f"no tile for {n} in {cands}")


def _sigmoid(x):
    return 0.5 * jnp.tanh(0.5 * x) + 0.5


def _softplus(x):
    return jnp.maximum(x, 0.0) + jnp.log(1.0 + jnp.exp(-jnp.abs(x)))


def _bs(shape, fn):
    return pl.BlockSpec(shape, fn)


class _Carry:
    def __init__(self, ins, outs, n_sems, make, alias=None):
        self.ins, self.outs, self.n_sems, self.make = list(ins), list(outs), n_sems, make
        self.alias = dict(alias or {})

    def aliases(self, first_in, first_out):
        return {first_in + i: first_out + o for i, o in self.alias.items()}

    def specs(self):
        hbm = pl.BlockSpec(memory_space=pl.ANY)
        return [hbm] * len(self.ins), [hbm] * len(self.outs), [pltpu.SemaphoreType.DMA((self.n_sems,))] * 2

    def at_edges(self, grid, in_refs, out_refs, sems):
        ids = [pl.program_id(a) for a in range(len(grid))]
        first = functools.reduce(jnp.logical_and, [i == 0 for i in ids])
        last = functools.reduce(jnp.logical_and, [i == n - 1 for i, n in zip(ids, grid)])

        def begin():
            @pl.when(first)
            def _():
                for cp in self.make(in_refs, out_refs, *sems):
                    cp.start()

        def end():
            @pl.when(last)
            def _():
                for cp in self.make(in_refs, out_refs, *sems):
                    cp.wait()

        return begin, end


def _sum_into(o_ref, v, first):
    @pl.when(first)
    def _():
        o_ref[...] = v

    @pl.when(jnp.logical_not(first))
    def _():
        o_ref[...] += v


def _mm(name, dn, grid, a, a_spec, b, b_spec, outs, *, extras=(), epilogue=None, carry=None, summed=()):
    nk = grid[2]
    ne, no = len(extras), len(outs)
    blk = tuple(d for d in outs[0][2].block_shape if d is not None)
    c_in, c_out, c_sem = carry.specs() if carry is not None else ([], [], [])
    nci, nco = len(c_in), len(c_out)

    def body(a_ref, b_ref, *rest):
        e_refs, o_refs = rest[:ne], rest[ne + nci:ne + nci + no]
        first_row_tile = pl.program_id(0) == 0
        if carry is not None:
            begin, end = carry.at_edges(grid, rest[ne:ne + nci], rest[ne + nci + no:ne + nci + no + nco], rest[-2:])
            begin()
        if len(b_ref.shape) == 3:
            rows = b_ref.shape[1]
            p = _dot(a_ref[:, :rows].astype(BF16), b_ref[0].astype(BF16), dn)
            for s in range(1, b_ref.shape[0]):
                p = p + _dot(a_ref[:, s * rows:(s + 1) * rows].astype(BF16), b_ref[s].astype(BF16), dn)
        else:
            p = _dot(a_ref[...].astype(BF16), b_ref[...].astype(BF16), dn)

        def finish(acc):
            outs = epilogue(acc, *[e[...] for e in e_refs]) if epilogue is not None else (acc,)
            for idx, (o, v) in enumerate(zip(o_refs, outs)):
                if idx in summed:
                    _sum_into(o, v, first_row_tile)
                else:
                    o[...] = v.astype(o.dtype)

        if nk == 1:
            finish(p)
        else:
            acc_ref = rest[ne + nci + no + nco]
            kk = pl.program_id(2)

            @pl.when(kk == 0)
            def _():
                acc_ref[...] = p

            @pl.when(kk > 0)
            def _():
                acc_ref[...] += p

            @pl.when(kk == nk - 1)
            def _():
                finish(acc_ref[...])
        if carry is not None:
            end()

    res = pl.pallas_call(
        body,
        name=name,
        grid=grid,
        in_specs=[a_spec, b_spec, *[s for _, s in extras]] + c_in,
        out_specs=[o[2] for o in outs] + c_out,
        out_shape=[jax.ShapeDtypeStruct(o[0], o[1]) for o in outs] + (carry.outs if carry is not None else []),
        scratch_shapes=([pltpu.VMEM(blk, F32)] if nk > 1 else []) + c_sem,
        input_output_aliases=carry.aliases(2 + ne, no) if carry is not None else {},
        compiler_params=_cparams(("arbitrary",) * 3 if carry is not None or summed else ("parallel", "parallel", "arbitrary")),
    )(a, b, *[e for e, _ in extras], *(carry.ins if carry is not None else []))
    if carry is not None:
        return res[:no], res[no:]
    return res[0] if no == 1 else res


def _norm_fwd(h, w, name):
    t, d = h.shape
    tm = _tile(t, (1056, 384, 128))

    def body(h_ref, w_ref, u_ref):
        x = h_ref[...]
        r = lax.rsqrt(jnp.mean(x * x, axis=-1, keepdims=True) + EPS)
        u_ref[...] = (x * r * w_ref[...]).astype(BF16)

    return pl.pallas_call(
        body, name=name, grid=(t // tm,),
        in_specs=[pl.BlockSpec((tm, d), lambda i: (i, 0)), pl.BlockSpec((1, d), lambda i: (0, 0))],
        out_specs=pl.BlockSpec((tm, d), lambda i: (i, 0)),
        out_shape=jax.ShapeDtypeStruct((t, d), BF16),
        compiler_params=_cparams(("parallel",)),
    )(h, w.reshape(1, d))


def _loss_head(h, w, tgt):
    t, d = h.shape
    tm = CHUNK
    nb = ROW_X // tm

    def body(h_ref, w_ref, t_ref, loss_ref, dh_ref, dw_ref):
        i = pl.program_id(0)
        x = h_ref[...]
        r = lax.rsqrt(jnp.mean(x * x, axis=-1, keepdims=True) + EPS)
        xhat = x * r
        wv = w_ref[...]
        live = i >= nb
        err = jnp.where(live, xhat * wv - t_ref[...], 0.0)
        dout = err * (1.0 / d)
        g = dout * wv
        dh_ref[...] = r * (g - xhat * jnp.mean(g * xhat, axis=-1, keepdims=True))

        @pl.when(i == 0)
        def _():
            loss_ref[...] = jnp.zeros_like(loss_ref)
            dw_ref[...] = jnp.zeros_like(dw_ref)

        loss_ref[...] += 0.5 * jnp.sum(jnp.sum(err * err, axis=-1, keepdims=True), axis=0, keepdims=True) * (1.0 / d)
        dw_ref[...] += jnp.sum(dout * xhat, axis=0, keepdims=True)

    row = pl.BlockSpec((tm, d), lambda i: (i, 0))
    vec = pl.BlockSpec((1, d), lambda i: (0, 0))
    loss, dh, dw = pl.pallas_call(
        body, name="loss_head", grid=(t // tm,),
        in_specs=[row, vec, pl.BlockSpec((tm, d), lambda i: (jnp.maximum(i - nb, 0), 0))],
        out_specs=[pl.BlockSpec((1, 1), lambda i: (0, 0)), row, vec],
        out_shape=[jax.ShapeDtypeStruct((1, 1), F32), jax.ShapeDtypeStruct((t, d), F32), jax.ShapeDtypeStruct((1, d), F32)],
        compiler_params=_cparams(("arbitrary",)),
    )(h, w.reshape(1, d), tgt)
    return loss[0, 0], dh, dw.reshape(d)


def _gate_fwd(proj, b_gate, y_pool, y_ssd):
    t = proj.shape[0]
    d = D_MODEL
    tm = _tile(t, (528, 384, 128))

    def body(gp_ref, gs_ref, bp_ref, bs_ref, yp_ref, ys_ref, o_ref):
        gp = _sigmoid(gp_ref[...] + bp_ref[...])
        gs = _sigmoid(gs_ref[...] + bs_ref[...])
        o_ref[...] = (gp * yp_ref[...] + gs * ys_ref[...]).astype(BF16)

    row = pl.BlockSpec((tm, d), lambda i: (i, 0))
    return pl.pallas_call(
        body, name="gate_fwd", grid=(t // tm,),
        in_specs=[pl.BlockSpec((tm, d), lambda i: (i, PGATE // d)), pl.BlockSpec((tm, d), lambda i: (i, PGATE // d + 1)),
                  pl.BlockSpec((1, d), lambda i: (0, 0)), pl.BlockSpec((1, d), lambda i: (0, 1)), row, row],
        out_specs=row, out_shape=jax.ShapeDtypeStruct((t, d), BF16),
        compiler_params=_cparams(("parallel",)),
    )(proj, proj, b_gate.reshape(1, 2 * d), b_gate.reshape(1, 2 * d), y_pool, y_ssd)


def _gate_bwd(dmerged, proj, b_gate, y_pool, y_ssd):
    t = proj.shape[0]
    d = D_MODEL
    tm = _tile(t, (384, 128))

    def body(dm_ref, gp_ref, gs_ref, bp_ref, bs_ref, yp_ref, ys_ref, dyp_ref, dys_ref, dg_ref, db_ref):
        dm = dm_ref[...]
        gp = _sigmoid(gp_ref[...] + bp_ref[...])
        gs = _sigmoid(gs_ref[...] + bs_ref[...])
        dyp_ref[...] = (dm * gp).astype(BF16)
        dys_ref[...] = (dm * gs).astype(BF16)
        dgp = dm * yp_ref[...] * gp * (1.0 - gp)
        dgs = dm * ys_ref[...] * gs * (1.0 - gs)
        dg_ref[:, :d] = dgp.astype(BF16)
        dg_ref[:, d:] = dgs.astype(BF16)

        @pl.when(pl.program_id(0) == 0)
        def _():
            db_ref[...] = jnp.zeros_like(db_ref)

        db_ref[:, :d] += jnp.sum(dgp, axis=0, keepdims=True)
        db_ref[:, d:] += jnp.sum(dgs, axis=0, keepdims=True)

    row = pl.BlockSpec((tm, d), lambda i: (i, 0))
    dyp, dys, dg, db = pl.pallas_call(
        body, name="gate_bwd", grid=(t // tm,),
        in_specs=[row, pl.BlockSpec((tm, d), lambda i: (i, PGATE // d)), pl.BlockSpec((tm, d), lambda i: (i, PGATE // d + 1)),
                  pl.BlockSpec((1, d), lambda i: (0, 0)), pl.BlockSpec((1, d), lambda i: (0, 1)), row, row],
        out_specs=[row, row, pl.BlockSpec((tm, 2 * d), lambda i: (i, PGATE // (2 * d))),
                   pl.BlockSpec((1, 2 * d), lambda i: (0, 0))],
        out_shape=[jax.ShapeDtypeStruct((t, d), BF16), jax.ShapeDtypeStruct((t, d), BF16),
                   jax.ShapeDtypeStruct((t, PCOLS), BF16), jax.ShapeDtypeStruct((1, 2 * d), F32)],
        compiler_params=_cparams(("arbitrary",)),
    )(dmerged, proj, proj, b_gate.reshape(1, 2 * d), b_gate.reshape(1, 2 * d), y_pool, y_ssd)
    return dyp, dys, dg, db.reshape(2 * d)


POOL_HALO = 16


def _pool_counts(row0, n, win):
    pos1 = row0 + lax.broadcasted_iota(jnp.int32, (n, 1), 0) - (N_PAD - 1)
    return jnp.clip(pos1, 1, win).astype(F32)


N_CHIPS = 4
WG_ROWS = POOL_GDIM // N_CHIPS
WUP_ROWS = D_MODEL // N_CHIPS


def _group_w(wg_ref, g):
    return jnp.concatenate([wg_ref[k, g] for k in range(N_CHIPS)], axis=0)


def _pool_w_specs(layer):
    return [pl.BlockSpec((None, N_CHIPS, len(POOL_WINDOWS), WG_ROWS, POOL_GDIM), lambda i: (0, 0, 0, 0, 0)),
            pl.BlockSpec((None, N_CHIPS, WUP_ROWS, D_MODEL), lambda i: (0, 0, 0, 0))]


def _pool_fwd(proj, wg, scale, wup, layer):
    t = proj.shape[0]
    d = D_MODEL
    tm = _tile(t, (384, 128))
    hb = tm // POOL_HALO

    def body(u_ref, halo_ref, sc_ref, wg_ref, wup_ref, pooled_ref, yg_ref, ypm_ref, yp_ref):
        i = pl.program_id(0)
        x = u_ref[...]
        halo = jnp.where(i > 0, halo_ref[...], 0.0)
        xc = jnp.concatenate([halo, x], axis=0)
        for g, win in enumerate(POOL_WINDOWS):
            sl = slice(g * POOL_GDIM, (g + 1) * POOL_GDIM)
            s = xc[:, sl]
            k = 1
            while k < win:
                s = s + pltpu.roll(s, k, axis=0)
                k *= 2
            pooled = s[POOL_HALO:] / _pool_counts(i * tm, tm, win) - x[:, sl]
            pb = pooled.astype(BF16)
            pooled_ref[:, sl] = pb
            yg_ref[:, sl] = _dot(pb, _group_w(wg_ref, g))
        ypm = (yg_ref[...] * sc_ref[...]).astype(BF16)
        ypm_ref[...] = ypm
        acc = _dot(ypm[:, :WUP_ROWS], wup_ref[0])
        for k in range(1, N_CHIPS):
            acc = acc + _dot(ypm[:, k * WUP_ROWS:(k + 1) * WUP_ROWS], wup_ref[k])
        yp_ref[...] = acc

    row = pl.BlockSpec((tm, d), lambda i: (i, 0))
    return pl.pallas_call(
        body, name=f"pool_fwd_{layer}", grid=(t // tm,),
        in_specs=[pl.BlockSpec((tm, d), lambda i: (i, PPOOL // d)),
                  pl.BlockSpec((POOL_HALO, d), lambda i: (jnp.maximum(i * hb - 1, 0), PPOOL // d)),
                  pl.BlockSpec((1, d), lambda i: (0, 0))] + _pool_w_specs(layer),
        out_specs=[row, row, row, row],
        out_shape=[jax.ShapeDtypeStruct((t, d), BF16), jax.ShapeDtypeStruct((t, d), F32),
                   jax.ShapeDtypeStruct((t, d), BF16), jax.ShapeDtypeStruct((t, d), F32)],
        compiler_params=_cparams(("parallel",)),
    )(proj, proj, scale.reshape(1, d), wg, wup)


def _pool_bwd_a(dy_pool, yg, scale, wg, wup, layer):
    t, d = yg.shape
    tm = _tile(t, (384, 128))

    def body(dy_ref, yg_ref, sc_ref, wg_ref, wup_ref, q_ref, dyg_ref, dsc_ref):
        dy = dy_ref[...]
        dypm = jnp.concatenate([_dot(dy, wup_ref[k], _NT) for k in range(N_CHIPS)], axis=1)

        @pl.when(pl.program_id(0) == 0)
        def _():
            dsc_ref[...] = jnp.zeros_like(dsc_ref)

        dsc_ref[...] += jnp.sum(dypm * yg_ref[...], axis=0, keepdims=True)
        dyg = (dypm * sc_ref[...]).astype(BF16)
        dyg_ref[...] = dyg
        for g in range(len(POOL_WINDOWS)):
            sl = slice(g * POOL_GDIM, (g + 1) * POOL_GDIM)
            q_ref[:, sl] = _dot(dyg[:, sl], _group_w(wg_ref, g), _NT)

    row = pl.BlockSpec((tm, d), lambda i: (i, 0))
    vec = pl.BlockSpec((1, d), lambda i: (0, 0))
    q, dyg, dsc = pl.pallas_call(
        body, name=f"pool_bwd_a_{layer}", grid=(t // tm,),
        in_specs=[row, row, vec] + _pool_w_specs(layer),
        out_specs=[row, row, vec],
        out_shape=[jax.ShapeDtypeStruct((t, d), F32), jax.ShapeDtypeStruct((t, d), BF16), jax.ShapeDtypeStruct((1, d), F32)],
        compiler_params=_cparams(("arbitrary",)),
    )(dy_pool, yg, scale.reshape(1, d), wg, wup)
    return q, dyg, dsc.reshape(d)


def _pool_bwd_b(q, dproj):
    t, d = q.shape
    tm = _tile(t, (384, 128))
    hb = tm // POOL_HALO
    nt = t // tm
    n = tm + POOL_HALO

    def body(q_ref, halo_ref, _, o_ref):
        i = pl.program_id(0)
        qv = q_ref[...]
        halo = jnp.where(i < nt - 1, halo_ref[...], 0.0)
        qc = jnp.concatenate([qv, halo], axis=0)
        for g, win in enumerate(POOL_WINDOWS):
            sl = slice(g * POOL_GDIM, (g + 1) * POOL_GDIM)
            s = qc[:, sl] / _pool_counts(i * tm, n, win)
            k = 1
            while k < win:
                s = s + pltpu.roll(s, n - k, axis=0)
                k *= 2
            o_ref[:, sl] = (s[:tm] - qv[:, sl]).astype(BF16)

    row = pl.BlockSpec((tm, d), lambda i: (i, 0))
    return pl.pallas_call(
        body, name="pool_bwd_b", grid=(nt,),
        in_specs=[row, pl.BlockSpec((POOL_HALO, d), lambda i: (jnp.minimum((i + 1) * hb, t // POOL_HALO - 1), 0)),
                  pl.BlockSpec(memory_space=pl.ANY)],
        out_specs=pl.BlockSpec((tm, d), lambda i: (i, PPOOL // d)), out_shape=jax.ShapeDtypeStruct(dproj.shape, BF16),
        input_output_aliases={2: 0},
        compiler_params=_cparams(("parallel",)),
    )(q, q, dproj)


def _pool_dwg(pooled, dyg, layer):
    t, d = pooled.shape
    tk = _tile(t, (1056, 384, 128))
    nk = t // tk
    gd = POOL_GDIM
    ng = d // gd

    def body(p_ref, g_ref, o_ref):
        @pl.when(pl.program_id(1) == 0)
        def _():
            o_ref[...] = jnp.zeros_like(o_ref)

        part = _dot(p_ref[...], g_ref[...], _TN)
        for k in range(N_CHIPS):
            o_ref[k] += part[k * WG_ROWS:(k + 1) * WG_ROWS]

    blk = pl.BlockSpec((tk, gd), lambda g, k: (k, g))
    return pl.pallas_call(
        body, name=f"pool_dwg_{layer}", grid=(ng, nk), in_specs=[blk, blk],
        out_specs=pl.BlockSpec((N_CHIPS, None, WG_ROWS, gd), lambda g, k: (0, g, 0, 0)),
        out_shape=jax.ShapeDtypeStruct((N_CHIPS, ng, WG_ROWS, gd), F32),
        compiler_params=_cparams(("parallel", "arbitrary")),
    )(pooled, dyg)


CONV_W = 4
CONV_HALO = 8
XBC_BLK = PXBC // 1024


def _conv_fwd(proj, conv_w, conv_b):
    t = proj.shape[0]
    cw = 1024
    tm = _tile(t, (1056, 384, 128))
    hb = tm // CONV_HALO

    def body(x_ref, halo_ref, w_ref, b_ref, o_ref, pre_ref):
        i = pl.program_id(1)
        x = x_ref[...]
        halo = jnp.where(i > 0, halo_ref[...], 0.0)
        xc = jnp.concatenate([halo, x], axis=0)
        w = w_ref[...]
        acc = b_ref[...] + x * w[CONV_W - 1:CONV_W, :]
        for k in range(CONV_W - 1):
            acc = acc + pltpu.roll(xc, CONV_W - 1 - k, axis=0)[CONV_HALO:] * w[k:k + 1, :]
        row = i * tm + lax.broadcasted_iota(jnp.int32, (tm, 1), 0)
        pre_ref[...] = acc
        o_ref[...] = jnp.where(row >= N_PAD, acc * _sigmoid(acc), 0.0)

    blk = pl.BlockSpec((tm, cw), lambda j, i: (i, j))
    return pl.pallas_call(
        body, name="conv_fwd", grid=(D_XBC // cw, t // tm),
        in_specs=[pl.BlockSpec((tm, cw), lambda j, i: (i, XBC_BLK + j)),
                  pl.BlockSpec((CONV_HALO, cw), lambda j, i: (jnp.maximum(i * hb - 1, 0), XBC_BLK + j)),
                  pl.BlockSpec((CONV_W, cw), lambda j, i: (0, j)), pl.BlockSpec((1, cw), lambda j, i: (0, j))],
        out_specs=[blk, blk],
        out_shape=[jax.ShapeDtypeStruct((t, D_XBC), F32), jax.ShapeDtypeStruct((t, D_XBC), F32)],
        compiler_params=_cparams(("parallel", "parallel")),
    )(proj, proj, conv_w, conv_b.reshape(1, D_XBC))


def _conv_bwd(dxa, coff, proj, pre, conv_w, dproj):
    t, ncols = dxa.shape
    cw = 1024
    tm = _tile(t, (528, 384, 128))
    hb = tm // CONV_HALO
    nt = t // tm
    n = tm + CONV_HALO

    def body(d_ref, dn_ref, p_ref, pn_ref, x_ref, w_ref, _, o_ref, dw_ref, db_ref):
        i = pl.program_id(1)
        last = i == nt - 1
        xc = jnp.concatenate([p_ref[...], pn_ref[...]], axis=0)
        df = jnp.concatenate([d_ref[...], dn_ref[...]], axis=0)
        w = w_ref[...]
        sig = _sigmoid(xc)
        row = i * tm + lax.broadcasted_iota(jnp.int32, (n, 1), 0)
        live = (row >= N_PAD) & ((row < (i + 1) * tm) | jnp.logical_not(last))
        dxc = jnp.where(live, df * (sig * (1.0 + xc * (1.0 - sig))), 0.0)
        ahead = [pltpu.roll(dxc, n - (CONV_W - 1 - k), axis=0)[:tm] for k in range(CONV_W - 1)] + [dxc[:tm]]
        acc = ahead[0] * w[0:1, :]
        for k in range(1, CONV_W):
            acc = acc + ahead[k] * w[k:k + 1, :]
        o_ref[...] = acc.astype(BF16)

        @pl.when(i == 0)
        def _():
            dw_ref[...] = jnp.zeros_like(dw_ref)
            db_ref[...] = jnp.zeros_like(db_ref)

        x = x_ref[...]
        db_ref[...] += jnp.sum(ahead[CONV_W - 1], axis=0, keepdims=True)
        for k in range(CONV_W):
            dw_ref[k:k + 1, :] += jnp.sum(ahead[k] * x, axis=0, keepdims=True)

    def pspec(rows, fn):
        return pl.BlockSpec((rows, cw), lambda j, i: (fn(i), coff + j))

    nxt = lambda i: jnp.minimum((i + 1) * hb, t // CONV_HALO - 1)
    dxbc, dw, db = pl.pallas_call(
        body, name=f"conv_bwd_{coff}", grid=(ncols // cw, nt),
        in_specs=[pl.BlockSpec((tm, cw), lambda j, i: (i, j)), pl.BlockSpec((CONV_HALO, cw), lambda j, i: (nxt(i), j)),
                  pspec(tm, lambda i: i), pspec(CONV_HALO, nxt),
                  pl.BlockSpec((tm, cw), lambda j, i: (i, XBC_BLK + coff + j)),
                  pl.BlockSpec((CONV_W, cw), lambda j, i: (0, coff + j)),
                  pl.BlockSpec(memory_space=pl.ANY)],
        out_specs=[pl.BlockSpec((tm, cw), lambda j, i: (i, XBC_BLK + coff + j)), pl.BlockSpec((CONV_W, cw), lambda j, i: (0, j)),
                   pl.BlockSpec((1, cw), lambda j, i: (0, j))],
        out_shape=[jax.ShapeDtypeStruct(dproj.shape, BF16), jax.ShapeDtypeStruct((CONV_W, ncols), F32),
                   jax.ShapeDtypeStruct((1, ncols), F32)],
        input_output_aliases={6: 0},
        compiler_params=_cparams(("parallel", "arbitrary")),
    )(dxa, dxa, pre, pre, proj, conv_w, dproj)
    return dxbc, dw, db.reshape(ncols)


def _cumsum(x, axis, reverse=False):
    n = x.shape[axis]
    idx = lax.broadcasted_iota(jnp.int32, x.shape, axis)
    k = 1
    while k < n:
        if reverse:
            x = x + jnp.where(idx < n - k, pltpu.roll(x, n - k, axis=axis), 0.0)
        else:
            x = x + jnp.where(idx >= k, pltpu.roll(x, k, axis=axis), 0.0)
        k *= 2
    return x


def _head_masks():
    lane = lax.broadcasted_iota(jnp.int32, (1, GROUP_W), 1)
    return [(lane >= r * HEAD_DIM) & (lane < (r + 1) * HEAD_DIM) for r in range(HEADS_PER_GROUP)]


def _expand_heads(cols, hm):
    out = jnp.where(hm[0], cols[:, 0:1], 0.0)
    for r in range(1, HEADS_PER_GROUP):
        out = out + jnp.where(hm[r], cols[:, r:r + 1], 0.0)
    return out


def _ssd_decay(dt_raw, dt_bias, a_log):
    t = dt_raw.shape[0]

    hpg = HEADS_PER_GROUP

    def body(raw_ref, b_ref, al_ref, dt_ref, acs_ref, sig_ref, rows_ref):
        raw = raw_ref[...] + b_ref[...]
        rowid = pl.program_id(0) * CHUNK + lax.broadcasted_iota(jnp.int32, (CHUNK, 1), 0)
        dt = jnp.where(rowid >= N_PAD, _softplus(raw), 0.0)
        acs = _cumsum(dt * -jnp.exp(al_ref[...]), 0)
        sig = _sigmoid(raw)
        acs_t = acs.T
        for g in range(N_GROUPS):
            heads = slice(g * hpg, (g + 1) * hpg)
            dt_ref[g] = dt[:, heads]
            acs_ref[g] = acs[:, heads]
            sig_ref[g] = sig[:, heads]
            rows_ref[g] = acs_t[heads, :]

    blk = pl.BlockSpec((CHUNK, DT_PAD), lambda c: (c, 0))
    vec = pl.BlockSpec((1, DT_PAD), lambda c: (0, 0))
    cols = pl.BlockSpec((N_GROUPS, CHUNK, hpg), lambda c: (0, c, 0))
    pad = lambda v: jnp.pad(v, (0, DT_PAD - N_HEADS)).reshape(1, DT_PAD)
    dt, acs, sig, rows = pl.pallas_call(
        body, name="ssd_decay", grid=(t // CHUNK,), in_specs=[blk, vec, vec],
        out_specs=[cols, cols, cols, pl.BlockSpec((N_GROUPS, hpg, CHUNK), lambda c: (0, 0, c))],
        out_shape=[jax.ShapeDtypeStruct((N_GROUPS, t, hpg), F32)] * 3 + [jax.ShapeDtypeStruct((N_GROUPS, hpg, t), F32)],
        compiler_params=_cparams(("parallel",)),
    )(dt_raw, pad(dt_bias), pad(a_log))
    return dict(dt=dt, acs=acs, acs_rows=rows, sig=sig, a_log=a_log.reshape(N_GROUPS, 1, hpg))


def _ssd_q(dtc, acs4, acs_r):
    hm = _head_masks()
    dt_exp = _expand_heads(dtc, hm)
    acs = _expand_heads(acs4, hm)
    atot = acs[CHUNK - 1:CHUNK, :]
    return dict(dtc=dtc, hm=hm, dt_exp=dt_exp, acs=acs, acs_r=acs_r,
                ea=jnp.exp(acs), ds=jnp.exp(atot - acs), dec=jnp.exp(atot))


def _stack4(x):
    return jnp.concatenate([x] * HEADS_PER_GROUP, axis=0)


def _ssd_decay_stacks(q):
    hpg = HEADS_PER_GROUP
    a_col = jnp.concatenate([q["acs"][:, r * HEAD_DIM:r * HEAD_DIM + 1] for r in range(hpg)], axis=0)
    a_row = jnp.concatenate([jnp.broadcast_to(q["acs_r"][r:r + 1, :], (CHUNK, CHUNK)) for r in range(hpg)], axis=0)
    ri = lax.broadcasted_iota(jnp.int32, (hpg * CHUNK, CHUNK), 0) % CHUNK
    ci = lax.broadcasted_iota(jnp.int32, (hpg * CHUNK, CHUNK), 1)
    diff = a_col - a_row
    lm = jnp.exp(jnp.where(ri >= ci, diff, -jnp.inf))
    lt = jnp.exp(jnp.where(ri <= ci, -diff, -jnp.inf))
    return lm, lt


def _pick_heads(stacked, hm):
    out = jnp.where(hm[0], stacked[:CHUNK], 0.0)
    for r in range(1, HEADS_PER_GROUP):
        out = out + jnp.where(hm[r], stacked[r * CHUNK:(r + 1) * CHUNK], 0.0)
    return out


def _mask_heads(x, hm):
    return jnp.concatenate([jnp.where(hm[r], x, 0.0) for r in range(HEADS_PER_GROUP)], axis=0)


SSD_GP_FWD, SSD_GP_BWD = 8, 8


def _ssd_specs(cidx, gp):
    hpg = HEADS_PER_GROUP
    return [
        pl.BlockSpec((CHUNK, gp * GROUP_W), lambda g, c: (cidx(c), g)),
        pl.BlockSpec((CHUNK, gp * D_STATE), lambda g, c: (cidx(c), D_INNER // (gp * D_STATE) + g)),
        pl.BlockSpec((CHUNK, gp * D_STATE), lambda g, c: (cidx(c), (D_INNER + 1024) // (gp * D_STATE) + g)),
        pl.BlockSpec((gp, CHUNK, hpg), lambda g, c: (g, cidx(c), 0)),
        pl.BlockSpec((gp, CHUNK, hpg), lambda g, c: (g, cidx(c), 0)),
        pl.BlockSpec((gp, hpg, CHUNK), lambda g, c: (g, 0, cidx(c))),
    ]


def _ssd_fwd(xa, dec, carry=None):
    t = xa.shape[0]
    nc = t // CHUNK
    gp, gw, ds = SSD_GP_FWD, GROUP_W, D_STATE
    grid = (N_GROUPS // gp, nc)
    c_in, c_out, c_sem = carry.specs() if carry is not None else ([], [], [])
    nci, nco = len(c_in), len(c_out)

    def body(*refs):
        xs_ref, b_ref, c_ref, dt_ref, acs_ref, acsr_ref = refs[:6]
        y_ref, prev_ref = refs[6 + nci:8 + nci]
        st_ref = refs[8 + nci + nco]
        if carry is not None:
            begin, end = carry.at_edges(grid, refs[6:6 + nci], refs[8 + nci:8 + nci + nco], refs[-2:])
            begin()

        @pl.when(pl.program_id(1) == 0)
        def _():
            st_ref[...] = jnp.zeros_like(st_ref)

        for k in range(gp):
            q = _ssd_q(dt_ref[k], acs_ref[k], acsr_ref[k])
            xdt = xs_ref[:, k * gw:(k + 1) * gw] * q["dt_exp"]
            bm = b_ref[:, k * ds:(k + 1) * ds].astype(BF16)
            cm = c_ref[:, k * ds:(k + 1) * ds].astype(BF16)
            cb = _dot(cm, bm, _NT)
            st = st_ref[k]
            prev_ref[0, k] = st
            lm, _ = _ssd_decay_stacks(q)
            y_diag = _pick_heads(_dot((_stack4(cb) * lm).astype(BF16), xdt.astype(BF16)), q["hm"])
            y_ref[:, k * gw:(k + 1) * gw] = y_diag + _dot(cm, st.astype(BF16)) * q["ea"]
            st_ref[k] = q["dec"] * st + _dot(bm, (xdt * q["ds"]).astype(BF16), _TN)
        if carry is not None:
            end()

    res = pl.pallas_call(
        body, name="ssd_fwd", grid=grid,
        in_specs=_ssd_specs(lambda c: c, gp) + c_in,
        out_specs=[pl.BlockSpec((CHUNK, gp * gw), lambda g, c: (c, g)),
                   pl.BlockSpec((1, gp, ds, gw), lambda g, c: (c, g, 0, 0))] + c_out,
        out_shape=[jax.ShapeDtypeStruct((t, D_INNER), F32), jax.ShapeDtypeStruct((nc, N_GROUPS, ds, gw), F32)]
        + (carry.outs if carry is not None else []),
        scratch_shapes=[pltpu.VMEM((gp, ds, gw), F32)] + c_sem,
        input_output_aliases=carry.aliases(6, 2) if carry is not None else {},
        compiler_params=_cparams(("arbitrary", "arbitrary") if carry is not None else ("parallel", "arbitrary")),
    )(xa, xa, xa, dec["dt"], dec["acs"], dec["acs_rows"], *(carry.ins if carry is not None else []))
    return res[0], res[1], res[2:]


def _ssd_bwd(dy, dsk, xa, prev, dec, carry=None):
    t = xa.shape[0]
    nc = t // CHUNK
    gp, gw, dstate = SSD_GP_BWD, GROUP_W, D_STATE
    hpg = HEADS_PER_GROUP

    grid = (N_GROUPS // gp, nc)
    c_in, c_out, c_sem = carry.specs() if carry is not None else ([], [], [])
    nci, nco = len(c_in), len(c_out)

    def body(*refs):
        ins, c_ins = refs[:11], refs[11:11 + nci]
        outs = refs[11 + nci:17 + nci]
        c_outs = refs[17 + nci:17 + nci + nco]
        dst_ref = refs[17 + nci + nco]
        dbias_ref, dalog_ref = outs[4], outs[5]
        if carry is not None:
            begin, end = carry.at_edges(grid, c_ins, c_outs, refs[-2:])
            begin()

        @pl.when(pl.program_id(1) == 0)
        def _():
            dst_ref[...] = jnp.zeros_like(dst_ref)
            dbias_ref[...] = jnp.zeros_like(dbias_ref)
            dalog_ref[...] = jnp.zeros_like(dalog_ref)

        for k in range(gp):
            one_group(k, *ins, *outs, dst_ref)
        if carry is not None:
            end()

    def one_group(k, xs_ref, b_ref, c_ref, dt_ref, acs_ref, acsr_ref, sig_ref, al_ref, dy_ref, sk_ref, prev_ref,
                  dxs_ref, db_ref, dc_ref, ddt_ref, dbias_ref, dalog_ref, dst_ref):
        wide = slice(k * gw, (k + 1) * gw)
        narrow = slice(k * dstate, (k + 1) * dstate)
        q = _ssd_q(dt_ref[k], acs_ref[k], acsr_ref[k])
        a_r = -jnp.exp(al_ref[k])
        hm, ds, dec, dt_exp = q["hm"], q["ds"], q["dec"], q["dt_exp"]
        xs = xs_ref[:, wide]
        xdt = xs * dt_exp
        xdtb = xdt.astype(BF16)
        bm = b_ref[:, narrow].astype(BF16)
        cm = c_ref[:, narrow].astype(BF16)
        cb = _dot(cm, bm, _NT)
        bc = _dot(bm, cm, _NT)
        dyv = dy_ref[:, wide]
        dye = (dyv * q["ea"]).astype(BF16)
        pst = prev_ref[0, k]
        dst = dst_ref[k]
        dstb = dst.astype(BF16)
        dx_state = ds * _dot(bm, dstb)
        onehot = [(lax.broadcasted_iota(jnp.int32, (1, hpg), 1) == r).astype(F32) for r in range(hpg)]
        lm, lt = _ssd_decay_stacks(q)
        dyb = dyv.astype(BF16)
        gl = _dot(_mask_heads(dyv, hm).astype(BF16), xdtb, _NT) * lm
        glt = _dot(_mask_heads(xdt, hm).astype(BF16), dyb, _NT) * lt
        bc4 = _stack4(bc)
        dxdt = dx_state + _pick_heads(_dot((bc4 * lt).astype(BF16), dyb), hm)
        wd = jnp.sum(gl * _stack4(cb), axis=1, keepdims=True) - jnp.sum(glt * bc4, axis=1, keepdims=True)
        dcb = gl[:CHUNK]
        dcbt = glt[:CHUNK]
        qa = wd[:CHUNK] * onehot[0]
        for r in range(1, hpg):
            dcb = dcb + gl[r * CHUNK:(r + 1) * CHUNK]
            dcbt = dcbt + glt[r * CHUNK:(r + 1) * CHUNK]
            qa = qa + wd[r * CHUNK:(r + 1) * CHUNK] * onehot[r]
        pstb = pst.astype(BF16)
        dc_ref[:, narrow] = _dot(dcb.astype(BF16), bm) + _dot(dye, pstb, _NT)
        db_ref[:, narrow] = _dot(dcbt.astype(BF16), cm) + _dot((xdt * ds).astype(BF16), dstb, _NT)
        dst_ref[k] = dec * dst + _dot(cm, dye, _TN)
        dxs_ref[:, wide] = dxdt * dt_exp + dyv * sk_ref[:, wide]

        t2 = xdt * dx_state
        t1 = dyv * (_dot(cm, pstb) * q["ea"]) - t2
        t4 = dxdt * xs
        last_row = jnp.sum(t2, axis=0, keepdims=True) + dec * jnp.sum(dst * pst, axis=0, keepdims=True)
        xd = jnp.zeros((CHUNK, hpg), F32)
        dal = jnp.zeros((1, hpg), F32)
        for r in range(hpg):
            qa = qa + jnp.sum(jnp.where(hm[r], t1, 0.0), axis=1, keepdims=True) * onehot[r]
            xd = xd + jnp.sum(jnp.where(hm[r], t4, 0.0), axis=1, keepdims=True) * onehot[r]
            dal = dal + jnp.sum(jnp.where(hm[r], last_row, 0.0), axis=1, keepdims=True) * onehot[r]
        rc = _cumsum(qa, 0, reverse=True) + dal
        rowid = (nc - 1 - pl.program_id(1)) * CHUNK + lax.broadcasted_iota(jnp.int32, (CHUNK, 1), 0)
        ddt_raw = jnp.where(rowid >= N_PAD, (rc * a_r + xd) * sig_ref[k], 0.0)
        ddt_ref[k] = ddt_raw
        dbias_ref[k] += jnp.sum(ddt_raw, axis=0, keepdims=True)
        dalog_ref[k] += jnp.sum(rc * q["dtc"], axis=0, keepdims=True) * a_r

    rev = lambda c: nc - 1 - c
    blk = pl.BlockSpec((CHUNK, gp * gw), lambda g, c: (rev(c), g))
    nblk = pl.BlockSpec((CHUNK, gp * dstate), lambda g, c: (rev(c), g))
    cols = pl.BlockSpec((gp, CHUNK, hpg), lambda g, c: (g, rev(c), 0))
    small = pl.BlockSpec((gp, 1, hpg), lambda g, c: (g, 0, 0))
    res = pl.pallas_call(
        body, name="ssd_bwd", grid=grid,
        in_specs=_ssd_specs(rev, gp) + [cols, small, blk, pl.BlockSpec((1, gp * gw), lambda g, c: (0, g)),
                                    pl.BlockSpec((1, gp, dstate, gw), lambda g, c: (rev(c), g, 0, 0))]
        + c_in,
        out_specs=[blk, nblk, nblk, cols, small, small] + c_out,
        out_shape=[jax.ShapeDtypeStruct((t, D_INNER), F32), jax.ShapeDtypeStruct((t, N_GROUPS * dstate), F32),
                   jax.ShapeDtypeStruct((t, N_GROUPS * dstate), F32), jax.ShapeDtypeStruct((N_GROUPS, t, hpg), F32),
                   jax.ShapeDtypeStruct((N_GROUPS, 1, hpg), F32), jax.ShapeDtypeStruct((N_GROUPS, 1, hpg), F32)]
        + (carry.outs if carry is not None else []),
        scratch_shapes=[pltpu.VMEM((gp, dstate, gw), F32)] + c_sem,
        compiler_params=_cparams(("arbitrary", "arbitrary") if carry is not None else ("parallel", "arbitrary")),
    )(xa, xa, xa, dec["dt"], dec["acs"], dec["acs_rows"], dec["sig"], dec["a_log"], dy, dsk.reshape(1, D_INNER), prev,
      *(carry.ins if carry is not None else []))
    dxs, db, dc, ddt, dbias, dalog = res[:6]
    ddt_raw = ddt.transpose(1, 0, 2).reshape(t, N_HEADS)
    return dxs, db, dc, ddt_raw, dbias.reshape(N_HEADS), dalog.reshape(N_HEADS), res[6:]


def _ssd_post_fwd(y, xa, proj, dsk, nw):
    t = y.shape[0]
    di = D_INNER
    tm = _tile(t, (384, 128))

    def body(y_ref, xs_ref, z_ref, dsk_ref, nw_ref, o_ref):
        z = z_ref[...]
        yz = (y_ref[...] + xs_ref[...] * dsk_ref[...]) * (z * _sigmoid(z))
        nwv = nw_ref[...]
        for g in range(N_GROUPS):
            sl = slice(g * GROUP_W, (g + 1) * GROUP_W)
            v = yz[:, sl]
            rg = lax.rsqrt(jnp.mean(v * v, axis=-1, keepdims=True) + EPS)
            o_ref[:, sl] = (v * rg * nwv[:, sl]).astype(BF16)

    row = pl.BlockSpec((tm, di), lambda i: (i, 0))
    vec = pl.BlockSpec((1, di), lambda i: (0, 0))
    return pl.pallas_call(
        body, name="ssd_post_fwd", grid=(t // tm,),
        in_specs=[row, row, row, vec, vec], out_specs=row, out_shape=jax.ShapeDtypeStruct((t, di), BF16),
        compiler_params=_cparams(("parallel",)),
    )(y, xa, proj, dsk.reshape(1, di), nw.reshape(1, di))


def _ssd_post_bwd(dyn, y, xa, proj, dsk, nw, dproj):
    t = y.shape[0]
    di = D_INNER
    tm = CHUNK

    def body(dyn_ref, y_ref, xs_ref, z_ref, dsk_ref, nw_ref, _, dy_ref, dz_ref, dnw_ref, ddsk_ref):
        @pl.when(pl.program_id(0) == 0)
        def _():
            dnw_ref[...] = jnp.zeros_like(dnw_ref)
            ddsk_ref[...] = jnp.zeros_like(ddsk_ref)

        for g in range(N_GROUPS):
            sl = slice(g * GROUP_W, (g + 1) * GROUP_W)
            z = z_ref[:, sl]
            sig = _sigmoid(z)
            s = z * sig
            xs = xs_ref[:, sl]
            dskv = dsk_ref[:, sl]
            yt = y_ref[:, sl] + xs * dskv
            yz = yt * s
            rg = lax.rsqrt(jnp.mean(yz * yz, axis=-1, keepdims=True) + EPS)
            xhat = yz * rg
            dynv = dyn_ref[:, sl]
            gg = dynv * nw_ref[:, sl]
            dnw_ref[:, sl] += jnp.sum(dynv * xhat, axis=0, keepdims=True)
            dyz = rg * (gg - xhat * jnp.mean(gg * xhat, axis=-1, keepdims=True))
            dyt = dyz * s
            dy_ref[:, sl] = dyt
            dz_ref[:, sl] = (dyz * yt * (sig * (1.0 + z * (1.0 - sig)))).astype(BF16)
            ddsk_ref[:, sl] += jnp.sum(dyt * xs, axis=0, keepdims=True)

    row = pl.BlockSpec((tm, di), lambda i: (i, 0))
    vec = pl.BlockSpec((1, di), lambda i: (0, 0))
    dy, dz, dnw, ddsk = pl.pallas_call(
        body, name="ssd_post_bwd", grid=(t // tm,),
        in_specs=[row, row, row, row, vec, vec, pl.BlockSpec(memory_space=pl.ANY)], out_specs=[row, row, vec, vec],
        out_shape=[jax.ShapeDtypeStruct((t, di), F32), jax.ShapeDtypeStruct(dproj.shape, BF16),
                   jax.ShapeDtypeStruct((1, di), F32), jax.ShapeDtypeStruct((1, di), F32)],
        input_output_aliases={6: 1},
        compiler_params=_cparams(("arbitrary",)),
    )(dyn, y, xa, proj, dsk.reshape(1, di), nw.reshape(1, di), dproj)
    return dy, dz, dnw.reshape(di), ddsk.reshape(N_HEADS, HEAD_DIM).sum(axis=1)


SHARD_COLS = IN_COLS // N_CHIPS


def _split_w_in(w_in_sh):
    nl = w_in_sh.shape[0]
    w = w_in_sh.transpose(0, 2, 1, 3).reshape(nl, D_MODEL, IN_COLS)
    main = jnp.concatenate([w[..., OFF_Z:OFF_XBC], w[..., OFF_GATE:], w[..., :OFF_Z], w[..., OFF_XBC:OFF_DT]], axis=-1)
    dt = jnp.pad(w[..., OFF_DT:OFF_GATE], ((0, 0), (0, 0), (0, DT_PAD - N_HEADS)))
    return main, dt


def _merge_dw_in(dmain, ddt):
    full = jnp.concatenate([dmain[:, PPOOL:PXBC], dmain[:, PZ:PGATE], dmain[:, PXBC:], ddt[:, :N_HEADS], dmain[:, PGATE:PPOOL]],
                           axis=1)
    return full.reshape(D_MODEL, N_CHIPS, SHARD_COLS).transpose(1, 0, 2)


def _relu2_epilogue(acc):
    hid = jnp.maximum(acc, 0.0)
    return acc, hid * hid


def _relu2_bwd_epilogue(acc, pre):
    return (acc * 2.0 * jnp.maximum(pre, 0.0),)


def _add_epilogue(acc, res):
    return (acc + res,)


def _norm_bwd_epilogue(acc, dres, h, w):
    r = lax.rsqrt(jnp.mean(h * h, axis=-1, keepdims=True) + EPS)
    xhat = h * r
    g = acc * w
    dh = dres + r * (g - xhat * jnp.mean(g * xhat, axis=-1, keepdims=True))
    return dh, jnp.sum(acc * xhat, axis=0, keepdims=True)


def _add_norm_bwd_epilogue(acc, part, dres, h, w):
    return _norm_bwd_epilogue(acc + part, dres, h, w)


def _add_norm_epilogue(acc, res, w):
    x = acc + res
    r = lax.rsqrt(jnp.mean(x * x, axis=-1, keepdims=True) + EPS)
    return x, x * r * w


def _w_spec(rows, cols, fn):
    return pl.BlockSpec((None, None, rows, cols), fn)


def _with_carry(stage, call):
    carry = stage[0]() if stage is not None else None
    res = call(carry)
    if carry is None:
        return res
    outs, got = res
    stage[1](got)
    return outs[0] if len(outs) == 1 else outs


def _layer_fwd(h, u, w, big, layer, stages=None, next_norm_w=None):
    t = h.shape[0]
    d = D_MODEL
    tm = _tile(t, (1408, 384, 128))
    nt = t // tm
    row = _bs((tm, d), lambda i, j, k: (i, 0))
    vec = _bs((1, d), lambda i, j, k: (0, 0))
    stages = stages or {}
    s = {"h": h}
    if u is None:
        u = _norm_fwd(h, w["mix_norm_w"], "norm_mix")
    proj = _with_carry(stages.get("a_ici"), lambda carry: _mm(
        f"mm_proj_{layer}", _NN, (nt, PCOLS // d, 1), u, row,
        big["w_in_main"], _bs((None, d, d), lambda i, j, k: (0, 0, j)),
        [((t, PCOLS), F32, _bs((tm, d), lambda i, j, k: (i, j)))], carry=carry))
    dt_raw = _mm(f"mm_dt_{layer}", _NN, (nt, 1, 1), u, row,
                 big["w_in_dt"], _bs((None, d, DT_PAD), lambda i, j, k: (0, 0, 0)),
                 [((t, DT_PAD), F32, _bs((tm, DT_PAD), lambda i, j, k: (i, 0)))])
    pooled, yg, ypm, y_pool = _pool_fwd(proj, big["pool_w_group"], w["pool_scale"], big["w_pool_up"], layer)
    xa, conv_pre = _conv_fwd(proj, w["conv_w"], w["conv_b"])
    dec = _ssd_decay(dt_raw, w["dt_bias"], w["a_log"])
    stage = stages.get("b_ici")
    carry = stage[0]() if stage is not None else None
    y, prev, got = _ssd_fwd(xa, dec, carry)
    if carry is not None:
        stage[1](got)
    yn = _ssd_post_fwd(y, xa, proj, w["d_skip_exp"], w["ssd_norm_w"])
    rs = D_INNER // N_CHIPS
    y_ssd = _with_carry(stages.get("a_d2d"), lambda carry: _mm(
        f"mm_ssd_out_{layer}", _NN, (nt, 1, 1), yn, _bs((tm, D_INNER), lambda i, j, k: (i, 0)),
        big["w_ssd_out"], _bs((None, N_CHIPS, rs, d), lambda i, j, k: (0, 0, 0, 0)), [((t, d), F32, row)], carry=carry))
    merged = _gate_fwd(proj, w["b_gate"], y_pool, y_ssd)
    ro = d // N_CHIPS
    h1, v = _mm(f"mm_o_{layer}", _NN, (nt, 1, 1), merged, row,
                big["w_o"], _bs((None, N_CHIPS, ro, d), lambda i, j, k: (0, 0, 0, 0)),
                [((t, d), F32, row), ((t, d), BF16, row)],
                extras=[(h, row), (w["mlp_norm_w"].reshape(1, d), vec)], epilogue=_add_norm_epilogue)
    tile = _bs((tm, d), lambda i, j, k: (i, j))
    pre, act = _with_carry(stages.get("b_d2d"), lambda carry: _mm(
        f"mm_ff1_{layer}", _NN, (nt, N_CHIPS, 1), v, row,
        big["w_ff1"], _w_spec(d, d, lambda i, j, k: (0, j, 0, 0)),
        [((t, D_FF), F32, tile), ((t, D_FF), BF16, tile)], epilogue=_relu2_epilogue, carry=carry))
    a_spec = _bs((tm, d), lambda i, j, k: (i, k))
    b_spec = _w_spec(d, d, lambda i, j, k: (0, k, 0, 0))
    if next_norm_w is None:
        h2 = _mm(f"mm_ff2_{layer}", _NN, (nt, 1, N_CHIPS), act, a_spec, big["w_ff2"], b_spec, [((t, d), F32, row)],
                 extras=[(h1, row)], epilogue=_add_epilogue)
        u_next = None
    else:
        h2, u_next = _mm(f"mm_ff2_{layer}", _NN, (nt, 1, N_CHIPS), act, a_spec, big["w_ff2"], b_spec,
                         [((t, d), F32, row), ((t, d), BF16, row)],
                         extras=[(h1, row), (next_norm_w.reshape(1, d), vec)], epilogue=_add_norm_epilogue)
    s.update(u=u, proj=proj, dec=dec, conv_pre=conv_pre, pooled=pooled, yg=yg, ypm=ypm, y_pool=y_pool, xa=xa, y=y, prev=prev, yn=yn,
             y_ssd=y_ssd, merged=merged, h1=h1, v=v, pre=pre, act=act)
    return h2, u_next, s


def _dw(name, layer, a, b, shard_shape, by, tk, carry=None):
    t = a.shape[0]
    nk = t // tk
    rows, cols = shard_shape
    if by == "rows":
        grid = (N_CHIPS, 1, nk)
        a_spec = _bs((tk, rows), lambda i, j, k: (k, i))
        b_spec = _bs((tk, cols), lambda i, j, k: (k, 0))
        o_spec = _bs((None, rows, cols), lambda i, j, k: (i, 0, 0))
    else:
        grid = (1, N_CHIPS, nk)
        a_spec = _bs((tk, rows), lambda i, j, k: (k, 0))
        b_spec = _bs((tk, cols), lambda i, j, k: (k, j))
        o_spec = _bs((None, rows, cols), lambda i, j, k: (j, 0, 0))
    return _mm(f"{name}_{layer}", _TN, grid, a, a_spec, b, b_spec, [((N_CHIPS, rows, cols), F32, o_spec)], carry=carry)


EARLY = ("w_ff2", "w_ff1", "w_o", "w_ssd_out", "w_pool_up", "pool_w_group")


def _layer_bwd(dh, s, w, big, layer, red=None, last=False):
    t = dh.shape[0]
    d = D_MODEL
    tm = _tile(t, (1408, 384, 128))
    tk = _tile(t, (1408, 384, 128))
    nt = t // tm
    row = _bs((tm, d), lambda i, j, k: (i, 0))
    tile = _bs((tm, d), lambda i, j, k: (i, j))
    g, gb = {}, {}
    carry = red.swap_carry() if red is not None else None
    dpre = _mm(f"mm_dact_{layer}", _NT, (nt, N_CHIPS, 1), dh, row,
               big["w_ff2"], _w_spec(d, d, lambda i, j, k: (0, j, 0, 0)), [((t, D_FF), BF16, tile)],
               extras=[(s["pre"], tile)], epilogue=_relu2_bwd_epilogue, carry=carry)
    if carry is not None:
        (dpre,), got = dpre
        red.after_swap(got)
    gb["w_ff2"] = _dw("mm_dw_ff2", layer, s["act"], dh, (d, d), "rows", tk)
    vec = _bs((1, d), lambda i, j, k: (0, 0))
    th = _tile(t, (704, 384, 128))
    half_row = _bs((th, d), lambda i, j, k: (i, 0))
    dh1, dnw = _mm(f"mm_dv_{layer}", _NT, (t // th, 1, N_CHIPS), dpre, _bs((th, d), lambda i, j, k: (i, k)),
                   big["w_ff1"], _w_spec(d, d, lambda i, j, k: (0, k, 0, 0)), [((t, d), F32, half_row), ((1, d), F32, vec)],
                   extras=[(dh, half_row), (s["h1"], half_row), (w["mlp_norm_w"].reshape(1, d), vec)],
                   epilogue=_norm_bwd_epilogue, summed=(1,))
    g["mlp_norm_w"] = dnw.reshape(d)
    gb["w_ff1"] = _dw("mm_dw_ff1", layer, s["v"], dpre, (d, d), "cols", tk)
    ro = d // N_CHIPS
    dmerged = _mm(f"mm_dmerged_{layer}", _NT, (nt, N_CHIPS, 1), dh1, row,
                  big["w_o"], _w_spec(ro, d, lambda i, j, k: (0, j, 0, 0)),
                  [((t, d), F32, _bs((tm, ro), lambda i, j, k: (i, j)))])
    gb["w_o"] = _dw("mm_dw_o", layer, s["merged"], dh1, (ro, d), "rows", tk)
    dy_pool, dy_ssd, dproj, g["b_gate"] = _gate_bwd(dmerged, s["proj"], w["b_gate"], s["y_pool"], s["y_ssd"])
    rs = D_INNER // N_CHIPS
    dyn = _mm(f"mm_dyn_{layer}", _NT, (nt, N_CHIPS, 1), dy_ssd, row,
              big["w_ssd_out"], _w_spec(rs, d, lambda i, j, k: (0, j, 0, 0)),
              [((t, D_INNER), F32, _bs((tm, rs), lambda i, j, k: (i, j)))])
    gb["w_ssd_out"] = _dw("mm_dw_ssd_out", layer, s["yn"], dy_ssd, (rs, d), "rows", tk)
    dy, dproj, g["ssd_norm_w"], g["d_skip"] = _ssd_post_bwd(dyn, s["y"], s["xa"], s["proj"], w["d_skip_exp"],
                                                            w["ssd_norm_w"], dproj)
    carry = red.ici_carry() if red is not None else None
    dxs, db, dc, ddt_raw, g["dt_bias"], g["a_log"], got = _ssd_bwd(dy, w["d_skip_exp"], s["xa"], s["prev"], s["dec"], carry)
    if carry is not None:
        red.after_ici(got)
    own = red is not None and last
    dproj, dcw1, dcb1 = _conv_bwd(dxs, 0, s["proj"], s["conv_pre"], w["conv_w"], dproj)
    dproj, dcw2, dcb2 = _conv_bwd(db, 2, s["proj"], s["conv_pre"], w["conv_w"], dproj)
    dproj, dcw3, dcb3 = _conv_bwd(dc, 3, s["proj"], s["conv_pre"], w["conv_w"], dproj)
    g["conv_w"] = jnp.concatenate([dcw1, dcw2, dcw3], axis=1)
    g["conv_b"] = jnp.concatenate([dcb1, dcb2, dcb3])
    q, dyg, g["pool_scale"] = _pool_bwd_a(dy_pool, s["yg"], w["pool_scale"], big["pool_w_group"], big["w_pool_up"], layer)
    gb["w_pool_up"] = _dw("mm_dw_pool_up", layer, s["ypm"], dy_pool, (ro, d), "rows", tk)
    gb["pool_w_group"] = _pool_dwg(s["pooled"], dyg, layer).reshape(N_CHIPS, POOL_GDIM, POOL_GDIM)
    dproj = _pool_bwd_b(q, dproj)
    if own:
        red.push(layer, {n: gb[n] for n in EARLY})
    ddt = jnp.pad(ddt_raw.astype(BF16), ((0, 0), (0, DT_PAD - N_HEADS)))
    nk = PCOLS // d
    kw = PCOLS // 4
    du = _with_carry((red.swap_carry, red.after_swap) if own else None, lambda carry: _mm(
        f"mm_du_{layer}", _NT, (nt, 1, PCOLS // kw), dproj, _bs((tm, kw), lambda i, j, k: (i, k)),
        big["w_in_main"], _bs((None, d, kw), lambda i, j, k: (0, 0, k)), [((t, d), F32, row)], carry=carry))
    dh0, dnw = _mm(f"mm_du_dt_{layer}", _NT, (t // th, 1, 1), ddt, _bs((th, DT_PAD), lambda i, j, k: (i, 0)),
                   big["w_in_dt"], _bs((None, d, DT_PAD), lambda i, j, k: (0, 0, 0)),
                   [((t, d), F32, half_row), ((1, d), F32, vec)],
                   extras=[(du, half_row), (dh1, half_row), (s["h"], half_row), (w["mix_norm_w"].reshape(1, d), vec)],
                   epilogue=_add_norm_bwd_epilogue, summed=(1,))
    g["mix_norm_w"] = dnw.reshape(d)
    ntk = t // tk
    u_spec = _bs((tk, d), lambda i, j, k: (k, 0))
    dmain = _with_carry((red.ici_carry, red.after_ici) if own else None, lambda carry: _mm(
        f"mm_dw_in_{layer}", _TN, (1, nk, ntk), s["u"], u_spec, dproj, _bs((tk, d), lambda i, j, k: (k, j)),
        [((d, PCOLS), F32, _bs((d, d), lambda i, j, k: (0, j)))], carry=carry))
    ddtw = _mm(f"mm_dw_dt_{layer}", _TN, (1, 1, ntk), s["u"], u_spec, ddt, _bs((tk, DT_PAD), lambda i, j, k: (k, 0)),
               [((d, DT_PAD), F32, _bs((d, DT_PAD), lambda i, j, k: (0, 0)))])
    gb["w_in"] = _merge_dw_in(dmain, ddtw)
    if red is not None:
        red.push(layer, {n: gb[n] for n in BIG if not (own and n in EARLY)})
    return dh0, g, gb


BIG = ("w_in", "pool_w_group", "w_pool_up", "w_ssd_out", "w_o", "w_ff1", "w_ff2")
PER_LAYER = ("mix_norm_w", "w_in", "b_gate", "pool_w_group", "pool_scale", "w_pool_up", "conv_w", "conv_b", "dt_bias",
             "a_log", "d_skip", "ssd_norm_w", "w_ssd_out", "w_o", "mlp_norm_w", "w_ff1", "w_ff2")


SMALL_PER_LAYER = tuple(n for n in PER_LAYER if n not in BIG)


SHARD_SHAPE = {"w_in": (1024, SHARD_COLS), "pool_w_group": (4, WG_ROWS, POOL_GDIM), "w_pool_up": (WUP_ROWS, D_MODEL),
               "w_ssd_out": (D_INNER // N_CHIPS, D_MODEL), "w_o": (D_MODEL // N_CHIPS, D_MODEL),
               "w_ff1": (D_MODEL, D_FF // N_CHIPS), "w_ff2": (D_FF // N_CHIPS, D_MODEL)}


def _layer_view(mats):
    big = {n: mats[n].reshape((1, N_CHIPS) + SHARD_SHAPE[n]) for n in BIG if n != "w_in"}
    big["w_in_main"], big["w_in_dt"] = _split_w_in(mats["w_in"])
    return big


def _local_step(x, tgt, weights, small, red=None):
    seq = x.shape[0]
    h = jnp.concatenate([jnp.zeros((N_PAD, D_MODEL), F32), small["meta_tokens"], x], axis=0)
    saved, ws, bigs = [], [], []
    u = None
    for i in range(DEPTH):
        w = {n: small[n][i] for n in SMALL_PER_LAYER}
        w["d_skip_exp"] = jnp.repeat(w["d_skip"], HEAD_DIM)
        big = weights.layer(i)
        next_norm_w = small["mix_norm_w"][i + 1] if i + 1 < DEPTH else None
        h, u, s = _layer_fwd(h, u, w, big, i, weights.stages(i), next_norm_w)
        saved.append(s)
        ws.append(w)
        bigs.append(big)
    loss, dh, g_final = _loss_head(h, small["final_norm_w"], tgt)
    layer_g, layer_gb = [None] * DEPTH, [None] * DEPTH
    for i in reversed(range(DEPTH)):
        dh, layer_g[i], layer_gb[i] = _layer_bwd(dh, saved[i], ws[i], bigs[i], i, red, last=(i == 0))
    grads = {n: jnp.stack([layer_g[i][n] for i in range(DEPTH)]) for n in SMALL_PER_LAYER}
    grads["final_norm_w"] = g_final
    grads["meta_tokens"] = dh[N_PAD:ROW_X]
    return loss, dh[ROW_X:ROW_X + seq], layer_gb, grads


MESH = pl.DeviceIdType.MESH
LANES = 128
ANY = pl.BlockSpec(memory_space=pl.ANY)


def _place():
    x, y, c = lax.axis_index("x"), lax.axis_index("y"), lax.axis_index("c")
    chips = [(1 - x, y), (x, 1 - y), (1 - x, 1 - y)]
    return x, y, c, chips


def _remote(src, dst, send_sem, recv_sem, to):
    return pltpu.make_async_remote_copy(src_ref=src, dst_ref=dst, send_sem=send_sem, recv_sem=recv_sem,
                                        device_id=to, device_id_type=MESH)


class _WeightGatherer:
    GROUPS = {"a": ("w_in",), "b": tuple(n for n in BIG if n != "w_in")}

    def __init__(self, mine):
        self.mine = mine
        self.landing = {}
        self.ready = {}

    def _ici(self, layer, group):
        names = self.GROUPS[group]
        srcs = [self.mine[n] for n in names]
        outs = [jax.ShapeDtypeStruct((1, N_CHIPS) + s.shape[1:], BF16) for s in srcs]

        def make(ins, out, send_sems, recv_sems):
            x, y, c, chips = _place()
            me = 2 * x + y
            cps = []
            for p, ref in enumerate(ins):
                half = ref.shape[1] // 2
                rows = pl.ds(c * half, half)
                cps += [_remote(ref.at[layer, rows, :], out[p].at[0, me, rows, :], send_sems.at[3 * p + j],
                                recv_sems.at[3 * p + j], (*chip, c)) for j, chip in enumerate(chips)]
            return cps

        return _Carry(srcs, outs, 3 * len(names), make)

    def _d2d(self, layer, group):
        names = self.GROUPS[group]
        n = len(names)
        bufs = self.landing[(layer, group)]
        outs = [jax.ShapeDtypeStruct(b.shape, b.dtype) for b in bufs]

        def make(ins, out, send_sems, recv_sems):
            x, y, c, chips = _place()
            me = 2 * x + y
            sibling = (x, y, 1 - c)
            cps = []
            for p in range(n):
                half = out[p].shape[2] // 2
                for j, (cx, cy) in enumerate(chips):
                    blk = out[p].at[0, 2 * cx + cy, pl.ds(c * half, half), :]
                    cps.append(_remote(blk, blk, send_sems.at[4 * p + j], recv_sems.at[4 * p + j], sibling))
                cps.append(_remote(ins[n + p].at[layer], out[p].at[0, me], send_sems.at[4 * p + 3],
                                   recv_sems.at[4 * p + 3], sibling))
            return cps

        return _Carry(list(bufs) + [self.mine[m] for m in names], outs, 4 * n, make, alias={p: p for p in range(n)})

    def _landed(self, layer, group, bufs):
        self.landing[(layer, group)] = bufs

    def _done(self, layer, group, bufs):
        self.ready.setdefault(layer, {}).update(zip(self.GROUPS[group], bufs))

    def layer(self, i):
        if i == 0:
            for g in self.GROUPS:
                self._landed(0, g, _run_carry(self._ici(0, g), f"gather_ici_{g}_0"))
            for g in self.GROUPS:
                self._done(0, g, _run_carry(self._d2d(0, g), f"gather_d2d_{g}_0"))
        return _layer_view(self.ready[i])

    def stages(self, i):
        nxt = i + 1
        if nxt == DEPTH:
            return None
        st = {}
        for g in self.GROUPS:
            st[f"{g}_ici"] = (functools.partial(self._ici, nxt, g), functools.partial(self._landed, nxt, g))
            st[f"{g}_d2d"] = (functools.partial(self._d2d, nxt, g), functools.partial(self._done, nxt, g))
        return st


def _exchange_small(v, reduce, name):
    rows_per = v.shape[0]
    vm = pl.BlockSpec(memory_space=pltpu.VMEM)

    def body(v_ref, out_ref, *scratch):
        if reduce:
            land_ref, send_sems, recv_sems, local_sem = scratch
        else:
            land_ref = out_ref
            send_sems, recv_sems, local_sem = scratch
        x, y, c, chips = _place()
        me, sibling = (x, y, c), (x, y, 1 - c)

        def rows(px, py, pc):
            return land_ref.at[4 * px + 2 * py + pc]

        def copy(k, block, to, src=None):
            return _remote(rows(*block) if src is None else src, rows(*block), send_sems.at[k], recv_sems.at[k], to)

        mine = pltpu.make_async_copy(v_ref, rows(*me), local_sem)
        mine.start()
        first = [copy(0, me, sibling, src=v_ref)]
        first += [copy(1 + j, me, (*chip, c), src=v_ref) for j, chip in enumerate(chips)]
        for cp in first:
            cp.start()
        passed = [copy(4 + j, (*chip, c), sibling) for j, chip in enumerate(chips)]
        for j, chip in enumerate(chips):
            copy(1 + j, (*chip, c), me).wait_recv()
            passed[j].start()
        copy(0, sibling, me).wait_recv()
        for j, chip in enumerate(chips):
            copy(4 + j, (*chip, 1 - c), me).wait_recv()
        for cp in first + passed:
            cp.wait_send()
        mine.wait()
        if reduce:
            acc = land_ref[0]
            for d in range(1, 8):
                acc = acc + land_ref[d]
            out_ref[...] = acc

    sems = [pltpu.SemaphoreType.DMA((7,)), pltpu.SemaphoreType.DMA((7,)), pltpu.SemaphoreType.DMA]
    if reduce:
        out_shape = jax.ShapeDtypeStruct((rows_per, LANES), F32)
        scratch = [pltpu.VMEM((8, rows_per, LANES), F32)] + sems
    else:
        out_shape = jax.ShapeDtypeStruct((8, rows_per, LANES), F32)
        scratch = sems
    return pl.pallas_call(
        body, name=name, in_specs=[vm], out_specs=vm, out_shape=out_shape, scratch_shapes=scratch,
        compiler_params=pltpu.CompilerParams(vmem_limit_bytes=VMEM_LIMIT),
    )(v)


def _run_carry(carry, name):
    c_in, c_out, c_sem = carry.specs()
    n_in, n_out = len(c_in), len(c_out)

    def body(*refs):
        cps = carry.make(refs[:n_in], refs[n_in:n_in + n_out], *refs[n_in + n_out:])
        for cp in cps:
            cp.start()
        for cp in cps:
            cp.wait()

    return pl.pallas_call(body, name=name, in_specs=c_in, out_specs=c_out, out_shape=carry.outs,
                          scratch_shapes=c_sem, input_output_aliases=carry.aliases(0, 0))(*carry.ins)


def _row_tile(rows, last, itemsize=4, budget=2 * 1024 * 1024):
    return _tile(rows, tuple(t for t in (2048, 1024, 512, 256, 128, 64, 32, 16) if t * last * itemsize <= budget))


def _rs_add(g, got, core, name):
    _, half, last = got.shape
    tr = _row_tile(half, last)
    nb = half // tr

    def body(c_ref, g_ref, got_ref, o_ref):
        o_ref[...] = (g_ref[...] + got_ref[...]).astype(BF16)

    blk = pl.BlockSpec((None, tr, last), lambda s, i, c_ref: (s, i, 0))
    return pl.pallas_call(
        body, name=name,
        grid_spec=pltpu.PrefetchScalarGridSpec(
            num_scalar_prefetch=1, grid=(N_CHIPS, nb),
            in_specs=[pl.BlockSpec((None, tr, last), lambda s, i, c_ref: (s, c_ref[0] * nb + i, 0)), blk],
            out_specs=blk),
        out_shape=jax.ShapeDtypeStruct(got.shape, BF16),
        compiler_params=_cparams(("parallel", "parallel")),
    )(core.reshape(1).astype(jnp.int32), g, got)


def _rs_sum(own, got, acc, layer, chip, core, name):
    _, half, last = own.shape
    tr = _row_tile(half, last, budget=1024 * 1024)
    nb = half // tr
    ni = 0 if acc is None else 1

    def body(k_ref, c_ref, own_ref, got_ref, *rest):
        o_ref = rest[-1]
        tot = own_ref[...].astype(F32)
        for j in range(3):
            tot = tot + got_ref[j].astype(F32)
        o_ref[...] = tot

    return pl.pallas_call(
        body, name=name,
        grid_spec=pltpu.PrefetchScalarGridSpec(
            num_scalar_prefetch=2, grid=(nb,),
            in_specs=[pl.BlockSpec((None, tr, last), lambda i, k_ref, c_ref: (k_ref[0], i, 0)),
                      pl.BlockSpec((3, tr, last), lambda i, k_ref, c_ref: (0, i, 0))]
            + [pl.BlockSpec(memory_space=pl.ANY)] * ni,
            out_specs=pl.BlockSpec((None, tr, last), lambda i, k_ref, c_ref: (layer, c_ref[0] * nb + i, 0))),
        out_shape=jax.ShapeDtypeStruct((DEPTH, 2 * half, last), F32),
        input_output_aliases={4: 0} if ni else {},
        compiler_params=_cparams(("parallel",)),
    )(chip.reshape(1).astype(jnp.int32), core.reshape(1).astype(jnp.int32), own, got, *([acc] if ni else []))


def _rs_share(arrs):
    n = len(arrs)

    def body(*refs):
        out = refs[n:2 * n]
        send_sems, recv_sems = refs[2 * n:]
        x, y, c, _ = _place()
        sibling = (x, y, 1 - c)
        cps = []
        for p in range(n):
            half = out[p].shape[1] // 2
            mine = out[p].at[:, pl.ds(c * half, half), :]
            cps.append(_remote(mine, mine, send_sems.at[p], recv_sems.at[p], sibling))
        for cp in cps:
            cp.start()
        for p, cp in enumerate(cps):
            cp.wait_send()
            half = out[p].shape[1] // 2
            other = out[p].at[:, pl.ds((1 - c) * half, half), :]
            _remote(other, other, send_sems.at[p], recv_sems.at[p], sibling).wait_recv()

    return pl.pallas_call(
        body, name="rs_share", in_specs=[ANY] * n, out_specs=[ANY] * n,
        out_shape=[jax.ShapeDtypeStruct(a.shape, a.dtype) for a in arrs],
        input_output_aliases={p: p for p in range(n)},
        scratch_shapes=[pltpu.SemaphoreType.DMA((n,)), pltpu.SemaphoreType.DMA((n,))],
    )(*arrs)


class _GradReducer:
    def __init__(self, chip, core):
        self.chip, self.core = chip, core
        self.pending = None
        self.sums = None
        self.acc = {n: None for n in BIG}

    def push(self, layer, gb):
        assert self.pending is None
        self.pending = (layer, list(gb), list(gb.values()))

    def swap_carry(self):
        if self.pending is None:
            return None
        _, _, gs = self.pending
        outs = [jax.ShapeDtypeStruct((g.shape[0], g.shape[1] // 2, g.shape[2]), F32) for g in gs]

        def make(ins, out, send_sems, recv_sems):
            x, y, c, _ = _place()
            cps = []
            for p, ref in enumerate(ins):
                half = ref.shape[1] // 2
                cps.append(_remote(ref.at[:, pl.ds((1 - c) * half, half), :], out[p], send_sems.at[p], recv_sems.at[p],
                                   (x, y, 1 - c)))
            return cps

        return _Carry(gs, outs, len(gs), make)

    def after_swap(self, got):
        layer, names, gs = self.pending
        self.pending = None
        self.sums = (layer, names, [_rs_add(g, a, self.core, f"rs_add_{n}_{layer}") for n, g, a in zip(names, gs, got)])

    def ici_carry(self):
        if self.sums is None:
            return None
        _, _, ps = self.sums
        outs = [jax.ShapeDtypeStruct((3,) + p.shape[1:], BF16) for p in ps]

        def make(ins, out, send_sems, recv_sems):
            x, y, c, chips = _place()
            return [_remote(ins[p].at[2 * cx + cy], out[p].at[j], send_sems.at[3 * p + j], recv_sems.at[3 * p + j],
                            (cx, cy, c)) for p in range(len(ins)) for j, (cx, cy) in enumerate(chips)]

        return _Carry(ps, outs, 3 * len(ps), make)

    def after_ici(self, got):
        layer, names, ps = self.sums
        self.sums = None
        for n, p, a in zip(names, ps, got):
            self.acc[n] = _rs_sum(p, a, self.acc[n], layer, self.chip, self.core, f"rs_sum_{n}_{layer}")

    def finish(self):
        if self.pending is not None:
            self.after_swap(_run_carry(self.swap_carry(), "rs_swap_last"))
        if self.sums is not None:
            self.after_ici(_run_carry(self.ici_carry(), "rs_ici_last"))
        return dict(zip(BIG, _rs_share([self.acc[n] for n in BIG])))


def _adamw(w, g, m, v, name):
    shape = w.shape
    if len(shape) == 2:
        shape3 = (1,) + shape
    else:
        shape3 = (-1,) + shape[-2:]
    w2, g2, m2, v2 = (a.reshape(shape3) for a in (w, g, m, v))
    lead, rows, last = w2.shape
    tr = max([t for t in range(8, rows + 1, 8) if rows % t == 0 and t * last * 4 <= 2 * 1024 * 1024] or [rows])

    def body(w_ref, g_ref, m_ref, v_ref, d_ref, nm_ref, nv_ref):
        gv = g_ref[...]
        mn = ADAM_B1 * m_ref[...] + (1.0 - ADAM_B1) * gv
        vn = ADAM_B2 * v_ref[...] + (1.0 - ADAM_B2) * (gv * gv)
        m_hat = mn / (1.0 - ADAM_B1 ** ADAM_STEP)
        v_hat = vn / (1.0 - ADAM_B2 ** ADAM_STEP)
        d_ref[...] = -ADAM_LR * (m_hat / (jnp.sqrt(v_hat) + ADAM_EPS) + ADAM_WD * w_ref[...])
        nm_ref[...] = mn
        nv_ref[...] = vn

    blk = pl.BlockSpec((None, tr, last), lambda a, i: (a, i, 0))
    outs = pl.pallas_call(
        body, name=name, grid=(lead, rows // tr), in_specs=[blk] * 4, out_specs=[blk] * 3,
        out_shape=[jax.ShapeDtypeStruct((lead, rows, last), F32)] * 3,
        compiler_params=_cparams(("parallel", "parallel")),
    )(w2, g2, m2, v2)
    return tuple(o.reshape(shape) for o in outs)


SHARD_AXIS = {"conv_w": 2, "meta_tokens": 1}
SMALL_SHARDED = ("conv_w", "meta_tokens")
REPLICATED = ("mix_norm_w", "b_gate", "pool_scale", "conv_b", "dt_bias", "a_log", "d_skip", "ssd_norm_w", "mlp_norm_w",
              "final_norm_w")


def _flatten(arrs, dtype, row_mult):
    flat = jnp.concatenate([a.astype(dtype).reshape(-1) for a in arrs])
    n = flat.shape[0]
    rows = -(-n // (LANES * row_mult)) * row_mult
    return jnp.pad(flat, (0, rows * LANES - n)).reshape(rows, LANES)


def _unflatten(flat2d, shapes):
    flat = flat2d.reshape(-1)
    out, off = [], 0
    for sh in shapes:
        n = 1
        for d in sh:
            n *= d
        out.append(flat[off:off + n].reshape(sh))
        off += n
    return out


def kernel(x, meta_tokens, mix_norm_w, w_in, b_gate, pool_w_group, pool_scale, w_pool_up, conv_w, conv_b, dt_bias, a_log, d_skip, ssd_norm_w, w_ssd_out, w_o, mlp_norm_w, w_ff1, w_ff2, final_norm_w, loss_target, m_meta_tokens, m_mix_norm_w, m_w_in, m_b_gate, m_pool_w_group, m_pool_scale, m_w_pool_up, m_conv_w, m_conv_b, m_dt_bias, m_a_log, m_d_skip, m_ssd_norm_w, m_w_ssd_out, m_w_o, m_mlp_norm_w, m_w_ff1, m_w_ff2, m_final_norm_w, v_meta_tokens, v_mix_norm_w, v_w_in, v_b_gate, v_pool_w_group, v_pool_scale, v_w_pool_up, v_conv_w, v_conv_b, v_dt_bias, v_a_log, v_d_skip, v_ssd_norm_w, v_w_ssd_out, v_w_o, v_mlp_norm_w, v_w_ff1, v_w_ff2, v_final_norm_w):
    names = ("meta_tokens",) + PER_LAYER + ("final_norm_w",)
    par = dict(meta_tokens=meta_tokens, mix_norm_w=mix_norm_w, w_in=w_in, b_gate=b_gate, pool_w_group=pool_w_group,
               pool_scale=pool_scale, w_pool_up=w_pool_up, conv_w=conv_w, conv_b=conv_b, dt_bias=dt_bias, a_log=a_log,
               d_skip=d_skip, ssd_norm_w=ssd_norm_w, w_ssd_out=w_ssd_out, w_o=w_o, mlp_norm_w=mlp_norm_w, w_ff1=w_ff1,
               w_ff2=w_ff2, final_norm_w=final_norm_w)
    mom = dict(meta_tokens=m_meta_tokens, mix_norm_w=m_mix_norm_w, w_in=m_w_in, b_gate=m_b_gate, pool_w_group=m_pool_w_group,
               pool_scale=m_pool_scale, w_pool_up=m_w_pool_up, conv_w=m_conv_w, conv_b=m_conv_b, dt_bias=m_dt_bias,
               a_log=m_a_log, d_skip=m_d_skip, ssd_norm_w=m_ssd_norm_w, w_ssd_out=m_w_ssd_out, w_o=m_w_o,
               mlp_norm_w=m_mlp_norm_w, w_ff1=m_w_ff1, w_ff2=m_w_ff2, final_norm_w=m_final_norm_w)
    var = dict(meta_tokens=v_meta_tokens, mix_norm_w=v_mix_norm_w, w_in=v_w_in, b_gate=v_b_gate, pool_w_group=v_pool_w_group,
               pool_scale=v_pool_scale, w_pool_up=v_w_pool_up, conv_w=v_conv_w, conv_b=v_conv_b, dt_bias=v_dt_bias,
               a_log=v_a_log, d_skip=v_d_skip, ssd_norm_w=v_ssd_norm_w, w_ssd_out=v_w_ssd_out, w_o=v_w_o,
               mlp_norm_w=v_mlp_norm_w, w_ff1=v_w_ff1, w_ff2=v_w_ff2, final_norm_w=v_final_norm_w)
    chip = 2 * lax.axis_index("x") + lax.axis_index("y")
    core = lax.axis_index("c")

    weights = _WeightGatherer({n: par[n].astype(BF16).reshape(DEPTH, -1, par[n].shape[-1]) for n in BIG})
    small = {n: par[n] for n in REPLICATED}
    small_shapes = [par[n].shape for n in SMALL_SHARDED]
    got_small = _exchange_small(_flatten([par[n] for n in SMALL_SHARDED], F32, 8), False, "gather_small")
    pieces = [_unflatten(got_small[2 * k], small_shapes) for k in range(N_CHIPS)]
    for j, n in enumerate(SMALL_SHARDED):
        small[n] = jnp.concatenate([pieces[k][j] for k in range(N_CHIPS)], axis=SHARD_AXIS[n])

    reducer = _GradReducer(chip, core)
    loss, dx, _, grads = _local_step(x[0], loss_target[0], weights, small, reducer)

    small_names = REPLICATED + SMALL_SHARDED
    red = _exchange_small(_flatten([grads[n] for n in small_names] + [loss.reshape(1)], F32, 8), True, "allreduce_small")
    *parts, loss = _unflatten(red, [grads[n].shape for n in small_names] + [(1,)])
    loss = loss[0]
    gsum = dict(zip(small_names, parts))
    for n in SMALL_SHARDED:
        width = par[n].shape[SHARD_AXIS[n]]
        gsum[n] = lax.dynamic_slice_in_dim(gsum[n], chip * width, width, axis=SHARD_AXIS[n])

    gsum.update({n: g.reshape(par[n].shape) for n, g in reducer.finish().items()})

    delta, new_m, new_v = {}, {}, {}
    for n in BIG:
        if n == "w_in":
            tr = lambda a: jnp.swapaxes(a, 1, 2)
            outs = _adamw(tr(par[n]), tr(gsum[n]), tr(mom[n]), tr(var[n]), "adamw_" + n)
            delta[n], new_m[n], new_v[n] = (tr(o) for o in outs)
        else:
            delta[n], new_m[n], new_v[n] = _adamw(par[n], gsum[n], mom[n], var[n], "adamw_" + n)
    rest = [n for n in names if n not in BIG]
    shapes = [par[n].shape for n in rest]
    packed = [_flatten([d[n] for n in rest], F32, 8) for d in (par, gsum, mom, var)]
    for d, flat in zip((delta, new_m, new_v), _adamw(*packed, "adamw_small")):
        d.update(zip(rest, _unflatten(flat, shapes)))

    order = ("meta_tokens", "mix_norm_w", "w_in", "b_gate", "pool_w_group", "pool_scale", "w_pool_up", "conv_w", "conv_b",
             "dt_bias", "a_log", "d_skip", "ssd_norm_w", "w_ssd_out", "w_o", "mlp_norm_w", "w_ff1", "w_ff2", "final_norm_w")
    return (loss, dx[None], *[gsum[n] for n in order], *[delta[n] for n in order], *[new_m[n] for n in order],
            *[new_v[n] for n in order])
```

```python
import functools

import jax
import jax.numpy as jnp
from jax import lax
from jax.experimental import pallas as pl
from jax.experimental.pallas import tpu as pltpu

F32 = jnp.float32
BF16 = jnp.bfloat16

D_MODEL = 1024
DEPTH = 4
N_META = 16
N_PAD = 112
ROW_X = N_PAD + N_META
POOL_WINDOWS = (2, 4, 8, 16)
POOL_GDIM = 256
D_INNER = 2048
N_HEADS = 32
HEAD_DIM = 64
N_GROUPS = 8
HEADS_PER_GROUP = 4
GROUP_W = HEADS_PER_GROUP * HEAD_DIM
D_STATE = 128
CHUNK = 128
D_XBC = 4096
D_FF = 4096
EPS = 1e-5
OFF_Z, OFF_XBC, OFF_DT, OFF_GATE, IN_COLS = 1024, 3072, 7168, 7200, 9248
PZ, PGATE, PPOOL, PXBC, PCOLS = 0, 2048, 4096, 5120, 9216
DT_PAD = 128

ADAM_LR, ADAM_B1, ADAM_B2, ADAM_EPS, ADAM_WD, ADAM_STEP = 0.001, 0.9, 0.999, 1e-08, 0.01, 10

VMEM_LIMIT = 56 * 1024 * 1024

_NN = (((1,), (0,)), ((), ()))
_NT = (((1,), (1,)), ((), ()))
_TN = (((0,), (0,)), ((), ()))


def _dot(a, b, dn=_NN):
    return lax.dot_general(a, b, dn, preferred_element_type=F32)


def _cparams(sem):
    return pltpu.CompilerParams(dimension_semantics=sem, vmem_limit_bytes=VMEM_LIMIT)


def _tile(n, cands):
    for c in cands:
        if n % c == 0:
            return c
    raise ValueError(f"no tile for {n} in {cands}")


def _sigmoid(x):
    return 0.5 * jnp.tanh(0.5 * x) + 0.5


def _softplus(x):
    return jnp.maximum(x, 0.0) + jnp.log(1.0 + jnp.exp(-jnp.abs(x)))


def _bs(shape, fn):
    return pl.BlockSpec(shape, fn)


class _Carry:
    def __init__(self, ins, outs, n_sems, make, alias=None):
        self.ins, self.outs, self.n_sems, self.make = list(ins), list(outs), n_sems, make
        self.alias = dict(alias or {})

    def aliases(self, first_in, first_out):
        return {first_in + i: first_out + o for i, o in self.alias.items()}

    def specs(self):
        hbm = pl.BlockSpec(memory_space=pl.ANY)
        return [hbm] * len(self.ins), [hbm] * len(self.outs), [pltpu.SemaphoreType.DMA((self.n_sems,))] * 2

    def at_edges(self, grid, in_refs, out_refs, sems):
        ids = [pl.program_id(a) for a in range(len(grid))]
        first = functools.reduce(jnp.logical_and, [i == 0 for i in ids])
        last = functools.reduce(jnp.logical_and, [i == n - 1 for i, n in zip(ids, grid)])

        def begin():
            @pl.when(first)
            def _():
                for cp in self.make(in_refs, out_refs, *sems):
                    cp.start()

        def end():
            @pl.when(last)
            def _():
                for cp in self.make(in_refs, out_refs, *sems):
                    cp.wait()

        return begin, end


def _sum_into(o_ref, v, first):
    @pl.when(first)
    def _():
        o_ref[...] = v

    @pl.when(jnp.logical_not(first))
    def _():
        o_ref[...] += v


def _mm(name, dn, grid, a, a_spec, b, b_spec, outs, *, extras=(), epilogue=None, carry=None, summed=()):
    nk = grid[2]
    ne, no = len(extras), len(outs)
    blk = tuple(d for d in outs[0][2].block_shape if d is not None)
    c_in, c_out, c_sem = carry.specs() if carry is not None else ([], [], [])
    nci, nco = len(c_in), len(c_out)

    def body(a_ref, b_ref, *rest):
        e_refs, o_refs = rest[:ne], rest[ne + nci:ne + nci + no]
        first_row_tile = pl.program_id(0) == 0
        if carry is not None:
            begin, end = carry.at_edges(grid, rest[ne:ne + nci], rest[ne + nci + no:ne + nci + no + nco], rest[-2:])
            begin()
        if len(b_ref.shape) == 3:
            rows = b_ref.shape[1]
            p = _dot(a_ref[:, :rows].astype(BF16), b_ref[0].astype(BF16), dn)
            for s in range(1, b_ref.shape[0]):
                p = p + _dot(a_ref[:, s * rows:(s + 1) * rows].astype(BF16), b_ref[s].astype(BF16), dn)
        else:
            p = _dot(a_ref[...].astype(BF16), b_ref[...].astype(BF16), dn)

        def finish(acc):
            outs = epilogue(acc, *[e[...] for e in e_refs]) if epilogue is not None else (acc,)
            for idx, (o, v) in enumerate(zip(o_refs, outs)):
                if idx in summed:
                    _sum_into(o, v, first_row_tile)
                else:
                    o[...] = v.astype(o.dtype)

        if nk == 1:
            finish(p)
        else:
            acc_ref = rest[ne + nci + no + nco]
            kk = pl.program_id(2)

            @pl.when(kk == 0)
            def _():
                acc_ref[...] = p

            @pl.when(kk > 0)
            def _():
                acc_ref[...] += p

            @pl.when(kk == nk - 1)
            def _():
                finish(acc_ref[...])
        if carry is not None:
            end()

    res = pl.pallas_call(
        body,
        name=name,
        grid=grid,
        in_specs=[a_spec, b_spec, *[s for _, s in extras]] + c_in,
        out_specs=[o[2] for o in outs] + c_out,
        out_shape=[jax.ShapeDtypeStruct(o[0], o[1]) for o in outs] + (carry.outs if carry is not None else []),
        scratch_shapes=([pltpu.VMEM(blk, F32)] if nk > 1 else []) + c_sem,
        input_output_aliases=carry.aliases(2 + ne, no) if carry is not None else {},
        compiler_params=_cparams(("arbitrary",) * 3 if carry is not None or summed else ("parallel", "parallel", "arbitrary")),
    )(a, b, *[e for e, _ in extras], *(carry.ins if carry is not None else []))
    if carry is not None:
        return res[:no], res[no:]
    return res[0] if no == 1 else res


def _norm_fwd(h, w, name):
    t, d = h.shape
    tm = _tile(t, (1056, 384, 128))

    def body(h_ref, w_ref, u_ref):
        x = h_ref[...]
        r = lax.rsqrt(jnp.mean(x * x, axis=-1, keepdims=True) + EPS)
        u_ref[...] = (x * r * w_ref[...]).astype(BF16)

    return pl.pallas_call(
        body, name=name, grid=(t // tm,),
        in_specs=[pl.BlockSpec((tm, d), lambda i: (i, 0)), pl.BlockSpec((1, d), lambda i: (0, 0))],
        out_specs=pl.BlockSpec((tm, d), lambda i: (i, 0)),
        out_shape=jax.ShapeDtypeStruct((t, d), BF16),
        compiler_params=_cparams(("parallel",)),
    )(h, w.reshape(1, d))


def _loss_head(h, w, tgt):
    t, d = h.shape
    tm = CHUNK
    nb = ROW_X // tm

    def body(h_ref, w_ref, t_ref, loss_ref, dh_ref, dw_ref):
        i = pl.program_id(0)
        x = h_ref[...]
        r = lax.rsqrt(jnp.mean(x * x, axis=-1, keepdims=True) + EPS)
        xhat = x * r
        wv = w_ref[...]
        live = i >= nb
        err = jnp.where(live, xhat * wv - t_ref[...], 0.0)
        dout = err * (1.0 / d)
        g = dout * wv
        dh_ref[...] = r * (g - xhat * jnp.mean(g * xhat, axis=-1, keepdims=True))

        @pl.when(i == 0)
        def _():
            loss_ref[...] = jnp.zeros_like(loss_ref)
            dw_ref[...] = jnp.zeros_like(dw_ref)

        loss_ref[...] += 0.5 * jnp.sum(jnp.sum(err * err, axis=-1, keepdims=True), axis=0, keepdims=True) * (1.0 / d)
        dw_ref[...] += jnp.sum(dout * xhat, axis=0, keepdims=True)

    row = pl.BlockSpec((tm, d), lambda i: (i, 0))
    vec = pl.BlockSpec((1, d), lambda i: (0, 0))
    loss, dh, dw = pl.pallas_call(
        body, name="loss_head", grid=(t // tm,),
        in_specs=[row, vec, pl.BlockSpec((tm, d), lambda i: (jnp.maximum(i - nb, 0), 0))],
        out_specs=[pl.BlockSpec((1, 1), lambda i: (0, 0)), row, vec],
        out_shape=[jax.ShapeDtypeStruct((1, 1), F32), jax.ShapeDtypeStruct((t, d), F32), jax.ShapeDtypeStruct((1, d), F32)],
        compiler_params=_cparams(("arbitrary",)),
    )(h, w.reshape(1, d), tgt)
    return loss[0, 0], dh, dw.reshape(d)


def _gate_fwd(proj, b_gate, y_pool, y_ssd):
    t = proj.shape[0]
    d = D_MODEL
    tm = _tile(t, (528, 384, 128))

    def body(gp_ref, gs_ref, bp_ref, bs_ref, yp_ref, ys_ref, o_ref):
        gp = _sigmoid(gp_ref[...] + bp_ref[...])
        gs = _sigmoid(gs_ref[...] + bs_ref[...])
        o_ref[...] = (gp * yp_ref[...] + gs * ys_ref[...]).astype(BF16)

    row = pl.BlockSpec((tm, d), lambda i: (i, 0))
    return pl.pallas_call(
        body, name="gate_fwd", grid=(t // tm,),
        in_specs=[pl.BlockSpec((tm, d), lambda i: (i, PGATE // d)), pl.BlockSpec((tm, d), lambda i: (i, PGATE // d + 1)),
                  pl.BlockSpec((1, d), lambda i: (0, 0)), pl.BlockSpec((1, d), lambda i: (0, 1)), row, row],
        out_specs=row, out_shape=jax.ShapeDtypeStruct((t, d), BF16),
        compiler_params=_cparams(("parallel",)),
    )(proj, proj, b_gate.reshape(1, 2 * d), b_gate.reshape(1, 2 * d), y_pool, y_ssd)


def _gate_bwd(dmerged, proj, b_gate, y_pool, y_ssd):
    t = proj.shape[0]
    d = D_MODEL
    tm = _tile(t, (384, 128))

    def body(dm_ref, gp_ref, gs_ref, bp_ref, bs_ref, yp_ref, ys_ref, dyp_ref, dys_ref, dg_ref, db_ref):
        dm = dm_ref[...]
        gp = _sigmoid(gp_ref[...] + bp_ref[...])
        gs = _sigmoid(gs_ref[...] + bs_ref[...])
        dyp_ref[...] = (dm * gp).astype(BF16)
        dys_ref[...] = (dm * gs).astype(BF16)
        dgp = dm * yp_ref[...] * gp * (1.0 - gp)
        dgs = dm * ys_ref[...] * gs * (1.0 - gs)
        dg_ref[:, :d] = dgp.astype(BF16)
        dg_ref[:, d:] = dgs.astype(BF16)

        @pl.when(pl.program_id(0) == 0)
        def _():
            db_ref[...] = jnp.zeros_like(db_ref)

        db_ref[:, :d] += jnp.sum(dgp, axis=0, keepdims=True)
        db_ref[:, d:] += jnp.sum(dgs, axis=0, keepdims=True)

    row = pl.BlockSpec((tm, d), lambda i: (i, 0))
    dyp, dys, dg, db = pl.pallas_call(
        body, name="gate_bwd", grid=(t // tm,),
        in_specs=[row, pl.BlockSpec((tm, d), lambda i: (i, PGATE // d)), pl.BlockSpec((tm, d), lambda i: (i, PGATE // d + 1)),
                  pl.BlockSpec((1, d), lambda i: (0, 0)), pl.BlockSpec((1, d), lambda i: (0, 1)), row, row],
        out_specs=[row, row, pl.BlockSpec((tm, 2 * d), lambda i: (i, PGATE // (2 * d))),
                   pl.BlockSpec((1, 2 * d), lambda i: (0, 0))],
        out_shape=[jax.ShapeDtypeStruct((t, d), BF16), jax.ShapeDtypeStruct((t, d), BF16),
                   jax.ShapeDtypeStruct((t, PCOLS), BF16), jax.ShapeDtypeStruct((1, 2 * d), F32)],
        compiler_params=_cparams(("arbitrary",)),
    )(dmerged, proj, proj, b_gate.reshape(1, 2 * d), b_gate.reshape(1, 2 * d), y_pool, y_ssd)
    return dyp, dys, dg, db.reshape(2 * d)


POOL_HALO = 16


def _pool_counts(row0, n, win):
    pos1 = row0 + lax.broadcasted_iota(jnp.int32, (n, 1), 0) - (N_PAD - 1)
    return jnp.clip(pos1, 1, win).astype(F32)


N_CHIPS = 4
WG_ROWS = POOL_GDIM // N_CHIPS
WUP_ROWS = D_MODEL // N_CHIPS


def _group_w(wg_ref, g):
    return jnp.concatenate([wg_ref[k, g] for k in range(N_CHIPS)], axis=0)


def _pool_w_specs(layer):
    return [pl.BlockSpec((None, N_CHIPS, len(POOL_WINDOWS), WG_ROWS, POOL_GDIM), lambda i: (0, 0, 0, 0, 0)),
            pl.BlockSpec((None, N_CHIPS, WUP_ROWS, D_MODEL), lambda i: (0, 0, 0, 0))]


def _pool_fwd(proj, wg, scale, wup, layer):
    t = proj.shape[0]
    d = D_MODEL
    tm = _tile(t, (384, 128))
    hb = tm // POOL_HALO

    def body(u_ref, halo_ref, sc_ref, wg_ref, wup_ref, pooled_ref, yg_ref, ypm_ref, yp_ref):
        i = pl.program_id(0)
        x = u_ref[...]
        halo = jnp.where(i > 0, halo_ref[...], 0.0)
        xc = jnp.concatenate([halo, x], axis=0)
        for g, win in enumerate(POOL_WINDOWS):
            sl = slice(g * POOL_GDIM, (g + 1) * POOL_GDIM)
            s = xc[:, sl]
            k = 1
            while k < win:
                s = s + pltpu.roll(s, k, axis=0)
                k *= 2
            pooled = s[POOL_HALO:] / _pool_counts(i * tm, tm, win) - x[:, sl]
            pb = pooled.astype(BF16)
            pooled_ref[:, sl] = pb
            yg_ref[:, sl] = _dot(pb, _group_w(wg_ref, g))
        ypm = (yg_ref[...] * sc_ref[...]).astype(BF16)
        ypm_ref[...] = ypm
        acc = _dot(ypm[:, :WUP_ROWS], wup_ref[0])
        for k in range(1, N_CHIPS):
            acc = acc + _dot(ypm[:, k * WUP_ROWS:(k + 1) * WUP_ROWS], wup_ref[k])
        yp_ref[...] = acc

    row = pl.BlockSpec((tm, d), lambda i: (i, 0))
    return pl.pallas_call(
        body, name=f"pool_fwd_{layer}", grid=(t // tm,),
        in_specs=[pl.BlockSpec((tm, d), lambda i: (i, PPOOL // d)),
                  pl.BlockSpec((POOL_HALO, d), lambda i: (jnp.maximum(i * hb - 1, 0), PPOOL // d)),
                  pl.BlockSpec((1, d), lambda i: (0, 0))] + _pool_w_specs(layer),
        out_specs=[row, row, row, row],
        out_shape=[jax.ShapeDtypeStruct((t, d), BF16), jax.ShapeDtypeStruct((t, d), F32),
                   jax.ShapeDtypeStruct((t, d), BF16), jax.ShapeDtypeStruct((t, d), F32)],
        compiler_params=_cparams(("parallel",)),
    )(proj, proj, scale.reshape(1, d), wg, wup)


def _pool_bwd_a(dy_pool, yg, scale, wg, wup, layer):
    t, d = yg.shape
    tm = _tile(t, (384, 128))

    def body(dy_ref, yg_ref, sc_ref, wg_ref, wup_ref, q_ref, dyg_ref, dsc_ref):
        dy = dy_ref[...]
        dypm = jnp.concatenate([_dot(dy, wup_ref[k], _NT) for k in range(N_CHIPS)], axis=1)

        @pl.when(pl.program_id(0) == 0)
        def _():
            dsc_ref[...] = jnp.zeros_like(dsc_ref)

        dsc_ref[...] += jnp.sum(dypm * yg_ref[...], axis=0, keepdims=True)
        dyg = (dypm * sc_ref[...]).astype(BF16)
        dyg_ref[...] = dyg
        for g in range(len(POOL_WINDOWS)):
            sl = slice(g * POOL_GDIM, (g + 1) * POOL_GDIM)
            q_ref[:, sl] = _dot(dyg[:, sl], _group_w(wg_ref, g), _NT)

    row = pl.BlockSpec((tm, d), lambda i: (i, 0))
    vec = pl.BlockSpec((1, d), lambda i: (0, 0))
    q, dyg, dsc = pl.pallas_call(
        body, name=f"pool_bwd_a_{layer}", grid=(t // tm,),
        in_specs=[row, row, vec] + _pool_w_specs(layer),
        out_specs=[row, row, vec],
        out_shape=[jax.ShapeDtypeStruct((t, d), F32), jax.ShapeDtypeStruct((t, d), BF16), jax.ShapeDtypeStruct((1, d), F32)],
        compiler_params=_cparams(("arbitrary",)),
    )(dy_pool, yg, scale.reshape(1, d), wg, wup)
    return q, dyg, dsc.reshape(d)


def _pool_bwd_b(q, dproj):
    t, d = q.shape
    tm = _tile(t, (384, 128))
    hb = tm // POOL_HALO
    nt = t // tm
    n = tm + POOL_HALO

    def body(q_ref, halo_ref, _, o_ref):
        i = pl.program_id(0)
        qv = q_ref[...]
        halo = jnp.where(i < nt - 1, halo_ref[...], 0.0)
        qc = jnp.concatenate([qv, halo], axis=0)
        for g, win in enumerate(POOL_WINDOWS):
            sl = slice(g * POOL_GDIM, (g + 1) * POOL_GDIM)
            s = qc[:, sl] / _pool_counts(i * tm, n, win)
            k = 1
            while k < win:
                s = s + pltpu.roll(s, n - k, axis=0)
                k *= 2
            o_ref[:, sl] = (s[:tm] - qv[:, sl]).astype(BF16)

    row = pl.BlockSpec((tm, d), lambda i: (i, 0))
    return pl.pallas_call(
        body, name="pool_bwd_b", grid=(nt,),
        in_specs=[row, pl.BlockSpec((POOL_HALO, d), lambda i: (jnp.minimum((i + 1) * hb, t // POOL_HALO - 1), 0)),
                  pl.BlockSpec(memory_space=pl.ANY)],
        out_specs=pl.BlockSpec((tm, d), lambda i: (i, PPOOL // d)), out_shape=jax.ShapeDtypeStruct(dproj.shape, BF16),
        input_output_aliases={2: 0},
        compiler_params=_cparams(("parallel",)),
    )(q, q, dproj)


def _pool_dwg(pooled, dyg, layer):
    t, d = pooled.shape
    tk = _tile(t, (1056, 384, 128))
    nk = t // tk
    gd = POOL_GDIM
    ng = d // gd

    def body(p_ref, g_ref, o_ref):
        @pl.when(pl.program_id(1) == 0)
        def _():
            o_ref[...] = jnp.zeros_like(o_ref)

        part = _dot(p_ref[...], g_ref[...], _TN)
        for k in range(N_CHIPS):
            o_ref[k] += part[k * WG_ROWS:(k + 1) * WG_ROWS]

    blk = pl.BlockSpec((tk, gd), lambda g, k: (k, g))
    return pl.pallas_call(
        body, name=f"pool_dwg_{layer}", grid=(ng, nk), in_specs=[blk, blk],
        out_specs=pl.BlockSpec((N_CHIPS, None, WG_ROWS, gd), lambda g, k: (0, g, 0, 0)),
        out_shape=jax.ShapeDtypeStruct((N_CHIPS, ng, WG_ROWS, gd), F32),
        compiler_params=_cparams(("parallel", "arbitrary")),
    )(pooled, dyg)


CONV_W = 4
CONV_HALO = 8
XBC_BLK = PXBC // 1024


def _conv_fwd(proj, conv_w, conv_b):
    t = proj.shape[0]
    cw = 1024
    tm = _tile(t, (1056, 384, 128))
    hb = tm // CONV_HALO

    def body(x_ref, halo_ref, w_ref, b_ref, o_ref, pre_ref):
        i = pl.program_id(1)
        x = x_ref[...]
        halo = jnp.where(i > 0, halo_ref[...], 0.0)
        xc = jnp.concatenate([halo, x], axis=0)
        w = w_ref[...]
        acc = b_ref[...] + x * w[CONV_W - 1:CONV_W, :]
        for k in range(CONV_W - 1):
            acc = acc + pltpu.roll(xc, CONV_W - 1 - k, axis=0)[CONV_HALO:] * w[k:k + 1, :]
        row = i * tm + lax.broadcasted_iota(jnp.int32, (tm, 1), 0)
        pre_ref[...] = acc
        o_ref[...] = jnp.where(row >= N_PAD, acc * _sigmoid(acc), 0.0)

    blk = pl.BlockSpec((tm, cw), lambda j, i: (i, j))
    return pl.pallas_call(
        body, name="conv_fwd", grid=(D_XBC // cw, t // tm),
        in_specs=[pl.BlockSpec((tm, cw), lambda j, i: (i, XBC_BLK + j)),
                  pl.BlockSpec((CONV_HALO, cw), lambda j, i: (jnp.maximum(i * hb - 1, 0), XBC_BLK + j)),
                  pl.BlockSpec((CONV_W, cw), lambda j, i: (0, j)), pl.BlockSpec((1, cw), lambda j, i: (0, j))],
        out_specs=[blk, blk],
        out_shape=[jax.ShapeDtypeStruct((t, D_XBC), F32), jax.ShapeDtypeStruct((t, D_XBC), F32)],
        compiler_params=_cparams(("parallel", "parallel")),
    )(proj, proj, conv_w, conv_b.reshape(1, D_XBC))


def _conv_bwd(dxa, coff, proj, pre, conv_w, dproj):
    t, ncols = dxa.shape
    cw = 1024
    tm = _tile(t, (528, 384, 128))
    hb = tm // CONV_HALO
    nt = t // tm
    n = tm + CONV_HALO

    def body(d_ref, dn_ref, p_ref, pn_ref, x_ref, w_ref, _, o_ref, dw_ref, db_ref):
        i = pl.program_id(1)
        last = i == nt - 1
        xc = jnp.concatenate([p_ref[...], pn_ref[...]], axis=0)
        df = jnp.concatenate([d_ref[...], dn_ref[...]], axis=0)
        w = w_ref[...]
        sig = _sigmoid(xc)
        row = i * tm + lax.broadcasted_iota(jnp.int32, (n, 1), 0)
        live = (row >= N_PAD) & ((row < (i + 1) * tm) | jnp.logical_not(last))
        dxc = jnp.where(live, df * (sig * (1.0 + xc * (1.0 - sig))), 0.0)
        ahead = [pltpu.roll(dxc, n - (CONV_W - 1 - k), axis=0)[:tm] for k in range(CONV_W - 1)] + [dxc[:tm]]
        acc = ahead[0] * w[0:1, :]
        for k in range(1, CONV_W):
            acc = acc + ahead[k] * w[k:k + 1, :]
        o_ref[...] = acc.astype(BF16)

        @pl.when(i == 0)
        def _():
            dw_ref[...] = jnp.zeros_like(dw_ref)
            db_ref[...] = jnp.zeros_like(db_ref)

        x = x_ref[...]
        db_ref[...] += jnp.sum(ahead[CONV_W - 1], axis=0, keepdims=True)
        for k in range(CONV_W):
            dw_ref[k:k + 1, :] += jnp.sum(ahead[k] * x, axis=0, keepdims=True)

    def pspec(rows, fn):
        return pl.BlockSpec((rows, cw), lambda j, i: (fn(i), coff + j))

    nxt = lambda i: jnp.minimum((i + 1) * hb, t // CONV_HALO - 1)
    dxbc, dw, db = pl.pallas_call(
        body, name=f"conv_bwd_{coff}", grid=(ncols // cw, nt),
        in_specs=[pl.BlockSpec((tm, cw), lambda j, i: (i, j)), pl.BlockSpec((CONV_HALO, cw), lambda j, i: (nxt(i), j)),
                  pspec(tm, lambda i: i), pspec(CONV_HALO, nxt),
                  pl.BlockSpec((tm, cw), lambda j, i: (i, XBC_BLK + coff + j)),
                  pl.BlockSpec((CONV_W, cw), lambda j, i: (0, coff + j)),
                  pl.BlockSpec(memory_space=pl.ANY)],
        out_specs=[pl.BlockSpec((tm, cw), lambda j, i: (i, XBC_BLK + coff + j)), pl.BlockSpec((CONV_W, cw), lambda j, i: (0, j)),
                   pl.BlockSpec((1, cw), lambda j, i: (0, j))],
        out_shape=[jax.ShapeDtypeStruct(dproj.shape, BF16), jax.ShapeDtypeStruct((CONV_W, ncols), F32),
                   jax.ShapeDtypeStruct((1, ncols), F32)],
        input_output_aliases={6: 0},
        compiler_params=_cparams(("parallel", "arbitrary")),
    )(dxa, dxa, pre, pre, proj, conv_w, dproj)
    return dxbc, dw, db.reshape(ncols)


def _cumsum(x, axis, reverse=False):
    n = x.shape[axis]
    idx = lax.broadcasted_iota(jnp.int32, x.shape, axis)
    k = 1
    while k < n:
        if reverse:
            x = x + jnp.where(idx < n - k, pltpu.roll(x, n - k, axis=axis), 0.0)
        else:
            x = x + jnp.where(idx >= k, pltpu.roll(x, k, axis=axis), 0.0)
        k *= 2
    return x


def _head_masks():
    lane = lax.broadcasted_iota(jnp.int32, (1, GROUP_W), 1)
    return [(lane >= r * HEAD_DIM) & (lane < (r + 1) * HEAD_DIM) for r in range(HEADS_PER_GROUP)]


def _expand_heads(cols, hm):
    out = jnp.where(hm[0], cols[:, 0:1], 0.0)
    for r in range(1, HEADS_PER_GROUP):
        out = out + jnp.where(hm[r], cols[:, r:r + 1], 0.0)
    return out


def _ssd_decay(dt_raw, dt_bias, a_log):
    t = dt_raw.shape[0]

    hpg = HEADS_PER_GROUP

    def body(raw_ref, b_ref, al_ref, dt_ref, acs_ref, sig_ref, rows_ref):
        raw = raw_ref[...] + b_ref[...]
        rowid = pl.program_id(0) * CHUNK + lax.broadcasted_iota(jnp.int32, (CHUNK, 1), 0)
        dt = jnp.where(rowid >= N_PAD, _softplus(raw), 0.0)
        acs = _cumsum(dt * -jnp.exp(al_ref[...]), 0)
        sig = _sigmoid(raw)
        acs_t = acs.T
        for g in range(N_GROUPS):
            heads = slice(g * hpg, (g + 1) * hpg)
            dt_ref[g] = dt[:, heads]
            acs_ref[g] = acs[:, heads]
            sig_ref[g] = sig[:, heads]
            rows_ref[g] = acs_t[heads, :]

    blk = pl.BlockSpec((CHUNK, DT_PAD), lambda c: (c, 0))
    vec = pl.BlockSpec((1, DT_PAD), lambda c: (0, 0))
    cols = pl.BlockSpec((N_GROUPS, CHUNK, hpg), lambda c: (0, c, 0))
    pad = lambda v: jnp.pad(v, (0, DT_PAD - N_HEADS)).reshape(1, DT_PAD)
    dt, acs, sig, rows = pl.pallas_call(
        body, name="ssd_decay", grid=(t // CHUNK,), in_specs=[blk, vec, vec],
        out_specs=[cols, cols, cols, pl.BlockSpec((N_GROUPS, hpg, CHUNK), lambda c: (0, 0, c))],
        out_shape=[jax.ShapeDtypeStruct((N_GROUPS, t, hpg), F32)] * 3 + [jax.ShapeDtypeStruct((N_GROUPS, hpg, t), F32)],
        compiler_params=_cparams(("parallel",)),
    )(dt_raw, pad(dt_bias), pad(a_log))
    return dict(dt=dt, acs=acs, acs_rows=rows, sig=sig, a_log=a_log.reshape(N_GROUPS, 1, hpg))


def _ssd_q(dtc, acs4, acs_r):
    hm = _head_masks()
    dt_exp = _expand_heads(dtc, hm)
    acs = _expand_heads(acs4, hm)
    atot = acs[CHUNK - 1:CHUNK, :]
    return dict(dtc=dtc, hm=hm, dt_exp=dt_exp, acs=acs, acs_r=acs_r,
                ea=jnp.exp(acs), ds=jnp.exp(atot - acs), dec=jnp.exp(atot))


def _stack4(x):
    return jnp.concatenate([x] * HEADS_PER_GROUP, axis=0)


def _ssd_decay_stacks(q):
    hpg = HEADS_PER_GROUP
    a_col = jnp.concatenate([q["acs"][:, r * HEAD_DIM:r * HEAD_DIM + 1] for r in range(hpg)], axis=0)
    a_row = jnp.concatenate([jnp.broadcast_to(q["acs_r"][r:r + 1, :], (CHUNK, CHUNK)) for r in range(hpg)], axis=0)
    ri = lax.broadcasted_iota(jnp.int32, (hpg * CHUNK, CHUNK), 0) % CHUNK
    ci = lax.broadcasted_iota(jnp.int32, (hpg * CHUNK, CHUNK), 1)
    diff = a_col - a_row
    lm = jnp.exp(jnp.where(ri >= ci, diff, -jnp.inf))
    lt = jnp.exp(jnp.where(ri <= ci, -diff, -jnp.inf))
    return lm, lt


def _pick_heads(stacked, hm):
    out = jnp.where(hm[0], stacked[:CHUNK], 0.0)
    for r in range(1, HEADS_PER_GROUP):
        out = out + jnp.where(hm[r], stacked[r * CHUNK:(r + 1) * CHUNK], 0.0)
    return out


def _mask_heads(x, hm):
    return jnp.concatenate([jnp.where(hm[r], x, 0.0) for r in range(HEADS_PER_GROUP)], axis=0)


SSD_GP_FWD, SSD_GP_BWD = 8, 8


def _ssd_specs(cidx, gp):
    hpg = HEADS_PER_GROUP
    return [
        pl.BlockSpec((CHUNK, gp * GROUP_W), lambda g, c: (cidx(c), g)),
        pl.BlockSpec((CHUNK, gp * D_STATE), lambda g, c: (cidx(c), D_INNER // (gp * D_STATE) + g)),
        pl.BlockSpec((CHUNK, gp * D_STATE), lambda g, c: (cidx(c), (D_INNER + 1024) // (gp * D_STATE) + g)),
        pl.BlockSpec((gp, CHUNK, hpg), lambda g, c: (g, cidx(c), 0)),
        pl.BlockSpec((gp, CHUNK, hpg), lambda g, c: (g, cidx(c), 0)),
        pl.BlockSpec((gp, hpg, CHUNK), lambda g, c: (g, 0, cidx(c))),
    ]


def _ssd_fwd(xa, dec, carry=None):
    t = xa.shape[0]
    nc = t // CHUNK
    gp, gw, ds = SSD_GP_FWD, GROUP_W, D_STATE
    grid = (N_GROUPS // gp, nc)
    c_in, c_out, c_sem = carry.specs() if carry is not None else ([], [], [])
    nci, nco = len(c_in), len(c_out)

    def body(*refs):
        xs_ref, b_ref, c_ref, dt_ref, acs_ref, acsr_ref = refs[:6]
        y_ref, prev_ref = refs[6 + nci:8 + nci]
        st_ref = refs[8 + nci + nco]
        if carry is not None:
            begin, end = carry.at_edges(grid, refs[6:6 + nci], refs[8 + nci:8 + nci + nco], refs[-2:])
            begin()

        @pl.when(pl.program_id(1) == 0)
        def _():
            st_ref[...] = jnp.zeros_like(st_ref)

        for k in range(gp):
            q = _ssd_q(dt_ref[k], acs_ref[k], acsr_ref[k])
            xdt = xs_ref[:, k * gw:(k + 1) * gw] * q["dt_exp"]
            bm = b_ref[:, k * ds:(k + 1) * ds].astype(BF16)
            cm = c_ref[:, k * ds:(k + 1) * ds].astype(BF16)
            cb = _dot(cm, bm, _NT)
            st = st_ref[k]
            prev_ref[0, k] = st
            lm, _ = _ssd_decay_stacks(q)
            y_diag = _pick_heads(_dot((_stack4(cb) * lm).astype(BF16), xdt.astype(BF16)), q["hm"])
            y_ref[:, k * gw:(k + 1) * gw] = y_diag + _dot(cm, st.astype(BF16)) * q["ea"]
            st_ref[k] = q["dec"] * st + _dot(bm, (xdt * q["ds"]).astype(BF16), _TN)
        if carry is not None:
            end()

    res = pl.pallas_call(
        body, name="ssd_fwd", grid=grid,
        in_specs=_ssd_specs(lambda c: c, gp) + c_in,
        out_specs=[pl.BlockSpec((CHUNK, gp * gw), lambda g, c: (c, g)),
                   pl.BlockSpec((1, gp, ds, gw), lambda g, c: (c, g, 0, 0))] + c_out,
        out_shape=[jax.ShapeDtypeStruct((t, D_INNER), F32), jax.ShapeDtypeStruct((nc, N_GROUPS, ds, gw), F32)]
        + (carry.outs if carry is not None else []),
        scratch_shapes=[pltpu.VMEM((gp, ds, gw), F32)] + c_sem,
        input_output_aliases=carry.aliases(6, 2) if carry is not None else {},
        compiler_params=_cparams(("arbitrary", "arbitrary") if carry is not None else ("parallel", "arbitrary")),
    )(xa, xa, xa, dec["dt"], dec["acs"], dec["acs_rows"], *(carry.ins if carry is not None else []))
    return res[0], res[1], res[2:]


def _ssd_bwd(dy, dsk, xa, prev, dec, carry=None):
    t = xa.shape[0]
    nc = t // CHUNK
    gp, gw, dstate = SSD_GP_BWD, GROUP_W, D_STATE
    hpg = HEADS_PER_GROUP

    grid = (N_GROUPS // gp, nc)
    c_in, c_out, c_sem = carry.specs() if carry is not None else ([], [], [])
    nci, nco = len(c_in), len(c_out)

    def body(*refs):
        ins, c_ins = refs[:11], refs[11:11 + nci]
        outs = refs[11 + nci:17 + nci]
        c_outs = refs[17 + nci:17 + nci + nco]
        dst_ref = refs[17 + nci + nco]
        dbias_ref, dalog_ref = outs[4], outs[5]
        if carry is not None:
            begin, end = carry.at_edges(grid, c_ins, c_outs, refs[-2:])
            begin()

        @pl.when(pl.program_id(1) == 0)
        def _():
            dst_ref[...] = jnp.zeros_like(dst_ref)
            dbias_ref[...] = jnp.zeros_like(dbias_ref)
            dalog_ref[...] = jnp.zeros_like(dalog_ref)

        for k in range(gp):
            one_group(k, *ins, *outs, dst_ref)
        if carry is not None:
            end()

    def one_group(k, xs_ref, b_ref, c_ref, dt_ref, acs_ref, acsr_ref, sig_ref, al_ref, dy_ref, sk_ref, prev_ref,
                  dxs_ref, db_ref, dc_ref, ddt_ref, dbias_ref, dalog_ref, dst_ref):
        wide = slice(k * gw, (k + 1) * gw)
        narrow = slice(k * dstate, (k + 1) * dstate)
        q = _ssd_q(dt_ref[k], acs_ref[k], acsr_ref[k])
        a_r = -jnp.exp(al_ref[k])
        hm, ds, dec, dt_exp = q["hm"], q["ds"], q["dec"], q["dt_exp"]
        xs = xs_ref[:, wide]
        xdt = xs * dt_exp
        xdtb = xdt.astype(BF16)
        bm = b_ref[:, narrow].astype(BF16)
        cm = c_ref[:, narrow].astype(BF16)
        cb = _dot(cm, bm, _NT)
        bc = _dot(bm, cm, _NT)
        dyv = dy_ref[:, wide]
        dye = (dyv * q["ea"]).astype(BF16)
        pst = prev_ref[0, k]
        dst = dst_ref[k]
        dstb = dst.astype(BF16)
        dx_state = ds * _dot(bm, dstb)
        onehot = [(lax.broadcasted_iota(jnp.int32, (1, hpg), 1) == r).astype(F32) for r in range(hpg)]
        lm, lt = _ssd_decay_stacks(q)
        dyb = dyv.astype(BF16)
        gl = _dot(_mask_heads(dyv, hm).astype(BF16), xdtb, _NT) * lm
        glt = _dot(_mask_heads(xdt, hm).astype(BF16), dyb, _NT) * lt
        bc4 = _stack4(bc)
        dxdt = dx_state + _pick_heads(_dot((bc4 * lt).astype(BF16), dyb), hm)
        wd = jnp.sum(gl * _stack4(cb), axis=1, keepdims=True) - jnp.sum(glt * bc4, axis=1, keepdims=True)
        dcb = gl[:CHUNK]
        dcbt = glt[:CHUNK]
        qa = wd[:CHUNK] * onehot[0]
        for r in range(1, hpg):
            dcb = dcb + gl[r * CHUNK:(r + 1) * CHUNK]
            dcbt = dcbt + glt[r * CHUNK:(r + 1) * CHUNK]
            qa = qa + wd[r * CHUNK:(r + 1) * CHUNK] * onehot[r]
        pstb = pst.astype(BF16)
        dc_ref[:, narrow] = _dot(dcb.astype(BF16), bm) + _dot(dye, pstb, _NT)
        db_ref[:, narrow] = _dot(dcbt.astype(BF16), cm) + _dot((xdt * ds).astype(BF16), dstb, _NT)
        dst_ref[k] = dec * dst + _dot(cm, dye, _TN)
        dxs_ref[:, wide] = dxdt * dt_exp + dyv * sk_ref[:, wide]

        t2 = xdt * dx_state
        t1 = dyv * (_dot(cm, pstb) * q["ea"]) - t2
        t4 = dxdt * xs
        last_row = jnp.sum(t2, axis=0, keepdims=True) + dec * jnp.sum(dst * pst, axis=0, keepdims=True)
        xd = jnp.zeros((CHUNK, hpg), F32)
        dal = jnp.zeros((1, hpg), F32)
        for r in range(hpg):
            qa = qa + jnp.sum(jnp.where(hm[r], t1, 0.0), axis=1, keepdims=True) * onehot[r]
            xd = xd + jnp.sum(jnp.where(hm[r], t4, 0.0), axis=1, keepdims=True) * onehot[r]
            dal = dal + jnp.sum(jnp.where(hm[r], last_row, 0.0), axis=1, keepdims=True) * onehot[r]
        rc = _cumsum(qa, 0, reverse=True) + dal
        rowid = (nc - 1 - pl.program_id(1)) * CHUNK + lax.broadcasted_iota(jnp.int32, (CHUNK, 1), 0)
        ddt_raw = jnp.where(rowid >= N_PAD, (rc * a_r + xd) * sig_ref[k], 0.0)
        ddt_ref[k] = ddt_raw
        dbias_ref[k] += jnp.sum(ddt_raw, axis=0, keepdims=True)
        dalog_ref[k] += jnp.sum(rc * q["dtc"], axis=0, keepdims=True) * a_r

    rev = lambda c: nc - 1 - c
    blk = pl.BlockSpec((CHUNK, gp * gw), lambda g, c: (rev(c), g))
    nblk = pl.BlockSpec((CHUNK, gp * dstate), lambda g, c: (rev(c), g))
    cols = pl.BlockSpec((gp, CHUNK, hpg), lambda g, c: (g, rev(c), 0))
    small = pl.BlockSpec((gp, 1, hpg), lambda g, c: (g, 0, 0))
    res = pl.pallas_call(
        body, name="ssd_bwd", grid=grid,
        in_specs=_ssd_specs(rev, gp) + [cols, small, blk, pl.BlockSpec((1, gp * gw), lambda g, c: (0, g)),
                                    pl.BlockSpec((1, gp, dstate, gw), lambda g, c: (rev(c), g, 0, 0))]
        + c_in,
        out_specs=[blk, nblk, nblk, cols, small, small] + c_out,
        out_shape=[jax.ShapeDtypeStruct((t, D_INNER), F32), jax.ShapeDtypeStruct((t, N_GROUPS * dstate), F32),
                   jax.ShapeDtypeStruct((t, N_GROUPS * dstate), F32), jax.ShapeDtypeStruct((N_GROUPS, t, hpg), F32),
                   jax.ShapeDtypeStruct((N_GROUPS, 1, hpg), F32), jax.ShapeDtypeStruct((N_GROUPS, 1, hpg), F32)]
        + (carry.outs if carry is not None else []),
        scratch_shapes=[pltpu.VMEM((gp, dstate, gw), F32)] + c_sem,
        compiler_params=_cparams(("arbitrary", "arbitrary") if carry is not None else ("parallel", "arbitrary")),
    )(xa, xa, xa, dec["dt"], dec["acs"], dec["acs_rows"], dec["sig"], dec["a_log"], dy, dsk.reshape(1, D_INNER), prev,
      *(carry.ins if carry is not None else []))
    dxs, db, dc, ddt, dbias, dalog = res[:6]
    ddt_raw = ddt.transpose(1, 0, 2).reshape(t, N_HEADS)
    return dxs, db, dc, ddt_raw, dbias.reshape(N_HEADS), dalog.reshape(N_HEADS), res[6:]


def _ssd_post_fwd(y, xa, proj, dsk, nw):
    t = y.shape[0]
    di = D_INNER
    tm = _tile(t, (384, 128))

    def body(y_ref, xs_ref, z_ref, dsk_ref, nw_ref, o_ref):
        z = z_ref[...]
        yz = (y_ref[...] + xs_ref[...] * dsk_ref[...]) * (z * _sigmoid(z))
        nwv = nw_ref[...]
        for g in range(N_GROUPS):
            sl = slice(g * GROUP_W, (g + 1) * GROUP_W)
            v = yz[:, sl]
            rg = lax.rsqrt(jnp.mean(v * v, axis=-1, keepdims=True) + EPS)
            o_ref[:, sl] = (v * rg * nwv[:, sl]).astype(BF16)

    row = pl.BlockSpec((tm, di), lambda i: (i, 0))
    vec = pl.BlockSpec((1, di), lambda i: (0, 0))
    return pl.pallas_call(
        body, name="ssd_post_fwd", grid=(t // tm,),
        in_specs=[row, row, row, vec, vec], out_specs=row, out_shape=jax.ShapeDtypeStruct((t, di), BF16),
        compiler_params=_cparams(("parallel",)),
    )(y, xa, proj, dsk.reshape(1, di), nw.reshape(1, di))


def _ssd_post_bwd(dyn, y, xa, proj, dsk, nw, dproj):
    t = y.shape[0]
    di = D_INNER
    tm = CHUNK

    def body(dyn_ref, y_ref, xs_ref, z_ref, dsk_ref, nw_ref, _, dy_ref, dz_ref, dnw_ref, ddsk_ref):
        @pl.when(pl.program_id(0) == 0)
        def _():
            dnw_ref[...] = jnp.zeros_like(dnw_ref)
            ddsk_ref[...] = jnp.zeros_like(ddsk_ref)

        for g in range(N_GROUPS):
            sl = slice(g * GROUP_W, (g + 1) * GROUP_W)
            z = z_ref[:, sl]
            sig = _sigmoid(z)
            s = z * sig
            xs = xs_ref[:, sl]
            dskv = dsk_ref[:, sl]
            yt = y_ref[:, sl] + xs * dskv
            yz = yt * s
            rg = lax.rsqrt(jnp.mean(yz * yz, axis=-1, keepdims=True) + EPS)
            xhat = yz * rg
            dynv = dyn_ref[:, sl]
            gg = dynv * nw_ref[:, sl]
            dnw_ref[:, sl] += jnp.sum(dynv * xhat, axis=0, keepdims=True)
            dyz = rg * (gg - xhat * jnp.mean(gg * xhat, axis=-1, keepdims=True))
            dyt = dyz * s
            dy_ref[:, sl] = dyt
            dz_ref[:, sl] = (dyz * yt * (sig * (1.0 + z * (1.0 - sig)))).astype(BF16)
            ddsk_ref[:, sl] += jnp.sum(dyt * xs, axis=0, keepdims=True)

    row = pl.BlockSpec((tm, di), lambda i: (i, 0))
    vec = pl.BlockSpec((1, di), lambda i: (0, 0))
    dy, dz, dnw, ddsk = pl.pallas_call(
        body, name="ssd_post_bwd", grid=(t // tm,),
        in_specs=[row, row, row, row, vec, vec, pl.BlockSpec(memory_space=pl.ANY)], out_specs=[row, row, vec, vec],
        out_shape=[jax.ShapeDtypeStruct((t, di), F32), jax.ShapeDtypeStruct(dproj.shape, BF16),
                   jax.ShapeDtypeStruct((1, di), F32), jax.ShapeDtypeStruct((1, di), F32)],
        input_output_aliases={6: 1},
        compiler_params=_cparams(("arbitrary",)),
    )(dyn, y, xa, proj, dsk.reshape(1, di), nw.reshape(1, di), dproj)
    return dy, dz, dnw.reshape(di), ddsk.reshape(N_HEADS, HEAD_DIM).sum(axis=1)


SHARD_COLS = IN_COLS // N_CHIPS


def _split_w_in(w_in_sh):
    nl = w_in_sh.shape[0]
    w = w_in_sh.transpose(0, 2, 1, 3).reshape(nl, D_MODEL, IN_COLS)
    main = jnp.concatenate([w[..., OFF_Z:OFF_XBC], w[..., OFF_GATE:], w[..., :OFF_Z], w[..., OFF_XBC:OFF_DT]], axis=-1)
    dt = jnp.pad(w[..., OFF_DT:OFF_GATE], ((0, 0), (0, 0), (0, DT_PAD - N_HEADS)))
    return main, dt


def _merge_dw_in(dmain, ddt):
    full = jnp.concatenate([dmain[:, PPOOL:PXBC], dmain[:, PZ:PGATE], dmain[:, PXBC:], ddt[:, :N_HEADS], dmain[:, PGATE:PPOOL]],
                           axis=1)
    return full.reshape(D_MODEL, N_CHIPS, SHARD_COLS).transpose(1, 0, 2)


def _relu2_epilogue(acc):
    hid = jnp.maximum(acc, 0.0)
    return acc, hid * hid


def _relu2_bwd_epilogue(acc, pre):
    return (acc * 2.0 * jnp.maximum(pre, 0.0),)


def _add_epilogue(acc, res):
    return (acc + res,)


def _norm_bwd_epilogue(acc, dres, h, w):
    r = lax.rsqrt(jnp.mean(h * h, axis=-1, keepdims=True) + EPS)
    xhat = h * r
    g = acc * w
    dh = dres + r * (g - xhat * jnp.mean(g * xhat, axis=-1, keepdims=True))
    return dh, jnp.sum(acc * xhat, axis=0, keepdims=True)


def _add_norm_bwd_epilogue(acc, part, dres, h, w):
    return _norm_bwd_epilogue(acc + part, dres, h, w)


def _add_norm_epilogue(acc, res, w):
    x = acc + res
    r = lax.rsqrt(jnp.mean(x * x, axis=-1, keepdims=True) + EPS)
    return x, x * r * w


def _w_spec(rows, cols, fn):
    return pl.BlockSpec((None, None, rows, cols), fn)


def _with_carry(stage, call):
    carry = stage[0]() if stage is not None else None
    res = call(carry)
    if carry is None:
        return res
    outs, got = res
    stage[1](got)
    return outs[0] if len(outs) == 1 else outs


def _layer_fwd(h, u, w, big, layer, stages=None, next_norm_w=None):
    t = h.shape[0]
    d = D_MODEL
    tm = _tile(t, (1408, 384, 128))
    nt = t // tm
    row = _bs((tm, d), lambda i, j, k: (i, 0))
    vec = _bs((1, d), lambda i, j, k: (0, 0))
    stages = stages or {}
    s = {"h": h}
    if u is None:
        u = _norm_fwd(h, w["mix_norm_w"], "norm_mix")
    proj = _with_carry(stages.get("a_ici"), lambda carry: _mm(
        f"mm_proj_{layer}", _NN, (nt, PCOLS // d, 1), u, row,
        big["w_in_main"], _bs((None, d, d), lambda i, j, k: (0, 0, j)),
        [((t, PCOLS), F32, _bs((tm, d), lambda i, j, k: (i, j)))], carry=carry))
    dt_raw = _mm(f"mm_dt_{layer}", _NN, (nt, 1, 1), u, row,
                 big["w_in_dt"], _bs((None, d, DT_PAD), lambda i, j, k: (0, 0, 0)),
                 [((t, DT_PAD), F32, _bs((tm, DT_PAD), lambda i, j, k: (i, 0)))])
    pooled, yg, ypm, y_pool = _pool_fwd(proj, big["pool_w_group"], w["pool_scale"], big["w_pool_up"], layer)
    xa, conv_pre = _conv_fwd(proj, w["conv_w"], w["conv_b"])
    dec = _ssd_decay(dt_raw, w["dt_bias"], w["a_log"])
    stage = stages.get("b_ici")
    carry = stage[0]() if stage is not None else None
    y, prev, got = _ssd_fwd(xa, dec, carry)
    if carry is not None:
        stage[1](got)
    yn = _ssd_post_fwd(y, xa, proj, w["d_skip_exp"], w["ssd_norm_w"])
    rs = D_INNER // N_CHIPS
    y_ssd = _with_carry(stages.get("a_d2d"), lambda carry: _mm(
        f"mm_ssd_out_{layer}", _NN, (nt, 1, 1), yn, _bs((tm, D_INNER), lambda i, j, k: (i, 0)),
        big["w_ssd_out"], _bs((None, N_CHIPS, rs, d), lambda i, j, k: (0, 0, 0, 0)), [((t, d), F32, row)], carry=carry))
    merged = _gate_fwd(proj, w["b_gate"], y_pool, y_ssd)
    ro = d // N_CHIPS
    h1, v = _mm(f"mm_o_{layer}", _NN, (nt, 1, 1), merged, row,
                big["w_o"], _bs((None, N_CHIPS, ro, d), lambda i, j, k: (0, 0, 0, 0)),
                [((t, d), F32, row), ((t, d), BF16, row)],
                extras=[(h, row), (w["mlp_norm_w"].reshape(1, d), vec)], epilogue=_add_norm_epilogue)
    tile = _bs((tm, d), lambda i, j, k: (i, j))
    pre, act = _with_carry(stages.get("b_d2d"), lambda carry: _mm(
        f"mm_ff1_{layer}", _NN, (nt, N_CHIPS, 1), v, row,
        big["w_ff1"], _w_spec(d, d, lambda i, j, k: (0, j, 0, 0)),
        [((t, D_FF), F32, tile), ((t, D_FF), BF16, tile)], epilogue=_relu2_epilogue, carry=carry))
    a_spec = _bs((tm, d), lambda i, j, k: (i, k))
    b_spec = _w_spec(d, d, lambda i, j, k: (0, k, 0, 0))
    if next_norm_w is None:
        h2 = _mm(f"mm_ff2_{layer}", _NN, (nt, 1, N_CHIPS), act, a_spec, big["w_ff2"], b_spec, [((t, d), F32, row)],
                 extras=[(h1, row)], epilogue=_add_epilogue)
        u_next = None
    else:
        h2, u_next = _mm(f"mm_ff2_{layer}", _NN, (nt, 1, N_CHIPS), act, a_spec, big["w_ff2"], b_spec,
                         [((t, d), F32, row), ((t, d), BF16, row)],
                         extras=[(h1, row), (next_norm_w.reshape(1, d), vec)], epilogue=_add_norm_epilogue)
    s.update(u=u, proj=proj, dec=dec, conv_pre=conv_pre, pooled=pooled, yg=yg, ypm=ypm, y_pool=y_pool, xa=xa, y=y, prev=prev, yn=yn,
             y_ssd=y_ssd, merged=merged, h1=h1, v=v, pre=pre, act=act)
    return h2, u_next, s


def _dw(name, layer, a, b, shard_shape, by, tk, carry=None):
    t = a.shape[0]
    nk = t // tk
    rows, cols = shard_shape
    if by == "rows":
        grid = (N_CHIPS, 1, nk)
        a_spec = _bs((tk, rows), lambda i, j, k: (k, i))
        b_spec = _bs((tk, cols), lambda i, j, k: (k, 0))
        o_spec = _bs((None, rows, cols), lambda i, j, k: (i, 0, 0))
    else:
        grid = (1, N_CHIPS, nk)
        a_spec = _bs((tk, rows), lambda i, j, k: (k, 0))
        b_spec = _bs((tk, cols), lambda i, j, k: (k, j))
        o_spec = _bs((None, rows, cols), lambda i, j, k: (j, 0, 0))
    return _mm(f"{name}_{layer}", _TN, grid, a, a_spec, b, b_spec, [((N_CHIPS, rows, cols), F32, o_spec)], carry=carry)


EARLY = ("w_ff2", "w_ff1", "w_o", "w_ssd_out", "w_pool_up", "pool_w_group")


def _layer_bwd(dh, s, w, big, layer, red=None, last=False):
    t = dh.shape[0]
    d = D_MODEL
    tm = _tile(t, (1408, 384, 128))
    tk = _tile(t, (1408, 384, 128))
    nt = t // tm
    row = _bs((tm, d), lambda i, j, k: (i, 0))
    tile = _bs((tm, d), lambda i, j, k: (i, j))
    g, gb = {}, {}
    carry = red.swap_carry() if red is not None else None
    dpre = _mm(f"mm_dact_{layer}", _NT, (nt, N_CHIPS, 1), dh, row,
               big["w_ff2"], _w_spec(d, d, lambda i, j, k: (0, j, 0, 0)), [((t, D_FF), BF16, tile)],
               extras=[(s["pre"], tile)], epilogue=_relu2_bwd_epilogue, carry=carry)
    if carry is not None:
        (dpre,), got = dpre
        red.after_swap(got)
    gb["w_ff2"] = _dw("mm_dw_ff2", layer, s["act"], dh, (d, d), "rows", tk)
    vec = _bs((1, d), lambda i, j, k: (0, 0))
    th = _tile(t, (704, 384, 128))
    half_row = _bs((th, d), lambda i, j, k: (i, 0))
    dh1, dnw = _mm(f"mm_dv_{layer}", _NT, (t // th, 1, N_CHIPS), dpre, _bs((th, d), lambda i, j, k: (i, k)),
                   big["w_ff1"], _w_spec(d, d, lambda i, j, k: (0, k, 0, 0)), [((t, d), F32, half_row), ((1, d), F32, vec)],
                   extras=[(dh, half_row), (s["h1"], half_row), (w["mlp_norm_w"].reshape(1, d), vec)],
                   epilogue=_norm_bwd_epilogue, summed=(1,))
    g["mlp_norm_w"] = dnw.reshape(d)
    gb["w_ff1"] = _dw("mm_dw_ff1", layer, s["v"], dpre, (d, d), "cols", t)
    ro = d // N_CHIPS
    dmerged = _mm(f"mm_dmerged_{layer}", _NT, (nt, N_CHIPS, 1), dh1, row,
                  big["w_o"], _w_spec(ro, d, lambda i, j, k: (0, j, 0, 0)),
                  [((t, d), F32, _bs((tm, ro), lambda i, j, k: (i, j)))])
    gb["w_o"] = _dw("mm_dw_o", layer, s["merged"], dh1, (ro, d), "rows", tk)
    dy_pool, dy_ssd, dproj, g["b_gate"] = _gate_bwd(dmerged, s["proj"], w["b_gate"], s["y_pool"], s["y_ssd"])
    rs = D_INNER // N_CHIPS
    dyn = _mm(f"mm_dyn_{layer}", _NT, (nt, N_CHIPS, 1), dy_ssd, row,
              big["w_ssd_out"], _w_spec(rs, d, lambda i, j, k: (0, j, 0, 0)),
              [((t, D_INNER), F32, _bs((tm, rs), lambda i, j, k: (i, j)))])
    gb["w_ssd_out"] = _dw("mm_dw_ssd_out", layer, s["yn"], dy_ssd, (rs, d), "rows", tk)
    dy, dproj, g["ssd_norm_w"], g["d_skip"] = _ssd_post_bwd(dyn, s["y"], s["xa"], s["proj"], w["d_skip_exp"],
                                                            w["ssd_norm_w"], dproj)
    carry = red.ici_carry() if red is not None else None
    dxs, db, dc, ddt_raw, g["dt_bias"], g["a_log"], got = _ssd_bwd(dy, w["d_skip_exp"], s["xa"], s["prev"], s["dec"], carry)
    if carry is not None:
        red.after_ici(got)
    own = red is not None and last
    dproj, dcw1, dcb1 = _conv_bwd(dxs, 0, s["proj"], s["conv_pre"], w["conv_w"], dproj)
    dproj, dcw2, dcb2 = _conv_bwd(db, 2, s["proj"], s["conv_pre"], w["conv_w"], dproj)
    dproj, dcw3, dcb3 = _conv_bwd(dc, 3, s["proj"], s["conv_pre"], w["conv_w"], dproj)
    g["conv_w"] = jnp.concatenate([dcw1, dcw2, dcw3], axis=1)
    g["conv_b"] = jnp.concatenate([dcb1, dcb2, dcb3])
    q, dyg, g["pool_scale"] = _pool_bwd_a(dy_pool, s["yg"], w["pool_scale"], big["pool_w_group"], big["w_pool_up"], layer)
    gb["w_pool_up"] = _dw("mm_dw_pool_up", layer, s["ypm"], dy_pool, (ro, d), "rows", tk)
    gb["pool_w_group"] = _pool_dwg(s["pooled"], dyg, layer).reshape(N_CHIPS, POOL_GDIM, POOL_GDIM)
    dproj = _pool_bwd_b(q, dproj)
    if own:
        red.push(layer, {n: gb[n] for n in EARLY})
    ddt = jnp.pad(ddt_raw.astype(BF16), ((0, 0), (0, DT_PAD - N_HEADS)))
    nk = PCOLS // d
    kw = PCOLS // 4
    du = _with_carry((red.swap_carry, red.after_swap) if own else None, lambda carry: _mm(
        f"mm_du_{layer}", _NT, (nt, 1, PCOLS // kw), dproj, _bs((tm, kw), lambda i, j, k: (i, k)),
        big["w_in_main"], _bs((None, d, kw), lambda i, j, k: (0, 0, k)), [((t, d), F32, row)], carry=carry))
    dh0, dnw = _mm(f"mm_du_dt_{layer}", _NT, (t // th, 1, 1), ddt, _bs((th, DT_PAD), lambda i, j, k: (i, 0)),
                   big["w_in_dt"], _bs((None, d, DT_PAD), lambda i, j, k: (0, 0, 0)),
                   [((t, d), F32, half_row), ((1, d), F32, vec)],
                   extras=[(du, half_row), (dh1, half_row), (s["h"], half_row), (w["mix_norm_w"].reshape(1, d), vec)],
                   epilogue=_add_norm_bwd_epilogue, summed=(1,))
    g["mix_norm_w"] = dnw.reshape(d)
    ntk = t // tk
    u_spec = _bs((tk, d), lambda i, j, k: (k, 0))
    dmain = _with_carry((red.ici_carry, red.after_ici) if own else None, lambda carry: _mm(
        f"mm_dw_in_{layer}", _TN, (1, nk, 1), s["u"], _bs((t, d), lambda i, j, k: (0, 0)), dproj,
        _bs((t, d), lambda i, j, k: (0, j)), [((d, PCOLS), F32, _bs((d, d), lambda i, j, k: (0, j)))], carry=carry))
    ddtw = _mm(f"mm_dw_dt_{layer}", _TN, (1, 1, ntk), s["u"], u_spec, ddt, _bs((tk, DT_PAD), lambda i, j, k: (k, 0)),
               [((d, DT_PAD), F32, _bs((d, DT_PAD), lambda i, j, k: (0, 0)))])
    gb["w_in"] = _merge_dw_in(dmain, ddtw)
    if red is not None:
        red.push(layer, {n: gb[n] for n in BIG if not (own and n in EARLY)})
    return dh0, g, gb


BIG = ("w_in", "pool_w_group", "w_pool_up", "w_ssd_out", "w_o", "w_ff1", "w_ff2")
PER_LAYER = ("mix_norm_w", "w_in", "b_gate", "pool_w_group", "pool_scale", "w_pool_up", "conv_w", "conv_b", "dt_bias",
             "a_log", "d_skip", "ssd_norm_w", "w_ssd_out", "w_o", "mlp_norm_w", "w_ff1", "w_ff2")


SMALL_PER_LAYER = tuple(n for n in PER_LAYER if n not in BIG)


SHARD_SHAPE = {"w_in": (1024, SHARD_COLS), "pool_w_group": (4, WG_ROWS, POOL_GDIM), "w_pool_up": (WUP_ROWS, D_MODEL),
               "w_ssd_out": (D_INNER // N_CHIPS, D_MODEL), "w_o": (D_MODEL // N_CHIPS, D_MODEL),
               "w_ff1": (D_MODEL, D_FF // N_CHIPS), "w_ff2": (D_FF // N_CHIPS, D_MODEL)}


def _layer_view(mats):
    big = {n: mats[n].reshape((1, N_CHIPS) + SHARD_SHAPE[n]) for n in BIG if n != "w_in"}
    big["w_in_main"], big["w_in_dt"] = _split_w_in(mats["w_in"])
    return big


def _local_step(x, tgt, weights, small, red=None):
    seq = x.shape[0]
    h = jnp.concatenate([jnp.zeros((N_PAD, D_MODEL), F32), small["meta_tokens"], x], axis=0)
    saved, ws, bigs = [], [], []
    u = None
    for i in range(DEPTH):
        w = {n: small[n][i] for n in SMALL_PER_LAYER}
        w["d_skip_exp"] = jnp.repeat(w["d_skip"], HEAD_DIM)
        big = weights.layer(i)
        next_norm_w = small["mix_norm_w"][i + 1] if i + 1 < DEPTH else None
        h, u, s = _layer_fwd(h, u, w, big, i, weights.stages(i), next_norm_w)
        saved.append(s)
        ws.append(w)
        bigs.append(big)
    loss, dh, g_final = _loss_head(h, small["final_norm_w"], tgt)
    layer_g, layer_gb = [None] * DEPTH, [None] * DEPTH
    for i in reversed(range(DEPTH)):
        dh, layer_g[i], layer_gb[i] = _layer_bwd(dh, saved[i], ws[i], bigs[i], i, red, last=(i == 0))
    grads = {n: jnp.stack([layer_g[i][n] for i in range(DEPTH)]) for n in SMALL_PER_LAYER}
    grads["final_norm_w"] = g_final
    grads["meta_tokens"] = dh[N_PAD:ROW_X]
    return loss, dh[ROW_X:ROW_X + seq], layer_gb, grads


MESH = pl.DeviceIdType.MESH
LANES = 128
ANY = pl.BlockSpec(memory_space=pl.ANY)


def _place():
    x, y, c = lax.axis_index("x"), lax.axis_index("y"), lax.axis_index("c")
    chips = [(1 - x, y), (x, 1 - y), (1 - x, 1 - y)]
    return x, y, c, chips


def _remote(src, dst, send_sem, recv_sem, to):
    return pltpu.make_async_remote_copy(src_ref=src, dst_ref=dst, send_sem=send_sem, recv_sem=recv_sem,
                                        device_id=to, device_id_type=MESH)


class _WeightGatherer:
    GROUPS = {"a": ("w_in",), "b": tuple(n for n in BIG if n != "w_in")}

    def __init__(self, mine):
        self.mine = mine
        self.landing = {}
        self.ready = {}

    def _ici(self, layer, group):
        names = self.GROUPS[group]
        srcs = [self.mine[n] for n in names]
        outs = [jax.ShapeDtypeStruct((1, N_CHIPS) + s.shape[1:], BF16) for s in srcs]

        def make(ins, out, send_sems, recv_sems):
            x, y, c, chips = _place()
            me = 2 * x + y
            cps = []
            for p, ref in enumerate(ins):
                half = ref.shape[1] // 2
                rows = pl.ds(c * half, half)
                cps += [_remote(ref.at[layer, rows, :], out[p].at[0, me, rows, :], send_sems.at[3 * p + j],
                                recv_sems.at[3 * p + j], (*chip, c)) for j, chip in enumerate(chips)]
            return cps

        return _Carry(srcs, outs, 3 * len(names), make)

    def _d2d(self, layer, group):
        names = self.GROUPS[group]
        n = len(names)
        bufs = self.landing[(layer, group)]
        outs = [jax.ShapeDtypeStruct(b.shape, b.dtype) for b in bufs]

        def make(ins, out, send_sems, recv_sems):
            x, y, c, chips = _place()
            me = 2 * x + y
            sibling = (x, y, 1 - c)
            cps = []
            for p in range(n):
                half = out[p].shape[2] // 2
                for j, (cx, cy) in enumerate(chips):
                    blk = out[p].at[0, 2 * cx + cy, pl.ds(c * half, half), :]
                    cps.append(_remote(blk, blk, send_sems.at[4 * p + j], recv_sems.at[4 * p + j], sibling))
                cps.append(_remote(ins[n + p].at[layer], out[p].at[0, me], send_sems.at[4 * p + 3],
                                   recv_sems.at[4 * p + 3], sibling))
            return cps

        return _Carry(list(bufs) + [self.mine[m] for m in names], outs, 4 * n, make, alias={p: p for p in range(n)})

    def _landed(self, layer, group, bufs):
        self.landing[(layer, group)] = bufs

    def _done(self, layer, group, bufs):
        self.ready.setdefault(layer, {}).update(zip(self.GROUPS[group], bufs))

    def layer(self, i):
        if i == 0:
            for g in self.GROUPS:
                self._landed(0, g, _run_carry(self._ici(0, g), f"gather_ici_{g}_0"))
            for g in self.GROUPS:
                self._done(0, g, _run_carry(self._d2d(0, g), f"gather_d2d_{g}_0"))
        return _layer_view(self.ready[i])

    def stages(self, i):
        nxt = i + 1
        if nxt == DEPTH:
            return None
        st = {}
        for g in self.GROUPS:
            st[f"{g}_ici"] = (functools.partial(self._ici, nxt, g), functools.partial(self._landed, nxt, g))
            st[f"{g}_d2d"] = (functools.partial(self._d2d, nxt, g), functools.partial(self._done, nxt, g))
        return st


def _exchange_small(v, reduce, name):
    rows_per = v.shape[0]
    vm = pl.BlockSpec(memory_space=pltpu.VMEM)

    def body(v_ref, out_ref, *scratch):
        if reduce:
            land_ref, send_sems, recv_sems, local_sem = scratch
        else:
            land_ref = out_ref
            send_sems, recv_sems, local_sem = scratch
        x, y, c, chips = _place()
        me, sibling = (x, y, c), (x, y, 1 - c)

        def rows(px, py, pc):
            return land_ref.at[4 * px + 2 * py + pc]

        def copy(k, block, to, src=None):
            return _remote(rows(*block) if src is None else src, rows(*block), send_sems.at[k], recv_sems.at[k], to)

        mine = pltpu.make_async_copy(v_ref, rows(*me), local_sem)
        mine.start()
        first = [copy(0, me, sibling, src=v_ref)]
        first += [copy(1 + j, me, (*chip, c), src=v_ref) for j, chip in enumerate(chips)]
        for cp in first:
            cp.start()
        passed = [copy(4 + j, (*chip, c), sibling) for j, chip in enumerate(chips)]
        for j, chip in enumerate(chips):
            copy(1 + j, (*chip, c), me).wait_recv()
            passed[j].start()
        copy(0, sibling, me).wait_recv()
        for j, chip in enumerate(chips):
            copy(4 + j, (*chip, 1 - c), me).wait_recv()
        for cp in first + passed:
            cp.wait_send()
        mine.wait()
        if reduce:
            acc = land_ref[0]
            for d in range(1, 8):
                acc = acc + land_ref[d]
            out_ref[...] = acc

    sems = [pltpu.SemaphoreType.DMA((7,)), pltpu.SemaphoreType.DMA((7,)), pltpu.SemaphoreType.DMA]
    if reduce:
        out_shape = jax.ShapeDtypeStruct((rows_per, LANES), F32)
        scratch = [pltpu.VMEM((8, rows_per, LANES), F32)] + sems
    else:
        out_shape = jax.ShapeDtypeStruct((8, rows_per, LANES), F32)
        scratch = sems
    return pl.pallas_call(
        body, name=name, in_specs=[vm], out_specs=vm, out_shape=out_shape, scratch_shapes=scratch,
        compiler_params=pltpu.CompilerParams(vmem_limit_bytes=VMEM_LIMIT),
    )(v)


def _run_carry(carry, name):
    c_in, c_out, c_sem = carry.specs()
    n_in, n_out = len(c_in), len(c_out)

    def body(*refs):
        cps = carry.make(refs[:n_in], refs[n_in:n_in + n_out], *refs[n_in + n_out:])
        for cp in cps:
            cp.start()
        for cp in cps:
            cp.wait()

    return pl.pallas_call(body, name=name, in_specs=c_in, out_specs=c_out, out_shape=carry.outs,
                          scratch_shapes=c_sem, input_output_aliases=carry.aliases(0, 0))(*carry.ins)


def _row_tile(rows, last, itemsize=4, budget=2 * 1024 * 1024):
    return _tile(rows, tuple(t for t in (2048, 1024, 512, 256, 128, 64, 32, 16) if t * last * itemsize <= budget))


def _rs_add(g, got, core, name):
    _, half, last = got.shape
    tr = _row_tile(half, last)
    nb = half // tr

    def body(c_ref, g_ref, got_ref, o_ref):
        o_ref[...] = (g_ref[...] + got_ref[...]).astype(BF16)

    blk = pl.BlockSpec((None, tr, last), lambda s, i, c_ref: (s, i, 0))
    return pl.pallas_call(
        body, name=name,
        grid_spec=pltpu.PrefetchScalarGridSpec(
            num_scalar_prefetch=1, grid=(N_CHIPS, nb),
            in_specs=[pl.BlockSpec((None, tr, last), lambda s, i, c_ref: (s, c_ref[0] * nb + i, 0)), blk],
            out_specs=blk),
        out_shape=jax.ShapeDtypeStruct(got.shape, BF16),
        compiler_params=_cparams(("parallel", "parallel")),
    )(core.reshape(1).astype(jnp.int32), g, got)


def _rs_sum(own, got, acc, layer, chip, core, name):
    _, half, last = own.shape
    tr = _row_tile(half, last, budget=1024 * 1024)
    nb = half // tr
    ni = 0 if acc is None else 1

    def body(k_ref, c_ref, own_ref, got_ref, *rest):
        o_ref = rest[-1]
        tot = own_ref[...].astype(F32)
        for j in range(3):
            tot = tot + got_ref[j].astype(F32)
        o_ref[...] = tot

    return pl.pallas_call(
        body, name=name,
        grid_spec=pltpu.PrefetchScalarGridSpec(
            num_scalar_prefetch=2, grid=(nb,),
            in_specs=[pl.BlockSpec((None, tr, last), lambda i, k_ref, c_ref: (k_ref[0], i, 0)),
                      pl.BlockSpec((3, tr, last), lambda i, k_ref, c_ref: (0, i, 0))]
            + [pl.BlockSpec(memory_space=pl.ANY)] * ni,
            out_specs=pl.BlockSpec((None, tr, last), lambda i, k_ref, c_ref: (layer, c_ref[0] * nb + i, 0))),
        out_shape=jax.ShapeDtypeStruct((DEPTH, 2 * half, last), F32),
        input_output_aliases={4: 0} if ni else {},
        compiler_params=_cparams(("parallel",)),
    )(chip.reshape(1).astype(jnp.int32), core.reshape(1).astype(jnp.int32), own, got, *([acc] if ni else []))


def _rs_share(arrs):
    n = len(arrs)

    def body(*refs):
        out = refs[n:2 * n]
        send_sems, recv_sems = refs[2 * n:]
        x, y, c, _ = _place()
        sibling = (x, y, 1 - c)
        cps = []
        for p in range(n):
            half = out[p].shape[1] // 2
            mine = out[p].at[:, pl.ds(c * half, half), :]
            cps.append(_remote(mine, mine, send_sems.at[p], recv_sems.at[p], sibling))
        for cp in cps:
            cp.start()
        for p, cp in enumerate(cps):
            cp.wait_send()
            half = out[p].shape[1] // 2
            other = out[p].at[:, pl.ds((1 - c) * half, half), :]
            _remote(other, other, send_sems.at[p], recv_sems.at[p], sibling).wait_recv()

    return pl.pallas_call(
        body, name="rs_share", in_specs=[ANY] * n, out_specs=[ANY] * n,
        out_shape=[jax.ShapeDtypeStruct(a.shape, a.dtype) for a in arrs],
        input_output_aliases={p: p for p in range(n)},
        scratch_shapes=[pltpu.SemaphoreType.DMA((n,)), pltpu.SemaphoreType.DMA((n,))],
    )(*arrs)


class _GradReducer:
    def __init__(self, chip, core):
        self.chip, self.core = chip, core
        self.pending = None
        self.sums = None
        self.acc = {n: None for n in BIG}

    def push(self, layer, gb):
        assert self.pending is None
        self.pending = (layer, list(gb), list(gb.values()))

    def swap_carry(self):
        if self.pending is None:
            return None
        _, _, gs = self.pending
        outs = [jax.ShapeDtypeStruct((g.shape[0], g.shape[1] // 2, g.shape[2]), F32) for g in gs]

        def make(ins, out, send_sems, recv_sems):
            x, y, c, _ = _place()
            cps = []
            for p, ref in enumerate(ins):
                half = ref.shape[1] // 2
                cps.append(_remote(ref.at[:, pl.ds((1 - c) * half, half), :], out[p], send_sems.at[p], recv_sems.at[p],
                                   (x, y, 1 - c)))
            return cps

        return _Carry(gs, outs, len(gs), make)

    def after_swap(self, got):
        layer, names, gs = self.pending
        self.pending = None
        self.sums = (layer, names, [_rs_add(g, a, self.core, f"rs_add_{n}_{layer}") for n, g, a in zip(names, gs, got)])

    def ici_carry(self):
        if self.sums is None:
            return None
        _, _, ps = self.sums
        outs = [jax.ShapeDtypeStruct((3,) + p.shape[1:], BF16) for p in ps]

        def make(ins, out, send_sems, recv_sems):
            x, y, c, chips = _place()
            return [_remote(ins[p].at[2 * cx + cy], out[p].at[j], send_sems.at[3 * p + j], recv_sems.at[3 * p + j],
                            (cx, cy, c)) for p in range(len(ins)) for j, (cx, cy) in enumerate(chips)]

        return _Carry(ps, outs, 3 * len(ps), make)

    def after_ici(self, got):
        layer, names, ps = self.sums
        self.sums = None
        for n, p, a in zip(names, ps, got):
            self.acc[n] = _rs_sum(p, a, self.acc[n], layer, self.chip, self.core, f"rs_sum_{n}_{layer}")

    def finish(self):
        if self.pending is not None:
            self.after_swap(_run_carry(self.swap_carry(), "rs_swap_last"))
        if self.sums is not None:
            self.after_ici(_run_carry(self.ici_carry(), "rs_ici_last"))
        return dict(zip(BIG, _rs_share([self.acc[n] for n in BIG])))


def _adamw(w, g, m, v, name):
    shape = w.shape
    if len(shape) == 2:
        shape3 = (1,) + shape
    else:
        shape3 = (-1,) + shape[-2:]
    w2, g2, m2, v2 = (a.reshape(shape3) for a in (w, g, m, v))
    lead, rows, last = w2.shape
    tr = max([t for t in range(8, rows + 1, 8) if rows % t == 0 and t * last * 4 <= 2 * 1024 * 1024] or [rows])

    def body(w_ref, g_ref, m_ref, v_ref, d_ref, nm_ref, nv_ref):
        gv = g_ref[...]
        mn = ADAM_B1 * m_ref[...] + (1.0 - ADAM_B1) * gv
        vn = ADAM_B2 * v_ref[...] + (1.0 - ADAM_B2) * (gv * gv)
        m_hat = mn / (1.0 - ADAM_B1 ** ADAM_STEP)
        v_hat = vn / (1.0 - ADAM_B2 ** ADAM_STEP)
        d_ref[...] = -ADAM_LR * (m_hat / (jnp.sqrt(v_hat) + ADAM_EPS) + ADAM_WD * w_ref[...])
        nm_ref[...] = mn
        nv_ref[...] = vn

    blk = pl.BlockSpec((None, tr, last), lambda a, i: (a, i, 0))
    outs = pl.pallas_call(
        body, name=name, grid=(lead, rows // tr), in_specs=[blk] * 4, out_specs=[blk] * 3,
        out_shape=[jax.ShapeDtypeStruct((lead, rows, last), F32)] * 3,
        compiler_params=_cparams(("parallel", "parallel")),
    )(w2, g2, m2, v2)
    return tuple(o.reshape(shape) for o in outs)


SHARD_AXIS = {"conv_w": 2, "meta_tokens": 1}
SMALL_SHARDED = ("conv_w", "meta_tokens")
REPLICATED = ("mix_norm_w", "b_gate", "pool_scale", "conv_b", "dt_bias", "a_log", "d_skip", "ssd_norm_w", "mlp_norm_w",
              "final_norm_w")


def _flatten(arrs, dtype, row_mult):
    flat = jnp.concatenate([a.astype(dtype).reshape(-1) for a in arrs])
    n = flat.shape[0]
    rows = -(-n // (LANES * row_mult)) * row_mult
    return jnp.pad(flat, (0, rows * LANES - n)).reshape(rows, LANES)


def _unflatten(flat2d, shapes):
    flat = flat2d.reshape(-1)
    out, off = [], 0
    for sh in shapes:
        n = 1
        for d in sh:
            n *= d
        out.append(flat[off:off + n].reshape(sh))
        off += n
    return out


def kernel(x, meta_tokens, mix_norm_w, w_in, b_gate, pool_w_group, pool_scale, w_pool_up, conv_w, conv_b, dt_bias, a_log, d_skip, ssd_norm_w, w_ssd_out, w_o, mlp_norm_w, w_ff1, w_ff2, final_norm_w, loss_target, m_meta_tokens, m_mix_norm_w, m_w_in, m_b_gate, m_pool_w_group, m_pool_scale, m_w_pool_up, m_conv_w, m_conv_b, m_dt_bias, m_a_log, m_d_skip, m_ssd_norm_w, m_w_ssd_out, m_w_o, m_mlp_norm_w, m_w_ff1, m_w_ff2, m_final_norm_w, v_meta_tokens, v_mix_norm_w, v_w_in, v_b_gate, v_pool_w_group, v_pool_scale, v_w_pool_up, v_conv_w, v_conv_b, v_dt_bias, v_a_log, v_d_skip, v_ssd_norm_w, v_w_ssd_out, v_w_o, v_mlp_norm_w, v_w_ff1, v_w_ff2, v_final_norm_w):
    names = ("meta_tokens",) + PER_LAYER + ("final_norm_w",)
    par = dict(meta_tokens=meta_tokens, mix_norm_w=mix_norm_w, w_in=w_in, b_gate=b_gate, pool_w_group=pool_w_group,
               pool_scale=pool_scale, w_pool_up=w_pool_up, conv_w=conv_w, conv_b=conv_b, dt_bias=dt_bias, a_log=a_log,
               d_skip=d_skip, ssd_norm_w=ssd_norm_w, w_ssd_out=w_ssd_out, w_o=w_o, mlp_norm_w=mlp_norm_w, w_ff1=w_ff1,
               w_ff2=w_ff2, final_norm_w=final_norm_w)
    mom = dict(meta_tokens=m_meta_tokens, mix_norm_w=m_mix_norm_w, w_in=m_w_in, b_gate=m_b_gate, pool_w_group=m_pool_w_group,
               pool_scale=m_pool_scale, w_pool_up=m_w_pool_up, conv_w=m_conv_w, conv_b=m_conv_b, dt_bias=m_dt_bias,
               a_log=m_a_log, d_skip=m_d_skip, ssd_norm_w=m_ssd_norm_w, w_ssd_out=m_w_ssd_out, w_o=m_w_o,
               mlp_norm_w=m_mlp_norm_w, w_ff1=m_w_ff1, w_ff2=m_w_ff2, final_norm_w=m_final_norm_w)
    var = dict(meta_tokens=v_meta_tokens, mix_norm_w=v_mix_norm_w, w_in=v_w_in, b_gate=v_b_gate, pool_w_group=v_pool_w_group,
               pool_scale=v_pool_scale, w_pool_up=v_w_pool_up, conv_w=v_conv_w, conv_b=v_conv_b, dt_bias=v_dt_bias,
               a_log=v_a_log, d_skip=v_d_skip, ssd_norm_w=v_ssd_norm_w, w_ssd_out=v_w_ssd_out, w_o=v_w_o,
               mlp_norm_w=v_mlp_norm_w, w_ff1=v_w_ff1, w_ff2=v_w_ff2, final_norm_w=v_final_norm_w)
    chip = 2 * lax.axis_index("x") + lax.axis_index("y")
    core = lax.axis_index("c")

    weights = _WeightGatherer({n: par[n].astype(BF16).reshape(DEPTH, -1, par[n].shape[-1]) for n in BIG})
    small = {n: par[n] for n in REPLICATED}
    small_shapes = [par[n].shape for n in SMALL_SHARDED]
    got_small = _exchange_small(_flatten([par[n] for n in SMALL_SHARDED], F32, 8), False, "gather_small")
    pieces = [_unflatten(got_small[2 * k], small_shapes) for k in range(N_CHIPS)]
    for j, n in enumerate(SMALL_SHARDED):
        small[n] = jnp.concatenate([pieces[k][j] for k in range(N_CHIPS)], axis=SHARD_AXIS[n])

    reducer = _GradReducer(chip, core)
    loss, dx, _, grads = _local_step(x[0], loss_target[0], weights, small, reducer)

    small_names = REPLICATED + SMALL_SHARDED
    red = _exchange_small(_flatten([grads[n] for n in small_names] + [loss.reshape(1)], F32, 8), True, "allreduce_small")
    *parts, loss = _unflatten(red, [grads[n].shape for n in small_names] + [(1,)])
    loss = loss[0]
    gsum = dict(zip(small_names, parts))
    for n in SMALL_SHARDED:
        width = par[n].shape[SHARD_AXIS[n]]
        gsum[n] = lax.dynamic_slice_in_dim(gsum[n], chip * width, width, axis=SHARD_AXIS[n])

    gsum.update({n: g.reshape(par[n].shape) for n, g in reducer.finish().items()})

    delta, new_m, new_v = {}, {}, {}
    for n in BIG:
        if n == "w_in":
            tr = lambda a: jnp.swapaxes(a, 1, 2)
            outs = _adamw(tr(par[n]), tr(gsum[n]), tr(mom[n]), tr(var[n]), "adamw_" + n)
            delta[n], new_m[n], new_v[n] = (tr(o) for o in outs)
        else:
            delta[n], new_m[n], new_v[n] = _adamw(par[n], gsum[n], mom[n], var[n], "adamw_" + n)
    rest = [n for n in names if n not in BIG]
    shapes = [par[n].shape for n in rest]
    packed = [_flatten([d[n] for n in rest], F32, 8) for d in (par, gsum, mom, var)]
    for d, flat in zip((delta, new_m, new_v), _adamw(*packed, "adamw_small")):
        d.update(zip(rest, _unflatten(flat, shapes)))

    order = ("meta_tokens", "mix_norm_w", "w_in", "b_gate", "pool_w_group", "pool_scale", "w_pool_up", "conv_w", "conv_b",
             "dt_bias", "a_log", "d_skip", "ssd_norm_w", "w_ssd_out", "w_o", "mlp_norm_w", "w_ff1", "w_ff2", "final_norm_w")
    return (loss, dx[None], *[gsum[n] for n in order], *[delta[n] for n in order], *[new_m[n] for n in order],
            *[new_v[n] for n in order])
```

```python
import functools

import jax
import jax.numpy as jnp
from jax import lax
from jax.experimental import pallas as pl
from jax.experimental.pallas import tpu as pltpu

F32 = jnp.float32
BF16 = jnp.bfloat16

D_MODEL = 1024
DEPTH = 4
N_META = 16
N_PAD = 112
ROW_X = N_PAD + N_META
POOL_WINDOWS = (2, 4, 8, 16)
POOL_GDIM = 256
D_INNER = 2048
N_HEADS = 32
HEAD_DIM = 64
N_GROUPS = 8
HEADS_PER_GROUP = 4
GROUP_W = HEADS_PER_GROUP * HEAD_DIM
D_STATE = 128
CHUNK = 128
D_XBC = 4096
D_FF = 4096
EPS = 1e-5
OFF_Z, OFF_XBC, OFF_DT, OFF_GATE, IN_COLS = 1024, 3072, 7168, 7200, 9248
PZ, PGATE, PPOOL, PXBC, PCOLS = 0, 2048, 4096, 5120, 9216
DT_PAD = 128

ADAM_LR, ADAM_B1, ADAM_B2, ADAM_EPS, ADAM_WD, ADAM_STEP = 0.001, 0.9, 0.999, 1e-08, 0.01, 10

VMEM_LIMIT = 56 * 1024 * 1024

_NN = (((1,), (0,)), ((), ()))
_NT = (((1,), (1,)), ((), ()))
_TN = (((0,), (0,)), ((), ()))


def _dot(a, b, dn=_NN):
    return lax.dot_general(a, b, dn, preferred_element_type=F32)


def _cparams(sem):
    return pltpu.CompilerParams(dimension_semantics=sem, vmem_limit_bytes=VMEM_LIMIT)


def _tile(n, cands):
    for c in cands:
        if n % c == 0:
            return c
    raise ValueError(f"no tile for {n} in {cands}")


def _sigmoid(x):
    return 0.5 * jnp.tanh(0.5 * x) + 0.5


def _softplus(x):
    return jnp.maximum(x, 0.0) + jnp.log(1.0 + jnp.exp(-jnp.abs(x)))


def _bs(shape, fn):
    return pl.BlockSpec(shape, fn)


class _Carry:
    def __init__(self, ins, outs, n_sems, make, alias=None):
        self.ins, self.outs, self.n_sems, self.make = list(ins), list(outs), n_sems, make
        self.alias = dict(alias or {})

    def aliases(self, first_in, first_out):
        return {first_in + i: first_out + o for i, o in self.alias.items()}

    def specs(self):
        hbm = pl.BlockSpec(memory_space=pl.ANY)
        return [hbm] * len(self.ins), [hbm] * len(self.outs), [pltpu.SemaphoreType.DMA((self.n_sems,))] * 2

    def at_edges(self, grid, in_refs, out_refs, sems):
        ids = [pl.program_id(a) for a in range(len(grid))]
        first = functools.reduce(jnp.logical_and, [i == 0 for i in ids])
        last = functools.reduce(jnp.logical_and, [i == n - 1 for i, n in zip(ids, grid)])

        def begin():
            @pl.when(first)
            def _():
                for cp in self.make(in_refs, out_refs, *sems):
                    cp.start()

        def end():
            @pl.when(last)
            def _():
                for cp in self.make(in_refs, out_refs, *sems):
                    cp.wait()

        return begin, end


def _sum_into(o_ref, v, first):
    @pl.when(first)
    def _():
        o_ref[...] = v

    @pl.when(jnp.logical_not(first))
    def _():
        o_ref[...] += v


def _mm(name, dn, grid, a, a_spec, b, b_spec, outs, *, extras=(), epilogue=None, carry=None, summed=()):
    nk = grid[2]
    ne, no = len(extras), len(outs)
    blk = tuple(d for d in outs[0][2].block_shape if d is not None)
    c_in, c_out, c_sem = carry.specs() if carry is not None else ([], [], [])
    nci, nco = len(c_in), len(c_out)

    def body(a_ref, b_ref, *rest):
        e_refs, o_refs = rest[:ne], rest[ne + nci:ne + nci + no]
        first_row_tile = pl.program_id(0) == 0
        if carry is not None:
            begin, end = carry.at_edges(grid, rest[ne:ne + nci], rest[ne + nci + no:ne + nci + no + nco], rest[-2:])
            begin()
        if len(b_ref.shape) == 3:
            rows = b_ref.shape[1]
            p = _dot(a_ref[:, :rows].astype(BF16), b_ref[0].astype(BF16), dn)
            for s in range(1, b_ref.shape[0]):
                p = p + _dot(a_ref[:, s * rows:(s + 1) * rows].astype(BF16), b_ref[s].astype(BF16), dn)
        else:
            p = _dot(a_ref[...].astype(BF16), b_ref[...].astype(BF16), dn)

        def finish(acc):
            outs = epilogue(acc, *[e[...] for e in e_refs]) if epilogue is not None else (acc,)
            for idx, (o, v) in enumerate(zip(o_refs, outs)):
                if idx in summed:
                    _sum_into(o, v, first_row_tile)
                else:
                    o[...] = v.astype(o.dtype)

        if nk == 1:
            finish(p)
        else:
            acc_ref = rest[ne + nci + no + nco]
            kk = pl.program_id(2)

            @pl.when(kk == 0)
            def _():
                acc_ref[...] = p

            @pl.when(kk > 0)
            def _():
                acc_ref[...] += p

            @pl.when(kk == nk - 1)
            def _():
                finish(acc_ref[...])
        if carry is not None:
            end()

    res = pl.pallas_call(
        body,
        name=name,
        grid=grid,
        in_specs=[a_spec, b_spec, *[s for _, s in extras]] + c_in,
        out_specs=[o[2] for o in outs] + c_out,
        out_shape=[jax.ShapeDtypeStruct(o[0], o[1]) for o in outs] + (carry.outs if carry is not None else []),
        scratch_shapes=([pltpu.VMEM(blk, F32)] if nk > 1 else []) + c_sem,
        input_output_aliases=carry.aliases(2 + ne, no) if carry is not None else {},
        compiler_params=_cparams(("arbitrary",) * 3 if carry is not None or summed else ("parallel", "parallel", "arbitrary")),
    )(a, b, *[e for e, _ in extras], *(carry.ins if carry is not None else []))
    if carry is not None:
        return res[:no], res[no:]
    return res[0] if no == 1 else res


def _norm_fwd(h, w, name):
    t, d = h.shape
    tm = _tile(t, (1056, 384, 128))

    def body(h_ref, w_ref, u_ref):
        x = h_ref[...]
        r = lax.rsqrt(jnp.mean(x * x, axis=-1, keepdims=True) + EPS)
        u_ref[...] = (x * r * w_ref[...]).astype(BF16)

    return pl.pallas_call(
        body, name=name, grid=(t // tm,),
        in_specs=[pl.BlockSpec((tm, d), lambda i: (i, 0)), pl.BlockSpec((1, d), lambda i: (0, 0))],
        out_specs=pl.BlockSpec((tm, d), lambda i: (i, 0)),
        out_shape=jax.ShapeDtypeStruct((t, d), BF16),
        compiler_params=_cparams(("parallel",)),
    )(h, w.reshape(1, d))


def _loss_head(h, w, tgt):
    t, d = h.shape
    tm = CHUNK
    nb = ROW_X // tm

    def body(h_ref, w_ref, t_ref, loss_ref, dh_ref, dw_ref):
        i = pl.program_id(0)
        x = h_ref[...]
        r = lax.rsqrt(jnp.mean(x * x, axis=-1, keepdims=True) + EPS)
        xhat = x * r
        wv = w_ref[...]
        live = i >= nb
        err = jnp.where(live, xhat * wv - t_ref[...], 0.0)
        dout = err * (1.0 / d)
        g = dout * wv
        dh_ref[...] = r * (g - xhat * jnp.mean(g * xhat, axis=-1, keepdims=True))

        @pl.when(i == 0)
        def _():
            loss_ref[...] = jnp.zeros_like(loss_ref)
            dw_ref[...] = jnp.zeros_like(dw_ref)

        loss_ref[...] += 0.5 * jnp.sum(jnp.sum(err * err, axis=-1, keepdims=True), axis=0, keepdims=True) * (1.0 / d)
        dw_ref[...] += jnp.sum(dout * xhat, axis=0, keepdims=True)

    row = pl.BlockSpec((tm, d), lambda i: (i, 0))
    vec = pl.BlockSpec((1, d), lambda i: (0, 0))
    loss, dh, dw = pl.pallas_call(
        body, name="loss_head", grid=(t // tm,),
        in_specs=[row, vec, pl.BlockSpec((tm, d), lambda i: (jnp.maximum(i - nb, 0), 0))],
        out_specs=[pl.BlockSpec((1, 1), lambda i: (0, 0)), row, vec],
        out_shape=[jax.ShapeDtypeStruct((1, 1), F32), jax.ShapeDtypeStruct((t, d), F32), jax.ShapeDtypeStruct((1, d), F32)],
        compiler_params=_cparams(("arbitrary",)),
    )(h, w.reshape(1, d), tgt)
    return loss[0, 0], dh, dw.reshape(d)


def _gate_fwd(proj, b_gate, y_pool, y_ssd):
    t = proj.shape[0]
    d = D_MODEL
    tm = _tile(t, (528, 384, 128))

    def body(gp_ref, gs_ref, bp_ref, bs_ref, yp_ref, ys_ref, o_ref):
        gp = _sigmoid(gp_ref[...] + bp_ref[...])
        gs = _sigmoid(gs_ref[...] + bs_ref[...])
        o_ref[...] = (gp * yp_ref[...] + gs * ys_ref[...]).astype(BF16)

    row = pl.BlockSpec((tm, d), lambda i: (i, 0))
    return pl.pallas_call(
        body, name="gate_fwd", grid=(t // tm,),
        in_specs=[pl.BlockSpec((tm, d), lambda i: (i, PGATE // d)), pl.BlockSpec((tm, d), lambda i: (i, PGATE // d + 1)),
                  pl.BlockSpec((1, d), lambda i: (0, 0)), pl.BlockSpec((1, d), lambda i: (0, 1)), row, row],
        out_specs=row, out_shape=jax.ShapeDtypeStruct((t, d), BF16),
        compiler_params=_cparams(("parallel",)),
    )(proj, proj, b_gate.reshape(1, 2 * d), b_gate.reshape(1, 2 * d), y_pool, y_ssd)


def _gate_bwd(dmerged, proj, b_gate, y_pool, y_ssd):
    t = proj.shape[0]
    d = D_MODEL
    tm = _tile(t, (384, 128))

    def body(dm_ref, gp_ref, gs_ref, bp_ref, bs_ref, yp_ref, ys_ref, dyp_ref, dys_ref, dg_ref, db_ref):
        dm = dm_ref[...]
        gp = _sigmoid(gp_ref[...] + bp_ref[...])
        gs = _sigmoid(gs_ref[...] + bs_ref[...])
        dyp_ref[...] = (dm * gp).astype(BF16)
        dys_ref[...] = (dm * gs).astype(BF16)
        dgp = dm * yp_ref[...] * gp * (1.0 - gp)
        dgs = dm * ys_ref[...] * gs * (1.0 - gs)
        dg_ref[:, :d] = dgp.astype(BF16)
        dg_ref[:, d:] = dgs.astype(BF16)

        @pl.when(pl.program_id(0) == 0)
        def _():
            db_ref[...] = jnp.zeros_like(db_ref)

        db_ref[:, :d] += jnp.sum(dgp, axis=0, keepdims=True)
        db_ref[:, d:] += jnp.sum(dgs, axis=0, keepdims=True)

    row = pl.BlockSpec((tm, d), lambda i: (i, 0))
    dyp, dys, dg, db = pl.pallas_call(
        body, name="gate_bwd", grid=(t // tm,),
        in_specs=[row, pl.BlockSpec((tm, d), lambda i: (i, PGATE // d)), pl.BlockSpec((tm, d), lambda i: (i, PGATE // d + 1)),
                  pl.BlockSpec((1, d), lambda i: (0, 0)), pl.BlockSpec((1, d), lambda i: (0, 1)), row, row],
        out_specs=[row, row, pl.BlockSpec((tm, 2 * d), lambda i: (i, PGATE // (2 * d))),
                   pl.BlockSpec((1, 2 * d), lambda i: (0, 0))],
        out_shape=[jax.ShapeDtypeStruct((t, d), BF16), jax.ShapeDtypeStruct((t, d), BF16),
                   jax.ShapeDtypeStruct((t, PCOLS), BF16), jax.ShapeDtypeStruct((1, 2 * d), F32)],
        compiler_params=_cparams(("arbitrary",)),
    )(dmerged, proj, proj, b_gate.reshape(1, 2 * d), b_gate.reshape(1, 2 * d), y_pool, y_ssd)
    return dyp, dys, dg, db.reshape(2 * d)


POOL_HALO = 16


def _pool_counts(row0, n, win):
    pos1 = row0 + lax.broadcasted_iota(jnp.int32, (n, 1), 0) - (N_PAD - 1)
    return jnp.clip(pos1, 1, win).astype(F32)


N_CHIPS = 4
WG_ROWS = POOL_GDIM // N_CHIPS
WUP_ROWS = D_MODEL // N_CHIPS


def _group_w(wg_ref, g):
    return jnp.concatenate([wg_ref[k, g] for k in range(N_CHIPS)], axis=0)


def _pool_w_specs(layer):
    return [pl.BlockSpec((None, N_CHIPS, len(POOL_WINDOWS), WG_ROWS, POOL_GDIM), lambda i: (0, 0, 0, 0, 0)),
            pl.BlockSpec((None, N_CHIPS, WUP_ROWS, D_MODEL), lambda i: (0, 0, 0, 0))]


def _pool_fwd(proj, wg, scale, wup, layer):
    t = proj.shape[0]
    d = D_MODEL
    tm = _tile(t, (384, 128))
    hb = tm // POOL_HALO

    def body(u_ref, halo_ref, sc_ref, wg_ref, wup_ref, pooled_ref, yg_ref, ypm_ref, yp_ref):
        i = pl.program_id(0)
        x = u_ref[...]
        halo = jnp.where(i > 0, halo_ref[...], 0.0)
        xc = jnp.concatenate([halo, x], axis=0)
        for g, win in enumerate(POOL_WINDOWS):
            sl = slice(g * POOL_GDIM, (g + 1) * POOL_GDIM)
            s = xc[:, sl]
            k = 1
            while k < win:
                s = s + pltpu.roll(s, k, axis=0)
                k *= 2
            pooled = s[POOL_HALO:] / _pool_counts(i * tm, tm, win) - x[:, sl]
            pb = pooled.astype(BF16)
            pooled_ref[:, sl] = pb
            yg_ref[:, sl] = _dot(pb, _group_w(wg_ref, g))
        ypm = (yg_ref[...] * sc_ref[...]).astype(BF16)
        ypm_ref[...] = ypm
        acc = _dot(ypm[:, :WUP_ROWS], wup_ref[0])
        for k in range(1, N_CHIPS):
            acc = acc + _dot(ypm[:, k * WUP_ROWS:(k + 1) * WUP_ROWS], wup_ref[k])
        yp_ref[...] = acc

    row = pl.BlockSpec((tm, d), lambda i: (i, 0))
    return pl.pallas_call(
        body, name=f"pool_fwd_{layer}", grid=(t // tm,),
        in_specs=[pl.BlockSpec((tm, d), lambda i: (i, PPOOL // d)),
                  pl.BlockSpec((POOL_HALO, d), lambda i: (jnp.maximum(i * hb - 1, 0), PPOOL // d)),
                  pl.BlockSpec((1, d), lambda i: (0, 0))] + _pool_w_specs(layer),
        out_specs=[row, row, row, row],
        out_shape=[jax.ShapeDtypeStruct((t, d), BF16), jax.ShapeDtypeStruct((t, d), F32),
                   jax.ShapeDtypeStruct((t, d), BF16), jax.ShapeDtypeStruct((t, d), F32)],
        compiler_params=_cparams(("parallel",)),
    )(proj, proj, scale.reshape(1, d), wg, wup)


def _pool_bwd_a(dy_pool, yg, scale, wg, wup, layer):
    t, d = yg.shape
    tm = _tile(t, (384, 128))

    def body(dy_ref, yg_ref, sc_ref, wg_ref, wup_ref, q_ref, dyg_ref, dsc_ref):
        dy = dy_ref[...]
        dypm = jnp.concatenate([_dot(dy, wup_ref[k], _NT) for k in range(N_CHIPS)], axis=1)

        @pl.when(pl.program_id(0) == 0)
        def _():
            dsc_ref[...] = jnp.zeros_like(dsc_ref)

        dsc_ref[...] += jnp.sum(dypm * yg_ref[...], axis=0, keepdims=True)
        dyg = (dypm * sc_ref[...]).astype(BF16)
        dyg_ref[...] = dyg
        for g in range(len(POOL_WINDOWS)):
            sl = slice(g * POOL_GDIM, (g + 1) * POOL_GDIM)
            q_ref[:, sl] = _dot(dyg[:, sl], _group_w(wg_ref, g), _NT)

    row = pl.BlockSpec((tm, d), lambda i: (i, 0))
    vec = pl.BlockSpec((1, d), lambda i: (0, 0))
    q, dyg, dsc = pl.pallas_call(
        body, name=f"pool_bwd_a_{layer}", grid=(t // tm,),
        in_specs=[row, row, vec] + _pool_w_specs(layer),
        out_specs=[row, row, vec],
        out_shape=[jax.ShapeDtypeStruct((t, d), F32), jax.ShapeDtypeStruct((t, d), BF16), jax.ShapeDtypeStruct((1, d), F32)],
        compiler_params=_cparams(("arbitrary",)),
    )(dy_pool, yg, scale.reshape(1, d), wg, wup)
    return q, dyg, dsc.reshape(d)


def _pool_bwd_b(q, dproj):
    t, d = q.shape
    tm = _tile(t, (384, 128))
    hb = tm // POOL_HALO
    nt = t // tm
    n = tm + POOL_HALO

    def body(q_ref, halo_ref, _, o_ref):
        i = pl.program_id(0)
        qv = q_ref[...]
        halo = jnp.where(i < nt - 1, halo_ref[...], 0.0)
        qc = jnp.concatenate([qv, halo], axis=0)
        for g, win in enumerate(POOL_WINDOWS):
            sl = slice(g * POOL_GDIM, (g + 1) * POOL_GDIM)
            s = qc[:, sl] / _pool_counts(i * tm, n, win)
            k = 1
            while k < win:
                s = s + pltpu.roll(s, n - k, axis=0)
                k *= 2
            o_ref[:, sl] = (s[:tm] - qv[:, sl]).astype(BF16)

    row = pl.BlockSpec((tm, d), lambda i: (i, 0))
    return pl.pallas_call(
        body, name="pool_bwd_b", grid=(nt,),
        in_specs=[row, pl.BlockSpec((POOL_HALO, d), lambda i: (jnp.minimum((i + 1) * hb, t // POOL_HALO - 1), 0)),
                  pl.BlockSpec(memory_space=pl.ANY)],
        out_specs=pl.BlockSpec((tm, d), lambda i: (i, PPOOL // d)), out_shape=jax.ShapeDtypeStruct(dproj.shape, BF16),
        input_output_aliases={2: 0},
        compiler_params=_cparams(("parallel",)),
    )(q, q, dproj)


def _pool_dwg(pooled, dyg, layer):
    t, d = pooled.shape
    tk = _tile(t, (1056, 384, 128))
    nk = t // tk
    gd = POOL_GDIM
    ng = d // gd

    def body(p_ref, g_ref, o_ref):
        @pl.when(pl.program_id(1) == 0)
        def _():
            o_ref[...] = jnp.zeros_like(o_ref)

        part = _dot(p_ref[...], g_ref[...], _TN)
        for k in range(N_CHIPS):
            o_ref[k] += part[k * WG_ROWS:(k + 1) * WG_ROWS]

    blk = pl.BlockSpec((tk, gd), lambda g, k: (k, g))
    return pl.pallas_call(
        body, name=f"pool_dwg_{layer}", grid=(ng, nk), in_specs=[blk, blk],
        out_specs=pl.BlockSpec((N_CHIPS, None, WG_ROWS, gd), lambda g, k: (0, g, 0, 0)),
        out_shape=jax.ShapeDtypeStruct((N_CHIPS, ng, WG_ROWS, gd), F32),
        compiler_params=_cparams(("parallel", "arbitrary")),
    )(pooled, dyg)


CONV_W = 4
CONV_HALO = 8
XBC_BLK = PXBC // 1024


def _conv_fwd(proj, conv_w, conv_b):
    t = proj.shape[0]
    cw = 1024
    tm = _tile(t, (1056, 384, 128))
    hb = tm // CONV_HALO

    def body(x_ref, halo_ref, w_ref, b_ref, o_ref, pre_ref):
        i = pl.program_id(1)
        x = x_ref[...]
        halo = jnp.where(i > 0, halo_ref[...], 0.0)
        xc = jnp.concatenate([halo, x], axis=0)
        w = w_ref[...]
        acc = b_ref[...] + x * w[CONV_W - 1:CONV_W, :]
        for k in range(CONV_W - 1):
            acc = acc + pltpu.roll(xc, CONV_W - 1 - k, axis=0)[CONV_HALO:] * w[k:k + 1, :]
        row = i * tm + lax.broadcasted_iota(jnp.int32, (tm, 1), 0)
        pre_ref[...] = acc
        o_ref[...] = jnp.where(row >= N_PAD, acc * _sigmoid(acc), 0.0)

    blk = pl.BlockSpec((tm, cw), lambda j, i: (i, j))
    return pl.pallas_call(
        body, name="conv_fwd", grid=(D_XBC // cw, t // tm),
        in_specs=[pl.BlockSpec((tm, cw), lambda j, i: (i, XBC_BLK + j)),
                  pl.BlockSpec((CONV_HALO, cw), lambda j, i: (jnp.maximum(i * hb - 1, 0), XBC_BLK + j)),
                  pl.BlockSpec((CONV_W, cw), lambda j, i: (0, j)), pl.BlockSpec((1, cw), lambda j, i: (0, j))],
        out_specs=[blk, blk],
        out_shape=[jax.ShapeDtypeStruct((t, D_XBC), F32), jax.ShapeDtypeStruct((t, D_XBC), F32)],
        compiler_params=_cparams(("parallel", "parallel")),
    )(proj, proj, conv_w, conv_b.reshape(1, D_XBC))


def _conv_bwd(dxa, coff, proj, pre, conv_w, dproj):
    t, ncols = dxa.shape
    cw = 1024
    tm = _tile(t, (528, 384, 128))
    hb = tm // CONV_HALO
    nt = t // tm
    n = tm + CONV_HALO

    def body(d_ref, dn_ref, p_ref, pn_ref, x_ref, w_ref, _, o_ref, dw_ref, db_ref):
        i = pl.program_id(1)
        last = i == nt - 1
        xc = jnp.concatenate([p_ref[...], pn_ref[...]], axis=0)
        df = jnp.concatenate([d_ref[...], dn_ref[...]], axis=0)
        w = w_ref[...]
        sig = _sigmoid(xc)
        row = i * tm + lax.broadcasted_iota(jnp.int32, (n, 1), 0)
        live = (row >= N_PAD) & ((row < (i + 1) * tm) | jnp.logical_not(last))
        dxc = jnp.where(live, df * (sig * (1.0 + xc * (1.0 - sig))), 0.0)
        ahead = [pltpu.roll(dxc, n - (CONV_W - 1 - k), axis=0)[:tm] for k in range(CONV_W - 1)] + [dxc[:tm]]
        acc = ahead[0] * w[0:1, :]
        for k in range(1, CONV_W):
            acc = acc + ahead[k] * w[k:k + 1, :]
        o_ref[...] = acc.astype(BF16)

        @pl.when(i == 0)
        def _():
            dw_ref[...] = jnp.zeros_like(dw_ref)
            db_ref[...] = jnp.zeros_like(db_ref)

        x = x_ref[...]
        db_ref[...] += jnp.sum(ahead[CONV_W - 1], axis=0, keepdims=True)
        for k in range(CONV_W):
            dw_ref[k:k + 1, :] += jnp.sum(ahead[k] * x, axis=0, keepdims=True)

    def pspec(rows, fn):
        return pl.BlockSpec((rows, cw), lambda j, i: (fn(i), coff + j))

    nxt = lambda i: jnp.minimum((i + 1) * hb, t // CONV_HALO - 1)
    dxbc, dw, db = pl.pallas_call(
        body, name=f"conv_bwd_{coff}", grid=(ncols // cw, nt),
        in_specs=[pl.BlockSpec((tm, cw), lambda j, i: (i, j)), pl.BlockSpec((CONV_HALO, cw), lambda j, i: (nxt(i), j)),
                  pspec(tm, lambda i: i), pspec(CONV_HALO, nxt),
                  pl.BlockSpec((tm, cw), lambda j, i: (i, XBC_BLK + coff + j)),
                  pl.BlockSpec((CONV_W, cw), lambda j, i: (0, coff + j)),
                  pl.BlockSpec(memory_space=pl.ANY)],
        out_specs=[pl.BlockSpec((tm, cw), lambda j, i: (i, XBC_BLK + coff + j)), pl.BlockSpec((CONV_W, cw), lambda j, i: (0, j)),
                   pl.BlockSpec((1, cw), lambda j, i: (0, j))],
        out_shape=[jax.ShapeDtypeStruct(dproj.shape, BF16), jax.ShapeDtypeStruct((CONV_W, ncols), F32),
                   jax.ShapeDtypeStruct((1, ncols), F32)],
        input_output_aliases={6: 0},
        compiler_params=_cparams(("parallel", "arbitrary")),
    )(dxa, dxa, pre, pre, proj, conv_w, dproj)
    return dxbc, dw, db.reshape(ncols)


def _cumsum(x, axis, reverse=False):
    n = x.shape[axis]
    idx = lax.broadcasted_iota(jnp.int32, x.shape, axis)
    k = 1
    while k < n:
        if reverse:
            x = x + jnp.where(idx < n - k, pltpu.roll(x, n - k, axis=axis), 0.0)
        else:
            x = x + jnp.where(idx >= k, pltpu.roll(x, k, axis=axis), 0.0)
        k *= 2
    return x


def _head_masks():
    lane = lax.broadcasted_iota(jnp.int32, (1, GROUP_W), 1)
    return [(lane >= r * HEAD_DIM) & (lane < (r + 1) * HEAD_DIM) for r in range(HEADS_PER_GROUP)]


def _expand_heads(cols, hm):
    out = jnp.where(hm[0], cols[:, 0:1], 0.0)
    for r in range(1, HEADS_PER_GROUP):
        out = out + jnp.where(hm[r], cols[:, r:r + 1], 0.0)
    return out


def _ssd_decay(dt_raw, dt_bias, a_log):
    t = dt_raw.shape[0]

    hpg = HEADS_PER_GROUP

    def body(raw_ref, b_ref, al_ref, dt_ref, acs_ref, sig_ref, rows_ref):
        raw = raw_ref[...] + b_ref[...]
        rowid = pl.program_id(0) * CHUNK + lax.broadcasted_iota(jnp.int32, (CHUNK, 1), 0)
        dt = jnp.where(rowid >= N_PAD, _softplus(raw), 0.0)
        acs = _cumsum(dt * -jnp.exp(al_ref[...]), 0)
        sig = _sigmoid(raw)
        acs_t = acs.T
        for g in range(N_GROUPS):
            heads = slice(g * hpg, (g + 1) * hpg)
            dt_ref[g] = dt[:, heads]
            acs_ref[g] = acs[:, heads]
            sig_ref[g] = sig[:, heads]
            rows_ref[g] = acs_t[heads, :]

    blk = pl.BlockSpec((CHUNK, DT_PAD), lambda c: (c, 0))
    vec = pl.BlockSpec((1, DT_PAD), lambda c: (0, 0))
    cols = pl.BlockSpec((N_GROUPS, CHUNK, hpg), lambda c: (0, c, 0))
    pad = lambda v: jnp.pad(v, (0, DT_PAD - N_HEADS)).reshape(1, DT_PAD)
    dt, acs, sig, rows = pl.pallas_call(
        body, name="ssd_decay", grid=(t // CHUNK,), in_specs=[blk, vec, vec],
        out_specs=[cols, cols, cols, pl.BlockSpec((N_GROUPS, hpg, CHUNK), lambda c: (0, 0, c))],
        out_shape=[jax.ShapeDtypeStruct((N_GROUPS, t, hpg), F32)] * 3 + [jax.ShapeDtypeStruct((N_GROUPS, hpg, t), F32)],
        compiler_params=_cparams(("parallel",)),
    )(dt_raw, pad(dt_bias), pad(a_log))
    return dict(dt=dt, acs=acs, acs_rows=rows, sig=sig, a_log=a_log.reshape(N_GROUPS, 1, hpg))


def _ssd_q(dtc, acs4, acs_r):
    hm = _head_masks()
    dt_exp = _expand_heads(dtc, hm)
    acs = _expand_heads(acs4, hm)
    atot = acs[CHUNK - 1:CHUNK, :]
    return dict(dtc=dtc, hm=hm, dt_exp=dt_exp, acs=acs, acs_r=acs_r,
                ea=jnp.exp(acs), ds=jnp.exp(atot - acs), dec=jnp.exp(atot))


def _stack4(x):
    return jnp.concatenate([x] * HEADS_PER_GROUP, axis=0)


def _ssd_decay_stacks(q):
    hpg = HEADS_PER_GROUP
    a_col = jnp.concatenate([q["acs"][:, r * HEAD_DIM:r * HEAD_DIM + 1] for r in range(hpg)], axis=0)
    a_row = jnp.concatenate([jnp.broadcast_to(q["acs_r"][r:r + 1, :], (CHUNK, CHUNK)) for r in range(hpg)], axis=0)
    ri = lax.broadcasted_iota(jnp.int32, (hpg * CHUNK, CHUNK), 0) % CHUNK
    ci = lax.broadcasted_iota(jnp.int32, (hpg * CHUNK, CHUNK), 1)
    diff = a_col - a_row
    lm = jnp.exp(jnp.where(ri >= ci, diff, -jnp.inf))
    lt = jnp.exp(jnp.where(ri <= ci, -diff, -jnp.inf))
    return lm, lt


def _pick_heads(stacked, hm):
    out = jnp.where(hm[0], stacked[:CHUNK], 0.0)
    for r in range(1, HEADS_PER_GROUP):
        out = out + jnp.where(hm[r], stacked[r * CHUNK:(r + 1) * CHUNK], 0.0)
    return out


def _mask_heads(x, hm):
    return jnp.concatenate([jnp.where(hm[r], x, 0.0) for r in range(HEADS_PER_GROUP)], axis=0)


SSD_GP_FWD, SSD_GP_BWD = 8, 8


def _ssd_specs(cidx, gp):
    hpg = HEADS_PER_GROUP
    return [
        pl.BlockSpec((CHUNK, gp * GROUP_W), lambda g, c: (cidx(c), g)),
        pl.BlockSpec((CHUNK, gp * D_STATE), lambda g, c: (cidx(c), D_INNER // (gp * D_STATE) + g)),
        pl.BlockSpec((CHUNK, gp * D_STATE), lambda g, c: (cidx(c), (D_INNER + 1024) // (gp * D_STATE) + g)),
        pl.BlockSpec((gp, CHUNK, hpg), lambda g, c: (g, cidx(c), 0)),
        pl.BlockSpec((gp, CHUNK, hpg), lambda g, c: (g, cidx(c), 0)),
        pl.BlockSpec((gp, hpg, CHUNK), lambda g, c: (g, 0, cidx(c))),
    ]


def _ssd_fwd(xa, dec, carry=None):
    t = xa.shape[0]
    nc = t // CHUNK
    gp, gw, ds = SSD_GP_FWD, GROUP_W, D_STATE
    grid = (N_GROUPS // gp, nc)
    c_in, c_out, c_sem = carry.specs() if carry is not None else ([], [], [])
    nci, nco = len(c_in), len(c_out)

    def body(*refs):
        xs_ref, b_ref, c_ref, dt_ref, acs_ref, acsr_ref = refs[:6]
        y_ref, prev_ref = refs[6 + nci:8 + nci]
        st_ref = refs[8 + nci + nco]
        if carry is not None:
            begin, end = carry.at_edges(grid, refs[6:6 + nci], refs[8 + nci:8 + nci + nco], refs[-2:])
            begin()

        @pl.when(pl.program_id(1) == 0)
        def _():
            st_ref[...] = jnp.zeros_like(st_ref)

        for k in range(gp):
            q = _ssd_q(dt_ref[k], acs_ref[k], acsr_ref[k])
            xdt = xs_ref[:, k * gw:(k + 1) * gw] * q["dt_exp"]
            bm = b_ref[:, k * ds:(k + 1) * ds].astype(BF16)
            cm = c_ref[:, k * ds:(k + 1) * ds].astype(BF16)
            cb = _dot(cm, bm, _NT)
            st = st_ref[k]
            prev_ref[0, k] = st
            lm, _ = _ssd_decay_stacks(q)
            y_diag = _pick_heads(_dot((_stack4(cb) * lm).astype(BF16), xdt.astype(BF16)), q["hm"])
            y_ref[:, k * gw:(k + 1) * gw] = y_diag + _dot(cm, st.astype(BF16)) * q["ea"]
            st_ref[k] = q["dec"] * st + _dot(bm, (xdt * q["ds"]).astype(BF16), _TN)
        if carry is not None:
            end()

    res = pl.pallas_call(
        body, name="ssd_fwd", grid=grid,
        in_specs=_ssd_specs(lambda c: c, gp) + c_in,
        out_specs=[pl.BlockSpec((CHUNK, gp * gw), lambda g, c: (c, g)),
                   pl.BlockSpec((1, gp, ds, gw), lambda g, c: (c, g, 0, 0))] + c_out,
        out_shape=[jax.ShapeDtypeStruct((t, D_INNER), F32), jax.ShapeDtypeStruct((nc, N_GROUPS, ds, gw), F32)]
        + (carry.outs if carry is not None else []),
        scratch_shapes=[pltpu.VMEM((gp, ds, gw), F32)] + c_sem,
        input_output_aliases=carry.aliases(6, 2) if carry is not None else {},
        compiler_params=_cparams(("arbitrary", "arbitrary") if carry is not None else ("parallel", "arbitrary")),
    )(xa, xa, xa, dec["dt"], dec["acs"], dec["acs_rows"], *(carry.ins if carry is not None else []))
    return res[0], res[1], res[2:]


def _ssd_bwd(dy, dsk, xa, prev, dec, carry=None):
    t = xa.shape[0]
    nc = t // CHUNK
    gp, gw, dstate = SSD_GP_BWD, GROUP_W, D_STATE
    hpg = HEADS_PER_GROUP

    grid = (N_GROUPS // gp, nc)
    c_in, c_out, c_sem = carry.specs() if carry is not None else ([], [], [])
    nci, nco = len(c_in), len(c_out)

    def body(*refs):
        ins, c_ins = refs[:11], refs[11:11 + nci]
        outs = refs[11 + nci:17 + nci]
        c_outs = refs[17 + nci:17 + nci + nco]
        dst_ref = refs[17 + nci + nco]
        dbias_ref, dalog_ref = outs[4], outs[5]
        if carry is not None:
            begin, end = carry.at_edges(grid, c_ins, c_outs, refs[-2:])
            begin()

        @pl.when(pl.program_id(1) == 0)
        def _():
            dst_ref[...] = jnp.zeros_like(dst_ref)
            dbias_ref[...] = jnp.zeros_like(dbias_ref)
            dalog_ref[...] = jnp.zeros_like(dalog_ref)

        for k in range(gp):
            one_group(k, *ins, *outs, dst_ref)
        if carry is not None:
            end()

    def one_group(k, xs_ref, b_ref, c_ref, dt_ref, acs_ref, acsr_ref, sig_ref, al_ref, dy_ref, sk_ref, prev_ref,
                  dxs_ref, db_ref, dc_ref, ddt_ref, dbias_ref, dalog_ref, dst_ref):
        wide = slice(k * gw, (k + 1) * gw)
        narrow = slice(k * dstate, (k + 1) * dstate)
        q = _ssd_q(dt_ref[k], acs_ref[k], acsr_ref[k])
        a_r = -jnp.exp(al_ref[k])
        hm, ds, dec, dt_exp = q["hm"], q["ds"], q["dec"], q["dt_exp"]
        xs = xs_ref[:, wide]
        xdt = xs * dt_exp
        xdtb = xdt.astype(BF16)
        bm = b_ref[:, narrow].astype(BF16)
        cm = c_ref[:, narrow].astype(BF16)
        cb = _dot(cm, bm, _NT)
        bc = _dot(bm, cm, _NT)
        dyv = dy_ref[:, wide]
        dye = (dyv * q["ea"]).astype(BF16)
        pst = prev_ref[0, k]
        dst = dst_ref[k]
        dstb = dst.astype(BF16)
        dx_state = ds * _dot(bm, dstb)
        onehot = [(lax.broadcasted_iota(jnp.int32, (1, hpg), 1) == r).astype(F32) for r in range(hpg)]
        lm, lt = _ssd_decay_stacks(q)
        dyb = dyv.astype(BF16)
        gl = _dot(_mask_heads(dyv, hm).astype(BF16), xdtb, _NT) * lm
        glt = _dot(_mask_heads(xdt, hm).astype(BF16), dyb, _NT) * lt
        bc4 = _stack4(bc)
        dxdt = dx_state + _pick_heads(_dot((bc4 * lt).astype(BF16), dyb), hm)
        wd = jnp.sum(gl * _stack4(cb), axis=1, keepdims=True) - jnp.sum(glt * bc4, axis=1, keepdims=True)
        dcb = gl[:CHUNK]
        dcbt = glt[:CHUNK]
        qa = wd[:CHUNK] * onehot[0]
        for r in range(1, hpg):
            dcb = dcb + gl[r * CHUNK:(r + 1) * CHUNK]
            dcbt = dcbt + glt[r * CHUNK:(r + 1) * CHUNK]
            qa = qa + wd[r * CHUNK:(r + 1) * CHUNK] * onehot[r]
        pstb = pst.astype(BF16)
        dc_ref[:, narrow] = _dot(dcb.astype(BF16), bm) + _dot(dye, pstb, _NT)
        db_ref[:, narrow] = _dot(dcbt.astype(BF16), cm) + _dot((xdt * ds).astype(BF16), dstb, _NT)
        dst_ref[k] = dec * dst + _dot(cm, dye, _TN)
        dxs_ref[:, wide] = dxdt * dt_exp + dyv * sk_ref[:, wide]

        t2 = xdt * dx_state
        t1 = dyv * (_dot(cm, pstb) * q["ea"]) - t2
        t4 = dxdt * xs
        last_row = jnp.sum(t2, axis=0, keepdims=True) + dec * jnp.sum(dst * pst, axis=0, keepdims=True)
        xd = jnp.zeros((CHUNK, hpg), F32)
        dal = jnp.zeros((1, hpg), F32)
        for r in range(hpg):
            qa = qa + jnp.sum(jnp.where(hm[r], t1, 0.0), axis=1, keepdims=True) * onehot[r]
            xd = xd + jnp.sum(jnp.where(hm[r], t4, 0.0), axis=1, keepdims=True) * onehot[r]
            dal = dal + jnp.sum(jnp.where(hm[r], last_row, 0.0), axis=1, keepdims=True) * onehot[r]
        rc = _cumsum(qa, 0, reverse=True) + dal
        rowid = (nc - 1 - pl.program_id(1)) * CHUNK + lax.broadcasted_iota(jnp.int32, (CHUNK, 1), 0)
        ddt_raw = jnp.where(rowid >= N_PAD, (rc * a_r + xd) * sig_ref[k], 0.0)
        ddt_ref[k] = ddt_raw
        dbias_ref[k] += jnp.sum(ddt_raw, axis=0, keepdims=True)
        dalog_ref[k] += jnp.sum(rc * q["dtc"], axis=0, keepdims=True) * a_r

    rev = lambda c: nc - 1 - c
    blk = pl.BlockSpec((CHUNK, gp * gw), lambda g, c: (rev(c), g))
    nblk = pl.BlockSpec((CHUNK, gp * dstate), lambda g, c: (rev(c), g))
    cols = pl.BlockSpec((gp, CHUNK, hpg), lambda g, c: (g, rev(c), 0))
    small = pl.BlockSpec((gp, 1, hpg), lambda g, c: (g, 0, 0))
    res = pl.pallas_call(
        body, name="ssd_bwd", grid=grid,
        in_specs=_ssd_specs(rev, gp) + [cols, small, blk, pl.BlockSpec((1, gp * gw), lambda g, c: (0, g)),
                                    pl.BlockSpec((1, gp, dstate, gw), lambda g, c: (rev(c), g, 0, 0))]
        + c_in,
        out_specs=[blk, nblk, nblk, cols, small, small] + c_out,
        out_shape=[jax.ShapeDtypeStruct((t, D_INNER), F32), jax.ShapeDtypeStruct((t, N_GROUPS * dstate), F32),
                   jax.ShapeDtypeStruct((t, N_GROUPS * dstate), F32), jax.ShapeDtypeStruct((N_GROUPS, t, hpg), F32),
                   jax.ShapeDtypeStruct((N_GROUPS, 1, hpg), F32), jax.ShapeDtypeStruct((N_GROUPS, 1, hpg), F32)]
        + (carry.outs if carry is not None else []),
        scratch_shapes=[pltpu.VMEM((gp, dstate, gw), F32)] + c_sem,
        compiler_params=_cparams(("arbitrary", "arbitrary") if carry is not None else ("parallel", "arbitrary")),
    )(xa, xa, xa, dec["dt"], dec["acs"], dec["acs_rows"], dec["sig"], dec["a_log"], dy, dsk.reshape(1, D_INNER), prev,
      *(carry.ins if carry is not None else []))
    dxs, db, dc, ddt, dbias, dalog = res[:6]
    ddt_raw = ddt.transpose(1, 0, 2).reshape(t, N_HEADS)
    return dxs, db, dc, ddt_raw, dbias.reshape(N_HEADS), dalog.reshape(N_HEADS), res[6:]


def _ssd_post_fwd(y, xa, proj, dsk, nw):
    t = y.shape[0]
    di = D_INNER
    tm = _tile(t, (384, 128))

    def body(y_ref, xs_ref, z_ref, dsk_ref, nw_ref, o_ref):
        z = z_ref[...]
        yz = (y_ref[...] + xs_ref[...] * dsk_ref[...]) * (z * _sigmoid(z))
        nwv = nw_ref[...]
        for g in range(N_GROUPS):
            sl = slice(g * GROUP_W, (g + 1) * GROUP_W)
            v = yz[:, sl]
            rg = lax.rsqrt(jnp.mean(v * v, axis=-1, keepdims=True) + EPS)
            o_ref[:, sl] = (v * rg * nwv[:, sl]).astype(BF16)

    row = pl.BlockSpec((tm, di), lambda i: (i, 0))
    vec = pl.BlockSpec((1, di), lambda i: (0, 0))
    return pl.pallas_call(
        body, name="ssd_post_fwd", grid=(t // tm,),
        in_specs=[row, row, row, vec, vec], out_specs=row, out_shape=jax.ShapeDtypeStruct((t, di), BF16),
        compiler_params=_cparams(("parallel",)),
    )(y, xa, proj, dsk.reshape(1, di), nw.reshape(1, di))


def _ssd_post_bwd(dyn, y, xa, proj, dsk, nw, dproj):
    t = y.shape[0]
    di = D_INNER
    tm = CHUNK

    def body(dyn_ref, y_ref, xs_ref, z_ref, dsk_ref, nw_ref, _, dy_ref, dz_ref, dnw_ref, ddsk_ref):
        @pl.when(pl.program_id(0) == 0)
        def _():
            dnw_ref[...] = jnp.zeros_like(dnw_ref)
            ddsk_ref[...] = jnp.zeros_like(ddsk_ref)

        for g in range(N_GROUPS):
            sl = slice(g * GROUP_W, (g + 1) * GROUP_W)
            z = z_ref[:, sl]
            sig = _sigmoid(z)
            s = z * sig
            xs = xs_ref[:, sl]
            dskv = dsk_ref[:, sl]
            yt = y_ref[:, sl] + xs * dskv
            yz = yt * s
            rg = lax.rsqrt(jnp.mean(yz * yz, axis=-1, keepdims=True) + EPS)
            xhat = yz * rg
            dynv = dyn_ref[:, sl]
            gg = dynv * nw_ref[:, sl]
            dnw_ref[:, sl] += jnp.sum(dynv * xhat, axis=0, keepdims=True)
            dyz = rg * (gg - xhat * jnp.mean(gg * xhat, axis=-1, keepdims=True))
            dyt = dyz * s
            dy_ref[:, sl] = dyt
            dz_ref[:, sl] = (dyz * yt * (sig * (1.0 + z * (1.0 - sig)))).astype(BF16)
            ddsk_ref[:, sl] += jnp.sum(dyt * xs, axis=0, keepdims=True)

    row = pl.BlockSpec((tm, di), lambda i: (i, 0))
    vec = pl.BlockSpec((1, di), lambda i: (0, 0))
    dy, dz, dnw, ddsk = pl.pallas_call(
        body, name="ssd_post_bwd", grid=(t // tm,),
        in_specs=[row, row, row, row, vec, vec, pl.BlockSpec(memory_space=pl.ANY)], out_specs=[row, row, vec, vec],
        out_shape=[jax.ShapeDtypeStruct((t, di), F32), jax.ShapeDtypeStruct(dproj.shape, BF16),
                   jax.ShapeDtypeStruct((1, di), F32), jax.ShapeDtypeStruct((1, di), F32)],
        input_output_aliases={6: 1},
        compiler_params=_cparams(("arbitrary",)),
    )(dyn, y, xa, proj, dsk.reshape(1, di), nw.reshape(1, di), dproj)
    return dy, dz, dnw.reshape(di), ddsk.reshape(N_HEADS, HEAD_DIM).sum(axis=1)


SHARD_COLS = IN_COLS // N_CHIPS


def _split_w_in(w_in_sh):
    nl = w_in_sh.shape[0]
    w = w_in_sh.transpose(0, 2, 1, 3).reshape(nl, D_MODEL, IN_COLS)
    main = jnp.concatenate([w[..., OFF_Z:OFF_XBC], w[..., OFF_GATE:], w[..., :OFF_Z], w[..., OFF_XBC:OFF_DT]], axis=-1)
    dt = jnp.pad(w[..., OFF_DT:OFF_GATE], ((0, 0), (0, 0), (0, DT_PAD - N_HEADS)))
    return main, dt


def _merge_dw_in(dmain, ddt):
    full = jnp.concatenate([dmain[:, PPOOL:PXBC], dmain[:, PZ:PGATE], dmain[:, PXBC:], ddt[:, :N_HEADS], dmain[:, PGATE:PPOOL]],
                           axis=1)
    return full.reshape(D_MODEL, N_CHIPS, SHARD_COLS).transpose(1, 0, 2)


def _relu2_epilogue(acc):
    hid = jnp.maximum(acc, 0.0)
    return acc, hid * hid


def _relu2_bwd_epilogue(acc, pre):
    return (acc * 2.0 * jnp.maximum(pre, 0.0),)


def _add_epilogue(acc, res):
    return (acc + res,)


def _norm_bwd_epilogue(acc, dres, h, w):
    r = lax.rsqrt(jnp.mean(h * h, axis=-1, keepdims=True) + EPS)
    xhat = h * r
    g = acc * w
    dh = dres + r * (g - xhat * jnp.mean(g * xhat, axis=-1, keepdims=True))
    return dh, jnp.sum(acc * xhat, axis=0, keepdims=True)


def _add_norm_bwd_epilogue(acc, part, dres, h, w):
    return _norm_bwd_epilogue(acc + part, dres, h, w)


def _add_norm_epilogue(acc, res, w):
    x = acc + res
    r = lax.rsqrt(jnp.mean(x * x, axis=-1, keepdims=True) + EPS)
    return x, x * r * w


def _w_spec(rows, cols, fn):
    return pl.BlockSpec((None, None, rows, cols), fn)


def _with_carry(stage, call):
    carry = stage[0]() if stage is not None else None
    res = call(carry)
    if carry is None:
        return res
    outs, got = res
    stage[1](got)
    return outs[0] if len(outs) == 1 else outs


def _layer_fwd(h, u, w, big, layer, stages=None, next_norm_w=None):
    t = h.shape[0]
    d = D_MODEL
    tm = _tile(t, (1408, 384, 128))
    nt = t // tm
    row = _bs((tm, d), lambda i, j, k: (i, 0))
    vec = _bs((1, d), lambda i, j, k: (0, 0))
    stages = stages or {}
    s = {"h": h}
    if u is None:
        u = _norm_fwd(h, w["mix_norm_w"], "norm_mix")
    proj = _with_carry(stages.get("a_ici"), lambda carry: _mm(
        f"mm_proj_{layer}", _NN, (nt, PCOLS // d, 1), u, row,
        big["w_in_main"], _bs((None, d, d), lambda i, j, k: (0, 0, j)),
        [((t, PCOLS), F32, _bs((tm, d), lambda i, j, k: (i, j)))], carry=carry))
    dt_raw = _mm(f"mm_dt_{layer}", _NN, (nt, 1, 1), u, row,
                 big["w_in_dt"], _bs((None, d, DT_PAD), lambda i, j, k: (0, 0, 0)),
                 [((t, DT_PAD), F32, _bs((tm, DT_PAD), lambda i, j, k: (i, 0)))])
    pooled, yg, ypm, y_pool = _pool_fwd(proj, big["pool_w_group"], w["pool_scale"], big["w_pool_up"], layer)
    xa, conv_pre = _conv_fwd(proj, w["conv_w"], w["conv_b"])
    dec = _ssd_decay(dt_raw, w["dt_bias"], w["a_log"])
    stage = stages.get("b_ici")
    carry = stage[0]() if stage is not None else None
    y, prev, got = _ssd_fwd(xa, dec, carry)
    if carry is not None:
        stage[1](got)
    yn = _ssd_post_fwd(y, xa, proj, w["d_skip_exp"], w["ssd_norm_w"])
    rs = D_INNER // N_CHIPS
    y_ssd = _with_carry(stages.get("a_d2d"), lambda carry: _mm(
        f"mm_ssd_out_{layer}", _NN, (nt, 1, 1), yn, _bs((tm, D_INNER), lambda i, j, k: (i, 0)),
        big["w_ssd_out"], _bs((None, N_CHIPS, rs, d), lambda i, j, k: (0, 0, 0, 0)), [((t, d), F32, row)], carry=carry))
    merged = _gate_fwd(proj, w["b_gate"], y_pool, y_ssd)
    ro = d // N_CHIPS
    h1, v = _mm(f"mm_o_{layer}", _NN, (nt, 1, 1), merged, row,
                big["w_o"], _bs((None, N_CHIPS, ro, d), lambda i, j, k: (0, 0, 0, 0)),
                [((t, d), F32, row), ((t, d), BF16, row)],
                extras=[(h, row), (w["mlp_norm_w"].reshape(1, d), vec)], epilogue=_add_norm_epilogue)
    tile = _bs((tm, d), lambda i, j, k: (i, j))
    pre, act = _with_carry(stages.get("b_d2d"), lambda carry: _mm(
        f"mm_ff1_{layer}", _NN, (nt, N_CHIPS, 1), v, row,
        big["w_ff1"], _w_spec(d, d, lambda i, j, k: (0, j, 0, 0)),
        [((t, D_FF), F32, tile), ((t, D_FF), BF16, tile)], epilogue=_relu2_epilogue, carry=carry))
    a_spec = _bs((tm, d), lambda i, j, k: (i, k))
    b_spec = _w_spec(d, d, lambda i, j, k: (0, k, 0, 0))
    if next_norm_w is None:
        h2 = _mm(f"mm_ff2_{layer}", _NN, (nt, 1, N_CHIPS), act, a_spec, big["w_ff2"], b_spec, [((t, d), F32, row)],
                 extras=[(h1, row)], epilogue=_add_epilogue)
        u_next = None
    else:
        h2, u_next = _mm(f"mm_ff2_{layer}", _NN, (nt, 1, N_CHIPS), act, a_spec, big["w_ff2"], b_spec,
                         [((t, d), F32, row), ((t, d), BF16, row)],
                         extras=[(h1, row), (next_norm_w.reshape(1, d), vec)], epilogue=_add_norm_epilogue)
    s.update(u=u, proj=proj, dec=dec, conv_pre=conv_pre, pooled=pooled, yg=yg, ypm=ypm, y_pool=y_pool, xa=xa, y=y, prev=prev, yn=yn,
             y_ssd=y_ssd, merged=merged, h1=h1, v=v, pre=pre, act=act)
    return h2, u_next, s


def _dw(name, layer, a, b, shard_shape, by, tk, carry=None):
    t = a.shape[0]
    nk = t // tk
    rows, cols = shard_shape
    if by == "rows":
        grid = (N_CHIPS, 1, nk)
        a_spec = _bs((tk, rows), lambda i, j, k: (k, i))
        b_spec = _bs((tk, cols), lambda i, j, k: (k, 0))
        o_spec = _bs((None, rows, cols), lambda i, j, k: (i, 0, 0))
    else:
        grid = (1, N_CHIPS, nk)
        a_spec = _bs((tk, rows), lambda i, j, k: (k, 0))
        b_spec = _bs((tk, cols), lambda i, j, k: (k, j))
        o_spec = _bs((None, rows, cols), lambda i, j, k: (j, 0, 0))
    return _mm(f"{name}_{layer}", _TN, grid, a, a_spec, b, b_spec, [((N_CHIPS, rows, cols), F32, o_spec)], carry=carry)


EARLY = ("w_ff2", "w_ff1", "w_o", "w_ssd_out", "w_pool_up", "pool_w_group")


def _layer_bwd(dh, s, w, big, layer, red=None, last=False):
    t = dh.shape[0]
    d = D_MODEL
    tm = _tile(t, (1408, 384, 128))
    tk = _tile(t, (1408, 384, 128))
    nt = t // tm
    row = _bs((tm, d), lambda i, j, k: (i, 0))
    tile = _bs((tm, d), lambda i, j, k: (i, j))
    g, gb = {}, {}
    carry = red.swap_carry() if red is not None else None
    dpre = _mm(f"mm_dact_{layer}", _NT, (nt, N_CHIPS, 1), dh, row,
               big["w_ff2"], _w_spec(d, d, lambda i, j, k: (0, j, 0, 0)), [((t, D_FF), BF16, tile)],
               extras=[(s["pre"], tile)], epilogue=_relu2_bwd_epilogue, carry=carry)
    if carry is not None:
        (dpre,), got = dpre
        red.after_swap(got)
    gb["w_ff2"] = _dw("mm_dw_ff2", layer, s["act"], dh, (d, d), "rows", tk)
    vec = _bs((1, d), lambda i, j, k: (0, 0))
    th = _tile(t, (704, 384, 128))
    half_row = _bs((th, d), lambda i, j, k: (i, 0))
    dh1, dnw = _mm(f"mm_dv_{layer}", _NT, (t // th, 1, N_CHIPS), dpre, _bs((th, d), lambda i, j, k: (i, k)),
                   big["w_ff1"], _w_spec(d, d, lambda i, j, k: (0, k, 0, 0)), [((t, d), F32, half_row), ((1, d), F32, vec)],
                   extras=[(dh, half_row), (s["h1"], half_row), (w["mlp_norm_w"].reshape(1, d), vec)],
                   epilogue=_norm_bwd_epilogue, summed=(1,))
    g["mlp_norm_w"] = dnw.reshape(d)
    gb["w_ff1"] = _dw("mm_dw_ff1", layer, s["v"], dpre, (d, d), "cols", t)
    ro = d // N_CHIPS
    dmerged = _mm(f"mm_dmerged_{layer}", _NT, (nt, N_CHIPS, 1), dh1, row,
                  big["w_o"], _w_spec(ro, d, lambda i, j, k: (0, j, 0, 0)),
                  [((t, d), F32, _bs((tm, ro), lambda i, j, k: (i, j)))])
    gb["w_o"] = _dw("mm_dw_o", layer, s["merged"], dh1, (ro, d), "rows", tk)
    dy_pool, dy_ssd, dproj, g["b_gate"] = _gate_bwd(dmerged, s["proj"], w["b_gate"], s["y_pool"], s["y_ssd"])
    rs = D_INNER // N_CHIPS
    dyn = _mm(f"mm_dyn_{layer}", _NT, (nt, N_CHIPS, 1), dy_ssd, row,
              big["w_ssd_out"], _w_spec(rs, d, lambda i, j, k: (0, j, 0, 0)),
              [((t, D_INNER), F32, _bs((tm, rs), lambda i, j, k: (i, j)))])
    gb["w_ssd_out"] = _dw("mm_dw_ssd_out", layer, s["yn"], dy_ssd, (rs, d), "rows", t)
    dy, dproj, g["ssd_norm_w"], g["d_skip"] = _ssd_post_bwd(dyn, s["y"], s["xa"], s["proj"], w["d_skip_exp"],
                                                            w["ssd_norm_w"], dproj)
    carry = red.ici_carry() if red is not None else None
    dxs, db, dc, ddt_raw, g["dt_bias"], g["a_log"], got = _ssd_bwd(dy, w["d_skip_exp"], s["xa"], s["prev"], s["dec"], carry)
    if carry is not None:
        red.after_ici(got)
    own = red is not None and last
    dproj, dcw1, dcb1 = _conv_bwd(dxs, 0, s["proj"], s["conv_pre"], w["conv_w"], dproj)
    dproj, dcw2, dcb2 = _conv_bwd(db, 2, s["proj"], s["conv_pre"], w["conv_w"], dproj)
    dproj, dcw3, dcb3 = _conv_bwd(dc, 3, s["proj"], s["conv_pre"], w["conv_w"], dproj)
    g["conv_w"] = jnp.concatenate([dcw1, dcw2, dcw3], axis=1)
    g["conv_b"] = jnp.concatenate([dcb1, dcb2, dcb3])
    q, dyg, g["pool_scale"] = _pool_bwd_a(dy_pool, s["yg"], w["pool_scale"], big["pool_w_group"], big["w_pool_up"], layer)
    gb["w_pool_up"] = _dw("mm_dw_pool_up", layer, s["ypm"], dy_pool, (ro, d), "rows", t)
    gb["pool_w_group"] = _pool_dwg(s["pooled"], dyg, layer).reshape(N_CHIPS, POOL_GDIM, POOL_GDIM)
    dproj = _pool_bwd_b(q, dproj)
    if own:
        red.push(layer, {n: gb[n] for n in EARLY})
    ddt = jnp.pad(ddt_raw.astype(BF16), ((0, 0), (0, DT_PAD - N_HEADS)))
    nk = PCOLS // d
    kw = PCOLS // 4
    du = _with_carry((red.swap_carry, red.after_swap) if own else None, lambda carry: _mm(
        f"mm_du_{layer}", _NT, (nt, 1, PCOLS // kw), dproj, _bs((tm, kw), lambda i, j, k: (i, k)),
        big["w_in_main"], _bs((None, d, kw), lambda i, j, k: (0, 0, k)), [((t, d), F32, row)], carry=carry))
    dh0, dnw = _mm(f"mm_du_dt_{layer}", _NT, (t // th, 1, 1), ddt, _bs((th, DT_PAD), lambda i, j, k: (i, 0)),
                   big["w_in_dt"], _bs((None, d, DT_PAD), lambda i, j, k: (0, 0, 0)),
                   [((t, d), F32, half_row), ((1, d), F32, vec)],
                   extras=[(du, half_row), (dh1, half_row), (s["h"], half_row), (w["mix_norm_w"].reshape(1, d), vec)],
                   epilogue=_add_norm_bwd_epilogue, summed=(1,))
    g["mix_norm_w"] = dnw.reshape(d)
    ntk = t // tk
    u_spec = _bs((tk, d), lambda i, j, k: (k, 0))
    dmain = _with_carry((red.ici_carry, red.after_ici) if own else None, lambda carry: _mm(
        f"mm_dw_in_{layer}", _TN, (1, nk, 1), s["u"], _bs((t, d), lambda i, j, k: (0, 0)), dproj,
        _bs((t, d), lambda i, j, k: (0, j)), [((d, PCOLS), F32, _bs((d, d), lambda i, j, k: (0, j)))], carry=carry))
    ddtw = _mm(f"mm_dw_dt_{layer}", _TN, (1, 1, ntk), s["u"], u_spec, ddt, _bs((tk, DT_PAD), lambda i, j, k: (k, 0)),
               [((d, DT_PAD), F32, _bs((d, DT_PAD), lambda i, j, k: (0, 0)))])
    gb["w_in"] = _merge_dw_in(dmain, ddtw)
    if red is not None:
        red.push(layer, {n: gb[n] for n in BIG if not (own and n in EARLY)})
    return dh0, g, gb


BIG = ("w_in", "pool_w_group", "w_pool_up", "w_ssd_out", "w_o", "w_ff1", "w_ff2")
PER_LAYER = ("mix_norm_w", "w_in", "b_gate", "pool_w_group", "pool_scale", "w_pool_up", "conv_w", "conv_b", "dt_bias",
             "a_log", "d_skip", "ssd_norm_w", "w_ssd_out", "w_o", "mlp_norm_w", "w_ff1", "w_ff2")


SMALL_PER_LAYER = tuple(n for n in PER_LAYER if n not in BIG)


SHARD_SHAPE = {"w_in": (1024, SHARD_COLS), "pool_w_group": (4, WG_ROWS, POOL_GDIM), "w_pool_up": (WUP_ROWS, D_MODEL),
               "w_ssd_out": (D_INNER // N_CHIPS, D_MODEL), "w_o": (D_MODEL // N_CHIPS, D_MODEL),
               "w_ff1": (D_MODEL, D_FF // N_CHIPS), "w_ff2": (D_FF // N_CHIPS, D_MODEL)}


def _layer_view(mats):
    big = {n: mats[n].reshape((1, N_CHIPS) + SHARD_SHAPE[n]) for n in BIG if n != "w_in"}
    big["w_in_main"], big["w_in_dt"] = _split_w_in(mats["w_in"])
    return big


def _local_step(x, tgt, weights, small, red=None):
    seq = x.shape[0]
    h = jnp.concatenate([jnp.zeros((N_PAD, D_MODEL), F32), small["meta_tokens"], x], axis=0)
    saved, ws, bigs = [], [], []
    u = None
    for i in range(DEPTH):
        w = {n: small[n][i] for n in SMALL_PER_LAYER}
        w["d_skip_exp"] = jnp.repeat(w["d_skip"], HEAD_DIM)
        big = weights.layer(i)
        next_norm_w = small["mix_norm_w"][i + 1] if i + 1 < DEPTH else None
        h, u, s = _layer_fwd(h, u, w, big, i, weights.stages(i), next_norm_w)
        saved.append(s)
        ws.append(w)
        bigs.append(big)
    loss, dh, g_final = _loss_head(h, small["final_norm_w"], tgt)
    layer_g, layer_gb = [None] * DEPTH, [None] * DEPTH
    for i in reversed(range(DEPTH)):
        dh, layer_g[i], layer_gb[i] = _layer_bwd(dh, saved[i], ws[i], bigs[i], i, red, last=(i == 0))
    grads = {n: jnp.stack([layer_g[i][n] for i in range(DEPTH)]) for n in SMALL_PER_LAYER}
    grads["final_norm_w"] = g_final
    grads["meta_tokens"] = dh[N_PAD:ROW_X]
    return loss, dh[ROW_X:ROW_X + seq], layer_gb, grads


MESH = pl.DeviceIdType.MESH
LANES = 128
ANY = pl.BlockSpec(memory_space=pl.ANY)


def _place():
    x, y, c = lax.axis_index("x"), lax.axis_index("y"), lax.axis_index("c")
    chips = [(1 - x, y), (x, 1 - y), (1 - x, 1 - y)]
    return x, y, c, chips


def _remote(src, dst, send_sem, recv_sem, to):
    return pltpu.make_async_remote_copy(src_ref=src, dst_ref=dst, send_sem=send_sem, recv_sem=recv_sem,
                                        device_id=to, device_id_type=MESH)


class _WeightGatherer:
    GROUPS = {"a": ("w_in",), "b": tuple(n for n in BIG if n != "w_in")}

    def __init__(self, mine):
        self.mine = mine
        self.landing = {}
        self.ready = {}

    def _ici(self, layer, group):
        names = self.GROUPS[group]
        srcs = [self.mine[n] for n in names]
        outs = [jax.ShapeDtypeStruct((1, N_CHIPS) + s.shape[1:], BF16) for s in srcs]

        def make(ins, out, send_sems, recv_sems):
            x, y, c, chips = _place()
            me = 2 * x + y
            cps = []
            for p, ref in enumerate(ins):
                half = ref.shape[1] // 2
                rows = pl.ds(c * half, half)
                cps += [_remote(ref.at[layer, rows, :], out[p].at[0, me, rows, :], send_sems.at[3 * p + j],
                                recv_sems.at[3 * p + j], (*chip, c)) for j, chip in enumerate(chips)]
            return cps

        return _Carry(srcs, outs, 3 * len(names), make)

    def _d2d(self, layer, group):
        names = self.GROUPS[group]
        n = len(names)
        bufs = self.landing[(layer, group)]
        outs = [jax.ShapeDtypeStruct(b.shape, b.dtype) for b in bufs]

        def make(ins, out, send_sems, recv_sems):
            x, y, c, chips = _place()
            me = 2 * x + y
            sibling = (x, y, 1 - c)
            cps = []
            for p in range(n):
                half = out[p].shape[2] // 2
                for j, (cx, cy) in enumerate(chips):
                    blk = out[p].at[0, 2 * cx + cy, pl.ds(c * half, half), :]
                    cps.append(_remote(blk, blk, send_sems.at[4 * p + j], recv_sems.at[4 * p + j], sibling))
                cps.append(_remote(ins[n + p].at[layer], out[p].at[0, me], send_sems.at[4 * p + 3],
                                   recv_sems.at[4 * p + 3], sibling))
            return cps

        return _Carry(list(bufs) + [self.mine[m] for m in names], outs, 4 * n, make, alias={p: p for p in range(n)})

    def _landed(self, layer, group, bufs):
        self.landing[(layer, group)] = bufs

    def _done(self, layer, group, bufs):
        self.ready.setdefault(layer, {}).update(zip(self.GROUPS[group], bufs))

    def layer(self, i):
        if i == 0:
            for g in self.GROUPS:
                self._landed(0, g, _run_carry(self._ici(0, g), f"gather_ici_{g}_0"))
            for g in self.GROUPS:
                self._done(0, g, _run_carry(self._d2d(0, g), f"gather_d2d_{g}_0"))
        return _layer_view(self.ready[i])

    def stages(self, i):
        nxt = i + 1
        if nxt == DEPTH:
            return None
        st = {}
        for g in self.GROUPS:
            st[f"{g}_ici"] = (functools.partial(self._ici, nxt, g), functools.partial(self._landed, nxt, g))
            st[f"{g}_d2d"] = (functools.partial(self._d2d, nxt, g), functools.partial(self._done, nxt, g))
        return st


def _exchange_small(v, reduce, name):
    rows_per = v.shape[0]
    vm = pl.BlockSpec(memory_space=pltpu.VMEM)

    def body(v_ref, out_ref, *scratch):
        if reduce:
            land_ref, send_sems, recv_sems, local_sem = scratch
        else:
            land_ref = out_ref
            send_sems, recv_sems, local_sem = scratch
        x, y, c, chips = _place()
        me, sibling = (x, y, c), (x, y, 1 - c)

        def rows(px, py, pc):
            return land_ref.at[4 * px + 2 * py + pc]

        def copy(k, block, to, src=None):
            return _remote(rows(*block) if src is None else src, rows(*block), send_sems.at[k], recv_sems.at[k], to)

        mine = pltpu.make_async_copy(v_ref, rows(*me), local_sem)
        mine.start()
        first = [copy(0, me, sibling, src=v_ref)]
        first += [copy(1 + j, me, (*chip, c), src=v_ref) for j, chip in enumerate(chips)]
        for cp in first:
            cp.start()
        passed = [copy(4 + j, (*chip, c), sibling) for j, chip in enumerate(chips)]
        for j, chip in enumerate(chips):
            copy(1 + j, (*chip, c), me).wait_recv()
            passed[j].start()
        copy(0, sibling, me).wait_recv()
        for j, chip in enumerate(chips):
            copy(4 + j, (*chip, 1 - c), me).wait_recv()
        for cp in first + passed:
            cp.wait_send()
        mine.wait()
        if reduce:
            acc = land_ref[0]
            for d in range(1, 8):
                acc = acc + land_ref[d]
            out_ref[...] = acc

    sems = [pltpu.SemaphoreType.DMA((7,)), pltpu.SemaphoreType.DMA((7,)), pltpu.SemaphoreType.DMA]
    if reduce:
        out_shape = jax.ShapeDtypeStruct((rows_per, LANES), F32)
        scratch = [pltpu.VMEM((8, rows_per, LANES), F32)] + sems
    else:
        out_shape = jax.ShapeDtypeStruct((8, rows_per, LANES), F32)
        scratch = sems
    return pl.pallas_call(
        body, name=name, in_specs=[vm], out_specs=vm, out_shape=out_shape, scratch_shapes=scratch,
        compiler_params=pltpu.CompilerParams(vmem_limit_bytes=VMEM_LIMIT),
    )(v)


def _run_carry(carry, name):
    c_in, c_out, c_sem = carry.specs()
    n_in, n_out = len(c_in), len(c_out)

    def body(*refs):
        cps = carry.make(refs[:n_in], refs[n_in:n_in + n_out], *refs[n_in + n_out:])
        for cp in cps:
            cp.start()
        for cp in cps:
            cp.wait()

    return pl.pallas_call(body, name=name, in_specs=c_in, out_specs=c_out, out_shape=carry.outs,
                          scratch_shapes=c_sem, input_output_aliases=carry.aliases(0, 0))(*carry.ins)


def _row_tile(rows, last, itemsize=4, budget=2 * 1024 * 1024):
    return _tile(rows, tuple(t for t in (2048, 1024, 512, 256, 128, 64, 32, 16) if t * last * itemsize <= budget))


def _rs_add(g, got, core, name):
    _, half, last = got.shape
    tr = _row_tile(half, last)
    nb = half // tr

    def body(c_ref, g_ref, got_ref, o_ref):
        o_ref[...] = (g_ref[...] + got_ref[...]).astype(BF16)

    blk = pl.BlockSpec((None, tr, last), lambda s, i, c_ref: (s, i, 0))
    return pl.pallas_call(
        body, name=name,
        grid_spec=pltpu.PrefetchScalarGridSpec(
            num_scalar_prefetch=1, grid=(N_CHIPS, nb),
            in_specs=[pl.BlockSpec((None, tr, last), lambda s, i, c_ref: (s, c_ref[0] * nb + i, 0)), blk],
            out_specs=blk),
        out_shape=jax.ShapeDtypeStruct(got.shape, BF16),
        compiler_params=_cparams(("parallel", "parallel")),
    )(core.reshape(1).astype(jnp.int32), g, got)


def _rs_sum(own, got, acc, layer, chip, core, name):
    _, half, last = own.shape
    tr = _row_tile(half, last, budget=1024 * 1024)
    nb = half // tr
    ni = 0 if acc is None else 1

    def body(k_ref, c_ref, own_ref, got_ref, *rest):
        o_ref = rest[-1]
        tot = own_ref[...].astype(F32)
        for j in range(3):
            tot = tot + got_ref[j].astype(F32)
        o_ref[...] = tot

    return pl.pallas_call(
        body, name=name,
        grid_spec=pltpu.PrefetchScalarGridSpec(
            num_scalar_prefetch=2, grid=(nb,),
            in_specs=[pl.BlockSpec((None, tr, last), lambda i, k_ref, c_ref: (k_ref[0], i, 0)),
                      pl.BlockSpec((3, tr, last), lambda i, k_ref, c_ref: (0, i, 0))]
            + [pl.BlockSpec(memory_space=pl.ANY)] * ni,
            out_specs=pl.BlockSpec((None, tr, last), lambda i, k_ref, c_ref: (layer, c_ref[0] * nb + i, 0))),
        out_shape=jax.ShapeDtypeStruct((DEPTH, 2 * half, last), F32),
        input_output_aliases={4: 0} if ni else {},
        compiler_params=_cparams(("parallel",)),
    )(chip.reshape(1).astype(jnp.int32), core.reshape(1).astype(jnp.int32), own, got, *([acc] if ni else []))


def _rs_share(arrs):
    n = len(arrs)

    def body(*refs):
        out = refs[n:2 * n]
        send_sems, recv_sems = refs[2 * n:]
        x, y, c, _ = _place()
        sibling = (x, y, 1 - c)
        cps = []
        for p in range(n):
            half = out[p].shape[1] // 2
            mine = out[p].at[:, pl.ds(c * half, half), :]
            cps.append(_remote(mine, mine, send_sems.at[p], recv_sems.at[p], sibling))
        for cp in cps:
            cp.start()
        for p, cp in enumerate(cps):
            cp.wait_send()
            half = out[p].shape[1] // 2
            other = out[p].at[:, pl.ds((1 - c) * half, half), :]
            _remote(other, other, send_sems.at[p], recv_sems.at[p], sibling).wait_recv()

    return pl.pallas_call(
        body, name="rs_share", in_specs=[ANY] * n, out_specs=[ANY] * n,
        out_shape=[jax.ShapeDtypeStruct(a.shape, a.dtype) for a in arrs],
        input_output_aliases={p: p for p in range(n)},
        scratch_shapes=[pltpu.SemaphoreType.DMA((n,)), pltpu.SemaphoreType.DMA((n,))],
    )(*arrs)


class _GradReducer:
    def __init__(self, chip, core):
        self.chip, self.core = chip, core
        self.pending = None
        self.sums = None
        self.acc = {n: None for n in BIG}

    def push(self, layer, gb):
        assert self.pending is None
        self.pending = (layer, list(gb), list(gb.values()))

    def swap_carry(self):
        if self.pending is None:
            return None
        _, _, gs = self.pending
        outs = [jax.ShapeDtypeStruct((g.shape[0], g.shape[1] // 2, g.shape[2]), F32) for g in gs]

        def make(ins, out, send_sems, recv_sems):
            x, y, c, _ = _place()
            cps = []
            for p, ref in enumerate(ins):
                half = ref.shape[1] // 2
                cps.append(_remote(ref.at[:, pl.ds((1 - c) * half, half), :], out[p], send_sems.at[p], recv_sems.at[p],
                                   (x, y, 1 - c)))
            return cps

        return _Carry(gs, outs, len(gs), make)

    def after_swap(self, got):
        layer, names, gs = self.pending
        self.pending = None
        self.sums = (layer, names, [_rs_add(g, a, self.core, f"rs_add_{n}_{layer}") for n, g, a in zip(names, gs, got)])

    def ici_carry(self):
        if self.sums is None:
            return None
        _, _, ps = self.sums
        outs = [jax.ShapeDtypeStruct((3,) + p.shape[1:], BF16) for p in ps]

        def make(ins, out, send_sems, recv_sems):
            x, y, c, chips = _place()
            return [_remote(ins[p].at[2 * cx + cy], out[p].at[j], send_sems.at[3 * p + j], recv_sems.at[3 * p + j],
                            (cx, cy, c)) for p in range(len(ins)) for j, (cx, cy) in enumerate(chips)]

        return _Carry(ps, outs, 3 * len(ps), make)

    def after_ici(self, got):
        layer, names, ps = self.sums
        self.sums = None
        for n, p, a in zip(names, ps, got):
            self.acc[n] = _rs_sum(p, a, self.acc[n], layer, self.chip, self.core, f"rs_sum_{n}_{layer}")

    def finish(self):
        if self.pending is not None:
            self.after_swap(_run_carry(self.swap_carry(), "rs_swap_last"))
        if self.sums is not None:
            self.after_ici(_run_carry(self.ici_carry(), "rs_ici_last"))
        return dict(zip(BIG, _rs_share([self.acc[n] for n in BIG])))


def _adamw(w, g, m, v, name):
    shape = w.shape
    if len(shape) == 2:
        shape3 = (1,) + shape
    else:
        shape3 = (-1,) + shape[-2:]
    w2, g2, m2, v2 = (a.reshape(shape3) for a in (w, g, m, v))
    lead, rows, last = w2.shape
    tr = max([t for t in range(8, rows + 1, 8) if rows % t == 0 and t * last * 4 <= 2 * 1024 * 1024] or [rows])

    def body(w_ref, g_ref, m_ref, v_ref, d_ref, nm_ref, nv_ref):
        gv = g_ref[...]
        mn = ADAM_B1 * m_ref[...] + (1.0 - ADAM_B1) * gv
        vn = ADAM_B2 * v_ref[...] + (1.0 - ADAM_B2) * (gv * gv)
        m_hat = mn / (1.0 - ADAM_B1 ** ADAM_STEP)
        v_hat = vn / (1.0 - ADAM_B2 ** ADAM_STEP)
        d_ref[...] = -ADAM_LR * (m_hat / (jnp.sqrt(v_hat) + ADAM_EPS) + ADAM_WD * w_ref[...])
        nm_ref[...] = mn
        nv_ref[...] = vn

    blk = pl.BlockSpec((None, tr, last), lambda a, i: (a, i, 0))
    outs = pl.pallas_call(
        body, name=name, grid=(lead, rows // tr), in_specs=[blk] * 4, out_specs=[blk] * 3,
        out_shape=[jax.ShapeDtypeStruct((lead, rows, last), F32)] * 3,
        compiler_params=_cparams(("parallel", "parallel")),
    )(w2, g2, m2, v2)
    return tuple(o.reshape(shape) for o in outs)


SHARD_AXIS = {"conv_w": 2, "meta_tokens": 1}
SMALL_SHARDED = ("conv_w", "meta_tokens")
REPLICATED = ("mix_norm_w", "b_gate", "pool_scale", "conv_b", "dt_bias", "a_log", "d_skip", "ssd_norm_w", "mlp_norm_w",
              "final_norm_w")


def _flatten(arrs, dtype, row_mult):
    flat = jnp.concatenate([a.astype(dtype).reshape(-1) for a in arrs])
    n = flat.shape[0]
    rows = -(-n // (LANES * row_mult)) * row_mult
    return jnp.pad(flat, (0, rows * LANES - n)).reshape(rows, LANES)


def _unflatten(flat2d, shapes):
    flat = flat2d.reshape(-1)
    out, off = [], 0
    for sh in shapes:
        n = 1
        for d in sh:
            n *= d
        out.append(flat[off:off + n].reshape(sh))
        off += n
    return out


def kernel(x, meta_tokens, mix_norm_w, w_in, b_gate, pool_w_group, pool_scale, w_pool_up, conv_w, conv_b, dt_bias, a_log, d_skip, ssd_norm_w, w_ssd_out, w_o, mlp_norm_w, w_ff1, w_ff2, final_norm_w, loss_target, m_meta_tokens, m_mix_norm_w, m_w_in, m_b_gate, m_pool_w_group, m_pool_scale, m_w_pool_up, m_conv_w, m_conv_b, m_dt_bias, m_a_log, m_d_skip, m_ssd_norm_w, m_w_ssd_out, m_w_o, m_mlp_norm_w, m_w_ff1, m_w_ff2, m_final_norm_w, v_meta_tokens, v_mix_norm_w, v_w_in, v_b_gate, v_pool_w_group, v_pool_scale, v_w_pool_up, v_conv_w, v_conv_b, v_dt_bias, v_a_log, v_d_skip, v_ssd_norm_w, v_w_ssd_out, v_w_o, v_mlp_norm_w, v_w_ff1, v_w_ff2, v_final_norm_w):
    names = ("meta_tokens",) + PER_LAYER + ("final_norm_w",)
    par = dict(meta_tokens=meta_tokens, mix_norm_w=mix_norm_w, w_in=w_in, b_gate=b_gate, pool_w_group=pool_w_group,
               pool_scale=pool_scale, w_pool_up=w_pool_up, conv_w=conv_w, conv_b=conv_b, dt_bias=dt_bias, a_log=a_log,
               d_skip=d_skip, ssd_norm_w=ssd_norm_w, w_ssd_out=w_ssd_out, w_o=w_o, mlp_norm_w=mlp_norm_w, w_ff1=w_ff1,
               w_ff2=w_ff2, final_norm_w=final_norm_w)
    mom = dict(meta_tokens=m_meta_tokens, mix_norm_w=m_mix_norm_w, w_in=m_w_in, b_gate=m_b_gate, pool_w_group=m_pool_w_group,
               pool_scale=m_pool_scale, w_pool_up=m_w_pool_up, conv_w=m_conv_w, conv_b=m_conv_b, dt_bias=m_dt_bias,
               a_log=m_a_log, d_skip=m_d_skip, ssd_norm_w=m_ssd_norm_w, w_ssd_out=m_w_ssd_out, w_o=m_w_o,
               mlp_norm_w=m_mlp_norm_w, w_ff1=m_w_ff1, w_ff2=m_w_ff2, final_norm_w=m_final_norm_w)
    var = dict(meta_tokens=v_meta_tokens, mix_norm_w=v_mix_norm_w, w_in=v_w_in, b_gate=v_b_gate, pool_w_group=v_pool_w_group,
               pool_scale=v_pool_scale, w_pool_up=v_w_pool_up, conv_w=v_conv_w, conv_b=v_conv_b, dt_bias=v_dt_bias,
               a_log=v_a_log, d_skip=v_d_skip, ssd_norm_w=v_ssd_norm_w, w_ssd_out=v_w_ssd_out, w_o=v_w_o,
               mlp_norm_w=v_mlp_norm_w, w_ff1=v_w_ff1, w_ff2=v_w_ff2, final_norm_w=v_final_norm_w)
    chip = 2 * lax.axis_index("x") + lax.axis_index("y")
    core = lax.axis_index("c")

    weights = _WeightGatherer({n: par[n].astype(BF16).reshape(DEPTH, -1, par[n].shape[-1]) for n in BIG})
    small = {n: par[n] for n in REPLICATED}
    small_shapes = [par[n].shape for n in SMALL_SHARDED]
    got_small = _exchange_small(_flatten([par[n] for n in SMALL_SHARDED], F32, 8), False, "gather_small")
    pieces = [_unflatten(got_small[2 * k], small_shapes) for k in range(N_CHIPS)]
    for j, n in enumerate(SMALL_SHARDED):
        small[n] = jnp.concatenate([pieces[k][j] for k in range(N_CHIPS)], axis=SHARD_AXIS[n])

    reducer = _GradReducer(chip, core)
    loss, dx, _, grads = _local_step(x[0], loss_target[0], weights, small, reducer)

    small_names = REPLICATED + SMALL_SHARDED
    red = _exchange_small(_flatten([grads[n] for n in small_names] + [loss.reshape(1)], F32, 8), True, "allreduce_small")
    *parts, loss = _unflatten(red, [grads[n].shape for n in small_names] + [(1,)])
    loss = loss[0]
    gsum = dict(zip(small_names, parts))
    for n in SMALL_SHARDED:
        width = par[n].shape[SHARD_AXIS[n]]
        gsum[n] = lax.dynamic_slice_in_dim(gsum[n], chip * width, width, axis=SHARD_AXIS[n])

    gsum.update({n: g.reshape(par[n].shape) for n, g in reducer.finish().items()})

    delta, new_m, new_v = {}, {}, {}
    for n in BIG:
        if n == "w_in":
            tr = lambda a: jnp.swapaxes(a, 1, 2)
            outs = _adamw(tr(par[n]), tr(gsum[n]), tr(mom[n]), tr(var[n]), "adamw_" + n)
            delta[n], new_m[n], new_v[n] = (tr(o) for o in outs)
        else:
            delta[n], new_m[n], new_v[n] = _adamw(par[n], gsum[n], mom[n], var[n], "adamw_" + n)
    rest = [n for n in names if n not in BIG]
    shapes = [par[n].shape for n in rest]
    packed = [_flatten([d[n] for n in rest], F32, 8) for d in (par, gsum, mom, var)]
    for d, flat in zip((delta, new_m, new_v), _adamw(*packed, "adamw_small")):
        d.update(zip(rest, _unflatten(flat, shapes)))

    order = ("meta_tokens", "mix_norm_w", "w_in", "b_gate", "pool_w_group", "pool_scale", "w_pool_up", "conv_w", "conv_b",
             "dt_bias", "a_log", "d_skip", "ssd_norm_w", "w_ssd_out", "w_o", "mlp_norm_w", "w_ff1", "w_ff2", "final_norm_w")
    return (loss, dx[None], *[gsum[n] for n in order], *[delta[n] for n in order], *[new_m[n] for n in order],
            *[new_v[n] for n in order])
```

```python
import functools

import jax
import jax.numpy as jnp
from jax import lax
from jax.experimental import pallas as pl
from jax.experimental.pallas import tpu as pltpu

F32 = jnp.float32
BF16 = jnp.bfloat16

D_MODEL = 1024
DEPTH = 4
N_META = 16
N_PAD = 112
ROW_X = N_PAD + N_META
POOL_WINDOWS = (2, 4, 8, 16)
POOL_GDIM = 256
D_INNER = 2048
N_HEADS = 32
HEAD_DIM = 64
N_GROUPS = 8
HEADS_PER_GROUP = 4
GROUP_W = HEADS_PER_GROUP * HEAD_DIM
D_STATE = 128
CHUNK = 128
D_XBC = 4096
D_FF = 4096
EPS = 1e-5
OFF_Z, OFF_XBC, OFF_DT, OFF_GATE, IN_COLS = 1024, 3072, 7168, 7200, 9248
PZ, PGATE, PPOOL, PXBC, PCOLS = 0, 2048, 4096, 5120, 9216
DT_PAD = 128

ADAM_LR, ADAM_B1, ADAM_B2, ADAM_EPS, ADAM_WD, ADAM_STEP = 0.001, 0.9, 0.999, 1e-08, 0.01, 10

VMEM_LIMIT = 56 * 1024 * 1024

_NN = (((1,), (0,)), ((), ()))
_NT = (((1,), (1,)), ((), ()))
_TN = (((0,), (0,)), ((), ()))


def _dot(a, b, dn=_NN):
    return lax.dot_general(a, b, dn, preferred_element_type=F32)


def _cparams(sem):
    return pltpu.CompilerParams(dimension_semantics=sem, vmem_limit_bytes=VMEM_LIMIT)


def _tile(n, cands):
    for c in cands:
        if n % c == 0:
            return c
    raise ValueError(f"no tile for {n} in {cands}")


def _sigmoid(x):
    return 0.5 * jnp.tanh(0.5 * x) + 0.5


def _softplus(x):
    return jnp.maximum(x, 0.0) + jnp.log(1.0 + jnp.exp(-jnp.abs(x)))


def _bs(shape, fn):
    return pl.BlockSpec(shape, fn)


class _Carry:
    def __init__(self, ins, outs, n_sems, make, alias=None):
        self.ins, self.outs, self.n_sems, self.make = list(ins), list(outs), n_sems, make
        self.alias = dict(alias or {})

    def aliases(self, first_in, first_out):
        return {first_in + i: first_out + o for i, o in self.alias.items()}

    def specs(self):
        hbm = pl.BlockSpec(memory_space=pl.ANY)
        return [hbm] * len(self.ins), [hbm] * len(self.outs), [pltpu.SemaphoreType.DMA((self.n_sems,))] * 2

    def at_edges(self, grid, in_refs, out_refs, sems):
        ids = [pl.program_id(a) for a in range(len(grid))]
        first = functools.reduce(jnp.logical_and, [i == 0 for i in ids])
        last = functools.reduce(jnp.logical_and, [i == n - 1 for i, n in zip(ids, grid)])

        def begin():
            @pl.when(first)
            def _():
                for cp in self.make(in_refs, out_refs, *sems):
                    cp.start()

        def end():
            @pl.when(last)
            def _():
                for cp in self.make(in_refs, out_refs, *sems):
                    cp.wait()

        return begin, end


def _sum_into(o_ref, v, first):
    @pl.when(first)
    def _():
        o_ref[...] = v

    @pl.when(jnp.logical_not(first))
    def _():
        o_ref[...] += v


def _mm(name, dn, grid, a, a_spec, b, b_spec, outs, *, extras=(), epilogue=None, carry=None, summed=()):
    nk = grid[2]
    ne, no = len(extras), len(outs)
    blk = tuple(d for d in outs[0][2].block_shape if d is not None)
    c_in, c_out, c_sem = carry.specs() if carry is not None else ([], [], [])
    nci, nco = len(c_in), len(c_out)

    def body(a_ref, b_ref, *rest):
        e_refs, o_refs = rest[:ne], rest[ne + nci:ne + nci + no]
        first_row_tile = pl.program_id(0) == 0
        if carry is not None:
            begin, end = carry.at_edges(grid, rest[ne:ne + nci], rest[ne + nci + no:ne + nci + no + nco], rest[-2:])
            begin()
        if len(b_ref.shape) == 3:
            rows = b_ref.shape[1]
            p = _dot(a_ref[:, :rows].astype(BF16), b_ref[0].astype(BF16), dn)
            for s in range(1, b_ref.shape[0]):
                p = p + _dot(a_ref[:, s * rows:(s + 1) * rows].astype(BF16), b_ref[s].astype(BF16), dn)
        else:
            p = _dot(a_ref[...].astype(BF16), b_ref[...].astype(BF16), dn)

        def finish(acc):
            outs = epilogue(acc, *[e[...] for e in e_refs]) if epilogue is not None else (acc,)
            for idx, (o, v) in enumerate(zip(o_refs, outs)):
                if idx in summed:
                    _sum_into(o, v, first_row_tile)
                else:
                    o[...] = v.astype(o.dtype)

        if nk == 1:
            finish(p)
        else:
            acc_ref = rest[ne + nci + no + nco]
            kk = pl.program_id(2)

            @pl.when(kk == 0)
            def _():
                acc_ref[...] = p

            @pl.when(kk > 0)
            def _():
                acc_ref[...] += p

            @pl.when(kk == nk - 1)
            def _():
                finish(acc_ref[...])
        if carry is not None:
            end()

    res = pl.pallas_call(
        body,
        name=name,
        grid=grid,
        in_specs=[a_spec, b_spec, *[s for _, s in extras]] + c_in,
        out_specs=[o[2] for o in outs] + c_out,
        out_shape=[jax.ShapeDtypeStruct(o[0], o[1]) for o in outs] + (carry.outs if carry is not None else []),
        scratch_shapes=([pltpu.VMEM(blk, F32)] if nk > 1 else []) + c_sem,
        input_output_aliases=carry.aliases(2 + ne, no) if carry is not None else {},
        compiler_params=_cparams(("arbitrary",) * 3 if carry is not None or summed else ("parallel", "parallel", "arbitrary")),
    )(a, b, *[e for e, _ in extras], *(carry.ins if carry is not None else []))
    if carry is not None:
        return res[:no], res[no:]
    return res[0] if no == 1 else res


def _norm_fwd(h, w, name):
    t, d = h.shape
    tm = _tile(t, (1056, 384, 128))

    def body(h_ref, w_ref, u_ref):
        x = h_ref[...]
        r = lax.rsqrt(jnp.mean(x * x, axis=-1, keepdims=True) + EPS)
        u_ref[...] = (x * r * w_ref[...]).astype(BF16)

    return pl.pallas_call(
        body, name=name, grid=(t // tm,),
        in_specs=[pl.BlockSpec((tm, d), lambda i: (i, 0)), pl.BlockSpec((1, d), lambda i: (0, 0))],
        out_specs=pl.BlockSpec((tm, d), lambda i: (i, 0)),
        out_shape=jax.ShapeDtypeStruct((t, d), BF16),
        compiler_params=_cparams(("parallel",)),
    )(h, w.reshape(1, d))


def _loss_head(h, w, tgt):
    t, d = h.shape
    tm = CHUNK
    nb = ROW_X // tm

    def body(h_ref, w_ref, t_ref, loss_ref, dh_ref, dw_ref):
        i = pl.program_id(0)
        x = h_ref[...]
        r = lax.rsqrt(jnp.mean(x * x, axis=-1, keepdims=True) + EPS)
        xhat = x * r
        wv = w_ref[...]
        live = i >= nb
        err = jnp.where(live, xhat * wv - t_ref[...], 0.0)
        dout = err * (1.0 / d)
        g = dout * wv
        dh_ref[...] = r * (g - xhat * jnp.mean(g * xhat, axis=-1, keepdims=True))

        @pl.when(i == 0)
        def _():
            loss_ref[...] = jnp.zeros_like(loss_ref)
            dw_ref[...] = jnp.zeros_like(dw_ref)

        loss_ref[...] += 0.5 * jnp.sum(jnp.sum(err * err, axis=-1, keepdims=True), axis=0, keepdims=True) * (1.0 / d)
        dw_ref[...] += jnp.sum(dout * xhat, axis=0, keepdims=True)

    row = pl.BlockSpec((tm, d), lambda i: (i, 0))
    vec = pl.BlockSpec((1, d), lambda i: (0, 0))
    loss, dh, dw = pl.pallas_call(
        body, name="loss_head", grid=(t // tm,),
        in_specs=[row, vec, pl.BlockSpec((tm, d), lambda i: (jnp.maximum(i - nb, 0), 0))],
        out_specs=[pl.BlockSpec((1, 1), lambda i: (0, 0)), row, vec],
        out_shape=[jax.ShapeDtypeStruct((1, 1), F32), jax.ShapeDtypeStruct((t, d), F32), jax.ShapeDtypeStruct((1, d), F32)],
        compiler_params=_cparams(("arbitrary",)),
    )(h, w.reshape(1, d), tgt)
    return loss[0, 0], dh, dw.reshape(d)


def _gate_fwd(proj, b_gate, y_pool, y_ssd):
    t = proj.shape[0]
    d = D_MODEL
    tm = _tile(t, (528, 384, 128))

    def body(gp_ref, gs_ref, bp_ref, bs_ref, yp_ref, ys_ref, o_ref):
        gp = _sigmoid(gp_ref[...] + bp_ref[...])
        gs = _sigmoid(gs_ref[...] + bs_ref[...])
        o_ref[...] = (gp * yp_ref[...] + gs * ys_ref[...]).astype(BF16)

    row = pl.BlockSpec((tm, d), lambda i: (i, 0))
    return pl.pallas_call(
        body, name="gate_fwd", grid=(t // tm,),
        in_specs=[pl.BlockSpec((tm, d), lambda i: (i, PGATE // d)), pl.BlockSpec((tm, d), lambda i: (i, PGATE // d + 1)),
                  pl.BlockSpec((1, d), lambda i: (0, 0)), pl.BlockSpec((1, d), lambda i: (0, 1)), row, row],
        out_specs=row, out_shape=jax.ShapeDtypeStruct((t, d), BF16),
        compiler_params=_cparams(("parallel",)),
    )(proj, proj, b_gate.reshape(1, 2 * d), b_gate.reshape(1, 2 * d), y_pool, y_ssd)


def _gate_bwd(dmerged, proj, b_gate, y_pool, y_ssd):
    t = proj.shape[0]
    d = D_MODEL
    tm = _tile(t, (384, 128))

    def body(dm_ref, gp_ref, gs_ref, bp_ref, bs_ref, yp_ref, ys_ref, dyp_ref, dys_ref, dg_ref, db_ref):
        dm = dm_ref[...]
        gp = _sigmoid(gp_ref[...] + bp_ref[...])
        gs = _sigmoid(gs_ref[...] + bs_ref[...])
        dyp_ref[...] = (dm * gp).astype(BF16)
        dys_ref[...] = (dm * gs).astype(BF16)
        dgp = dm * yp_ref[...] * gp * (1.0 - gp)
        dgs = dm * ys_ref[...] * gs * (1.0 - gs)
        dg_ref[:, :d] = dgp.astype(BF16)
        dg_ref[:, d:] = dgs.astype(BF16)

        @pl.when(pl.program_id(0) == 0)
        def _():
            db_ref[...] = jnp.zeros_like(db_ref)

        db_ref[:, :d] += jnp.sum(dgp, axis=0, keepdims=True)
        db_ref[:, d:] += jnp.sum(dgs, axis=0, keepdims=True)

    row = pl.BlockSpec((tm, d), lambda i: (i, 0))
    dyp, dys, dg, db = pl.pallas_call(
        body, name="gate_bwd", grid=(t // tm,),
        in_specs=[row, pl.BlockSpec((tm, d), lambda i: (i, PGATE // d)), pl.BlockSpec((tm, d), lambda i: (i, PGATE // d + 1)),
                  pl.BlockSpec((1, d), lambda i: (0, 0)), pl.BlockSpec((1, d), lambda i: (0, 1)), row, row],
        out_specs=[row, row, pl.BlockSpec((tm, 2 * d), lambda i: (i, PGATE // (2 * d))),
                   pl.BlockSpec((1, 2 * d), lambda i: (0, 0))],
        out_shape=[jax.ShapeDtypeStruct((t, d), BF16), jax.ShapeDtypeStruct((t, d), BF16),
                   jax.ShapeDtypeStruct((t, PCOLS), BF16), jax.ShapeDtypeStruct((1, 2 * d), F32)],
        compiler_params=_cparams(("arbitrary",)),
    )(dmerged, proj, proj, b_gate.reshape(1, 2 * d), b_gate.reshape(1, 2 * d), y_pool, y_ssd)
    return dyp, dys, dg, db.reshape(2 * d)


POOL_HALO = 16


def _pool_counts(row0, n, win):
    pos1 = row0 + lax.broadcasted_iota(jnp.int32, (n, 1), 0) - (N_PAD - 1)
    return jnp.clip(pos1, 1, win).astype(F32)


N_CHIPS = 4
WG_ROWS = POOL_GDIM // N_CHIPS
WUP_ROWS = D_MODEL // N_CHIPS


def _group_w(wg_ref, g):
    return jnp.concatenate([wg_ref[k, g] for k in range(N_CHIPS)], axis=0)


def _pool_w_specs(layer):
    return [pl.BlockSpec((None, N_CHIPS, len(POOL_WINDOWS), WG_ROWS, POOL_GDIM), lambda i: (0, 0, 0, 0, 0)),
            pl.BlockSpec((None, N_CHIPS, WUP_ROWS, D_MODEL), lambda i: (0, 0, 0, 0))]


def _pool_fwd(proj, wg, scale, wup, layer):
    t = proj.shape[0]
    d = D_MODEL
    tm = _tile(t, (384, 128))
    hb = tm // POOL_HALO

    def body(u_ref, halo_ref, sc_ref, wg_ref, wup_ref, pooled_ref, yg_ref, ypm_ref, yp_ref):
        i = pl.program_id(0)
        x = u_ref[...]
        halo = jnp.where(i > 0, halo_ref[...], 0.0)
        xc = jnp.concatenate([halo, x], axis=0)
        for g, win in enumerate(POOL_WINDOWS):
            sl = slice(g * POOL_GDIM, (g + 1) * POOL_GDIM)
            s = xc[:, sl]
            k = 1
            while k < win:
                s = s + pltpu.roll(s, k, axis=0)
                k *= 2
            pooled = s[POOL_HALO:] / _pool_counts(i * tm, tm, win) - x[:, sl]
            pb = pooled.astype(BF16)
            pooled_ref[:, sl] = pb
            yg_ref[:, sl] = _dot(pb, _group_w(wg_ref, g))
        ypm = (yg_ref[...] * sc_ref[...]).astype(BF16)
        ypm_ref[...] = ypm
        acc = _dot(ypm[:, :WUP_ROWS], wup_ref[0])
        for k in range(1, N_CHIPS):
            acc = acc + _dot(ypm[:, k * WUP_ROWS:(k + 1) * WUP_ROWS], wup_ref[k])
        yp_ref[...] = acc

    row = pl.BlockSpec((tm, d), lambda i: (i, 0))
    return pl.pallas_call(
        body, name=f"pool_fwd_{layer}", grid=(t // tm,),
        in_specs=[pl.BlockSpec((tm, d), lambda i: (i, PPOOL // d)),
                  pl.BlockSpec((POOL_HALO, d), lambda i: (jnp.maximum(i * hb - 1, 0), PPOOL // d)),
                  pl.BlockSpec((1, d), lambda i: (0, 0))] + _pool_w_specs(layer),
        out_specs=[row, row, row, row],
        out_shape=[jax.ShapeDtypeStruct((t, d), BF16), jax.ShapeDtypeStruct((t, d), F32),
                   jax.ShapeDtypeStruct((t, d), BF16), jax.ShapeDtypeStruct((t, d), F32)],
        compiler_params=_cparams(("parallel",)),
    )(proj, proj, scale.reshape(1, d), wg, wup)


def _pool_bwd_a(dy_pool, yg, scale, wg, wup, layer):
    t, d = yg.shape
    tm = _tile(t, (384, 128))

    def body(dy_ref, yg_ref, sc_ref, wg_ref, wup_ref, q_ref, dyg_ref, dsc_ref):
        dy = dy_ref[...]
        dypm = jnp.concatenate([_dot(dy, wup_ref[k], _NT) for k in range(N_CHIPS)], axis=1)

        @pl.when(pl.program_id(0) == 0)
        def _():
            dsc_ref[...] = jnp.zeros_like(dsc_ref)

        dsc_ref[...] += jnp.sum(dypm * yg_ref[...], axis=0, keepdims=True)
        dyg = (dypm * sc_ref[...]).astype(BF16)
        dyg_ref[...] = dyg
        for g in range(len(POOL_WINDOWS)):
            sl = slice(g * POOL_GDIM, (g + 1) * POOL_GDIM)
            q_ref[:, sl] = _dot(dyg[:, sl], _group_w(wg_ref, g), _NT)

    row = pl.BlockSpec((tm, d), lambda i: (i, 0))
    vec = pl.BlockSpec((1, d), lambda i: (0, 0))
    q, dyg, dsc = pl.pallas_call(
        body, name=f"pool_bwd_a_{layer}", grid=(t // tm,),
        in_specs=[row, row, vec] + _pool_w_specs(layer),
        out_specs=[row, row, vec],
        out_shape=[jax.ShapeDtypeStruct((t, d), F32), jax.ShapeDtypeStruct((t, d), BF16), jax.ShapeDtypeStruct((1, d), F32)],
        compiler_params=_cparams(("arbitrary",)),
    )(dy_pool, yg, scale.reshape(1, d), wg, wup)
    return q, dyg, dsc.reshape(d)


def _pool_bwd_b(q, dproj):
    t, d = q.shape
    tm = _tile(t, (384, 128))
    hb = tm // POOL_HALO
    nt = t // tm
    n = tm + POOL_HALO

    def body(q_ref, halo_ref, _, o_ref):
        i = pl.program_id(0)
        qv = q_ref[...]
        halo = jnp.where(i < nt - 1, halo_ref[...], 0.0)
        qc = jnp.concatenate([qv, halo], axis=0)
        for g, win in enumerate(POOL_WINDOWS):
            sl = slice(g * POOL_GDIM, (g + 1) * POOL_GDIM)
            s = qc[:, sl] / _pool_counts(i * tm, n, win)
            k = 1
            while k < win:
                s = s + pltpu.roll(s, n - k, axis=0)
                k *= 2
            o_ref[:, sl] = (s[:tm] - qv[:, sl]).astype(BF16)

    row = pl.BlockSpec((tm, d), lambda i: (i, 0))
    return pl.pallas_call(
        body, name="pool_bwd_b", grid=(nt,),
        in_specs=[row, pl.BlockSpec((POOL_HALO, d), lambda i: (jnp.minimum((i + 1) * hb, t // POOL_HALO - 1), 0)),
                  pl.BlockSpec(memory_space=pl.ANY)],
        out_specs=pl.BlockSpec((tm, d), lambda i: (i, PPOOL // d)), out_shape=jax.ShapeDtypeStruct(dproj.shape, BF16),
        input_output_aliases={2: 0},
        compiler_params=_cparams(("parallel",)),
    )(q, q, dproj)


def _pool_dwg(pooled, dyg, layer):
    t, d = pooled.shape
    tk = _tile(t, (1056, 384, 128))
    nk = t // tk
    gd = POOL_GDIM
    ng = d // gd

    def body(p_ref, g_ref, o_ref):
        @pl.when(pl.program_id(1) == 0)
        def _():
            o_ref[...] = jnp.zeros_like(o_ref)

        part = _dot(p_ref[...], g_ref[...], _TN)
        for k in range(N_CHIPS):
            o_ref[k] += part[k * WG_ROWS:(k + 1) * WG_ROWS]

    blk = pl.BlockSpec((tk, gd), lambda g, k: (k, g))
    return pl.pallas_call(
        body, name=f"pool_dwg_{layer}", grid=(ng, nk), in_specs=[blk, blk],
        out_specs=pl.BlockSpec((N_CHIPS, None, WG_ROWS, gd), lambda g, k: (0, g, 0, 0)),
        out_shape=jax.ShapeDtypeStruct((N_CHIPS, ng, WG_ROWS, gd), F32),
        compiler_params=_cparams(("parallel", "arbitrary")),
    )(pooled, dyg)


CONV_W = 4
CONV_HALO = 8
XBC_BLK = PXBC // 1024


def _conv_fwd(proj, conv_w, conv_b):
    t = proj.shape[0]
    cw = 1024
    tm = _tile(t, (1056, 384, 128))
    hb = tm // CONV_HALO

    def body(x_ref, halo_ref, w_ref, b_ref, o_ref, pre_ref):
        i = pl.program_id(1)
        x = x_ref[...]
        halo = jnp.where(i > 0, halo_ref[...], 0.0)
        xc = jnp.concatenate([halo, x], axis=0)
        w = w_ref[...]
        acc = b_ref[...] + x * w[CONV_W - 1:CONV_W, :]
        for k in range(CONV_W - 1):
            acc = acc + pltpu.roll(xc, CONV_W - 1 - k, axis=0)[CONV_HALO:] * w[k:k + 1, :]
        row = i * tm + lax.broadcasted_iota(jnp.int32, (tm, 1), 0)
        pre_ref[...] = acc
        o_ref[...] = jnp.where(row >= N_PAD, acc * _sigmoid(acc), 0.0)

    blk = pl.BlockSpec((tm, cw), lambda j, i: (i, j))
    return pl.pallas_call(
        body, name="conv_fwd", grid=(D_XBC // cw, t // tm),
        in_specs=[pl.BlockSpec((tm, cw), lambda j, i: (i, XBC_BLK + j)),
                  pl.BlockSpec((CONV_HALO, cw), lambda j, i: (jnp.maximum(i * hb - 1, 0), XBC_BLK + j)),
                  pl.BlockSpec((CONV_W, cw), lambda j, i: (0, j)), pl.BlockSpec((1, cw), lambda j, i: (0, j))],
        out_specs=[blk, blk],
        out_shape=[jax.ShapeDtypeStruct((t, D_XBC), F32), jax.ShapeDtypeStruct((t, D_XBC), F32)],
        compiler_params=_cparams(("parallel", "parallel")),
    )(proj, proj, conv_w, conv_b.reshape(1, D_XBC))


def _conv_bwd(dxa, coff, proj, pre, conv_w, dproj):
    t, ncols = dxa.shape
    cw = 1024
    tm = _tile(t, (528, 384, 128))
    hb = tm // CONV_HALO
    nt = t // tm
    n = tm + CONV_HALO

    def body(d_ref, dn_ref, p_ref, pn_ref, x_ref, w_ref, _, o_ref, dw_ref, db_ref):
        i = pl.program_id(1)
        last = i == nt - 1
        xc = jnp.concatenate([p_ref[...], pn_ref[...]], axis=0)
        df = jnp.concatenate([d_ref[...], dn_ref[...]], axis=0)
        w = w_ref[...]
        sig = _sigmoid(xc)
        row = i * tm + lax.broadcasted_iota(jnp.int32, (n, 1), 0)
        live = (row >= N_PAD) & ((row < (i + 1) * tm) | jnp.logical_not(last))
        dxc = jnp.where(live, df * (sig * (1.0 + xc * (1.0 - sig))), 0.0)
        ahead = [pltpu.roll(dxc, n - (CONV_W - 1 - k), axis=0)[:tm] for k in range(CONV_W - 1)] + [dxc[:tm]]
        acc = ahead[0] * w[0:1, :]
        for k in range(1, CONV_W):
            acc = acc + ahead[k] * w[k:k + 1, :]
        o_ref[...] = acc.astype(BF16)

        @pl.when(i == 0)
        def _():
            dw_ref[...] = jnp.zeros_like(dw_ref)
            db_ref[...] = jnp.zeros_like(db_ref)

        x = x_ref[...]
        db_ref[...] += jnp.sum(ahead[CONV_W - 1], axis=0, keepdims=True)
        for k in range(CONV_W):
            dw_ref[k:k + 1, :] += jnp.sum(ahead[k] * x, axis=0, keepdims=True)

    def pspec(rows, fn):
        return pl.BlockSpec((rows, cw), lambda j, i: (fn(i), coff + j))

    nxt = lambda i: jnp.minimum((i + 1) * hb, t // CONV_HALO - 1)
    dxbc, dw, db = pl.pallas_call(
        body, name=f"conv_bwd_{coff}", grid=(ncols // cw, nt),
        in_specs=[pl.BlockSpec((tm, cw), lambda j, i: (i, j)), pl.BlockSpec((CONV_HALO, cw), lambda j, i: (nxt(i), j)),
                  pspec(tm, lambda i: i), pspec(CONV_HALO, nxt),
                  pl.BlockSpec((tm, cw), lambda j, i: (i, XBC_BLK + coff + j)),
                  pl.BlockSpec((CONV_W, cw), lambda j, i: (0, coff + j)),
                  pl.BlockSpec(memory_space=pl.ANY)],
        out_specs=[pl.BlockSpec((tm, cw), lambda j, i: (i, XBC_BLK + coff + j)), pl.BlockSpec((CONV_W, cw), lambda j, i: (0, j)),
                   pl.BlockSpec((1, cw), lambda j, i: (0, j))],
        out_shape=[jax.ShapeDtypeStruct(dproj.shape, BF16), jax.ShapeDtypeStruct((CONV_W, ncols), F32),
                   jax.ShapeDtypeStruct((1, ncols), F32)],
        input_output_aliases={6: 0},
        compiler_params=_cparams(("parallel", "arbitrary")),
    )(dxa, dxa, pre, pre, proj, conv_w, dproj)
    return dxbc, dw, db.reshape(ncols)


def _cumsum(x, axis, reverse=False):
    n = x.shape[axis]
    idx = lax.broadcasted_iota(jnp.int32, x.shape, axis)
    k = 1
    while k < n:
        if reverse:
            x = x + jnp.where(idx < n - k, pltpu.roll(x, n - k, axis=axis), 0.0)
        else:
            x = x + jnp.where(idx >= k, pltpu.roll(x, k, axis=axis), 0.0)
        k *= 2
    return x


def _head_masks():
    lane = lax.broadcasted_iota(jnp.int32, (1, GROUP_W), 1)
    return [(lane >= r * HEAD_DIM) & (lane < (r + 1) * HEAD_DIM) for r in range(HEADS_PER_GROUP)]


def _expand_heads(cols, hm):
    out = jnp.where(hm[0], cols[:, 0:1], 0.0)
    for r in range(1, HEADS_PER_GROUP):
        out = out + jnp.where(hm[r], cols[:, r:r + 1], 0.0)
    return out


def _ssd_decay(dt_raw, dt_bias, a_log):
    t = dt_raw.shape[0]

    hpg = HEADS_PER_GROUP

    def body(raw_ref, b_ref, al_ref, dt_ref, acs_ref, sig_ref, rows_ref):
        raw = raw_ref[...] + b_ref[...]
        rowid = pl.program_id(0) * CHUNK + lax.broadcasted_iota(jnp.int32, (CHUNK, 1), 0)
        dt = jnp.where(rowid >= N_PAD, _softplus(raw), 0.0)
        acs = _cumsum(dt * -jnp.exp(al_ref[...]), 0)
        sig = _sigmoid(raw)
        acs_t = acs.T
        for g in range(N_GROUPS):
            heads = slice(g * hpg, (g + 1) * hpg)
            dt_ref[g] = dt[:, heads]
            acs_ref[g] = acs[:, heads]
            sig_ref[g] = sig[:, heads]
            rows_ref[g] = acs_t[heads, :]

    blk = pl.BlockSpec((CHUNK, DT_PAD), lambda c: (c, 0))
    vec = pl.BlockSpec((1, DT_PAD), lambda c: (0, 0))
    cols = pl.BlockSpec((N_GROUPS, CHUNK, hpg), lambda c: (0, c, 0))
    pad = lambda v: jnp.pad(v, (0, DT_PAD - N_HEADS)).reshape(1, DT_PAD)
    dt, acs, sig, rows = pl.pallas_call(
        body, name="ssd_decay", grid=(t // CHUNK,), in_specs=[blk, vec, vec],
        out_specs=[cols, cols, cols, pl.BlockSpec((N_GROUPS, hpg, CHUNK), lambda c: (0, 0, c))],
        out_shape=[jax.ShapeDtypeStruct((N_GROUPS, t, hpg), F32)] * 3 + [jax.ShapeDtypeStruct((N_GROUPS, hpg, t), F32)],
        compiler_params=_cparams(("parallel",)),
    )(dt_raw, pad(dt_bias), pad(a_log))
    return dict(dt=dt, acs=acs, acs_rows=rows, sig=sig, a_log=a_log.reshape(N_GROUPS, 1, hpg))


def _ssd_q(dtc, acs4, acs_r):
    hm = _head_masks()
    dt_exp = _expand_heads(dtc, hm)
    acs = _expand_heads(acs4, hm)
    atot = acs[CHUNK - 1:CHUNK, :]
    return dict(dtc=dtc, hm=hm, dt_exp=dt_exp, acs=acs, acs_r=acs_r,
                ea=jnp.exp(acs), ds=jnp.exp(atot - acs), dec=jnp.exp(atot))


def _stack4(x):
    return jnp.concatenate([x] * HEADS_PER_GROUP, axis=0)


def _ssd_decay_stacks(q):
    hpg = HEADS_PER_GROUP
    a_col = jnp.concatenate([q["acs"][:, r * HEAD_DIM:r * HEAD_DIM + 1] for r in range(hpg)], axis=0)
    a_row = jnp.concatenate([jnp.broadcast_to(q["acs_r"][r:r + 1, :], (CHUNK, CHUNK)) for r in range(hpg)], axis=0)
    ri = lax.broadcasted_iota(jnp.int32, (hpg * CHUNK, CHUNK), 0) % CHUNK
    ci = lax.broadcasted_iota(jnp.int32, (hpg * CHUNK, CHUNK), 1)
    diff = a_col - a_row
    lm = jnp.exp(jnp.where(ri >= ci, diff, -jnp.inf))
    lt = jnp.exp(jnp.where(ri <= ci, -diff, -jnp.inf))
    return lm, lt


def _pick_heads(stacked, hm):
    out = jnp.where(hm[0], stacked[:CHUNK], 0.0)
    for r in range(1, HEADS_PER_GROUP):
        out = out + jnp.where(hm[r], stacked[r * CHUNK:(r + 1) * CHUNK], 0.0)
    return out


def _mask_heads(x, hm):
    return jnp.concatenate([jnp.where(hm[r], x, 0.0) for r in range(HEADS_PER_GROUP)], axis=0)


SSD_GP_FWD, SSD_GP_BWD = 8, 8


def _ssd_specs(cidx, gp):
    hpg = HEADS_PER_GROUP
    return [
        pl.BlockSpec((CHUNK, gp * GROUP_W), lambda g, c: (cidx(c), g)),
        pl.BlockSpec((CHUNK, gp * D_STATE), lambda g, c: (cidx(c), D_INNER // (gp * D_STATE) + g)),
        pl.BlockSpec((CHUNK, gp * D_STATE), lambda g, c: (cidx(c), (D_INNER + 1024) // (gp * D_STATE) + g)),
        pl.BlockSpec((gp, CHUNK, hpg), lambda g, c: (g, cidx(c), 0)),
        pl.BlockSpec((gp, CHUNK, hpg), lambda g, c: (g, cidx(c), 0)),
        pl.BlockSpec((gp, hpg, CHUNK), lambda g, c: (g, 0, cidx(c))),
    ]


def _ssd_fwd(xa, dec, carry=None):
    t = xa.shape[0]
    nc = t // CHUNK
    gp, gw, ds = SSD_GP_FWD, GROUP_W, D_STATE
    grid = (N_GROUPS // gp, nc)
    c_in, c_out, c_sem = carry.specs() if carry is not None else ([], [], [])
    nci, nco = len(c_in), len(c_out)

    def body(*refs):
        xs_ref, b_ref, c_ref, dt_ref, acs_ref, acsr_ref = refs[:6]
        y_ref, prev_ref = refs[6 + nci:8 + nci]
        st_ref = refs[8 + nci + nco]
        if carry is not None:
            begin, end = carry.at_edges(grid, refs[6:6 + nci], refs[8 + nci:8 + nci + nco], refs[-2:])
            begin()

        @pl.when(pl.program_id(1) == 0)
        def _():
            st_ref[...] = jnp.zeros_like(st_ref)

        for k in range(gp):
            q = _ssd_q(dt_ref[k], acs_ref[k], acsr_ref[k])
            xdt = xs_ref[:, k * gw:(k + 1) * gw] * q["dt_exp"]
            bm = b_ref[:, k * ds:(k + 1) * ds].astype(BF16)
            cm = c_ref[:, k * ds:(k + 1) * ds].astype(BF16)
            cb = _dot(cm, bm, _NT)
            st = st_ref[k]
            prev_ref[0, k] = st
            lm, _ = _ssd_decay_stacks(q)
            y_diag = _pick_heads(_dot((_stack4(cb) * lm).astype(BF16), xdt.astype(BF16)), q["hm"])
            y_ref[:, k * gw:(k + 1) * gw] = y_diag + _dot(cm, st.astype(BF16)) * q["ea"]
            st_ref[k] = q["dec"] * st + _dot(bm, (xdt * q["ds"]).astype(BF16), _TN)
        if carry is not None:
            end()

    res = pl.pallas_call(
        body, name="ssd_fwd", grid=grid,
        in_specs=_ssd_specs(lambda c: c, gp) + c_in,
        out_specs=[pl.BlockSpec((CHUNK, gp * gw), lambda g, c: (c, g)),
                   pl.BlockSpec((1, gp, ds, gw), lambda g, c: (c, g, 0, 0))] + c_out,
        out_shape=[jax.ShapeDtypeStruct((t, D_INNER), F32), jax.ShapeDtypeStruct((nc, N_GROUPS, ds, gw), F32)]
        + (carry.outs if carry is not None else []),
        scratch_shapes=[pltpu.VMEM((gp, ds, gw), F32)] + c_sem,
        input_output_aliases=carry.aliases(6, 2) if carry is not None else {},
        compiler_params=_cparams(("arbitrary", "arbitrary") if carry is not None else ("parallel", "arbitrary")),
    )(xa, xa, xa, dec["dt"], dec["acs"], dec["acs_rows"], *(carry.ins if carry is not None else []))
    return res[0], res[1], res[2:]


def _ssd_bwd(dy, dsk, xa, prev, dec, carry=None):
    t = xa.shape[0]
    nc = t // CHUNK
    gp, gw, dstate = SSD_GP_BWD, GROUP_W, D_STATE
    hpg = HEADS_PER_GROUP

    grid = (N_GROUPS // gp, nc)
    c_in, c_out, c_sem = carry.specs() if carry is not None else ([], [], [])
    nci, nco = len(c_in), len(c_out)

    def body(*refs):
        ins, c_ins = refs[:11], refs[11:11 + nci]
        outs = refs[11 + nci:17 + nci]
        c_outs = refs[17 + nci:17 + nci + nco]
        dst_ref = refs[17 + nci + nco]
        dbias_ref, dalog_ref = outs[4], outs[5]
        if carry is not None:
            begin, end = carry.at_edges(grid, c_ins, c_outs, refs[-2:])
            begin()

        @pl.when(pl.program_id(1) == 0)
        def _():
            dst_ref[...] = jnp.zeros_like(dst_ref)
            dbias_ref[...] = jnp.zeros_like(dbias_ref)
            dalog_ref[...] = jnp.zeros_like(dalog_ref)

        for k in range(gp):
            one_group(k, *ins, *outs, dst_ref)
        if carry is not None:
            end()

    def one_group(k, xs_ref, b_ref, c_ref, dt_ref, acs_ref, acsr_ref, sig_ref, al_ref, dy_ref, sk_ref, prev_ref,
                  dxs_ref, db_ref, dc_ref, ddt_ref, dbias_ref, dalog_ref, dst_ref):
        wide = slice(k * gw, (k + 1) * gw)
        narrow = slice(k * dstate, (k + 1) * dstate)
        q = _ssd_q(dt_ref[k], acs_ref[k], acsr_ref[k])
        a_r = -jnp.exp(al_ref[k])
        hm, ds, dec, dt_exp = q["hm"], q["ds"], q["dec"], q["dt_exp"]
        xs = xs_ref[:, wide]
        xdt = xs * dt_exp
        xdtb = xdt.astype(BF16)
        bm = b_ref[:, narrow].astype(BF16)
        cm = c_ref[:, narrow].astype(BF16)
        cb = _dot(cm, bm, _NT)
        bc = _dot(bm, cm, _NT)
        dyv = dy_ref[:, wide]
        dye = (dyv * q["ea"]).astype(BF16)
        pst = prev_ref[0, k]
        dst = dst_ref[k]
        dstb = dst.astype(BF16)
        dx_state = ds * _dot(bm, dstb)
        onehot = [(lax.broadcasted_iota(jnp.int32, (1, hpg), 1) == r).astype(F32) for r in range(hpg)]
        lm, lt = _ssd_decay_stacks(q)
        dyb = dyv.astype(BF16)
        gl = _dot(_mask_heads(dyv, hm).astype(BF16), xdtb, _NT) * lm
        glt = _dot(_mask_heads(xdt, hm).astype(BF16), dyb, _NT) * lt
        bc4 = _stack4(bc)
        dxdt = dx_state + _pick_heads(_dot((bc4 * lt).astype(BF16), dyb), hm)
        wd = jnp.sum(gl * _stack4(cb), axis=1, keepdims=True) - jnp.sum(glt * bc4, axis=1, keepdims=True)
        dcb = gl[:CHUNK]
        dcbt = glt[:CHUNK]
        qa = wd[:CHUNK] * onehot[0]
        for r in range(1, hpg):
            dcb = dcb + gl[r * CHUNK:(r + 1) * CHUNK]
            dcbt = dcbt + glt[r * CHUNK:(r + 1) * CHUNK]
            qa = qa + wd[r * CHUNK:(r + 1) * CHUNK] * onehot[r]
        pstb = pst.astype(BF16)
        dc_ref[:, narrow] = _dot(dcb.astype(BF16), bm) + _dot(dye, pstb, _NT)
        db_ref[:, narrow] = _dot(dcbt.astype(BF16), cm) + _dot((xdt * ds).astype(BF16), dstb, _NT)
        dst_ref[k] = dec * dst + _dot(cm, dye, _TN)
        dxs_ref[:, wide] = dxdt * dt_exp + dyv * sk_ref[:, wide]

        t2 = xdt * dx_state
        t1 = dyv * (_dot(cm, pstb) * q["ea"]) - t2
        t4 = dxdt * xs
        last_row = jnp.sum(t2, axis=0, keepdims=True) + dec * jnp.sum(dst * pst, axis=0, keepdims=True)
        xd = jnp.zeros((CHUNK, hpg), F32)
        dal = jnp.zeros((1, hpg), F32)
        for r in range(hpg):
            qa = qa + jnp.sum(jnp.where(hm[r], t1, 0.0), axis=1, keepdims=True) * onehot[r]
            xd = xd + jnp.sum(jnp.where(hm[r], t4, 0.0), axis=1, keepdims=True) * onehot[r]
            dal = dal + jnp.sum(jnp.where(hm[r], last_row, 0.0), axis=1, keepdims=True) * onehot[r]
        rc = _cumsum(qa, 0, reverse=True) + dal
        rowid = (nc - 1 - pl.program_id(1)) * CHUNK + lax.broadcasted_iota(jnp.int32, (CHUNK, 1), 0)
        ddt_raw = jnp.where(rowid >= N_PAD, (rc * a_r + xd) * sig_ref[k], 0.0)
        ddt_ref[k] = ddt_raw
        dbias_ref[k] += jnp.sum(ddt_raw, axis=0, keepdims=True)
        dalog_ref[k] += jnp.sum(rc * q["dtc"], axis=0, keepdims=True) * a_r

    rev = lambda c: nc - 1 - c
    blk = pl.BlockSpec((CHUNK, gp * gw), lambda g, c: (rev(c), g))
    nblk = pl.BlockSpec((CHUNK, gp * dstate), lambda g, c: (rev(c), g))
    cols = pl.BlockSpec((gp, CHUNK, hpg), lambda g, c: (g, rev(c), 0))
    small = pl.BlockSpec((gp, 1, hpg), lambda g, c: (g, 0, 0))
    res = pl.pallas_call(
        body, name="ssd_bwd", grid=grid,
        in_specs=_ssd_specs(rev, gp) + [cols, small, blk, pl.BlockSpec((1, gp * gw), lambda g, c: (0, g)),
                                    pl.BlockSpec((1, gp, dstate, gw), lambda g, c: (rev(c), g, 0, 0))]
        + c_in,
        out_specs=[blk, nblk, nblk, cols, small, small] + c_out,
        out_shape=[jax.ShapeDtypeStruct((t, D_INNER), F32), jax.ShapeDtypeStruct((t, N_GROUPS * dstate), F32),
                   jax.ShapeDtypeStruct((t, N_GROUPS * dstate), F32), jax.ShapeDtypeStruct((N_GROUPS, t, hpg), F32),
                   jax.ShapeDtypeStruct((N_GROUPS, 1, hpg), F32), jax.ShapeDtypeStruct((N_GROUPS, 1, hpg), F32)]
        + (carry.outs if carry is not None else []),
        scratch_shapes=[pltpu.VMEM((gp, dstate, gw), F32)] + c_sem,
        compiler_params=_cparams(("arbitrary", "arbitrary") if carry is not None else ("parallel", "arbitrary")),
    )(xa, xa, xa, dec["dt"], dec["acs"], dec["acs_rows"], dec["sig"], dec["a_log"], dy, dsk.reshape(1, D_INNER), prev,
      *(carry.ins if carry is not None else []))
    dxs, db, dc, ddt, dbias, dalog = res[:6]
    ddt_raw = ddt.transpose(1, 0, 2).reshape(t, N_HEADS)
    return dxs, db, dc, ddt_raw, dbias.reshape(N_HEADS), dalog.reshape(N_HEADS), res[6:]


def _ssd_post_fwd(y, xa, proj, dsk, nw):
    t = y.shape[0]
    di = D_INNER
    tm = _tile(t, (384, 128))

    def body(y_ref, xs_ref, z_ref, dsk_ref, nw_ref, o_ref):
        z = z_ref[...]
        yz = (y_ref[...] + xs_ref[...] * dsk_ref[...]) * (z * _sigmoid(z))
        nwv = nw_ref[...]
        for g in range(N_GROUPS):
            sl = slice(g * GROUP_W, (g + 1) * GROUP_W)
            v = yz[:, sl]
            rg = lax.rsqrt(jnp.mean(v * v, axis=-1, keepdims=True) + EPS)
            o_ref[:, sl] = (v * rg * nwv[:, sl]).astype(BF16)

    row = pl.BlockSpec((tm, di), lambda i: (i, 0))
    vec = pl.BlockSpec((1, di), lambda i: (0, 0))
    return pl.pallas_call(
        body, name="ssd_post_fwd", grid=(t // tm,),
        in_specs=[row, row, row, vec, vec], out_specs=row, out_shape=jax.ShapeDtypeStruct((t, di), BF16),
        compiler_params=_cparams(("parallel",)),
    )(y, xa, proj, dsk.reshape(1, di), nw.reshape(1, di))


def _ssd_post_bwd(dyn, y, xa, proj, dsk, nw, dproj):
    t = y.shape[0]
    di = D_INNER
    tm = CHUNK

    def body(dyn_ref, y_ref, xs_ref, z_ref, dsk_ref, nw_ref, _, dy_ref, dz_ref, dnw_ref, ddsk_ref):
        @pl.when(pl.program_id(0) == 0)
        def _():
            dnw_ref[...] = jnp.zeros_like(dnw_ref)
            ddsk_ref[...] = jnp.zeros_like(ddsk_ref)

        for g in range(N_GROUPS):
            sl = slice(g * GROUP_W, (g + 1) * GROUP_W)
            z = z_ref[:, sl]
            sig = _sigmoid(z)
            s = z * sig
            xs = xs_ref[:, sl]
            dskv = dsk_ref[:, sl]
            yt = y_ref[:, sl] + xs * dskv
            yz = yt * s
            rg = lax.rsqrt(jnp.mean(yz * yz, axis=-1, keepdims=True) + EPS)
            xhat = yz * rg
            dynv = dyn_ref[:, sl]
            gg = dynv * nw_ref[:, sl]
            dnw_ref[:, sl] += jnp.sum(dynv * xhat, axis=0, keepdims=True)
            dyz = rg * (gg - xhat * jnp.mean(gg * xhat, axis=-1, keepdims=True))
            dyt = dyz * s
            dy_ref[:, sl] = dyt
            dz_ref[:, sl] = (dyz * yt * (sig * (1.0 + z * (1.0 - sig)))).astype(BF16)
            ddsk_ref[:, sl] += jnp.sum(dyt * xs, axis=0, keepdims=True)

    row = pl.BlockSpec((tm, di), lambda i: (i, 0))
    vec = pl.BlockSpec((1, di), lambda i: (0, 0))
    dy, dz, dnw, ddsk = pl.pallas_call(
        body, name="ssd_post_bwd", grid=(t // tm,),
        in_specs=[row, row, row, row, vec, vec, pl.BlockSpec(memory_space=pl.ANY)], out_specs=[row, row, vec, vec],
        out_shape=[jax.ShapeDtypeStruct((t, di), F32), jax.ShapeDtypeStruct(dproj.shape, BF16),
                   jax.ShapeDtypeStruct((1, di), F32), jax.ShapeDtypeStruct((1, di), F32)],
        input_output_aliases={6: 1},
        compiler_params=_cparams(("arbitrary",)),
    )(dyn, y, xa, proj, dsk.reshape(1, di), nw.reshape(1, di), dproj)
    return dy, dz, dnw.reshape(di), ddsk.reshape(N_HEADS, HEAD_DIM).sum(axis=1)


SHARD_COLS = IN_COLS // N_CHIPS


def _split_w_in(w_in_sh):
    nl = w_in_sh.shape[0]
    w = w_in_sh.transpose(0, 2, 1, 3).reshape(nl, D_MODEL, IN_COLS)
    main = jnp.concatenate([w[..., OFF_Z:OFF_XBC], w[..., OFF_GATE:], w[..., :OFF_Z], w[..., OFF_XBC:OFF_DT]], axis=-1)
    dt = jnp.pad(w[..., OFF_DT:OFF_GATE], ((0, 0), (0, 0), (0, DT_PAD - N_HEADS)))
    return main, dt


def _merge_dw_in(dmain, ddt):
    full = jnp.concatenate([dmain[:, PPOOL:PXBC], dmain[:, PZ:PGATE], dmain[:, PXBC:], ddt[:, :N_HEADS], dmain[:, PGATE:PPOOL]],
                           axis=1)
    return full.reshape(D_MODEL, N_CHIPS, SHARD_COLS).transpose(1, 0, 2)


def _relu2_epilogue(acc):
    hid = jnp.maximum(acc, 0.0)
    return acc, hid * hid


def _relu2_bwd_epilogue(acc, pre):
    return (acc * 2.0 * jnp.maximum(pre, 0.0),)


def _add_epilogue(acc, res):
    return (acc + res,)


def _norm_bwd_epilogue(acc, dres, h, w):
    r = lax.rsqrt(jnp.mean(h * h, axis=-1, keepdims=True) + EPS)
    xhat = h * r
    g = acc * w
    dh = dres + r * (g - xhat * jnp.mean(g * xhat, axis=-1, keepdims=True))
    return dh, jnp.sum(acc * xhat, axis=0, keepdims=True), dh


def _add_norm_bwd_epilogue(acc, part, dres, h, w):
    return _norm_bwd_epilogue(acc + part, dres, h, w)


def _add_norm_epilogue(acc, res, w):
    x = acc + res
    r = lax.rsqrt(jnp.mean(x * x, axis=-1, keepdims=True) + EPS)
    return x, x * r * w


def _w_spec(rows, cols, fn):
    return pl.BlockSpec((None, None, rows, cols), fn)


def _with_carry(stage, call):
    carry = stage[0]() if stage is not None else None
    res = call(carry)
    if carry is None:
        return res
    outs, got = res
    stage[1](got)
    return outs[0] if len(outs) == 1 else outs


def _layer_fwd(h, u, w, big, layer, stages=None, next_norm_w=None):
    t = h.shape[0]
    d = D_MODEL
    tm = _tile(t, (1408, 384, 128))
    nt = t // tm
    row = _bs((tm, d), lambda i, j, k: (i, 0))
    vec = _bs((1, d), lambda i, j, k: (0, 0))
    stages = stages or {}
    s = {"h": h}
    if u is None:
        u = _norm_fwd(h, w["mix_norm_w"], "norm_mix")
    proj = _with_carry(stages.get("a_ici"), lambda carry: _mm(
        f"mm_proj_{layer}", _NN, (nt, PCOLS // d, 1), u, row,
        big["w_in_main"], _bs((None, d, d), lambda i, j, k: (0, 0, j)),
        [((t, PCOLS), F32, _bs((tm, d), lambda i, j, k: (i, j)))], carry=carry))
    dt_raw = _mm(f"mm_dt_{layer}", _NN, (nt, 1, 1), u, row,
                 big["w_in_dt"], _bs((None, d, DT_PAD), lambda i, j, k: (0, 0, 0)),
                 [((t, DT_PAD), F32, _bs((tm, DT_PAD), lambda i, j, k: (i, 0)))])
    pooled, yg, ypm, y_pool = _pool_fwd(proj, big["pool_w_group"], w["pool_scale"], big["w_pool_up"], layer)
    xa, conv_pre = _conv_fwd(proj, w["conv_w"], w["conv_b"])
    dec = _ssd_decay(dt_raw, w["dt_bias"], w["a_log"])
    stage = stages.get("b_ici")
    carry = stage[0]() if stage is not None else None
    y, prev, got = _ssd_fwd(xa, dec, carry)
    if carry is not None:
        stage[1](got)
    yn = _ssd_post_fwd(y, xa, proj, w["d_skip_exp"], w["ssd_norm_w"])
    rs = D_INNER // N_CHIPS
    y_ssd = _with_carry(stages.get("a_d2d"), lambda carry: _mm(
        f"mm_ssd_out_{layer}", _NN, (nt, 1, 1), yn, _bs((tm, D_INNER), lambda i, j, k: (i, 0)),
        big["w_ssd_out"], _bs((None, N_CHIPS, rs, d), lambda i, j, k: (0, 0, 0, 0)), [((t, d), F32, row)], carry=carry))
    merged = _gate_fwd(proj, w["b_gate"], y_pool, y_ssd)
    ro = d // N_CHIPS
    h1, v = _mm(f"mm_o_{layer}", _NN, (nt, 1, 1), merged, row,
                big["w_o"], _bs((None, N_CHIPS, ro, d), lambda i, j, k: (0, 0, 0, 0)),
                [((t, d), F32, row), ((t, d), BF16, row)],
                extras=[(h, row), (w["mlp_norm_w"].reshape(1, d), vec)], epilogue=_add_norm_epilogue)
    tile = _bs((tm, d), lambda i, j, k: (i, j))
    pre, act = _with_carry(stages.get("b_d2d"), lambda carry: _mm(
        f"mm_ff1_{layer}", _NN, (nt, N_CHIPS, 1), v, row,
        big["w_ff1"], _w_spec(d, d, lambda i, j, k: (0, j, 0, 0)),
        [((t, D_FF), F32, tile), ((t, D_FF), BF16, tile)], epilogue=_relu2_epilogue, carry=carry))
    a_spec = _bs((tm, d), lambda i, j, k: (i, k))
    b_spec = _w_spec(d, d, lambda i, j, k: (0, k, 0, 0))
    if next_norm_w is None:
        h2 = _mm(f"mm_ff2_{layer}", _NN, (nt, 1, N_CHIPS), act, a_spec, big["w_ff2"], b_spec, [((t, d), F32, row)],
                 extras=[(h1, row)], epilogue=_add_epilogue)
        u_next = None
    else:
        h2, u_next = _mm(f"mm_ff2_{layer}", _NN, (nt, 1, N_CHIPS), act, a_spec, big["w_ff2"], b_spec,
                         [((t, d), F32, row), ((t, d), BF16, row)],
                         extras=[(h1, row), (next_norm_w.reshape(1, d), vec)], epilogue=_add_norm_epilogue)
    s.update(u=u, proj=proj, dec=dec, conv_pre=conv_pre, pooled=pooled, yg=yg, ypm=ypm, y_pool=y_pool, xa=xa, y=y, prev=prev, yn=yn,
             y_ssd=y_ssd, merged=merged, h1=h1, v=v, pre=pre, act=act)
    return h2, u_next, s


def _dw(name, layer, a, b, shard_shape, by, tk, carry=None):
    t = a.shape[0]
    nk = t // tk
    rows, cols = shard_shape
    if by == "rows":
        grid = (N_CHIPS, 1, nk)
        a_spec = _bs((tk, rows), lambda i, j, k: (k, i))
        b_spec = _bs((tk, cols), lambda i, j, k: (k, 0))
        o_spec = _bs((None, rows, cols), lambda i, j, k: (i, 0, 0))
    else:
        grid = (1, N_CHIPS, nk)
        a_spec = _bs((tk, rows), lambda i, j, k: (k, 0))
        b_spec = _bs((tk, cols), lambda i, j, k: (k, j))
        o_spec = _bs((None, rows, cols), lambda i, j, k: (j, 0, 0))
    return _mm(f"{name}_{layer}", _TN, grid, a, a_spec, b, b_spec, [((N_CHIPS, rows, cols), F32, o_spec)], carry=carry)


EARLY = ("w_ff2", "w_ff1", "w_o", "w_ssd_out", "w_pool_up", "pool_w_group")


def _layer_bwd(dh, dh_b, s, w, big, layer, red=None, last=False):
    t = dh.shape[0]
    d = D_MODEL
    tm = _tile(t, (1408, 384, 128))
    tk = _tile(t, (1408, 384, 128))
    nt = t // tm
    row = _bs((tm, d), lambda i, j, k: (i, 0))
    tile = _bs((tm, d), lambda i, j, k: (i, j))
    g, gb = {}, {}
    carry = red.swap_carry() if red is not None else None
    dh_mm = dh if dh_b is None else dh_b
    dpre = _mm(f"mm_dact_{layer}", _NT, (nt, N_CHIPS, 1), dh_mm, row,
               big["w_ff2"], _w_spec(d, d, lambda i, j, k: (0, j, 0, 0)), [((t, D_FF), BF16, tile)],
               extras=[(s["pre"], tile)], epilogue=_relu2_bwd_epilogue, carry=carry)
    if carry is not None:
        (dpre,), got = dpre
        red.after_swap(got)
    gb["w_ff2"] = _dw("mm_dw_ff2", layer, s["act"], dh_mm, (d, d), "rows", tk if dh_b is None else t)
    vec = _bs((1, d), lambda i, j, k: (0, 0))
    th = _tile(t, (704, 384, 128))
    half_row = _bs((th, d), lambda i, j, k: (i, 0))
    norm_outs = [((t, d), F32, half_row), ((1, d), F32, vec), ((t, d), BF16, half_row)]
    dh1, dnw, dh1_b = _mm(f"mm_dv_{layer}", _NT, (t // th, 1, N_CHIPS), dpre, _bs((th, d), lambda i, j, k: (i, k)),
                   big["w_ff1"], _w_spec(d, d, lambda i, j, k: (0, k, 0, 0)), norm_outs,
                   extras=[(dh, half_row), (s["h1"], half_row), (w["mlp_norm_w"].reshape(1, d), vec)],
                   epilogue=_norm_bwd_epilogue, summed=(1,))
    g["mlp_norm_w"] = dnw.reshape(d)
    gb["w_ff1"] = _dw("mm_dw_ff1", layer, s["v"], dpre, (d, d), "cols", t)
    ro = d // N_CHIPS
    dmerged = _mm(f"mm_dmerged_{layer}", _NT, (nt, N_CHIPS, 1), dh1_b, row,
                  big["w_o"], _w_spec(ro, d, lambda i, j, k: (0, j, 0, 0)),
                  [((t, d), F32, _bs((tm, ro), lambda i, j, k: (i, j)))])
    gb["w_o"] = _dw("mm_dw_o", layer, s["merged"], dh1_b, (ro, d), "rows", t)
    dy_pool, dy_ssd, dproj, g["b_gate"] = _gate_bwd(dmerged, s["proj"], w["b_gate"], s["y_pool"], s["y_ssd"])
    rs = D_INNER // N_CHIPS
    dyn = _mm(f"mm_dyn_{layer}", _NT, (nt, N_CHIPS, 1), dy_ssd, row,
              big["w_ssd_out"], _w_spec(rs, d, lambda i, j, k: (0, j, 0, 0)),
              [((t, D_INNER), F32, _bs((tm, rs), lambda i, j, k: (i, j)))])
    gb["w_ssd_out"] = _dw("mm_dw_ssd_out", layer, s["yn"], dy_ssd, (rs, d), "rows", t)
    dy, dproj, g["ssd_norm_w"], g["d_skip"] = _ssd_post_bwd(dyn, s["y"], s["xa"], s["proj"], w["d_skip_exp"],
                                                            w["ssd_norm_w"], dproj)
    carry = red.ici_carry() if red is not None else None
    dxs, db, dc, ddt_raw, g["dt_bias"], g["a_log"], got = _ssd_bwd(dy, w["d_skip_exp"], s["xa"], s["prev"], s["dec"], carry)
    if carry is not None:
        red.after_ici(got)
    own = red is not None and last
    dproj, dcw1, dcb1 = _conv_bwd(dxs, 0, s["proj"], s["conv_pre"], w["conv_w"], dproj)
    dproj, dcw2, dcb2 = _conv_bwd(db, 2, s["proj"], s["conv_pre"], w["conv_w"], dproj)
    dproj, dcw3, dcb3 = _conv_bwd(dc, 3, s["proj"], s["conv_pre"], w["conv_w"], dproj)
    g["conv_w"] = jnp.concatenate([dcw1, dcw2, dcw3], axis=1)
    g["conv_b"] = jnp.concatenate([dcb1, dcb2, dcb3])
    q, dyg, g["pool_scale"] = _pool_bwd_a(dy_pool, s["yg"], w["pool_scale"], big["pool_w_group"], big["w_pool_up"], layer)
    gb["w_pool_up"] = _dw("mm_dw_pool_up", layer, s["ypm"], dy_pool, (ro, d), "rows", t)
    gb["pool_w_group"] = _pool_dwg(s["pooled"], dyg, layer).reshape(N_CHIPS, POOL_GDIM, POOL_GDIM)
    dproj = _pool_bwd_b(q, dproj)
    if own:
        red.push(layer, {n: gb[n] for n in EARLY})
    ddt = jnp.pad(ddt_raw.astype(BF16), ((0, 0), (0, DT_PAD - N_HEADS)))
    nk = PCOLS // d
    kw = PCOLS // 4
    du = _with_carry((red.swap_carry, red.after_swap) if own else None, lambda carry: _mm(
        f"mm_du_{layer}", _NT, (nt, 1, PCOLS // kw), dproj, _bs((tm, kw), lambda i, j, k: (i, k)),
        big["w_in_main"], _bs((None, d, kw), lambda i, j, k: (0, 0, k)), [((t, d), F32, row)], carry=carry))
    dh0, dnw, dh0_b = _mm(f"mm_du_dt_{layer}", _NT, (t // th, 1, 1), ddt, _bs((th, DT_PAD), lambda i, j, k: (i, 0)),
                   big["w_in_dt"], _bs((None, d, DT_PAD), lambda i, j, k: (0, 0, 0)), norm_outs,
                   extras=[(du, half_row), (dh1, half_row), (s["h"], half_row), (w["mix_norm_w"].reshape(1, d), vec)],
                   epilogue=_add_norm_bwd_epilogue, summed=(1,))
    g["mix_norm_w"] = dnw.reshape(d)
    ntk = t // tk
    u_spec = _bs((tk, d), lambda i, j, k: (k, 0))
    dmain = _with_carry((red.ici_carry, red.after_ici) if own else None, lambda carry: _mm(
        f"mm_dw_in_{layer}", _TN, (1, nk, 1), s["u"], _bs((t, d), lambda i, j, k: (0, 0)), dproj,
        _bs((t, d), lambda i, j, k: (0, j)), [((d, PCOLS), F32, _bs((d, d), lambda i, j, k: (0, j)))], carry=carry))
    ddtw = _mm(f"mm_dw_dt_{layer}", _TN, (1, 1, ntk), s["u"], u_spec, ddt, _bs((tk, DT_PAD), lambda i, j, k: (k, 0)),
               [((d, DT_PAD), F32, _bs((d, DT_PAD), lambda i, j, k: (0, 0)))])
    gb["w_in"] = _merge_dw_in(dmain, ddtw)
    if red is not None:
        red.push(layer, {n: gb[n] for n in BIG if not (own and n in EARLY)})
    return dh0, dh0_b, g, gb


BIG = ("w_in", "pool_w_group", "w_pool_up", "w_ssd_out", "w_o", "w_ff1", "w_ff2")
PER_LAYER = ("mix_norm_w", "w_in", "b_gate", "pool_w_group", "pool_scale", "w_pool_up", "conv_w", "conv_b", "dt_bias",
             "a_log", "d_skip", "ssd_norm_w", "w_ssd_out", "w_o", "mlp_norm_w", "w_ff1", "w_ff2")


SMALL_PER_LAYER = tuple(n for n in PER_LAYER if n not in BIG)


SHARD_SHAPE = {"w_in": (1024, SHARD_COLS), "pool_w_group": (4, WG_ROWS, POOL_GDIM), "w_pool_up": (WUP_ROWS, D_MODEL),
               "w_ssd_out": (D_INNER // N_CHIPS, D_MODEL), "w_o": (D_MODEL // N_CHIPS, D_MODEL),
               "w_ff1": (D_MODEL, D_FF // N_CHIPS), "w_ff2": (D_FF // N_CHIPS, D_MODEL)}


def _layer_view(mats):
    big = {n: mats[n].reshape((1, N_CHIPS) + SHARD_SHAPE[n]) for n in BIG if n != "w_in"}
    big["w_in_main"], big["w_in_dt"] = _split_w_in(mats["w_in"])
    return big


def _local_step(x, tgt, weights, small, red=None):
    seq = x.shape[0]
    h = jnp.concatenate([jnp.zeros((N_PAD, D_MODEL), F32), small["meta_tokens"], x], axis=0)
    saved, ws, bigs = [], [], []
    u = None
    for i in range(DEPTH):
        w = {n: small[n][i] for n in SMALL_PER_LAYER}
        w["d_skip_exp"] = jnp.repeat(w["d_skip"], HEAD_DIM)
        big = weights.layer(i)
        next_norm_w = small["mix_norm_w"][i + 1] if i + 1 < DEPTH else None
        h, u, s = _layer_fwd(h, u, w, big, i, weights.stages(i), next_norm_w)
        saved.append(s)
        ws.append(w)
        bigs.append(big)
    loss, dh, g_final = _loss_head(h, small["final_norm_w"], tgt)
    layer_g, layer_gb = [None] * DEPTH, [None] * DEPTH
    dh_b = None
    for i in reversed(range(DEPTH)):
        dh, dh_b, layer_g[i], layer_gb[i] = _layer_bwd(dh, dh_b, saved[i], ws[i], bigs[i], i, red, last=(i == 0))
    grads = {n: jnp.stack([layer_g[i][n] for i in range(DEPTH)]) for n in SMALL_PER_LAYER}
    grads["final_norm_w"] = g_final
    grads["meta_tokens"] = dh[N_PAD:ROW_X]
    return loss, dh[ROW_X:ROW_X + seq], layer_gb, grads


MESH = pl.DeviceIdType.MESH
LANES = 128
ANY = pl.BlockSpec(memory_space=pl.ANY)


def _place():
    x, y, c = lax.axis_index("x"), lax.axis_index("y"), lax.axis_index("c")
    chips = [(1 - x, y), (x, 1 - y), (1 - x, 1 - y)]
    return x, y, c, chips


def _remote(src, dst, send_sem, recv_sem, to):
    return pltpu.make_async_remote_copy(src_ref=src, dst_ref=dst, send_sem=send_sem, recv_sem=recv_sem,
                                        device_id=to, device_id_type=MESH)


class _WeightGatherer:
    GROUPS = {"a": ("w_in",), "b": tuple(n for n in BIG if n != "w_in")}

    def __init__(self, mine):
        self.mine = mine
        self.landing = {}
        self.ready = {}

    def _ici(self, layer, group):
        names = self.GROUPS[group]
        srcs = [self.mine[n] for n in names]
        outs = [jax.ShapeDtypeStruct((1, N_CHIPS) + s.shape[1:], BF16) for s in srcs]

        def make(ins, out, send_sems, recv_sems):
            x, y, c, chips = _place()
            me = 2 * x + y
            cps = []
            for p, ref in enumerate(ins):
                half = ref.shape[1] // 2
                rows = pl.ds(c * half, half)
                cps += [_remote(ref.at[layer, rows, :], out[p].at[0, me, rows, :], send_sems.at[3 * p + j],
                                recv_sems.at[3 * p + j], (*chip, c)) for j, chip in enumerate(chips)]
            return cps

        return _Carry(srcs, outs, 3 * len(names), make)

    def _d2d(self, layer, group):
        names = self.GROUPS[group]
        n = len(names)
        bufs = self.landing[(layer, group)]
        outs = [jax.ShapeDtypeStruct(b.shape, b.dtype) for b in bufs]

        def make(ins, out, send_sems, recv_sems):
            x, y, c, chips = _place()
            me = 2 * x + y
            sibling = (x, y, 1 - c)
            cps = []
            for p in range(n):
                half = out[p].shape[2] // 2
                for j, (cx, cy) in enumerate(chips):
                    blk = out[p].at[0, 2 * cx + cy, pl.ds(c * half, half), :]
                    cps.append(_remote(blk, blk, send_sems.at[4 * p + j], recv_sems.at[4 * p + j], sibling))
                cps.append(_remote(ins[n + p].at[layer], out[p].at[0, me], send_sems.at[4 * p + 3],
                                   recv_sems.at[4 * p + 3], sibling))
            return cps

        return _Carry(list(bufs) + [self.mine[m] for m in names], outs, 4 * n, make, alias={p: p for p in range(n)})

    def _landed(self, layer, group, bufs):
        self.landing[(layer, group)] = bufs

    def _done(self, layer, group, bufs):
        self.ready.setdefault(layer, {}).update(zip(self.GROUPS[group], bufs))

    def layer(self, i):
        if i == 0:
            for g in self.GROUPS:
                self._landed(0, g, _run_carry(self._ici(0, g), f"gather_ici_{g}_0"))
            for g in self.GROUPS:
                self._done(0, g, _run_carry(self._d2d(0, g), f"gather_d2d_{g}_0"))
        return _layer_view(self.ready[i])

    def stages(self, i):
        nxt = i + 1
        if nxt == DEPTH:
            return None
        st = {}
        for g in self.GROUPS:
            st[f"{g}_ici"] = (functools.partial(self._ici, nxt, g), functools.partial(self._landed, nxt, g))
            st[f"{g}_d2d"] = (functools.partial(self._d2d, nxt, g), functools.partial(self._done, nxt, g))
        return st


def _exchange_small(v, reduce, name):
    rows_per = v.shape[0]
    vm = pl.BlockSpec(memory_space=pltpu.VMEM)

    def body(v_ref, out_ref, *scratch):
        if reduce:
            land_ref, send_sems, recv_sems, local_sem = scratch
        else:
            land_ref = out_ref
            send_sems, recv_sems, local_sem = scratch
        x, y, c, chips = _place()
        me, sibling = (x, y, c), (x, y, 1 - c)

        def rows(px, py, pc):
            return land_ref.at[4 * px + 2 * py + pc]

        def copy(k, block, to, src=None):
            return _remote(rows(*block) if src is None else src, rows(*block), send_sems.at[k], recv_sems.at[k], to)

        mine = pltpu.make_async_copy(v_ref, rows(*me), local_sem)
        mine.start()
        first = [copy(0, me, sibling, src=v_ref)]
        first += [copy(1 + j, me, (*chip, c), src=v_ref) for j, chip in enumerate(chips)]
        for cp in first:
            cp.start()
        passed = [copy(4 + j, (*chip, c), sibling) for j, chip in enumerate(chips)]
        for j, chip in enumerate(chips):
            copy(1 + j, (*chip, c), me).wait_recv()
            passed[j].start()
        copy(0, sibling, me).wait_recv()
        for j, chip in enumerate(chips):
            copy(4 + j, (*chip, 1 - c), me).wait_recv()
        for cp in first + passed:
            cp.wait_send()
        mine.wait()
        if reduce:
            acc = land_ref[0]
            for d in range(1, 8):
                acc = acc + land_ref[d]
            out_ref[...] = acc

    sems = [pltpu.SemaphoreType.DMA((7,)), pltpu.SemaphoreType.DMA((7,)), pltpu.SemaphoreType.DMA]
    if reduce:
        out_shape = jax.ShapeDtypeStruct((rows_per, LANES), F32)
        scratch = [pltpu.VMEM((8, rows_per, LANES), F32)] + sems
    else:
        out_shape = jax.ShapeDtypeStruct((8, rows_per, LANES), F32)
        scratch = sems
    return pl.pallas_call(
        body, name=name, in_specs=[vm], out_specs=vm, out_shape=out_shape, scratch_shapes=scratch,
        compiler_params=pltpu.CompilerParams(vmem_limit_bytes=VMEM_LIMIT),
    )(v)


def _run_carry(carry, name):
    c_in, c_out, c_sem = carry.specs()
    n_in, n_out = len(c_in), len(c_out)

    def body(*refs):
        cps = carry.make(refs[:n_in], refs[n_in:n_in + n_out], *refs[n_in + n_out:])
        for cp in cps:
            cp.start()
        for cp in cps:
            cp.wait()

    return pl.pallas_call(body, name=name, in_specs=c_in, out_specs=c_out, out_shape=carry.outs,
                          scratch_shapes=c_sem, input_output_aliases=carry.aliases(0, 0))(*carry.ins)


def _row_tile(rows, last, itemsize=4, budget=2 * 1024 * 1024):
    return _tile(rows, tuple(t for t in (2048, 1024, 512, 256, 128, 64, 32, 16) if t * last * itemsize <= budget))


def _rs_add(g, got, core, name):
    _, half, last = got.shape
    tr = _row_tile(half, last)
    nb = half // tr

    def body(c_ref, g_ref, got_ref, o_ref):
        o_ref[...] = (g_ref[...] + got_ref[...]).astype(BF16)

    blk = pl.BlockSpec((None, tr, last), lambda s, i, c_ref: (s, i, 0))
    return pl.pallas_call(
        body, name=name,
        grid_spec=pltpu.PrefetchScalarGridSpec(
            num_scalar_prefetch=1, grid=(N_CHIPS, nb),
            in_specs=[pl.BlockSpec((None, tr, last), lambda s, i, c_ref: (s, c_ref[0] * nb + i, 0)), blk],
            out_specs=blk),
        out_shape=jax.ShapeDtypeStruct(got.shape, BF16),
        compiler_params=_cparams(("parallel", "parallel")),
    )(core.reshape(1).astype(jnp.int32), g, got)


def _rs_sum(own, got, acc, layer, chip, core, name):
    _, half, last = own.shape
    tr = _row_tile(half, last, budget=1024 * 1024)
    nb = half // tr
    ni = 0 if acc is None else 1

    def body(k_ref, c_ref, own_ref, got_ref, *rest):
        o_ref = rest[-1]
        tot = own_ref[...].astype(F32)
        for j in range(3):
            tot = tot + got_ref[j].astype(F32)
        o_ref[...] = tot

    return pl.pallas_call(
        body, name=name,
        grid_spec=pltpu.PrefetchScalarGridSpec(
            num_scalar_prefetch=2, grid=(nb,),
            in_specs=[pl.BlockSpec((None, tr, last), lambda i, k_ref, c_ref: (k_ref[0], i, 0)),
                      pl.BlockSpec((3, tr, last), lambda i, k_ref, c_ref: (0, i, 0))]
            + [pl.BlockSpec(memory_space=pl.ANY)] * ni,
            out_specs=pl.BlockSpec((None, tr, last), lambda i, k_ref, c_ref: (layer, c_ref[0] * nb + i, 0))),
        out_shape=jax.ShapeDtypeStruct((DEPTH, 2 * half, last), F32),
        input_output_aliases={4: 0} if ni else {},
        compiler_params=_cparams(("parallel",)),
    )(chip.reshape(1).astype(jnp.int32), core.reshape(1).astype(jnp.int32), own, got, *([acc] if ni else []))


def _rs_share(arrs):
    n = len(arrs)

    def body(*refs):
        out = refs[n:2 * n]
        send_sems, recv_sems = refs[2 * n:]
        x, y, c, _ = _place()
        sibling = (x, y, 1 - c)
        cps = []
        for p in range(n):
            half = out[p].shape[1] // 2
            mine = out[p].at[:, pl.ds(c * half, half), :]
            cps.append(_remote(mine, mine, send_sems.at[p], recv_sems.at[p], sibling))
        for cp in cps:
            cp.start()
        for p, cp in enumerate(cps):
            cp.wait_send()
            half = out[p].shape[1] // 2
            other = out[p].at[:, pl.ds((1 - c) * half, half), :]
            _remote(other, other, send_sems.at[p], recv_sems.at[p], sibling).wait_recv()

    return pl.pallas_call(
        body, name="rs_share", in_specs=[ANY] * n, out_specs=[ANY] * n,
        out_shape=[jax.ShapeDtypeStruct(a.shape, a.dtype) for a in arrs],
        input_output_aliases={p: p for p in range(n)},
        scratch_shapes=[pltpu.SemaphoreType.DMA((n,)), pltpu.SemaphoreType.DMA((n,))],
    )(*arrs)


class _GradReducer:
    def __init__(self, chip, core):
        self.chip, self.core = chip, core
        self.pending = None
        self.sums = None
        self.acc = {n: None for n in BIG}

    def push(self, layer, gb):
        assert self.pending is None
        self.pending = (layer, list(gb), list(gb.values()))

    def swap_carry(self):
        if self.pending is None:
            return None
        _, _, gs = self.pending
        outs = [jax.ShapeDtypeStruct((g.shape[0], g.shape[1] // 2, g.shape[2]), F32) for g in gs]

        def make(ins, out, send_sems, recv_sems):
            x, y, c, _ = _place()
            cps = []
            for p, ref in enumerate(ins):
                half = ref.shape[1] // 2
                cps.append(_remote(ref.at[:, pl.ds((1 - c) * half, half), :], out[p], send_sems.at[p], recv_sems.at[p],
                                   (x, y, 1 - c)))
            return cps

        return _Carry(gs, outs, len(gs), make)

    def after_swap(self, got):
        layer, names, gs = self.pending
        self.pending = None
        self.sums = (layer, names, [_rs_add(g, a, self.core, f"rs_add_{n}_{layer}") for n, g, a in zip(names, gs, got)])

    def ici_carry(self):
        if self.sums is None:
            return None
        _, _, ps = self.sums
        outs = [jax.ShapeDtypeStruct((3,) + p.shape[1:], BF16) for p in ps]

        def make(ins, out, send_sems, recv_sems):
            x, y, c, chips = _place()
            return [_remote(ins[p].at[2 * cx + cy], out[p].at[j], send_sems.at[3 * p + j], recv_sems.at[3 * p + j],
                            (cx, cy, c)) for p in range(len(ins)) for j, (cx, cy) in enumerate(chips)]

        return _Carry(ps, outs, 3 * len(ps), make)

    def after_ici(self, got):
        layer, names, ps = self.sums
        self.sums = None
        for n, p, a in zip(names, ps, got):
            self.acc[n] = _rs_sum(p, a, self.acc[n], layer, self.chip, self.core, f"rs_sum_{n}_{layer}")

    def finish(self):
        if self.pending is not None:
            self.after_swap(_run_carry(self.swap_carry(), "rs_swap_last"))
        if self.sums is not None:
            self.after_ici(_run_carry(self.ici_carry(), "rs_ici_last"))
        return dict(zip(BIG, _rs_share([self.acc[n] for n in BIG])))


def _adamw(w, g, m, v, name):
    shape = w.shape
    if len(shape) == 2:
        shape3 = (1,) + shape
    else:
        shape3 = (-1,) + shape[-2:]
    w2, g2, m2, v2 = (a.reshape(shape3) for a in (w, g, m, v))
    lead, rows, last = w2.shape
    tr = max([t for t in range(8, rows + 1, 8) if rows % t == 0 and t * last * 4 <= 2 * 1024 * 1024] or [rows])

    def body(w_ref, g_ref, m_ref, v_ref, d_ref, nm_ref, nv_ref):
        gv = g_ref[...]
        mn = ADAM_B1 * m_ref[...] + (1.0 - ADAM_B1) * gv
        vn = ADAM_B2 * v_ref[...] + (1.0 - ADAM_B2) * (gv * gv)
        m_hat = mn / (1.0 - ADAM_B1 ** ADAM_STEP)
        v_hat = vn / (1.0 - ADAM_B2 ** ADAM_STEP)
        d_ref[...] = -ADAM_LR * (m_hat / (jnp.sqrt(v_hat) + ADAM_EPS) + ADAM_WD * w_ref[...])
        nm_ref[...] = mn
        nv_ref[...] = vn

    blk = pl.BlockSpec((None, tr, last), lambda a, i: (a, i, 0))
    outs = pl.pallas_call(
        body, name=name, grid=(lead, rows // tr), in_specs=[blk] * 4, out_specs=[blk] * 3,
        out_shape=[jax.ShapeDtypeStruct((lead, rows, last), F32)] * 3,
        compiler_params=_cparams(("parallel", "parallel")),
    )(w2, g2, m2, v2)
    return tuple(o.reshape(shape) for o in outs)


SHARD_AXIS = {"conv_w": 2, "meta_tokens": 1}
SMALL_SHARDED = ("conv_w", "meta_tokens")
REPLICATED = ("mix_norm_w", "b_gate", "pool_scale", "conv_b", "dt_bias", "a_log", "d_skip", "ssd_norm_w", "mlp_norm_w",
              "final_norm_w")


def _flatten(arrs, dtype, row_mult):
    flat = jnp.concatenate([a.astype(dtype).reshape(-1) for a in arrs])
    n = flat.shape[0]
    rows = -(-n // (LANES * row_mult)) * row_mult
    return jnp.pad(flat, (0, rows * LANES - n)).reshape(rows, LANES)


def _unflatten(flat2d, shapes):
    flat = flat2d.reshape(-1)
    out, off = [], 0
    for sh in shapes:
        n = 1
        for d in sh:
            n *= d
        out.append(flat[off:off + n].reshape(sh))
        off += n
    return out


def kernel(x, meta_tokens, mix_norm_w, w_in, b_gate, pool_w_group, pool_scale, w_pool_up, conv_w, conv_b, dt_bias, a_log, d_skip, ssd_norm_w, w_ssd_out, w_o, mlp_norm_w, w_ff1, w_ff2, final_norm_w, loss_target, m_meta_tokens, m_mix_norm_w, m_w_in, m_b_gate, m_pool_w_group, m_pool_scale, m_w_pool_up, m_conv_w, m_conv_b, m_dt_bias, m_a_log, m_d_skip, m_ssd_norm_w, m_w_ssd_out, m_w_o, m_mlp_norm_w, m_w_ff1, m_w_ff2, m_final_norm_w, v_meta_tokens, v_mix_norm_w, v_w_in, v_b_gate, v_pool_w_group, v_pool_scale, v_w_pool_up, v_conv_w, v_conv_b, v_dt_bias, v_a_log, v_d_skip, v_ssd_norm_w, v_w_ssd_out, v_w_o, v_mlp_norm_w, v_w_ff1, v_w_ff2, v_final_norm_w):
    names = ("meta_tokens",) + PER_LAYER + ("final_norm_w",)
    par = dict(meta_tokens=meta_tokens, mix_norm_w=mix_norm_w, w_in=w_in, b_gate=b_gate, pool_w_group=pool_w_group,
               pool_scale=pool_scale, w_pool_up=w_pool_up, conv_w=conv_w, conv_b=conv_b, dt_bias=dt_bias, a_log=a_log,
               d_skip=d_skip, ssd_norm_w=ssd_norm_w, w_ssd_out=w_ssd_out, w_o=w_o, mlp_norm_w=mlp_norm_w, w_ff1=w_ff1,
               w_ff2=w_ff2, final_norm_w=final_norm_w)
    mom = dict(meta_tokens=m_meta_tokens, mix_norm_w=m_mix_norm_w, w_in=m_w_in, b_gate=m_b_gate, pool_w_group=m_pool_w_group,
               pool_scale=m_pool_scale, w_pool_up=m_w_pool_up, conv_w=m_conv_w, conv_b=m_conv_b, dt_bias=m_dt_bias,
               a_log=m_a_log, d_skip=m_d_skip, ssd_norm_w=m_ssd_norm_w, w_ssd_out=m_w_ssd_out, w_o=m_w_o,
               mlp_norm_w=m_mlp_norm_w, w_ff1=m_w_ff1, w_ff2=m_w_ff2, final_norm_w=m_final_norm_w)
    var = dict(meta_tokens=v_meta_tokens, mix_norm_w=v_mix_norm_w, w_in=v_w_in, b_gate=v_b_gate, pool_w_group=v_pool_w_group,
               pool_scale=v_pool_scale, w_pool_up=v_w_pool_up, conv_w=v_conv_w, conv_b=v_conv_b, dt_bias=v_dt_bias,
               a_log=v_a_log, d_skip=v_d_skip, ssd_norm_w=v_ssd_norm_w, w_ssd_out=v_w_ssd_out, w_o=v_w_o,
               mlp_norm_w=v_mlp_norm_w, w_ff1=v_w_ff1, w_ff2=v_w_ff2, final_norm_w=v_final_norm_w)
    chip = 2 * lax.axis_index("x") + lax.axis_index("y")
    core = lax.axis_index("c")

    weights = _WeightGatherer({n: par[n].astype(BF16).reshape(DEPTH, -1, par[n].shape[-1]) for n in BIG})
    small = {n: par[n] for n in REPLICATED}
    small_shapes = [par[n].shape for n in SMALL_SHARDED]
    got_small = _exchange_small(_flatten([par[n] for n in SMALL_SHARDED], F32, 8), False, "gather_small")
    pieces = [_unflatten(got_small[2 * k], small_shapes) for k in range(N_CHIPS)]
    for j, n in enumerate(SMALL_SHARDED):
        small[n] = jnp.concatenate([pieces[k][j] for k in range(N_CHIPS)], axis=SHARD_AXIS[n])

    reducer = _GradReducer(chip, core)
    loss, dx, _, grads = _local_step(x[0], loss_target[0], weights, small, reducer)

    small_names = REPLICATED + SMALL_SHARDED
    red = _exchange_small(_flatten([grads[n] for n in small_names] + [loss.reshape(1)], F32, 8), True, "allreduce_small")
    *parts, loss = _unflatten(red, [grads[n].shape for n in small_names] + [(1,)])
    loss = loss[0]
    gsum = dict(zip(small_names, parts))
    for n in SMALL_SHARDED:
        width = par[n].shape[SHARD_AXIS[n]]
        gsum[n] = lax.dynamic_slice_in_dim(gsum[n], chip * width, width, axis=SHARD_AXIS[n])

    gsum.update({n: g.reshape(par[n].shape) for n, g in reducer.finish().items()})

    delta, new_m, new_v = {}, {}, {}
    for n in BIG:
        if n == "w_in":
            tr = lambda a: jnp.swapaxes(a, 1, 2)
            outs = _adamw(tr(par[n]), tr(gsum[n]), tr(mom[n]), tr(var[n]), "adamw_" + n)
            delta[n], new_m[n], new_v[n] = (tr(o) for o in outs)
        else:
            delta[n], new_m[n], new_v[n] = _adamw(par[n], gsum[n], mom[n], var[n], "adamw_" + n)
    rest = [n for n in names if n not in BIG]
    shapes = [par[n].shape for n in rest]
    packed = [_flatten([d[n] for n in rest], F32, 8) for d in (par, gsum, mom, var)]
    for d, flat in zip((delta, new_m, new_v), _adamw(*packed, "adamw_small")):
        d.update(zip(rest, _unflatten(flat, shapes)))

    order = ("meta_tokens", "mix_norm_w", "w_in", "b_gate", "pool_w_group", "pool_scale", "w_pool_up", "conv_w", "conv_b",
             "dt_bias", "a_log", "d_skip", "ssd_norm_w", "w_ssd_out", "w_o", "mlp_norm_w", "w_ff1", "w_ff2", "final_norm_w")
    return (loss, dx[None], *[gsum[n] for n in order], *[delta[n] for n in order], *[new_m[n] for n in order],
            *[new_v[n] for n in order])
```
